```python
import math
import jax
import jax.numpy as jnp
from jax import lax
import numpy as np

D_MODEL = 1024
BATCH = 8
SEQ = 2048
DEPTH = 2

S5_WIDTH = D_MODEL
S5_GROUP = 16
S5_GROUPS = S5_WIDTH // S5_GROUP
S5_STATE = 64
SSD_HEAD_DIM = 64
SSD_WIDTH = D_MODEL
SSD_HEADS = SSD_WIDTH // SSD_HEAD_DIM
SSD_GROUPS = 2
SSD_STATE = 128
SSD_CONV = 4
SSD_CHUNK = 128
SSD_CONV_DIM = SSD_WIDTH + 2 * SSD_GROUPS * SSD_STATE
MIX_WIDTH = S5_WIDTH + SSD_WIDTH
IN_PROJ = S5_WIDTH + SSD_WIDTH + SSD_CONV_DIM + SSD_HEADS
IN_SPLITS = (S5_WIDTH, S5_WIDTH + SSD_WIDTH, S5_WIDTH + SSD_WIDTH + SSD_CONV_DIM)
FFN_HIDDEN = ((8 * D_MODEL + 3 * 256 - 1) // (3 * 256)) * 256
EPS = 1e-6

kernel_name = 'hybrid_s5_ssd_parallel_heads'


def rmsnorm(x, g):
    xf = x.astype(jnp.float32)
    y = xf * lax.rsqrt(jnp.mean(xf * xf, axis=-1, keepdims=True) + EPS)
    return (y * g.astype(jnp.float32)).astype(x.dtype)


def s5_mixer(u, lam_re, lam_im, log_step, b_re, b_im, c_re, c_im, d_skip, w_glu, b_glu):
    bsz, L, _ = u.shape
    f32 = jnp.float32
    uf = u.astype(f32).reshape(bsz, L, S5_GROUPS, S5_GROUP)
    step = jnp.exp(log_step.astype(f32))[:, None]
    lr = lam_re.astype(f32)
    li = lam_im.astype(f32)
    mag = jnp.exp(lr * step)
    ang = li * step
    abar_re = mag * jnp.cos(ang)
    abar_im = mag * jnp.sin(ang)
    den = lr * lr + li * li
    nr = abar_re - 1.0
    ni = abar_im
    coef_re = (nr * lr + ni * li) / den
    coef_im = (ni * lr - nr * li) / den
    bre = b_re.astype(f32)
    bim = b_im.astype(f32)
    bbar_re = coef_re[..., None] * bre - coef_im[..., None] * bim
    bbar_im = coef_re[..., None] * bim + coef_im[..., None] * bre
    bu_re = jnp.einsum('blgh,gph->blgp', uf, bbar_re)
    bu_im = jnp.einsum('blgh,gph->blgp', uf, bbar_im)
    a_re = jnp.broadcast_to(abar_re, (1, L, S5_GROUPS, S5_STATE))
    a_im = jnp.broadcast_to(abar_im, (1, L, S5_GROUPS, S5_STATE))

    def combine(e1, e2):
        a1r, a1i, b1r, b1i = e1
        a2r, a2i, b2r, b2i = e2
        return (a2r * a1r - a2i * a1i,
                a2r * a1i + a2i * a1r,
                a2r * b1r - a2i * b1i + b2r,
                a2r * b1i + a2i * b1r + b2i)

    _, _, xr, xi = lax.associative_scan(combine, (a_re, a_im, bu_re, bu_im), axis=1)
    y = (jnp.einsum('blgp,ghp->blgh', xr, c_re.astype(f32))
         - jnp.einsum('blgp,ghp->blgh', xi, c_im.astype(f32)))
    y = y.reshape(bsz, L, S5_WIDTH) + d_skip.astype(f32) * uf.reshape(bsz, L, S5_WIDTH)
    g = jax.nn.gelu(y)
    out = g * jax.nn.sigmoid(g @ w_glu.astype(f32) + b_glu.astype(f32))
    return out.astype(u.dtype)


def segsum(a):
    T = a.shape[-1]
    rep = jnp.broadcast_to(a[..., None], a.shape + (T,))
    strict = jnp.tril(jnp.ones((T, T), dtype=bool), -1)
    cs = jnp.cumsum(jnp.where(strict, rep, 0.0), axis=-2)
    incl = jnp.tril(jnp.ones((T, T), dtype=bool))
    return jnp.where(incl, cs, -jnp.inf)


def ssd_mixer(z, xbc, dt, conv_w, conv_b, dt_bias, a_log, d_skip):
    bsz, L, _ = xbc.shape
    f32 = jnp.float32
    nc = L // SSD_CHUNK
    R = SSD_HEADS // SSD_GROUPS
    xbc = lax.conv_general_dilated(
        xbc, conv_w[:, None, :].astype(xbc.dtype), window_strides=(1,),
        padding=[(SSD_CONV - 1, 0)], dimension_numbers=('NWC', 'WIO', 'NWC'),
        feature_group_count=SSD_CONV_DIM) + conv_b
    xbc = jax.nn.silu(xbc).astype(f32)
    xs, bs, cs = jnp.split(xbc, (SSD_WIDTH, SSD_WIDTH + SSD_GROUPS * SSD_STATE), axis=-1)
    dtp = jax.nn.softplus(dt.astype(f32) + dt_bias.astype(f32))
    A = -jnp.exp(a_log.astype(f32))
    dta = (dtp * A).reshape(bsz, nc, SSD_CHUNK, SSD_GROUPS, R).transpose(0, 3, 4, 1, 2)
    xh = xs.reshape(bsz, L, SSD_HEADS, SSD_HEAD_DIM)
    xdt = (xh * dtp[..., None]).reshape(bsz, nc, SSD_CHUNK, SSD_GROUPS, R, SSD_HEAD_DIM)
    bmat = bs.reshape(bsz, nc, SSD_CHUNK, SSD_GROUPS, SSD_STATE)
    cmat = cs.reshape(bsz, nc, SSD_CHUNK, SSD_GROUPS, SSD_STATE)
    a_cum = jnp.cumsum(dta, axis=-1)
    lmat = jnp.exp(segsum(dta))
    y_diag = jnp.einsum('bclgn,bcsgn,bgrcls,bcsgrp->bclgrp', cmat, bmat, lmat, xdt)
    decay_states = jnp.exp(a_cum[..., -1:] - a_cum)
    states = jnp.einsum('bclgn,bgrcl,bclgrp->bcgrpn', bmat, decay_states, xdt)
    chunk_tot = jnp.pad(a_cum[..., -1], ((0, 0), (0, 0), (0, 0), (1, 0)))
    decay_chunk = jnp.exp(segsum(chunk_tot))
    states0 = jnp.concatenate([jnp.zeros_like(states[:, :1]), states], axis=1)
    new_states = jnp.einsum('bgrzc,bcgrpn->bzgrpn', decay_chunk, states0)
    prev = new_states[:, :-1]
    y_off = jnp.einsum('bclgn,bcgrpn,bgrcl->bclgrp', cmat, prev, jnp.exp(a_cum))
    y = (y_diag + y_off).reshape(bsz, L, SSD_HEADS, SSD_HEAD_DIM) + d_skip.astype(f32)[:, None] * xh
    y = y.reshape(bsz, L, SSD_WIDTH) * jax.nn.silu(z.astype(f32))
    return y.astype(z.dtype)


def _fwd_setup_inputs(seed: int = 0) -> dict:
    key = jax.random.key(seed)
    ks = jax.random.split(key, 32)
    f32 = jnp.float32

    def nrm(k, shape, scale):
        return jax.random.normal(k, shape, f32) * scale

    def gain(k, n):
        return 1.0 + 0.01 * jax.random.normal(k, (DEPTH, n), f32)

    n_idx = jnp.arange(S5_STATE, dtype=f32)
    lam_re = -0.5 + 0.01 * jax.random.normal(ks[3], (DEPTH, S5_GROUPS, S5_STATE), f32)
    lam_im = (jnp.broadcast_to(math.pi * n_idx, (DEPTH, S5_GROUPS, S5_STATE))
              + 0.01 * jax.random.normal(ks[4], (DEPTH, S5_GROUPS, S5_STATE), f32))
    log_step = jax.random.uniform(ks[5], (DEPTH, S5_GROUPS), f32, math.log(1e-3), math.log(1e-1))
    dt0 = jnp.exp(jax.random.uniform(ks[17], (DEPTH, SSD_HEADS), f32, math.log(1e-3), math.log(1e-1)))
    dt_bias = dt0 + jnp.log(-jnp.expm1(-dt0))
    a_log = jnp.log(jax.random.uniform(ks[18], (DEPTH, SSD_HEADS), f32, 1.0, 16.0))
    return {
        'x': jax.random.normal(ks[0], (BATCH, SEQ, D_MODEL), f32),
        'norm_mix': gain(ks[1], D_MODEL),
        'w_in': nrm(ks[2], (DEPTH, D_MODEL, IN_PROJ), D_MODEL ** -0.5),
        's5_lam_re': lam_re,
        's5_lam_im': lam_im,
        's5_log_step': log_step,
        's5_b_re': nrm(ks[6], (DEPTH, S5_GROUPS, S5_STATE, S5_GROUP), (2 * S5_GROUP) ** -0.5),
        's5_b_im': nrm(ks[7], (DEPTH, S5_GROUPS, S5_STATE, S5_GROUP), (2 * S5_GROUP) ** -0.5),
        's5_c_re': nrm(ks[8], (DEPTH, S5_GROUPS, S5_GROUP, S5_STATE), (2 * S5_STATE) ** -0.5),
        's5_c_im': nrm(ks[9], (DEPTH, S5_GROUPS, S5_GROUP, S5_STATE), (2 * S5_STATE) ** -0.5),
        's5_d': nrm(ks[10], (DEPTH, S5_WIDTH), 1.0),
        's5_w_glu': nrm(ks[11], (DEPTH, S5_WIDTH, S5_WIDTH), S5_WIDTH ** -0.5),
        's5_b_glu': nrm(ks[12], (DEPTH, S5_WIDTH), 0.01),
        's5_norm': gain(ks[13], S5_WIDTH),
        'ssd_conv_w': nrm(ks[14], (DEPTH, SSD_CONV, SSD_CONV_DIM), SSD_CONV ** -0.5),
        'ssd_conv_b': nrm(ks[15], (DEPTH, SSD_CONV_DIM), 0.01),
        'ssd_dt_bias': dt_bias,
        'ssd_a_log': a_log,
        'ssd_d': 1.0 + 0.1 * jax.random.normal(ks[19], (DEPTH, SSD_HEADS), f32),
        'ssd_norm': gain(ks[20], SSD_WIDTH),
        'w_out': nrm(ks[21], (DEPTH, MIX_WIDTH, D_MODEL), MIX_WIDTH ** -0.5),
        'norm_ffn': gain(ks[22], D_MODEL),
        'w_gate': nrm(ks[23], (DEPTH, D_MODEL, FFN_HIDDEN), D_MODEL ** -0.5),
        'w_up': nrm(ks[24], (DEPTH, D_MODEL, FFN_HIDDEN), D_MODEL ** -0.5),
        'w_down': nrm(ks[25], (DEPTH, FFN_HIDDEN, D_MODEL), FFN_HIDDEN ** -0.5),
        'norm_final': 1.0 + 0.01 * jax.random.normal(ks[26], (D_MODEL,), f32),
    }


def _fwd_reference(x, norm_mix, w_in, s5_lam_re, s5_lam_im, s5_log_step, s5_b_re, s5_b_im,
              s5_c_re, s5_c_im, s5_d, s5_w_glu, s5_b_glu, s5_norm, ssd_conv_w, ssd_conv_b,
              ssd_dt_bias, ssd_a_log, ssd_d, ssd_norm, w_out, norm_ffn, w_gate, w_up,
              w_down, norm_final):
    for i in range(DEPTH):
        h = rmsnorm(x, norm_mix[i])
        proj = h @ w_in[i]
        u_a, z_b, xbc_b, dt_b = jnp.split(proj, IN_SPLITS, axis=-1)
        y_a = s5_mixer(u_a, s5_lam_re[i], s5_lam_im[i], s5_log_step[i], s5_b_re[i],
                       s5_b_im[i], s5_c_re[i], s5_c_im[i], s5_d[i], s5_w_glu[i], s5_b_glu[i])
        y_a = rmsnorm(y_a, s5_norm[i])
        y_b = ssd_mixer(z_b, xbc_b, dt_b, ssd_conv_w[i], ssd_conv_b[i], ssd_dt_bias[i],
                        ssd_a_log[i], ssd_d[i])
        y_b = rmsnorm(y_b, ssd_norm[i])
        x = x + jnp.concatenate([y_a, y_b], axis=-1) @ w_out[i]
        h = rmsnorm(x, norm_ffn[i])
        x = x + (jax.nn.silu(h @ w_gate[i]) * (h @ w_up[i])) @ w_down[i]
    return rmsnorm(x, norm_final)


import jax as _jax
import jax.numpy as _jnp

TWIN_FORMAT = 'train_step'
FWD_PARAMS = ['x', 'norm_mix', 'w_in', 's5_lam_re', 's5_lam_im', 's5_log_step', 's5_b_re', 's5_b_im', 's5_c_re', 's5_c_im', 's5_d', 's5_w_glu', 's5_b_glu', 's5_norm', 'ssd_conv_w', 'ssd_conv_b', 'ssd_dt_bias', 'ssd_a_log', 'ssd_d', 'ssd_norm', 'w_out', 'norm_ffn', 'w_gate', 'w_up', 'w_down', 'norm_final']
TWIN_WEIGHTS = ['norm_mix', 'w_in', 's5_lam_re', 's5_lam_im', 's5_log_step', 's5_b_re', 's5_b_im', 's5_c_re', 's5_c_im', 's5_d', 's5_w_glu', 's5_b_glu', 's5_norm', 'ssd_conv_w', 'ssd_conv_b', 'ssd_dt_bias', 'ssd_a_log', 'ssd_d', 'ssd_norm', 'w_out', 'norm_ffn', 'w_gate', 'w_up', 'w_down', 'norm_final']
TWIN_DIFF_INPUT = 'x'
TWIN_INPUTS = ['x', 'norm_mix', 'w_in', 's5_lam_re', 's5_lam_im', 's5_log_step', 's5_b_re', 's5_b_im', 's5_c_re', 's5_c_im', 's5_d', 's5_w_glu', 's5_b_glu', 's5_norm', 'ssd_conv_w', 'ssd_conv_b', 'ssd_dt_bias', 'ssd_a_log', 'ssd_d', 'ssd_norm', 'w_out', 'norm_ffn', 'w_gate', 'w_up', 'w_down', 'norm_final', 'loss_target', 'm_norm_mix', 'm_w_in', 'm_s5_lam_re', 'm_s5_lam_im', 'm_s5_log_step', 'm_s5_b_re', 'm_s5_b_im', 'm_s5_c_re', 'm_s5_c_im', 'm_s5_d', 'm_s5_w_glu', 'm_s5_b_glu', 'm_s5_norm', 'm_ssd_conv_w', 'm_ssd_conv_b', 'm_ssd_dt_bias', 'm_ssd_a_log', 'm_ssd_d', 'm_ssd_norm', 'm_w_out', 'm_norm_ffn', 'm_w_gate', 'm_w_up', 'm_w_down', 'm_norm_final', 'v_norm_mix', 'v_w_in', 'v_s5_lam_re', 'v_s5_lam_im', 'v_s5_log_step', 'v_s5_b_re', 'v_s5_b_im', 'v_s5_c_re', 'v_s5_c_im', 'v_s5_d', 'v_s5_w_glu', 'v_s5_b_glu', 'v_s5_norm', 'v_ssd_conv_w', 'v_ssd_conv_b', 'v_ssd_dt_bias', 'v_ssd_a_log', 'v_ssd_d', 'v_ssd_norm', 'v_w_out', 'v_norm_ffn', 'v_w_gate', 'v_w_up', 'v_w_down', 'v_norm_final']
TWIN_OUTPUTS = ['loss', 'grad_x', 'grad_norm_mix', 'grad_w_in', 'grad_s5_lam_re', 'grad_s5_lam_im', 'grad_s5_log_step', 'grad_s5_b_re', 'grad_s5_b_im', 'grad_s5_c_re', 'grad_s5_c_im', 'grad_s5_d', 'grad_s5_w_glu', 'grad_s5_b_glu', 'grad_s5_norm', 'grad_ssd_conv_w', 'grad_ssd_conv_b', 'grad_ssd_dt_bias', 'grad_ssd_a_log', 'grad_ssd_d', 'grad_ssd_norm', 'grad_w_out', 'grad_norm_ffn', 'grad_w_gate', 'grad_w_up', 'grad_w_down', 'grad_norm_final', 'delta_norm_mix', 'delta_w_in', 'delta_s5_lam_re', 'delta_s5_lam_im', 'delta_s5_log_step', 'delta_s5_b_re', 'delta_s5_b_im', 'delta_s5_c_re', 'delta_s5_c_im', 'delta_s5_d', 'delta_s5_w_glu', 'delta_s5_b_glu', 'delta_s5_norm', 'delta_ssd_conv_w', 'delta_ssd_conv_b', 'delta_ssd_dt_bias', 'delta_ssd_a_log', 'delta_ssd_d', 'delta_ssd_norm', 'delta_w_out', 'delta_norm_ffn', 'delta_w_gate', 'delta_w_up', 'delta_w_down', 'delta_norm_final', 'new_m_norm_mix', 'new_m_w_in', 'new_m_s5_lam_re', 'new_m_s5_lam_im', 'new_m_s5_log_step', 'new_m_s5_b_re', 'new_m_s5_b_im', 'new_m_s5_c_re', 'new_m_s5_c_im', 'new_m_s5_d', 'new_m_s5_w_glu', 'new_m_s5_b_glu', 'new_m_s5_norm', 'new_m_ssd_conv_w', 'new_m_ssd_conv_b', 'new_m_ssd_dt_bias', 'new_m_ssd_a_log', 'new_m_ssd_d', 'new_m_ssd_norm', 'new_m_w_out', 'new_m_norm_ffn', 'new_m_w_gate', 'new_m_w_up', 'new_m_w_down', 'new_m_norm_final', 'new_v_norm_mix', 'new_v_w_in', 'new_v_s5_lam_re', 'new_v_s5_lam_im', 'new_v_s5_log_step', 'new_v_s5_b_re', 'new_v_s5_b_im', 'new_v_s5_c_re', 'new_v_s5_c_im', 'new_v_s5_d', 'new_v_s5_w_glu', 'new_v_s5_b_glu', 'new_v_s5_norm', 'new_v_ssd_conv_w', 'new_v_ssd_conv_b', 'new_v_ssd_dt_bias', 'new_v_ssd_a_log', 'new_v_ssd_d', 'new_v_ssd_norm', 'new_v_w_out', 'new_v_norm_ffn', 'new_v_w_gate', 'new_v_w_up', 'new_v_w_down', 'new_v_norm_final']
TWIN_LEAF_KINDS = {'loss': 'loss', 'grad_x': 'grad_x', 'grad_norm_mix': 'grad_w', 'grad_w_in': 'grad_w', 'grad_s5_lam_re': 'grad_w', 'grad_s5_lam_im': 'grad_w', 'grad_s5_log_step': 'grad_w', 'grad_s5_b_re': 'grad_w', 'grad_s5_b_im': 'grad_w', 'grad_s5_c_re': 'grad_w', 'grad_s5_c_im': 'grad_w', 'grad_s5_d': 'grad_w', 'grad_s5_w_glu': 'grad_w', 'grad_s5_b_glu': 'grad_w', 'grad_s5_norm': 'grad_w', 'grad_ssd_conv_w': 'grad_w', 'grad_ssd_conv_b': 'grad_w', 'grad_ssd_dt_bias': 'grad_w', 'grad_ssd_a_log': 'grad_w', 'grad_ssd_d': 'grad_w', 'grad_ssd_norm': 'grad_w', 'grad_w_out': 'grad_w', 'grad_norm_ffn': 'grad_w', 'grad_w_gate': 'grad_w', 'grad_w_up': 'grad_w', 'grad_w_down': 'grad_w', 'grad_norm_final': 'grad_w', 'delta_norm_mix': 'delta_w', 'delta_w_in': 'delta_w', 'delta_s5_lam_re': 'delta_w', 'delta_s5_lam_im': 'delta_w', 'delta_s5_log_step': 'delta_w', 'delta_s5_b_re': 'delta_w', 'delta_s5_b_im': 'delta_w', 'delta_s5_c_re': 'delta_w', 'delta_s5_c_im': 'delta_w', 'delta_s5_d': 'delta_w', 'delta_s5_w_glu': 'delta_w', 'delta_s5_b_glu': 'delta_w', 'delta_s5_norm': 'delta_w', 'delta_ssd_conv_w': 'delta_w', 'delta_ssd_conv_b': 'delta_w', 'delta_ssd_dt_bias': 'delta_w', 'delta_ssd_a_log': 'delta_w', 'delta_ssd_d': 'delta_w', 'delta_ssd_norm': 'delta_w', 'delta_w_out': 'delta_w', 'delta_norm_ffn': 'delta_w', 'delta_w_gate': 'delta_w', 'delta_w_up': 'delta_w', 'delta_w_down': 'delta_w', 'delta_norm_final': 'delta_w', 'new_m_norm_mix': 'new_m', 'new_m_w_in': 'new_m', 'new_m_s5_lam_re': 'new_m', 'new_m_s5_lam_im': 'new_m', 'new_m_s5_log_step': 'new_m', 'new_m_s5_b_re': 'new_m', 'new_m_s5_b_im': 'new_m', 'new_m_s5_c_re': 'new_m', 'new_m_s5_c_im': 'new_m', 'new_m_s5_d': 'new_m', 'new_m_s5_w_glu': 'new_m', 'new_m_s5_b_glu': 'new_m', 'new_m_s5_norm': 'new_m', 'new_m_ssd_conv_w': 'new_m', 'new_m_ssd_conv_b': 'new_m', 'new_m_ssd_dt_bias': 'new_m', 'new_m_ssd_a_log': 'new_m', 'new_m_ssd_d': 'new_m', 'new_m_ssd_norm': 'new_m', 'new_m_w_out': 'new_m', 'new_m_norm_ffn': 'new_m', 'new_m_w_gate': 'new_m', 'new_m_w_up': 'new_m', 'new_m_w_down': 'new_m', 'new_m_norm_final': 'new_m', 'new_v_norm_mix': 'new_v', 'new_v_w_in': 'new_v', 'new_v_s5_lam_re': 'new_v', 'new_v_s5_lam_im': 'new_v', 'new_v_s5_log_step': 'new_v', 'new_v_s5_b_re': 'new_v', 'new_v_s5_b_im': 'new_v', 'new_v_s5_c_re': 'new_v', 'new_v_s5_c_im': 'new_v', 'new_v_s5_d': 'new_v', 'new_v_s5_w_glu': 'new_v', 'new_v_s5_b_glu': 'new_v', 'new_v_s5_norm': 'new_v', 'new_v_ssd_conv_w': 'new_v', 'new_v_ssd_conv_b': 'new_v', 'new_v_ssd_dt_bias': 'new_v', 'new_v_ssd_a_log': 'new_v', 'new_v_ssd_d': 'new_v', 'new_v_ssd_norm': 'new_v', 'new_v_w_out': 'new_v', 'new_v_norm_ffn': 'new_v', 'new_v_w_gate': 'new_v', 'new_v_w_up': 'new_v', 'new_v_w_down': 'new_v', 'new_v_norm_final': 'new_v'}


def _forward(args):
    return _fwd_reference(*[args[k] for k in FWD_PARAMS])


def _output_shape():
    out = _jax.eval_shape(lambda: _forward(_fwd_setup_inputs(0)))
    return out.shape, out.dtype

N_MICROBATCH = 1
ADAM_LR = 0.001
ADAM_B1 = 0.9
ADAM_B2 = 0.999
ADAM_EPS = 1e-08
ADAM_WD = 0.01
ADAM_STEP = 10
PER_EXAMPLE_BATCH_AXIS = {'x': 0, 'loss_target': 0}
SHARED_INPUTS = []
_WEIGHT_DTYPES = {'norm_mix': _jnp.float32, 'w_in': _jnp.float32, 's5_lam_re': _jnp.float32, 's5_lam_im': _jnp.float32, 's5_log_step': _jnp.float32, 's5_b_re': _jnp.float32, 's5_b_im': _jnp.float32, 's5_c_re': _jnp.float32, 's5_c_im': _jnp.float32, 's5_d': _jnp.float32, 's5_w_glu': _jnp.float32, 's5_b_glu': _jnp.float32, 's5_norm': _jnp.float32, 'ssd_conv_w': _jnp.float32, 'ssd_conv_b': _jnp.float32, 'ssd_dt_bias': _jnp.float32, 'ssd_a_log': _jnp.float32, 'ssd_d': _jnp.float32, 'ssd_norm': _jnp.float32, 'w_out': _jnp.float32, 'norm_ffn': _jnp.float32, 'w_gate': _jnp.float32, 'w_up': _jnp.float32, 'w_down': _jnp.float32, 'norm_final': _jnp.float32}
MOMENT_SCALE = {'norm_mix': 1.251457e-01, 'w_in': 6.461683e-02, 's5_lam_re': 3.707317e-03, 's5_lam_im': 3.456360e-03, 's5_log_step': 2.317552e+00, 's5_b_re': 2.374409e-03, 's5_b_im': 2.415804e-03, 's5_c_re': 4.813954e-03, 's5_c_im': 4.800538e-03, 's5_d': 7.210829e-02, 's5_w_glu': 1.932333e-02, 's5_b_glu': 2.950852e-02, 's5_norm': 6.769169e-02, 'ssd_conv_w': 5.685593e-02, 'ssd_conv_b': 7.083296e-02, 'ssd_dt_bias': 2.199808e-01, 'ssd_a_log': 2.020545e-01, 'ssd_d': 3.635838e-01, 'ssd_norm': 6.415509e-02, 'w_out': 9.351001e-02, 'norm_ffn': 6.940777e-02, 'w_gate': 2.999743e-02, 'w_up': 2.911115e-02, 'w_down': 4.823796e-02, 'norm_final': 1.606827e+01}


def _to_microbatches(a, axis):
    t = _jnp.moveaxis(a, axis, 0)
    t = t.reshape((N_MICROBATCH, t.shape[0] // N_MICROBATCH) + t.shape[1:])
    return _jnp.moveaxis(t, 1, axis + 1)


def setup_inputs(seed: int = 0) -> dict:
    inp = _fwd_setup_inputs(seed)
    key = _jax.random.fold_in(_jax.random.key(seed), 7919)
    shape, _ = _output_shape()
    out = dict(inp)
    out["loss_target"] = _jax.random.normal(_jax.random.fold_in(key, 0), shape, _jnp.float32)
    for i, name in enumerate(TWIN_WEIGHTS):
        w = inp[name].astype(_jnp.float32)
        if MOMENT_SCALE is None:
            s = _jnp.sqrt(_jnp.mean(_jnp.square(w)) + 1e-30)
        else:
            s = MOMENT_SCALE[name]
        km, kv = _jax.random.split(_jax.random.fold_in(key, i + 1))
        out[name] = w
        out["m_" + name] = s * _jax.random.normal(km, w.shape, _jnp.float32)
        out["v_" + name] = (s * s) * _jax.random.uniform(kv, w.shape, _jnp.float32, 0.5, 1.5)
    if N_MICROBATCH > 1:
        for name, axis in PER_EXAMPLE_BATCH_AXIS.items():
            out[name] = _to_microbatches(out[name], axis)
    return {'x': out['x'], 'norm_mix': out['norm_mix'], 'w_in': out['w_in'], 's5_lam_re': out['s5_lam_re'], 's5_lam_im': out['s5_lam_im'], 's5_log_step': out['s5_log_step'], 's5_b_re': out['s5_b_re'], 's5_b_im': out['s5_b_im'], 's5_c_re': out['s5_c_re'], 's5_c_im': out['s5_c_im'], 's5_d': out['s5_d'], 's5_w_glu': out['s5_w_glu'], 's5_b_glu': out['s5_b_glu'], 's5_norm': out['s5_norm'], 'ssd_conv_w': out['ssd_conv_w'], 'ssd_conv_b': out['ssd_conv_b'], 'ssd_dt_bias': out['ssd_dt_bias'], 'ssd_a_log': out['ssd_a_log'], 'ssd_d': out['ssd_d'], 'ssd_norm': out['ssd_norm'], 'w_out': out['w_out'], 'norm_ffn': out['norm_ffn'], 'w_gate': out['w_gate'], 'w_up': out['w_up'], 'w_down': out['w_down'], 'norm_final': out['norm_final'], 'loss_target': out['loss_target'], 'm_norm_mix': out['m_norm_mix'], 'm_w_in': out['m_w_in'], 'm_s5_lam_re': out['m_s5_lam_re'], 'm_s5_lam_im': out['m_s5_lam_im'], 'm_s5_log_step': out['m_s5_log_step'], 'm_s5_b_re': out['m_s5_b_re'], 'm_s5_b_im': out['m_s5_b_im'], 'm_s5_c_re': out['m_s5_c_re'], 'm_s5_c_im': out['m_s5_c_im'], 'm_s5_d': out['m_s5_d'], 'm_s5_w_glu': out['m_s5_w_glu'], 'm_s5_b_glu': out['m_s5_b_glu'], 'm_s5_norm': out['m_s5_norm'], 'm_ssd_conv_w': out['m_ssd_conv_w'], 'm_ssd_conv_b': out['m_ssd_conv_b'], 'm_ssd_dt_bias': out['m_ssd_dt_bias'], 'm_ssd_a_log': out['m_ssd_a_log'], 'm_ssd_d': out['m_ssd_d'], 'm_ssd_norm': out['m_ssd_norm'], 'm_w_out': out['m_w_out'], 'm_norm_ffn': out['m_norm_ffn'], 'm_w_gate': out['m_w_gate'], 'm_w_up': out['m_w_up'], 'm_w_down': out['m_w_down'], 'm_norm_final': out['m_norm_final'], 'v_norm_mix': out['v_norm_mix'], 'v_w_in': out['v_w_in'], 'v_s5_lam_re': out['v_s5_lam_re'], 'v_s5_lam_im': out['v_s5_lam_im'], 'v_s5_log_step': out['v_s5_log_step'], 'v_s5_b_re': out['v_s5_b_re'], 'v_s5_b_im': out['v_s5_b_im'], 'v_s5_c_re': out['v_s5_c_re'], 'v_s5_c_im': out['v_s5_c_im'], 'v_s5_d': out['v_s5_d'], 'v_s5_w_glu': out['v_s5_w_glu'], 'v_s5_b_glu': out['v_s5_b_glu'], 'v_s5_norm': out['v_s5_norm'], 'v_ssd_conv_w': out['v_ssd_conv_w'], 'v_ssd_conv_b': out['v_ssd_conv_b'], 'v_ssd_dt_bias': out['v_ssd_dt_bias'], 'v_ssd_a_log': out['v_ssd_a_log'], 'v_ssd_d': out['v_ssd_d'], 'v_ssd_norm': out['v_ssd_norm'], 'v_w_out': out['v_w_out'], 'v_norm_ffn': out['v_norm_ffn'], 'v_w_gate': out['v_w_gate'], 'v_w_up': out['v_w_up'], 'v_w_down': out['v_w_down'], 'v_norm_final': out['v_norm_final']}


def _loss(weights, diff, rest, loss_target):
    with _jax.named_scope("forward"):
        args = {**rest, TWIN_DIFF_INPUT: diff, **{k: w.astype(_WEIGHT_DTYPES[k]) for k, w in weights.items()}}
        y = _forward(args)
    with _jax.named_scope("loss_head"):
        err = _jnp.square(y.astype(_jnp.float32) - loss_target)
        return 0.5 * _jnp.sum(_jnp.mean(err, axis=-1)) if err.ndim else 0.5 * err


def _adamw(w, g, m, v):
    m = ADAM_B1 * m + (1.0 - ADAM_B1) * g
    v = ADAM_B2 * v + (1.0 - ADAM_B2) * _jnp.square(g)
    m_hat = m / (1.0 - ADAM_B1 ** ADAM_STEP)
    v_hat = v / (1.0 - ADAM_B2 ** ADAM_STEP)
    delta = -ADAM_LR * (m_hat / (_jnp.sqrt(v_hat) + ADAM_EPS) + ADAM_WD * w)
    return delta, m, v


def reference(x, norm_mix, w_in, s5_lam_re, s5_lam_im, s5_log_step, s5_b_re, s5_b_im, s5_c_re, s5_c_im, s5_d, s5_w_glu, s5_b_glu, s5_norm, ssd_conv_w, ssd_conv_b, ssd_dt_bias, ssd_a_log, ssd_d, ssd_norm, w_out, norm_ffn, w_gate, w_up, w_down, norm_final, loss_target, m_norm_mix, m_w_in, m_s5_lam_re, m_s5_lam_im, m_s5_log_step, m_s5_b_re, m_s5_b_im, m_s5_c_re, m_s5_c_im, m_s5_d, m_s5_w_glu, m_s5_b_glu, m_s5_norm, m_ssd_conv_w, m_ssd_conv_b, m_ssd_dt_bias, m_ssd_a_log, m_ssd_d, m_ssd_norm, m_w_out, m_norm_ffn, m_w_gate, m_w_up, m_w_down, m_norm_final, v_norm_mix, v_w_in, v_s5_lam_re, v_s5_lam_im, v_s5_log_step, v_s5_b_re, v_s5_b_im, v_s5_c_re, v_s5_c_im, v_s5_d, v_s5_w_glu, v_s5_b_glu, v_s5_norm, v_ssd_conv_w, v_ssd_conv_b, v_ssd_dt_bias, v_ssd_a_log, v_ssd_d, v_ssd_norm, v_w_out, v_norm_ffn, v_w_gate, v_w_up, v_w_down, v_norm_final):
    given = dict(x=x, norm_mix=norm_mix, w_in=w_in, s5_lam_re=s5_lam_re, s5_lam_im=s5_lam_im, s5_log_step=s5_log_step, s5_b_re=s5_b_re, s5_b_im=s5_b_im, s5_c_re=s5_c_re, s5_c_im=s5_c_im, s5_d=s5_d, s5_w_glu=s5_w_glu, s5_b_glu=s5_b_glu, s5_norm=s5_norm, ssd_conv_w=ssd_conv_w, ssd_conv_b=ssd_conv_b, ssd_dt_bias=ssd_dt_bias, ssd_a_log=ssd_a_log, ssd_d=ssd_d, ssd_norm=ssd_norm, w_out=w_out, norm_ffn=norm_ffn, w_gate=w_gate, w_up=w_up, w_down=w_down, norm_final=norm_final, loss_target=loss_target, m_norm_mix=m_norm_mix, m_w_in=m_w_in, m_s5_lam_re=m_s5_lam_re, m_s5_lam_im=m_s5_lam_im, m_s5_log_step=m_s5_log_step, m_s5_b_re=m_s5_b_re, m_s5_b_im=m_s5_b_im, m_s5_c_re=m_s5_c_re, m_s5_c_im=m_s5_c_im, m_s5_d=m_s5_d, m_s5_w_glu=m_s5_w_glu, m_s5_b_glu=m_s5_b_glu, m_s5_norm=m_s5_norm, m_ssd_conv_w=m_ssd_conv_w, m_ssd_conv_b=m_ssd_conv_b, m_ssd_dt_bias=m_ssd_dt_bias, m_ssd_a_log=m_ssd_a_log, m_ssd_d=m_ssd_d, m_ssd_norm=m_ssd_norm, m_w_out=m_w_out, m_norm_ffn=m_norm_ffn, m_w_gate=m_w_gate, m_w_up=m_w_up, m_w_down=m_w_down, m_norm_final=m_norm_final, v_norm_mix=v_norm_mix, v_w_in=v_w_in, v_s5_lam_re=v_s5_lam_re, v_s5_lam_im=v_s5_lam_im, v_s5_log_step=v_s5_log_step, v_s5_b_re=v_s5_b_re, v_s5_b_im=v_s5_b_im, v_s5_c_re=v_s5_c_re, v_s5_c_im=v_s5_c_im, v_s5_d=v_s5_d, v_s5_w_glu=v_s5_w_glu, v_s5_b_glu=v_s5_b_glu, v_s5_norm=v_s5_norm, v_ssd_conv_w=v_ssd_conv_w, v_ssd_conv_b=v_ssd_conv_b, v_ssd_dt_bias=v_ssd_dt_bias, v_ssd_a_log=v_ssd_a_log, v_ssd_d=v_ssd_d, v_ssd_norm=v_ssd_norm, v_w_out=v_w_out, v_norm_ffn=v_norm_ffn, v_w_gate=v_w_gate, v_w_up=v_w_up, v_w_down=v_w_down, v_norm_final=v_norm_final)
    weights = {n: given[n] for n in TWIN_WEIGHTS}
    shared = {n: given[n] for n in SHARED_INPUTS}
    per_example = {n: given[n] for n in ['x']}
    grad_fn = _jax.value_and_grad(_loss, argnums=(0, 1))

    def one_microbatch(ex, loss_target):
        ex = dict(ex)
        diff = ex.pop(TWIN_DIFF_INPUT)
        return grad_fn(weights, diff, {**shared, **ex}, loss_target)

    if N_MICROBATCH == 1:
        loss, (grad_w, grad_x) = one_microbatch(per_example, given["loss_target"])
    else:
        def body(carry, xs):
            loss_sum, grad_sum = carry
            l_k, (gw_k, gx_k) = one_microbatch(xs[0], xs[1])
            with _jax.named_scope("update"):
                return (loss_sum + l_k, _jax.tree.map(_jnp.add, grad_sum, gw_k)), gx_k

        init = (_jnp.zeros((), _jnp.float32), _jax.tree.map(_jnp.zeros_like, weights))
        (loss, grad_w), grad_x = _jax.lax.scan(body, init, (per_example, given["loss_target"]))
    with _jax.named_scope("update"):
        delta_w, new_m, new_v = {}, {}, {}
        for n in TWIN_WEIGHTS:
            delta_w[n], new_m[n], new_v[n] = _adamw(weights[n], grad_w[n], given["m_" + n], given["v_" + n])
    return (loss, grad_x, *[grad_w[n] for n in TWIN_WEIGHTS], *[delta_w[n] for n in TWIN_WEIGHTS],
            *[new_m[n] for n in TWIN_WEIGHTS], *[new_v[n] for n in TWIN_WEIGHTS])
```

```python
import functools
import math

import jax
import jax.numpy as jnp
from jax import lax
from jax.experimental import pallas as pl
from jax.experimental.pallas import tpu as pltpu

F32 = jnp.float32
BF16 = jnp.bfloat16
MESH = pl.DeviceIdType.MESH
ANY = pl.BlockSpec(memory_space=pl.ANY)

D_MODEL = 1024
DEPTH = 2
S5_GROUPS = 64
S5_GROUP = 16
S5_STATE = 64
S5_COLS = S5_GROUPS * S5_STATE
S5_TILE_GROUPS = 8
S5_TILES = S5_GROUPS // S5_TILE_GROUPS
S5_TILE_IN = S5_TILE_GROUPS * S5_GROUP
S5_TILE_ST = S5_TILE_GROUPS * S5_STATE
SEGS = 8
SSD_HEADS = 16
SSD_HEAD_DIM = 64
SSD_GROUPS = 2
SSD_GROUP_HEADS = SSD_HEADS // SSD_GROUPS
SSD_STATE = 128
SSD_CONV = 4
SSD_CHUNK = 128
SSD_WIDTH = 1024
SSD_CONV_DIM = SSD_WIDTH + 2 * SSD_GROUPS * SSD_STATE
IN_PROJ = 3600
IN_MAIN = 3584
IN_PAD = IN_MAIN + 2 * 128
FFN = 2816
EPS = 1e-6
LANES = 128
ROW_TILE = 256

ADAM_LR = 0.001
ADAM_B1 = 0.9
ADAM_B2 = 0.999
ADAM_EPS = 1e-08
ADAM_WD = 0.01
ADAM_STEP = 10

HIGHEST = lax.Precision.HIGHEST


def _sigmoid(x):
    return 1.0 / (1.0 + jnp.exp(-x))


def _silu(x):
    return x * _sigmoid(x)


def _dsilu(x):
    s = _sigmoid(x)
    return s * (1.0 + x * (1.0 - s))


_GELU_K = math.sqrt(2.0 / math.pi)
_GELU_C = 0.044715


def _gelu(x):
    t = jnp.tanh(_GELU_K * (x + _GELU_C * x * x * x))
    return 0.5 * x * (1.0 + t)


def _dgelu(x):
    t = jnp.tanh(_GELU_K * (x + _GELU_C * x * x * x))
    return 0.5 * (1.0 + t) + 0.5 * x * (1.0 - t * t) * _GELU_K * (1.0 + 3.0 * _GELU_C * x * x)


def _softplus(x):
    e = jnp.exp(-jnp.abs(x))
    u = 1.0 + e
    log1p = jnp.where(u == 1.0, e, jnp.log(u) * e / jnp.where(u == 1.0, 1.0, u - 1.0))
    return jnp.maximum(x, 0.0) + log1p


def _rstd(x):
    return lax.rsqrt(jnp.mean(x * x, axis=-1, keepdims=True) + EPS)


def _rms_bwd(x, r, gain, dy):
    dyg = dy * gain
    dx = r * dyg - x * (r * r * r) * jnp.mean(x * dyg, axis=-1, keepdims=True)
    dgain = jnp.sum(dy * x * r, axis=0, keepdims=True)
    return dx, dgain


def _dot(a, b):
    return jnp.dot(a, b, preferred_element_type=F32)


def _dot_nt(a, b):
    return lax.dot_general(a, b, (((1,), (1,)), ((), ())), preferred_element_type=F32)


def _dot_tn(a, b):
    return lax.dot_general(a, b, (((0,), (0,)), ((), ())), preferred_element_type=F32)


def _row_spec(tile, cols):
    return pl.BlockSpec((tile, cols), lambda i: (i, 0))


def _full_spec(shape):
    nd = len(shape)
    return pl.BlockSpec(shape, lambda *_: (0,) * nd)


def _const_spec(shape):
    nd = len(shape)
    return pl.BlockSpec(shape, lambda *_: (0,) * nd, pipeline_mode=pl.Buffered(1))


def _acc_rows(ref, val, first):
    @pl.when(first)
    def _():
        ref[...] = val

    @pl.when(jnp.logical_not(first))
    def _():
        ref[...] += val


def _pick_tile(n, cap):
    best = LANES
    for t in range(LANES, cap + 1, LANES):
        if n % t == 0:
            best = t
    return best


def _mm_tn(a, b, name):
    k, m = a.shape
    _, n = b.shape
    tm = _pick_tile(m, 512)
    tn = _pick_tile(n, 1536)

    def body(a_ref, b_ref, o_ref):
        o_ref[...] = _dot_tn(a_ref[...], b_ref[...])

    return pl.pallas_call(
        body, name=name, grid=(n // tn, m // tm),
        in_specs=[pl.BlockSpec((k, tm), lambda j, i: (0, i)), pl.BlockSpec((k, tn), lambda j, i: (0, j))],
        out_specs=pl.BlockSpec((tm, tn), lambda j, i: (i, j)),
        out_shape=jax.ShapeDtypeStruct((m, n), F32),
    )(a, b)


def _rms_inproj(x, gain, w_pad, name):
    L = x.shape[0]

    def body(x_ref, g_ref, w_ref, u_ref, z_ref, xbc_ref, dt_ref, h_ref):
        xv = x_ref[...]
        h = (xv * _rstd(xv) * g_ref[...]).astype(BF16)
        h_ref[...] = h
        p = _dot(h, w_ref[...])
        u_ref[...] = p[:, :1024]
        z_ref[...] = p[:, 1024:2048]
        xbc_ref[...] = p[:, 2048:IN_MAIN]
        dt_ref[...] = p[:, IN_MAIN:IN_PAD]

    return pl.pallas_call(
        body, name=name, grid=(L // ROW_TILE,),
        in_specs=[_row_spec(ROW_TILE, D_MODEL), _full_spec((1, D_MODEL)), _const_spec((D_MODEL, IN_PAD))],
        out_specs=[_row_spec(ROW_TILE, 1024), _row_spec(ROW_TILE, 1024), _row_spec(ROW_TILE, SSD_CONV_DIM),
                   _row_spec(ROW_TILE, 256), _row_spec(ROW_TILE, D_MODEL)],
        out_shape=[jax.ShapeDtypeStruct((L, 1024), F32), jax.ShapeDtypeStruct((L, 1024), F32),
                   jax.ShapeDtypeStruct((L, SSD_CONV_DIM), F32), jax.ShapeDtypeStruct((L, 256), F32),
                   jax.ShapeDtypeStruct((L, D_MODEL), BF16)],
    )(x, gain, w_pad)


def _s5_prep_math(lr, li, ls, bre, bim):
    step = jnp.exp(ls)
    mag = jnp.exp(lr * step)
    ang = li * step
    are = mag * jnp.cos(ang)
    aim = mag * jnp.sin(ang)
    den = lr * lr + li * li
    nr = are - 1.0
    ni = aim
    cre = (nr * lr + ni * li) / den
    cim = (ni * lr - nr * li) / den
    bbre = cre[None] * bre - cim[None] * bim
    bbim = cre[None] * bim + cim[None] * bre
    return are, aim, bbre, bbim


def _s5_prep(lr, li, ls, bre, bim, name):
    def body(lr_ref, li_ref, ls_ref, bre_ref, bim_ref, are_ref, aim_ref, bbre_ref, bbim_ref):
        are, aim, bbre, bbim = _s5_prep_math(lr_ref[...], li_ref[...], ls_ref[...], bre_ref[...], bim_ref[...])
        are_ref[...] = are
        aim_ref[...] = aim
        bbre_ref[...] = bbre
        bbim_ref[...] = bbim

    gp = jax.ShapeDtypeStruct((S5_GROUPS, S5_STATE), F32)
    hgp = jax.ShapeDtypeStruct((S5_GROUP, S5_GROUPS, S5_STATE), F32)
    return pl.pallas_call(body, name=name, out_shape=[gp, gp, hgp, hgp])(lr, li, ls, bre, bim)


def _s5_prep_bwd(lr, li, ls, bre, bim, dare, daim, dbbre, dbbim, name):
    def body(lr_ref, li_ref, ls_ref, bre_ref, bim_ref, dare_ref, daim_ref, dbbre_ref, dbbim_ref,
             dlr_ref, dli_ref, dls_ref, dbre_ref, dbim_ref):
        _, vjp = jax.vjp(_s5_prep_math, lr_ref[...], li_ref[...], ls_ref[...], bre_ref[...], bim_ref[...])
        dlr, dli, dls, dbre, dbim = vjp((dare_ref[...], daim_ref[...], dbbre_ref[...], dbbim_ref[...]))
        dlr_ref[...] = dlr
        dli_ref[...] = dli
        dls_ref[...] = dls
        dbre_ref[...] = dbre
        dbim_ref[...] = dbim

    gp = jax.ShapeDtypeStruct((S5_GROUPS, S5_STATE), F32)
    g1 = jax.ShapeDtypeStruct((S5_GROUPS, 1), F32)
    hgp = jax.ShapeDtypeStruct((S5_GROUP, S5_GROUPS, S5_STATE), F32)
    return pl.pallas_call(body, name=name, out_shape=[gp, gp, g1, hgp, hgp])(
        lr, li, ls, bre, bim, dare, daim, dbbre, dbbim)


def _cmul_add(ar, ai, sr, si, br, bi):
    return ar * sr - ai * si + br, ar * si + ai * sr + bi


def _shift_rows_down(v):
    rolled = pltpu.roll(v, 1, 0)
    row = lax.broadcasted_iota(jnp.int32, v.shape, 0)
    return jnp.where(row == 0, 0.0, rolled)


def _shift_rows_up(v):
    rolled = pltpu.roll(v, SEGS - 1, 0)
    row = lax.broadcasted_iota(jnp.int32, v.shape, 0)
    return jnp.where(row == SEGS - 1, 0.0, rolled)


def _segment_power(ar, ai, steps):
    n = 1
    while n < steps:
        ar, ai = ar * ar - ai * ai, 2.0 * ar * ai
        n *= 2
    assert n == steps
    return ar, ai


def _segment_entries(ar, ai, fr, fi, steps, shift):
    pr, pi = _segment_power(ar, ai, steps)
    er = jnp.zeros_like(fr)
    ei = jnp.zeros_like(fi)
    for _ in range(SEGS - 1):
        nr, ni = _cmul_add(pr, pi, er, ei, fr, fi)
        er, ei = shift(nr), shift(ni)
    return er, ei


def _s5_scan(u_perm, bre_bd, bim_bd, cre_bd, cim_bd, are, aim, name):
    L = u_perm.shape[0]
    steps = L // SEGS

    def body(u_ref, bre_ref, bim_ref, cre_ref, cim_ref, are_ref, aim_ref, y_ref, xr_ref, xi_ref):
        u = u_ref[...].astype(BF16)
        xr_ref[...] = _dot(u, bre_ref[0])
        xi_ref[...] = _dot(u, bim_ref[0])
        ar = jnp.broadcast_to(are_ref[0], (SEGS, S5_TILE_ST))
        ai = jnp.broadcast_to(aim_ref[0], (SEGS, S5_TILE_ST))
        zero = jnp.zeros((SEGS, S5_TILE_ST), F32)

        def finals(j, c):
            rows = pl.ds(pl.multiple_of(j * SEGS, SEGS), SEGS)
            return _cmul_add(ar, ai, c[0], c[1], xr_ref[rows, :], xi_ref[rows, :])

        fr, fi = lax.fori_loop(0, steps, finals, (zero, zero), unroll=4)
        er, ei = _segment_entries(ar, ai, fr, fi, steps, _shift_rows_down)

        def scan(j, c):
            rows = pl.ds(pl.multiple_of(j * SEGS, SEGS), SEGS)
            sr, si = _cmul_add(ar, ai, c[0], c[1], xr_ref[rows, :], xi_ref[rows, :])
            xr_ref[rows, :] = sr
            xi_ref[rows, :] = si
            return sr, si

        lax.fori_loop(0, steps, scan, (er, ei), unroll=4)
        y_ref[...] = (_dot(xr_ref[...].astype(BF16), cre_ref[0]) - _dot(xi_ref[...].astype(BF16), cim_ref[0]))

    tile3 = lambda a, b: pl.BlockSpec((1, a, b), lambda k: (k, 0, 0))
    return pl.pallas_call(
        body, name=name, grid=(S5_TILES,),
        in_specs=[pl.BlockSpec((L, S5_TILE_IN), lambda k: (0, k)),
                  tile3(S5_TILE_IN, S5_TILE_ST), tile3(S5_TILE_IN, S5_TILE_ST),
                  tile3(S5_TILE_ST, S5_TILE_IN), tile3(S5_TILE_ST, S5_TILE_IN),
                  tile3(1, S5_TILE_ST), tile3(1, S5_TILE_ST)],
        out_specs=[pl.BlockSpec((L, S5_TILE_IN), lambda k: (0, k)),
                   pl.BlockSpec((L, S5_TILE_ST), lambda k: (0, k)), pl.BlockSpec((L, S5_TILE_ST), lambda k: (0, k))],
        out_shape=[jax.ShapeDtypeStruct((L, 1024), F32), jax.ShapeDtypeStruct((L, S5_COLS), F32),
                   jax.ShapeDtypeStruct((L, S5_COLS), F32)],
    )(u_perm, bre_bd, bim_bd, cre_bd, cim_bd, are, aim)


def _s5_scan_bwd(dy_perm, u_perm, xr, xi, bret_bd, bimt_bd, cret_bd, cimt_bd, are, aim, name):
    L = u_perm.shape[0]
    steps = L // SEGS

    def body(dy_ref, u_ref, xr_ref, xi_ref, bret_ref, bimt_ref, cret_ref, cimt_ref, are_ref, aim_ref,
             du_ref, dar_ref, dai_ref, dcre_ref, dcim_ref, dbre_ref, dbim_ref, gr_ref, gi_ref):
        dy = dy_ref[...].astype(BF16)
        u = u_ref[...].astype(BF16)
        gr_ref[...] = _dot(dy, cret_ref[0])
        gi_ref[...] = -_dot(dy, cimt_ref[0])
        ar = jnp.broadcast_to(are_ref[0], (SEGS, S5_TILE_ST))
        ai = -jnp.broadcast_to(aim_ref[0], (SEGS, S5_TILE_ST))
        zero = jnp.zeros((SEGS, S5_TILE_ST), F32)

        def finals(k, c):
            rows = pl.ds(pl.multiple_of((steps - 1 - k) * SEGS, SEGS), SEGS)
            return _cmul_add(ar, ai, c[0], c[1], gr_ref[rows, :], gi_ref[rows, :])

        fr, fi = lax.fori_loop(0, steps, finals, (zero, zero), unroll=4)
        er, ei = _segment_entries(ar, ai, fr, fi, steps, _shift_rows_up)

        def scan(k, c):
            sr0, si0, accr, acci = c
            j = steps - 1 - k
            rows = pl.ds(pl.multiple_of(j * SEGS, SEGS), SEGS)
            sr, si = _cmul_add(ar, ai, sr0, si0, gr_ref[rows, :], gi_ref[rows, :])
            gr_ref[rows, :] = sr
            gi_ref[rows, :] = si
            prev = pl.ds(pl.multiple_of(jnp.maximum(j - 1, 0) * SEGS, SEGS), SEGS)
            live = (j > 0).astype(F32)
            xpr = xr_ref[prev, :] * live
            xpi = xi_ref[prev, :] * live
            return sr, si, accr + sr * xpr + si * xpi, acci + si * xpr - sr * xpi

        _, _, accr, acci = lax.fori_loop(0, steps, scan, (er, ei, zero, zero), unroll=4)
        first = pl.ds(0, SEGS)
        last = pl.ds((steps - 1) * SEGS, SEGS)
        xpr = _shift_rows_down(xr_ref[last, :])
        xpi = _shift_rows_down(xi_ref[last, :])
        g0r = gr_ref[first, :]
        g0i = gi_ref[first, :]
        accr = accr + g0r * xpr + g0i * xpi
        acci = acci + g0i * xpr - g0r * xpi
        dar_ref[0] = jnp.sum(accr, axis=0, keepdims=True)
        dai_ref[0] = jnp.sum(acci, axis=0, keepdims=True)

        grb = gr_ref[...].astype(BF16)
        gib = gi_ref[...].astype(BF16)
        du_ref[...] = _dot(grb, bret_ref[0]) + _dot(gib, bimt_ref[0])
        dbre_ref[0] = _dot_tn(u, grb)
        dbim_ref[0] = _dot_tn(u, gib)
        dcre_ref[0] = _dot_tn(dy, xr_ref[...].astype(BF16))
        dcim_ref[0] = -_dot_tn(dy, xi_ref[...].astype(BF16))

    tile3 = lambda a, b: pl.BlockSpec((1, a, b), lambda k: (k, 0, 0))
    col_in = pl.BlockSpec((L, S5_TILE_IN), lambda k: (0, k))
    col_st = pl.BlockSpec((L, S5_TILE_ST), lambda k: (0, k))
    dense = jax.ShapeDtypeStruct((S5_TILES, S5_TILE_IN, S5_TILE_ST), F32)
    vec = jax.ShapeDtypeStruct((S5_TILES, 1, S5_TILE_ST), F32)
    return pl.pallas_call(
        body, name=name, grid=(S5_TILES,),
        in_specs=[col_in, col_in, col_st, col_st,
                  tile3(S5_TILE_ST, S5_TILE_IN), tile3(S5_TILE_ST, S5_TILE_IN),
                  tile3(S5_TILE_IN, S5_TILE_ST), tile3(S5_TILE_IN, S5_TILE_ST),
                  tile3(1, S5_TILE_ST), tile3(1, S5_TILE_ST)],
        out_specs=[col_in, tile3(1, S5_TILE_ST), tile3(1, S5_TILE_ST),
                   tile3(S5_TILE_IN, S5_TILE_ST), tile3(S5_TILE_IN, S5_TILE_ST),
                   tile3(S5_TILE_IN, S5_TILE_ST), tile3(S5_TILE_IN, S5_TILE_ST)],
        out_shape=[jax.ShapeDtypeStruct((L, 1024), F32), vec, vec, dense, dense, dense, dense],
        scratch_shapes=[pltpu.VMEM((L, S5_TILE_ST), F32), pltpu.VMEM((L, S5_TILE_ST), F32)],
    )(dy_perm, u_perm, xr, xi, bret_bd, bimt_bd, cret_bd, cimt_bd, are, aim)


def _s5_post(ys, u, d_skip, w_glu, b_glu, gain, name):
    L = ys.shape[0]

    def body(ys_ref, u_ref, d_ref, w_ref, b_ref, g_ref, ya_ref):
        g = _gelu(ys_ref[...] + d_ref[...] * u_ref[...])
        q = _dot(g.astype(BF16), w_ref[...]) + b_ref[...]
        oa = g * _sigmoid(q)
        ya_ref[...] = (oa * _rstd(oa) * g_ref[...]).astype(BF16)

    vec = _full_spec((1, 1024))
    return pl.pallas_call(
        body, name=name, grid=(L // ROW_TILE,),
        in_specs=[_row_spec(ROW_TILE, 1024), _row_spec(ROW_TILE, 1024), vec, _const_spec((1024, 1024)), vec, vec],
        out_specs=_row_spec(ROW_TILE, 1024),
        out_shape=jax.ShapeDtypeStruct((L, 1024), BF16),
    )(ys, u, d_skip, w_glu, b_glu, gain)


def _s5_post_bwd(dx, w_out_a, ys, u, d_skip, w_glu, b_glu, gain, name):
    L = ys.shape[0]

    def body(dx_ref, wo_ref, ys_ref, u_ref, d_ref, w_ref, b_ref, gn_ref,
             dys_ref, dus_ref, g_ref, dq_ref, dgain_ref, dd_ref, db_ref):
        first = pl.program_id(0) == 0
        uv = u_ref[...]
        yt = ys_ref[...] + d_ref[...] * uv
        g = _gelu(yt)
        gb = g.astype(BF16)
        q = _dot(gb, w_ref[...]) + b_ref[...]
        s = _sigmoid(q)
        oa = g * s
        dya = _dot_nt(dx_ref[...], wo_ref[...])
        doa, dgain = _rms_bwd(oa, _rstd(oa), gn_ref[...], dya)
        dq = doa * g * s * (1.0 - s)
        dqb = dq.astype(BF16)
        dg = doa * s + _dot_nt(dqb, w_ref[...])
        dyt = dg * _dgelu(yt)
        dys_ref[...] = dyt
        dus_ref[...] = dyt * d_ref[...]
        g_ref[...] = gb
        dq_ref[...] = dqb
        _acc_rows(dgain_ref, dgain, first)
        _acc_rows(dd_ref, jnp.sum(dyt * uv, axis=0, keepdims=True), first)
        _acc_rows(db_ref, jnp.sum(dq, axis=0, keepdims=True), first)

    vec = _full_spec((1, 1024))
    row = _row_spec(ROW_TILE, 1024)
    vshape = jax.ShapeDtypeStruct((1, 1024), F32)
    return pl.pallas_call(
        body, name=name, grid=(L // ROW_TILE,),
        in_specs=[row, _const_spec((1024, 1024)), row, row, vec, _const_spec((1024, 1024)), vec, vec],
        out_specs=[row, row, row, row, vec, vec, vec],
        out_shape=[jax.ShapeDtypeStruct((L, 1024), F32), jax.ShapeDtypeStruct((L, 1024), F32),
                   jax.ShapeDtypeStruct((L, 1024), BF16), jax.ShapeDtypeStruct((L, 1024), BF16),
                   vshape, vshape, vshape],
    )(dx, w_out_a, ys, u, d_skip, w_glu, b_glu, gain)


CONV_TILE = 256


def _shift_time(v, d):
    if d == 0:
        return v
    rolled = pltpu.roll(v, d, 0)
    row = lax.broadcasted_iota(jnp.int32, v.shape, 0)
    return jnp.where(row < d, 0.0, rolled)


def _unshift_time(v, d):
    if d == 0:
        return v
    n = v.shape[0]
    rolled = pltpu.roll(v, n - d, 0)
    row = lax.broadcasted_iota(jnp.int32, v.shape, 0)
    return jnp.where(row >= n - d, 0.0, rolled)


def _ssd_conv(xbc, w, b, name):
    L = xbc.shape[0]

    def body(x_ref, w_ref, b_ref, o_ref):
        xv = x_ref[...]
        pre = jnp.broadcast_to(b_ref[...], xv.shape)
        for k in range(SSD_CONV):
            pre = pre + w_ref[k:k + 1, :] * _shift_time(xv, SSD_CONV - 1 - k)
        o_ref[...] = _silu(pre)

    col = pl.BlockSpec((L, CONV_TILE), lambda j: (0, j))
    return pl.pallas_call(
        body, name=name, grid=(SSD_CONV_DIM // CONV_TILE,),
        in_specs=[col, pl.BlockSpec((8, CONV_TILE), lambda j: (0, j)), pl.BlockSpec((1, CONV_TILE), lambda j: (0, j))],
        out_specs=col, out_shape=jax.ShapeDtypeStruct((L, SSD_CONV_DIM), F32),
    )(xbc, w, b)


def _ssd_conv_bwd(dxc, xbc, w, b, name):
    L = xbc.shape[0]

    def body(d_ref, x_ref, w_ref, b_ref, dx_ref, dw_ref, db_ref):
        xv = x_ref[...]
        shifted = [_shift_time(xv, SSD_CONV - 1 - k) for k in range(SSD_CONV)]
        pre = jnp.broadcast_to(b_ref[...], xv.shape)
        for k in range(SSD_CONV):
            pre = pre + w_ref[k:k + 1, :] * shifted[k]
        dpre = d_ref[...] * _dsilu(pre)
        dx = jnp.zeros_like(xv)
        rows = []
        for k in range(SSD_CONV):
            dx = dx + w_ref[k:k + 1, :] * _unshift_time(dpre, SSD_CONV - 1 - k)
            rows.append(jnp.sum(dpre * shifted[k], axis=0, keepdims=True))
        dx_ref[...] = dx
        dw_ref[...] = jnp.concatenate(rows + [jnp.zeros((8 - SSD_CONV, CONV_TILE), F32)], axis=0)
        db_ref[...] = jnp.sum(dpre, axis=0, keepdims=True)

    col = pl.BlockSpec((L, CONV_TILE), lambda j: (0, j))
    w_spec = pl.BlockSpec((8, CONV_TILE), lambda j: (0, j))
    b_spec = pl.BlockSpec((1, CONV_TILE), lambda j: (0, j))
    return pl.pallas_call(
        body, name=name, grid=(SSD_CONV_DIM // CONV_TILE,),
        in_specs=[col, col, w_spec, b_spec], out_specs=[col, w_spec, b_spec],
        out_shape=[jax.ShapeDtypeStruct((L, SSD_CONV_DIM), F32), jax.ShapeDtypeStruct((8, SSD_CONV_DIM), F32),
                   jax.ShapeDtypeStruct((1, SSD_CONV_DIM), F32)],
    )(dxc, xbc, w, b)


def _tri(lower):
    r = lax.broadcasted_iota(jnp.int32, (SSD_CHUNK, SSD_CHUNK), 0)
    c = lax.broadcasted_iota(jnp.int32, (SSD_CHUNK, SSD_CHUNK), 1)
    return (r >= c) if lower else (r <= c)


def _ssd_chunk_common(dt_ref, bias_ref, alog_ref):
    pre = dt_ref[...] + bias_ref[0]
    dtp = _softplus(pre)
    a_neg = -jnp.exp(alog_ref[0])
    dta = dtp * a_neg
    ltri = _tri(True).astype(F32)
    acum = jnp.dot(ltri, dta, precision=HIGHEST, preferred_element_type=F32)
    return pre, dtp, a_neg, dta, acum


def _ssd_scan(xc, dt, dt_bias, a_log, d_skip, name):
    L = xc.shape[0]
    nc = L // SSD_CHUNK
    gw = SSD_GROUP_HEADS * SSD_HEAD_DIM

    def body(x_ref, b_ref, c_ref, dt_ref, bias_ref, alog_ref, d_ref, y_ref, sp_ref, s_ref):
        @pl.when(pl.program_id(1) == 0)
        def _():
            s_ref[...] = jnp.zeros_like(s_ref)

        _, dtp_all, _, _, acum_all = _ssd_chunk_common(dt_ref, bias_ref, alog_ref)
        acum_t = acum_all.T
        bm = b_ref[...].astype(BF16)
        cm = c_ref[...].astype(BF16)
        cb = _dot_nt(cm, bm)
        lower = _tri(True)
        for h in range(SSD_GROUP_HEADS):
            acum = acum_all[:, h:h + 1]
            dtp = dtp_all[:, h:h + 1]
            lm = jnp.where(lower, jnp.exp(jnp.minimum(acum - acum_t[h:h + 1, :], 0.0)), 0.0)
            xh = x_ref[:, h * SSD_HEAD_DIM:(h + 1) * SSD_HEAD_DIM]
            xdt = xh * dtp
            sp = s_ref[h]
            y = _dot((cb * lm).astype(BF16), xdt.astype(BF16))
            y = y + _dot_nt(cm, sp.astype(BF16)) * jnp.exp(acum)
            y = y + d_ref[0, :, h:h + 1] * xh
            y_ref[:, h * SSD_HEAD_DIM:(h + 1) * SSD_HEAD_DIM] = y
            alast = acum[SSD_CHUNK - 1:SSD_CHUNK, :]
            wgt = xdt * jnp.exp(alast - acum)
            sp_ref[0, 0, h] = sp
            s_ref[h] = jnp.exp(alast) * sp + _dot_tn(wgt.astype(BF16), bm)

    par = lambda: pl.BlockSpec((1, 1, LANES), lambda g, c: (g, 0, 0))
    return pl.pallas_call(
        body, name=name, grid=(SSD_GROUPS, nc),
        in_specs=[pl.BlockSpec((SSD_CHUNK, gw), lambda g, c: (c, g)),
                  pl.BlockSpec((SSD_CHUNK, SSD_STATE), lambda g, c: (c, 8 + g)),
                  pl.BlockSpec((SSD_CHUNK, SSD_STATE), lambda g, c: (c, 10 + g)),
                  pl.BlockSpec((SSD_CHUNK, LANES), lambda g, c: (c, g)),
                  par(), par(), par()],
        out_specs=[pl.BlockSpec((SSD_CHUNK, gw), lambda g, c: (c, g)),
                   pl.BlockSpec((1, 1, SSD_GROUP_HEADS, SSD_HEAD_DIM, SSD_STATE), lambda g, c: (c, g, 0, 0, 0))],
        out_shape=[jax.ShapeDtypeStruct((L, SSD_WIDTH), F32),
                   jax.ShapeDtypeStruct((nc, SSD_GROUPS, SSD_GROUP_HEADS, SSD_HEAD_DIM, SSD_STATE), F32)],
        scratch_shapes=[pltpu.VMEM((SSD_GROUP_HEADS, SSD_HEAD_DIM, SSD_STATE), F32)],
    )(xc, xc, xc, dt, dt_bias, a_log, d_skip)


def _ssd_scan_bwd(dy, xc, dt, sprev, dt_bias, a_log, d_skip, name):
    L = xc.shape[0]
    nc = L // SSD_CHUNK
    gw = SSD_GROUP_HEADS * SSD_HEAD_DIM

    def body(dy_ref, x_ref, b_ref, c_ref, dt_ref, sp_ref, bias_ref, alog_ref, d_ref,
             dx_ref, db_ref, dc_ref, ddt_ref, dbias_ref, dalog_ref, dd_ref, ds_ref):
        first = pl.program_id(1) == 0

        @pl.when(first)
        def _():
            ds_ref[...] = jnp.zeros_like(ds_ref)

        pre, dtp_all, a_neg, _, acum_all = _ssd_chunk_common(dt_ref, bias_ref, alog_ref)
        acum_t = acum_all.T
        bm = b_ref[...].astype(BF16)
        cm = c_ref[...].astype(BF16)
        cb = _dot_nt(cm, bm)
        lower = _tri(True)
        lane = lax.broadcasted_iota(jnp.int32, (SSD_CHUNK, LANES), 1)
        lane1 = lax.broadcasted_iota(jnp.int32, (1, LANES), 1)
        last_row = lax.broadcasted_iota(jnp.int32, (SSD_CHUNK, 1), 0) == SSD_CHUNK - 1
        ones_col = jnp.ones((SSD_CHUNK, 1), F32)
        dacum_all = jnp.zeros((SSD_CHUNK, LANES), F32)
        ddtp_all = jnp.zeros((SSD_CHUNK, LANES), F32)
        dd_row = jnp.zeros((1, LANES), F32)
        dcb = jnp.zeros((SSD_CHUNK, SSD_CHUNK), F32)
        db_acc = jnp.zeros((SSD_CHUNK, SSD_STATE), F32)
        dc_acc = jnp.zeros((SSD_CHUNK, SSD_STATE), F32)
        for h in range(SSD_GROUP_HEADS):
            cols = slice(h * SSD_HEAD_DIM, (h + 1) * SSD_HEAD_DIM)
            acum = acum_all[:, h:h + 1]
            dtp = dtp_all[:, h:h + 1]
            lm = jnp.where(lower, jnp.exp(jnp.minimum(acum - acum_t[h:h + 1, :], 0.0)), 0.0)
            xh = x_ref[:, cols]
            dyh = dy_ref[:, cols]
            dyb = dyh.astype(BF16)
            xdt = xh * dtp
            xdtb = xdt.astype(BF16)
            sp = sp_ref[0, 0, h]
            spb = sp.astype(BF16)
            dsn = ds_ref[h]
            dsb = dsn.astype(BF16)
            eacum = jnp.exp(acum)
            alast = acum[SSD_CHUNK - 1:SSD_CHUNK, :]
            ealast = jnp.exp(alast)
            dstate = jnp.exp(alast - acum)
            mb = (cb * lm).astype(BF16)
            dm = _dot_nt(dyb, xdtb)
            dxdt = _dot_tn(mb, dyb)
            dlm_lm = dm * cb * lm
            dcb = dcb + dm * lm
            dacum = jnp.sum(dlm_lm, axis=1, keepdims=True) - lax.dot_general(
                dlm_lm, ones_col, (((0,), (0,)), ((), ())), precision=HIGHEST, preferred_element_type=F32)
            z = _dot_nt(cm, spb)
            dz = dyh * eacum
            dzb = dz.astype(BF16)
            dacum = dacum + jnp.sum(dz * z, axis=1, keepdims=True)
            dc_acc = dc_acc + _dot(dzb, spb)
            dsp = _dot_tn(dzb, cm) + ealast * dsn
            dw = _dot_nt(bm, dsb)
            wgt = xdt * dstate
            db_acc = db_acc + _dot(wgt.astype(BF16), dsb)
            dxdt = dxdt + dw * dstate
            dds_ds = jnp.sum(dw * xdt, axis=1, keepdims=True) * dstate
            dalast = jnp.sum(dsn * sp, keepdims=True).reshape(1, 1) * ealast + jnp.sum(dds_ds, axis=0, keepdims=True)
            dacum = dacum - dds_ds + jnp.where(last_row, dalast, 0.0)
            ds_ref[h] = dsp
            dx_ref[:, cols] = d_ref[0, :, h:h + 1] * dyh + dxdt * dtp
            dd_row = dd_row + jnp.where(lane1 == h, jnp.sum(dyh * xh, keepdims=True).reshape(1, 1), 0.0)
            ddtp_all = jnp.where(lane == h, jnp.sum(dxdt * xh, axis=1, keepdims=True), ddtp_all)
            dacum_all = jnp.where(lane == h, dacum, dacum_all)
        dcbb = dcb.astype(BF16)
        dc_ref[...] = dc_acc + _dot(dcbb, bm)
        db_ref[...] = db_acc + _dot_tn(dcbb, cm)
        utri = _tri(False).astype(F32)
        ddta = jnp.dot(utri, dacum_all, precision=HIGHEST, preferred_element_type=F32)
        ddtp_all = ddtp_all + ddta * a_neg
        ddt = ddtp_all * _sigmoid(pre)
        ddt_ref[...] = ddt
        _acc_rows(dbias_ref, jnp.sum(ddt, axis=0, keepdims=True)[None], first)
        _acc_rows(dalog_ref, (jnp.sum(ddta * dtp_all, axis=0, keepdims=True) * a_neg)[None], first)
        _acc_rows(dd_ref, dd_row[None], first)

    rev = lambda c: nc - 1 - c
    par = lambda: pl.BlockSpec((1, 1, LANES), lambda g, c: (g, 0, 0))
    pshape = jax.ShapeDtypeStruct((SSD_GROUPS, 1, LANES), F32)
    return pl.pallas_call(
        body, name=name, grid=(SSD_GROUPS, nc),
        in_specs=[pl.BlockSpec((SSD_CHUNK, gw), lambda g, c: (rev(c), g)),
                  pl.BlockSpec((SSD_CHUNK, gw), lambda g, c: (rev(c), g)),
                  pl.BlockSpec((SSD_CHUNK, SSD_STATE), lambda g, c: (rev(c), 8 + g)),
                  pl.BlockSpec((SSD_CHUNK, SSD_STATE), lambda g, c: (rev(c), 10 + g)),
                  pl.BlockSpec((SSD_CHUNK, LANES), lambda g, c: (rev(c), g)),
                  pl.BlockSpec((1, 1, SSD_GROUP_HEADS, SSD_HEAD_DIM, SSD_STATE), lambda g, c: (rev(c), g, 0, 0, 0)),
                  par(), par(), par()],
        out_specs=[pl.BlockSpec((SSD_CHUNK, gw), lambda g, c: (rev(c), g)),
                   pl.BlockSpec((SSD_CHUNK, SSD_STATE), lambda g, c: (rev(c), g)),
                   pl.BlockSpec((SSD_CHUNK, SSD_STATE), lambda g, c: (rev(c), g)),
                   pl.BlockSpec((SSD_CHUNK, LANES), lambda g, c: (rev(c), g)),
                   par(), par(), par()],
        out_shape=[jax.ShapeDtypeStruct((L, SSD_WIDTH), F32), jax.ShapeDtypeStruct((L, 256), F32),
                   jax.ShapeDtypeStruct((L, 256), F32), jax.ShapeDtypeStruct((L, 256), F32),
                   pshape, pshape, pshape],
        scratch_shapes=[pltpu.VMEM((SSD_GROUP_HEADS, SSD_HEAD_DIM, SSD_STATE), F32)],
    )(dy, xc, xc, xc, dt, sprev, dt_bias, a_log, d_skip)


def _ssd_post(y, z, gain, name):
    L = y.shape[0]

    def body(y_ref, z_ref, g_ref, o_ref):
        ob = y_ref[...] * _silu(z_ref[...])
        o_ref[...] = (ob * _rstd(ob) * g_ref[...]).astype(BF16)

    row = _row_spec(ROW_TILE, 1024)
    return pl.pallas_call(body, name=name, grid=(L // ROW_TILE,), in_specs=[row, row, _full_spec((1, 1024))],
                          out_specs=row, out_shape=jax.ShapeDtypeStruct((L, 1024), BF16))(y, z, gain)


def _ssd_post_bwd(dx, w_out_b, y, z, gain, name):
    L = y.shape[0]

    def body(dx_ref, wo_ref, y_ref, z_ref, g_ref, dy_ref, dz_ref, dgain_ref):
        first = pl.program_id(0) == 0
        yv = y_ref[...]
        zv = z_ref[...]
        sz = _silu(zv)
        ob = yv * sz
        dyb = _dot_nt(dx_ref[...], wo_ref[...])
        dob, dgain = _rms_bwd(ob, _rstd(ob), g_ref[...], dyb)
        dy_ref[...] = dob * sz
        dz_ref[...] = dob * yv * _dsilu(zv)
        _acc_rows(dgain_ref, dgain, first)

    row = _row_spec(ROW_TILE, 1024)
    vec = _full_spec((1, 1024))
    return pl.pallas_call(
        body, name=name, grid=(L // ROW_TILE,), in_specs=[row, _const_spec((1024, 1024)), row, row, vec],
        out_specs=[row, row, vec],
        out_shape=[jax.ShapeDtypeStruct((L, 1024), F32), jax.ShapeDtypeStruct((L, 1024), F32),
                   jax.ShapeDtypeStruct((1, 1024), F32)],
    )(dx, w_out_b, y, z, gain)


def _out_proj(x, ya, yb, w_out, name):
    L = x.shape[0]

    def body(x_ref, ya_ref, yb_ref, w_ref, o_ref):
        o_ref[...] = x_ref[...] + _dot(ya_ref[...], w_ref[:1024, :]) + _dot(yb_ref[...], w_ref[1024:, :])

    row = _row_spec(ROW_TILE, 1024)
    return pl.pallas_call(body, name=name, grid=(L // ROW_TILE,), in_specs=[row, row, row, _const_spec((2048, 1024))],
                          out_specs=row, out_shape=jax.ShapeDtypeStruct((L, D_MODEL), F32))(x, ya, yb, w_out)


def _ffn(x, gain, w_gate, w_up, w_down, name):
    L = x.shape[0]

    def body(x_ref, g_ref, wg_ref, wu_ref, wd_ref, o_ref, gt_ref, up_ref):
        xv = x_ref[...]
        h = (xv * _rstd(xv) * g_ref[...]).astype(BF16)
        gt = _dot(h, wg_ref[...])
        up = _dot(h, wu_ref[...])
        gt_ref[...] = gt
        up_ref[...] = up
        o_ref[...] = xv + _dot((_silu(gt) * up).astype(BF16), wd_ref[...])

    row = _row_spec(ROW_TILE, D_MODEL)
    hid = _row_spec(ROW_TILE, FFN)
    return pl.pallas_call(
        body, name=name, grid=(L // ROW_TILE,),
        in_specs=[row, _full_spec((1, D_MODEL)), _const_spec((D_MODEL, FFN)), _const_spec((D_MODEL, FFN)),
                  _const_spec((FFN, D_MODEL))],
        out_specs=[row, hid, hid],
        out_shape=[jax.ShapeDtypeStruct((L, D_MODEL), F32), jax.ShapeDtypeStruct((L, FFN), F32),
                   jax.ShapeDtypeStruct((L, FFN), F32)],
    )(x, gain, w_gate, w_up, w_down)


def _ffn_bwd(dx2, x1, gt, up, gain, w_gate, w_up, w_down, name):
    L = x1.shape[0]

    def body(d_ref, x_ref, gt_ref, up_ref, g_ref, wg_ref, wu_ref, wd_ref,
             dx_ref, h_ref, act_ref, dgt_ref, dup_ref, dgain_ref):
        first = pl.program_id(0) == 0
        dv = d_ref[...]
        xv = x_ref[...]
        r = _rstd(xv)
        h_ref[...] = (xv * r * g_ref[...]).astype(BF16)
        gtv = gt_ref[...]
        upv = up_ref[...]
        sg = _silu(gtv)
        act_ref[...] = (sg * upv).astype(BF16)
        dact = _dot_nt(dv.astype(BF16), wd_ref[...])
        dgt = (dact * upv * _dsilu(gtv)).astype(BF16)
        dup = (dact * sg).astype(BF16)
        dgt_ref[...] = dgt
        dup_ref[...] = dup
        dh = _dot_nt(dgt, wg_ref[...]) + _dot_nt(dup, wu_ref[...])
        dxn, dgain = _rms_bwd(xv, r, g_ref[...], dh)
        dx_ref[...] = dv + dxn
        _acc_rows(dgain_ref, dgain, first)

    row = _row_spec(ROW_TILE, D_MODEL)
    hid = _row_spec(ROW_TILE, FFN)
    vec = _full_spec((1, D_MODEL))
    return pl.pallas_call(
        body, name=name, grid=(L // ROW_TILE,),
        in_specs=[row, row, hid, hid, vec, _const_spec((D_MODEL, FFN)), _const_spec((D_MODEL, FFN)),
                  _const_spec((FFN, D_MODEL))],
        out_specs=[row, row, hid, hid, hid, vec],
        out_shape=[jax.ShapeDtypeStruct((L, D_MODEL), F32), jax.ShapeDtypeStruct((L, D_MODEL), BF16),
                   jax.ShapeDtypeStruct((L, FFN), BF16), jax.ShapeDtypeStruct((L, FFN), BF16),
                   jax.ShapeDtypeStruct((L, FFN), BF16), jax.ShapeDtypeStruct((1, D_MODEL), F32)],
    )(dx2, x1, gt, up, gain, w_gate, w_up, w_down)


def _inproj_bwd(dx1, x0, dproj, gain, w_pad, name):
    L = x0.shape[0]

    def body(d_ref, x_ref, dp_ref, g_ref, w_ref, dx_ref, dgain_ref):
        first = pl.program_id(0) == 0
        xv = x_ref[...]
        dh = _dot_nt(dp_ref[...], w_ref[...])
        dxn, dgain = _rms_bwd(xv, _rstd(xv), g_ref[...], dh)
        dx_ref[...] = d_ref[...] + dxn
        _acc_rows(dgain_ref, dgain, first)

    row = _row_spec(ROW_TILE, D_MODEL)
    vec = _full_spec((1, D_MODEL))
    return pl.pallas_call(
        body, name=name, grid=(L // ROW_TILE,),
        in_specs=[row, row, _row_spec(ROW_TILE, IN_PAD), vec, _const_spec((D_MODEL, IN_PAD))],
        out_specs=[row, vec],
        out_shape=[jax.ShapeDtypeStruct((L, D_MODEL), F32), jax.ShapeDtypeStruct((1, D_MODEL), F32)],
    )(dx1, x0, dproj, gain, w_pad)


def _final_loss(x, gain, target, name):
    L = x.shape[0]

    def body(x_ref, g_ref, t_ref, loss_ref, dx_ref, dxb_ref, dgain_ref):
        first = pl.program_id(0) == 0
        xv = x_ref[...]
        r = _rstd(xv)
        err = xv * r * g_ref[...] - t_ref[...]
        part = 0.5 * jnp.sum(jnp.mean(err * err, axis=-1, keepdims=True), axis=0, keepdims=True)
        dx, dgain = _rms_bwd(xv, r, g_ref[...], err * (1.0 / D_MODEL))
        dx_ref[...] = dx
        dxb_ref[...] = dx.astype(BF16)
        _acc_rows(loss_ref, jnp.broadcast_to(part, (1, LANES)), first)
        _acc_rows(dgain_ref, dgain, first)

    row = _row_spec(ROW_TILE, D_MODEL)
    vec = _full_spec((1, D_MODEL))
    return pl.pallas_call(
        body, name=name, grid=(L // ROW_TILE,), in_specs=[row, vec, row],
        out_specs=[_full_spec((1, LANES)), row, row, vec],
        out_shape=[jax.ShapeDtypeStruct((1, LANES), F32), jax.ShapeDtypeStruct((L, D_MODEL), F32),
                   jax.ShapeDtypeStruct((L, D_MODEL), BF16), jax.ShapeDtypeStruct((1, D_MODEL), F32)],
    )(x, gain, target)


def _to_segments(a):
    L, n = a.shape
    return a.reshape(SEGS, L // SEGS, n).transpose(1, 0, 2).reshape(L, n)


def _from_segments(a):
    L, n = a.shape
    return a.reshape(L // SEGS, SEGS, n).transpose(1, 0, 2).reshape(L, n)


def _block_diag_in_to_state(m):
    m = m.reshape(S5_TILES, S5_TILE_GROUPS, S5_GROUP, S5_STATE)
    eye = jnp.eye(S5_TILE_GROUPS, dtype=m.dtype)
    out = m[:, :, :, None, :] * eye[None, :, None, :, None]
    return out.reshape(S5_TILES, S5_TILE_IN, S5_TILE_ST)


def _block_diag_extract(d):
    d = d.reshape(S5_TILES, S5_TILE_GROUPS, S5_GROUP, S5_TILE_GROUPS, S5_STATE)
    d = jnp.stack([d[:, a, :, a, :] for a in range(S5_TILE_GROUPS)], axis=1)
    return d.reshape(S5_GROUPS, S5_GROUP, S5_STATE)


def _pad_in_proj(w):
    z = jnp.zeros((w.shape[0], LANES - SSD_GROUP_HEADS), w.dtype)
    return jnp.concatenate([w[:, :IN_MAIN], w[:, IN_MAIN:IN_MAIN + 8], z, w[:, IN_MAIN + 8:], z], axis=1)


def _unpad_in_proj(w):
    return jnp.concatenate([w[:, :IN_MAIN + 8], w[:, IN_MAIN + LANES:IN_MAIN + LANES + 8]], axis=1)


def _pad_heads(v):
    v = v.reshape(SSD_GROUPS, 1, SSD_GROUP_HEADS)
    return jnp.pad(v, ((0, 0), (0, 0), (0, LANES - SSD_GROUP_HEADS)))


def _unpad_heads(v):
    return v[:, 0, :SSD_GROUP_HEADS].reshape(SSD_HEADS)


def _layer_forward(x0, p, i):
    tag = "l%d_" % i
    ls = p["s5_log_step"].reshape(S5_GROUPS, 1)
    b_hgp = (p["s5_b_re"].transpose(2, 0, 1), p["s5_b_im"].transpose(2, 0, 1))
    are, aim, bbre, bbim = _s5_prep(p["s5_lam_re"], p["s5_lam_im"], ls, b_hgp[0], b_hgp[1], tag + "s5_prep")
    bre_ghp = bbre.transpose(1, 0, 2)
    bim_ghp = bbim.transpose(1, 0, 2)
    bre_bd = _block_diag_in_to_state(bre_ghp).astype(BF16)
    bim_bd = _block_diag_in_to_state(bim_ghp).astype(BF16)
    cret_bd = _block_diag_in_to_state(p["s5_c_re"]).astype(BF16)
    cimt_bd = _block_diag_in_to_state(p["s5_c_im"]).astype(BF16)
    s5mats = dict(bre_bd=bre_bd, bim_bd=bim_bd, cret_bd=cret_bd, cimt_bd=cimt_bd,
                  bret_bd=bre_bd.transpose(0, 2, 1), bimt_bd=bim_bd.transpose(0, 2, 1),
                  cre_bd=cret_bd.transpose(0, 2, 1), cim_bd=cimt_bd.transpose(0, 2, 1),
                  are=are.reshape(S5_TILES, 1, S5_TILE_ST), aim=aim.reshape(S5_TILES, 1, S5_TILE_ST))

    w_pad = _pad_in_proj(p["w_in"])
    u, z, xbc, dt, h1 = _rms_inproj(x0, p["norm_mix"].reshape(1, -1), w_pad, tag + "rms_inproj")
    u_perm = _to_segments(u)
    ys_perm, xr, xi = _s5_scan(u_perm, bre_bd, bim_bd, s5mats["cre_bd"], s5mats["cim_bd"],
                               s5mats["are"], s5mats["aim"], tag + "s5_scan")
    ys = _from_segments(ys_perm)
    ya = _s5_post(ys, u, p["s5_d"].reshape(1, -1), p["s5_w_glu"], p["s5_b_glu"].reshape(1, -1),
                  p["s5_norm"].reshape(1, -1), tag + "s5_post")

    conv_w = jnp.pad(p["ssd_conv_w"], ((0, 8 - SSD_CONV), (0, 0)))
    conv_b = p["ssd_conv_b"].reshape(1, -1)
    xc = _ssd_conv(xbc, conv_w, conv_b, tag + "ssd_conv")
    heads = dict(dt_bias=_pad_heads(p["ssd_dt_bias"]), a_log=_pad_heads(p["ssd_a_log"]), d=_pad_heads(p["ssd_d"]))
    y, sprev = _ssd_scan(xc, dt, heads["dt_bias"], heads["a_log"], heads["d"], tag + "ssd_scan")
    yb = _ssd_post(y, z, p["ssd_norm"].reshape(1, -1), tag + "ssd_post")

    x1 = _out_proj(x0, ya, yb, p["w_out"], tag + "out_proj")
    x2, gt, up = _ffn(x1, p["norm_ffn"].reshape(1, -1), p["w_gate"], p["w_up"], p["w_down"], tag + "ffn")
    saved = dict(x0=x0, h1=h1, u=u, u_perm=u_perm, z=z, xbc=xbc, dt=dt, xr=xr, xi=xi, ys=ys, ya=ya, xc=xc, y=y,
                 sprev=sprev, yb=yb, x1=x1, gt=gt, up=up, w_pad=w_pad, s5mats=s5mats, heads=heads, conv_w=conv_w,
                 conv_b=conv_b, ls=ls, b_hgp=b_hgp)
    return x2, saved


def _layer_backward(dx2, p, s, i):
    tag = "l%d_" % i
    g = {}
    dx1, h2, act, dgt, dup, dgain = _ffn_bwd(dx2, s["x1"], s["gt"], s["up"], p["norm_ffn"].reshape(1, -1),
                                            p["w_gate"], p["w_up"], p["w_down"], tag + "ffn_bwd")
    g["norm_ffn"] = dgain[0]
    g["w_down"] = _mm_tn(act, dx2.astype(BF16), tag + "dw_down")
    g["w_gate"] = _mm_tn(h2, dgt, tag + "dw_gate")
    g["w_up"] = _mm_tn(h2, dup, tag + "dw_up")

    dx1b = dx1.astype(BF16)
    g["w_out"] = _mm_tn(jnp.concatenate([s["ya"], s["yb"]], axis=1), dx1b, tag + "dw_out")

    dys, du_skip, gelu_b, dq_b, dgain, dd, dbg = _s5_post_bwd(
        dx1b, p["w_out"][:1024], s["ys"], s["u"], p["s5_d"].reshape(1, -1), p["s5_w_glu"],
        p["s5_b_glu"].reshape(1, -1), p["s5_norm"].reshape(1, -1), tag + "s5_post_bwd")
    g["s5_norm"] = dgain[0]
    g["s5_d"] = dd[0]
    g["s5_b_glu"] = dbg[0]
    g["s5_w_glu"] = _mm_tn(gelu_b, dq_b, tag + "dw_glu")
    m = s["s5mats"]
    du_perm, dar, dai, dcre_d, dcim_d, dbre_d, dbim_d = _s5_scan_bwd(
        _to_segments(dys), s["u_perm"], s["xr"], s["xi"], m["bret_bd"], m["bimt_bd"], m["cret_bd"], m["cimt_bd"],
        m["are"], m["aim"], tag + "s5_scan_bwd")
    du = du_skip + _from_segments(du_perm)
    g["s5_c_re"] = _block_diag_extract(dcre_d)
    g["s5_c_im"] = _block_diag_extract(dcim_d)
    dbbre = _block_diag_extract(dbre_d).transpose(1, 0, 2)
    dbbim = _block_diag_extract(dbim_d).transpose(1, 0, 2)
    dlr, dli, dls, dbre, dbim = _s5_prep_bwd(
        p["s5_lam_re"], p["s5_lam_im"], s["ls"], s["b_hgp"][0], s["b_hgp"][1],
        dar.reshape(S5_GROUPS, S5_STATE), dai.reshape(S5_GROUPS, S5_STATE), dbbre, dbbim, tag + "s5_prep_bwd")
    g["s5_lam_re"] = dlr
    g["s5_lam_im"] = dli
    g["s5_log_step"] = dls[:, 0]
    g["s5_b_re"] = dbre.transpose(1, 2, 0)
    g["s5_b_im"] = dbim.transpose(1, 2, 0)

    dy, dz, dgain = _ssd_post_bwd(dx1b, p["w_out"][1024:], s["y"], s["z"], p["ssd_norm"].reshape(1, -1),
                                  tag + "ssd_post_bwd")
    g["ssd_norm"] = dgain[0]
    hd = s["heads"]
    dxs, dbm, dcm, ddt, dbias, dalog, dd = _ssd_scan_bwd(dy, s["xc"], s["dt"], s["sprev"], hd["dt_bias"],
                                                       hd["a_log"], hd["d"], tag + "ssd_scan_bwd")
    g["ssd_dt_bias"] = _unpad_heads(dbias)
    g["ssd_a_log"] = _unpad_heads(dalog)
    g["ssd_d"] = _unpad_heads(dd)
    dxc = jnp.concatenate([dxs, dbm, dcm], axis=1)
    dxbc, dcw, dcb = _ssd_conv_bwd(dxc, s["xbc"], s["conv_w"], s["conv_b"], tag + "ssd_conv_bwd")
    g["ssd_conv_w"] = dcw[:SSD_CONV]
    g["ssd_conv_b"] = dcb[0]

    dproj = jnp.concatenate([du, dz, dxbc, ddt], axis=1).astype(BF16)
    dx0, dgain = _inproj_bwd(dx1, s["x0"], dproj, p["norm_mix"].reshape(1, -1), s["w_pad"], tag + "inproj_bwd")
    g["norm_mix"] = dgain[0]
    g["w_in"] = _unpad_in_proj(_mm_tn(s["h1"], dproj, tag + "dw_in"))
    return dx0, g


BIG = ("w_in", "s5_w_glu", "w_out", "w_gate", "w_up", "w_down")
COL_SHARDED = ("w_in", "w_gate", "w_up")
LAYER_SMALL = ("norm_mix", "s5_lam_re", "s5_lam_im", "s5_log_step", "s5_b_re", "s5_b_im", "s5_c_re", "s5_c_im",
               "s5_d", "s5_b_glu", "s5_norm", "ssd_conv_w", "ssd_conv_b", "ssd_dt_bias", "ssd_a_log", "ssd_d",
               "ssd_norm", "norm_ffn")
WEIGHTS = ("norm_mix", "w_in", "s5_lam_re", "s5_lam_im", "s5_log_step", "s5_b_re", "s5_b_im", "s5_c_re", "s5_c_im",
           "s5_d", "s5_w_glu", "s5_b_glu", "s5_norm", "ssd_conv_w", "ssd_conv_b", "ssd_dt_bias", "ssd_a_log",
           "ssd_d", "ssd_norm", "w_out", "norm_ffn", "w_gate", "w_up", "w_down", "norm_final")


def _local_step(x, target, big, small, norm_final):
    saved = []
    h = x
    for i in range(DEPTH):
        p = {k: v[i] for k, v in big.items()}
        p.update({k: v[i] for k, v in small.items()})
        h, s = _layer_forward(h, p, i)
        saved.append((p, s))
    loss, dx, _, dgf = _final_loss(h, norm_final.reshape(1, -1), target, "final_loss")
    grads = [None] * DEPTH
    for i in reversed(range(DEPTH)):
        p, s = saved[i]
        dx, grads[i] = _layer_backward(dx, p, s, i)
    gbig = {k: jnp.stack([grads[i][k] for i in range(DEPTH)]) for k in BIG}
    gsmall = {k: jnp.stack([grads[i][k] for i in range(DEPTH)]) for k in LAYER_SMALL}
    return loss[0, 0], dx, gbig, gsmall, dgf[0]


def _my_place():
    return lax.axis_index("x"), lax.axis_index("y"), lax.axis_index("c")


def _all_gather8(blocks, name):
    nt = len(blocks)

    def body(*refs):
        ins = refs[:nt]
        outs = refs[nt:2 * nt]
        send_sems, recv_sems, local_sems = refs[2 * nt:]
        x, y, c = _my_place()
        me, sibling = (x, y, c), (x, y, 1 - c)
        chips = [(1 - x, y), (x, 1 - y), (1 - x, 1 - y)]

        def slot(t, place):
            px, py, pc = place
            return outs[t].at[4 * px + 2 * py + pc]

        def copy(t, k, block, to, src=None):
            return pltpu.make_async_remote_copy(
                src_ref=slot(t, block) if src is None else src, dst_ref=slot(t, block),
                send_sem=send_sems.at[t, k], recv_sem=recv_sems.at[t, k], device_id=to, device_id_type=MESH)

        mine = [pltpu.make_async_copy(ins[t], slot(t, me), local_sems.at[t]) for t in range(nt)]
        for cp in mine:
            cp.start()
        first = []
        for t in range(nt):
            first.append(copy(t, 0, me, sibling, src=ins[t]))
            first += [copy(t, 1 + j, me, (*chip, c), src=ins[t]) for j, chip in enumerate(chips)]
        for cp in first:
            cp.start()
        passed = []
        for j, chip in enumerate(chips):
            for t in range(nt):
                copy(t, 1 + j, (*chip, c), me).wait_recv()
                fwd = copy(t, 4 + j, (*chip, c), sibling)
                fwd.start()
                passed.append(fwd)
        for t in range(nt):
            copy(t, 0, sibling, me).wait_recv()
            for j, chip in enumerate(chips):
                copy(t, 4 + j, (*chip, 1 - c), me).wait_recv()
        for cp in first + passed:
            cp.wait_send()
        for cp in mine:
            cp.wait()

    return pl.pallas_call(
        body, name=name, in_specs=[ANY] * nt, out_specs=[ANY] * nt,
        out_shape=[jax.ShapeDtypeStruct((8,) + b.shape, b.dtype) for b in blocks],
        scratch_shapes=[pltpu.SemaphoreType.DMA((nt, 7)), pltpu.SemaphoreType.DMA((nt, 7)),
                        pltpu.SemaphoreType.DMA((nt,))],
    )(*blocks)


def _sibling_swap(arrs, name):
    nt = len(arrs)

    def body(*refs):
        ins = refs[:nt]
        outs = refs[nt:2 * nt]
        send_sems, recv_sems = refs[2 * nt:]
        x, y, c = _my_place()
        copies = [pltpu.make_async_remote_copy(src_ref=ins[t], dst_ref=outs[t], send_sem=send_sems.at[t],
                                               recv_sem=recv_sems.at[t], device_id=(x, y, 1 - c), device_id_type=MESH)
                  for t in range(nt)]
        for cp in copies:
            cp.start()
        for cp in copies:
            cp.wait()

    return pl.pallas_call(
        body, name=name, in_specs=[ANY] * nt, out_specs=[ANY] * nt,
        out_shape=[jax.ShapeDtypeStruct(a.shape, a.dtype) for a in arrs],
        scratch_shapes=[pltpu.SemaphoreType.DMA((nt,)), pltpu.SemaphoreType.DMA((nt,))],
    )(*arrs)


def _chip_all_to_all(arrs, name):
    nt = len(arrs)

    def body(*refs):
        ins = refs[:nt]
        outs = refs[nt:2 * nt]
        send_sems, recv_sems, local_sems = refs[2 * nt:]
        x, y, c = _my_place()
        mine = 2 * x + y
        chips = [(1 - x, y), (x, 1 - y), (1 - x, 1 - y)]
        local = [pltpu.make_async_copy(ins[t].at[mine], outs[t].at[mine], local_sems.at[t]) for t in range(nt)]
        for cp in local:
            cp.start()
        sends = []
        for t in range(nt):
            for j, (px, py) in enumerate(chips):
                sends.append(pltpu.make_async_remote_copy(
                    src_ref=ins[t].at[2 * px + py], dst_ref=outs[t].at[mine], send_sem=send_sems.at[t, j],
                    recv_sem=recv_sems.at[t, j], device_id=(px, py, c), device_id_type=MESH))
        for cp in sends:
            cp.start()
        for t in range(nt):
            for j, (px, py) in enumerate(chips):
                pltpu.make_async_remote_copy(
                    src_ref=ins[t].at[mine], dst_ref=outs[t].at[2 * px + py], send_sem=send_sems.at[t, j],
                    recv_sem=recv_sems.at[t, j], device_id=(px, py, c), device_id_type=MESH).wait_recv()
        for cp in sends:
            cp.wait_send()
        for cp in local:
            cp.wait()

    return pl.pallas_call(
        body, name=name, in_specs=[ANY] * nt, out_specs=[ANY] * nt,
        out_shape=[jax.ShapeDtypeStruct(a.shape, a.dtype) for a in arrs],
        scratch_shapes=[pltpu.SemaphoreType.DMA((nt, 3)), pltpu.SemaphoreType.DMA((nt, 3)),
                        pltpu.SemaphoreType.DMA((nt,))],
    )(*arrs)


def _as_rows(a):
    return a.reshape(-1, a.shape[-1])


STREAM_VMEM_BYTES = 16 * 1024 * 1024


def _row_tile(rows, cols, n_arrays):
    lanes = -(-cols // LANES) * LANES
    for t in (512, 256, 128, 64, 32, 16, 8):
        if rows % t == 0 and 2 * n_arrays * t * lanes * 4 <= STREAM_VMEM_BYTES:
            return t
    return rows


def _add2(a, b, name):
    rows, cols = a.shape
    tile = _row_tile(rows, cols, 3)

    def body(a_ref, b_ref, o_ref):
        o_ref[...] = a_ref[...] + b_ref[...].astype(F32)

    spec = pl.BlockSpec((tile, cols), lambda i: (i, 0))
    return pl.pallas_call(body, name=name, grid=(rows // tile,), in_specs=[spec, spec], out_specs=spec,
                          out_shape=jax.ShapeDtypeStruct((rows, cols), F32))(a, b)


def _sum_leading(a, name):
    n, rows, cols = a.shape
    tile = _row_tile(rows, cols, n + 1)

    def body(a_ref, o_ref):
        acc = a_ref[0].astype(F32)
        for k in range(1, n):
            acc = acc + a_ref[k].astype(F32)
        o_ref[...] = acc

    return pl.pallas_call(
        body, name=name, grid=(rows // tile,), in_specs=[pl.BlockSpec((n, tile, cols), lambda i: (0, i, 0))],
        out_specs=pl.BlockSpec((tile, cols), lambda i: (i, 0)),
        out_shape=jax.ShapeDtypeStruct((rows, cols), F32))(a)


def _adamw(w, g, m, v, name):
    rows, cols = w.shape
    tile = _row_tile(rows, cols, 7)

    def body(w_ref, g_ref, m_ref, v_ref, d_ref, nm_ref, nv_ref):
        gv = g_ref[...]
        mn = ADAM_B1 * m_ref[...] + (1.0 - ADAM_B1) * gv
        vn = ADAM_B2 * v_ref[...] + (1.0 - ADAM_B2) * jnp.square(gv)
        m_hat = mn / (1.0 - ADAM_B1 ** ADAM_STEP)
        v_hat = vn / (1.0 - ADAM_B2 ** ADAM_STEP)
        d_ref[...] = -ADAM_LR * (m_hat / (jnp.sqrt(v_hat) + ADAM_EPS) + ADAM_WD * w_ref[...])
        nm_ref[...] = mn
        nv_ref[...] = vn

    spec = pl.BlockSpec((tile, cols), lambda i: (i, 0))
    shape = jax.ShapeDtypeStruct((rows, cols), F32)
    return pl.pallas_call(body, name=name, grid=(rows // tile,), in_specs=[spec] * 4, out_specs=[spec] * 3,
                          out_shape=[shape] * 3)(w, g, m, v)


def _flat_pack(arrs):
    flat = jnp.concatenate([a.reshape(-1) for a in arrs])
    pad = (-flat.shape[0]) % (8 * LANES)
    return jnp.pad(flat, (0, pad)).reshape(-1, LANES)


def _flat_unpack(buf, shapes):
    flat = buf.reshape(-1)
    out = []
    off = 0
    for shp in shapes:
        n = math.prod(shp)
        out.append(flat[off:off + n].reshape(shp))
        off += n
    return out


def _to_chunks(a, name):
    rows, cols = a.shape
    if name in COL_SHARDED:
        return a.reshape(rows, 4, cols // 4).transpose(1, 0, 2)
    return a.reshape(4, rows // 4, cols)


def _from_chunks(a, name):
    _, depth, r, cc = a.shape
    if name in COL_SHARDED:
        return a.transpose(1, 2, 0, 3).reshape(depth, r, 4 * cc)
    return a.transpose(1, 0, 2, 3).reshape(depth, 4 * r, cc)


def kernel(x, norm_mix, w_in, s5_lam_re, s5_lam_im, s5_log_step, s5_b_re, s5_b_im, s5_c_re, s5_c_im, s5_d, s5_w_glu, s5_b_glu, s5_norm, ssd_conv_w, ssd_conv_b, ssd_dt_bias, ssd_a_log, ssd_d, ssd_norm, w_out, norm_ffn, w_gate, w_up, w_down, norm_final, loss_target, m_norm_mix, m_w_in, m_s5_lam_re, m_s5_lam_im, m_s5_log_step, m_s5_b_re, m_s5_b_im, m_s5_c_re, m_s5_c_im, m_s5_d, m_s5_w_glu, m_s5_b_glu, m_s5_norm, m_ssd_conv_w, m_ssd_conv_b, m_ssd_dt_bias, m_ssd_a_log, m_ssd_d, m_ssd_norm, m_w_out, m_norm_ffn, m_w_gate, m_w_up, m_w_down, m_norm_final, v_norm_mix, v_w_in, v_s5_lam_re, v_s5_lam_im, v_s5_log_step, v_s5_b_re, v_s5_b_im, v_s5_c_re, v_s5_c_im, v_s5_d, v_s5_w_glu, v_s5_b_glu, v_s5_norm, v_ssd_conv_w, v_ssd_conv_b, v_ssd_dt_bias, v_ssd_a_log, v_ssd_d, v_ssd_norm, v_w_out, v_norm_ffn, v_w_gate, v_w_up, v_w_down, v_norm_final):
    args = dict(locals())
    w = {k: args[k] for k in WEIGHTS}
    m = {k: args["m_" + k] for k in WEIGHTS}
    v = {k: args["v_" + k] for k in WEIGHTS}
    cx, cy, cc = _my_place()
    chip = 2 * cx + cy

    mine = [lax.dynamic_index_in_dim(w[k], cc, 0, keepdims=False).astype(BF16) for k in BIG]
    gathered = _all_gather8(mine + [w["ssd_conv_w"].reshape(DEPTH * SSD_CONV, -1)], "gather_weights")
    big = {}
    for k, a in zip(BIG, gathered):
        big[k] = _from_chunks(a.reshape((4, DEPTH) + a.shape[1:]), k)
    conv_rows = gathered[-1]
    conv_full = conv_rows.reshape(4, 2, DEPTH, SSD_CONV, -1)[:, 0].transpose(1, 2, 0, 3).reshape(
        DEPTH, SSD_CONV, SSD_CONV_DIM)
    small = {k: w[k] for k in LAYER_SMALL}
    small["ssd_conv_w"] = conv_full

    loss_part, grad_x, gbig, gsmall, g_final = _local_step(x[0], loss_target[0], big, small, w["norm_final"])

    other = [lax.dynamic_index_in_dim(gbig[k], 1 - cc, 0, keepdims=False).astype(BF16) for k in BIG]
    from_sib = _sibling_swap(other, "grad_swap")
    chunks = []
    for k, r in zip(BIG, from_sib):
        own = lax.dynamic_index_in_dim(gbig[k], cc, 0, keepdims=False)
        part = _add2(own, r, "grad_pair_" + k)
        chunks.append(_to_chunks(part, k).astype(BF16))
    landed = _chip_all_to_all(chunks, "grad_scatter")
    reduced = []
    for k, a in zip(BIG, landed):
        s = _sum_leading(a, "grad_sum_" + k)
        reduced.append(s)
    from_sib = _sibling_swap(reduced, "grad_share")
    grads = {}
    for k, a, b in zip(BIG, reduced, from_sib):
        layer0 = jnp.where(cc == 0, a, b)
        layer1 = jnp.where(cc == 0, b, a)
        grads[k] = jnp.stack([layer0, layer1])

    small_names = list(LAYER_SMALL) + ["norm_final"]
    parts = [gsmall[k] for k in LAYER_SMALL] + [g_final, loss_part.reshape(1)]
    shapes = [p.shape for p in parts]
    packed = _flat_pack(parts)
    allparts = _all_gather8([packed], "gather_small")[0]
    total = _sum_leading(allparts, "sum_small")
    unpacked = _flat_unpack(total, shapes)
    loss = unpacked[-1][0]
    for k, a in zip(small_names, unpacked[:-1]):
        grads[k] = a
    width = SSD_CONV_DIM // 4
    grads["ssd_conv_w"] = lax.dynamic_slice_in_dim(grads["ssd_conv_w"], chip * width, width, axis=2)

    delta, new_m, new_v = {}, {}, {}
    for k in BIG:
        shp = w[k].shape
        d, nm, nv = _adamw(_as_rows(w[k]), _as_rows(grads[k]), _as_rows(m[k]), _as_rows(v[k]), "adamw_" + k)
        delta[k], new_m[k], new_v[k] = d.reshape(shp), nm.reshape(shp), nv.reshape(shp)
    sm_shapes = [w[k].shape for k in small_names]
    d, nm, nv = _adamw(_flat_pack([w[k] for k in small_names]), _flat_pack([grads[k] for k in small_names]),
                       _flat_pack([m[k] for k in small_names]), _flat_pack([v[k] for k in small_names]),
                       "adamw_small")
    for k, a, b, c in zip(small_names, _flat_unpack(d, sm_shapes), _flat_unpack(nm, sm_shapes),
                          _flat_unpack(nv, sm_shapes)):
        delta[k], new_m[k], new_v[k] = a, b, c

    return (loss, grad_x[None], *[grads[k] for k in WEIGHTS], *[delta[k] for k in WEIGHTS],
            *[new_m[k] for k in WEIGHTS], *[new_v[k] for k in WEIGHTS])
```

```python
import functools
import math

import jax
import jax.numpy as jnp
from jax import lax
from jax.experimental import pallas as pl
from jax.experimental.pallas import tpu as pltpu

F32 = jnp.float32
BF16 = jnp.bfloat16
MESH = pl.DeviceIdType.MESH
ANY = pl.BlockSpec(memory_space=pl.ANY)

D_MODEL = 1024
DEPTH = 2
S5_GROUPS = 64
S5_GROUP = 16
S5_STATE = 64
S5_COLS = S5_GROUPS * S5_STATE
S5_TILE_GROUPS = 8
S5_TILES = S5_GROUPS // S5_TILE_GROUPS
S5_TILE_IN = S5_TILE_GROUPS * S5_GROUP
S5_TILE_ST = S5_TILE_GROUPS * S5_STATE
SEGS = 8
SSD_HEADS = 16
SSD_HEAD_DIM = 64
SSD_GROUPS = 2
SSD_GROUP_HEADS = SSD_HEADS // SSD_GROUPS
SSD_STATE = 128
SSD_CONV = 4
SSD_CHUNK = 128
SSD_WIDTH = 1024
SSD_CONV_DIM = SSD_WIDTH + 2 * SSD_GROUPS * SSD_STATE
IN_PROJ = 3600
IN_MAIN = 3584
IN_PAD = IN_MAIN + 2 * 128
FFN = 2816
EPS = 1e-6
LANES = 128
ROW_TILE = 256

ADAM_LR = 0.001
ADAM_B1 = 0.9
ADAM_B2 = 0.999
ADAM_EPS = 1e-08
ADAM_WD = 0.01
ADAM_STEP = 10

HIGHEST = lax.Precision.HIGHEST


def _sigmoid(x):
    return 1.0 / (1.0 + jnp.exp(-x))


def _silu(x):
    return x * _sigmoid(x)


def _dsilu(x):
    s = _sigmoid(x)
    return s * (1.0 + x * (1.0 - s))


_GELU_K = math.sqrt(2.0 / math.pi)
_GELU_C = 0.044715


def _gelu(x):
    t = jnp.tanh(_GELU_K * (x + _GELU_C * x * x * x))
    return 0.5 * x * (1.0 + t)


def _dgelu(x):
    t = jnp.tanh(_GELU_K * (x + _GELU_C * x * x * x))
    return 0.5 * (1.0 + t) + 0.5 * x * (1.0 - t * t) * _GELU_K * (1.0 + 3.0 * _GELU_C * x * x)


def _softplus(x):
    e = jnp.exp(-jnp.abs(x))
    u = 1.0 + e
    log1p = jnp.where(u == 1.0, e, jnp.log(u) * e / jnp.where(u == 1.0, 1.0, u - 1.0))
    return jnp.maximum(x, 0.0) + log1p


def _rstd(x):
    return lax.rsqrt(jnp.mean(x * x, axis=-1, keepdims=True) + EPS)


def _rms_bwd(x, r, gain, dy):
    dyg = dy * gain
    dx = r * dyg - x * (r * r * r) * jnp.mean(x * dyg, axis=-1, keepdims=True)
    dgain = jnp.sum(dy * x * r, axis=0, keepdims=True)
    return dx, dgain


def _dot(a, b):
    return jnp.dot(a, b, preferred_element_type=F32)


def _dot_nt(a, b):
    return lax.dot_general(a, b, (((1,), (1,)), ((), ())), preferred_element_type=F32)


def _dot_tn(a, b):
    return lax.dot_general(a, b, (((0,), (0,)), ((), ())), preferred_element_type=F32)


def _row_spec(tile, cols):
    return pl.BlockSpec((tile, cols), lambda i: (i, 0))


def _full_spec(shape):
    nd = len(shape)
    return pl.BlockSpec(shape, lambda *_: (0,) * nd)


def _const_spec(shape):
    nd = len(shape)
    return pl.BlockSpec(shape, lambda *_: (0,) * nd, pipeline_mode=pl.Buffered(1))


def _layer_spec(shape, layer, block=0):
    return pl.BlockSpec((None,) + tuple(shape), lambda *_: (layer, block, 0), pipeline_mode=pl.Buffered(1))


def _acc_rows(ref, val, first):
    @pl.when(first)
    def _():
        ref[...] = val

    @pl.when(jnp.logical_not(first))
    def _():
        ref[...] += val


def _pick_tile(n, cap):
    best = LANES
    for t in range(LANES, cap + 1, LANES):
        if n % t == 0:
            best = t
    return best


def _mm_tn(a, b, name):
    k, m = a.shape
    _, n = b.shape
    tm = _pick_tile(m, 512)
    tn = _pick_tile(n, 1536)

    def body(a_ref, b_ref, o_ref):
        o_ref[...] = _dot_tn(a_ref[...], b_ref[...]).astype(BF16)

    return pl.pallas_call(
        body, name=name, grid=(n // tn, m // tm),
        in_specs=[pl.BlockSpec((k, tm), lambda j, i: (0, i)), pl.BlockSpec((k, tn), lambda j, i: (0, j))],
        out_specs=pl.BlockSpec((tm, tn), lambda j, i: (i, j)),
        out_shape=jax.ShapeDtypeStruct((m, n), BF16),
    )(a, b)


def _rms_inproj(x, gain, w_pad, layer, name):
    L = x.shape[0]

    def body(x_ref, g_ref, w_ref, u_ref, z_ref, xbc_ref, dt_ref, h_ref):
        xv = x_ref[...]
        h = (xv * _rstd(xv) * g_ref[...]).astype(BF16)
        h_ref[...] = h
        p = _dot(h, w_ref[...])
        u_ref[...] = p[:, :1024]
        z_ref[...] = p[:, 1024:2048]
        xbc_ref[...] = p[:, 2048:IN_MAIN]
        dt_ref[...] = p[:, IN_MAIN:IN_PAD]

    return pl.pallas_call(
        body, name=name, grid=(L // ROW_TILE,),
        in_specs=[_row_spec(ROW_TILE, D_MODEL), _full_spec((1, D_MODEL)), _layer_spec((D_MODEL, IN_PAD), layer)],
        out_specs=[_row_spec(ROW_TILE, 1024), _row_spec(ROW_TILE, 1024), _row_spec(ROW_TILE, SSD_CONV_DIM),
                   _row_spec(ROW_TILE, 256), _row_spec(ROW_TILE, D_MODEL)],
        out_shape=[jax.ShapeDtypeStruct((L, 1024), F32), jax.ShapeDtypeStruct((L, 1024), F32),
                   jax.ShapeDtypeStruct((L, SSD_CONV_DIM), F32), jax.ShapeDtypeStruct((L, 256), F32),
                   jax.ShapeDtypeStruct((L, D_MODEL), BF16)],
    )(x, gain, w_pad)


def _s5_prep_math(lr, li, ls, bre, bim):
    step = jnp.exp(ls)
    mag = jnp.exp(lr * step)
    ang = li * step
    are = mag * jnp.cos(ang)
    aim = mag * jnp.sin(ang)
    den = lr * lr + li * li
    nr = are - 1.0
    ni = aim
    cre = (nr * lr + ni * li) / den
    cim = (ni * lr - nr * li) / den
    bbre = cre[None] * bre - cim[None] * bim
    bbim = cre[None] * bim + cim[None] * bre
    return are, aim, bbre, bbim


def _s5_prep(lr, li, ls, bre, bim, name):
    def body(lr_ref, li_ref, ls_ref, bre_ref, bim_ref, are_ref, aim_ref, bbre_ref, bbim_ref):
        are, aim, bbre, bbim = _s5_prep_math(lr_ref[...], li_ref[...], ls_ref[...], bre_ref[...], bim_ref[...])
        are_ref[...] = are
        aim_ref[...] = aim
        bbre_ref[...] = bbre
        bbim_ref[...] = bbim

    gp = jax.ShapeDtypeStruct((S5_GROUPS, S5_STATE), F32)
    hgp = jax.ShapeDtypeStruct((S5_GROUP, S5_GROUPS, S5_STATE), F32)
    return pl.pallas_call(body, name=name, out_shape=[gp, gp, hgp, hgp])(lr, li, ls, bre, bim)


def _s5_prep_bwd(lr, li, ls, bre, bim, dare, daim, dbbre, dbbim, name):
    def body(lr_ref, li_ref, ls_ref, bre_ref, bim_ref, dare_ref, daim_ref, dbbre_ref, dbbim_ref,
             dlr_ref, dli_ref, dls_ref, dbre_ref, dbim_ref):
        _, vjp = jax.vjp(_s5_prep_math, lr_ref[...], li_ref[...], ls_ref[...], bre_ref[...], bim_ref[...])
        dlr, dli, dls, dbre, dbim = vjp((dare_ref[...], daim_ref[...], dbbre_ref[...], dbbim_ref[...]))
        dlr_ref[...] = dlr
        dli_ref[...] = dli
        dls_ref[...] = dls
        dbre_ref[...] = dbre
        dbim_ref[...] = dbim

    gp = jax.ShapeDtypeStruct((S5_GROUPS, S5_STATE), F32)
    g1 = jax.ShapeDtypeStruct((S5_GROUPS, 1), F32)
    hgp = jax.ShapeDtypeStruct((S5_GROUP, S5_GROUPS, S5_STATE), F32)
    return pl.pallas_call(body, name=name, out_shape=[gp, gp, g1, hgp, hgp])(
        lr, li, ls, bre, bim, dare, daim, dbbre, dbbim)


def _cmul_add(ar, ai, sr, si, br, bi):
    return ar * sr - ai * si + br, ar * si + ai * sr + bi


def _shift_rows_down(v):
    rolled = pltpu.roll(v, 1, 0)
    row = lax.broadcasted_iota(jnp.int32, v.shape, 0)
    return jnp.where(row == 0, 0.0, rolled)


def _shift_rows_up(v):
    rolled = pltpu.roll(v, SEGS - 1, 0)
    row = lax.broadcasted_iota(jnp.int32, v.shape, 0)
    return jnp.where(row == SEGS - 1, 0.0, rolled)


def _segment_power(ar, ai, steps):
    n = 1
    while n < steps:
        ar, ai = ar * ar - ai * ai, 2.0 * ar * ai
        n *= 2
    assert n == steps
    return ar, ai


def _segment_entries(ar, ai, fr, fi, steps, shift):
    pr, pi = _segment_power(ar, ai, steps)
    er = jnp.zeros_like(fr)
    ei = jnp.zeros_like(fi)
    for _ in range(SEGS - 1):
        nr, ni = _cmul_add(pr, pi, er, ei, fr, fi)
        er, ei = shift(nr), shift(ni)
    return er, ei


def _s5_scan(u_perm, bre_bd, bim_bd, cre_bd, cim_bd, are, aim, name):
    L = u_perm.shape[0]
    steps = L // SEGS

    def body(u_ref, bre_ref, bim_ref, cre_ref, cim_ref, are_ref, aim_ref, y_ref, xr_ref, xi_ref):
        u = u_ref[...].astype(BF16)
        xr_ref[...] = _dot(u, bre_ref[0])
        xi_ref[...] = _dot(u, bim_ref[0])
        ar = jnp.broadcast_to(are_ref[0], (SEGS, S5_TILE_ST))
        ai = jnp.broadcast_to(aim_ref[0], (SEGS, S5_TILE_ST))
        zero = jnp.zeros((SEGS, S5_TILE_ST), F32)

        def finals(j, c):
            rows = pl.ds(pl.multiple_of(j * SEGS, SEGS), SEGS)
            return _cmul_add(ar, ai, c[0], c[1], xr_ref[rows, :], xi_ref[rows, :])

        fr, fi = lax.fori_loop(0, steps, finals, (zero, zero), unroll=4)
        er, ei = _segment_entries(ar, ai, fr, fi, steps, _shift_rows_down)

        def scan(j, c):
            rows = pl.ds(pl.multiple_of(j * SEGS, SEGS), SEGS)
            sr, si = _cmul_add(ar, ai, c[0], c[1], xr_ref[rows, :], xi_ref[rows, :])
            xr_ref[rows, :] = sr
            xi_ref[rows, :] = si
            return sr, si

        lax.fori_loop(0, steps, scan, (er, ei), unroll=4)
        y_ref[...] = (_dot(xr_ref[...].astype(BF16), cre_ref[0]) - _dot(xi_ref[...].astype(BF16), cim_ref[0]))

    tile3 = lambda a, b: pl.BlockSpec((1, a, b), lambda k: (k, 0, 0))
    return pl.pallas_call(
        body, name=name, grid=(S5_TILES,),
        in_specs=[pl.BlockSpec((L, S5_TILE_IN), lambda k: (0, k)),
                  tile3(S5_TILE_IN, S5_TILE_ST), tile3(S5_TILE_IN, S5_TILE_ST),
                  tile3(S5_TILE_ST, S5_TILE_IN), tile3(S5_TILE_ST, S5_TILE_IN),
                  tile3(1, S5_TILE_ST), tile3(1, S5_TILE_ST)],
        out_specs=[pl.BlockSpec((L, S5_TILE_IN), lambda k: (0, k)),
                   pl.BlockSpec((L, S5_TILE_ST), lambda k: (0, k)), pl.BlockSpec((L, S5_TILE_ST), lambda k: (0, k))],
        out_shape=[jax.ShapeDtypeStruct((L, 1024), F32), jax.ShapeDtypeStruct((L, S5_COLS), F32),
                   jax.ShapeDtypeStruct((L, S5_COLS), F32)],
    )(u_perm, bre_bd, bim_bd, cre_bd, cim_bd, are, aim)


def _s5_scan_bwd(dy_perm, u_perm, xr, xi, bret_bd, bimt_bd, cret_bd, cimt_bd, are, aim, name):
    L = u_perm.shape[0]
    steps = L // SEGS

    def body(dy_ref, u_ref, xr_ref, xi_ref, bret_ref, bimt_ref, cret_ref, cimt_ref, are_ref, aim_ref,
             du_ref, dar_ref, dai_ref, dcre_ref, dcim_ref, dbre_ref, dbim_ref, gr_ref, gi_ref):
        dy = dy_ref[...].astype(BF16)
        u = u_ref[...].astype(BF16)
        gr_ref[...] = _dot(dy, cret_ref[0])
        gi_ref[...] = -_dot(dy, cimt_ref[0])
        ar = jnp.broadcast_to(are_ref[0], (SEGS, S5_TILE_ST))
        ai = -jnp.broadcast_to(aim_ref[0], (SEGS, S5_TILE_ST))
        zero = jnp.zeros((SEGS, S5_TILE_ST), F32)

        def finals(k, c):
            rows = pl.ds(pl.multiple_of((steps - 1 - k) * SEGS, SEGS), SEGS)
            return _cmul_add(ar, ai, c[0], c[1], gr_ref[rows, :], gi_ref[rows, :])

        fr, fi = lax.fori_loop(0, steps, finals, (zero, zero), unroll=4)
        er, ei = _segment_entries(ar, ai, fr, fi, steps, _shift_rows_up)

        def scan(k, c):
            sr0, si0, accr, acci = c
            j = steps - 1 - k
            rows = pl.ds(pl.multiple_of(j * SEGS, SEGS), SEGS)
            sr, si = _cmul_add(ar, ai, sr0, si0, gr_ref[rows, :], gi_ref[rows, :])
            gr_ref[rows, :] = sr
            gi_ref[rows, :] = si
            prev = pl.ds(pl.multiple_of(jnp.maximum(j - 1, 0) * SEGS, SEGS), SEGS)
            live = (j > 0).astype(F32)
            xpr = xr_ref[prev, :] * live
            xpi = xi_ref[prev, :] * live
            return sr, si, accr + sr * xpr + si * xpi, acci + si * xpr - sr * xpi

        _, _, accr, acci = lax.fori_loop(0, steps, scan, (er, ei, zero, zero), unroll=4)
        first = pl.ds(0, SEGS)
        last = pl.ds((steps - 1) * SEGS, SEGS)
        xpr = _shift_rows_down(xr_ref[last, :])
        xpi = _shift_rows_down(xi_ref[last, :])
        g0r = gr_ref[first, :]
        g0i = gi_ref[first, :]
        accr = accr + g0r * xpr + g0i * xpi
        acci = acci + g0i * xpr - g0r * xpi
        dar_ref[0] = jnp.sum(accr, axis=0, keepdims=True)
        dai_ref[0] = jnp.sum(acci, axis=0, keepdims=True)

        grb = gr_ref[...].astype(BF16)
        gib = gi_ref[...].astype(BF16)
        du_ref[...] = _dot(grb, bret_ref[0]) + _dot(gib, bimt_ref[0])
        dbre_ref[0] = _dot_tn(u, grb)
        dbim_ref[0] = _dot_tn(u, gib)
        dcre_ref[0] = _dot_tn(dy, xr_ref[...].astype(BF16))
        dcim_ref[0] = -_dot_tn(dy, xi_ref[...].astype(BF16))

    tile3 = lambda a, b: pl.BlockSpec((1, a, b), lambda k: (k, 0, 0))
    col_in = pl.BlockSpec((L, S5_TILE_IN), lambda k: (0, k))
    col_st = pl.BlockSpec((L, S5_TILE_ST), lambda k: (0, k))
    dense = jax.ShapeDtypeStruct((S5_TILES, S5_TILE_IN, S5_TILE_ST), F32)
    vec = jax.ShapeDtypeStruct((S5_TILES, 1, S5_TILE_ST), F32)
    return pl.pallas_call(
        body, name=name, grid=(S5_TILES,),
        in_specs=[col_in, col_in, col_st, col_st,
                  tile3(S5_TILE_ST, S5_TILE_IN), tile3(S5_TILE_ST, S5_TILE_IN),
                  tile3(S5_TILE_IN, S5_TILE_ST), tile3(S5_TILE_IN, S5_TILE_ST),
                  tile3(1, S5_TILE_ST), tile3(1, S5_TILE_ST)],
        out_specs=[col_in, tile3(1, S5_TILE_ST), tile3(1, S5_TILE_ST),
                   tile3(S5_TILE_IN, S5_TILE_ST), tile3(S5_TILE_IN, S5_TILE_ST),
                   tile3(S5_TILE_IN, S5_TILE_ST), tile3(S5_TILE_IN, S5_TILE_ST)],
        out_shape=[jax.ShapeDtypeStruct((L, 1024), F32), vec, vec, dense, dense, dense, dense],
        scratch_shapes=[pltpu.VMEM((L, S5_TILE_ST), F32), pltpu.VMEM((L, S5_TILE_ST), F32)],
    )(dy_perm, u_perm, xr, xi, bret_bd, bimt_bd, cret_bd, cimt_bd, are, aim)


def _s5_post(ys, u, d_skip, w_glu, b_glu, gain, layer, name):
    L = ys.shape[0]

    def body(ys_ref, u_ref, d_ref, w_ref, b_ref, g_ref, ya_ref):
        g = _gelu(ys_ref[...] + d_ref[...] * u_ref[...])
        q = _dot(g.astype(BF16), w_ref[...]) + b_ref[...]
        oa = g * _sigmoid(q)
        ya_ref[...] = (oa * _rstd(oa) * g_ref[...]).astype(BF16)

    vec = _full_spec((1, 1024))
    return pl.pallas_call(
        body, name=name, grid=(L // ROW_TILE,),
        in_specs=[_row_spec(ROW_TILE, 1024), _row_spec(ROW_TILE, 1024), vec, _layer_spec((1024, 1024), layer), vec,
                  vec],
        out_specs=_row_spec(ROW_TILE, 1024),
        out_shape=jax.ShapeDtypeStruct((L, 1024), BF16),
    )(ys, u, d_skip, w_glu, b_glu, gain)


def _s5_post_bwd(dx, w_out, ys, u, d_skip, w_glu, b_glu, gain, layer, name):
    L = ys.shape[0]

    def body(dx_ref, wo_ref, ys_ref, u_ref, d_ref, w_ref, b_ref, gn_ref,
             dys_ref, dus_ref, g_ref, dq_ref, dgain_ref, dd_ref, db_ref):
        first = pl.program_id(0) == 0
        uv = u_ref[...]
        yt = ys_ref[...] + d_ref[...] * uv
        g = _gelu(yt)
        gb = g.astype(BF16)
        q = _dot(gb, w_ref[...]) + b_ref[...]
        s = _sigmoid(q)
        oa = g * s
        dya = _dot_nt(dx_ref[...], wo_ref[...])
        doa, dgain = _rms_bwd(oa, _rstd(oa), gn_ref[...], dya)
        dq = doa * g * s * (1.0 - s)
        dqb = dq.astype(BF16)
        dg = doa * s + _dot_nt(dqb, w_ref[...])
        dyt = dg * _dgelu(yt)
        dys_ref[...] = dyt
        dus_ref[...] = dyt * d_ref[...]
        g_ref[...] = gb
        dq_ref[...] = dqb
        _acc_rows(dgain_ref, dgain, first)
        _acc_rows(dd_ref, jnp.sum(dyt * uv, axis=0, keepdims=True), first)
        _acc_rows(db_ref, jnp.sum(dq, axis=0, keepdims=True), first)

    vec = _full_spec((1, 1024))
    row = _row_spec(ROW_TILE, 1024)
    vshape = jax.ShapeDtypeStruct((1, 1024), F32)
    return pl.pallas_call(
        body, name=name, grid=(L // ROW_TILE,),
        in_specs=[row, _layer_spec((1024, 1024), layer, 0), row, row, vec, _layer_spec((1024, 1024), layer), vec,
                  vec],
        out_specs=[row, row, row, row, vec, vec, vec],
        out_shape=[jax.ShapeDtypeStruct((L, 1024), F32), jax.ShapeDtypeStruct((L, 1024), F32),
                   jax.ShapeDtypeStruct((L, 1024), BF16), jax.ShapeDtypeStruct((L, 1024), BF16),
                   vshape, vshape, vshape],
    )(dx, w_out, ys, u, d_skip, w_glu, b_glu, gain)


CONV_TILE = 256


def _shift_time(v, d):
    if d == 0:
        return v
    rolled = pltpu.roll(v, d, 0)
    row = lax.broadcasted_iota(jnp.int32, v.shape, 0)
    return jnp.where(row < d, 0.0, rolled)


def _unshift_time(v, d):
    if d == 0:
        return v
    n = v.shape[0]
    rolled = pltpu.roll(v, n - d, 0)
    row = lax.broadcasted_iota(jnp.int32, v.shape, 0)
    return jnp.where(row >= n - d, 0.0, rolled)


def _ssd_conv(xbc, w, b, name):
    L = xbc.shape[0]

    def body(x_ref, w_ref, b_ref, o_ref):
        xv = x_ref[...]
        pre = jnp.broadcast_to(b_ref[...], xv.shape)
        for k in range(SSD_CONV):
            pre = pre + w_ref[k:k + 1, :] * _shift_time(xv, SSD_CONV - 1 - k)
        o_ref[...] = _silu(pre)

    col = pl.BlockSpec((L, CONV_TILE), lambda j: (0, j))
    return pl.pallas_call(
        body, name=name, grid=(SSD_CONV_DIM // CONV_TILE,),
        in_specs=[col, pl.BlockSpec((8, CONV_TILE), lambda j: (0, j)), pl.BlockSpec((1, CONV_TILE), lambda j: (0, j))],
        out_specs=col, out_shape=jax.ShapeDtypeStruct((L, SSD_CONV_DIM), F32),
    )(xbc, w, b)


def _ssd_conv_bwd(dxc, xbc, w, b, name):
    L = xbc.shape[0]

    def body(d_ref, x_ref, w_ref, b_ref, dx_ref, dw_ref, db_ref):
        xv = x_ref[...]
        shifted = [_shift_time(xv, SSD_CONV - 1 - k) for k in range(SSD_CONV)]
        pre = jnp.broadcast_to(b_ref[...], xv.shape)
        for k in range(SSD_CONV):
            pre = pre + w_ref[k:k + 1, :] * shifted[k]
        dpre = d_ref[...] * _dsilu(pre)
        dx = jnp.zeros_like(xv)
        rows = []
        for k in range(SSD_CONV):
            dx = dx + w_ref[k:k + 1, :] * _unshift_time(dpre, SSD_CONV - 1 - k)
            rows.append(jnp.sum(dpre * shifted[k], axis=0, keepdims=True))
        dx_ref[...] = dx
        dw_ref[...] = jnp.concatenate(rows + [jnp.zeros((8 - SSD_CONV, CONV_TILE), F32)], axis=0)
        db_ref[...] = jnp.sum(dpre, axis=0, keepdims=True)

    col = pl.BlockSpec((L, CONV_TILE), lambda j: (0, j))
    w_spec = pl.BlockSpec((8, CONV_TILE), lambda j: (0, j))
    b_spec = pl.BlockSpec((1, CONV_TILE), lambda j: (0, j))
    return pl.pallas_call(
        body, name=name, grid=(SSD_CONV_DIM // CONV_TILE,),
        in_specs=[col, col, w_spec, b_spec], out_specs=[col, w_spec, b_spec],
        out_shape=[jax.ShapeDtypeStruct((L, SSD_CONV_DIM), F32), jax.ShapeDtypeStruct((8, SSD_CONV_DIM), F32),
                   jax.ShapeDtypeStruct((1, SSD_CONV_DIM), F32)],
    )(dxc, xbc, w, b)


def _tri(lower):
    r = lax.broadcasted_iota(jnp.int32, (SSD_CHUNK, SSD_CHUNK), 0)
    c = lax.broadcasted_iota(jnp.int32, (SSD_CHUNK, SSD_CHUNK), 1)
    return (r >= c) if lower else (r <= c)


def _ssd_chunk_common(dt_ref, bias_ref, alog_ref):
    pre = dt_ref[...] + bias_ref[0]
    dtp = _softplus(pre)
    a_neg = -jnp.exp(alog_ref[0])
    dta = dtp * a_neg
    ltri = _tri(True).astype(F32)
    acum = jnp.dot(ltri, dta, precision=HIGHEST, preferred_element_type=F32)
    return pre, dtp, a_neg, dta, acum


def _ssd_scan(xc, dt, dt_bias, a_log, d_skip, name):
    L = xc.shape[0]
    nc = L // SSD_CHUNK
    gw = SSD_GROUP_HEADS * SSD_HEAD_DIM

    def body(x_ref, b_ref, c_ref, dt_ref, bias_ref, alog_ref, d_ref, y_ref, sp_ref, s_ref):
        @pl.when(pl.program_id(1) == 0)
        def _():
            s_ref[...] = jnp.zeros_like(s_ref)

        _, dtp_all, _, _, acum_all = _ssd_chunk_common(dt_ref, bias_ref, alog_ref)
        acum_t = acum_all.T
        bm = b_ref[...].astype(BF16)
        cm = c_ref[...].astype(BF16)
        cb = _dot_nt(cm, bm)
        lower = _tri(True)
        for h in range(SSD_GROUP_HEADS):
            acum = acum_all[:, h:h + 1]
            dtp = dtp_all[:, h:h + 1]
            lm = jnp.where(lower, jnp.exp(jnp.minimum(acum - acum_t[h:h + 1, :], 0.0)), 0.0)
            xh = x_ref[:, h * SSD_HEAD_DIM:(h + 1) * SSD_HEAD_DIM]
            xdt = xh * dtp
            sp = s_ref[h]
            y = _dot((cb * lm).astype(BF16), xdt.astype(BF16))
            y = y + _dot_nt(cm, sp.astype(BF16)) * jnp.exp(acum)
            y = y + d_ref[0, :, h:h + 1] * xh
            y_ref[:, h * SSD_HEAD_DIM:(h + 1) * SSD_HEAD_DIM] = y
            alast = acum[SSD_CHUNK - 1:SSD_CHUNK, :]
            wgt = xdt * jnp.exp(alast - acum)
            sp_ref[0, 0, h] = sp
            s_ref[h] = jnp.exp(alast) * sp + _dot_tn(wgt.astype(BF16), bm)

    par = lambda: pl.BlockSpec((1, 1, LANES), lambda g, c: (g, 0, 0))
    return pl.pallas_call(
        body, name=name, grid=(SSD_GROUPS, nc),
        in_specs=[pl.BlockSpec((SSD_CHUNK, gw), lambda g, c: (c, g)),
                  pl.BlockSpec((SSD_CHUNK, SSD_STATE), lambda g, c: (c, 8 + g)),
                  pl.BlockSpec((SSD_CHUNK, SSD_STATE), lambda g, c: (c, 10 + g)),
                  pl.BlockSpec((SSD_CHUNK, LANES), lambda g, c: (c, g)),
                  par(), par(), par()],
        out_specs=[pl.BlockSpec((SSD_CHUNK, gw), lambda g, c: (c, g)),
                   pl.BlockSpec((1, 1, SSD_GROUP_HEADS, SSD_HEAD_DIM, SSD_STATE), lambda g, c: (c, g, 0, 0, 0))],
        out_shape=[jax.ShapeDtypeStruct((L, SSD_WIDTH), F32),
                   jax.ShapeDtypeStruct((nc, SSD_GROUPS, SSD_GROUP_HEADS, SSD_HEAD_DIM, SSD_STATE), F32)],
        scratch_shapes=[pltpu.VMEM((SSD_GROUP_HEADS, SSD_HEAD_DIM, SSD_STATE), F32)],
    )(xc, xc, xc, dt, dt_bias, a_log, d_skip)


def _ssd_scan_bwd(dy, xc, dt, sprev, dt_bias, a_log, d_skip, name):
    L = xc.shape[0]
    nc = L // SSD_CHUNK
    gw = SSD_GROUP_HEADS * SSD_HEAD_DIM

    def body(dy_ref, x_ref, b_ref, c_ref, dt_ref, sp_ref, bias_ref, alog_ref, d_ref,
             dx_ref, db_ref, dc_ref, ddt_ref, dbias_ref, dalog_ref, dd_ref, ds_ref):
        first = pl.program_id(1) == 0

        @pl.when(first)
        def _():
            ds_ref[...] = jnp.zeros_like(ds_ref)

        pre, dtp_all, a_neg, _, acum_all = _ssd_chunk_common(dt_ref, bias_ref, alog_ref)
        acum_t = acum_all.T
        bm = b_ref[...].astype(BF16)
        cm = c_ref[...].astype(BF16)
        cb = _dot_nt(cm, bm)
        lower = _tri(True)
        lane = lax.broadcasted_iota(jnp.int32, (SSD_CHUNK, LANES), 1)
        lane1 = lax.broadcasted_iota(jnp.int32, (1, LANES), 1)
        last_row = lax.broadcasted_iota(jnp.int32, (SSD_CHUNK, 1), 0) == SSD_CHUNK - 1
        ones_col = jnp.ones((SSD_CHUNK, 1), F32)
        dacum_all = jnp.zeros((SSD_CHUNK, LANES), F32)
        ddtp_all = jnp.zeros((SSD_CHUNK, LANES), F32)
        dd_row = jnp.zeros((1, LANES), F32)
        dcb = jnp.zeros((SSD_CHUNK, SSD_CHUNK), F32)
        db_acc = jnp.zeros((SSD_CHUNK, SSD_STATE), F32)
        dc_acc = jnp.zeros((SSD_CHUNK, SSD_STATE), F32)
        for h in range(SSD_GROUP_HEADS):
            cols = slice(h * SSD_HEAD_DIM, (h + 1) * SSD_HEAD_DIM)
            acum = acum_all[:, h:h + 1]
            dtp = dtp_all[:, h:h + 1]
            lm = jnp.where(lower, jnp.exp(jnp.minimum(acum - acum_t[h:h + 1, :], 0.0)), 0.0)
            xh = x_ref[:, cols]
            dyh = dy_ref[:, cols]
            dyb = dyh.astype(BF16)
            xdt = xh * dtp
            xdtb = xdt.astype(BF16)
            sp = sp_ref[0, 0, h]
            spb = sp.astype(BF16)
            dsn = ds_ref[h]
            dsb = dsn.astype(BF16)
            eacum = jnp.exp(acum)
            alast = acum[SSD_CHUNK - 1:SSD_CHUNK, :]
            ealast = jnp.exp(alast)
            dstate = jnp.exp(alast - acum)
            mb = (cb * lm).astype(BF16)
            dm = _dot_nt(dyb, xdtb)
            dxdt = _dot_tn(mb, dyb)
            dlm_lm = dm * cb * lm
            dcb = dcb + dm * lm
            dacum = jnp.sum(dlm_lm, axis=1, keepdims=True) - lax.dot_general(
                dlm_lm, ones_col, (((0,), (0,)), ((), ())), precision=HIGHEST, preferred_element_type=F32)
            z = _dot_nt(cm, spb)
            dz = dyh * eacum
            dzb = dz.astype(BF16)
            dacum = dacum + jnp.sum(dz * z, axis=1, keepdims=True)
            dc_acc = dc_acc + _dot(dzb, spb)
            dsp = _dot_tn(dzb, cm) + ealast * dsn
            dw = _dot_nt(bm, dsb)
            wgt = xdt * dstate
            db_acc = db_acc + _dot(wgt.astype(BF16), dsb)
            dxdt = dxdt + dw * dstate
            dds_ds = jnp.sum(dw * xdt, axis=1, keepdims=True) * dstate
            dalast = jnp.sum(dsn * sp, keepdims=True).reshape(1, 1) * ealast + jnp.sum(dds_ds, axis=0, keepdims=True)
            dacum = dacum - dds_ds + jnp.where(last_row, dalast, 0.0)
            ds_ref[h] = dsp
            dx_ref[:, cols] = d_ref[0, :, h:h + 1] * dyh + dxdt * dtp
            dd_row = dd_row + jnp.where(lane1 == h, jnp.sum(dyh * xh, keepdims=True).reshape(1, 1), 0.0)
            ddtp_all = jnp.where(lane == h, jnp.sum(dxdt * xh, axis=1, keepdims=True), ddtp_all)
            dacum_all = jnp.where(lane == h, dacum, dacum_all)
        dcbb = dcb.astype(BF16)
        dc_ref[...] = dc_acc + _dot(dcbb, bm)
        db_ref[...] = db_acc + _dot_tn(dcbb, cm)
        utri = _tri(False).astype(F32)
        ddta = jnp.dot(utri, dacum_all, precision=HIGHEST, preferred_element_type=F32)
        ddtp_all = ddtp_all + ddta * a_neg
        ddt = ddtp_all * _sigmoid(pre)
        ddt_ref[...] = ddt
        _acc_rows(dbias_ref, jnp.sum(ddt, axis=0, keepdims=True)[None], first)
        _acc_rows(dalog_ref, (jnp.sum(ddta * dtp_all, axis=0, keepdims=True) * a_neg)[None], first)
        _acc_rows(dd_ref, dd_row[None], first)

    rev = lambda c: nc - 1 - c
    par = lambda: pl.BlockSpec((1, 1, LANES), lambda g, c: (g, 0, 0))
    pshape = jax.ShapeDtypeStruct((SSD_GROUPS, 1, LANES), F32)
    return pl.pallas_call(
        body, name=name, grid=(SSD_GROUPS, nc),
        in_specs=[pl.BlockSpec((SSD_CHUNK, gw), lambda g, c: (rev(c), g)),
                  pl.BlockSpec((SSD_CHUNK, gw), lambda g, c: (rev(c), g)),
                  pl.BlockSpec((SSD_CHUNK, SSD_STATE), lambda g, c: (rev(c), 8 + g)),
                  pl.BlockSpec((SSD_CHUNK, SSD_STATE), lambda g, c: (rev(c), 10 + g)),
                  pl.BlockSpec((SSD_CHUNK, LANES), lambda g, c: (rev(c), g)),
                  pl.BlockSpec((1, 1, SSD_GROUP_HEADS, SSD_HEAD_DIM, SSD_STATE), lambda g, c: (rev(c), g, 0, 0, 0)),
                  par(), par(), par()],
        out_specs=[pl.BlockSpec((SSD_CHUNK, gw), lambda g, c: (rev(c), g)),
                   pl.BlockSpec((SSD_CHUNK, SSD_STATE), lambda g, c: (rev(c), g)),
                   pl.BlockSpec((SSD_CHUNK, SSD_STATE), lambda g, c: (rev(c), g)),
                   pl.BlockSpec((SSD_CHUNK, LANES), lambda g, c: (rev(c), g)),
                   par(), par(), par()],
        out_shape=[jax.ShapeDtypeStruct((L, SSD_WIDTH), F32), jax.ShapeDtypeStruct((L, 256), F32),
                   jax.ShapeDtypeStruct((L, 256), F32), jax.ShapeDtypeStruct((L, 256), F32),
                   pshape, pshape, pshape],
        scratch_shapes=[pltpu.VMEM((SSD_GROUP_HEADS, SSD_HEAD_DIM, SSD_STATE), F32)],
    )(dy, xc, xc, xc, dt, sprev, dt_bias, a_log, d_skip)


def _ssd_post(y, z, gain, name):
    L = y.shape[0]

    def body(y_ref, z_ref, g_ref, o_ref):
        ob = y_ref[...] * _silu(z_ref[...])
        o_ref[...] = (ob * _rstd(ob) * g_ref[...]).astype(BF16)

    row = _row_spec(ROW_TILE, 1024)
    return pl.pallas_call(body, name=name, grid=(L // ROW_TILE,), in_specs=[row, row, _full_spec((1, 1024))],
                          out_specs=row, out_shape=jax.ShapeDtypeStruct((L, 1024), BF16))(y, z, gain)


def _ssd_post_bwd(dx, w_out, y, z, gain, layer, name):
    L = y.shape[0]

    def body(dx_ref, wo_ref, y_ref, z_ref, g_ref, dy_ref, dz_ref, dgain_ref):
        first = pl.program_id(0) == 0
        yv = y_ref[...]
        zv = z_ref[...]
        sz = _silu(zv)
        ob = yv * sz
        dyb = _dot_nt(dx_ref[...], wo_ref[...])
        dob, dgain = _rms_bwd(ob, _rstd(ob), g_ref[...], dyb)
        dy_ref[...] = dob * sz
        dz_ref[...] = dob * yv * _dsilu(zv)
        _acc_rows(dgain_ref, dgain, first)

    row = _row_spec(ROW_TILE, 1024)
    vec = _full_spec((1, 1024))
    return pl.pallas_call(
        body, name=name, grid=(L // ROW_TILE,),
        in_specs=[row, _layer_spec((1024, 1024), layer, 1), row, row, vec],
        out_specs=[row, row, vec],
        out_shape=[jax.ShapeDtypeStruct((L, 1024), F32), jax.ShapeDtypeStruct((L, 1024), F32),
                   jax.ShapeDtypeStruct((1, 1024), F32)],
    )(dx, w_out, y, z, gain)


def _out_proj(x, ya, yb, w_out, layer, name):
    L = x.shape[0]

    def body(x_ref, ya_ref, yb_ref, w_ref, o_ref):
        o_ref[...] = x_ref[...] + _dot(ya_ref[...], w_ref[:1024, :]) + _dot(yb_ref[...], w_ref[1024:, :])

    row = _row_spec(ROW_TILE, 1024)
    return pl.pallas_call(body, name=name, grid=(L // ROW_TILE,),
                          in_specs=[row, row, row, _layer_spec((2048, 1024), layer)],
                          out_specs=row, out_shape=jax.ShapeDtypeStruct((L, D_MODEL), F32))(x, ya, yb, w_out)


def _ffn(x, gain, w_gate, w_up, w_down, layer, name):
    L = x.shape[0]

    def body(x_ref, g_ref, wg_ref, wu_ref, wd_ref, o_ref, gt_ref, up_ref):
        xv = x_ref[...]
        h = (xv * _rstd(xv) * g_ref[...]).astype(BF16)
        gt = _dot(h, wg_ref[...])
        up = _dot(h, wu_ref[...])
        gt_ref[...] = gt
        up_ref[...] = up
        o_ref[...] = xv + _dot((_silu(gt) * up).astype(BF16), wd_ref[...])

    row = _row_spec(ROW_TILE, D_MODEL)
    hid = _row_spec(ROW_TILE, FFN)
    return pl.pallas_call(
        body, name=name, grid=(L // ROW_TILE,),
        in_specs=[row, _full_spec((1, D_MODEL)), _layer_spec((D_MODEL, FFN), layer),
                  _layer_spec((D_MODEL, FFN), layer), _layer_spec((FFN, D_MODEL), layer)],
        out_specs=[row, hid, hid],
        out_shape=[jax.ShapeDtypeStruct((L, D_MODEL), F32), jax.ShapeDtypeStruct((L, FFN), F32),
                   jax.ShapeDtypeStruct((L, FFN), F32)],
    )(x, gain, w_gate, w_up, w_down)


def _ffn_bwd(dx2, x1, gt, up, gain, w_gate, w_up, w_down, layer, name):
    L = x1.shape[0]

    def body(d_ref, x_ref, gt_ref, up_ref, g_ref, wg_ref, wu_ref, wd_ref,
             dx_ref, dxb_ref, h_ref, act_ref, dgt_ref, dup_ref, dgain_ref):
        first = pl.program_id(0) == 0
        dv = d_ref[...]
        xv = x_ref[...]
        r = _rstd(xv)
        h_ref[...] = (xv * r * g_ref[...]).astype(BF16)
        gtv = gt_ref[...]
        upv = up_ref[...]
        sg = _silu(gtv)
        act_ref[...] = (sg * upv).astype(BF16)
        dact = _dot_nt(dv.astype(BF16), wd_ref[...])
        dgt = (dact * upv * _dsilu(gtv)).astype(BF16)
        dup = (dact * sg).astype(BF16)
        dgt_ref[...] = dgt
        dup_ref[...] = dup
        dh = _dot_nt(dgt, wg_ref[...]) + _dot_nt(dup, wu_ref[...])
        dxn, dgain = _rms_bwd(xv, r, g_ref[...], dh)
        dx = dv + dxn
        dx_ref[...] = dx
        dxb_ref[...] = dx.astype(BF16)
        _acc_rows(dgain_ref, dgain, first)

    row = _row_spec(ROW_TILE, D_MODEL)
    hid = _row_spec(ROW_TILE, FFN)
    vec = _full_spec((1, D_MODEL))
    return pl.pallas_call(
        body, name=name, grid=(L // ROW_TILE,),
        in_specs=[row, row, hid, hid, vec, _layer_spec((D_MODEL, FFN), layer), _layer_spec((D_MODEL, FFN), layer),
                  _layer_spec((FFN, D_MODEL), layer)],
        out_specs=[row, row, row, hid, hid, hid, vec],
        out_shape=[jax.ShapeDtypeStruct((L, D_MODEL), F32), jax.ShapeDtypeStruct((L, D_MODEL), BF16),
                   jax.ShapeDtypeStruct((L, D_MODEL), BF16),
                   jax.ShapeDtypeStruct((L, FFN), BF16), jax.ShapeDtypeStruct((L, FFN), BF16),
                   jax.ShapeDtypeStruct((L, FFN), BF16), jax.ShapeDtypeStruct((1, D_MODEL), F32)],
    )(dx2, x1, gt, up, gain, w_gate, w_up, w_down)


def _inproj_bwd(dx1, x0, du_skip, du_scan, dz, dxbc, ddt, gain, w_pad, layer, name):
    L = x0.shape[0]

    def body(d_ref, x_ref, dus_ref, duc_ref, dz_ref, dxbc_ref, ddt_ref, g_ref, w_ref,
             dx_ref, dxb_ref, dp_ref, dgain_ref):
        first = pl.program_id(0) == 0
        xv = x_ref[...]
        dp = jnp.concatenate([dus_ref[...] + duc_ref[...], dz_ref[...], dxbc_ref[...], ddt_ref[...]],
                             axis=1).astype(BF16)
        dp_ref[...] = dp
        dh = _dot_nt(dp, w_ref[...])
        dxn, dgain = _rms_bwd(xv, _rstd(xv), g_ref[...], dh)
        dx = d_ref[...] + dxn
        dx_ref[...] = dx
        dxb_ref[...] = dx.astype(BF16)
        _acc_rows(dgain_ref, dgain, first)

    row = _row_spec(ROW_TILE, D_MODEL)
    vec = _full_spec((1, D_MODEL))
    return pl.pallas_call(
        body, name=name, grid=(L // ROW_TILE,),
        in_specs=[row, row, row, row, row, _row_spec(ROW_TILE, SSD_CONV_DIM), _row_spec(ROW_TILE, 256), vec,
                  _layer_spec((D_MODEL, IN_PAD), layer)],
        out_specs=[row, row, _row_spec(ROW_TILE, IN_PAD), vec],
        out_shape=[jax.ShapeDtypeStruct((L, D_MODEL), F32), jax.ShapeDtypeStruct((L, D_MODEL), BF16),
                   jax.ShapeDtypeStruct((L, IN_PAD), BF16), jax.ShapeDtypeStruct((1, D_MODEL), F32)],
    )(dx1, x0, du_skip, du_scan, dz, dxbc, ddt, gain, w_pad)


def _final_loss(x, gain, target, name):
    L = x.shape[0]

    def body(x_ref, g_ref, t_ref, loss_ref, dx_ref, dxb_ref, dgain_ref):
        first = pl.program_id(0) == 0
        xv = x_ref[...]
        r = _rstd(xv)
        err = xv * r * g_ref[...] - t_ref[...]
        part = 0.5 * jnp.sum(jnp.mean(err * err, axis=-1, keepdims=True), axis=0, keepdims=True)
        dx, dgain = _rms_bwd(xv, r, g_ref[...], err * (1.0 / D_MODEL))
        dx_ref[...] = dx
        dxb_ref[...] = dx.astype(BF16)
        _acc_rows(loss_ref, jnp.broadcast_to(part, (1, LANES)), first)
        _acc_rows(dgain_ref, dgain, first)

    row = _row_spec(ROW_TILE, D_MODEL)
    vec = _full_spec((1, D_MODEL))
    return pl.pallas_call(
        body, name=name, grid=(L // ROW_TILE,), in_specs=[row, vec, row],
        out_specs=[_full_spec((1, LANES)), row, row, vec],
        out_shape=[jax.ShapeDtypeStruct((1, LANES), F32), jax.ShapeDtypeStruct((L, D_MODEL), F32),
                   jax.ShapeDtypeStruct((L, D_MODEL), BF16), jax.ShapeDtypeStruct((1, D_MODEL), F32)],
    )(x, gain, target)


def _to_segments(a):
    L, n = a.shape
    return a.reshape(SEGS, L // SEGS, n).transpose(1, 0, 2).reshape(L, n)


def _from_segments(a):
    L, n = a.shape
    return a.reshape(L // SEGS, SEGS, n).transpose(1, 0, 2).reshape(L, n)


def _block_diag_in_to_state(m):
    m = m.reshape(S5_TILES, S5_TILE_GROUPS, S5_GROUP, S5_STATE)
    eye = jnp.eye(S5_TILE_GROUPS, dtype=m.dtype)
    out = m[:, :, :, None, :] * eye[None, :, None, :, None]
    return out.reshape(S5_TILES, S5_TILE_IN, S5_TILE_ST)


def _block_diag_extract(d):
    d = d.reshape(S5_TILES, S5_TILE_GROUPS, S5_GROUP, S5_TILE_GROUPS, S5_STATE)
    d = jnp.stack([d[:, a, :, a, :] for a in range(S5_TILE_GROUPS)], axis=1)
    return d.reshape(S5_GROUPS, S5_GROUP, S5_STATE)


def _pad_in_proj(w):
    z = jnp.zeros(w.shape[:-1] + (LANES - SSD_GROUP_HEADS,), w.dtype)
    return jnp.concatenate([w[..., :IN_MAIN + 8], z, w[..., IN_MAIN + 8:], z], axis=-1)


def _unpad_in_proj(w):
    return jnp.concatenate([w[..., :IN_MAIN + 8], w[..., IN_MAIN + LANES:IN_MAIN + LANES + 8]], axis=-1)


def _pad_heads(v):
    v = v.reshape(SSD_GROUPS, 1, SSD_GROUP_HEADS)
    return jnp.pad(v, ((0, 0), (0, 0), (0, LANES - SSD_GROUP_HEADS)))


def _unpad_heads(v):
    return v[:, 0, :SSD_GROUP_HEADS].reshape(SSD_HEADS)


def _layer_forward(x0, p, big, i):
    tag = "l%d_" % i
    ls = p["s5_log_step"].reshape(S5_GROUPS, 1)
    b_hgp = (p["s5_b_re"].transpose(2, 0, 1), p["s5_b_im"].transpose(2, 0, 1))
    are, aim, bbre, bbim = _s5_prep(p["s5_lam_re"], p["s5_lam_im"], ls, b_hgp[0], b_hgp[1], tag + "s5_prep")
    bre_ghp = bbre.transpose(1, 0, 2)
    bim_ghp = bbim.transpose(1, 0, 2)
    bre_bd = _block_diag_in_to_state(bre_ghp).astype(BF16)
    bim_bd = _block_diag_in_to_state(bim_ghp).astype(BF16)
    cret_bd = _block_diag_in_to_state(p["s5_c_re"]).astype(BF16)
    cimt_bd = _block_diag_in_to_state(p["s5_c_im"]).astype(BF16)
    s5mats = dict(bre_bd=bre_bd, bim_bd=bim_bd, cret_bd=cret_bd, cimt_bd=cimt_bd,
                  bret_bd=bre_bd.transpose(0, 2, 1), bimt_bd=bim_bd.transpose(0, 2, 1),
                  cre_bd=cret_bd.transpose(0, 2, 1), cim_bd=cimt_bd.transpose(0, 2, 1),
                  are=are.reshape(S5_TILES, 1, S5_TILE_ST), aim=aim.reshape(S5_TILES, 1, S5_TILE_ST))

    u, z, xbc, dt, h1 = _rms_inproj(x0, p["norm_mix"].reshape(1, -1), big["w_in"], i, tag + "rms_inproj")
    u_perm = _to_segments(u)
    ys_perm, xr, xi = _s5_scan(u_perm, bre_bd, bim_bd, s5mats["cre_bd"], s5mats["cim_bd"],
                               s5mats["are"], s5mats["aim"], tag + "s5_scan")
    ys = _from_segments(ys_perm)
    ya = _s5_post(ys, u, p["s5_d"].reshape(1, -1), big["s5_w_glu"], p["s5_b_glu"].reshape(1, -1),
                  p["s5_norm"].reshape(1, -1), i, tag + "s5_post")

    conv_w = jnp.pad(p["ssd_conv_w"], ((0, 8 - SSD_CONV), (0, 0)))
    conv_b = p["ssd_conv_b"].reshape(1, -1)
    xc = _ssd_conv(xbc, conv_w, conv_b, tag + "ssd_conv")
    heads = dict(dt_bias=_pad_heads(p["ssd_dt_bias"]), a_log=_pad_heads(p["ssd_a_log"]), d=_pad_heads(p["ssd_d"]))
    y, sprev = _ssd_scan(xc, dt, heads["dt_bias"], heads["a_log"], heads["d"], tag + "ssd_scan")
    yb = _ssd_post(y, z, p["ssd_norm"].reshape(1, -1), tag + "ssd_post")

    x1 = _out_proj(x0, ya, yb, big["w_out"], i, tag + "out_proj")
    x2, gt, up = _ffn(x1, p["norm_ffn"].reshape(1, -1), big["w_gate"], big["w_up"], big["w_down"], i, tag + "ffn")
    saved = dict(x0=x0, h1=h1, u=u, u_perm=u_perm, z=z, xbc=xbc, dt=dt, xr=xr, xi=xi, ys=ys, ya=ya, xc=xc, y=y,
                 sprev=sprev, yb=yb, x1=x1, gt=gt, up=up, s5mats=s5mats, heads=heads, conv_w=conv_w,
                 conv_b=conv_b, ls=ls, b_hgp=b_hgp)
    return x2, saved


def _layer_backward(dx2, dx2b, p, big, s, i):
    tag = "l%d_" % i
    g = {}
    dx1, dx1b, h2, act, dgt, dup, dgain = _ffn_bwd(dx2, s["x1"], s["gt"], s["up"], p["norm_ffn"].reshape(1, -1),
                                                  big["w_gate"], big["w_up"], big["w_down"], i, tag + "ffn_bwd")
    g["norm_ffn"] = dgain[0]
    g["w_down"] = _mm_tn(act, dx2b, tag + "dw_down")
    g["w_gate"] = _mm_tn(h2, dgt, tag + "dw_gate")
    g["w_up"] = _mm_tn(h2, dup, tag + "dw_up")
    g["w_out"] = _mm_tn(jnp.concatenate([s["ya"], s["yb"]], axis=1), dx1b, tag + "dw_out")

    dys, du_skip, gelu_b, dq_b, dgain, dd, dbg = _s5_post_bwd(
        dx1b, big["w_out"], s["ys"], s["u"], p["s5_d"].reshape(1, -1), big["s5_w_glu"],
        p["s5_b_glu"].reshape(1, -1), p["s5_norm"].reshape(1, -1), i, tag + "s5_post_bwd")
    g["s5_norm"] = dgain[0]
    g["s5_d"] = dd[0]
    g["s5_b_glu"] = dbg[0]
    g["s5_w_glu"] = _mm_tn(gelu_b, dq_b, tag + "dw_glu")
    m = s["s5mats"]
    du_perm, dar, dai, dcre_d, dcim_d, dbre_d, dbim_d = _s5_scan_bwd(
        _to_segments(dys), s["u_perm"], s["xr"], s["xi"], m["bret_bd"], m["bimt_bd"], m["cret_bd"], m["cimt_bd"],
        m["are"], m["aim"], tag + "s5_scan_bwd")
    du_scan = _from_segments(du_perm)
    g["s5_c_re"] = _block_diag_extract(dcre_d)
    g["s5_c_im"] = _block_diag_extract(dcim_d)
    dbbre = _block_diag_extract(dbre_d).transpose(1, 0, 2)
    dbbim = _block_diag_extract(dbim_d).transpose(1, 0, 2)
    dlr, dli, dls, dbre, dbim = _s5_prep_bwd(
        p["s5_lam_re"], p["s5_lam_im"], s["ls"], s["b_hgp"][0], s["b_hgp"][1],
        dar.reshape(S5_GROUPS, S5_STATE), dai.reshape(S5_GROUPS, S5_STATE), dbbre, dbbim, tag + "s5_prep_bwd")
    g["s5_lam_re"] = dlr
    g["s5_lam_im"] = dli
    g["s5_log_step"] = dls[:, 0]
    g["s5_b_re"] = dbre
    g["s5_b_im"] = dbim

    dy, dz, dgain = _ssd_post_bwd(dx1b, big["w_out"], s["y"], s["z"], p["ssd_norm"].reshape(1, -1), i,
                                  tag + "ssd_post_bwd")
    g["ssd_norm"] = dgain[0]
    hd = s["heads"]
    dxs, dbm, dcm, ddt, dbias, dalog, dd = _ssd_scan_bwd(dy, s["xc"], s["dt"], s["sprev"], hd["dt_bias"],
                                                       hd["a_log"], hd["d"], tag + "ssd_scan_bwd")
    g["ssd_dt_bias"] = _unpad_heads(dbias)
    g["ssd_a_log"] = _unpad_heads(dalog)
    g["ssd_d"] = _unpad_heads(dd)
    dxc = jnp.concatenate([dxs, dbm, dcm], axis=1)
    dxbc, dcw, dcb = _ssd_conv_bwd(dxc, s["xbc"], s["conv_w"], s["conv_b"], tag + "ssd_conv_bwd")
    g["ssd_conv_w"] = dcw[:SSD_CONV]
    g["ssd_conv_b"] = dcb[0]

    dx0, dx0b, dproj, dgain = _inproj_bwd(dx1, s["x0"], du_skip, du_scan, dz, dxbc, ddt, p["norm_mix"].reshape(1, -1),
                                          big["w_in"], i, tag + "inproj_bwd")
    g["norm_mix"] = dgain[0]
    g["w_in"] = _mm_tn(s["h1"], dproj, tag + "dw_in")
    return dx0, dx0b, g


BIG = ("w_in", "s5_w_glu", "w_out", "w_gate", "w_up", "w_down")
COL_SHARDED = ("w_in", "w_gate", "w_up")
LAYER_SMALL = ("norm_mix", "s5_lam_re", "s5_lam_im", "s5_log_step", "s5_b_re", "s5_b_im", "s5_c_re", "s5_c_im",
               "s5_d", "s5_b_glu", "s5_norm", "ssd_conv_w", "ssd_conv_b", "ssd_dt_bias", "ssd_a_log", "ssd_d",
               "ssd_norm", "norm_ffn")
WEIGHTS = ("norm_mix", "w_in", "s5_lam_re", "s5_lam_im", "s5_log_step", "s5_b_re", "s5_b_im", "s5_c_re", "s5_c_im",
           "s5_d", "s5_w_glu", "s5_b_glu", "s5_norm", "ssd_conv_w", "ssd_conv_b", "ssd_dt_bias", "ssd_a_log",
           "ssd_d", "ssd_norm", "w_out", "norm_ffn", "w_gate", "w_up", "w_down", "norm_final")


S5_BC = ("s5_b_re", "s5_b_im", "s5_c_re", "s5_c_im")
TINY = tuple(k for k in LAYER_SMALL if k not in S5_BC)


def _local_step(x, target, big, small, norm_final):
    saved = []
    h = x
    for i in range(DEPTH):
        p = {k: v[i] for k, v in small.items()}
        h, s = _layer_forward(h, p, big, i)
        saved.append((p, s))
    loss, dx, dxb, dgf = _final_loss(h, norm_final.reshape(1, -1), target, "final_loss")
    grads = [None] * DEPTH
    for i in reversed(range(DEPTH)):
        p, s = saved[i]
        dx, dxb, grads[i] = _layer_backward(dx, dxb, p, big, s, i)
    by_name = {k: [grads[i][k] for i in range(DEPTH)] for k in BIG + LAYER_SMALL}
    return loss[0, 0], dx, by_name, dgf[0]


def _my_place():
    return lax.axis_index("x"), lax.axis_index("y"), lax.axis_index("c")


def _all_gather8(blocks, name):
    nt = len(blocks)

    def body(*refs):
        ins = refs[:nt]
        outs = refs[nt:2 * nt]
        send_sems, recv_sems, local_sems = refs[2 * nt:]
        x, y, c = _my_place()
        me, sibling = (x, y, c), (x, y, 1 - c)
        chips = [(1 - x, y), (x, 1 - y), (1 - x, 1 - y)]

        def slot(t, place):
            px, py, pc = place
            return outs[t].at[4 * px + 2 * py + pc]

        def copy(t, k, block, to, src=None):
            return pltpu.make_async_remote_copy(
                src_ref=slot(t, block) if src is None else src, dst_ref=slot(t, block),
                send_sem=send_sems.at[t, k], recv_sem=recv_sems.at[t, k], device_id=to, device_id_type=MESH)

        mine = [pltpu.make_async_copy(ins[t], slot(t, me), local_sems.at[t]) for t in range(nt)]
        for cp in mine:
            cp.start()
        first = []
        for t in range(nt):
            first.append(copy(t, 0, me, sibling, src=ins[t]))
            first += [copy(t, 1 + j, me, (*chip, c), src=ins[t]) for j, chip in enumerate(chips)]
        for cp in first:
            cp.start()
        passed = []
        for j, chip in enumerate(chips):
            for t in range(nt):
                copy(t, 1 + j, (*chip, c), me).wait_recv()
                fwd = copy(t, 4 + j, (*chip, c), sibling)
                fwd.start()
                passed.append(fwd)
        for t in range(nt):
            copy(t, 0, sibling, me).wait_recv()
            for j, chip in enumerate(chips):
                copy(t, 4 + j, (*chip, 1 - c), me).wait_recv()
        for cp in first + passed:
            cp.wait_send()
        for cp in mine:
            cp.wait()

    return pl.pallas_call(
        body, name=name, in_specs=[ANY] * nt, out_specs=[ANY] * nt,
        out_shape=[jax.ShapeDtypeStruct((8,) + b.shape, b.dtype) for b in blocks],
        scratch_shapes=[pltpu.SemaphoreType.DMA((nt, 7)), pltpu.SemaphoreType.DMA((nt, 7)),
                        pltpu.SemaphoreType.DMA((nt,))],
    )(*blocks)


def _sibling_swap_other(pairs, name):
    nt = len(pairs)

    def body(*refs):
        ins = refs[:2 * nt]
        outs = refs[2 * nt:3 * nt]
        send_sems, recv_sems = refs[3 * nt:]
        x, y, c = _my_place()

        def copy(t, src):
            return pltpu.make_async_remote_copy(src_ref=src, dst_ref=outs[t], send_sem=send_sems.at[t],
                                                recv_sem=recv_sems.at[t], device_id=(x, y, 1 - c), device_id_type=MESH)

        for t in range(nt):
            @pl.when(c == 0)
            def _():
                copy(t, ins[2 * t + 1]).start()

            @pl.when(c == 1)
            def _():
                copy(t, ins[2 * t]).start()
        for t in range(nt):
            copy(t, ins[2 * t]).wait()

    flat = [a for pair in pairs for a in pair]
    return pl.pallas_call(
        body, name=name, in_specs=[ANY] * (2 * nt), out_specs=[ANY] * nt,
        out_shape=[jax.ShapeDtypeStruct(a0.shape, a0.dtype) for a0, _ in pairs],
        scratch_shapes=[pltpu.SemaphoreType.DMA((nt,)), pltpu.SemaphoreType.DMA((nt,))],
    )(*flat)


def _share_layers(arrs, name):
    nt = len(arrs)

    def body(*refs):
        ins = refs[:nt]
        outs = refs[nt:2 * nt]
        send_sems, recv_sems, local_sems = refs[2 * nt:]
        x, y, c = _my_place()
        local = [pltpu.make_async_copy(ins[t], outs[t].at[c], local_sems.at[t]) for t in range(nt)]
        sends = [pltpu.make_async_remote_copy(src_ref=ins[t], dst_ref=outs[t].at[c], send_sem=send_sems.at[t],
                                              recv_sem=recv_sems.at[t], device_id=(x, y, 1 - c), device_id_type=MESH)
                 for t in range(nt)]
        for cp in local + sends:
            cp.start()
        for t in range(nt):
            pltpu.make_async_remote_copy(src_ref=ins[t], dst_ref=outs[t].at[1 - c], send_sem=send_sems.at[t],
                                         recv_sem=recv_sems.at[t], device_id=(x, y, 1 - c),
                                         device_id_type=MESH).wait_recv()
        for cp in sends:
            cp.wait_send()
        for cp in local:
            cp.wait()

    return pl.pallas_call(
        body, name=name, in_specs=[ANY] * nt, out_specs=[ANY] * nt,
        out_shape=[jax.ShapeDtypeStruct((2,) + a.shape, a.dtype) for a in arrs],
        scratch_shapes=[pltpu.SemaphoreType.DMA((nt,)), pltpu.SemaphoreType.DMA((nt,)),
                        pltpu.SemaphoreType.DMA((nt,))],
    )(*arrs)


def _chip_all_to_all(arrs, name):
    nt = len(arrs)

    def body(*refs):
        ins = refs[:nt]
        outs = refs[nt:2 * nt]
        send_sems, recv_sems, local_sems = refs[2 * nt:]
        x, y, c = _my_place()
        mine = 2 * x + y
        chips = [(1 - x, y), (x, 1 - y), (1 - x, 1 - y)]
        local = [pltpu.make_async_copy(ins[t].at[mine], outs[t].at[mine], local_sems.at[t]) for t in range(nt)]
        for cp in local:
            cp.start()
        sends = []
        for t in range(nt):
            for j, (px, py) in enumerate(chips):
                sends.append(pltpu.make_async_remote_copy(
                    src_ref=ins[t].at[2 * px + py], dst_ref=outs[t].at[mine], send_sem=send_sems.at[t, j],
                    recv_sem=recv_sems.at[t, j], device_id=(px, py, c), device_id_type=MESH))
        for cp in sends:
            cp.start()
        for t in range(nt):
            for j, (px, py) in enumerate(chips):
                pltpu.make_async_remote_copy(
                    src_ref=ins[t].at[mine], dst_ref=outs[t].at[2 * px + py], send_sem=send_sems.at[t, j],
                    recv_sem=recv_sems.at[t, j], device_id=(px, py, c), device_id_type=MESH).wait_recv()
        for cp in sends:
            cp.wait_send()
        for cp in local:
            cp.wait()

    return pl.pallas_call(
        body, name=name, in_specs=[ANY] * nt, out_specs=[ANY] * nt,
        out_shape=[jax.ShapeDtypeStruct(a.shape, a.dtype) for a in arrs],
        scratch_shapes=[pltpu.SemaphoreType.DMA((nt, 3)), pltpu.SemaphoreType.DMA((nt, 3)),
                        pltpu.SemaphoreType.DMA((nt,))],
    )(*arrs)


def _as_rows(a):
    return a.reshape(-1, a.shape[-1])


STREAM_VMEM_BYTES = 16 * 1024 * 1024


def _row_tile(rows, cols, n_arrays):
    lanes = -(-cols // LANES) * LANES
    for t in (512, 256, 128, 64, 32, 16, 8):
        if rows % t == 0 and 2 * n_arrays * t * lanes * 4 <= STREAM_VMEM_BYTES:
            return t
    return rows


def _pair_add(a0, a1, recv, name):
    rows, cols = a0.shape
    tile = _row_tile(rows, cols, 4)

    def body(a0_ref, a1_ref, r_ref, o_ref):
        mine = jnp.where(lax.axis_index("c") == 0, a0_ref[...], a1_ref[...])
        o_ref[...] = (mine.astype(F32) + r_ref[...].astype(F32)).astype(o_ref.dtype)

    spec = pl.BlockSpec((tile, cols), lambda i: (i, 0))
    return pl.pallas_call(body, name=name, grid=(rows // tile,), in_specs=[spec] * 3, out_specs=spec,
                          out_shape=jax.ShapeDtypeStruct((rows, cols), a0.dtype))(a0, a1, recv)


def _sum_leading(a, name):
    n, rows, cols = a.shape
    tile = _row_tile(rows, cols, n + 1)

    def body(a_ref, o_ref):
        acc = a_ref[0].astype(F32)
        for k in range(1, n):
            acc = acc + a_ref[k].astype(F32)
        o_ref[...] = acc

    return pl.pallas_call(
        body, name=name, grid=(rows // tile,), in_specs=[pl.BlockSpec((n, tile, cols), lambda i: (0, i, 0))],
        out_specs=pl.BlockSpec((tile, cols), lambda i: (i, 0)),
        out_shape=jax.ShapeDtypeStruct((rows, cols), F32))(a)


def _adamw_math(w, g, m, v):
    mn = ADAM_B1 * m + (1.0 - ADAM_B1) * g
    vn = ADAM_B2 * v + (1.0 - ADAM_B2) * jnp.square(g)
    m_hat = mn / (1.0 - ADAM_B1 ** ADAM_STEP)
    v_hat = vn / (1.0 - ADAM_B2 ** ADAM_STEP)
    delta = -ADAM_LR * (m_hat / (jnp.sqrt(v_hat) + ADAM_EPS) + ADAM_WD * w)
    return delta, mn, vn


def _adamw_layers(w, g, m, v, name):
    depth, rows, cols = w.shape
    tile = _row_tile(rows, cols, 8)

    def body(w_ref, g_ref, m_ref, v_ref, d_ref, nm_ref, nv_ref, go_ref):
        gv = g_ref[...]
        d_ref[...], nm_ref[...], nv_ref[...] = _adamw_math(w_ref[...], gv, m_ref[...], v_ref[...])
        go_ref[...] = gv

    spec = pl.BlockSpec((None, tile, cols), lambda l, i: (l, i, 0))
    shape = jax.ShapeDtypeStruct((depth, rows, cols), F32)
    return pl.pallas_call(body, name=name, grid=(depth, rows // tile), in_specs=[spec] * 4, out_specs=[spec] * 4,
                          out_shape=[shape] * 4)(w, g, m, v)


def _adamw_many(ws, gs, ms, vs, name):
    nt = len(ws)

    def body(*refs):
        for t in range(nt):
            w_ref, g_ref, m_ref, v_ref = (refs[k * nt + t] for k in range(4))
            d_ref, nm_ref, nv_ref = (refs[(4 + k) * nt + t] for k in range(3))
            d_ref[...], nm_ref[...], nv_ref[...] = _adamw_math(w_ref[...], g_ref[...], m_ref[...], v_ref[...])

    shapes = [jax.ShapeDtypeStruct(a.shape, F32) for a in ws]
    out = pl.pallas_call(body, name=name, out_shape=shapes * 3)(*ws, *gs, *ms, *vs)
    return out[:nt], out[nt:2 * nt], out[2 * nt:]


TINY_ROWS_MULTIPLE = 128


def _flat_pack(arrs):
    flat = jnp.concatenate([a.reshape(-1) for a in arrs])
    pad = (-flat.shape[0]) % (TINY_ROWS_MULTIPLE * LANES)
    return jnp.pad(flat, (0, pad)).reshape(-1, LANES)


def _flat_unpack(buf, shapes):
    flat = buf.reshape(-1)
    out = []
    off = 0
    for shp in shapes:
        n = math.prod(shp)
        out.append(flat[off:off + n].reshape(shp))
        off += n
    return out


def _to_chunks(a, name):
    if name == "w_in":
        a = _unpad_in_proj(a)
    rows, cols = a.shape
    if name in COL_SHARDED:
        return a.reshape(rows, 4, cols // 4).transpose(1, 0, 2)
    return a.reshape(4, rows // 4, cols)


def _from_chunks(a, name):
    _, depth, r, cc = a.shape
    if name in COL_SHARDED:
        return a.transpose(1, 2, 0, 3).reshape(depth, r, 4 * cc)
    return a.transpose(1, 0, 2, 3).reshape(depth, 4 * r, cc)


def kernel(x, norm_mix, w_in, s5_lam_re, s5_lam_im, s5_log_step, s5_b_re, s5_b_im, s5_c_re, s5_c_im, s5_d, s5_w_glu, s5_b_glu, s5_norm, ssd_conv_w, ssd_conv_b, ssd_dt_bias, ssd_a_log, ssd_d, ssd_norm, w_out, norm_ffn, w_gate, w_up, w_down, norm_final, loss_target, m_norm_mix, m_w_in, m_s5_lam_re, m_s5_lam_im, m_s5_log_step, m_s5_b_re, m_s5_b_im, m_s5_c_re, m_s5_c_im, m_s5_d, m_s5_w_glu, m_s5_b_glu, m_s5_norm, m_ssd_conv_w, m_ssd_conv_b, m_ssd_dt_bias, m_ssd_a_log, m_ssd_d, m_ssd_norm, m_w_out, m_norm_ffn, m_w_gate, m_w_up, m_w_down, m_norm_final, v_norm_mix, v_w_in, v_s5_lam_re, v_s5_lam_im, v_s5_log_step, v_s5_b_re, v_s5_b_im, v_s5_c_re, v_s5_c_im, v_s5_d, v_s5_w_glu, v_s5_b_glu, v_s5_norm, v_ssd_conv_w, v_ssd_conv_b, v_ssd_dt_bias, v_ssd_a_log, v_ssd_d, v_ssd_norm, v_w_out, v_norm_ffn, v_w_gate, v_w_up, v_w_down, v_norm_final):
    args = dict(locals())
    w = {k: args[k] for k in WEIGHTS}
    m = {k: args["m_" + k] for k in WEIGHTS}
    v = {k: args["v_" + k] for k in WEIGHTS}
    cx, cy, cc = _my_place()
    chip = 2 * cx + cy

    mine = [lax.dynamic_index_in_dim(w[k], cc, 0, keepdims=False).astype(BF16) for k in BIG]
    gathered = _all_gather8(mine + [w["ssd_conv_w"].reshape(DEPTH * SSD_CONV, -1)], "gather_weights")
    big = {}
    for k, a in zip(BIG, gathered):
        big[k] = _from_chunks(a.reshape((4, DEPTH) + a.shape[1:]), k)
    big["w_in"] = _pad_in_proj(big["w_in"])
    conv_rows = gathered[-1]
    conv_full = conv_rows.reshape(4, 2, DEPTH, SSD_CONV, -1)[:, 0].transpose(1, 2, 0, 3).reshape(
        DEPTH, SSD_CONV, SSD_CONV_DIM)
    small = {k: w[k] for k in LAYER_SMALL}
    small["ssd_conv_w"] = conv_full

    loss_part, grad_x, g, g_final = _local_step(x[0], loss_target[0], big, small, w["norm_final"])

    bc_rows = 2 * DEPTH * S5_GROUP * S5_GROUPS
    b_all = jnp.stack([g["s5_b_re"][0], g["s5_b_im"][0], g["s5_b_re"][1], g["s5_b_im"][1]]).reshape(bc_rows, S5_STATE)
    c_all = jnp.stack([g["s5_c_re"][0], g["s5_c_im"][0], g["s5_c_re"][1], g["s5_c_im"][1]]).reshape(bc_rows, S5_STATE)
    pairs = [(g[k][0], g[k][1]) for k in BIG] + [(b_all, c_all)]
    from_sib = _sibling_swap_other(pairs, "grad_swap")
    chunks = []
    for k, (a0, a1), r in zip(BIG + ("s5_bc",), pairs, from_sib):
        part = _pair_add(a0, a1, r, "grad_pair_" + k)
        chunks.append(part.reshape(4, bc_rows // 4, S5_STATE) if k == "s5_bc" else _to_chunks(part, k))
    landed = _chip_all_to_all(chunks, "grad_scatter")
    reduced = [_sum_leading(a, "grad_sum_" + k) for k, a in zip(BIG + ("s5_bc",), landed)]
    shared = _share_layers(reduced[:-1], "grad_share")
    grads = dict(zip(BIG, shared))

    tiny_names = TINY + ("norm_final",)
    parts = [jnp.stack(g[k]) for k in TINY] + [g_final, loss_part.reshape(1)]
    shapes = [p.shape for p in parts]
    allparts, bc_eighths = _all_gather8([_flat_pack(parts), reduced[-1]], "gather_small")
    unpacked = _flat_unpack(_sum_leading(allparts, "sum_small"), shapes)
    loss = unpacked[-1][0]
    grads.update(zip(tiny_names, unpacked[:-1]))
    width = SSD_CONV_DIM // 4
    grads["ssd_conv_w"] = lax.dynamic_slice_in_dim(grads["ssd_conv_w"], chip * width, width, axis=2)
    bc = bc_eighths.reshape(4, 2, bc_rows // 4, S5_STATE)
    b_sum = bc[:, 0].reshape(DEPTH, 2, S5_GROUP, S5_GROUPS, S5_STATE)
    c_sum = bc[:, 1].reshape(DEPTH, 2, S5_GROUPS, S5_GROUP, S5_STATE)
    grads["s5_b_re"] = b_sum[:, 0].transpose(0, 2, 3, 1)
    grads["s5_b_im"] = b_sum[:, 1].transpose(0, 2, 3, 1)
    grads["s5_c_re"] = c_sum[:, 0]
    grads["s5_c_im"] = c_sum[:, 1]

    delta, new_m, new_v = {}, {}, {}
    for k in BIG:
        delta[k], new_m[k], new_v[k], grads[k] = _adamw_layers(w[k], grads[k], m[k], v[k], "adamw_" + k)
    for k in ("s5_b_re", "s5_b_im"):
        shp = w[k].shape
        rows = lambda a: a.reshape(DEPTH, -1, shp[-1])
        d, nm, nv, _ = _adamw_layers(rows(w[k]), rows(grads[k]), rows(m[k]), rows(v[k]), "adamw_" + k)
        delta[k], new_m[k], new_v[k] = d.reshape(shp), nm.reshape(shp), nv.reshape(shp)
    names = tiny_names + ("s5_c_re", "s5_c_im")
    as2d = lambda a: a.reshape(1, -1) if a.ndim == 1 else a
    ds, nms, nvs = _adamw_many([as2d(w[k]) for k in names], [as2d(grads[k]) for k in names],
                               [as2d(m[k]) for k in names], [as2d(v[k]) for k in names], "adamw_small")
    for k, a, b, c in zip(names, ds, nms, nvs):
        delta[k], new_m[k], new_v[k] = (t.reshape(w[k].shape) for t in (a, b, c))

    return (loss, grad_x[None], *[grads[k] for k in WEIGHTS], *[delta[k] for k in WEIGHTS],
            *[new_m[k] for k in WEIGHTS], *[new_v[k] for k in WEIGHTS])
```

```python
import functools
import math

import jax
import jax.numpy as jnp
from jax import lax
from jax.experimental import pallas as pl
from jax.experimental.pallas import tpu as pltpu

F32 = jnp.float32
BF16 = jnp.bfloat16
MESH = pl.DeviceIdType.MESH
ANY = pl.BlockSpec(memory_space=pl.ANY)

D_MODEL = 1024
DEPTH = 2
S5_GROUPS = 64
S5_GROUP = 16
S5_STATE = 64
S5_COLS = S5_GROUPS * S5_STATE
S5_TILE_GROUPS = 8
S5_TILES = S5_GROUPS // S5_TILE_GROUPS
S5_TILE_IN = S5_TILE_GROUPS * S5_GROUP
S5_TILE_ST = S5_TILE_GROUPS * S5_STATE
SEGS = 8
SSD_HEADS = 16
SSD_HEAD_DIM = 64
SSD_GROUPS = 2
SSD_GROUP_HEADS = SSD_HEADS // SSD_GROUPS
SSD_STATE = 128
SSD_CONV = 4
SSD_CHUNK = 128
SSD_WIDTH = 1024
SSD_CONV_DIM = SSD_WIDTH + 2 * SSD_GROUPS * SSD_STATE
IN_PROJ = 3600
IN_MAIN = 3584
IN_PAD = IN_MAIN + 2 * 128
FFN = 2816
EPS = 1e-6
LANES = 128
ROW_TILE = 256

ADAM_LR = 0.001
ADAM_B1 = 0.9
ADAM_B2 = 0.999
ADAM_EPS = 1e-08
ADAM_WD = 0.01
ADAM_STEP = 10

HIGHEST = lax.Precision.HIGHEST


def _sigmoid(x):
    return 1.0 / (1.0 + jnp.exp(-x))


def _silu(x):
    return x * _sigmoid(x)


def _dsilu(x):
    s = _sigmoid(x)
    return s * (1.0 + x * (1.0 - s))


_GELU_K = math.sqrt(2.0 / math.pi)
_GELU_C = 0.044715


def _gelu(x):
    t = jnp.tanh(_GELU_K * (x + _GELU_C * x * x * x))
    return 0.5 * x * (1.0 + t)


def _dgelu(x):
    t = jnp.tanh(_GELU_K * (x + _GELU_C * x * x * x))
    return 0.5 * (1.0 + t) + 0.5 * x * (1.0 - t * t) * _GELU_K * (1.0 + 3.0 * _GELU_C * x * x)


def _softplus(x):
    e = jnp.exp(-jnp.abs(x))
    u = 1.0 + e
    log1p = jnp.where(u == 1.0, e, jnp.log(u) * e / jnp.where(u == 1.0, 1.0, u - 1.0))
    return jnp.maximum(x, 0.0) + log1p


def _rstd(x):
    return lax.rsqrt(jnp.mean(x * x, axis=-1, keepdims=True) + EPS)


def _rms_bwd(x, r, gain, dy):
    dyg = dy * gain
    dx = r * dyg - x * (r * r * r) * jnp.mean(x * dyg, axis=-1, keepdims=True)
    dgain = jnp.sum(dy * x * r, axis=0, keepdims=True)
    return dx, dgain


def _dot(a, b):
    return jnp.dot(a, b, preferred_element_type=F32)


def _dot_nt(a, b):
    return lax.dot_general(a, b, (((1,), (1,)), ((), ())), preferred_element_type=F32)


def _dot_tn(a, b):
    return lax.dot_general(a, b, (((0,), (0,)), ((), ())), preferred_element_type=F32)


def _row_spec(tile, cols):
    return pl.BlockSpec((tile, cols), lambda i: (i, 0))


def _full_spec(shape):
    nd = len(shape)
    return pl.BlockSpec(shape, lambda *_: (0,) * nd)


def _const_spec(shape):
    nd = len(shape)
    return pl.BlockSpec(shape, lambda *_: (0,) * nd, pipeline_mode=pl.Buffered(1))


def _layer_spec(shape, layer, block=0):
    return pl.BlockSpec((None,) + tuple(shape), lambda *_: (layer, block, 0), pipeline_mode=pl.Buffered(1))


def _acc_rows(ref, val, first):
    @pl.when(first)
    def _():
        ref[...] = val

    @pl.when(jnp.logical_not(first))
    def _():
        ref[...] += val


def _pick_tile(n, cap):
    best = LANES
    for t in range(LANES, cap + 1, LANES):
        if n % t == 0:
            best = t
    return best


def _mm_tn(a, b, name):
    k, m = a.shape
    _, n = b.shape
    tm = _pick_tile(m, 512)
    tn = _pick_tile(n, 1536)

    def body(a_ref, b_ref, o_ref):
        o_ref[...] = _dot_tn(a_ref[...], b_ref[...]).astype(BF16)

    return pl.pallas_call(
        body, name=name, grid=(n // tn, m // tm),
        in_specs=[pl.BlockSpec((k, tm), lambda j, i: (0, i)), pl.BlockSpec((k, tn), lambda j, i: (0, j))],
        out_specs=pl.BlockSpec((tm, tn), lambda j, i: (i, j)),
        out_shape=jax.ShapeDtypeStruct((m, n), BF16),
    )(a, b)


def _rms_inproj(x, gain, w_pad, layer, name):
    L = x.shape[0]

    def body(x_ref, g_ref, w_ref, u_ref, z_ref, xbc_ref, dt_ref, h_ref):
        xv = x_ref[...]
        h = (xv * _rstd(xv) * g_ref[...]).astype(BF16)
        h_ref[...] = h
        p = _dot(h, w_ref[...])
        u_ref[...] = p[:, :1024]
        z_ref[...] = p[:, 1024:2048]
        xbc_ref[...] = p[:, 2048:IN_MAIN]
        dt_ref[...] = p[:, IN_MAIN:IN_PAD]

    return pl.pallas_call(
        body, name=name, grid=(L // ROW_TILE,),
        in_specs=[_row_spec(ROW_TILE, D_MODEL), _full_spec((1, D_MODEL)), _layer_spec((D_MODEL, IN_PAD), layer)],
        out_specs=[_row_spec(ROW_TILE, 1024), _row_spec(ROW_TILE, 1024), _row_spec(ROW_TILE, SSD_CONV_DIM),
                   _row_spec(ROW_TILE, 256), _row_spec(ROW_TILE, D_MODEL)],
        out_shape=[jax.ShapeDtypeStruct((L, 1024), F32), jax.ShapeDtypeStruct((L, 1024), F32),
                   jax.ShapeDtypeStruct((L, SSD_CONV_DIM), F32), jax.ShapeDtypeStruct((L, 256), F32),
                   jax.ShapeDtypeStruct((L, D_MODEL), BF16)],
    )(x, gain, w_pad)


def _s5_prep_math(lr, li, ls, bre, bim):
    step = jnp.exp(ls)
    mag = jnp.exp(lr * step)
    ang = li * step
    are = mag * jnp.cos(ang)
    aim = mag * jnp.sin(ang)
    den = lr * lr + li * li
    nr = are - 1.0
    ni = aim
    cre = (nr * lr + ni * li) / den
    cim = (ni * lr - nr * li) / den
    bbre = cre[None] * bre - cim[None] * bim
    bbim = cre[None] * bim + cim[None] * bre
    return are, aim, bbre, bbim


def _s5_prep(lr, li, ls, bre, bim, name):
    def body(lr_ref, li_ref, ls_ref, bre_ref, bim_ref, are_ref, aim_ref, bbre_ref, bbim_ref):
        are, aim, bbre, bbim = _s5_prep_math(lr_ref[...], li_ref[...], ls_ref[...], bre_ref[...], bim_ref[...])
        are_ref[...] = are
        aim_ref[...] = aim
        bbre_ref[...] = bbre
        bbim_ref[...] = bbim

    gp = jax.ShapeDtypeStruct((S5_GROUPS, S5_STATE), F32)
    hgp = jax.ShapeDtypeStruct((S5_GROUP, S5_GROUPS, S5_STATE), F32)
    return pl.pallas_call(body, name=name, out_shape=[gp, gp, hgp, hgp])(lr, li, ls, bre, bim)


def _s5_prep_bwd(lr, li, ls, bre, bim, dare, daim, dbbre, dbbim, name):
    def body(lr_ref, li_ref, ls_ref, bre_ref, bim_ref, dare_ref, daim_ref, dbbre_ref, dbbim_ref,
             dlr_ref, dli_ref, dls_ref, dbre_ref, dbim_ref):
        _, vjp = jax.vjp(_s5_prep_math, lr_ref[...], li_ref[...], ls_ref[...], bre_ref[...], bim_ref[...])
        dlr, dli, dls, dbre, dbim = vjp((dare_ref[...], daim_ref[...], dbbre_ref[...], dbbim_ref[...]))
        dlr_ref[...] = dlr
        dli_ref[...] = dli
        dls_ref[...] = dls
        dbre_ref[...] = dbre
        dbim_ref[...] = dbim

    gp = jax.ShapeDtypeStruct((S5_GROUPS, S5_STATE), F32)
    g1 = jax.ShapeDtypeStruct((S5_GROUPS, 1), F32)
    hgp = jax.ShapeDtypeStruct((S5_GROUP, S5_GROUPS, S5_STATE), F32)
    return pl.pallas_call(body, name=name, out_shape=[gp, gp, g1, hgp, hgp])(
        lr, li, ls, bre, bim, dare, daim, dbbre, dbbim)


def _cmul_add(ar, ai, sr, si, br, bi):
    return ar * sr - ai * si + br, ar * si + ai * sr + bi


def _shift_rows_down(v):
    rolled = pltpu.roll(v, 1, 0)
    row = lax.broadcasted_iota(jnp.int32, v.shape, 0)
    return jnp.where(row == 0, 0.0, rolled)


def _shift_rows_up(v):
    rolled = pltpu.roll(v, SEGS - 1, 0)
    row = lax.broadcasted_iota(jnp.int32, v.shape, 0)
    return jnp.where(row == SEGS - 1, 0.0, rolled)


def _segment_power(ar, ai, steps):
    n = 1
    while n < steps:
        ar, ai = ar * ar - ai * ai, 2.0 * ar * ai
        n *= 2
    assert n == steps
    return ar, ai


def _segment_entries(ar, ai, fr, fi, steps, shift):
    pr, pi = _segment_power(ar, ai, steps)
    er = jnp.zeros_like(fr)
    ei = jnp.zeros_like(fi)
    for _ in range(SEGS - 1):
        nr, ni = _cmul_add(pr, pi, er, ei, fr, fi)
        er, ei = shift(nr), shift(ni)
    return er, ei


def _s5_scan(u_perm, bre_bd, bim_bd, cre_bd, cim_bd, are, aim, name):
    L = u_perm.shape[0]
    steps = L // SEGS

    def body(u_ref, bre_ref, bim_ref, cre_ref, cim_ref, are_ref, aim_ref, y_ref, xr_ref, xi_ref):
        u = u_ref[...].astype(BF16)
        xr_ref[...] = _dot(u, bre_ref[0])
        xi_ref[...] = _dot(u, bim_ref[0])
        ar = jnp.broadcast_to(are_ref[0], (SEGS, S5_TILE_ST))
        ai = jnp.broadcast_to(aim_ref[0], (SEGS, S5_TILE_ST))
        zero = jnp.zeros((SEGS, S5_TILE_ST), F32)

        def finals(j, c):
            rows = pl.ds(pl.multiple_of(j * SEGS, SEGS), SEGS)
            return _cmul_add(ar, ai, c[0], c[1], xr_ref[rows, :], xi_ref[rows, :])

        fr, fi = lax.fori_loop(0, steps, finals, (zero, zero), unroll=4)
        er, ei = _segment_entries(ar, ai, fr, fi, steps, _shift_rows_down)

        def scan(j, c):
            rows = pl.ds(pl.multiple_of(j * SEGS, SEGS), SEGS)
            sr, si = _cmul_add(ar, ai, c[0], c[1], xr_ref[rows, :], xi_ref[rows, :])
            xr_ref[rows, :] = sr
            xi_ref[rows, :] = si
            return sr, si

        lax.fori_loop(0, steps, scan, (er, ei), unroll=4)
        y_ref[...] = (_dot(xr_ref[...].astype(BF16), cre_ref[0]) - _dot(xi_ref[...].astype(BF16), cim_ref[0]))

    tile3 = lambda a, b: pl.BlockSpec((1, a, b), lambda k: (k, 0, 0))
    return pl.pallas_call(
        body, name=name, grid=(S5_TILES,),
        in_specs=[pl.BlockSpec((L, S5_TILE_IN), lambda k: (0, k)),
                  tile3(S5_TILE_IN, S5_TILE_ST), tile3(S5_TILE_IN, S5_TILE_ST),
                  tile3(S5_TILE_ST, S5_TILE_IN), tile3(S5_TILE_ST, S5_TILE_IN),
                  tile3(1, S5_TILE_ST), tile3(1, S5_TILE_ST)],
        out_specs=[pl.BlockSpec((L, S5_TILE_IN), lambda k: (0, k)),
                   pl.BlockSpec((L, S5_TILE_ST), lambda k: (0, k)), pl.BlockSpec((L, S5_TILE_ST), lambda k: (0, k))],
        out_shape=[jax.ShapeDtypeStruct((L, 1024), F32), jax.ShapeDtypeStruct((L, S5_COLS), F32),
                   jax.ShapeDtypeStruct((L, S5_COLS), F32)],
    )(u_perm, bre_bd, bim_bd, cre_bd, cim_bd, are, aim)


def _s5_scan_bwd(dy_perm, u_perm, xr, xi, bret_bd, bimt_bd, cret_bd, cimt_bd, are, aim, name):
    L = u_perm.shape[0]
    steps = L // SEGS

    def body(dy_ref, u_ref, xr_ref, xi_ref, bret_ref, bimt_ref, cret_ref, cimt_ref, are_ref, aim_ref,
             du_ref, dar_ref, dai_ref, dcre_ref, dcim_ref, dbre_ref, dbim_ref, gr_ref, gi_ref):
        dy = dy_ref[...].astype(BF16)
        u = u_ref[...].astype(BF16)
        gr_ref[...] = _dot(dy, cret_ref[0])
        gi_ref[...] = -_dot(dy, cimt_ref[0])
        ar = jnp.broadcast_to(are_ref[0], (SEGS, S5_TILE_ST))
        ai = -jnp.broadcast_to(aim_ref[0], (SEGS, S5_TILE_ST))
        zero = jnp.zeros((SEGS, S5_TILE_ST), F32)

        def finals(k, c):
            rows = pl.ds(pl.multiple_of((steps - 1 - k) * SEGS, SEGS), SEGS)
            return _cmul_add(ar, ai, c[0], c[1], gr_ref[rows, :], gi_ref[rows, :])

        fr, fi = lax.fori_loop(0, steps, finals, (zero, zero), unroll=4)
        er, ei = _segment_entries(ar, ai, fr, fi, steps, _shift_rows_up)

        def scan(k, c):
            sr0, si0, accr, acci = c
            j = steps - 1 - k
            rows = pl.ds(pl.multiple_of(j * SEGS, SEGS), SEGS)
            sr, si = _cmul_add(ar, ai, sr0, si0, gr_ref[rows, :], gi_ref[rows, :])
            gr_ref[rows, :] = sr
            gi_ref[rows, :] = si
            prev = pl.ds(pl.multiple_of(jnp.maximum(j - 1, 0) * SEGS, SEGS), SEGS)
            live = (j > 0).astype(F32)
            xpr = xr_ref[prev, :] * live
            xpi = xi_ref[prev, :] * live
            return sr, si, accr + sr * xpr + si * xpi, acci + si * xpr - sr * xpi

        _, _, accr, acci = lax.fori_loop(0, steps, scan, (er, ei, zero, zero), unroll=4)
        first = pl.ds(0, SEGS)
        last = pl.ds((steps - 1) * SEGS, SEGS)
        xpr = _shift_rows_down(xr_ref[last, :])
        xpi = _shift_rows_down(xi_ref[last, :])
        g0r = gr_ref[first, :]
        g0i = gi_ref[first, :]
        accr = accr + g0r * xpr + g0i * xpi
        acci = acci + g0i * xpr - g0r * xpi
        dar_ref[0] = jnp.sum(accr, axis=0, keepdims=True)
        dai_ref[0] = jnp.sum(acci, axis=0, keepdims=True)

        grb = gr_ref[...].astype(BF16)
        gib = gi_ref[...].astype(BF16)
        du_ref[...] = _dot(grb, bret_ref[0]) + _dot(gib, bimt_ref[0])
        dbre_ref[0] = _dot_tn(u, grb)
        dbim_ref[0] = _dot_tn(u, gib)
        dcre_ref[0] = _dot_tn(dy, xr_ref[...].astype(BF16))
        dcim_ref[0] = -_dot_tn(dy, xi_ref[...].astype(BF16))

    tile3 = lambda a, b: pl.BlockSpec((1, a, b), lambda k: (k, 0, 0))
    col_in = pl.BlockSpec((L, S5_TILE_IN), lambda k: (0, k))
    col_st = pl.BlockSpec((L, S5_TILE_ST), lambda k: (0, k))
    dense = jax.ShapeDtypeStruct((S5_TILES, S5_TILE_IN, S5_TILE_ST), F32)
    vec = jax.ShapeDtypeStruct((S5_TILES, 1, S5_TILE_ST), F32)
    return pl.pallas_call(
        body, name=name, grid=(S5_TILES,),
        in_specs=[col_in, col_in, col_st, col_st,
                  tile3(S5_TILE_ST, S5_TILE_IN), tile3(S5_TILE_ST, S5_TILE_IN),
                  tile3(S5_TILE_IN, S5_TILE_ST), tile3(S5_TILE_IN, S5_TILE_ST),
                  tile3(1, S5_TILE_ST), tile3(1, S5_TILE_ST)],
        out_specs=[col_in, tile3(1, S5_TILE_ST), tile3(1, S5_TILE_ST),
                   tile3(S5_TILE_IN, S5_TILE_ST), tile3(S5_TILE_IN, S5_TILE_ST),
                   tile3(S5_TILE_IN, S5_TILE_ST), tile3(S5_TILE_IN, S5_TILE_ST)],
        out_shape=[jax.ShapeDtypeStruct((L, 1024), F32), vec, vec, dense, dense, dense, dense],
        scratch_shapes=[pltpu.VMEM((L, S5_TILE_ST), F32), pltpu.VMEM((L, S5_TILE_ST), F32)],
    )(dy_perm, u_perm, xr, xi, bret_bd, bimt_bd, cret_bd, cimt_bd, are, aim)


def _s5_post(ys, u, d_skip, w_glu, b_glu, gain, layer, name):
    L = ys.shape[0]

    def body(ys_ref, u_ref, d_ref, w_ref, b_ref, g_ref, ya_ref):
        g = _gelu(ys_ref[...] + d_ref[...] * u_ref[...])
        q = _dot(g.astype(BF16), w_ref[...]) + b_ref[...]
        oa = g * _sigmoid(q)
        ya_ref[...] = (oa * _rstd(oa) * g_ref[...]).astype(BF16)

    vec = _full_spec((1, 1024))
    return pl.pallas_call(
        body, name=name, grid=(L // ROW_TILE,),
        in_specs=[_row_spec(ROW_TILE, 1024), _row_spec(ROW_TILE, 1024), vec, _layer_spec((1024, 1024), layer), vec,
                  vec],
        out_specs=_row_spec(ROW_TILE, 1024),
        out_shape=jax.ShapeDtypeStruct((L, 1024), BF16),
    )(ys, u, d_skip, w_glu, b_glu, gain)


def _s5_post_bwd(dx, w_out, ys, u, d_skip, w_glu, b_glu, gain, layer, name):
    L = ys.shape[0]

    def body(dx_ref, wo_ref, ys_ref, u_ref, d_ref, w_ref, b_ref, gn_ref,
             dys_ref, dus_ref, g_ref, dq_ref, dgain_ref, dd_ref, db_ref):
        first = pl.program_id(0) == 0
        uv = u_ref[...]
        yt = ys_ref[...] + d_ref[...] * uv
        g = _gelu(yt)
        gb = g.astype(BF16)
        q = _dot(gb, w_ref[...]) + b_ref[...]
        s = _sigmoid(q)
        oa = g * s
        dya = _dot_nt(dx_ref[...], wo_ref[...])
        doa, dgain = _rms_bwd(oa, _rstd(oa), gn_ref[...], dya)
        dq = doa * g * s * (1.0 - s)
        dqb = dq.astype(BF16)
        dg = doa * s + _dot_nt(dqb, w_ref[...])
        dyt = dg * _dgelu(yt)
        dys_ref[...] = dyt
        dus_ref[...] = dyt * d_ref[...]
        g_ref[...] = gb
        dq_ref[...] = dqb
        _acc_rows(dgain_ref, dgain, first)
        _acc_rows(dd_ref, jnp.sum(dyt * uv, axis=0, keepdims=True), first)
        _acc_rows(db_ref, jnp.sum(dq, axis=0, keepdims=True), first)

    vec = _full_spec((1, 1024))
    row = _row_spec(ROW_TILE, 1024)
    vshape = jax.ShapeDtypeStruct((1, 1024), F32)
    return pl.pallas_call(
        body, name=name, grid=(L // ROW_TILE,),
        in_specs=[row, _layer_spec((1024, 1024), layer, 0), row, row, vec, _layer_spec((1024, 1024), layer), vec,
                  vec],
        out_specs=[row, row, row, row, vec, vec, vec],
        out_shape=[jax.ShapeDtypeStruct((L, 1024), F32), jax.ShapeDtypeStruct((L, 1024), F32),
                   jax.ShapeDtypeStruct((L, 1024), BF16), jax.ShapeDtypeStruct((L, 1024), BF16),
                   vshape, vshape, vshape],
    )(dx, w_out, ys, u, d_skip, w_glu, b_glu, gain)


CONV_TILE = 256


def _shift_time(v, d):
    if d == 0:
        return v
    rolled = pltpu.roll(v, d, 0)
    row = lax.broadcasted_iota(jnp.int32, v.shape, 0)
    return jnp.where(row < d, 0.0, rolled)


def _unshift_time(v, d):
    if d == 0:
        return v
    n = v.shape[0]
    rolled = pltpu.roll(v, n - d, 0)
    row = lax.broadcasted_iota(jnp.int32, v.shape, 0)
    return jnp.where(row >= n - d, 0.0, rolled)


def _ssd_conv(xbc, w, b, name):
    L = xbc.shape[0]

    def body(x_ref, w_ref, b_ref, o_ref):
        xv = x_ref[...]
        pre = jnp.broadcast_to(b_ref[...], xv.shape)
        for k in range(SSD_CONV):
            pre = pre + w_ref[k:k + 1, :] * _shift_time(xv, SSD_CONV - 1 - k)
        o_ref[...] = _silu(pre)

    col = pl.BlockSpec((L, CONV_TILE), lambda j: (0, j))
    return pl.pallas_call(
        body, name=name, grid=(SSD_CONV_DIM // CONV_TILE,),
        in_specs=[col, pl.BlockSpec((8, CONV_TILE), lambda j: (0, j)), pl.BlockSpec((1, CONV_TILE), lambda j: (0, j))],
        out_specs=col, out_shape=jax.ShapeDtypeStruct((L, SSD_CONV_DIM), F32),
    )(xbc, w, b)


def _ssd_conv_bwd(dxc, xbc, w, b, name):
    L = xbc.shape[0]

    def body(d_ref, x_ref, w_ref, b_ref, dx_ref, dw_ref, db_ref):
        xv = x_ref[...]
        shifted = [_shift_time(xv, SSD_CONV - 1 - k) for k in range(SSD_CONV)]
        pre = jnp.broadcast_to(b_ref[...], xv.shape)
        for k in range(SSD_CONV):
            pre = pre + w_ref[k:k + 1, :] * shifted[k]
        dpre = d_ref[...] * _dsilu(pre)
        dx = jnp.zeros_like(xv)
        rows = []
        for k in range(SSD_CONV):
            dx = dx + w_ref[k:k + 1, :] * _unshift_time(dpre, SSD_CONV - 1 - k)
            rows.append(jnp.sum(dpre * shifted[k], axis=0, keepdims=True))
        dx_ref[...] = dx
        dw_ref[...] = jnp.concatenate(rows + [jnp.zeros((8 - SSD_CONV, CONV_TILE), F32)], axis=0)
        db_ref[...] = jnp.sum(dpre, axis=0, keepdims=True)

    col = pl.BlockSpec((L, CONV_TILE), lambda j: (0, j))
    w_spec = pl.BlockSpec((8, CONV_TILE), lambda j: (0, j))
    b_spec = pl.BlockSpec((1, CONV_TILE), lambda j: (0, j))
    return pl.pallas_call(
        body, name=name, grid=(SSD_CONV_DIM // CONV_TILE,),
        in_specs=[col, col, w_spec, b_spec], out_specs=[col, w_spec, b_spec],
        out_shape=[jax.ShapeDtypeStruct((L, SSD_CONV_DIM), F32), jax.ShapeDtypeStruct((8, SSD_CONV_DIM), F32),
                   jax.ShapeDtypeStruct((1, SSD_CONV_DIM), F32)],
    )(dxc, xbc, w, b)


def _tri(lower):
    r = lax.broadcasted_iota(jnp.int32, (SSD_CHUNK, SSD_CHUNK), 0)
    c = lax.broadcasted_iota(jnp.int32, (SSD_CHUNK, SSD_CHUNK), 1)
    return (r >= c) if lower else (r <= c)


def _ssd_chunk_common(dt_ref, bias_ref, alog_ref):
    pre = dt_ref[...] + bias_ref[0]
    dtp = _softplus(pre)
    a_neg = -jnp.exp(alog_ref[0])
    dta = dtp * a_neg
    ltri = _tri(True).astype(F32)
    acum = jnp.dot(ltri, dta, precision=HIGHEST, preferred_element_type=F32)
    return pre, dtp, a_neg, dta, acum


GROUP_W = SSD_GROUP_HEADS * SSD_HEAD_DIM


def _head_expander():
    r = lax.broadcasted_iota(jnp.int32, (LANES, GROUP_W), 0)
    c = lax.broadcasted_iota(jnp.int32, (LANES, GROUP_W), 1)
    return (c // SSD_HEAD_DIM == r).astype(F32)


def _dot_exact(a, b):
    return jnp.dot(a, b, precision=HIGHEST, preferred_element_type=F32)


def _decay_mask(acum_all, acum_t, h, lower):
    seg = acum_all[:, h:h + 1] - acum_t[h:h + 1, :]
    return jnp.where(lower, jnp.exp(jnp.minimum(seg, 0.0)), 0.0)


def _ssd_scan(xc, dt, dt_bias, a_log, d_wide, expand, expand_t, name):
    L = xc.shape[0]
    nc = L // SSD_CHUNK

    def body(x_ref, b_ref, c_ref, dt_ref, bias_ref, alog_ref, d_ref, e_ref, et_ref, y_ref, sp_ref, s_ref, xdt_ref):
        @pl.when(pl.program_id(1) == 0)
        def _():
            s_ref[...] = jnp.zeros_like(s_ref)

        _, dtp_all, _, _, acum_all = _ssd_chunk_common(dt_ref, bias_ref, alog_ref)
        acum_t = acum_all.T
        e = e_ref[...]
        acum_e = _dot_exact(acum_all, e)
        alast_e = acum_e[SSD_CHUNK - 1:SSD_CHUNK, :]
        x = x_ref[...]
        xdt = x * _dot_exact(dtp_all, e)
        xdt_ref[...] = xdt.astype(BF16)
        bm = b_ref[...].astype(BF16)
        cm = c_ref[...].astype(BF16)
        cb = _dot_nt(cm, bm)
        lower = _tri(True)
        sp = s_ref[...]
        for h in range(SSD_GROUP_HEADS):
            cols = slice(h * SSD_HEAD_DIM, (h + 1) * SSD_HEAD_DIM)
            lm = _decay_mask(acum_all, acum_t, h, lower)
            y_ref[:, cols] = _dot((cb * lm).astype(BF16), xdt_ref[:, cols])
        y_ref[...] += _dot_nt(cm, sp.astype(BF16)) * jnp.exp(acum_e) + d_ref[0] * x
        wgt = xdt * jnp.exp(alast_e - acum_e)
        ealast = jnp.exp(_dot_exact(et_ref[...], acum_t)[:, SSD_CHUNK - 1:SSD_CHUNK])
        sp_ref[0, 0] = sp
        s_ref[...] = ealast * sp + _dot_tn(wgt.astype(BF16), bm)

    par = lambda n: pl.BlockSpec((1, 1, n), lambda g, c: (g, 0, 0))
    return pl.pallas_call(
        body, name=name, grid=(SSD_GROUPS, nc),
        in_specs=[pl.BlockSpec((SSD_CHUNK, GROUP_W), lambda g, c: (c, g)),
                  pl.BlockSpec((SSD_CHUNK, SSD_STATE), lambda g, c: (c, 8 + g)),
                  pl.BlockSpec((SSD_CHUNK, SSD_STATE), lambda g, c: (c, 10 + g)),
                  pl.BlockSpec((SSD_CHUNK, LANES), lambda g, c: (c, g)),
                  par(LANES), par(LANES), par(GROUP_W), _full_spec((LANES, GROUP_W)), _full_spec((GROUP_W, LANES))],
        out_specs=[pl.BlockSpec((SSD_CHUNK, GROUP_W), lambda g, c: (c, g)),
                   pl.BlockSpec((1, 1, GROUP_W, SSD_STATE), lambda g, c: (c, g, 0, 0))],
        out_shape=[jax.ShapeDtypeStruct((L, SSD_WIDTH), F32),
                   jax.ShapeDtypeStruct((nc, SSD_GROUPS, GROUP_W, SSD_STATE), F32)],
        scratch_shapes=[pltpu.VMEM((GROUP_W, SSD_STATE), F32), pltpu.VMEM((SSD_CHUNK, GROUP_W), BF16)],
    )(xc, xc, xc, dt, dt_bias, a_log, d_wide, expand, expand_t)


def _ssd_scan_bwd(dy, xc, dt, sprev, dt_bias, a_log, d_wide, expand, expand_t, name):
    L = xc.shape[0]
    nc = L // SSD_CHUNK

    def body(dy_ref, x_ref, b_ref, c_ref, dt_ref, sp_ref, bias_ref, alog_ref, d_ref, e_ref, et_ref,
             dx_ref, db_ref, dc_ref, ddt_ref, dbias_ref, dalog_ref, dd_ref, ds_ref, xdt_ref, dyb_ref):
        first = pl.program_id(1) == 0

        @pl.when(first)
        def _():
            ds_ref[...] = jnp.zeros_like(ds_ref)

        pre, dtp_all, a_neg, _, acum_all = _ssd_chunk_common(dt_ref, bias_ref, alog_ref)
        acum_t = acum_all.T
        e = e_ref[...]
        et = et_ref[...]
        acum_e = _dot_exact(acum_all, e)
        dtp_e = _dot_exact(dtp_all, e)
        alast_e = acum_e[SSD_CHUNK - 1:SSD_CHUNK, :]
        dstate_e = jnp.exp(alast_e - acum_e)
        x = x_ref[...]
        dy = dy_ref[...]
        xdt = x * dtp_e
        xdt_ref[...] = xdt.astype(BF16)
        dyb_ref[...] = dy.astype(BF16)
        bm = b_ref[...].astype(BF16)
        cm = c_ref[...].astype(BF16)
        cb = _dot_nt(cm, bm)
        sp = sp_ref[0, 0]
        spb = sp.astype(BF16)
        dsn = ds_ref[...]
        dsb = dsn.astype(BF16)
        z = _dot_nt(cm, spb)
        dz = dy * jnp.exp(acum_e)
        dzb = dz.astype(BF16)
        dc_acc = _dot(dzb, spb)
        ealast = jnp.exp(_dot_exact(et, acum_t)[:, SSD_CHUNK - 1:SSD_CHUNK])
        ds_ref[...] = _dot_tn(dzb, cm) + ealast * dsn
        dw = _dot_nt(bm, dsb)
        wgt = xdt * dstate_e
        db_acc = _dot(wgt.astype(BF16), dsb)
        lower = _tri(True)
        lane = lax.broadcasted_iota(jnp.int32, (SSD_CHUNK, LANES), 1)
        row = lax.broadcasted_iota(jnp.int32, (SSD_CHUNK, LANES), 0)
        dcb = jnp.zeros((SSD_CHUNK, SSD_CHUNK), F32)
        dacum_all = jnp.zeros((SSD_CHUNK, LANES), F32)
        dacum_cols = jnp.zeros((SSD_CHUNK, LANES), F32)
        for h in range(SSD_GROUP_HEADS):
            cols = slice(h * SSD_HEAD_DIM, (h + 1) * SSD_HEAD_DIM)
            lm = _decay_mask(acum_all, acum_t, h, lower)
            dm = _dot_nt(dyb_ref[:, cols], xdt_ref[:, cols])
            dx_ref[:, cols] = _dot_tn((cb * lm).astype(BF16), dyb_ref[:, cols])
            dm_lm = dm * lm
            dcb = dcb + dm_lm
            q = dm_lm * cb
            dacum_all = jnp.where(lane == h, jnp.sum(q, axis=1, keepdims=True), dacum_all)
            dacum_cols = jnp.where(row == h, jnp.sum(q, axis=0, keepdims=True), dacum_cols)
        dxdt = dx_ref[...] + dw * dstate_e
        sums = _dot_exact(jnp.concatenate([dz * z, dw * wgt, dxdt * x, dy * x], axis=0), et)
        dacum_off = sums[0:SSD_CHUNK]
        dds_ds = sums[SSD_CHUNK:2 * SSD_CHUNK]
        ddtp_x = sums[2 * SSD_CHUNK:3 * SSD_CHUNK]
        dd_part = sums[3 * SSD_CHUNK:4 * SSD_CHUNK]
        ds_s = jnp.sum(_dot_exact(e, dsn * sp).T, axis=0, keepdims=True)
        dalast = ds_s * jnp.exp(acum_all[SSD_CHUNK - 1:SSD_CHUNK, :]) + jnp.sum(dds_ds, axis=0, keepdims=True)
        dacum_all = dacum_all - dacum_cols.T + dacum_off - dds_ds + jnp.where(row == SSD_CHUNK - 1, dalast, 0.0)
        dx_ref[...] = d_ref[0] * dy + dxdt * dtp_e
        dcbb = dcb.astype(BF16)
        dc_ref[...] = dc_acc + _dot(dcbb, bm)
        db_ref[...] = db_acc + _dot_tn(dcbb, cm)
        utri = _tri(False).astype(F32)
        ddta = _dot_exact(utri, dacum_all)
        ddt = (ddtp_x + ddta * a_neg) * _sigmoid(pre)
        ddt_ref[...] = ddt
        _acc_rows(dbias_ref, jnp.sum(ddt, axis=0, keepdims=True)[None], first)
        _acc_rows(dalog_ref, (jnp.sum(ddta * dtp_all, axis=0, keepdims=True) * a_neg)[None], first)
        _acc_rows(dd_ref, jnp.sum(dd_part, axis=0, keepdims=True)[None], first)

    rev = lambda c: nc - 1 - c
    par = lambda n: pl.BlockSpec((1, 1, n), lambda g, c: (g, 0, 0))
    pshape = jax.ShapeDtypeStruct((SSD_GROUPS, 1, LANES), F32)
    return pl.pallas_call(
        body, name=name, grid=(SSD_GROUPS, nc),
        in_specs=[pl.BlockSpec((SSD_CHUNK, GROUP_W), lambda g, c: (rev(c), g)),
                  pl.BlockSpec((SSD_CHUNK, GROUP_W), lambda g, c: (rev(c), g)),
                  pl.BlockSpec((SSD_CHUNK, SSD_STATE), lambda g, c: (rev(c), 8 + g)),
                  pl.BlockSpec((SSD_CHUNK, SSD_STATE), lambda g, c: (rev(c), 10 + g)),
                  pl.BlockSpec((SSD_CHUNK, LANES), lambda g, c: (rev(c), g)),
                  pl.BlockSpec((1, 1, GROUP_W, SSD_STATE), lambda g, c: (rev(c), g, 0, 0)),
                  par(LANES), par(LANES), par(GROUP_W), _full_spec((LANES, GROUP_W)), _full_spec((GROUP_W, LANES))],
        out_specs=[pl.BlockSpec((SSD_CHUNK, GROUP_W), lambda g, c: (rev(c), g)),
                   pl.BlockSpec((SSD_CHUNK, SSD_STATE), lambda g, c: (rev(c), g)),
                   pl.BlockSpec((SSD_CHUNK, SSD_STATE), lambda g, c: (rev(c), g)),
                   pl.BlockSpec((SSD_CHUNK, LANES), lambda g, c: (rev(c), g)),
                   par(LANES), par(LANES), par(LANES)],
        out_shape=[jax.ShapeDtypeStruct((L, SSD_WIDTH), F32), jax.ShapeDtypeStruct((L, 256), F32),
                   jax.ShapeDtypeStruct((L, 256), F32), jax.ShapeDtypeStruct((L, 256), F32),
                   pshape, pshape, pshape],
        scratch_shapes=[pltpu.VMEM((GROUP_W, SSD_STATE), F32), pltpu.VMEM((SSD_CHUNK, GROUP_W), BF16),
                        pltpu.VMEM((SSD_CHUNK, GROUP_W), BF16)],
    )(dy, xc, xc, xc, dt, sprev, dt_bias, a_log, d_wide, expand, expand_t)


def _ssd_post(y, z, gain, name):
    L = y.shape[0]

    def body(y_ref, z_ref, g_ref, o_ref):
        ob = y_ref[...] * _silu(z_ref[...])
        o_ref[...] = (ob * _rstd(ob) * g_ref[...]).astype(BF16)

    row = _row_spec(ROW_TILE, 1024)
    return pl.pallas_call(body, name=name, grid=(L // ROW_TILE,), in_specs=[row, row, _full_spec((1, 1024))],
                          out_specs=row, out_shape=jax.ShapeDtypeStruct((L, 1024), BF16))(y, z, gain)


def _ssd_post_bwd(dx, w_out, y, z, gain, layer, name):
    L = y.shape[0]

    def body(dx_ref, wo_ref, y_ref, z_ref, g_ref, dy_ref, dz_ref, dgain_ref):
        first = pl.program_id(0) == 0
        yv = y_ref[...]
        zv = z_ref[...]
        sz = _silu(zv)
        ob = yv * sz
        dyb = _dot_nt(dx_ref[...], wo_ref[...])
        dob, dgain = _rms_bwd(ob, _rstd(ob), g_ref[...], dyb)
        dy_ref[...] = dob * sz
        dz_ref[...] = dob * yv * _dsilu(zv)
        _acc_rows(dgain_ref, dgain, first)

    row = _row_spec(ROW_TILE, 1024)
    vec = _full_spec((1, 1024))
    return pl.pallas_call(
        body, name=name, grid=(L // ROW_TILE,),
        in_specs=[row, _layer_spec((1024, 1024), layer, 1), row, row, vec],
        out_specs=[row, row, vec],
        out_shape=[jax.ShapeDtypeStruct((L, 1024), F32), jax.ShapeDtypeStruct((L, 1024), F32),
                   jax.ShapeDtypeStruct((1, 1024), F32)],
    )(dx, w_out, y, z, gain)


def _out_proj(x, ya, yb, w_out, layer, name):
    L = x.shape[0]

    def body(x_ref, ya_ref, yb_ref, w_ref, o_ref):
        o_ref[...] = x_ref[...] + _dot(ya_ref[...], w_ref[:1024, :]) + _dot(yb_ref[...], w_ref[1024:, :])

    row = _row_spec(ROW_TILE, 1024)
    return pl.pallas_call(body, name=name, grid=(L // ROW_TILE,),
                          in_specs=[row, row, row, _layer_spec((2048, 1024), layer)],
                          out_specs=row, out_shape=jax.ShapeDtypeStruct((L, D_MODEL), F32))(x, ya, yb, w_out)


def _ffn(x, gain, w_gate, w_up, w_down, layer, name):
    L = x.shape[0]

    def body(x_ref, g_ref, wg_ref, wu_ref, wd_ref, o_ref, gt_ref, up_ref):
        xv = x_ref[...]
        h = (xv * _rstd(xv) * g_ref[...]).astype(BF16)
        gt = _dot_nt(h, wg_ref[...])
        up = _dot_nt(h, wu_ref[...])
        gt_ref[...] = gt
        up_ref[...] = up
        o_ref[...] = xv + _dot((_silu(gt) * up).astype(BF16), wd_ref[...])

    row = _row_spec(ROW_TILE, D_MODEL)
    hid = _row_spec(ROW_TILE, FFN)
    return pl.pallas_call(
        body, name=name, grid=(L // ROW_TILE,),
        in_specs=[row, _full_spec((1, D_MODEL)), _layer_spec((FFN, D_MODEL), layer),
                  _layer_spec((FFN, D_MODEL), layer), _layer_spec((FFN, D_MODEL), layer)],
        out_specs=[row, hid, hid],
        out_shape=[jax.ShapeDtypeStruct((L, D_MODEL), F32), jax.ShapeDtypeStruct((L, FFN), F32),
                   jax.ShapeDtypeStruct((L, FFN), F32)],
    )(x, gain, w_gate, w_up, w_down)


def _ffn_bwd(dx2, x1, gt, up, gain, w_gate, w_up, w_down, layer, name):
    L = x1.shape[0]

    def body(d_ref, x_ref, gt_ref, up_ref, g_ref, wg_ref, wu_ref, wd_ref,
             dx_ref, dxb_ref, h_ref, act_ref, dgt_ref, dup_ref, dgain_ref):
        first = pl.program_id(0) == 0
        dv = d_ref[...]
        xv = x_ref[...]
        r = _rstd(xv)
        h_ref[...] = (xv * r * g_ref[...]).astype(BF16)
        gtv = gt_ref[...]
        upv = up_ref[...]
        sg = _silu(gtv)
        act_ref[...] = (sg * upv).astype(BF16)
        dact = _dot_nt(dv.astype(BF16), wd_ref[...])
        dgt = (dact * upv * _dsilu(gtv)).astype(BF16)
        dup = (dact * sg).astype(BF16)
        dgt_ref[...] = dgt
        dup_ref[...] = dup
        dh = _dot(dgt, wg_ref[...]) + _dot(dup, wu_ref[...])
        dxn, dgain = _rms_bwd(xv, r, g_ref[...], dh)
        dx = dv + dxn
        dx_ref[...] = dx
        dxb_ref[...] = dx.astype(BF16)
        _acc_rows(dgain_ref, dgain, first)

    row = _row_spec(ROW_TILE, D_MODEL)
    hid = _row_spec(ROW_TILE, FFN)
    vec = _full_spec((1, D_MODEL))
    return pl.pallas_call(
        body, name=name, grid=(L // ROW_TILE,),
        in_specs=[row, row, hid, hid, vec, _layer_spec((FFN, D_MODEL), layer), _layer_spec((FFN, D_MODEL), layer),
                  _layer_spec((FFN, D_MODEL), layer)],
        out_specs=[row, row, row, hid, hid, hid, vec],
        out_shape=[jax.ShapeDtypeStruct((L, D_MODEL), F32), jax.ShapeDtypeStruct((L, D_MODEL), BF16),
                   jax.ShapeDtypeStruct((L, D_MODEL), BF16),
                   jax.ShapeDtypeStruct((L, FFN), BF16), jax.ShapeDtypeStruct((L, FFN), BF16),
                   jax.ShapeDtypeStruct((L, FFN), BF16), jax.ShapeDtypeStruct((1, D_MODEL), F32)],
    )(dx2, x1, gt, up, gain, w_gate, w_up, w_down)


def _inproj_bwd(dx1, x0, du_skip, du_scan, dz, dxbc, ddt, gain, w_pad, layer, name):
    L = x0.shape[0]

    def body(d_ref, x_ref, dus_ref, duc_ref, dz_ref, dxbc_ref, ddt_ref, g_ref, w_ref,
             dx_ref, dxb_ref, dp_ref, dgain_ref):
        first = pl.program_id(0) == 0
        xv = x_ref[...]
        dp = jnp.concatenate([dus_ref[...] + duc_ref[...], dz_ref[...], dxbc_ref[...], ddt_ref[...]],
                             axis=1).astype(BF16)
        dp_ref[...] = dp
        dh = _dot_nt(dp, w_ref[...])
        dxn, dgain = _rms_bwd(xv, _rstd(xv), g_ref[...], dh)
        dx = d_ref[...] + dxn
        dx_ref[...] = dx
        dxb_ref[...] = dx.astype(BF16)
        _acc_rows(dgain_ref, dgain, first)

    row = _row_spec(ROW_TILE, D_MODEL)
    vec = _full_spec((1, D_MODEL))
    return pl.pallas_call(
        body, name=name, grid=(L // ROW_TILE,),
        in_specs=[row, row, row, row, row, _row_spec(ROW_TILE, SSD_CONV_DIM), _row_spec(ROW_TILE, 256), vec,
                  _layer_spec((D_MODEL, IN_PAD), layer)],
        out_specs=[row, row, _row_spec(ROW_TILE, IN_PAD), vec],
        out_shape=[jax.ShapeDtypeStruct((L, D_MODEL), F32), jax.ShapeDtypeStruct((L, D_MODEL), BF16),
                   jax.ShapeDtypeStruct((L, IN_PAD), BF16), jax.ShapeDtypeStruct((1, D_MODEL), F32)],
    )(dx1, x0, du_skip, du_scan, dz, dxbc, ddt, gain, w_pad)


def _final_loss(x, gain, target, name):
    L = x.shape[0]

    def body(x_ref, g_ref, t_ref, loss_ref, dx_ref, dxb_ref, dgain_ref):
        first = pl.program_id(0) == 0
        xv = x_ref[...]
        r = _rstd(xv)
        err = xv * r * g_ref[...] - t_ref[...]
        part = 0.5 * jnp.sum(jnp.mean(err * err, axis=-1, keepdims=True), axis=0, keepdims=True)
        dx, dgain = _rms_bwd(xv, r, g_ref[...], err * (1.0 / D_MODEL))
        dx_ref[...] = dx
        dxb_ref[...] = dx.astype(BF16)
        _acc_rows(loss_ref, jnp.broadcast_to(part, (1, LANES)), first)
        _acc_rows(dgain_ref, dgain, first)

    row = _row_spec(ROW_TILE, D_MODEL)
    vec = _full_spec((1, D_MODEL))
    return pl.pallas_call(
        body, name=name, grid=(L // ROW_TILE,), in_specs=[row, vec, row],
        out_specs=[_full_spec((1, LANES)), row, row, vec],
        out_shape=[jax.ShapeDtypeStruct((1, LANES), F32), jax.ShapeDtypeStruct((L, D_MODEL), F32),
                   jax.ShapeDtypeStruct((L, D_MODEL), BF16), jax.ShapeDtypeStruct((1, D_MODEL), F32)],
    )(x, gain, target)


def _to_segments(a):
    L, n = a.shape
    return a.reshape(SEGS, L // SEGS, n).transpose(1, 0, 2).reshape(L, n)


def _from_segments(a):
    L, n = a.shape
    return a.reshape(L // SEGS, SEGS, n).transpose(1, 0, 2).reshape(L, n)


def _block_diag_in_to_state(m):
    m = m.reshape(S5_TILES, S5_TILE_GROUPS, S5_GROUP, S5_STATE)
    eye = jnp.eye(S5_TILE_GROUPS, dtype=m.dtype)
    out = m[:, :, :, None, :] * eye[None, :, None, :, None]
    return out.reshape(S5_TILES, S5_TILE_IN, S5_TILE_ST)


def _block_diag_extract(d):
    d = d.reshape(S5_TILES, S5_TILE_GROUPS, S5_GROUP, S5_TILE_GROUPS, S5_STATE)
    d = jnp.stack([d[:, a, :, a, :] for a in range(S5_TILE_GROUPS)], axis=1)
    return d.reshape(S5_GROUPS, S5_GROUP, S5_STATE)


def _pad_in_proj(w):
    z = jnp.zeros(w.shape[:-1] + (LANES - SSD_GROUP_HEADS,), w.dtype)
    return jnp.concatenate([w[..., :IN_MAIN + 8], z, w[..., IN_MAIN + 8:], z], axis=-1)


def _unpad_in_proj(w):
    return jnp.concatenate([w[..., :IN_MAIN + 8], w[..., IN_MAIN + LANES:IN_MAIN + LANES + 8]], axis=-1)


def _pad_heads(v):
    v = v.reshape(SSD_GROUPS, 1, SSD_GROUP_HEADS)
    return jnp.pad(v, ((0, 0), (0, 0), (0, LANES - SSD_GROUP_HEADS)))


def _unpad_heads(v):
    return v[:, 0, :SSD_GROUP_HEADS].reshape(SSD_HEADS)


def _layer_forward(x0, p, big, i):
    tag = "l%d_" % i
    ls = p["s5_log_step"].reshape(S5_GROUPS, 1)
    b_hgp = (p["s5_b_re"].transpose(2, 0, 1), p["s5_b_im"].transpose(2, 0, 1))
    are, aim, bbre, bbim = _s5_prep(p["s5_lam_re"], p["s5_lam_im"], ls, b_hgp[0], b_hgp[1], tag + "s5_prep")
    bre_ghp = bbre.transpose(1, 0, 2)
    bim_ghp = bbim.transpose(1, 0, 2)
    bre_bd = _block_diag_in_to_state(bre_ghp).astype(BF16)
    bim_bd = _block_diag_in_to_state(bim_ghp).astype(BF16)
    cret_bd = _block_diag_in_to_state(p["s5_c_re"]).astype(BF16)
    cimt_bd = _block_diag_in_to_state(p["s5_c_im"]).astype(BF16)
    s5mats = dict(bre_bd=bre_bd, bim_bd=bim_bd, cret_bd=cret_bd, cimt_bd=cimt_bd,
                  bret_bd=bre_bd.transpose(0, 2, 1), bimt_bd=bim_bd.transpose(0, 2, 1),
                  cre_bd=cret_bd.transpose(0, 2, 1), cim_bd=cimt_bd.transpose(0, 2, 1),
                  are=are.reshape(S5_TILES, 1, S5_TILE_ST), aim=aim.reshape(S5_TILES, 1, S5_TILE_ST))

    u, z, xbc, dt, h1 = _rms_inproj(x0, p["norm_mix"].reshape(1, -1), big["w_in"], i, tag + "rms_inproj")
    u_perm = _to_segments(u)
    ys_perm, xr, xi = _s5_scan(u_perm, bre_bd, bim_bd, s5mats["cre_bd"], s5mats["cim_bd"],
                               s5mats["are"], s5mats["aim"], tag + "s5_scan")
    ys = _from_segments(ys_perm)
    ya = _s5_post(ys, u, p["s5_d"].reshape(1, -1), big["s5_w_glu"], p["s5_b_glu"].reshape(1, -1),
                  p["s5_norm"].reshape(1, -1), i, tag + "s5_post")

    conv_w = jnp.pad(p["ssd_conv_w"], ((0, 8 - SSD_CONV), (0, 0)))
    conv_b = p["ssd_conv_b"].reshape(1, -1)
    xc = _ssd_conv(xbc, conv_w, conv_b, tag + "ssd_conv")
    expand = _head_expander()
    heads = dict(dt_bias=_pad_heads(p["ssd_dt_bias"]), a_log=_pad_heads(p["ssd_a_log"]),
                 d=jnp.repeat(p["ssd_d"], SSD_HEAD_DIM).reshape(SSD_GROUPS, 1, GROUP_W),
                 expand=expand, expand_t=expand.T)
    y, sprev = _ssd_scan(xc, dt, heads["dt_bias"], heads["a_log"], heads["d"], expand, heads["expand_t"],
                         tag + "ssd_scan")
    yb = _ssd_post(y, z, p["ssd_norm"].reshape(1, -1), tag + "ssd_post")

    x1 = _out_proj(x0, ya, yb, big["w_out"], i, tag + "out_proj")
    x2, gt, up = _ffn(x1, p["norm_ffn"].reshape(1, -1), big["w_gate"], big["w_up"], big["w_down"], i, tag + "ffn")
    saved = dict(x0=x0, h1=h1, u=u, u_perm=u_perm, z=z, xbc=xbc, dt=dt, xr=xr, xi=xi, ys=ys, ya=ya, xc=xc, y=y,
                 sprev=sprev, yb=yb, x1=x1, gt=gt, up=up, s5mats=s5mats, heads=heads, conv_w=conv_w,
                 conv_b=conv_b, ls=ls, b_hgp=b_hgp)
    return x2, saved


def _layer_backward(dx2, dx2b, p, big, s, i):
    tag = "l%d_" % i
    g = {}
    dx1, dx1b, h2, act, dgt, dup, dgain = _ffn_bwd(dx2, s["x1"], s["gt"], s["up"], p["norm_ffn"].reshape(1, -1),
                                                  big["w_gate"], big["w_up"], big["w_down"], i, tag + "ffn_bwd")
    g["norm_ffn"] = dgain[0]
    g["w_down"] = _mm_tn(act, dx2b, tag + "dw_down")
    g["w_gate"] = _mm_tn(dgt, h2, tag + "dw_gate")
    g["w_up"] = _mm_tn(dup, h2, tag + "dw_up")
    g["w_out"] = _mm_tn(jnp.concatenate([s["ya"], s["yb"]], axis=1), dx1b, tag + "dw_out")

    dys, du_skip, gelu_b, dq_b, dgain, dd, dbg = _s5_post_bwd(
        dx1b, big["w_out"], s["ys"], s["u"], p["s5_d"].reshape(1, -1), big["s5_w_glu"],
        p["s5_b_glu"].reshape(1, -1), p["s5_norm"].reshape(1, -1), i, tag + "s5_post_bwd")
    g["s5_norm"] = dgain[0]
    g["s5_d"] = dd[0]
    g["s5_b_glu"] = dbg[0]
    g["s5_w_glu"] = _mm_tn(gelu_b, dq_b, tag + "dw_glu")
    m = s["s5mats"]
    du_perm, dar, dai, dcre_d, dcim_d, dbre_d, dbim_d = _s5_scan_bwd(
        _to_segments(dys), s["u_perm"], s["xr"], s["xi"], m["bret_bd"], m["bimt_bd"], m["cret_bd"], m["cimt_bd"],
        m["are"], m["aim"], tag + "s5_scan_bwd")
    du_scan = _from_segments(du_perm)
    g["s5_c_re"] = _block_diag_extract(dcre_d)
    g["s5_c_im"] = _block_diag_extract(dcim_d)
    dbbre = _block_diag_extract(dbre_d).transpose(1, 0, 2)
    dbbim = _block_diag_extract(dbim_d).transpose(1, 0, 2)
    dlr, dli, dls, dbre, dbim = _s5_prep_bwd(
        p["s5_lam_re"], p["s5_lam_im"], s["ls"], s["b_hgp"][0], s["b_hgp"][1],
        dar.reshape(S5_GROUPS, S5_STATE), dai.reshape(S5_GROUPS, S5_STATE), dbbre, dbbim, tag + "s5_prep_bwd")
    g["s5_lam_re"] = dlr
    g["s5_lam_im"] = dli
    g["s5_log_step"] = dls[:, 0]
    g["s5_b_re"] = dbre
    g["s5_b_im"] = dbim

    dy, dz, dgain = _ssd_post_bwd(dx1b, big["w_out"], s["y"], s["z"], p["ssd_norm"].reshape(1, -1), i,
                                  tag + "ssd_post_bwd")
    g["ssd_norm"] = dgain[0]
    hd = s["heads"]
    dxs, dbm, dcm, ddt, dbias, dalog, dd = _ssd_scan_bwd(dy, s["xc"], s["dt"], s["sprev"], hd["dt_bias"],
                                                       hd["a_log"], hd["d"], hd["expand"], hd["expand_t"],
                                                       tag + "ssd_scan_bwd")
    g["ssd_dt_bias"] = _unpad_heads(dbias)
    g["ssd_a_log"] = _unpad_heads(dalog)
    g["ssd_d"] = _unpad_heads(dd)
    dxc = jnp.concatenate([dxs, dbm, dcm], axis=1)
    dxbc, dcw, dcb = _ssd_conv_bwd(dxc, s["xbc"], s["conv_w"], s["conv_b"], tag + "ssd_conv_bwd")
    g["ssd_conv_w"] = dcw[:SSD_CONV]
    g["ssd_conv_b"] = dcb[0]

    dx0, dx0b, dproj, dgain = _inproj_bwd(dx1, s["x0"], du_skip, du_scan, dz, dxbc, ddt, p["norm_mix"].reshape(1, -1),
                                          big["w_in"], i, tag + "inproj_bwd")
    g["norm_mix"] = dgain[0]
    g["w_in"] = _mm_tn(s["h1"], dproj, tag + "dw_in")
    return dx0, dx0b, g


BIG = ("w_in", "s5_w_glu", "w_out", "w_gate", "w_up", "w_down")
COL_SHARDED = ("w_in",)
T_STORED = ("w_gate", "w_up")
LAYER_SMALL = ("norm_mix", "s5_lam_re", "s5_lam_im", "s5_log_step", "s5_b_re", "s5_b_im", "s5_c_re", "s5_c_im",
               "s5_d", "s5_b_glu", "s5_norm", "ssd_conv_w", "ssd_conv_b", "ssd_dt_bias", "ssd_a_log", "ssd_d",
               "ssd_norm", "norm_ffn")
WEIGHTS = ("norm_mix", "w_in", "s5_lam_re", "s5_lam_im", "s5_log_step", "s5_b_re", "s5_b_im", "s5_c_re", "s5_c_im",
           "s5_d", "s5_w_glu", "s5_b_glu", "s5_norm", "ssd_conv_w", "ssd_conv_b", "ssd_dt_bias", "ssd_a_log",
           "ssd_d", "ssd_norm", "w_out", "norm_ffn", "w_gate", "w_up", "w_down", "norm_final")


S5_BC = ("s5_b_re", "s5_b_im", "s5_c_re", "s5_c_im")
TINY = tuple(k for k in LAYER_SMALL if k not in S5_BC)


def _local_step(x, target, big, small, norm_final):
    saved = []
    h = x
    for i in range(DEPTH):
        p = {k: v[i] for k, v in small.items()}
        h, s = _layer_forward(h, p, big, i)
        saved.append((p, s))
    loss, dx, dxb, dgf = _final_loss(h, norm_final.reshape(1, -1), target, "final_loss")
    grads = [None] * DEPTH
    for i in reversed(range(DEPTH)):
        p, s = saved[i]
        dx, dxb, grads[i] = _layer_backward(dx, dxb, p, big, s, i)
    by_name = {k: [grads[i][k] for i in range(DEPTH)] for k in BIG + LAYER_SMALL}
    return loss[0, 0], dx, by_name, dgf[0]


def _my_place():
    return lax.axis_index("x"), lax.axis_index("y"), lax.axis_index("c")


def _all_gather8(blocks, name):
    nt = len(blocks)

    def body(*refs):
        ins = refs[:nt]
        outs = refs[nt:2 * nt]
        send_sems, recv_sems, local_sems = refs[2 * nt:]
        x, y, c = _my_place()
        me, sibling = (x, y, c), (x, y, 1 - c)
        chips = [(1 - x, y), (x, 1 - y), (1 - x, 1 - y)]

        def slot(t, place):
            px, py, pc = place
            return outs[t].at[4 * px + 2 * py + pc]

        def copy(t, k, block, to, src=None):
            return pltpu.make_async_remote_copy(
                src_ref=slot(t, block) if src is None else src, dst_ref=slot(t, block),
                send_sem=send_sems.at[t, k], recv_sem=recv_sems.at[t, k], device_id=to, device_id_type=MESH)

        mine = [pltpu.make_async_copy(ins[t], slot(t, me), local_sems.at[t]) for t in range(nt)]
        for cp in mine:
            cp.start()
        first = []
        for t in range(nt):
            first.append(copy(t, 0, me, sibling, src=ins[t]))
            first += [copy(t, 1 + j, me, (*chip, c), src=ins[t]) for j, chip in enumerate(chips)]
        for cp in first:
            cp.start()
        passed = []
        for j, chip in enumerate(chips):
            for t in range(nt):
                copy(t, 1 + j, (*chip, c), me).wait_recv()
                fwd = copy(t, 4 + j, (*chip, c), sibling)
                fwd.start()
                passed.append(fwd)
        for t in range(nt):
            copy(t, 0, sibling, me).wait_recv()
            for j, chip in enumerate(chips):
                copy(t, 4 + j, (*chip, 1 - c), me).wait_recv()
        for cp in first + passed:
            cp.wait_send()
        for cp in mine:
            cp.wait()

    return pl.pallas_call(
        body, name=name, in_specs=[ANY] * nt, out_specs=[ANY] * nt,
        out_shape=[jax.ShapeDtypeStruct((8,) + b.shape, b.dtype) for b in blocks],
        scratch_shapes=[pltpu.SemaphoreType.DMA((nt, 7)), pltpu.SemaphoreType.DMA((nt, 7)),
                        pltpu.SemaphoreType.DMA((nt,))],
    )(*blocks)


def _sibling_swap_other(pairs, name):
    nt = len(pairs)

    def body(*refs):
        ins = refs[:2 * nt]
        outs = refs[2 * nt:3 * nt]
        send_sems, recv_sems = refs[3 * nt:]
        x, y, c = _my_place()

        def copy(t, src):
            return pltpu.make_async_remote_copy(src_ref=src, dst_ref=outs[t], send_sem=send_sems.at[t],
                                                recv_sem=recv_sems.at[t], device_id=(x, y, 1 - c), device_id_type=MESH)

        for t in range(nt):
            @pl.when(c == 0)
            def _():
                copy(t, ins[2 * t + 1]).start()

            @pl.when(c == 1)
            def _():
                copy(t, ins[2 * t]).start()
        for t in range(nt):
            copy(t, ins[2 * t]).wait()

    flat = [a for pair in pairs for a in pair]
    return pl.pallas_call(
        body, name=name, in_specs=[ANY] * (2 * nt), out_specs=[ANY] * nt,
        out_shape=[jax.ShapeDtypeStruct(a0.shape, a0.dtype) for a0, _ in pairs],
        scratch_shapes=[pltpu.SemaphoreType.DMA((nt,)), pltpu.SemaphoreType.DMA((nt,))],
    )(*flat)


def _sibling_swap(arrs, name):
    nt = len(arrs)

    def body(*refs):
        ins = refs[:nt]
        outs = refs[nt:2 * nt]
        send_sems, recv_sems = refs[2 * nt:]
        x, y, c = _my_place()
        copies = [pltpu.make_async_remote_copy(src_ref=ins[t], dst_ref=outs[t], send_sem=send_sems.at[t],
                                               recv_sem=recv_sems.at[t], device_id=(x, y, 1 - c), device_id_type=MESH)
                  for t in range(nt)]
        for cp in copies:
            cp.start()
        for cp in copies:
            cp.wait()

    return pl.pallas_call(
        body, name=name, in_specs=[ANY] * nt, out_specs=[ANY] * nt,
        out_shape=[jax.ShapeDtypeStruct(a.shape, a.dtype) for a in arrs],
        scratch_shapes=[pltpu.SemaphoreType.DMA((nt,)), pltpu.SemaphoreType.DMA((nt,))],
    )(*arrs)


def _chip_all_to_all(arrs, name):
    nt = len(arrs)

    def body(*refs):
        ins = refs[:nt]
        outs = refs[nt:2 * nt]
        send_sems, recv_sems, local_sems = refs[2 * nt:]
        x, y, c = _my_place()
        mine = 2 * x + y
        chips = [(1 - x, y), (x, 1 - y), (1 - x, 1 - y)]
        local = [pltpu.make_async_copy(ins[t].at[mine], outs[t].at[mine], local_sems.at[t]) for t in range(nt)]
        for cp in local:
            cp.start()
        sends = []
        for t in range(nt):
            for j, (px, py) in enumerate(chips):
                sends.append(pltpu.make_async_remote_copy(
                    src_ref=ins[t].at[2 * px + py], dst_ref=outs[t].at[mine], send_sem=send_sems.at[t, j],
                    recv_sem=recv_sems.at[t, j], device_id=(px, py, c), device_id_type=MESH))
        for cp in sends:
            cp.start()
        for t in range(nt):
            for j, (px, py) in enumerate(chips):
                pltpu.make_async_remote_copy(
                    src_ref=ins[t].at[mine], dst_ref=outs[t].at[2 * px + py], send_sem=send_sems.at[t, j],
                    recv_sem=recv_sems.at[t, j], device_id=(px, py, c), device_id_type=MESH).wait_recv()
        for cp in sends:
            cp.wait_send()
        for cp in local:
            cp.wait()

    return pl.pallas_call(
        body, name=name, in_specs=[ANY] * nt, out_specs=[ANY] * nt,
        out_shape=[jax.ShapeDtypeStruct(a.shape, a.dtype) for a in arrs],
        scratch_shapes=[pltpu.SemaphoreType.DMA((nt, 3)), pltpu.SemaphoreType.DMA((nt, 3)),
                        pltpu.SemaphoreType.DMA((nt,))],
    )(*arrs)


def _as_rows(a):
    return a.reshape(-1, a.shape[-1])


STREAM_VMEM_BYTES = 16 * 1024 * 1024


def _row_tile(rows, cols, n_arrays):
    lanes = -(-cols // LANES) * LANES
    for t in (512, 256, 128, 64, 32, 16, 8):
        if rows % t == 0 and 2 * n_arrays * t * lanes * 4 <= STREAM_VMEM_BYTES:
            return t
    return rows


def _pair_add(a0, a1, recv, name):
    rows, cols = a0.shape
    tile = _row_tile(rows, cols, 4)

    def body(a0_ref, a1_ref, r_ref, o_ref):
        mine = jnp.where(lax.axis_index("c") == 0, a0_ref[...], a1_ref[...])
        o_ref[...] = (mine.astype(F32) + r_ref[...].astype(F32)).astype(o_ref.dtype)

    spec = pl.BlockSpec((tile, cols), lambda i: (i, 0))
    return pl.pallas_call(body, name=name, grid=(rows // tile,), in_specs=[spec] * 3, out_specs=spec,
                          out_shape=jax.ShapeDtypeStruct((rows, cols), a0.dtype))(a0, a1, recv)


def _sum_leading(a, name):
    n, rows, cols = a.shape
    tile = _row_tile(rows, cols, n + 1)

    def body(a_ref, o_ref):
        acc = a_ref[0].astype(F32)
        for k in range(1, n):
            acc = acc + a_ref[k].astype(F32)
        o_ref[...] = acc

    return pl.pallas_call(
        body, name=name, grid=(rows // tile,), in_specs=[pl.BlockSpec((n, tile, cols), lambda i: (0, i, 0))],
        out_specs=pl.BlockSpec((tile, cols), lambda i: (i, 0)),
        out_shape=jax.ShapeDtypeStruct((rows, cols), F32))(a)


def _adamw_math(w, g, m, v):
    mn = ADAM_B1 * m + (1.0 - ADAM_B1) * g
    vn = ADAM_B2 * v + (1.0 - ADAM_B2) * jnp.square(g)
    m_hat = mn / (1.0 - ADAM_B1 ** ADAM_STEP)
    v_hat = vn / (1.0 - ADAM_B2 ** ADAM_STEP)
    delta = -ADAM_LR * (m_hat / (jnp.sqrt(v_hat) + ADAM_EPS) + ADAM_WD * w)
    return delta, mn, vn


def _adamw_layers(w, g_mine, g_sibling, m, v, name):
    depth, rows, cols = w.shape
    tile = _row_tile(rows, cols, 10)

    def body(w_ref, gm_ref, gs_ref, m_ref, v_ref, d_ref, nm_ref, nv_ref, go_ref):
        gv = jnp.where(pl.program_id(0) == lax.axis_index("c"), gm_ref[...], gs_ref[...])
        d_ref[...], nm_ref[...], nv_ref[...] = _adamw_math(w_ref[...], gv, m_ref[...], v_ref[...])
        go_ref[...] = gv

    spec = pl.BlockSpec((None, tile, cols), lambda l, i: (l, i, 0))
    gspec = pl.BlockSpec((tile, cols), lambda l, i: (i, 0))
    shape = jax.ShapeDtypeStruct((depth, rows, cols), F32)
    return pl.pallas_call(body, name=name, grid=(depth, rows // tile), in_specs=[spec, gspec, gspec, spec, spec],
                          out_specs=[spec] * 4, out_shape=[shape] * 4)(w, g_mine, g_sibling, m, v)


def _adamw_rows(w, g, m, v, name):
    depth, rows, cols = w.shape
    tile = _row_tile(rows, cols, 7)

    def body(w_ref, g_ref, m_ref, v_ref, d_ref, nm_ref, nv_ref):
        d_ref[...], nm_ref[...], nv_ref[...] = _adamw_math(w_ref[...], g_ref[...], m_ref[...], v_ref[...])

    spec = pl.BlockSpec((None, tile, cols), lambda l, i: (l, i, 0))
    shape = jax.ShapeDtypeStruct((depth, rows, cols), F32)
    return pl.pallas_call(body, name=name, grid=(depth, rows // tile), in_specs=[spec] * 4, out_specs=[spec] * 3,
                          out_shape=[shape] * 3)(w, g, m, v)


def _adamw_many(ws, gs, ms, vs, name):
    nt = len(ws)

    def body(*refs):
        for t in range(nt):
            w_ref, g_ref, m_ref, v_ref = (refs[k * nt + t] for k in range(4))
            d_ref, nm_ref, nv_ref = (refs[(4 + k) * nt + t] for k in range(3))
            d_ref[...], nm_ref[...], nv_ref[...] = _adamw_math(w_ref[...], g_ref[...], m_ref[...], v_ref[...])

    shapes = [jax.ShapeDtypeStruct(a.shape, F32) for a in ws]
    out = pl.pallas_call(body, name=name, out_shape=shapes * 3)(*ws, *gs, *ms, *vs)
    return out[:nt], out[nt:2 * nt], out[2 * nt:]


TINY_ROWS_MULTIPLE = 128


def _flat_pack(arrs):
    flat = jnp.concatenate([a.reshape(-1) for a in arrs])
    pad = (-flat.shape[0]) % (TINY_ROWS_MULTIPLE * LANES)
    return jnp.pad(flat, (0, pad)).reshape(-1, LANES)


def _flat_unpack(buf, shapes):
    flat = buf.reshape(-1)
    out = []
    off = 0
    for shp in shapes:
        n = math.prod(shp)
        out.append(flat[off:off + n].reshape(shp))
        off += n
    return out


def _to_chunks(a, name):
    if name == "w_in":
        a = _unpad_in_proj(a)
    rows, cols = a.shape
    if name in COL_SHARDED:
        return a.reshape(rows, 4, cols // 4).transpose(1, 0, 2)
    return a.reshape(4, rows // 4, cols)


def _from_chunks(a, name):
    _, depth, r, cc = a.shape
    if name in COL_SHARDED:
        return a.transpose(1, 2, 0, 3).reshape(depth, r, 4 * cc)
    return a.transpose(1, 0, 2, 3).reshape(depth, 4 * r, cc)


def kernel(x, norm_mix, w_in, s5_lam_re, s5_lam_im, s5_log_step, s5_b_re, s5_b_im, s5_c_re, s5_c_im, s5_d, s5_w_glu, s5_b_glu, s5_norm, ssd_conv_w, ssd_conv_b, ssd_dt_bias, ssd_a_log, ssd_d, ssd_norm, w_out, norm_ffn, w_gate, w_up, w_down, norm_final, loss_target, m_norm_mix, m_w_in, m_s5_lam_re, m_s5_lam_im, m_s5_log_step, m_s5_b_re, m_s5_b_im, m_s5_c_re, m_s5_c_im, m_s5_d, m_s5_w_glu, m_s5_b_glu, m_s5_norm, m_ssd_conv_w, m_ssd_conv_b, m_ssd_dt_bias, m_ssd_a_log, m_ssd_d, m_ssd_norm, m_w_out, m_norm_ffn, m_w_gate, m_w_up, m_w_down, m_norm_final, v_norm_mix, v_w_in, v_s5_lam_re, v_s5_lam_im, v_s5_log_step, v_s5_b_re, v_s5_b_im, v_s5_c_re, v_s5_c_im, v_s5_d, v_s5_w_glu, v_s5_b_glu, v_s5_norm, v_ssd_conv_w, v_ssd_conv_b, v_ssd_dt_bias, v_ssd_a_log, v_ssd_d, v_ssd_norm, v_w_out, v_norm_ffn, v_w_gate, v_w_up, v_w_down, v_norm_final):
    args = dict(locals())
    w = {k: args[k] for k in WEIGHTS}
    m = {k: args["m_" + k] for k in WEIGHTS}
    v = {k: args["v_" + k] for k in WEIGHTS}
    cx, cy, cc = _my_place()
    chip = 2 * cx + cy

    stored = lambda k, a: jnp.swapaxes(a, 1, 2) if k in T_STORED else a
    mine = [lax.dynamic_index_in_dim(stored(k, w[k]), cc, 0, keepdims=False).astype(BF16) for k in BIG]
    gathered = _all_gather8(mine + [w["ssd_conv_w"].reshape(DEPTH * SSD_CONV, -1)], "gather_weights")
    big = {}
    for k, a in zip(BIG, gathered):
        big[k] = _from_chunks(a.reshape((4, DEPTH) + a.shape[1:]), k)
    big["w_in"] = _pad_in_proj(big["w_in"])
    conv_rows = gathered[-1]
    conv_full = conv_rows.reshape(4, 2, DEPTH, SSD_CONV, -1)[:, 0].transpose(1, 2, 0, 3).reshape(
        DEPTH, SSD_CONV, SSD_CONV_DIM)
    small = {k: w[k] for k in LAYER_SMALL}
    small["ssd_conv_w"] = conv_full

    loss_part, grad_x, g, g_final = _local_step(x[0], loss_target[0], big, small, w["norm_final"])

    bc_rows = 2 * DEPTH * S5_GROUP * S5_GROUPS
    b_all = jnp.stack([g["s5_b_re"][0], g["s5_b_im"][0], g["s5_b_re"][1], g["s5_b_im"][1]]).reshape(bc_rows, S5_STATE)
    c_all = jnp.stack([g["s5_c_re"][0], g["s5_c_im"][0], g["s5_c_re"][1], g["s5_c_im"][1]]).reshape(bc_rows, S5_STATE)
    pairs = [(g[k][0], g[k][1]) for k in BIG] + [(b_all, c_all)]
    from_sib = _sibling_swap_other(pairs, "grad_swap")
    chunks = []
    for k, (a0, a1), r in zip(BIG + ("s5_bc",), pairs, from_sib):
        part = _pair_add(a0, a1, r, "grad_pair_" + k)
        chunks.append(part.reshape(4, bc_rows // 4, S5_STATE) if k == "s5_bc" else _to_chunks(part, k))
    landed = _chip_all_to_all(chunks, "grad_scatter")
    reduced = [_sum_leading(a, "grad_sum_" + k) for k, a in zip(BIG + ("s5_bc",), landed)]
    from_sib = _sibling_swap(reduced[:-1], "grad_share")
    grads = {}

    tiny_names = TINY + ("norm_final",)
    parts = [jnp.stack(g[k]) for k in TINY] + [g_final, loss_part.reshape(1)]
    shapes = [p.shape for p in parts]
    allparts, bc_eighths = _all_gather8([_flat_pack(parts), reduced[-1]], "gather_small")
    unpacked = _flat_unpack(_sum_leading(allparts, "sum_small"), shapes)
    loss = unpacked[-1][0]
    grads.update(zip(tiny_names, unpacked[:-1]))
    width = SSD_CONV_DIM // 4
    grads["ssd_conv_w"] = lax.dynamic_slice_in_dim(grads["ssd_conv_w"], chip * width, width, axis=2)
    bc = bc_eighths.reshape(4, 2, bc_rows // 4, S5_STATE)
    b_sum = bc[:, 0].reshape(DEPTH, 2, S5_GROUP, S5_GROUPS, S5_STATE)
    c_sum = bc[:, 1].reshape(DEPTH, 2, S5_GROUPS, S5_GROUP, S5_STATE)
    grads["s5_b_re"] = b_sum[:, 0].transpose(0, 2, 3, 1)
    grads["s5_b_im"] = b_sum[:, 1].transpose(0, 2, 3, 1)
    grads["s5_c_re"] = c_sum[:, 0]
    grads["s5_c_im"] = c_sum[:, 1]

    delta, new_m, new_v = {}, {}, {}
    for k, g_mine, g_sib in zip(BIG, reduced, from_sib):
        outs = _adamw_layers(stored(k, w[k]), g_mine, g_sib, stored(k, m[k]), stored(k, v[k]), "adamw_" + k)
        delta[k], new_m[k], new_v[k], grads[k] = (stored(k, a) for a in outs)
    for k in ("s5_b_re", "s5_b_im"):
        shp = w[k].shape
        rows = lambda a: a.reshape(DEPTH, -1, shp[-1])
        d, nm, nv = _adamw_rows(rows(w[k]), rows(grads[k]), rows(m[k]), rows(v[k]), "adamw_" + k)
        delta[k], new_m[k], new_v[k] = d.reshape(shp), nm.reshape(shp), nv.reshape(shp)
    names = tiny_names + ("s5_c_re", "s5_c_im")
    as2d = lambda a: a.reshape(1, -1) if a.ndim == 1 else a
    ds, nms, nvs = _adamw_many([as2d(w[k]) for k in names], [as2d(grads[k]) for k in names],
                               [as2d(m[k]) for k in names], [as2d(v[k]) for k in names], "adamw_small")
    for k, a, b, c in zip(names, ds, nms, nvs):
        delta[k], new_m[k], new_v[k] = (t.reshape(w[k].shape) for t in (a, b, c))

    return (loss, grad_x[None], *[grads[k] for k in WEIGHTS], *[delta[k] for k in WEIGHTS],
            *[new_m[k] for k in WEIGHTS], *[new_v[k] for k in WEIGHTS])
```

```python
import functools
import math

import jax
import jax.numpy as jnp
from jax import lax
from jax.experimental import pallas as pl
from jax.experimental.pallas import tpu as pltpu

F32 = jnp.float32
BF16 = jnp.bfloat16
MESH = pl.DeviceIdType.MESH
ANY = pl.BlockSpec(memory_space=pl.ANY)

D_MODEL = 1024
DEPTH = 2
S5_GROUPS = 64
S5_GROUP = 16
S5_STATE = 64
S5_COLS = S5_GROUPS * S5_STATE
S5_TILE_GROUPS = 8
S5_TILES = S5_GROUPS // S5_TILE_GROUPS
S5_TILE_IN = S5_TILE_GROUPS * S5_GROUP
S5_TILE_ST = S5_TILE_GROUPS * S5_STATE
SEGS = 8
SSD_HEADS = 16
SSD_HEAD_DIM = 64
SSD_GROUPS = 2
SSD_GROUP_HEADS = SSD_HEADS // SSD_GROUPS
SSD_STATE = 128
SSD_CONV = 4
SSD_CHUNK = 128
SSD_WIDTH = 1024
SSD_CONV_DIM = SSD_WIDTH + 2 * SSD_GROUPS * SSD_STATE
IN_PROJ = 3600
IN_MAIN = 3584
IN_PAD = IN_MAIN + 2 * 128
FFN = 2816
EPS = 1e-6
LANES = 128
ROW_TILE = 256

ADAM_LR = 0.001
ADAM_B1 = 0.9
ADAM_B2 = 0.999
ADAM_EPS = 1e-08
ADAM_WD = 0.01
ADAM_STEP = 10

HIGHEST = lax.Precision.HIGHEST


def _sigmoid(x):
    return 1.0 / (1.0 + jnp.exp(-x))


def _silu(x):
    return x * _sigmoid(x)


def _dsilu(x):
    s = _sigmoid(x)
    return s * (1.0 + x * (1.0 - s))


_GELU_K = math.sqrt(2.0 / math.pi)
_GELU_C = 0.044715


def _gelu(x):
    t = jnp.tanh(_GELU_K * (x + _GELU_C * x * x * x))
    return 0.5 * x * (1.0 + t)


def _dgelu(x):
    t = jnp.tanh(_GELU_K * (x + _GELU_C * x * x * x))
    return 0.5 * (1.0 + t) + 0.5 * x * (1.0 - t * t) * _GELU_K * (1.0 + 3.0 * _GELU_C * x * x)


def _softplus(x):
    e = jnp.exp(-jnp.abs(x))
    u = 1.0 + e
    log1p = jnp.where(u == 1.0, e, jnp.log(u) * e / jnp.where(u == 1.0, 1.0, u - 1.0))
    return jnp.maximum(x, 0.0) + log1p


def _rstd(x):
    return lax.rsqrt(jnp.mean(x * x, axis=-1, keepdims=True) + EPS)


def _rms_bwd(x, r, gain, dy):
    dyg = dy * gain
    dx = r * dyg - x * (r * r * r) * jnp.mean(x * dyg, axis=-1, keepdims=True)
    dgain = jnp.sum(dy * x * r, axis=0, keepdims=True)
    return dx, dgain


def _dot(a, b):
    return jnp.dot(a, b, preferred_element_type=F32)


def _dot_nt(a, b):
    return lax.dot_general(a, b, (((1,), (1,)), ((), ())), preferred_element_type=F32)


def _dot_tn(a, b):
    return lax.dot_general(a, b, (((0,), (0,)), ((), ())), preferred_element_type=F32)


def _row_spec(tile, cols):
    return pl.BlockSpec((tile, cols), lambda i: (i, 0))


def _full_spec(shape):
    nd = len(shape)
    return pl.BlockSpec(shape, lambda *_: (0,) * nd)


def _const_spec(shape):
    nd = len(shape)
    return pl.BlockSpec(shape, lambda *_: (0,) * nd, pipeline_mode=pl.Buffered(1))


def _layer_spec(shape, layer, block=0):
    if layer is None:
        return pl.BlockSpec(tuple(shape), lambda *_: (block, 0), pipeline_mode=pl.Buffered(1))
    return pl.BlockSpec((None,) + tuple(shape), lambda *_: (layer, block, 0), pipeline_mode=pl.Buffered(1))


def _acc_rows(ref, val, first):
    @pl.when(first)
    def _():
        ref[...] = val

    @pl.when(jnp.logical_not(first))
    def _():
        ref[...] += val


def _pick_tile(n, cap):
    best = LANES
    for t in range(LANES, cap + 1, LANES):
        if n % t == 0:
            best = t
    return best


def _mm_tn(a, b, name):
    k, m = a.shape
    _, n = b.shape
    tm = _pick_tile(m, 512)
    tn = _pick_tile(n, 1536)

    def body(a_ref, b_ref, o_ref):
        o_ref[...] = _dot_tn(a_ref[...], b_ref[...]).astype(BF16)

    return pl.pallas_call(
        body, name=name, grid=(n // tn, m // tm),
        in_specs=[pl.BlockSpec((k, tm), lambda j, i: (0, i)), pl.BlockSpec((k, tn), lambda j, i: (0, j))],
        out_specs=pl.BlockSpec((tm, tn), lambda j, i: (i, j)),
        out_shape=jax.ShapeDtypeStruct((m, n), BF16),
    )(a, b)


def _rms_inproj(x, gain, w_pad, layer, name):
    L = x.shape[0]

    def body(x_ref, g_ref, w_ref, u_ref, z_ref, xbc_ref, dt_ref, h_ref):
        xv = x_ref[...]
        h = (xv * _rstd(xv) * g_ref[...]).astype(BF16)
        h_ref[...] = h
        p = _dot(h, w_ref[...])
        u_ref[...] = p[:, :1024]
        z_ref[...] = p[:, 1024:2048]
        xbc_ref[...] = p[:, 2048:IN_MAIN]
        dt_ref[...] = p[:, IN_MAIN:IN_PAD]

    return pl.pallas_call(
        body, name=name, grid=(L // ROW_TILE,),
        in_specs=[_row_spec(ROW_TILE, D_MODEL), _full_spec((1, D_MODEL)), _layer_spec((D_MODEL, IN_PAD), layer)],
        out_specs=[_row_spec(ROW_TILE, 1024), _row_spec(ROW_TILE, 1024), _row_spec(ROW_TILE, SSD_CONV_DIM),
                   _row_spec(ROW_TILE, 256), _row_spec(ROW_TILE, D_MODEL)],
        out_shape=[jax.ShapeDtypeStruct((L, 1024), F32), jax.ShapeDtypeStruct((L, 1024), F32),
                   jax.ShapeDtypeStruct((L, SSD_CONV_DIM), F32), jax.ShapeDtypeStruct((L, 256), F32),
                   jax.ShapeDtypeStruct((L, D_MODEL), BF16)],
    )(x, gain, w_pad)


def _s5_prep_math(lr, li, ls, bre, bim):
    step = jnp.exp(ls)
    mag = jnp.exp(lr * step)
    ang = li * step
    are = mag * jnp.cos(ang)
    aim = mag * jnp.sin(ang)
    den = lr * lr + li * li
    nr = are - 1.0
    ni = aim
    cre = (nr * lr + ni * li) / den
    cim = (ni * lr - nr * li) / den
    bbre = cre[None] * bre - cim[None] * bim
    bbim = cre[None] * bim + cim[None] * bre
    return are, aim, bbre, bbim


def _s5_prep(lr, li, ls, bre, bim, name):
    def body(lr_ref, li_ref, ls_ref, bre_ref, bim_ref, are_ref, aim_ref, bbre_ref, bbim_ref):
        are, aim, bbre, bbim = _s5_prep_math(lr_ref[...], li_ref[...], ls_ref[...], bre_ref[...], bim_ref[...])
        are_ref[...] = are
        aim_ref[...] = aim
        bbre_ref[...] = bbre
        bbim_ref[...] = bbim

    gp = jax.ShapeDtypeStruct((S5_GROUPS, S5_STATE), F32)
    hgp = jax.ShapeDtypeStruct((S5_GROUP, S5_GROUPS, S5_STATE), F32)
    return pl.pallas_call(body, name=name, out_shape=[gp, gp, hgp, hgp])(lr, li, ls, bre, bim)


def _s5_prep_bwd(lr, li, ls, bre, bim, dare, daim, dbbre, dbbim, name):
    def body(lr_ref, li_ref, ls_ref, bre_ref, bim_ref, dare_ref, daim_ref, dbbre_ref, dbbim_ref,
             dlr_ref, dli_ref, dls_ref, dbre_ref, dbim_ref):
        _, vjp = jax.vjp(_s5_prep_math, lr_ref[...], li_ref[...], ls_ref[...], bre_ref[...], bim_ref[...])
        dlr, dli, dls, dbre, dbim = vjp((dare_ref[...], daim_ref[...], dbbre_ref[...], dbbim_ref[...]))
        dlr_ref[...] = dlr
        dli_ref[...] = dli
        dls_ref[...] = dls
        dbre_ref[...] = dbre
        dbim_ref[...] = dbim

    gp = jax.ShapeDtypeStruct((S5_GROUPS, S5_STATE), F32)
    g1 = jax.ShapeDtypeStruct((S5_GROUPS, 1), F32)
    hgp = jax.ShapeDtypeStruct((S5_GROUP, S5_GROUPS, S5_STATE), F32)
    return pl.pallas_call(body, name=name, out_shape=[gp, gp, g1, hgp, hgp])(
        lr, li, ls, bre, bim, dare, daim, dbbre, dbbim)


def _cmul_add(ar, ai, sr, si, br, bi):
    return ar * sr - ai * si + br, ar * si + ai * sr + bi


def _shift_rows_down(v):
    rolled = pltpu.roll(v, 1, 0)
    row = lax.broadcasted_iota(jnp.int32, v.shape, 0)
    return jnp.where(row == 0, 0.0, rolled)


def _shift_rows_up(v):
    rolled = pltpu.roll(v, SEGS - 1, 0)
    row = lax.broadcasted_iota(jnp.int32, v.shape, 0)
    return jnp.where(row == SEGS - 1, 0.0, rolled)


def _segment_power(ar, ai, steps):
    n = 1
    while n < steps:
        ar, ai = ar * ar - ai * ai, 2.0 * ar * ai
        n *= 2
    assert n == steps
    return ar, ai


def _segment_entries(ar, ai, fr, fi, steps, shift):
    pr, pi = _segment_power(ar, ai, steps)
    er = jnp.zeros_like(fr)
    ei = jnp.zeros_like(fi)
    for _ in range(SEGS - 1):
        nr, ni = _cmul_add(pr, pi, er, ei, fr, fi)
        er, ei = shift(nr), shift(ni)
    return er, ei


def _s5_scan(u_perm, bre_bd, bim_bd, cre_bd, cim_bd, are, aim, name):
    L = u_perm.shape[0]
    steps = L // SEGS

    def body(u_ref, bre_ref, bim_ref, cre_ref, cim_ref, are_ref, aim_ref, y_ref, xr_ref, xi_ref):
        u = u_ref[...].astype(BF16)
        xr_ref[...] = _dot(u, bre_ref[0])
        xi_ref[...] = _dot(u, bim_ref[0])
        ar = jnp.broadcast_to(are_ref[0], (SEGS, S5_TILE_ST))
        ai = jnp.broadcast_to(aim_ref[0], (SEGS, S5_TILE_ST))
        zero = jnp.zeros((SEGS, S5_TILE_ST), F32)

        def finals(j, c):
            rows = pl.ds(pl.multiple_of(j * SEGS, SEGS), SEGS)
            return _cmul_add(ar, ai, c[0], c[1], xr_ref[rows, :], xi_ref[rows, :])

        fr, fi = lax.fori_loop(0, steps, finals, (zero, zero), unroll=4)
        er, ei = _segment_entries(ar, ai, fr, fi, steps, _shift_rows_down)

        def scan(j, c):
            rows = pl.ds(pl.multiple_of(j * SEGS, SEGS), SEGS)
            sr, si = _cmul_add(ar, ai, c[0], c[1], xr_ref[rows, :], xi_ref[rows, :])
            xr_ref[rows, :] = sr
            xi_ref[rows, :] = si
            return sr, si

        lax.fori_loop(0, steps, scan, (er, ei), unroll=4)
        y_ref[...] = (_dot(xr_ref[...].astype(BF16), cre_ref[0]) - _dot(xi_ref[...].astype(BF16), cim_ref[0]))

    tile3 = lambda a, b: pl.BlockSpec((1, a, b), lambda k: (k, 0, 0))
    return pl.pallas_call(
        body, name=name, grid=(S5_TILES,),
        in_specs=[pl.BlockSpec((L, S5_TILE_IN), lambda k: (0, k)),
                  tile3(S5_TILE_IN, S5_TILE_ST), tile3(S5_TILE_IN, S5_TILE_ST),
                  tile3(S5_TILE_ST, S5_TILE_IN), tile3(S5_TILE_ST, S5_TILE_IN),
                  tile3(1, S5_TILE_ST), tile3(1, S5_TILE_ST)],
        out_specs=[pl.BlockSpec((L, S5_TILE_IN), lambda k: (0, k)),
                   pl.BlockSpec((L, S5_TILE_ST), lambda k: (0, k)), pl.BlockSpec((L, S5_TILE_ST), lambda k: (0, k))],
        out_shape=[jax.ShapeDtypeStruct((L, 1024), F32), jax.ShapeDtypeStruct((L, S5_COLS), F32),
                   jax.ShapeDtypeStruct((L, S5_COLS), F32)],
    )(u_perm, bre_bd, bim_bd, cre_bd, cim_bd, are, aim)


def _s5_scan_bwd(dy_perm, u_perm, xr, xi, bret_bd, bimt_bd, cret_bd, cimt_bd, are, aim, name):
    L = u_perm.shape[0]
    steps = L // SEGS

    def body(dy_ref, u_ref, xr_ref, xi_ref, bret_ref, bimt_ref, cret_ref, cimt_ref, are_ref, aim_ref,
             du_ref, dar_ref, dai_ref, dcre_ref, dcim_ref, dbre_ref, dbim_ref, gr_ref, gi_ref):
        dy = dy_ref[...].astype(BF16)
        u = u_ref[...].astype(BF16)
        gr_ref[...] = _dot(dy, cret_ref[0])
        gi_ref[...] = -_dot(dy, cimt_ref[0])
        ar = jnp.broadcast_to(are_ref[0], (SEGS, S5_TILE_ST))
        ai = -jnp.broadcast_to(aim_ref[0], (SEGS, S5_TILE_ST))
        zero = jnp.zeros((SEGS, S5_TILE_ST), F32)

        def finals(k, c):
            rows = pl.ds(pl.multiple_of((steps - 1 - k) * SEGS, SEGS), SEGS)
            return _cmul_add(ar, ai, c[0], c[1], gr_ref[rows, :], gi_ref[rows, :])

        fr, fi = lax.fori_loop(0, steps, finals, (zero, zero), unroll=4)
        er, ei = _segment_entries(ar, ai, fr, fi, steps, _shift_rows_up)

        def scan(k, c):
            sr0, si0, accr, acci = c
            j = steps - 1 - k
            rows = pl.ds(pl.multiple_of(j * SEGS, SEGS), SEGS)
            sr, si = _cmul_add(ar, ai, sr0, si0, gr_ref[rows, :], gi_ref[rows, :])
            gr_ref[rows, :] = sr
            gi_ref[rows, :] = si
            prev = pl.ds(pl.multiple_of(jnp.maximum(j - 1, 0) * SEGS, SEGS), SEGS)
            live = (j > 0).astype(F32)
            xpr = xr_ref[prev, :] * live
            xpi = xi_ref[prev, :] * live
            return sr, si, accr + sr * xpr + si * xpi, acci + si * xpr - sr * xpi

        _, _, accr, acci = lax.fori_loop(0, steps, scan, (er, ei, zero, zero), unroll=4)
        first = pl.ds(0, SEGS)
        last = pl.ds((steps - 1) * SEGS, SEGS)
        xpr = _shift_rows_down(xr_ref[last, :])
        xpi = _shift_rows_down(xi_ref[last, :])
        g0r = gr_ref[first, :]
        g0i = gi_ref[first, :]
        accr = accr + g0r * xpr + g0i * xpi
        acci = acci + g0i * xpr - g0r * xpi
        dar_ref[0] = jnp.sum(accr, axis=0, keepdims=True)
        dai_ref[0] = jnp.sum(acci, axis=0, keepdims=True)

        grb = gr_ref[...].astype(BF16)
        gib = gi_ref[...].astype(BF16)
        du_ref[...] = _dot(grb, bret_ref[0]) + _dot(gib, bimt_ref[0])
        dbre_ref[0] = _dot_tn(u, grb)
        dbim_ref[0] = _dot_tn(u, gib)
        dcre_ref[0] = _dot_tn(dy, xr_ref[...].astype(BF16))
        dcim_ref[0] = -_dot_tn(dy, xi_ref[...].astype(BF16))

    tile3 = lambda a, b: pl.BlockSpec((1, a, b), lambda k: (k, 0, 0))
    col_in = pl.BlockSpec((L, S5_TILE_IN), lambda k: (0, k))
    col_st = pl.BlockSpec((L, S5_TILE_ST), lambda k: (0, k))
    dense = jax.ShapeDtypeStruct((S5_TILES, S5_TILE_IN, S5_TILE_ST), F32)
    vec = jax.ShapeDtypeStruct((S5_TILES, 1, S5_TILE_ST), F32)
    return pl.pallas_call(
        body, name=name, grid=(S5_TILES,),
        in_specs=[col_in, col_in, col_st, col_st,
                  tile3(S5_TILE_ST, S5_TILE_IN), tile3(S5_TILE_ST, S5_TILE_IN),
                  tile3(S5_TILE_IN, S5_TILE_ST), tile3(S5_TILE_IN, S5_TILE_ST),
                  tile3(1, S5_TILE_ST), tile3(1, S5_TILE_ST)],
        out_specs=[col_in, tile3(1, S5_TILE_ST), tile3(1, S5_TILE_ST),
                   tile3(S5_TILE_IN, S5_TILE_ST), tile3(S5_TILE_IN, S5_TILE_ST),
                   tile3(S5_TILE_IN, S5_TILE_ST), tile3(S5_TILE_IN, S5_TILE_ST)],
        out_shape=[jax.ShapeDtypeStruct((L, 1024), F32), vec, vec, dense, dense, dense, dense],
        scratch_shapes=[pltpu.VMEM((L, S5_TILE_ST), F32), pltpu.VMEM((L, S5_TILE_ST), F32)],
    )(dy_perm, u_perm, xr, xi, bret_bd, bimt_bd, cret_bd, cimt_bd, are, aim)


def _s5_post(ys, u, d_skip, w_glu, b_glu, gain, layer, name):
    L = ys.shape[0]

    def body(ys_ref, u_ref, d_ref, w_ref, b_ref, g_ref, ya_ref):
        g = _gelu(ys_ref[...] + d_ref[...] * u_ref[...])
        q = _dot(g.astype(BF16), w_ref[...]) + b_ref[...]
        oa = g * _sigmoid(q)
        ya_ref[...] = (oa * _rstd(oa) * g_ref[...]).astype(BF16)

    vec = _full_spec((1, 1024))
    return pl.pallas_call(
        body, name=name, grid=(L // ROW_TILE,),
        in_specs=[_row_spec(ROW_TILE, 1024), _row_spec(ROW_TILE, 1024), vec, _layer_spec((1024, 1024), layer), vec,
                  vec],
        out_specs=_row_spec(ROW_TILE, 1024),
        out_shape=jax.ShapeDtypeStruct((L, 1024), BF16),
    )(ys, u, d_skip, w_glu, b_glu, gain)


def _s5_post_bwd(dx, w_out, ys, u, d_skip, w_glu, b_glu, gain, layer, name):
    L = ys.shape[0]

    def body(dx_ref, wo_ref, ys_ref, u_ref, d_ref, w_ref, b_ref, gn_ref,
             dys_ref, dus_ref, g_ref, dq_ref, dgain_ref, dd_ref, db_ref):
        first = pl.program_id(0) == 0
        uv = u_ref[...]
        yt = ys_ref[...] + d_ref[...] * uv
        g = _gelu(yt)
        gb = g.astype(BF16)
        q = _dot(gb, w_ref[...]) + b_ref[...]
        s = _sigmoid(q)
        oa = g * s
        dya = _dot_nt(dx_ref[...], wo_ref[...])
        doa, dgain = _rms_bwd(oa, _rstd(oa), gn_ref[...], dya)
        dq = doa * g * s * (1.0 - s)
        dqb = dq.astype(BF16)
        dg = doa * s + _dot_nt(dqb, w_ref[...])
        dyt = dg * _dgelu(yt)
        dys_ref[...] = dyt
        dus_ref[...] = dyt * d_ref[...]
        g_ref[...] = gb
        dq_ref[...] = dqb
        _acc_rows(dgain_ref, dgain, first)
        _acc_rows(dd_ref, jnp.sum(dyt * uv, axis=0, keepdims=True), first)
        _acc_rows(db_ref, jnp.sum(dq, axis=0, keepdims=True), first)

    vec = _full_spec((1, 1024))
    row = _row_spec(ROW_TILE, 1024)
    vshape = jax.ShapeDtypeStruct((1, 1024), F32)
    return pl.pallas_call(
        body, name=name, grid=(L // ROW_TILE,),
        in_specs=[row, _layer_spec((1024, 1024), layer, 0), row, row, vec, _layer_spec((1024, 1024), layer), vec,
                  vec],
        out_specs=[row, row, row, row, vec, vec, vec],
        out_shape=[jax.ShapeDtypeStruct((L, 1024), F32), jax.ShapeDtypeStruct((L, 1024), F32),
                   jax.ShapeDtypeStruct((L, 1024), BF16), jax.ShapeDtypeStruct((L, 1024), BF16),
                   vshape, vshape, vshape],
    )(dx, w_out, ys, u, d_skip, w_glu, b_glu, gain)


CONV_TILE = 256


def _shift_time(v, d):
    if d == 0:
        return v
    rolled = pltpu.roll(v, d, 0)
    row = lax.broadcasted_iota(jnp.int32, v.shape, 0)
    return jnp.where(row < d, 0.0, rolled)


def _unshift_time(v, d):
    if d == 0:
        return v
    n = v.shape[0]
    rolled = pltpu.roll(v, n - d, 0)
    row = lax.broadcasted_iota(jnp.int32, v.shape, 0)
    return jnp.where(row >= n - d, 0.0, rolled)


def _ssd_conv(xbc, w, b, name):
    L = xbc.shape[0]

    def body(x_ref, w_ref, b_ref, o_ref):
        xv = x_ref[...]
        pre = jnp.broadcast_to(b_ref[...], xv.shape)
        for k in range(SSD_CONV):
            pre = pre + w_ref[k:k + 1, :] * _shift_time(xv, SSD_CONV - 1 - k)
        o_ref[...] = _silu(pre)

    col = pl.BlockSpec((L, CONV_TILE), lambda j: (0, j))
    return pl.pallas_call(
        body, name=name, grid=(SSD_CONV_DIM // CONV_TILE,),
        in_specs=[col, pl.BlockSpec((8, CONV_TILE), lambda j: (0, j)), pl.BlockSpec((1, CONV_TILE), lambda j: (0, j))],
        out_specs=col, out_shape=jax.ShapeDtypeStruct((L, SSD_CONV_DIM), F32),
    )(xbc, w, b)


def _ssd_conv_bwd(dxc, xbc, w, b, name):
    L = xbc.shape[0]

    def body(d_ref, x_ref, w_ref, b_ref, dx_ref, dw_ref, db_ref):
        xv = x_ref[...]
        shifted = [_shift_time(xv, SSD_CONV - 1 - k) for k in range(SSD_CONV)]
        pre = jnp.broadcast_to(b_ref[...], xv.shape)
        for k in range(SSD_CONV):
            pre = pre + w_ref[k:k + 1, :] * shifted[k]
        dpre = d_ref[...] * _dsilu(pre)
        dx = jnp.zeros_like(xv)
        rows = []
        for k in range(SSD_CONV):
            dx = dx + w_ref[k:k + 1, :] * _unshift_time(dpre, SSD_CONV - 1 - k)
            rows.append(jnp.sum(dpre * shifted[k], axis=0, keepdims=True))
        dx_ref[...] = dx
        dw_ref[...] = jnp.concatenate(rows + [jnp.zeros((8 - SSD_CONV, CONV_TILE), F32)], axis=0)
        db_ref[...] = jnp.sum(dpre, axis=0, keepdims=True)

    col = pl.BlockSpec((L, CONV_TILE), lambda j: (0, j))
    w_spec = pl.BlockSpec((8, CONV_TILE), lambda j: (0, j))
    b_spec = pl.BlockSpec((1, CONV_TILE), lambda j: (0, j))
    return pl.pallas_call(
        body, name=name, grid=(SSD_CONV_DIM // CONV_TILE,),
        in_specs=[col, col, w_spec, b_spec], out_specs=[col, w_spec, b_spec],
        out_shape=[jax.ShapeDtypeStruct((L, SSD_CONV_DIM), F32), jax.ShapeDtypeStruct((8, SSD_CONV_DIM), F32),
                   jax.ShapeDtypeStruct((1, SSD_CONV_DIM), F32)],
    )(dxc, xbc, w, b)


def _tri(lower):
    r = lax.broadcasted_iota(jnp.int32, (SSD_CHUNK, SSD_CHUNK), 0)
    c = lax.broadcasted_iota(jnp.int32, (SSD_CHUNK, SSD_CHUNK), 1)
    return (r >= c) if lower else (r <= c)


def _ssd_chunk_common(dt_ref, bias_ref, alog_ref):
    pre = dt_ref[...] + bias_ref[0]
    dtp = _softplus(pre)
    a_neg = -jnp.exp(alog_ref[0])
    dta = dtp * a_neg
    ltri = _tri(True).astype(F32)
    acum = jnp.dot(ltri, dta, precision=HIGHEST, preferred_element_type=F32)
    return pre, dtp, a_neg, dta, acum


GROUP_W = SSD_GROUP_HEADS * SSD_HEAD_DIM


def _head_expander():
    r = lax.broadcasted_iota(jnp.int32, (LANES, GROUP_W), 0)
    c = lax.broadcasted_iota(jnp.int32, (LANES, GROUP_W), 1)
    return (c // SSD_HEAD_DIM == r).astype(F32)


def _dot_exact(a, b):
    return jnp.dot(a, b, precision=HIGHEST, preferred_element_type=F32)


def _decay_mask(acum_all, acum_t, h, lower):
    seg = acum_all[:, h:h + 1] - acum_t[h:h + 1, :]
    return jnp.where(lower, jnp.exp(jnp.minimum(seg, 0.0)), 0.0)


def _ssd_scan(xc, dt, dt_bias, a_log, d_wide, expand, expand_t, name):
    L = xc.shape[0]
    nc = L // SSD_CHUNK

    def body(x_ref, b_ref, c_ref, dt_ref, bias_ref, alog_ref, d_ref, e_ref, et_ref, y_ref, sp_ref, s_ref, xdt_ref):
        @pl.when(pl.program_id(1) == 0)
        def _():
            s_ref[...] = jnp.zeros_like(s_ref)

        _, dtp_all, _, _, acum_all = _ssd_chunk_common(dt_ref, bias_ref, alog_ref)
        acum_t = acum_all.T
        e = e_ref[...]
        acum_e = _dot_exact(acum_all, e)
        alast_e = acum_e[SSD_CHUNK - 1:SSD_CHUNK, :]
        x = x_ref[...]
        xdt = x * _dot_exact(dtp_all, e)
        xdt_ref[...] = xdt.astype(BF16)
        bm = b_ref[...].astype(BF16)
        cm = c_ref[...].astype(BF16)
        cb = _dot_nt(cm, bm)
        lower = _tri(True)
        sp = s_ref[...]
        for h in range(SSD_GROUP_HEADS):
            cols = slice(h * SSD_HEAD_DIM, (h + 1) * SSD_HEAD_DIM)
            lm = _decay_mask(acum_all, acum_t, h, lower)
            y_ref[:, cols] = _dot((cb * lm).astype(BF16), xdt_ref[:, cols])
        y_ref[...] += _dot_nt(cm, sp.astype(BF16)) * jnp.exp(acum_e) + d_ref[0] * x
        wgt = xdt * jnp.exp(alast_e - acum_e)
        ealast = jnp.exp(_dot_exact(et_ref[...], acum_t)[:, SSD_CHUNK - 1:SSD_CHUNK])
        sp_ref[0, 0] = sp
        s_ref[...] = ealast * sp + _dot_tn(wgt.astype(BF16), bm)

    par = lambda n: pl.BlockSpec((1, 1, n), lambda g, c: (g, 0, 0))
    return pl.pallas_call(
        body, name=name, grid=(SSD_GROUPS, nc),
        in_specs=[pl.BlockSpec((SSD_CHUNK, GROUP_W), lambda g, c: (c, g)),
                  pl.BlockSpec((SSD_CHUNK, SSD_STATE), lambda g, c: (c, 8 + g)),
                  pl.BlockSpec((SSD_CHUNK, SSD_STATE), lambda g, c: (c, 10 + g)),
                  pl.BlockSpec((SSD_CHUNK, LANES), lambda g, c: (c, g)),
                  par(LANES), par(LANES), par(GROUP_W), _full_spec((LANES, GROUP_W)), _full_spec((GROUP_W, LANES))],
        out_specs=[pl.BlockSpec((SSD_CHUNK, GROUP_W), lambda g, c: (c, g)),
                   pl.BlockSpec((1, 1, GROUP_W, SSD_STATE), lambda g, c: (c, g, 0, 0))],
        out_shape=[jax.ShapeDtypeStruct((L, SSD_WIDTH), F32),
                   jax.ShapeDtypeStruct((nc, SSD_GROUPS, GROUP_W, SSD_STATE), F32)],
        scratch_shapes=[pltpu.VMEM((GROUP_W, SSD_STATE), F32), pltpu.VMEM((SSD_CHUNK, GROUP_W), BF16)],
    )(xc, xc, xc, dt, dt_bias, a_log, d_wide, expand, expand_t)


def _ssd_scan_bwd(dy, xc, dt, sprev, dt_bias, a_log, d_wide, expand, expand_t, name):
    L = xc.shape[0]
    nc = L // SSD_CHUNK

    def body(dy_ref, x_ref, b_ref, c_ref, dt_ref, sp_ref, bias_ref, alog_ref, d_ref, e_ref, et_ref,
             dx_ref, db_ref, dc_ref, ddt_ref, dbias_ref, dalog_ref, dd_ref, ds_ref, xdt_ref, dyb_ref):
        first = pl.program_id(1) == 0

        @pl.when(first)
        def _():
            ds_ref[...] = jnp.zeros_like(ds_ref)

        pre, dtp_all, a_neg, _, acum_all = _ssd_chunk_common(dt_ref, bias_ref, alog_ref)
        acum_t = acum_all.T
        e = e_ref[...]
        et = et_ref[...]
        acum_e = _dot_exact(acum_all, e)
        dtp_e = _dot_exact(dtp_all, e)
        alast_e = acum_e[SSD_CHUNK - 1:SSD_CHUNK, :]
        dstate_e = jnp.exp(alast_e - acum_e)
        x = x_ref[...]
        dy = dy_ref[...]
        xdt = x * dtp_e
        xdt_ref[...] = xdt.astype(BF16)
        dyb_ref[...] = dy.astype(BF16)
        bm = b_ref[...].astype(BF16)
        cm = c_ref[...].astype(BF16)
        cb = _dot_nt(cm, bm)
        sp = sp_ref[0, 0]
        spb = sp.astype(BF16)
        dsn = ds_ref[...]
        dsb = dsn.astype(BF16)
        z = _dot_nt(cm, spb)
        dz = dy * jnp.exp(acum_e)
        dzb = dz.astype(BF16)
        dc_acc = _dot(dzb, spb)
        ealast = jnp.exp(_dot_exact(et, acum_t)[:, SSD_CHUNK - 1:SSD_CHUNK])
        ds_ref[...] = _dot_tn(dzb, cm) + ealast * dsn
        dw = _dot_nt(bm, dsb)
        wgt = xdt * dstate_e
        db_acc = _dot(wgt.astype(BF16), dsb)
        lower = _tri(True)
        lane = lax.broadcasted_iota(jnp.int32, (SSD_CHUNK, LANES), 1)
        row = lax.broadcasted_iota(jnp.int32, (SSD_CHUNK, LANES), 0)
        dcb = jnp.zeros((SSD_CHUNK, SSD_CHUNK), F32)
        dacum_all = jnp.zeros((SSD_CHUNK, LANES), F32)
        dacum_cols = jnp.zeros((SSD_CHUNK, LANES), F32)
        for h in range(SSD_GROUP_HEADS):
            cols = slice(h * SSD_HEAD_DIM, (h + 1) * SSD_HEAD_DIM)
            lm = _decay_mask(acum_all, acum_t, h, lower)
            dm = _dot_nt(dyb_ref[:, cols], xdt_ref[:, cols])
            dx_ref[:, cols] = _dot_tn((cb * lm).astype(BF16), dyb_ref[:, cols])
            dm_lm = dm * lm
            dcb = dcb + dm_lm
            q = dm_lm * cb
            dacum_all = jnp.where(lane == h, jnp.sum(q, axis=1, keepdims=True), dacum_all)
            dacum_cols = jnp.where(row == h, jnp.sum(q, axis=0, keepdims=True), dacum_cols)
        dxdt = dx_ref[...] + dw * dstate_e
        sums = _dot_exact(jnp.concatenate([dz * z, dw * wgt, dxdt * x, dy * x], axis=0), et)
        dacum_off = sums[0:SSD_CHUNK]
        dds_ds = sums[SSD_CHUNK:2 * SSD_CHUNK]
        ddtp_x = sums[2 * SSD_CHUNK:3 * SSD_CHUNK]
        dd_part = sums[3 * SSD_CHUNK:4 * SSD_CHUNK]
        ds_s = jnp.sum(_dot_exact(e, dsn * sp).T, axis=0, keepdims=True)
        dalast = ds_s * jnp.exp(acum_all[SSD_CHUNK - 1:SSD_CHUNK, :]) + jnp.sum(dds_ds, axis=0, keepdims=True)
        dacum_all = dacum_all - dacum_cols.T + dacum_off - dds_ds + jnp.where(row == SSD_CHUNK - 1, dalast, 0.0)
        dx_ref[...] = d_ref[0] * dy + dxdt * dtp_e
        dcbb = dcb.astype(BF16)
        dc_ref[...] = dc_acc + _dot(dcbb, bm)
        db_ref[...] = db_acc + _dot_tn(dcbb, cm)
        utri = _tri(False).astype(F32)
        ddta = _dot_exact(utri, dacum_all)
        ddt = (ddtp_x + ddta * a_neg) * _sigmoid(pre)
        ddt_ref[...] = ddt
        _acc_rows(dbias_ref, jnp.sum(ddt, axis=0, keepdims=True)[None], first)
        _acc_rows(dalog_ref, (jnp.sum(ddta * dtp_all, axis=0, keepdims=True) * a_neg)[None], first)
        _acc_rows(dd_ref, jnp.sum(dd_part, axis=0, keepdims=True)[None], first)

    rev = lambda c: nc - 1 - c
    par = lambda n: pl.BlockSpec((1, 1, n), lambda g, c: (g, 0, 0))
    pshape = jax.ShapeDtypeStruct((SSD_GROUPS, 1, LANES), F32)
    return pl.pallas_call(
        body, name=name, grid=(SSD_GROUPS, nc),
        in_specs=[pl.BlockSpec((SSD_CHUNK, GROUP_W), lambda g, c: (rev(c), g)),
                  pl.BlockSpec((SSD_CHUNK, GROUP_W), lambda g, c: (rev(c), g)),
                  pl.BlockSpec((SSD_CHUNK, SSD_STATE), lambda g, c: (rev(c), 8 + g)),
                  pl.BlockSpec((SSD_CHUNK, SSD_STATE), lambda g, c: (rev(c), 10 + g)),
                  pl.BlockSpec((SSD_CHUNK, LANES), lambda g, c: (rev(c), g)),
                  pl.BlockSpec((1, 1, GROUP_W, SSD_STATE), lambda g, c: (rev(c), g, 0, 0)),
                  par(LANES), par(LANES), par(GROUP_W), _full_spec((LANES, GROUP_W)), _full_spec((GROUP_W, LANES))],
        out_specs=[pl.BlockSpec((SSD_CHUNK, GROUP_W), lambda g, c: (rev(c), g)),
                   pl.BlockSpec((SSD_CHUNK, SSD_STATE), lambda g, c: (rev(c), g)),
                   pl.BlockSpec((SSD_CHUNK, SSD_STATE), lambda g, c: (rev(c), g)),
                   pl.BlockSpec((SSD_CHUNK, LANES), lambda g, c: (rev(c), g)),
                   par(LANES), par(LANES), par(LANES)],
        out_shape=[jax.ShapeDtypeStruct((L, SSD_WIDTH), F32), jax.ShapeDtypeStruct((L, 256), F32),
                   jax.ShapeDtypeStruct((L, 256), F32), jax.ShapeDtypeStruct((L, 256), F32),
                   pshape, pshape, pshape],
        scratch_shapes=[pltpu.VMEM((GROUP_W, SSD_STATE), F32), pltpu.VMEM((SSD_CHUNK, GROUP_W), BF16),
                        pltpu.VMEM((SSD_CHUNK, GROUP_W), BF16)],
    )(dy, xc, xc, xc, dt, sprev, dt_bias, a_log, d_wide, expand, expand_t)


def _ssd_post(y, z, gain, name):
    L = y.shape[0]

    def body(y_ref, z_ref, g_ref, o_ref):
        ob = y_ref[...] * _silu(z_ref[...])
        o_ref[...] = (ob * _rstd(ob) * g_ref[...]).astype(BF16)

    row = _row_spec(ROW_TILE, 1024)
    return pl.pallas_call(body, name=name, grid=(L // ROW_TILE,), in_specs=[row, row, _full_spec((1, 1024))],
                          out_specs=row, out_shape=jax.ShapeDtypeStruct((L, 1024), BF16))(y, z, gain)


def _ssd_post_bwd(dx, w_out, y, z, gain, layer, name):
    L = y.shape[0]

    def body(dx_ref, wo_ref, y_ref, z_ref, g_ref, dy_ref, dz_ref, dgain_ref):
        first = pl.program_id(0) == 0
        yv = y_ref[...]
        zv = z_ref[...]
        sz = _silu(zv)
        ob = yv * sz
        dyb = _dot_nt(dx_ref[...], wo_ref[...])
        dob, dgain = _rms_bwd(ob, _rstd(ob), g_ref[...], dyb)
        dy_ref[...] = dob * sz
        dz_ref[...] = dob * yv * _dsilu(zv)
        _acc_rows(dgain_ref, dgain, first)

    row = _row_spec(ROW_TILE, 1024)
    vec = _full_spec((1, 1024))
    return pl.pallas_call(
        body, name=name, grid=(L // ROW_TILE,),
        in_specs=[row, _layer_spec((1024, 1024), layer, 1), row, row, vec],
        out_specs=[row, row, vec],
        out_shape=[jax.ShapeDtypeStruct((L, 1024), F32), jax.ShapeDtypeStruct((L, 1024), F32),
                   jax.ShapeDtypeStruct((1, 1024), F32)],
    )(dx, w_out, y, z, gain)


def _out_proj(x, ya, yb, w_out, layer, name):
    L = x.shape[0]

    def body(x_ref, ya_ref, yb_ref, w_ref, o_ref):
        o_ref[...] = x_ref[...] + _dot(ya_ref[...], w_ref[:1024, :]) + _dot(yb_ref[...], w_ref[1024:, :])

    row = _row_spec(ROW_TILE, 1024)
    return pl.pallas_call(body, name=name, grid=(L // ROW_TILE,),
                          in_specs=[row, row, row, _layer_spec((2048, 1024), layer)],
                          out_specs=row, out_shape=jax.ShapeDtypeStruct((L, D_MODEL), F32))(x, ya, yb, w_out)


def _ffn(x, gain, w_gate, w_up, w_down, layer, name):
    L = x.shape[0]

    def body(x_ref, g_ref, wg_ref, wu_ref, wd_ref, o_ref, gt_ref, up_ref):
        xv = x_ref[...]
        h = (xv * _rstd(xv) * g_ref[...]).astype(BF16)
        gt = _dot_nt(h, wg_ref[...])
        up = _dot_nt(h, wu_ref[...])
        gt_ref[...] = gt
        up_ref[...] = up
        o_ref[...] = xv + _dot((_silu(gt) * up).astype(BF16), wd_ref[...])

    row = _row_spec(ROW_TILE, D_MODEL)
    hid = _row_spec(ROW_TILE, FFN)
    return pl.pallas_call(
        body, name=name, grid=(L // ROW_TILE,),
        in_specs=[row, _full_spec((1, D_MODEL)), _layer_spec((FFN, D_MODEL), layer),
                  _layer_spec((FFN, D_MODEL), layer), _layer_spec((FFN, D_MODEL), layer)],
        out_specs=[row, hid, hid],
        out_shape=[jax.ShapeDtypeStruct((L, D_MODEL), F32), jax.ShapeDtypeStruct((L, FFN), F32),
                   jax.ShapeDtypeStruct((L, FFN), F32)],
    )(x, gain, w_gate, w_up, w_down)


def _ffn_bwd(dx2, x1, gt, up, gain, w_gate, w_up, w_down, layer, name):
    L = x1.shape[0]

    def body(d_ref, x_ref, gt_ref, up_ref, g_ref, wg_ref, wu_ref, wd_ref,
             dx_ref, dxb_ref, h_ref, act_ref, dgt_ref, dup_ref, dgain_ref):
        first = pl.program_id(0) == 0
        dv = d_ref[...]
        xv = x_ref[...]
        r = _rstd(xv)
        h_ref[...] = (xv * r * g_ref[...]).astype(BF16)
        gtv = gt_ref[...]
        upv = up_ref[...]
        sg = _silu(gtv)
        act_ref[...] = (sg * upv).astype(BF16)
        dact = _dot_nt(dv.astype(BF16), wd_ref[...])
        dgt = (dact * upv * _dsilu(gtv)).astype(BF16)
        dup = (dact * sg).astype(BF16)
        dgt_ref[...] = dgt
        dup_ref[...] = dup
        dh = _dot(dgt, wg_ref[...]) + _dot(dup, wu_ref[...])
        dxn, dgain = _rms_bwd(xv, r, g_ref[...], dh)
        dx = dv + dxn
        dx_ref[...] = dx
        dxb_ref[...] = dx.astype(BF16)
        _acc_rows(dgain_ref, dgain, first)

    row = _row_spec(ROW_TILE, D_MODEL)
    hid = _row_spec(ROW_TILE, FFN)
    vec = _full_spec((1, D_MODEL))
    return pl.pallas_call(
        body, name=name, grid=(L // ROW_TILE,),
        in_specs=[row, row, hid, hid, vec, _layer_spec((FFN, D_MODEL), layer), _layer_spec((FFN, D_MODEL), layer),
                  _layer_spec((FFN, D_MODEL), layer)],
        out_specs=[row, row, row, hid, hid, hid, vec],
        out_shape=[jax.ShapeDtypeStruct((L, D_MODEL), F32), jax.ShapeDtypeStruct((L, D_MODEL), BF16),
                   jax.ShapeDtypeStruct((L, D_MODEL), BF16),
                   jax.ShapeDtypeStruct((L, FFN), BF16), jax.ShapeDtypeStruct((L, FFN), BF16),
                   jax.ShapeDtypeStruct((L, FFN), BF16), jax.ShapeDtypeStruct((1, D_MODEL), F32)],
    )(dx2, x1, gt, up, gain, w_gate, w_up, w_down)


def _inproj_bwd(dx1, x0, du_skip, du_scan, dz, dxbc, ddt, gain, w_pad, layer, name):
    L = x0.shape[0]

    def body(d_ref, x_ref, dus_ref, duc_ref, dz_ref, dxbc_ref, ddt_ref, g_ref, w_ref,
             dx_ref, dxb_ref, dp_ref, dgain_ref):
        first = pl.program_id(0) == 0
        xv = x_ref[...]
        dp = jnp.concatenate([dus_ref[...] + duc_ref[...], dz_ref[...], dxbc_ref[...], ddt_ref[...]],
                             axis=1).astype(BF16)
        dp_ref[...] = dp
        dh = _dot_nt(dp, w_ref[...])
        dxn, dgain = _rms_bwd(xv, _rstd(xv), g_ref[...], dh)
        dx = d_ref[...] + dxn
        dx_ref[...] = dx
        dxb_ref[...] = dx.astype(BF16)
        _acc_rows(dgain_ref, dgain, first)

    row = _row_spec(ROW_TILE, D_MODEL)
    vec = _full_spec((1, D_MODEL))
    return pl.pallas_call(
        body, name=name, grid=(L // ROW_TILE,),
        in_specs=[row, row, row, row, row, _row_spec(ROW_TILE, SSD_CONV_DIM), _row_spec(ROW_TILE, 256), vec,
                  _layer_spec((D_MODEL, IN_PAD), layer)],
        out_specs=[row, row, _row_spec(ROW_TILE, IN_PAD), vec],
        out_shape=[jax.ShapeDtypeStruct((L, D_MODEL), F32), jax.ShapeDtypeStruct((L, D_MODEL), BF16),
                   jax.ShapeDtypeStruct((L, IN_PAD), BF16), jax.ShapeDtypeStruct((1, D_MODEL), F32)],
    )(dx1, x0, du_skip, du_scan, dz, dxbc, ddt, gain, w_pad)


def _final_loss(x, gain, target, name):
    L = x.shape[0]

    def body(x_ref, g_ref, t_ref, loss_ref, dx_ref, dxb_ref, dgain_ref):
        first = pl.program_id(0) == 0
        xv = x_ref[...]
        r = _rstd(xv)
        err = xv * r * g_ref[...] - t_ref[...]
        part = 0.5 * jnp.sum(jnp.mean(err * err, axis=-1, keepdims=True), axis=0, keepdims=True)
        dx, dgain = _rms_bwd(xv, r, g_ref[...], err * (1.0 / D_MODEL))
        dx_ref[...] = dx
        dxb_ref[...] = dx.astype(BF16)
        _acc_rows(loss_ref, jnp.broadcast_to(part, (1, LANES)), first)
        _acc_rows(dgain_ref, dgain, first)

    row = _row_spec(ROW_TILE, D_MODEL)
    vec = _full_spec((1, D_MODEL))
    return pl.pallas_call(
        body, name=name, grid=(L // ROW_TILE,), in_specs=[row, vec, row],
        out_specs=[_full_spec((1, LANES)), row, row, vec],
        out_shape=[jax.ShapeDtypeStruct((1, LANES), F32), jax.ShapeDtypeStruct((L, D_MODEL), F32),
                   jax.ShapeDtypeStruct((L, D_MODEL), BF16), jax.ShapeDtypeStruct((1, D_MODEL), F32)],
    )(x, gain, target)


def _to_segments(a):
    L, n = a.shape
    return a.reshape(SEGS, L // SEGS, n).transpose(1, 0, 2).reshape(L, n)


def _from_segments(a):
    L, n = a.shape
    return a.reshape(L // SEGS, SEGS, n).transpose(1, 0, 2).reshape(L, n)


def _block_diag_in_to_state(m):
    m = m.reshape(S5_TILES, S5_TILE_GROUPS, S5_GROUP, S5_STATE)
    eye = jnp.eye(S5_TILE_GROUPS, dtype=m.dtype)
    out = m[:, :, :, None, :] * eye[None, :, None, :, None]
    return out.reshape(S5_TILES, S5_TILE_IN, S5_TILE_ST)


def _block_diag_extract(d):
    d = d.reshape(S5_TILES, S5_TILE_GROUPS, S5_GROUP, S5_TILE_GROUPS, S5_STATE)
    d = jnp.stack([d[:, a, :, a, :] for a in range(S5_TILE_GROUPS)], axis=1)
    return d.reshape(S5_GROUPS, S5_GROUP, S5_STATE)


def _pad_in_proj(w):
    z = jnp.zeros(w.shape[:-1] + (LANES - SSD_GROUP_HEADS,), w.dtype)
    return jnp.concatenate([w[..., :IN_MAIN + 8], z, w[..., IN_MAIN + 8:], z], axis=-1)


def _unpad_in_proj(w):
    return jnp.concatenate([w[..., :IN_MAIN + 8], w[..., IN_MAIN + LANES:IN_MAIN + LANES + 8]], axis=-1)


def _pad_heads(v):
    v = v.reshape(SSD_GROUPS, 1, SSD_GROUP_HEADS)
    return jnp.pad(v, ((0, 0), (0, 0), (0, LANES - SSD_GROUP_HEADS)))


def _unpad_heads(v):
    return v[:, 0, :SSD_GROUP_HEADS].reshape(SSD_HEADS)


def _layer_forward(x0, p, big, i):
    tag = "l%d_" % i
    ls = p["s5_log_step"].reshape(S5_GROUPS, 1)
    b_hgp = (p["s5_b_re"].transpose(2, 0, 1), p["s5_b_im"].transpose(2, 0, 1))
    are, aim, bbre, bbim = _s5_prep(p["s5_lam_re"], p["s5_lam_im"], ls, b_hgp[0], b_hgp[1], tag + "s5_prep")
    bre_ghp = bbre.transpose(1, 0, 2)
    bim_ghp = bbim.transpose(1, 0, 2)
    bre_bd = _block_diag_in_to_state(bre_ghp).astype(BF16)
    bim_bd = _block_diag_in_to_state(bim_ghp).astype(BF16)
    cret_bd = _block_diag_in_to_state(p["s5_c_re"]).astype(BF16)
    cimt_bd = _block_diag_in_to_state(p["s5_c_im"]).astype(BF16)
    s5mats = dict(bre_bd=bre_bd, bim_bd=bim_bd, cret_bd=cret_bd, cimt_bd=cimt_bd,
                  bret_bd=bre_bd.transpose(0, 2, 1), bimt_bd=bim_bd.transpose(0, 2, 1),
                  cre_bd=cret_bd.transpose(0, 2, 1), cim_bd=cimt_bd.transpose(0, 2, 1),
                  are=are.reshape(S5_TILES, 1, S5_TILE_ST), aim=aim.reshape(S5_TILES, 1, S5_TILE_ST))

    u, z, xbc, dt, h1 = _rms_inproj(x0, p["norm_mix"].reshape(1, -1), big["w_in"], None, tag + "rms_inproj")
    u_perm = _to_segments(u)
    ys_perm, xr, xi = _s5_scan(u_perm, bre_bd, bim_bd, s5mats["cre_bd"], s5mats["cim_bd"],
                               s5mats["are"], s5mats["aim"], tag + "s5_scan")
    ys = _from_segments(ys_perm)
    ya = _s5_post(ys, u, p["s5_d"].reshape(1, -1), big["s5_w_glu"], p["s5_b_glu"].reshape(1, -1),
                  p["s5_norm"].reshape(1, -1), None, tag + "s5_post")

    conv_w = jnp.pad(p["ssd_conv_w"], ((0, 8 - SSD_CONV), (0, 0)))
    conv_b = p["ssd_conv_b"].reshape(1, -1)
    xc = _ssd_conv(xbc, conv_w, conv_b, tag + "ssd_conv")
    expand = _head_expander()
    heads = dict(dt_bias=_pad_heads(p["ssd_dt_bias"]), a_log=_pad_heads(p["ssd_a_log"]),
                 d=jnp.repeat(p["ssd_d"], SSD_HEAD_DIM).reshape(SSD_GROUPS, 1, GROUP_W),
                 expand=expand, expand_t=expand.T)
    y, sprev = _ssd_scan(xc, dt, heads["dt_bias"], heads["a_log"], heads["d"], expand, heads["expand_t"],
                         tag + "ssd_scan")
    yb = _ssd_post(y, z, p["ssd_norm"].reshape(1, -1), tag + "ssd_post")

    x1 = _out_proj(x0, ya, yb, big["w_out"], None, tag + "out_proj")
    x2, gt, up = _ffn(x1, p["norm_ffn"].reshape(1, -1), big["w_gate"], big["w_up"], big["w_down"], None,
                      tag + "ffn")
    saved = dict(x0=x0, h1=h1, u=u, u_perm=u_perm, z=z, xbc=xbc, dt=dt, xr=xr, xi=xi, ys=ys, ya=ya, xc=xc, y=y,
                 sprev=sprev, yb=yb, x1=x1, gt=gt, up=up, s5mats=s5mats, heads=heads, conv_w=conv_w,
                 conv_b=conv_b, ls=ls, b_hgp=b_hgp)
    return x2, saved


def _layer_backward(dx2, dx2b, p, big, s, i):
    tag = "l%d_" % i
    g = {}
    dx1, dx1b, h2, act, dgt, dup, dgain = _ffn_bwd(dx2, s["x1"], s["gt"], s["up"], p["norm_ffn"].reshape(1, -1),
                                                  big["w_gate"], big["w_up"], big["w_down"], None, tag + "ffn_bwd")
    g["norm_ffn"] = dgain[0]
    g["w_down"] = _mm_tn(act, dx2b, tag + "dw_down")
    g["w_gate"] = _mm_tn(dgt, h2, tag + "dw_gate")
    g["w_up"] = _mm_tn(dup, h2, tag + "dw_up")
    g["w_out"] = _mm_tn(jnp.concatenate([s["ya"], s["yb"]], axis=1), dx1b, tag + "dw_out")

    dys, du_skip, gelu_b, dq_b, dgain, dd, dbg = _s5_post_bwd(
        dx1b, big["w_out"], s["ys"], s["u"], p["s5_d"].reshape(1, -1), big["s5_w_glu"],
        p["s5_b_glu"].reshape(1, -1), p["s5_norm"].reshape(1, -1), None, tag + "s5_post_bwd")
    g["s5_norm"] = dgain[0]
    g["s5_d"] = dd[0]
    g["s5_b_glu"] = dbg[0]
    g["s5_w_glu"] = _mm_tn(gelu_b, dq_b, tag + "dw_glu")
    m = s["s5mats"]
    du_perm, dar, dai, dcre_d, dcim_d, dbre_d, dbim_d = _s5_scan_bwd(
        _to_segments(dys), s["u_perm"], s["xr"], s["xi"], m["bret_bd"], m["bimt_bd"], m["cret_bd"], m["cimt_bd"],
        m["are"], m["aim"], tag + "s5_scan_bwd")
    du_scan = _from_segments(du_perm)
    g["s5_c_re"] = _block_diag_extract(dcre_d)
    g["s5_c_im"] = _block_diag_extract(dcim_d)
    dbbre = _block_diag_extract(dbre_d).transpose(1, 0, 2)
    dbbim = _block_diag_extract(dbim_d).transpose(1, 0, 2)
    dlr, dli, dls, dbre, dbim = _s5_prep_bwd(
        p["s5_lam_re"], p["s5_lam_im"], s["ls"], s["b_hgp"][0], s["b_hgp"][1],
        dar.reshape(S5_GROUPS, S5_STATE), dai.reshape(S5_GROUPS, S5_STATE), dbbre, dbbim, tag + "s5_prep_bwd")
    g["s5_lam_re"] = dlr
    g["s5_lam_im"] = dli
    g["s5_log_step"] = dls[:, 0]
    g["s5_b_re"] = dbre
    g["s5_b_im"] = dbim

    dy, dz, dgain = _ssd_post_bwd(dx1b, big["w_out"], s["y"], s["z"], p["ssd_norm"].reshape(1, -1), None,
                                  tag + "ssd_post_bwd")
    g["ssd_norm"] = dgain[0]
    hd = s["heads"]
    dxs, dbm, dcm, ddt, dbias, dalog, dd = _ssd_scan_bwd(dy, s["xc"], s["dt"], s["sprev"], hd["dt_bias"],
                                                       hd["a_log"], hd["d"], hd["expand"], hd["expand_t"],
                                                       tag + "ssd_scan_bwd")
    g["ssd_dt_bias"] = _unpad_heads(dbias)
    g["ssd_a_log"] = _unpad_heads(dalog)
    g["ssd_d"] = _unpad_heads(dd)
    dxc = jnp.concatenate([dxs, dbm, dcm], axis=1)
    dxbc, dcw, dcb = _ssd_conv_bwd(dxc, s["xbc"], s["conv_w"], s["conv_b"], tag + "ssd_conv_bwd")
    g["ssd_conv_w"] = dcw[:SSD_CONV]
    g["ssd_conv_b"] = dcb[0]

    dx0, dx0b, dproj, dgain = _inproj_bwd(dx1, s["x0"], du_skip, du_scan, dz, dxbc, ddt, p["norm_mix"].reshape(1, -1),
                                          big["w_in"], None, tag + "inproj_bwd")
    g["norm_mix"] = dgain[0]
    g["w_in"] = _mm_tn(s["h1"], dproj, tag + "dw_in")
    return dx0, dx0b, g


BIG = ("w_in", "s5_w_glu", "w_out", "w_gate", "w_up", "w_down")
COL_SHARDED = ("w_in",)
T_STORED = ("w_gate", "w_up")
LAYER_SMALL = ("norm_mix", "s5_lam_re", "s5_lam_im", "s5_log_step", "s5_b_re", "s5_b_im", "s5_c_re", "s5_c_im",
               "s5_d", "s5_b_glu", "s5_norm", "ssd_conv_w", "ssd_conv_b", "ssd_dt_bias", "ssd_a_log", "ssd_d",
               "ssd_norm", "norm_ffn")
WEIGHTS = ("norm_mix", "w_in", "s5_lam_re", "s5_lam_im", "s5_log_step", "s5_b_re", "s5_b_im", "s5_c_re", "s5_c_im",
           "s5_d", "s5_w_glu", "s5_b_glu", "s5_norm", "ssd_conv_w", "ssd_conv_b", "ssd_dt_bias", "ssd_a_log",
           "ssd_d", "ssd_norm", "w_out", "norm_ffn", "w_gate", "w_up", "w_down", "norm_final")


S5_BC = ("s5_b_re", "s5_b_im", "s5_c_re", "s5_c_im")
TINY = tuple(k for k in LAYER_SMALL if k not in S5_BC)


def _local_step(x, target, big, small, norm_final):
    saved = []
    h = x
    for i in range(DEPTH):
        p = {k: v[i] for k, v in small.items()}
        h, s = _layer_forward(h, p, big, i)
        saved.append((p, s))
    loss, dx, dxb, dgf = _final_loss(h, norm_final.reshape(1, -1), target, "final_loss")
    grads = [None] * DEPTH
    for i in reversed(range(DEPTH)):
        p, s = saved[i]
        dx, dxb, grads[i] = _layer_backward(dx, dxb, p, big, s, i)
    by_name = {k: [grads[i][k] for i in range(DEPTH)] for k in BIG + LAYER_SMALL}
    return loss[0, 0], dx, by_name, dgf[0]


def _my_place():
    return lax.axis_index("x"), lax.axis_index("y"), lax.axis_index("c")


def _all_gather8(blocks, name):
    nt = len(blocks)

    def body(*refs):
        ins = refs[:nt]
        outs = refs[nt:2 * nt]
        send_sems, recv_sems, local_sems = refs[2 * nt:]
        x, y, c = _my_place()
        me, sibling = (x, y, c), (x, y, 1 - c)
        chips = [(1 - x, y), (x, 1 - y), (1 - x, 1 - y)]

        def slot(t, place):
            px, py, pc = place
            return outs[t].at[4 * px + 2 * py + pc]

        def copy(t, k, block, to, src=None):
            return pltpu.make_async_remote_copy(
                src_ref=slot(t, block) if src is None else src, dst_ref=slot(t, block),
                send_sem=send_sems.at[t, k], recv_sem=recv_sems.at[t, k], device_id=to, device_id_type=MESH)

        mine = [pltpu.make_async_copy(ins[t], slot(t, me), local_sems.at[t]) for t in range(nt)]
        for cp in mine:
            cp.start()
        first = []
        for t in range(nt):
            first.append(copy(t, 0, me, sibling, src=ins[t]))
            first += [copy(t, 1 + j, me, (*chip, c), src=ins[t]) for j, chip in enumerate(chips)]
        for cp in first:
            cp.start()
        passed = []
        for j, chip in enumerate(chips):
            for t in range(nt):
                copy(t, 1 + j, (*chip, c), me).wait_recv()
                fwd = copy(t, 4 + j, (*chip, c), sibling)
                fwd.start()
                passed.append(fwd)
        for t in range(nt):
            copy(t, 0, sibling, me).wait_recv()
            for j, chip in enumerate(chips):
                copy(t, 4 + j, (*chip, 1 - c), me).wait_recv()
        for cp in first + passed:
            cp.wait_send()
        for cp in mine:
            cp.wait()

    return pl.pallas_call(
        body, name=name, in_specs=[ANY] * nt, out_specs=[ANY] * nt,
        out_shape=[jax.ShapeDtypeStruct((8,) + b.shape, b.dtype) for b in blocks],
        scratch_shapes=[pltpu.SemaphoreType.DMA((nt, 7)), pltpu.SemaphoreType.DMA((nt, 7)),
                        pltpu.SemaphoreType.DMA((nt,))],
    )(*blocks)


HBM = pl.BlockSpec(memory_space=pltpu.HBM)
SEM = pl.BlockSpec(memory_space=pltpu.SEMAPHORE)
DATAFLOW = pltpu.SideEffectType.DATAFLOW_SIDE_EFFECTING


def _in_hbm(a):
    return pltpu.with_memory_space_constraint(a, pltpu.HBM)


TOKEN = jax.ShapeDtypeStruct((8, LANES), F32)
VMEM_SPEC = pl.BlockSpec(memory_space=pltpu.VMEM)


def _gather_start(blocks, after, name):
    nt = len(blocks)

    def body(*refs):
        ins = refs[:nt]
        lands = refs[nt:2 * nt]
        send_sems, recv_sems = refs[2 * nt + 1:2 * nt + 3]
        refs[-1][...] = jnp.zeros(TOKEN.shape, F32)
        x, y, c = _my_place()
        me = 4 * x + 2 * y + c
        peers = [(x, y, 1 - c), (1 - x, y, c), (x, 1 - y, c), (1 - x, 1 - y, c)]
        for t in range(nt):
            for k, peer in enumerate(peers):
                pltpu.make_async_remote_copy(src_ref=ins[t], dst_ref=lands[t].at[me], send_sem=send_sems.at[4 * t + k],
                                             recv_sem=recv_sems.at[4 * t + k], device_id=peer,
                                             device_id_type=MESH).start()

    lands = [_in_hbm(lax.empty((8,) + b.shape, b.dtype)) for b in blocks]
    out = pl.pallas_call(
        body, name=name, in_specs=[HBM] * (2 * nt) + [ANY],
        out_shape=(pltpu.SemaphoreType.DMA((4 * nt,)), pltpu.SemaphoreType.DMA((4 * nt,)),
                   *[pltpu.HBM(b.shape, b.dtype) for b in blocks],
                   *[pltpu.HBM((8,) + b.shape, b.dtype) for b in blocks], TOKEN),
        out_specs=(SEM, SEM, *[HBM] * (2 * nt), VMEM_SPEC),
        input_output_aliases={i: 2 + i for i in range(2 * nt)},
        compiler_params=pltpu.CompilerParams(has_side_effects=DATAFLOW),
    )(*[_in_hbm(b) for b in blocks], *lands, after)
    return out[:2], list(out[2:2 + nt]), list(out[2 + nt:2 + 2 * nt]), out[-1]


def _gather_forward(sems, blocks, lands, after, name):
    nt = len(blocks)

    def body(*refs):
        ins = refs[:nt]
        lands_in = refs[nt:2 * nt]
        send1, recv1 = refs[2 * nt:2 * nt + 2]
        send2, recv2 = refs[2 * nt + 3:2 * nt + 5]
        x, y, c = _my_place()
        me = 4 * x + 2 * y + c
        sibling = (x, y, 1 - c)
        sources = [4 * x + 2 * y + (1 - c), 4 * (1 - x) + 2 * y + c, 4 * x + 2 * (1 - y) + c,
                   4 * (1 - x) + 2 * (1 - y) + c]
        for t in range(nt):
            for k, src in enumerate(sources):
                cp = pltpu.make_async_remote_copy(src_ref=ins[t], dst_ref=lands_in[t].at[src],
                                                  send_sem=send1.at[4 * t + k], recv_sem=recv1.at[4 * t + k],
                                                  device_id=sibling, device_id_type=MESH)
                cp.wait_send()
                cp.wait_recv()
            for k, src in enumerate(sources[1:]):
                pltpu.make_async_remote_copy(src_ref=lands_in[t].at[src], dst_ref=lands_in[t].at[src],
                                             send_sem=send2.at[3 * t + k], recv_sem=recv2.at[3 * t + k],
                                             device_id=sibling, device_id_type=MESH).start()

    out = pl.pallas_call(
        body, name=name, in_specs=[HBM] * (2 * nt) + [SEM, SEM, pl.BlockSpec(memory_space=pl.ANY)],
        out_shape=(pltpu.SemaphoreType.DMA((3 * nt,)), pltpu.SemaphoreType.DMA((3 * nt,)),
                   *[pltpu.HBM(b.shape, b.dtype) for b in blocks],
                   *[pltpu.HBM(a.shape, a.dtype) for a in lands]),
        out_specs=(SEM, SEM, *[HBM] * (2 * nt)),
        input_output_aliases={i: 2 + i for i in range(2 * nt)},
        compiler_params=pltpu.CompilerParams(has_side_effects=DATAFLOW),
    )(*blocks, *lands, *sems, after)
    return out[:2], list(out[2 + nt:])


def _gather_finish(sems, lands, after, name):
    nt = len(lands)

    def body(*refs):
        lands_in = refs[:nt]
        send2, recv2 = refs[nt:nt + 2]
        x, y, c = _my_place()
        sibling = (x, y, 1 - c)
        mine = [4 * (1 - x) + 2 * y + c, 4 * x + 2 * (1 - y) + c, 4 * (1 - x) + 2 * (1 - y) + c]
        theirs = [4 * (1 - x) + 2 * y + 1 - c, 4 * x + 2 * (1 - y) + 1 - c, 4 * (1 - x) + 2 * (1 - y) + 1 - c]
        for t in range(nt):
            for k in range(3):
                cp = pltpu.make_async_remote_copy(src_ref=lands_in[t].at[mine[k]], dst_ref=lands_in[t].at[theirs[k]],
                                                  send_sem=send2.at[3 * t + k], recv_sem=recv2.at[3 * t + k],
                                                  device_id=sibling, device_id_type=MESH)
                cp.wait_send()
                cp.wait_recv()

    out = pl.pallas_call(
        body, name=name, in_specs=[HBM] * nt + [SEM, SEM, pl.BlockSpec(memory_space=pl.ANY)],
        out_shape=tuple(pltpu.HBM(a.shape, a.dtype) for a in lands), out_specs=tuple([HBM] * nt),
        input_output_aliases={i: i for i in range(nt)},
        compiler_params=pltpu.CompilerParams(has_side_effects=DATAFLOW),
    )(*lands, *sems, after)
    return list(out)


def _other_chips():
    x, y, _ = _my_place()
    return [(1 - x, y), (x, 1 - y), (1 - x, 1 - y)]


def _scatter_start(chunks, name):
    nt = len(chunks)

    def body(*refs):
        ins = refs[:nt]
        lands = refs[nt:2 * nt]
        send_sems, recv_sems = refs[2 * nt:2 * nt + 2]
        refs[-1][...] = jnp.zeros(TOKEN.shape, F32)
        x, y, c = _my_place()
        for t in range(nt):
            for j, (px, py) in enumerate(_other_chips()):
                pltpu.make_async_remote_copy(src_ref=ins[t].at[2 * px + py], dst_ref=lands[t].at[2 * x + y],
                                             send_sem=send_sems.at[3 * t + j], recv_sem=recv_sems.at[3 * t + j],
                                             device_id=(px, py, c), device_id_type=MESH).start()

    lands = [_in_hbm(lax.empty(a.shape, a.dtype)) for a in chunks]
    out = pl.pallas_call(
        body, name=name, in_specs=[HBM] * (2 * nt),
        out_shape=(pltpu.SemaphoreType.DMA((3 * nt,)), pltpu.SemaphoreType.DMA((3 * nt,)),
                   *[pltpu.HBM(a.shape, a.dtype) for a in chunks] * 2, TOKEN),
        out_specs=(SEM, SEM, *[HBM] * (2 * nt), VMEM_SPEC),
        input_output_aliases={i: 2 + i for i in range(2 * nt)},
        compiler_params=pltpu.CompilerParams(has_side_effects=DATAFLOW),
    )(*[_in_hbm(a) for a in chunks], *lands)
    return out[:2], list(out[2:2 + nt]), list(out[2 + nt:2 + 2 * nt]), out[-1]


def _scatter_finish(sems, chunks, lands, after, name):
    nt = len(chunks)

    def body(*refs):
        ins = refs[:nt]
        lands_in = refs[nt:2 * nt]
        send_sems, recv_sems = refs[2 * nt:2 * nt + 2]
        _, _, c = _my_place()
        for t in range(nt):
            for j, (px, py) in enumerate(_other_chips()):
                cp = pltpu.make_async_remote_copy(src_ref=ins[t].at[2 * px + py], dst_ref=lands_in[t].at[2 * px + py],
                                                  send_sem=send_sems.at[3 * t + j], recv_sem=recv_sems.at[3 * t + j],
                                                  device_id=(px, py, c), device_id_type=MESH)
                cp.wait_send()
                cp.wait_recv()

    out = pl.pallas_call(
        body, name=name, in_specs=[HBM] * (2 * nt) + [SEM, SEM, ANY],
        out_shape=tuple(pltpu.HBM(a.shape, a.dtype) for a in lands), out_specs=tuple([HBM] * nt),
        input_output_aliases={nt + i: i for i in range(nt)},
        compiler_params=pltpu.CompilerParams(has_side_effects=DATAFLOW),
    )(*chunks, *lands, *sems, after)
    return list(out)


def _swap_halves(views, name):
    nt = len(views)

    def body(*refs):
        ins = refs[:nt]
        outs = refs[nt:2 * nt]
        send_sems, recv_sems = refs[2 * nt:]
        x, y, c = _my_place()
        copies = [pltpu.make_async_remote_copy(
            src_ref=ins[t].at[pl.ds(0, views[t].shape[0]), pl.ds(1 - c, 1)], dst_ref=outs[t],
            send_sem=send_sems.at[t], recv_sem=recv_sems.at[t], device_id=(x, y, 1 - c), device_id_type=MESH)
            for t in range(nt)]
        for cp in copies:
            cp.start()
        for cp in copies:
            cp.wait()

    return pl.pallas_call(
        body, name=name, in_specs=[ANY] * nt, out_specs=[ANY] * nt,
        out_shape=[jax.ShapeDtypeStruct((a.shape[0], 1) + a.shape[2:], a.dtype) for a in views],
        scratch_shapes=[pltpu.SemaphoreType.DMA((nt,)), pltpu.SemaphoreType.DMA((nt,))],
    )(*views)


def _pair_add_halves(view, recv, name):
    n, _, rows, cols = view.shape
    tile = _row_tile(rows, cols, 4)

    def body(a0_ref, a1_ref, r_ref, o_ref):
        mine = jnp.where(lax.axis_index("c") == 0, a0_ref[...], a1_ref[...])
        o_ref[...] = (mine.astype(F32) + r_ref[...].astype(F32)).astype(o_ref.dtype)

    half = lambda h: pl.BlockSpec((None, None, tile, cols), lambda p, i: (p, h, i, 0))
    return pl.pallas_call(
        body, name=name, grid=(n, rows // tile), in_specs=[half(0), half(1), half(0)],
        out_specs=pl.BlockSpec((None, tile, cols), lambda p, i: (p, i, 0)),
        out_shape=jax.ShapeDtypeStruct((n, rows, cols), view.dtype))(view, view, recv)


def _sum_chunks(lands, chunks, order, name):
    _, rows, cols = chunks.shape
    tile = _row_tile(rows, cols, 5)

    def body(order_ref, l0_ref, l1_ref, l2_ref, own_ref, o_ref):
        o_ref[...] = ((l0_ref[...].astype(F32) + l1_ref[...].astype(F32)) + l2_ref[...].astype(F32)
                      + own_ref[...].astype(F32))

    slot = lambda j: pl.BlockSpec((None, tile, cols), lambda i, order_ref: (order_ref[j], i, 0))
    grid_spec = pltpu.PrefetchScalarGridSpec(
        num_scalar_prefetch=1, grid=(rows // tile,), in_specs=[slot(0), slot(1), slot(2), slot(3)],
        out_specs=pl.BlockSpec((tile, cols), lambda i, order_ref: (i, 0)))
    return pl.pallas_call(body, name=name, grid_spec=grid_spec,
                          out_shape=jax.ShapeDtypeStruct((rows, cols), F32))(order, lands, lands, lands, chunks)


def _adamw_layer(w, g_mine, g_sibling, m, v, layer, prev, name):
    depth, rows, cols = w.shape
    half = rows // 2
    tile = _row_tile(half, cols, 10)
    tiles = half // tile

    def body(w_ref, gm_ref, gs_ref, m_ref, v_ref, *rest):
        d_ref, nm_ref, nv_ref, go_ref = rest[-4:]
        gv = jnp.where(pl.program_id(0) == lax.axis_index("c"), gm_ref[...], gs_ref[...])
        d_ref[...], nm_ref[...], nv_ref[...] = _adamw_math(w_ref[...], gv, m_ref[...], v_ref[...])
        go_ref[...] = gv

    spec = pl.BlockSpec((None, tile, cols), lambda h, i: (layer, h * tiles + i, 0))
    gspec = pl.BlockSpec((tile, cols), lambda h, i: (i, 0))
    shape = jax.ShapeDtypeStruct((depth, rows, cols), F32)
    extra = list(prev)
    aliases = {5 + j: j for j in range(4)} if len(extra) == 4 else {}
    return pl.pallas_call(
        body, name=name, grid=(2, tiles), in_specs=[spec, gspec, gspec, spec, spec] + [ANY] * len(extra),
        out_specs=[spec] * 4, out_shape=[shape] * 4, input_output_aliases=aliases)(w, g_mine, g_sibling, m, v, *extra)


def _sibling_swap_other(pairs, name):
    nt = len(pairs)

    def body(*refs):
        ins = refs[:2 * nt]
        outs = refs[2 * nt:3 * nt]
        send_sems, recv_sems = refs[3 * nt:]
        x, y, c = _my_place()

        def copy(t, src):
            return pltpu.make_async_remote_copy(src_ref=src, dst_ref=outs[t], send_sem=send_sems.at[t],
                                                recv_sem=recv_sems.at[t], device_id=(x, y, 1 - c), device_id_type=MESH)

        for t in range(nt):
            @pl.when(c == 0)
            def _():
                copy(t, ins[2 * t + 1]).start()

            @pl.when(c == 1)
            def _():
                copy(t, ins[2 * t]).start()
        for t in range(nt):
            copy(t, ins[2 * t]).wait()

    flat = [a for pair in pairs for a in pair]
    return pl.pallas_call(
        body, name=name, in_specs=[ANY] * (2 * nt), out_specs=[ANY] * nt,
        out_shape=[jax.ShapeDtypeStruct(a0.shape, a0.dtype) for a0, _ in pairs],
        scratch_shapes=[pltpu.SemaphoreType.DMA((nt,)), pltpu.SemaphoreType.DMA((nt,))],
    )(*flat)


def _sibling_swap(arrs, name):
    nt = len(arrs)

    def body(*refs):
        ins = refs[:nt]
        outs = refs[nt:2 * nt]
        send_sems, recv_sems = refs[2 * nt:]
        x, y, c = _my_place()
        copies = [pltpu.make_async_remote_copy(src_ref=ins[t], dst_ref=outs[t], send_sem=send_sems.at[t],
                                               recv_sem=recv_sems.at[t], device_id=(x, y, 1 - c), device_id_type=MESH)
                  for t in range(nt)]
        for cp in copies:
            cp.start()
        for cp in copies:
            cp.wait()

    return pl.pallas_call(
        body, name=name, in_specs=[ANY] * nt, out_specs=[ANY] * nt,
        out_shape=[jax.ShapeDtypeStruct(a.shape, a.dtype) for a in arrs],
        scratch_shapes=[pltpu.SemaphoreType.DMA((nt,)), pltpu.SemaphoreType.DMA((nt,))],
    )(*arrs)


def _chip_all_to_all(arrs, name):
    nt = len(arrs)

    def body(*refs):
        ins = refs[:nt]
        outs = refs[nt:2 * nt]
        send_sems, recv_sems, local_sems = refs[2 * nt:]
        x, y, c = _my_place()
        mine = 2 * x + y
        chips = [(1 - x, y), (x, 1 - y), (1 - x, 1 - y)]
        local = [pltpu.make_async_copy(ins[t].at[mine], outs[t].at[mine], local_sems.at[t]) for t in range(nt)]
        for cp in local:
            cp.start()
        sends = []
        for t in range(nt):
            for j, (px, py) in enumerate(chips):
                sends.append(pltpu.make_async_remote_copy(
                    src_ref=ins[t].at[2 * px + py], dst_ref=outs[t].at[mine], send_sem=send_sems.at[t, j],
                    recv_sem=recv_sems.at[t, j], device_id=(px, py, c), device_id_type=MESH))
        for cp in sends:
            cp.start()
        for t in range(nt):
            for j, (px, py) in enumerate(chips):
                pltpu.make_async_remote_copy(
                    src_ref=ins[t].at[mine], dst_ref=outs[t].at[2 * px + py], send_sem=send_sems.at[t, j],
                    recv_sem=recv_sems.at[t, j], device_id=(px, py, c), device_id_type=MESH).wait_recv()
        for cp in sends:
            cp.wait_send()
        for cp in local:
            cp.wait()

    return pl.pallas_call(
        body, name=name, in_specs=[ANY] * nt, out_specs=[ANY] * nt,
        out_shape=[jax.ShapeDtypeStruct(a.shape, a.dtype) for a in arrs],
        scratch_shapes=[pltpu.SemaphoreType.DMA((nt, 3)), pltpu.SemaphoreType.DMA((nt, 3)),
                        pltpu.SemaphoreType.DMA((nt,))],
    )(*arrs)


def _as_rows(a):
    return a.reshape(-1, a.shape[-1])


STREAM_VMEM_BYTES = 16 * 1024 * 1024


def _row_tile(rows, cols, n_arrays):
    lanes = -(-cols // LANES) * LANES
    for t in (512, 256, 128, 64, 32, 16, 8):
        if rows % t == 0 and 2 * n_arrays * t * lanes * 4 <= STREAM_VMEM_BYTES:
            return t
    return rows


def _pair_add(a0, a1, recv, name):
    rows, cols = a0.shape
    tile = _row_tile(rows, cols, 4)

    def body(a0_ref, a1_ref, r_ref, o_ref):
        mine = jnp.where(lax.axis_index("c") == 0, a0_ref[...], a1_ref[...])
        o_ref[...] = (mine.astype(F32) + r_ref[...].astype(F32)).astype(o_ref.dtype)

    spec = pl.BlockSpec((tile, cols), lambda i: (i, 0))
    return pl.pallas_call(body, name=name, grid=(rows // tile,), in_specs=[spec] * 3, out_specs=spec,
                          out_shape=jax.ShapeDtypeStruct((rows, cols), a0.dtype))(a0, a1, recv)


def _sum_leading(a, name):
    n, rows, cols = a.shape
    tile = _row_tile(rows, cols, n + 1)

    def body(a_ref, o_ref):
        acc = a_ref[0].astype(F32)
        for k in range(1, n):
            acc = acc + a_ref[k].astype(F32)
        o_ref[...] = acc

    return pl.pallas_call(
        body, name=name, grid=(rows // tile,), in_specs=[pl.BlockSpec((n, tile, cols), lambda i: (0, i, 0))],
        out_specs=pl.BlockSpec((tile, cols), lambda i: (i, 0)),
        out_shape=jax.ShapeDtypeStruct((rows, cols), F32))(a)


def _adamw_math(w, g, m, v):
    mn = ADAM_B1 * m + (1.0 - ADAM_B1) * g
    vn = ADAM_B2 * v + (1.0 - ADAM_B2) * jnp.square(g)
    m_hat = mn / (1.0 - ADAM_B1 ** ADAM_STEP)
    v_hat = vn / (1.0 - ADAM_B2 ** ADAM_STEP)
    delta = -ADAM_LR * (m_hat / (jnp.sqrt(v_hat) + ADAM_EPS) + ADAM_WD * w)
    return delta, mn, vn


def _adamw_layers(w, g_mine, g_sibling, m, v, name):
    depth, rows, cols = w.shape
    tile = _row_tile(rows, cols, 10)

    def body(w_ref, gm_ref, gs_ref, m_ref, v_ref, d_ref, nm_ref, nv_ref, go_ref):
        gv = jnp.where(pl.program_id(0) == lax.axis_index("c"), gm_ref[...], gs_ref[...])
        d_ref[...], nm_ref[...], nv_ref[...] = _adamw_math(w_ref[...], gv, m_ref[...], v_ref[...])
        go_ref[...] = gv

    spec = pl.BlockSpec((None, tile, cols), lambda l, i: (l, i, 0))
    gspec = pl.BlockSpec((tile, cols), lambda l, i: (i, 0))
    shape = jax.ShapeDtypeStruct((depth, rows, cols), F32)
    return pl.pallas_call(body, name=name, grid=(depth, rows // tile), in_specs=[spec, gspec, gspec, spec, spec],
                          out_specs=[spec] * 4, out_shape=[shape] * 4)(w, g_mine, g_sibling, m, v)


def _adamw_rows(w, g, m, v, name):
    depth, rows, cols = w.shape
    tile = _row_tile(rows, cols, 7)

    def body(w_ref, g_ref, m_ref, v_ref, d_ref, nm_ref, nv_ref):
        d_ref[...], nm_ref[...], nv_ref[...] = _adamw_math(w_ref[...], g_ref[...], m_ref[...], v_ref[...])

    spec = pl.BlockSpec((None, tile, cols), lambda l, i: (l, i, 0))
    shape = jax.ShapeDtypeStruct((depth, rows, cols), F32)
    return pl.pallas_call(body, name=name, grid=(depth, rows // tile), in_specs=[spec] * 4, out_specs=[spec] * 3,
                          out_shape=[shape] * 3)(w, g, m, v)


def _adamw_many(ws, gs, ms, vs, name):
    nt = len(ws)

    def body(*refs):
        for t in range(nt):
            w_ref, g_ref, m_ref, v_ref = (refs[k * nt + t] for k in range(4))
            d_ref, nm_ref, nv_ref = (refs[(4 + k) * nt + t] for k in range(3))
            d_ref[...], nm_ref[...], nv_ref[...] = _adamw_math(w_ref[...], g_ref[...], m_ref[...], v_ref[...])

    shapes = [jax.ShapeDtypeStruct(a.shape, F32) for a in ws]
    out = pl.pallas_call(body, name=name, out_shape=shapes * 3)(*ws, *gs, *ms, *vs)
    return out[:nt], out[nt:2 * nt], out[2 * nt:]


TINY_ROWS_MULTIPLE = 128


def _flat_pack(arrs):
    flat = jnp.concatenate([a.reshape(-1) for a in arrs])
    pad = (-flat.shape[0]) % (TINY_ROWS_MULTIPLE * LANES)
    return jnp.pad(flat, (0, pad)).reshape(-1, LANES)


def _flat_unpack(buf, shapes):
    flat = buf.reshape(-1)
    out = []
    off = 0
    for shp in shapes:
        n = math.prod(shp)
        out.append(flat[off:off + n].reshape(shp))
        off += n
    return out


def _to_chunks(a, name):
    if name == "w_in":
        a = _unpad_in_proj(a)
    rows, cols = a.shape
    if name in COL_SHARDED:
        return a.reshape(rows, 4, cols // 4).transpose(1, 0, 2)
    return a.reshape(4, rows // 4, cols)


def _from_chunks(a, name):
    _, depth, r, cc = a.shape
    if name in COL_SHARDED:
        return a.transpose(1, 2, 0, 3).reshape(depth, r, 4 * cc)
    return a.transpose(1, 0, 2, 3).reshape(depth, 4 * r, cc)


def kernel(x, norm_mix, w_in, s5_lam_re, s5_lam_im, s5_log_step, s5_b_re, s5_b_im, s5_c_re, s5_c_im, s5_d, s5_w_glu, s5_b_glu, s5_norm, ssd_conv_w, ssd_conv_b, ssd_dt_bias, ssd_a_log, ssd_d, ssd_norm, w_out, norm_ffn, w_gate, w_up, w_down, norm_final, loss_target, m_norm_mix, m_w_in, m_s5_lam_re, m_s5_lam_im, m_s5_log_step, m_s5_b_re, m_s5_b_im, m_s5_c_re, m_s5_c_im, m_s5_d, m_s5_w_glu, m_s5_b_glu, m_s5_norm, m_ssd_conv_w, m_ssd_conv_b, m_ssd_dt_bias, m_ssd_a_log, m_ssd_d, m_ssd_norm, m_w_out, m_norm_ffn, m_w_gate, m_w_up, m_w_down, m_norm_final, v_norm_mix, v_w_in, v_s5_lam_re, v_s5_lam_im, v_s5_log_step, v_s5_b_re, v_s5_b_im, v_s5_c_re, v_s5_c_im, v_s5_d, v_s5_w_glu, v_s5_b_glu, v_s5_norm, v_ssd_conv_w, v_ssd_conv_b, v_ssd_dt_bias, v_ssd_a_log, v_ssd_d, v_ssd_norm, v_w_out, v_norm_ffn, v_w_gate, v_w_up, v_w_down, v_norm_final):
    args = dict(locals())
    w = {k: args[k] for k in WEIGHTS}
    m = {k: args["m_" + k] for k in WEIGHTS}
    v = {k: args["v_" + k] for k in WEIGHTS}
    cx, cy, cc = _my_place()
    chip = 2 * cx + cy

    me = 4 * cx + 2 * cy + cc
    others = _other_chips()
    chunk_order = jnp.stack([2 * px + py for px, py in others] + [chip]).astype(jnp.int32)
    stored = lambda k, a: jnp.swapaxes(a, 1, 2) if k in T_STORED else a

    def my_half(k, layer):
        a = stored(k, w[k])[layer]
        return lax.dynamic_slice_in_dim(a, cc * (a.shape[0] // 2), a.shape[0] // 2, 0).astype(BF16)

    def assemble(lands, blocks):
        full = {}
        for k, a, b in zip(BIG, lands, blocks):
            a = lax.dynamic_update_index_in_dim(a, b, me, 0)
            a = a.reshape(4, 2 * a.shape[1], a.shape[2])
            if k in COL_SHARDED:
                full[k] = _pad_in_proj(a.transpose(1, 0, 2).reshape(a.shape[1], 4 * a.shape[2]))
            else:
                full[k] = a.reshape(4 * a.shape[1], a.shape[2])
        return full

    conv_block = w["ssd_conv_w"].reshape(DEPTH * SSD_CONV, -1)
    blocks0 = [my_half(k, 0) for k in BIG] + [conv_block]
    blocks1 = [my_half(k, 1) for k in BIG]
    sems0, kept0, lands0, token0 = _gather_start(blocks0, x, "gather0_start")
    sems1, kept1, lands1, token1 = _gather_start(blocks1, token0, "gather1_start")
    sems0, lands0 = _gather_forward(sems0, kept0, lands0, token1, "gather0_forward")
    lands0 = _gather_finish(sems0, lands0, token1, "gather0_finish")
    big0 = assemble(lands0, blocks0)
    conv_rows = lax.dynamic_update_index_in_dim(lands0[-1], conv_block, me, 0)
    conv_full = conv_rows.reshape(4, 2, DEPTH, SSD_CONV, -1)[:, 0].transpose(1, 2, 0, 3).reshape(
        DEPTH, SSD_CONV, SSD_CONV_DIM)
    small = {k: w[k] for k in LAYER_SMALL}
    small["ssd_conv_w"] = conv_full
    p0 = {k: a[0] for k, a in small.items()}
    p1 = {k: a[1] for k, a in small.items()}

    p0["norm_mix"] = p0["norm_mix"] + token1[0, 0]
    h1, saved0 = _layer_forward(x[0], p0, big0, 0)
    sems1, lands1 = _gather_forward(sems1, kept1, lands1, saved0["xr"], "gather1_forward")
    lands1 = _gather_finish(sems1, lands1, h1, "gather1_finish")
    big1 = assemble(lands1, blocks1)
    h2, saved1 = _layer_forward(h1, p1, big1, 1)
    loss_row, dx, dxb, g_final = _final_loss(h2, w["norm_final"].reshape(1, -1), loss_target[0], "final_loss")
    loss_part, g_final = loss_row[0, 0], g_final[0]

    def halves_view(k, a):
        if k in COL_SHARDED:
            return a.reshape(1, 2, a.shape[0] // 2, a.shape[1])
        return a.reshape(4, 2, a.shape[0] // 8, a.shape[1])

    def to_chunks(k, part):
        if k in COL_SHARDED:
            a = _unpad_in_proj(part[0])
            return a.reshape(a.shape[0], 4, a.shape[1] // 4).transpose(1, 0, 2)
        return part.reshape(4, -1, part.shape[-1])

    def reduce_begin(names, views, tag):
        recv = _swap_halves(views, tag + "swap")
        parts = [_pair_add_halves(a, r, tag + "pair_" + k) for k, a, r in zip(names, views, recv)]
        chunks = [to_chunks(k, p) for k, p in zip(names, parts)]
        return _scatter_start(chunks, tag + "scatter_start")

    def reduce_end(names, handle, after, tag):
        sems, kept, lands, _ = handle
        lands = _scatter_finish(sems, kept, lands, after, tag + "scatter_finish")
        return [_sum_chunks(a, b, chunk_order, tag + "sum_" + k) for k, a, b in zip(names, lands, kept)]

    dx, dxb, g1 = _layer_backward(dx, dxb, p1, big1, saved1, 1)
    round1 = reduce_begin(BIG, [halves_view(k, g1[k]) for k in BIG], "grad1_")
    p0["norm_ffn"] = p0["norm_ffn"] + round1[3][0, 0]
    grad_x, _, g0 = _layer_backward(dx, dxb, p0, big0, saved0, 0)
    g = {k: [g0[k], g1[k]] for k in LAYER_SMALL}
    reduced1 = reduce_end(BIG, round1, grad_x, "grad1_")
    shared1 = _sibling_swap(reduced1, "grad1_share")

    bc_rows = 2 * DEPTH * S5_GROUP * S5_GROUPS
    b_all = jnp.stack([g["s5_b_re"][0], g["s5_b_im"][0], g["s5_b_re"][1], g["s5_b_im"][1]]).reshape(bc_rows, S5_STATE)
    c_all = jnp.stack([g["s5_c_re"][0], g["s5_c_im"][0], g["s5_c_re"][1], g["s5_c_im"][1]]).reshape(bc_rows, S5_STATE)
    names0 = BIG + ("s5_bc",)
    round0 = reduce_begin(names0, [halves_view(k, g0[k]) for k in BIG] + [jnp.stack([b_all, c_all])[None]], "grad0_")

    delta, new_m, new_v, grads = {}, {}, {}, {}
    adam1 = {}
    for k, g_mine, g_sib in zip(BIG, reduced1, shared1):
        adam1[k] = _adamw_layer(stored(k, w[k]), g_mine, g_sib, stored(k, m[k]), stored(k, v[k]), 1, [round0[3]],
                                "adamw1_" + k)
    reduced0 = reduce_end(names0, round0, adam1[BIG[-1]][0], "grad0_")
    shared0 = _sibling_swap(reduced0[:-1], "grad0_share")
    for k, g_mine, g_sib in zip(BIG, reduced0, shared0):
        outs = _adamw_layer(stored(k, w[k]), g_mine, g_sib, stored(k, m[k]), stored(k, v[k]), 0, adam1[k],
                            "adamw0_" + k)
        delta[k], new_m[k], new_v[k], grads[k] = (stored(k, a) for a in outs)
    reduced = reduced0

    tiny_names = TINY + ("norm_final",)
    parts = [jnp.stack(g[k]) for k in TINY] + [g_final, loss_part.reshape(1)]
    shapes = [p.shape for p in parts]
    allparts, bc_eighths = _all_gather8([_flat_pack(parts), reduced[-1]], "gather_small")
    unpacked = _flat_unpack(_sum_leading(allparts, "sum_small"), shapes)
    loss = unpacked[-1][0]
    grads.update(zip(tiny_names, unpacked[:-1]))
    width = SSD_CONV_DIM // 4
    grads["ssd_conv_w"] = lax.dynamic_slice_in_dim(grads["ssd_conv_w"], chip * width, width, axis=2)
    bc = bc_eighths.reshape(4, 2, bc_rows // 4, S5_STATE)
    b_sum = bc[:, 0].reshape(DEPTH, 2, S5_GROUP, S5_GROUPS, S5_STATE)
    c_sum = bc[:, 1].reshape(DEPTH, 2, S5_GROUPS, S5_GROUP, S5_STATE)
    grads["s5_b_re"] = b_sum[:, 0].transpose(0, 2, 3, 1)
    grads["s5_b_im"] = b_sum[:, 1].transpose(0, 2, 3, 1)
    grads["s5_c_re"] = c_sum[:, 0]
    grads["s5_c_im"] = c_sum[:, 1]

    for k in ("s5_b_re", "s5_b_im"):
        shp = w[k].shape
        rows = lambda a: a.reshape(DEPTH, -1, shp[-1])
        d, nm, nv = _adamw_rows(rows(w[k]), rows(grads[k]), rows(m[k]), rows(v[k]), "adamw_" + k)
        delta[k], new_m[k], new_v[k] = d.reshape(shp), nm.reshape(shp), nv.reshape(shp)
    names = tiny_names + ("s5_c_re", "s5_c_im")
    as2d = lambda a: a.reshape(1, -1) if a.ndim == 1 else a
    ds, nms, nvs = _adamw_many([as2d(w[k]) for k in names], [as2d(grads[k]) for k in names],
                               [as2d(m[k]) for k in names], [as2d(v[k]) for k in names], "adamw_small")
    for k, a, b, c in zip(names, ds, nms, nvs):
        delta[k], new_m[k], new_v[k] = (t.reshape(w[k].shape) for t in (a, b, c))

    return (loss, grad_x[None], *[grads[k] for k in WEIGHTS], *[delta[k] for k in WEIGHTS],
            *[new_m[k] for k in WEIGHTS], *[new_v[k] for k in WEIGHTS])
```

```python
import functools
import math

import jax
import jax.numpy as jnp
from jax import lax
from jax.experimental import pallas as pl
from jax.experimental.pallas import tpu as pltpu

F32 = jnp.float32
BF16 = jnp.bfloat16
MESH = pl.DeviceIdType.MESH
ANY = pl.BlockSpec(memory_space=pl.ANY)

D_MODEL = 1024
DEPTH = 2
S5_GROUPS = 64
S5_GROUP = 16
S5_STATE = 64
S5_COLS = S5_GROUPS * S5_STATE
S5_TILE_GROUPS = 8
S5_TILES = S5_GROUPS // S5_TILE_GROUPS
S5_TILE_IN = S5_TILE_GROUPS * S5_GROUP
S5_TILE_ST = S5_TILE_GROUPS * S5_STATE
SEGS = 8
SSD_HEADS = 16
SSD_HEAD_DIM = 64
SSD_GROUPS = 2
SSD_GROUP_HEADS = SSD_HEADS // SSD_GROUPS
SSD_STATE = 128
SSD_CONV = 4
SSD_CHUNK = 128
SSD_WIDTH = 1024
SSD_CONV_DIM = SSD_WIDTH + 2 * SSD_GROUPS * SSD_STATE
IN_PROJ = 3600
IN_MAIN = 3584
IN_PAD = IN_MAIN + 2 * 128
FFN = 2816
EPS = 1e-6
LANES = 128
ROW_TILE = 256

ADAM_LR = 0.001
ADAM_B1 = 0.9
ADAM_B2 = 0.999
ADAM_EPS = 1e-08
ADAM_WD = 0.01
ADAM_STEP = 10

HIGHEST = lax.Precision.HIGHEST


def _sigmoid(x):
    return 1.0 / (1.0 + jnp.exp(-x))


def _silu(x):
    return x * _sigmoid(x)


def _dsilu(x):
    s = _sigmoid(x)
    return s * (1.0 + x * (1.0 - s))


_GELU_K = math.sqrt(2.0 / math.pi)
_GELU_C = 0.044715


def _gelu(x):
    t = jnp.tanh(_GELU_K * (x + _GELU_C * x * x * x))
    return 0.5 * x * (1.0 + t)


def _dgelu(x):
    t = jnp.tanh(_GELU_K * (x + _GELU_C * x * x * x))
    return 0.5 * (1.0 + t) + 0.5 * x * (1.0 - t * t) * _GELU_K * (1.0 + 3.0 * _GELU_C * x * x)


def _softplus(x):
    e = jnp.exp(-jnp.abs(x))
    u = 1.0 + e
    log1p = jnp.where(u == 1.0, e, jnp.log(u) * e / jnp.where(u == 1.0, 1.0, u - 1.0))
    return jnp.maximum(x, 0.0) + log1p


def _rstd(x):
    return lax.rsqrt(jnp.mean(x * x, axis=-1, keepdims=True) + EPS)


def _rms_bwd(x, r, gain, dy):
    dyg = dy * gain
    dx = r * dyg - x * (r * r * r) * jnp.mean(x * dyg, axis=-1, keepdims=True)
    dgain = jnp.sum(dy * x * r, axis=0, keepdims=True)
    return dx, dgain


def _dot(a, b):
    return jnp.dot(a, b, preferred_element_type=F32)


def _dot_nt(a, b):
    return lax.dot_general(a, b, (((1,), (1,)), ((), ())), preferred_element_type=F32)


def _dot_tn(a, b):
    return lax.dot_general(a, b, (((0,), (0,)), ((), ())), preferred_element_type=F32)


def _row_spec(tile, cols):
    return pl.BlockSpec((tile, cols), lambda i: (i, 0))


def _full_spec(shape):
    nd = len(shape)
    return pl.BlockSpec(shape, lambda *_: (0,) * nd)


def _const_spec(shape):
    nd = len(shape)
    return pl.BlockSpec(shape, lambda *_: (0,) * nd, pipeline_mode=pl.Buffered(1))


def _layer_spec(shape, layer, block=0):
    if layer is None:
        return pl.BlockSpec(tuple(shape), lambda *_: (block, 0), pipeline_mode=pl.Buffered(1))
    return pl.BlockSpec((None,) + tuple(shape), lambda *_: (layer, block, 0), pipeline_mode=pl.Buffered(1))


def _acc_rows(ref, val, first):
    @pl.when(first)
    def _():
        ref[...] = val

    @pl.when(jnp.logical_not(first))
    def _():
        ref[...] += val


def _pick_tile(n, cap):
    best = LANES
    for t in range(LANES, cap + 1, LANES):
        if n % t == 0:
            best = t
    return best


def _mm_tn(a, b, name):
    k, m = a.shape
    _, n = b.shape
    tm = _pick_tile(m, 512)
    tn = _pick_tile(n, 1536)

    def body(a_ref, b_ref, o_ref):
        o_ref[...] = _dot_tn(a_ref[...], b_ref[...]).astype(BF16)

    return pl.pallas_call(
        body, name=name, grid=(n // tn, m // tm),
        in_specs=[pl.BlockSpec((k, tm), lambda j, i: (0, i)), pl.BlockSpec((k, tn), lambda j, i: (0, j))],
        out_specs=pl.BlockSpec((tm, tn), lambda j, i: (i, j)),
        out_shape=jax.ShapeDtypeStruct((m, n), BF16),
    )(a, b)


def _rms_inproj(x, gain, w_pad, layer, name):
    L = x.shape[0]

    def body(x_ref, g_ref, w_ref, u_ref, z_ref, xbc_ref, dt_ref, h_ref):
        xv = x_ref[...]
        h = (xv * _rstd(xv) * g_ref[...]).astype(BF16)
        h_ref[...] = h
        p = _dot(h, w_ref[...])
        u_ref[...] = p[:, :1024]
        z_ref[...] = p[:, 1024:2048]
        xbc_ref[...] = p[:, 2048:IN_MAIN]
        dt_ref[...] = p[:, IN_MAIN:IN_PAD]

    return pl.pallas_call(
        body, name=name, grid=(L // ROW_TILE,),
        in_specs=[_row_spec(ROW_TILE, D_MODEL), _full_spec((1, D_MODEL)), _layer_spec((D_MODEL, IN_PAD), layer)],
        out_specs=[_row_spec(ROW_TILE, 1024), _row_spec(ROW_TILE, 1024), _row_spec(ROW_TILE, SSD_CONV_DIM),
                   _row_spec(ROW_TILE, 256), _row_spec(ROW_TILE, D_MODEL)],
        out_shape=[jax.ShapeDtypeStruct((L, 1024), F32), jax.ShapeDtypeStruct((L, 1024), F32),
                   jax.ShapeDtypeStruct((L, SSD_CONV_DIM), F32), jax.ShapeDtypeStruct((L, 256), F32),
                   jax.ShapeDtypeStruct((L, D_MODEL), BF16)],
    )(x, gain, w_pad)


def _s5_prep_math(lr, li, ls, bre, bim):
    step = jnp.exp(ls)
    mag = jnp.exp(lr * step)
    ang = li * step
    are = mag * jnp.cos(ang)
    aim = mag * jnp.sin(ang)
    den = lr * lr + li * li
    nr = are - 1.0
    ni = aim
    cre = (nr * lr + ni * li) / den
    cim = (ni * lr - nr * li) / den
    bbre = cre[None] * bre - cim[None] * bim
    bbim = cre[None] * bim + cim[None] * bre
    return are, aim, bbre, bbim


def _s5_prep(lr, li, ls, bre, bim, name):
    def body(lr_ref, li_ref, ls_ref, bre_ref, bim_ref, are_ref, aim_ref, bbre_ref, bbim_ref):
        are, aim, bbre, bbim = _s5_prep_math(lr_ref[...], li_ref[...], ls_ref[...], bre_ref[...], bim_ref[...])
        are_ref[...] = are
        aim_ref[...] = aim
        bbre_ref[...] = bbre
        bbim_ref[...] = bbim

    gp = jax.ShapeDtypeStruct((S5_GROUPS, S5_STATE), F32)
    hgp = jax.ShapeDtypeStruct((S5_GROUP, S5_GROUPS, S5_STATE), F32)
    return pl.pallas_call(body, name=name, out_shape=[gp, gp, hgp, hgp])(lr, li, ls, bre, bim)


def _s5_prep_bwd(lr, li, ls, bre, bim, dare, daim, dbbre, dbbim, name):
    def body(lr_ref, li_ref, ls_ref, bre_ref, bim_ref, dare_ref, daim_ref, dbbre_ref, dbbim_ref,
             dlr_ref, dli_ref, dls_ref, dbre_ref, dbim_ref):
        _, vjp = jax.vjp(_s5_prep_math, lr_ref[...], li_ref[...], ls_ref[...], bre_ref[...], bim_ref[...])
        dlr, dli, dls, dbre, dbim = vjp((dare_ref[...], daim_ref[...], dbbre_ref[...], dbbim_ref[...]))
        dlr_ref[...] = dlr
        dli_ref[...] = dli
        dls_ref[...] = dls
        dbre_ref[...] = dbre
        dbim_ref[...] = dbim

    gp = jax.ShapeDtypeStruct((S5_GROUPS, S5_STATE), F32)
    g1 = jax.ShapeDtypeStruct((S5_GROUPS, 1), F32)
    hgp = jax.ShapeDtypeStruct((S5_GROUP, S5_GROUPS, S5_STATE), F32)
    return pl.pallas_call(body, name=name, out_shape=[gp, gp, g1, hgp, hgp])(
        lr, li, ls, bre, bim, dare, daim, dbbre, dbbim)


def _cmul_add(ar, ai, sr, si, br, bi):
    return ar * sr - ai * si + br, ar * si + ai * sr + bi


def _shift_rows_down(v):
    rolled = pltpu.roll(v, 1, 0)
    row = lax.broadcasted_iota(jnp.int32, v.shape, 0)
    return jnp.where(row == 0, 0.0, rolled)


def _shift_rows_up(v):
    rolled = pltpu.roll(v, SEGS - 1, 0)
    row = lax.broadcasted_iota(jnp.int32, v.shape, 0)
    return jnp.where(row == SEGS - 1, 0.0, rolled)


def _segment_power(ar, ai, steps):
    n = 1
    while n < steps:
        ar, ai = ar * ar - ai * ai, 2.0 * ar * ai
        n *= 2
    assert n == steps
    return ar, ai


def _segment_entries(ar, ai, fr, fi, steps, shift):
    pr, pi = _segment_power(ar, ai, steps)
    er = jnp.zeros_like(fr)
    ei = jnp.zeros_like(fi)
    for _ in range(SEGS - 1):
        nr, ni = _cmul_add(pr, pi, er, ei, fr, fi)
        er, ei = shift(nr), shift(ni)
    return er, ei


def _s5_scan(u_perm, bre_bd, bim_bd, cre_bd, cim_bd, are, aim, name):
    L = u_perm.shape[0]
    steps = L // SEGS

    def body(u_ref, bre_ref, bim_ref, cre_ref, cim_ref, are_ref, aim_ref, y_ref, xr_ref, xi_ref):
        u = u_ref[...].astype(BF16)
        xr_ref[...] = _dot(u, bre_ref[0])
        xi_ref[...] = _dot(u, bim_ref[0])
        ar = jnp.broadcast_to(are_ref[0], (SEGS, S5_TILE_ST))
        ai = jnp.broadcast_to(aim_ref[0], (SEGS, S5_TILE_ST))
        zero = jnp.zeros((SEGS, S5_TILE_ST), F32)

        def finals(j, c):
            rows = pl.ds(pl.multiple_of(j * SEGS, SEGS), SEGS)
            return _cmul_add(ar, ai, c[0], c[1], xr_ref[rows, :], xi_ref[rows, :])

        fr, fi = lax.fori_loop(0, steps, finals, (zero, zero), unroll=4)
        er, ei = _segment_entries(ar, ai, fr, fi, steps, _shift_rows_down)

        def scan(j, c):
            rows = pl.ds(pl.multiple_of(j * SEGS, SEGS), SEGS)
            sr, si = _cmul_add(ar, ai, c[0], c[1], xr_ref[rows, :], xi_ref[rows, :])
            xr_ref[rows, :] = sr
            xi_ref[rows, :] = si
            return sr, si

        lax.fori_loop(0, steps, scan, (er, ei), unroll=4)
        y_ref[...] = (_dot(xr_ref[...].astype(BF16), cre_ref[0]) - _dot(xi_ref[...].astype(BF16), cim_ref[0]))

    tile3 = lambda a, b: pl.BlockSpec((1, a, b), lambda k: (k, 0, 0))
    return pl.pallas_call(
        body, name=name, grid=(S5_TILES,),
        in_specs=[pl.BlockSpec((L, S5_TILE_IN), lambda k: (0, k)),
                  tile3(S5_TILE_IN, S5_TILE_ST), tile3(S5_TILE_IN, S5_TILE_ST),
                  tile3(S5_TILE_ST, S5_TILE_IN), tile3(S5_TILE_ST, S5_TILE_IN),
                  tile3(1, S5_TILE_ST), tile3(1, S5_TILE_ST)],
        out_specs=[pl.BlockSpec((L, S5_TILE_IN), lambda k: (0, k)),
                   pl.BlockSpec((L, S5_TILE_ST), lambda k: (0, k)), pl.BlockSpec((L, S5_TILE_ST), lambda k: (0, k))],
        out_shape=[jax.ShapeDtypeStruct((L, 1024), F32), jax.ShapeDtypeStruct((L, S5_COLS), F32),
                   jax.ShapeDtypeStruct((L, S5_COLS), F32)],
    )(u_perm, bre_bd, bim_bd, cre_bd, cim_bd, are, aim)


def _s5_scan_bwd(dy_perm, u_perm, xr, xi, bret_bd, bimt_bd, cret_bd, cimt_bd, are, aim, name):
    L = u_perm.shape[0]
    steps = L // SEGS

    def body(dy_ref, u_ref, xr_ref, xi_ref, bret_ref, bimt_ref, cret_ref, cimt_ref, are_ref, aim_ref,
             du_ref, dar_ref, dai_ref, dcre_ref, dcim_ref, dbre_ref, dbim_ref, gr_ref, gi_ref):
        dy = dy_ref[...].astype(BF16)
        u = u_ref[...].astype(BF16)
        gr_ref[...] = _dot(dy, cret_ref[0])
        gi_ref[...] = -_dot(dy, cimt_ref[0])
        ar = jnp.broadcast_to(are_ref[0], (SEGS, S5_TILE_ST))
        ai = -jnp.broadcast_to(aim_ref[0], (SEGS, S5_TILE_ST))
        zero = jnp.zeros((SEGS, S5_TILE_ST), F32)

        def finals(k, c):
            rows = pl.ds(pl.multiple_of((steps - 1 - k) * SEGS, SEGS), SEGS)
            return _cmul_add(ar, ai, c[0], c[1], gr_ref[rows, :], gi_ref[rows, :])

        fr, fi = lax.fori_loop(0, steps, finals, (zero, zero), unroll=4)
        er, ei = _segment_entries(ar, ai, fr, fi, steps, _shift_rows_up)

        def scan(k, c):
            sr0, si0, accr, acci = c
            j = steps - 1 - k
            rows = pl.ds(pl.multiple_of(j * SEGS, SEGS), SEGS)
            sr, si = _cmul_add(ar, ai, sr0, si0, gr_ref[rows, :], gi_ref[rows, :])
            gr_ref[rows, :] = sr
            gi_ref[rows, :] = si
            prev = pl.ds(pl.multiple_of(jnp.maximum(j - 1, 0) * SEGS, SEGS), SEGS)
            live = (j > 0).astype(F32)
            xpr = xr_ref[prev, :] * live
            xpi = xi_ref[prev, :] * live
            return sr, si, accr + sr * xpr + si * xpi, acci + si * xpr - sr * xpi

        _, _, accr, acci = lax.fori_loop(0, steps, scan, (er, ei, zero, zero), unroll=4)
        first = pl.ds(0, SEGS)
        last = pl.ds((steps - 1) * SEGS, SEGS)
        xpr = _shift_rows_down(xr_ref[last, :])
        xpi = _shift_rows_down(xi_ref[last, :])
        g0r = gr_ref[first, :]
        g0i = gi_ref[first, :]
        accr = accr + g0r * xpr + g0i * xpi
        acci = acci + g0i * xpr - g0r * xpi
        dar_ref[0] = jnp.sum(accr, axis=0, keepdims=True)
        dai_ref[0] = jnp.sum(acci, axis=0, keepdims=True)

        grb = gr_ref[...].astype(BF16)
        gib = gi_ref[...].astype(BF16)
        du_ref[...] = _dot(grb, bret_ref[0]) + _dot(gib, bimt_ref[0])
        dbre_ref[0] = _dot_tn(u, grb)
        dbim_ref[0] = _dot_tn(u, gib)
        dcre_ref[0] = _dot_tn(dy, xr_ref[...].astype(BF16))
        dcim_ref[0] = -_dot_tn(dy, xi_ref[...].astype(BF16))

    tile3 = lambda a, b: pl.BlockSpec((1, a, b), lambda k: (k, 0, 0))
    col_in = pl.BlockSpec((L, S5_TILE_IN), lambda k: (0, k))
    col_st = pl.BlockSpec((L, S5_TILE_ST), lambda k: (0, k))
    dense = jax.ShapeDtypeStruct((S5_TILES, S5_TILE_IN, S5_TILE_ST), F32)
    vec = jax.ShapeDtypeStruct((S5_TILES, 1, S5_TILE_ST), F32)
    return pl.pallas_call(
        body, name=name, grid=(S5_TILES,),
        in_specs=[col_in, col_in, col_st, col_st,
                  tile3(S5_TILE_ST, S5_TILE_IN), tile3(S5_TILE_ST, S5_TILE_IN),
                  tile3(S5_TILE_IN, S5_TILE_ST), tile3(S5_TILE_IN, S5_TILE_ST),
                  tile3(1, S5_TILE_ST), tile3(1, S5_TILE_ST)],
        out_specs=[col_in, tile3(1, S5_TILE_ST), tile3(1, S5_TILE_ST),
                   tile3(S5_TILE_IN, S5_TILE_ST), tile3(S5_TILE_IN, S5_TILE_ST),
                   tile3(S5_TILE_IN, S5_TILE_ST), tile3(S5_TILE_IN, S5_TILE_ST)],
        out_shape=[jax.ShapeDtypeStruct((L, 1024), F32), vec, vec, dense, dense, dense, dense],
        scratch_shapes=[pltpu.VMEM((L, S5_TILE_ST), F32), pltpu.VMEM((L, S5_TILE_ST), F32)],
    )(dy_perm, u_perm, xr, xi, bret_bd, bimt_bd, cret_bd, cimt_bd, are, aim)


def _s5_post(ys, u, d_skip, w_glu, b_glu, gain, layer, name):
    L = ys.shape[0]

    def body(ys_ref, u_ref, d_ref, w_ref, b_ref, g_ref, ya_ref):
        g = _gelu(ys_ref[...] + d_ref[...] * u_ref[...])
        q = _dot(g.astype(BF16), w_ref[...]) + b_ref[...]
        oa = g * _sigmoid(q)
        ya_ref[...] = (oa * _rstd(oa) * g_ref[...]).astype(BF16)

    vec = _full_spec((1, 1024))
    return pl.pallas_call(
        body, name=name, grid=(L // ROW_TILE,),
        in_specs=[_row_spec(ROW_TILE, 1024), _row_spec(ROW_TILE, 1024), vec, _layer_spec((1024, 1024), layer), vec,
                  vec],
        out_specs=_row_spec(ROW_TILE, 1024),
        out_shape=jax.ShapeDtypeStruct((L, 1024), BF16),
    )(ys, u, d_skip, w_glu, b_glu, gain)


def _s5_post_bwd(dx, w_out, ys, u, d_skip, w_glu, b_glu, gain, layer, name):
    L = ys.shape[0]

    def body(dx_ref, wo_ref, ys_ref, u_ref, d_ref, w_ref, b_ref, gn_ref,
             dys_ref, dus_ref, g_ref, dq_ref, dgain_ref, dd_ref, db_ref):
        first = pl.program_id(0) == 0
        uv = u_ref[...]
        yt = ys_ref[...] + d_ref[...] * uv
        g = _gelu(yt)
        gb = g.astype(BF16)
        q = _dot(gb, w_ref[...]) + b_ref[...]
        s = _sigmoid(q)
        oa = g * s
        dya = _dot_nt(dx_ref[...], wo_ref[...])
        doa, dgain = _rms_bwd(oa, _rstd(oa), gn_ref[...], dya)
        dq = doa * g * s * (1.0 - s)
        dqb = dq.astype(BF16)
        dg = doa * s + _dot_nt(dqb, w_ref[...])
        dyt = dg * _dgelu(yt)
        dys_ref[...] = dyt
        dus_ref[...] = dyt * d_ref[...]
        g_ref[...] = gb
        dq_ref[...] = dqb
        _acc_rows(dgain_ref, dgain, first)
        _acc_rows(dd_ref, jnp.sum(dyt * uv, axis=0, keepdims=True), first)
        _acc_rows(db_ref, jnp.sum(dq, axis=0, keepdims=True), first)

    vec = _full_spec((1, 1024))
    row = _row_spec(ROW_TILE, 1024)
    vshape = jax.ShapeDtypeStruct((1, 1024), F32)
    return pl.pallas_call(
        body, name=name, grid=(L // ROW_TILE,),
        in_specs=[row, _layer_spec((1024, 1024), layer, 0), row, row, vec, _layer_spec((1024, 1024), layer), vec,
                  vec],
        out_specs=[row, row, row, row, vec, vec, vec],
        out_shape=[jax.ShapeDtypeStruct((L, 1024), F32), jax.ShapeDtypeStruct((L, 1024), F32),
                   jax.ShapeDtypeStruct((L, 1024), BF16), jax.ShapeDtypeStruct((L, 1024), BF16),
                   vshape, vshape, vshape],
    )(dx, w_out, ys, u, d_skip, w_glu, b_glu, gain)


CONV_TILE = 256


def _shift_time(v, d):
    if d == 0:
        return v
    rolled = pltpu.roll(v, d, 0)
    row = lax.broadcasted_iota(jnp.int32, v.shape, 0)
    return jnp.where(row < d, 0.0, rolled)


def _unshift_time(v, d):
    if d == 0:
        return v
    n = v.shape[0]
    rolled = pltpu.roll(v, n - d, 0)
    row = lax.broadcasted_iota(jnp.int32, v.shape, 0)
    return jnp.where(row >= n - d, 0.0, rolled)


def _ssd_conv(xbc, w, b, name):
    L = xbc.shape[0]

    def body(x_ref, w_ref, b_ref, o_ref):
        xv = x_ref[...]
        pre = jnp.broadcast_to(b_ref[...], xv.shape)
        for k in range(SSD_CONV):
            pre = pre + w_ref[k:k + 1, :] * _shift_time(xv, SSD_CONV - 1 - k)
        o_ref[...] = _silu(pre)

    col = pl.BlockSpec((L, CONV_TILE), lambda j: (0, j))
    return pl.pallas_call(
        body, name=name, grid=(SSD_CONV_DIM // CONV_TILE,),
        in_specs=[col, pl.BlockSpec((8, CONV_TILE), lambda j: (0, j)), pl.BlockSpec((1, CONV_TILE), lambda j: (0, j))],
        out_specs=col, out_shape=jax.ShapeDtypeStruct((L, SSD_CONV_DIM), F32),
    )(xbc, w, b)


def _ssd_conv_bwd(dxc, xbc, w, b, name):
    L = xbc.shape[0]

    def body(d_ref, x_ref, w_ref, b_ref, dx_ref, dw_ref, db_ref):
        xv = x_ref[...]
        shifted = [_shift_time(xv, SSD_CONV - 1 - k) for k in range(SSD_CONV)]
        pre = jnp.broadcast_to(b_ref[...], xv.shape)
        for k in range(SSD_CONV):
            pre = pre + w_ref[k:k + 1, :] * shifted[k]
        dpre = d_ref[...] * _dsilu(pre)
        dx = jnp.zeros_like(xv)
        rows = []
        for k in range(SSD_CONV):
            dx = dx + w_ref[k:k + 1, :] * _unshift_time(dpre, SSD_CONV - 1 - k)
            rows.append(jnp.sum(dpre * shifted[k], axis=0, keepdims=True))
        dx_ref[...] = dx
        dw_ref[...] = jnp.concatenate(rows + [jnp.zeros((8 - SSD_CONV, CONV_TILE), F32)], axis=0)
        db_ref[...] = jnp.sum(dpre, axis=0, keepdims=True)

    col = pl.BlockSpec((L, CONV_TILE), lambda j: (0, j))
    w_spec = pl.BlockSpec((8, CONV_TILE), lambda j: (0, j))
    b_spec = pl.BlockSpec((1, CONV_TILE), lambda j: (0, j))
    return pl.pallas_call(
        body, name=name, grid=(SSD_CONV_DIM // CONV_TILE,),
        in_specs=[col, col, w_spec, b_spec], out_specs=[col, w_spec, b_spec],
        out_shape=[jax.ShapeDtypeStruct((L, SSD_CONV_DIM), F32), jax.ShapeDtypeStruct((8, SSD_CONV_DIM), F32),
                   jax.ShapeDtypeStruct((1, SSD_CONV_DIM), F32)],
    )(dxc, xbc, w, b)


def _tri(lower):
    r = lax.broadcasted_iota(jnp.int32, (SSD_CHUNK, SSD_CHUNK), 0)
    c = lax.broadcasted_iota(jnp.int32, (SSD_CHUNK, SSD_CHUNK), 1)
    return (r >= c) if lower else (r <= c)


def _ssd_chunk_common(dt_ref, bias_ref, alog_ref):
    pre = dt_ref[...] + bias_ref[0]
    dtp = _softplus(pre)
    a_neg = -jnp.exp(alog_ref[0])
    dta = dtp * a_neg
    ltri = _tri(True).astype(F32)
    acum = jnp.dot(ltri, dta, precision=HIGHEST, preferred_element_type=F32)
    return pre, dtp, a_neg, dta, acum


GROUP_W = SSD_GROUP_HEADS * SSD_HEAD_DIM


def _head_expander():
    r = lax.broadcasted_iota(jnp.int32, (LANES, GROUP_W), 0)
    c = lax.broadcasted_iota(jnp.int32, (LANES, GROUP_W), 1)
    return (c // SSD_HEAD_DIM == r).astype(F32)


def _dot_exact(a, b):
    return jnp.dot(a, b, precision=HIGHEST, preferred_element_type=F32)


def _decay_mask(acum_all, acum_t, h, lower):
    seg = acum_all[:, h:h + 1] - acum_t[h:h + 1, :]
    return jnp.where(lower, jnp.exp(jnp.minimum(seg, 0.0)), 0.0)


def _ssd_scan(xc, dt, dt_bias, a_log, d_wide, expand, expand_t, name):
    L = xc.shape[0]
    nc = L // SSD_CHUNK

    def body(x_ref, b_ref, c_ref, dt_ref, bias_ref, alog_ref, d_ref, e_ref, et_ref, y_ref, sp_ref, s_ref, xdt_ref):
        @pl.when(pl.program_id(1) == 0)
        def _():
            s_ref[...] = jnp.zeros_like(s_ref)

        _, dtp_all, _, _, acum_all = _ssd_chunk_common(dt_ref, bias_ref, alog_ref)
        acum_t = acum_all.T
        e = e_ref[...]
        acum_e = _dot_exact(acum_all, e)
        alast_e = acum_e[SSD_CHUNK - 1:SSD_CHUNK, :]
        x = x_ref[...]
        xdt = x * _dot_exact(dtp_all, e)
        xdt_ref[...] = xdt.astype(BF16)
        bm = b_ref[...].astype(BF16)
        cm = c_ref[...].astype(BF16)
        cb = _dot_nt(cm, bm)
        lower = _tri(True)
        sp = s_ref[...]
        for h in range(SSD_GROUP_HEADS):
            cols = slice(h * SSD_HEAD_DIM, (h + 1) * SSD_HEAD_DIM)
            lm = _decay_mask(acum_all, acum_t, h, lower)
            y_ref[:, cols] = _dot((cb * lm).astype(BF16), xdt_ref[:, cols])
        y_ref[...] += _dot_nt(cm, sp.astype(BF16)) * jnp.exp(acum_e) + d_ref[0] * x
        wgt = xdt * jnp.exp(alast_e - acum_e)
        ealast = jnp.exp(_dot_exact(et_ref[...], acum_t)[:, SSD_CHUNK - 1:SSD_CHUNK])
        sp_ref[0, 0] = sp
        s_ref[...] = ealast * sp + _dot_tn(wgt.astype(BF16), bm)

    par = lambda n: pl.BlockSpec((1, 1, n), lambda g, c: (g, 0, 0))
    return pl.pallas_call(
        body, name=name, grid=(SSD_GROUPS, nc),
        in_specs=[pl.BlockSpec((SSD_CHUNK, GROUP_W), lambda g, c: (c, g)),
                  pl.BlockSpec((SSD_CHUNK, SSD_STATE), lambda g, c: (c, 8 + g)),
                  pl.BlockSpec((SSD_CHUNK, SSD_STATE), lambda g, c: (c, 10 + g)),
                  pl.BlockSpec((SSD_CHUNK, LANES), lambda g, c: (c, g)),
                  par(LANES), par(LANES), par(GROUP_W), _full_spec((LANES, GROUP_W)), _full_spec((GROUP_W, LANES))],
        out_specs=[pl.BlockSpec((SSD_CHUNK, GROUP_W), lambda g, c: (c, g)),
                   pl.BlockSpec((1, 1, GROUP_W, SSD_STATE), lambda g, c: (c, g, 0, 0))],
        out_shape=[jax.ShapeDtypeStruct((L, SSD_WIDTH), F32),
                   jax.ShapeDtypeStruct((nc, SSD_GROUPS, GROUP_W, SSD_STATE), F32)],
        scratch_shapes=[pltpu.VMEM((GROUP_W, SSD_STATE), F32), pltpu.VMEM((SSD_CHUNK, GROUP_W), BF16)],
    )(xc, xc, xc, dt, dt_bias, a_log, d_wide, expand, expand_t)


def _ssd_scan_bwd(dy, xc, dt, sprev, dt_bias, a_log, d_wide, expand, expand_t, name):
    L = xc.shape[0]
    nc = L // SSD_CHUNK

    def body(dy_ref, x_ref, b_ref, c_ref, dt_ref, sp_ref, bias_ref, alog_ref, d_ref, e_ref, et_ref,
             dx_ref, db_ref, dc_ref, ddt_ref, dbias_ref, dalog_ref, dd_ref, ds_ref, xdt_ref, dyb_ref):
        first = pl.program_id(1) == 0

        @pl.when(first)
        def _():
            ds_ref[...] = jnp.zeros_like(ds_ref)

        pre, dtp_all, a_neg, _, acum_all = _ssd_chunk_common(dt_ref, bias_ref, alog_ref)
        acum_t = acum_all.T
        e = e_ref[...]
        et = et_ref[...]
        acum_e = _dot_exact(acum_all, e)
        dtp_e = _dot_exact(dtp_all, e)
        alast_e = acum_e[SSD_CHUNK - 1:SSD_CHUNK, :]
        dstate_e = jnp.exp(alast_e - acum_e)
        x = x_ref[...]
        dy = dy_ref[...]
        xdt = x * dtp_e
        xdt_ref[...] = xdt.astype(BF16)
        dyb_ref[...] = dy.astype(BF16)
        bm = b_ref[...].astype(BF16)
        cm = c_ref[...].astype(BF16)
        cb = _dot_nt(cm, bm)
        sp = sp_ref[0, 0]
        spb = sp.astype(BF16)
        dsn = ds_ref[...]
        dsb = dsn.astype(BF16)
        z = _dot_nt(cm, spb)
        dz = dy * jnp.exp(acum_e)
        dzb = dz.astype(BF16)
        dc_acc = _dot(dzb, spb)
        ealast = jnp.exp(_dot_exact(et, acum_t)[:, SSD_CHUNK - 1:SSD_CHUNK])
        ds_ref[...] = _dot_tn(dzb, cm) + ealast * dsn
        dw = _dot_nt(bm, dsb)
        wgt = xdt * dstate_e
        db_acc = _dot(wgt.astype(BF16), dsb)
        lower = _tri(True)
        lane = lax.broadcasted_iota(jnp.int32, (SSD_CHUNK, LANES), 1)
        row = lax.broadcasted_iota(jnp.int32, (SSD_CHUNK, LANES), 0)
        dcb = jnp.zeros((SSD_CHUNK, SSD_CHUNK), F32)
        dacum_all = jnp.zeros((SSD_CHUNK, LANES), F32)
        dacum_cols = jnp.zeros((SSD_CHUNK, LANES), F32)
        for h in range(SSD_GROUP_HEADS):
            cols = slice(h * SSD_HEAD_DIM, (h + 1) * SSD_HEAD_DIM)
            lm = _decay_mask(acum_all, acum_t, h, lower)
            dm = _dot_nt(dyb_ref[:, cols], xdt_ref[:, cols])
            dx_ref[:, cols] = _dot_tn((cb * lm).astype(BF16), dyb_ref[:, cols])
            dm_lm = dm * lm
            dcb = dcb + dm_lm
            q = dm_lm * cb
            dacum_all = jnp.where(lane == h, jnp.sum(q, axis=1, keepdims=True), dacum_all)
            dacum_cols = jnp.where(row == h, jnp.sum(q, axis=0, keepdims=True), dacum_cols)
        dxdt = dx_ref[...] + dw * dstate_e
        sums = _dot_exact(jnp.concatenate([dz * z, dw * wgt, dxdt * x, dy * x], axis=0), et)
        dacum_off = sums[0:SSD_CHUNK]
        dds_ds = sums[SSD_CHUNK:2 * SSD_CHUNK]
        ddtp_x = sums[2 * SSD_CHUNK:3 * SSD_CHUNK]
        dd_part = sums[3 * SSD_CHUNK:4 * SSD_CHUNK]
        ds_s = jnp.sum(_dot_exact(e, dsn * sp).T, axis=0, keepdims=True)
        dalast = ds_s * jnp.exp(acum_all[SSD_CHUNK - 1:SSD_CHUNK, :]) + jnp.sum(dds_ds, axis=0, keepdims=True)
        dacum_all = dacum_all - dacum_cols.T + dacum_off - dds_ds + jnp.where(row == SSD_CHUNK - 1, dalast, 0.0)
        dx_ref[...] = d_ref[0] * dy + dxdt * dtp_e
        dcbb = dcb.astype(BF16)
        dc_ref[...] = dc_acc + _dot(dcbb, bm)
        db_ref[...] = db_acc + _dot_tn(dcbb, cm)
        utri = _tri(False).astype(F32)
        ddta = _dot_exact(utri, dacum_all)
        ddt = (ddtp_x + ddta * a_neg) * _sigmoid(pre)
        ddt_ref[...] = ddt
        _acc_rows(dbias_ref, jnp.sum(ddt, axis=0, keepdims=True)[None], first)
        _acc_rows(dalog_ref, (jnp.sum(ddta * dtp_all, axis=0, keepdims=True) * a_neg)[None], first)
        _acc_rows(dd_ref, jnp.sum(dd_part, axis=0, keepdims=True)[None], first)

    rev = lambda c: nc - 1 - c
    par = lambda n: pl.BlockSpec((1, 1, n), lambda g, c: (g, 0, 0))
    pshape = jax.ShapeDtypeStruct((SSD_GROUPS, 1, LANES), F32)
    return pl.pallas_call(
        body, name=name, grid=(SSD_GROUPS, nc),
        in_specs=[pl.BlockSpec((SSD_CHUNK, GROUP_W), lambda g, c: (rev(c), g)),
                  pl.BlockSpec((SSD_CHUNK, GROUP_W), lambda g, c: (rev(c), g)),
                  pl.BlockSpec((SSD_CHUNK, SSD_STATE), lambda g, c: (rev(c), 8 + g)),
                  pl.BlockSpec((SSD_CHUNK, SSD_STATE), lambda g, c: (rev(c), 10 + g)),
                  pl.BlockSpec((SSD_CHUNK, LANES), lambda g, c: (rev(c), g)),
                  pl.BlockSpec((1, 1, GROUP_W, SSD_STATE), lambda g, c: (rev(c), g, 0, 0)),
                  par(LANES), par(LANES), par(GROUP_W), _full_spec((LANES, GROUP_W)), _full_spec((GROUP_W, LANES))],
        out_specs=[pl.BlockSpec((SSD_CHUNK, GROUP_W), lambda g, c: (rev(c), g)),
                   pl.BlockSpec((SSD_CHUNK, SSD_STATE), lambda g, c: (rev(c), g)),
                   pl.BlockSpec((SSD_CHUNK, SSD_STATE), lambda g, c: (rev(c), g)),
                   pl.BlockSpec((SSD_CHUNK, LANES), lambda g, c: (rev(c), g)),
                   par(LANES), par(LANES), par(LANES)],
        out_shape=[jax.ShapeDtypeStruct((L, SSD_WIDTH), F32), jax.ShapeDtypeStruct((L, 256), F32),
                   jax.ShapeDtypeStruct((L, 256), F32), jax.ShapeDtypeStruct((L, 256), F32),
                   pshape, pshape, pshape],
        scratch_shapes=[pltpu.VMEM((GROUP_W, SSD_STATE), F32), pltpu.VMEM((SSD_CHUNK, GROUP_W), BF16),
                        pltpu.VMEM((SSD_CHUNK, GROUP_W), BF16)],
    )(dy, xc, xc, xc, dt, sprev, dt_bias, a_log, d_wide, expand, expand_t)


def _ssd_post(y, z, gain, name):
    L = y.shape[0]

    def body(y_ref, z_ref, g_ref, o_ref):
        ob = y_ref[...] * _silu(z_ref[...])
        o_ref[...] = (ob * _rstd(ob) * g_ref[...]).astype(BF16)

    row = _row_spec(ROW_TILE, 1024)
    return pl.pallas_call(body, name=name, grid=(L // ROW_TILE,), in_specs=[row, row, _full_spec((1, 1024))],
                          out_specs=row, out_shape=jax.ShapeDtypeStruct((L, 1024), BF16))(y, z, gain)


def _ssd_post_bwd(dx, w_out, y, z, gain, layer, name):
    L = y.shape[0]

    def body(dx_ref, wo_ref, y_ref, z_ref, g_ref, dy_ref, dz_ref, dgain_ref):
        first = pl.program_id(0) == 0
        yv = y_ref[...]
        zv = z_ref[...]
        sz = _silu(zv)
        ob = yv * sz
        dyb = _dot_nt(dx_ref[...], wo_ref[...])
        dob, dgain = _rms_bwd(ob, _rstd(ob), g_ref[...], dyb)
        dy_ref[...] = dob * sz
        dz_ref[...] = dob * yv * _dsilu(zv)
        _acc_rows(dgain_ref, dgain, first)

    row = _row_spec(ROW_TILE, 1024)
    vec = _full_spec((1, 1024))
    return pl.pallas_call(
        body, name=name, grid=(L // ROW_TILE,),
        in_specs=[row, _layer_spec((1024, 1024), layer, 1), row, row, vec],
        out_specs=[row, row, vec],
        out_shape=[jax.ShapeDtypeStruct((L, 1024), F32), jax.ShapeDtypeStruct((L, 1024), F32),
                   jax.ShapeDtypeStruct((1, 1024), F32)],
    )(dx, w_out, y, z, gain)


def _out_proj(x, ya, yb, w_out, layer, name):
    L = x.shape[0]

    def body(x_ref, ya_ref, yb_ref, w_ref, o_ref):
        o_ref[...] = x_ref[...] + _dot(ya_ref[...], w_ref[:1024, :]) + _dot(yb_ref[...], w_ref[1024:, :])

    row = _row_spec(ROW_TILE, 1024)
    return pl.pallas_call(body, name=name, grid=(L // ROW_TILE,),
                          in_specs=[row, row, row, _layer_spec((2048, 1024), layer)],
                          out_specs=row, out_shape=jax.ShapeDtypeStruct((L, D_MODEL), F32))(x, ya, yb, w_out)


def _ffn(x, gain, w_gate, w_up, w_down, layer, name):
    L = x.shape[0]

    def body(x_ref, g_ref, wg_ref, wu_ref, wd_ref, o_ref, gt_ref, up_ref):
        xv = x_ref[...]
        h = (xv * _rstd(xv) * g_ref[...]).astype(BF16)
        gt = _dot_nt(h, wg_ref[...])
        up = _dot_nt(h, wu_ref[...])
        gt_ref[...] = gt
        up_ref[...] = up
        o_ref[...] = xv + _dot((_silu(gt) * up).astype(BF16), wd_ref[...])

    row = _row_spec(ROW_TILE, D_MODEL)
    hid = _row_spec(ROW_TILE, FFN)
    return pl.pallas_call(
        body, name=name, grid=(L // ROW_TILE,),
        in_specs=[row, _full_spec((1, D_MODEL)), _layer_spec((FFN, D_MODEL), layer),
                  _layer_spec((FFN, D_MODEL), layer), _layer_spec((FFN, D_MODEL), layer)],
        out_specs=[row, hid, hid],
        out_shape=[jax.ShapeDtypeStruct((L, D_MODEL), F32), jax.ShapeDtypeStruct((L, FFN), F32),
                   jax.ShapeDtypeStruct((L, FFN), F32)],
    )(x, gain, w_gate, w_up, w_down)


def _ffn_bwd(dx2, x1, gt, up, gain, w_gate, w_up, w_down, layer, name):
    L = x1.shape[0]

    def body(d_ref, x_ref, gt_ref, up_ref, g_ref, wg_ref, wu_ref, wd_ref,
             dx_ref, dxb_ref, h_ref, act_ref, dgt_ref, dup_ref, dgain_ref):
        first = pl.program_id(0) == 0
        dv = d_ref[...]
        xv = x_ref[...]
        r = _rstd(xv)
        h_ref[...] = (xv * r * g_ref[...]).astype(BF16)
        gtv = gt_ref[...]
        upv = up_ref[...]
        sg = _silu(gtv)
        act_ref[...] = (sg * upv).astype(BF16)
        dact = _dot_nt(dv.astype(BF16), wd_ref[...])
        dgt = (dact * upv * _dsilu(gtv)).astype(BF16)
        dup = (dact * sg).astype(BF16)
        dgt_ref[...] = dgt
        dup_ref[...] = dup
        dh = _dot(dgt, wg_ref[...]) + _dot(dup, wu_ref[...])
        dxn, dgain = _rms_bwd(xv, r, g_ref[...], dh)
        dx = dv + dxn
        dx_ref[...] = dx
        dxb_ref[...] = dx.astype(BF16)
        _acc_rows(dgain_ref, dgain, first)

    row = _row_spec(ROW_TILE, D_MODEL)
    hid = _row_spec(ROW_TILE, FFN)
    vec = _full_spec((1, D_MODEL))
    return pl.pallas_call(
        body, name=name, grid=(L // ROW_TILE,),
        in_specs=[row, row, hid, hid, vec, _layer_spec((FFN, D_MODEL), layer), _layer_spec((FFN, D_MODEL), layer),
                  _layer_spec((FFN, D_MODEL), layer)],
        out_specs=[row, row, row, hid, hid, hid, vec],
        out_shape=[jax.ShapeDtypeStruct((L, D_MODEL), F32), jax.ShapeDtypeStruct((L, D_MODEL), BF16),
                   jax.ShapeDtypeStruct((L, D_MODEL), BF16),
                   jax.ShapeDtypeStruct((L, FFN), BF16), jax.ShapeDtypeStruct((L, FFN), BF16),
                   jax.ShapeDtypeStruct((L, FFN), BF16), jax.ShapeDtypeStruct((1, D_MODEL), F32)],
    )(dx2, x1, gt, up, gain, w_gate, w_up, w_down)


def _inproj_bwd(dx1, x0, du_skip, du_scan, dz, dxbc, ddt, gain, w_pad, layer, name):
    L = x0.shape[0]

    def body(d_ref, x_ref, dus_ref, duc_ref, dz_ref, dxbc_ref, ddt_ref, g_ref, w_ref,
             dx_ref, dxb_ref, dp_ref, dgain_ref):
        first = pl.program_id(0) == 0
        xv = x_ref[...]
        dp = jnp.concatenate([dus_ref[...] + duc_ref[...], dz_ref[...], dxbc_ref[...], ddt_ref[...]],
                             axis=1).astype(BF16)
        dp_ref[...] = dp
        dh = _dot_nt(dp, w_ref[...])
        dxn, dgain = _rms_bwd(xv, _rstd(xv), g_ref[...], dh)
        dx = d_ref[...] + dxn
        dx_ref[...] = dx
        dxb_ref[...] = dx.astype(BF16)
        _acc_rows(dgain_ref, dgain, first)

    row = _row_spec(ROW_TILE, D_MODEL)
    vec = _full_spec((1, D_MODEL))
    return pl.pallas_call(
        body, name=name, grid=(L // ROW_TILE,),
        in_specs=[row, row, row, row, row, _row_spec(ROW_TILE, SSD_CONV_DIM), _row_spec(ROW_TILE, 256), vec,
                  _layer_spec((D_MODEL, IN_PAD), layer)],
        out_specs=[row, row, _row_spec(ROW_TILE, IN_PAD), vec],
        out_shape=[jax.ShapeDtypeStruct((L, D_MODEL), F32), jax.ShapeDtypeStruct((L, D_MODEL), BF16),
                   jax.ShapeDtypeStruct((L, IN_PAD), BF16), jax.ShapeDtypeStruct((1, D_MODEL), F32)],
    )(dx1, x0, du_skip, du_scan, dz, dxbc, ddt, gain, w_pad)


def _final_loss(x, gain, target, name):
    L = x.shape[0]

    def body(x_ref, g_ref, t_ref, loss_ref, dx_ref, dxb_ref, dgain_ref):
        first = pl.program_id(0) == 0
        xv = x_ref[...]
        r = _rstd(xv)
        err = xv * r * g_ref[...] - t_ref[...]
        part = 0.5 * jnp.sum(jnp.mean(err * err, axis=-1, keepdims=True), axis=0, keepdims=True)
        dx, dgain = _rms_bwd(xv, r, g_ref[...], err * (1.0 / D_MODEL))
        dx_ref[...] = dx
        dxb_ref[...] = dx.astype(BF16)
        _acc_rows(loss_ref, jnp.broadcast_to(part, (1, LANES)), first)
        _acc_rows(dgain_ref, dgain, first)

    row = _row_spec(ROW_TILE, D_MODEL)
    vec = _full_spec((1, D_MODEL))
    return pl.pallas_call(
        body, name=name, grid=(L // ROW_TILE,), in_specs=[row, vec, row],
        out_specs=[_full_spec((1, LANES)), row, row, vec],
        out_shape=[jax.ShapeDtypeStruct((1, LANES), F32), jax.ShapeDtypeStruct((L, D_MODEL), F32),
                   jax.ShapeDtypeStruct((L, D_MODEL), BF16), jax.ShapeDtypeStruct((1, D_MODEL), F32)],
    )(x, gain, target)


def _to_segments(a):
    L, n = a.shape
    return a.reshape(SEGS, L // SEGS, n).transpose(1, 0, 2).reshape(L, n)


def _from_segments(a):
    L, n = a.shape
    return a.reshape(L // SEGS, SEGS, n).transpose(1, 0, 2).reshape(L, n)


def _block_diag_in_to_state(m):
    m = m.reshape(S5_TILES, S5_TILE_GROUPS, S5_GROUP, S5_STATE)
    eye = jnp.eye(S5_TILE_GROUPS, dtype=m.dtype)
    out = m[:, :, :, None, :] * eye[None, :, None, :, None]
    return out.reshape(S5_TILES, S5_TILE_IN, S5_TILE_ST)


def _block_diag_extract(d):
    d = d.reshape(S5_TILES, S5_TILE_GROUPS, S5_GROUP, S5_TILE_GROUPS, S5_STATE)
    d = jnp.stack([d[:, a, :, a, :] for a in range(S5_TILE_GROUPS)], axis=1)
    return d.reshape(S5_GROUPS, S5_GROUP, S5_STATE)


def _pad_in_proj(w):
    z = jnp.zeros(w.shape[:-1] + (LANES - SSD_GROUP_HEADS,), w.dtype)
    return jnp.concatenate([w[..., :IN_MAIN + 8], z, w[..., IN_MAIN + 8:], z], axis=-1)


def _unpad_in_proj(w):
    return jnp.concatenate([w[..., :IN_MAIN + 8], w[..., IN_MAIN + LANES:IN_MAIN + LANES + 8]], axis=-1)


def _pad_heads(v):
    v = v.reshape(SSD_GROUPS, 1, SSD_GROUP_HEADS)
    return jnp.pad(v, ((0, 0), (0, 0), (0, LANES - SSD_GROUP_HEADS)))


def _unpad_heads(v):
    return v[:, 0, :SSD_GROUP_HEADS].reshape(SSD_HEADS)


def _layer_forward(x0, p, big, i, after_inproj=None, before_ffn=None):
    tag = "l%d_" % i
    ls = p["s5_log_step"].reshape(S5_GROUPS, 1)
    b_hgp = (p["s5_b_re"].transpose(2, 0, 1), p["s5_b_im"].transpose(2, 0, 1))
    are, aim, bbre, bbim = _s5_prep(p["s5_lam_re"], p["s5_lam_im"], ls, b_hgp[0], b_hgp[1], tag + "s5_prep")
    bre_ghp = bbre.transpose(1, 0, 2)
    bim_ghp = bbim.transpose(1, 0, 2)
    bre_bd = _block_diag_in_to_state(bre_ghp).astype(BF16)
    bim_bd = _block_diag_in_to_state(bim_ghp).astype(BF16)
    cret_bd = _block_diag_in_to_state(p["s5_c_re"]).astype(BF16)
    cimt_bd = _block_diag_in_to_state(p["s5_c_im"]).astype(BF16)
    s5mats = dict(bre_bd=bre_bd, bim_bd=bim_bd, cret_bd=cret_bd, cimt_bd=cimt_bd,
                  bret_bd=bre_bd.transpose(0, 2, 1), bimt_bd=bim_bd.transpose(0, 2, 1),
                  cre_bd=cret_bd.transpose(0, 2, 1), cim_bd=cimt_bd.transpose(0, 2, 1),
                  are=are.reshape(S5_TILES, 1, S5_TILE_ST), aim=aim.reshape(S5_TILES, 1, S5_TILE_ST))

    u, z, xbc, dt, h1 = _rms_inproj(x0, p["norm_mix"].reshape(1, -1), big["w_in"], None, tag + "rms_inproj")
    if after_inproj is not None:
        after_inproj(u)
    u_perm = _to_segments(u)
    ys_perm, xr, xi = _s5_scan(u_perm, bre_bd, bim_bd, s5mats["cre_bd"], s5mats["cim_bd"],
                               s5mats["are"], s5mats["aim"], tag + "s5_scan")
    ys = _from_segments(ys_perm)
    ya = _s5_post(ys, u, p["s5_d"].reshape(1, -1), big["s5_w_glu"], p["s5_b_glu"].reshape(1, -1),
                  p["s5_norm"].reshape(1, -1), None, tag + "s5_post")

    conv_w = jnp.pad(p["ssd_conv_w"], ((0, 8 - SSD_CONV), (0, 0)))
    conv_b = p["ssd_conv_b"].reshape(1, -1)
    xc = _ssd_conv(xbc, conv_w, conv_b, tag + "ssd_conv")
    expand = _head_expander()
    heads = dict(dt_bias=_pad_heads(p["ssd_dt_bias"]), a_log=_pad_heads(p["ssd_a_log"]),
                 d=jnp.repeat(p["ssd_d"], SSD_HEAD_DIM).reshape(SSD_GROUPS, 1, GROUP_W),
                 expand=expand, expand_t=expand.T)
    y, sprev = _ssd_scan(xc, dt, heads["dt_bias"], heads["a_log"], heads["d"], expand, heads["expand_t"],
                         tag + "ssd_scan")
    yb = _ssd_post(y, z, p["ssd_norm"].reshape(1, -1), tag + "ssd_post")

    x1 = _out_proj(x0, ya, yb, big["w_out"], None, tag + "out_proj")
    ffn_matrices = before_ffn(x1) if before_ffn is not None else {}
    big = {**big, **ffn_matrices}
    x2, gt, up = _ffn(x1, p["norm_ffn"].reshape(1, -1), big["w_gate"], big["w_up"], big["w_down"], None,
                      tag + "ffn")
    saved = dict(x0=x0, h1=h1, u=u, u_perm=u_perm, z=z, xbc=xbc, dt=dt, xr=xr, xi=xi, ys=ys, ya=ya, xc=xc, y=y,
                 sprev=sprev, yb=yb, x1=x1, gt=gt, up=up, s5mats=s5mats, heads=heads, conv_w=conv_w,
                 conv_b=conv_b, ls=ls, b_hgp=b_hgp, ffn_matrices=ffn_matrices)
    return x2, saved


def _layer_backward(dx2, dx2b, p, big, s, i, after_ffn_grads=None):
    tag = "l%d_" % i
    g = {}
    dx1, dx1b, h2, act, dgt, dup, dgain = _ffn_bwd(dx2, s["x1"], s["gt"], s["up"], p["norm_ffn"].reshape(1, -1),
                                                  big["w_gate"], big["w_up"], big["w_down"], None, tag + "ffn_bwd")
    g["norm_ffn"] = dgain[0]
    g["w_down"] = _mm_tn(act, dx2b, tag + "dw_down")
    g["w_gate"] = _mm_tn(dgt, h2, tag + "dw_gate")
    g["w_up"] = _mm_tn(dup, h2, tag + "dw_up")
    g["w_out"] = _mm_tn(jnp.concatenate([s["ya"], s["yb"]], axis=1), dx1b, tag + "dw_out")
    if after_ffn_grads is not None:
        p = {**p, "s5_norm": p["s5_norm"] + after_ffn_grads(g)[0, 0]}

    dys, du_skip, gelu_b, dq_b, dgain, dd, dbg = _s5_post_bwd(
        dx1b, big["w_out"], s["ys"], s["u"], p["s5_d"].reshape(1, -1), big["s5_w_glu"],
        p["s5_b_glu"].reshape(1, -1), p["s5_norm"].reshape(1, -1), None, tag + "s5_post_bwd")
    g["s5_norm"] = dgain[0]
    g["s5_d"] = dd[0]
    g["s5_b_glu"] = dbg[0]
    g["s5_w_glu"] = _mm_tn(gelu_b, dq_b, tag + "dw_glu")
    m = s["s5mats"]
    du_perm, dar, dai, dcre_d, dcim_d, dbre_d, dbim_d = _s5_scan_bwd(
        _to_segments(dys), s["u_perm"], s["xr"], s["xi"], m["bret_bd"], m["bimt_bd"], m["cret_bd"], m["cimt_bd"],
        m["are"], m["aim"], tag + "s5_scan_bwd")
    du_scan = _from_segments(du_perm)
    g["s5_c_re"] = _block_diag_extract(dcre_d)
    g["s5_c_im"] = _block_diag_extract(dcim_d)
    dbbre = _block_diag_extract(dbre_d).transpose(1, 0, 2)
    dbbim = _block_diag_extract(dbim_d).transpose(1, 0, 2)
    dlr, dli, dls, dbre, dbim = _s5_prep_bwd(
        p["s5_lam_re"], p["s5_lam_im"], s["ls"], s["b_hgp"][0], s["b_hgp"][1],
        dar.reshape(S5_GROUPS, S5_STATE), dai.reshape(S5_GROUPS, S5_STATE), dbbre, dbbim, tag + "s5_prep_bwd")
    g["s5_lam_re"] = dlr
    g["s5_lam_im"] = dli
    g["s5_log_step"] = dls[:, 0]
    g["s5_b_re"] = dbre
    g["s5_b_im"] = dbim

    dy, dz, dgain = _ssd_post_bwd(dx1b, big["w_out"], s["y"], s["z"], p["ssd_norm"].reshape(1, -1), None,
                                  tag + "ssd_post_bwd")
    g["ssd_norm"] = dgain[0]
    hd = s["heads"]
    dxs, dbm, dcm, ddt, dbias, dalog, dd = _ssd_scan_bwd(dy, s["xc"], s["dt"], s["sprev"], hd["dt_bias"],
                                                       hd["a_log"], hd["d"], hd["expand"], hd["expand_t"],
                                                       tag + "ssd_scan_bwd")
    g["ssd_dt_bias"] = _unpad_heads(dbias)
    g["ssd_a_log"] = _unpad_heads(dalog)
    g["ssd_d"] = _unpad_heads(dd)
    dxc = jnp.concatenate([dxs, dbm, dcm], axis=1)
    dxbc, dcw, dcb = _ssd_conv_bwd(dxc, s["xbc"], s["conv_w"], s["conv_b"], tag + "ssd_conv_bwd")
    g["ssd_conv_w"] = dcw[:SSD_CONV]
    g["ssd_conv_b"] = dcb[0]

    dx0, dx0b, dproj, dgain = _inproj_bwd(dx1, s["x0"], du_skip, du_scan, dz, dxbc, ddt, p["norm_mix"].reshape(1, -1),
                                          big["w_in"], None, tag + "inproj_bwd")
    g["norm_mix"] = dgain[0]
    g["w_in"] = _mm_tn(s["h1"], dproj, tag + "dw_in")
    return dx0, dx0b, g


MIXER_BIG = ("w_in", "s5_w_glu", "w_out")
FFN_BIG = ("w_gate", "w_up", "w_down")
BIG = MIXER_BIG + FFN_BIG
COL_SHARDED = ("w_in",)
T_STORED = ("w_gate", "w_up")
LAYER_SMALL = ("norm_mix", "s5_lam_re", "s5_lam_im", "s5_log_step", "s5_b_re", "s5_b_im", "s5_c_re", "s5_c_im",
               "s5_d", "s5_b_glu", "s5_norm", "ssd_conv_w", "ssd_conv_b", "ssd_dt_bias", "ssd_a_log", "ssd_d",
               "ssd_norm", "norm_ffn")
WEIGHTS = ("norm_mix", "w_in", "s5_lam_re", "s5_lam_im", "s5_log_step", "s5_b_re", "s5_b_im", "s5_c_re", "s5_c_im",
           "s5_d", "s5_w_glu", "s5_b_glu", "s5_norm", "ssd_conv_w", "ssd_conv_b", "ssd_dt_bias", "ssd_a_log",
           "ssd_d", "ssd_norm", "w_out", "norm_ffn", "w_gate", "w_up", "w_down", "norm_final")


S5_BC = ("s5_b_re", "s5_b_im", "s5_c_re", "s5_c_im")
TINY = tuple(k for k in LAYER_SMALL if k not in S5_BC)


def _local_step(x, target, big, small, norm_final):
    saved = []
    h = x
    for i in range(DEPTH):
        p = {k: v[i] for k, v in small.items()}
        h, s = _layer_forward(h, p, big, i)
        saved.append((p, s))
    loss, dx, dxb, dgf = _final_loss(h, norm_final.reshape(1, -1), target, "final_loss")
    grads = [None] * DEPTH
    for i in reversed(range(DEPTH)):
        p, s = saved[i]
        dx, dxb, grads[i] = _layer_backward(dx, dxb, p, big, s, i)
    by_name = {k: [grads[i][k] for i in range(DEPTH)] for k in BIG + LAYER_SMALL}
    return loss[0, 0], dx, by_name, dgf[0]


def _my_place():
    return lax.axis_index("x"), lax.axis_index("y"), lax.axis_index("c")


def _all_gather8(blocks, name):
    nt = len(blocks)

    def body(*refs):
        ins = refs[:nt]
        outs = refs[nt:2 * nt]
        send_sems, recv_sems, local_sems = refs[2 * nt:]
        x, y, c = _my_place()
        me, sibling = (x, y, c), (x, y, 1 - c)
        chips = [(1 - x, y), (x, 1 - y), (1 - x, 1 - y)]

        def slot(t, place):
            px, py, pc = place
            return outs[t].at[4 * px + 2 * py + pc]

        def copy(t, k, block, to, src=None):
            return pltpu.make_async_remote_copy(
                src_ref=slot(t, block) if src is None else src, dst_ref=slot(t, block),
                send_sem=send_sems.at[t, k], recv_sem=recv_sems.at[t, k], device_id=to, device_id_type=MESH)

        mine = [pltpu.make_async_copy(ins[t], slot(t, me), local_sems.at[t]) for t in range(nt)]
        for cp in mine:
            cp.start()
        first = []
        for t in range(nt):
            first.append(copy(t, 0, me, sibling, src=ins[t]))
            first += [copy(t, 1 + j, me, (*chip, c), src=ins[t]) for j, chip in enumerate(chips)]
        for cp in first:
            cp.start()
        passed = []
        for j, chip in enumerate(chips):
            for t in range(nt):
                copy(t, 1 + j, (*chip, c), me).wait_recv()
                fwd = copy(t, 4 + j, (*chip, c), sibling)
                fwd.start()
                passed.append(fwd)
        for t in range(nt):
            copy(t, 0, sibling, me).wait_recv()
            for j, chip in enumerate(chips):
                copy(t, 4 + j, (*chip, 1 - c), me).wait_recv()
        for cp in first + passed:
            cp.wait_send()
        for cp in mine:
            cp.wait()

    return pl.pallas_call(
        body, name=name, in_specs=[ANY] * nt, out_specs=[ANY] * nt,
        out_shape=[jax.ShapeDtypeStruct((8,) + b.shape, b.dtype) for b in blocks],
        scratch_shapes=[pltpu.SemaphoreType.DMA((nt, 7)), pltpu.SemaphoreType.DMA((nt, 7)),
                        pltpu.SemaphoreType.DMA((nt,))],
    )(*blocks)


HBM = pl.BlockSpec(memory_space=pltpu.HBM)
SEM = pl.BlockSpec(memory_space=pltpu.SEMAPHORE)
DATAFLOW = pltpu.SideEffectType.DATAFLOW_SIDE_EFFECTING


def _in_hbm(a):
    return pltpu.with_memory_space_constraint(a, pltpu.HBM)


TOKEN = jax.ShapeDtypeStruct((8, LANES), F32)
VMEM_SPEC = pl.BlockSpec(memory_space=pltpu.VMEM)


def _gather_start(blocks, after, name):
    nt = len(blocks)

    def body(*refs):
        ins = refs[:nt]
        lands = refs[nt:2 * nt]
        send_sems, recv_sems = refs[2 * nt + 1:2 * nt + 3]
        refs[-1][...] = jnp.zeros(TOKEN.shape, F32)
        x, y, c = _my_place()
        me = 4 * x + 2 * y + c
        peers = [(x, y, 1 - c), (1 - x, y, c), (x, 1 - y, c), (1 - x, 1 - y, c)]
        for t in range(nt):
            for k, peer in enumerate(peers):
                pltpu.make_async_remote_copy(src_ref=ins[t], dst_ref=lands[t].at[me], send_sem=send_sems.at[4 * t + k],
                                             recv_sem=recv_sems.at[4 * t + k], device_id=peer,
                                             device_id_type=MESH).start()

    lands = [_in_hbm(lax.empty((8,) + b.shape, b.dtype)) for b in blocks]
    out = pl.pallas_call(
        body, name=name, in_specs=[HBM] * (2 * nt) + [ANY],
        out_shape=(pltpu.SemaphoreType.DMA((4 * nt,)), pltpu.SemaphoreType.DMA((4 * nt,)),
                   *[pltpu.HBM(b.shape, b.dtype) for b in blocks],
                   *[pltpu.HBM((8,) + b.shape, b.dtype) for b in blocks], TOKEN),
        out_specs=(SEM, SEM, *[HBM] * (2 * nt), VMEM_SPEC),
        input_output_aliases={i: 2 + i for i in range(2 * nt)},
        compiler_params=pltpu.CompilerParams(has_side_effects=DATAFLOW),
    )(*[_in_hbm(b) for b in blocks], *lands, after)
    return out[:2], list(out[2:2 + nt]), list(out[2 + nt:2 + 2 * nt]), out[-1]


def _gather_forward(sems, blocks, lands, after, name):
    nt = len(blocks)

    def body(*refs):
        ins = refs[:nt]
        lands_in = refs[nt:2 * nt]
        send1, recv1 = refs[2 * nt:2 * nt + 2]
        send2, recv2 = refs[2 * nt + 3:2 * nt + 5]
        x, y, c = _my_place()
        me = 4 * x + 2 * y + c
        sibling = (x, y, 1 - c)
        sources = [4 * x + 2 * y + (1 - c), 4 * (1 - x) + 2 * y + c, 4 * x + 2 * (1 - y) + c,
                   4 * (1 - x) + 2 * (1 - y) + c]
        for t in range(nt):
            for k, src in enumerate(sources):
                cp = pltpu.make_async_remote_copy(src_ref=ins[t], dst_ref=lands_in[t].at[src],
                                                  send_sem=send1.at[4 * t + k], recv_sem=recv1.at[4 * t + k],
                                                  device_id=sibling, device_id_type=MESH)
                cp.wait_send()
                cp.wait_recv()
            for k, src in enumerate(sources[1:]):
                pltpu.make_async_remote_copy(src_ref=lands_in[t].at[src], dst_ref=lands_in[t].at[src],
                                             send_sem=send2.at[3 * t + k], recv_sem=recv2.at[3 * t + k],
                                             device_id=sibling, device_id_type=MESH).start()

    out = pl.pallas_call(
        body, name=name, in_specs=[HBM] * (2 * nt) + [SEM, SEM, pl.BlockSpec(memory_space=pl.ANY)],
        out_shape=(pltpu.SemaphoreType.DMA((3 * nt,)), pltpu.SemaphoreType.DMA((3 * nt,)),
                   *[pltpu.HBM(b.shape, b.dtype) for b in blocks],
                   *[pltpu.HBM(a.shape, a.dtype) for a in lands]),
        out_specs=(SEM, SEM, *[HBM] * (2 * nt)),
        input_output_aliases={i: 2 + i for i in range(2 * nt)},
        compiler_params=pltpu.CompilerParams(has_side_effects=DATAFLOW),
    )(*blocks, *lands, *sems, after)
    return out[:2], list(out[2 + nt:])


def _gather_finish(sems, lands, after, name):
    nt = len(lands)

    def body(*refs):
        lands_in = refs[:nt]
        send2, recv2 = refs[nt:nt + 2]
        x, y, c = _my_place()
        sibling = (x, y, 1 - c)
        mine = [4 * (1 - x) + 2 * y + c, 4 * x + 2 * (1 - y) + c, 4 * (1 - x) + 2 * (1 - y) + c]
        theirs = [4 * (1 - x) + 2 * y + 1 - c, 4 * x + 2 * (1 - y) + 1 - c, 4 * (1 - x) + 2 * (1 - y) + 1 - c]
        for t in range(nt):
            for k in range(3):
                cp = pltpu.make_async_remote_copy(src_ref=lands_in[t].at[mine[k]], dst_ref=lands_in[t].at[theirs[k]],
                                                  send_sem=send2.at[3 * t + k], recv_sem=recv2.at[3 * t + k],
                                                  device_id=sibling, device_id_type=MESH)
                cp.wait_send()
                cp.wait_recv()

    out = pl.pallas_call(
        body, name=name, in_specs=[HBM] * nt + [SEM, SEM, pl.BlockSpec(memory_space=pl.ANY)],
        out_shape=tuple(pltpu.HBM(a.shape, a.dtype) for a in lands), out_specs=tuple([HBM] * nt),
        input_output_aliases={i: i for i in range(nt)},
        compiler_params=pltpu.CompilerParams(has_side_effects=DATAFLOW),
    )(*lands, *sems, after)
    return list(out)


def _other_chips():
    x, y, _ = _my_place()
    return [(1 - x, y), (x, 1 - y), (1 - x, 1 - y)]


def _scatter_start(chunks, name):
    nt = len(chunks)

    def body(*refs):
        ins = refs[:nt]
        lands = refs[nt:2 * nt]
        send_sems, recv_sems = refs[2 * nt:2 * nt + 2]
        refs[-1][...] = jnp.zeros(TOKEN.shape, F32)
        x, y, c = _my_place()
        for t in range(nt):
            for j, (px, py) in enumerate(_other_chips()):
                pltpu.make_async_remote_copy(src_ref=ins[t].at[2 * px + py], dst_ref=lands[t].at[2 * x + y],
                                             send_sem=send_sems.at[3 * t + j], recv_sem=recv_sems.at[3 * t + j],
                                             device_id=(px, py, c), device_id_type=MESH).start()

    lands = [_in_hbm(lax.empty(a.shape, a.dtype)) for a in chunks]
    out = pl.pallas_call(
        body, name=name, in_specs=[HBM] * (2 * nt),
        out_shape=(pltpu.SemaphoreType.DMA((3 * nt,)), pltpu.SemaphoreType.DMA((3 * nt,)),
                   *[pltpu.HBM(a.shape, a.dtype) for a in chunks] * 2, TOKEN),
        out_specs=(SEM, SEM, *[HBM] * (2 * nt), VMEM_SPEC),
        input_output_aliases={i: 2 + i for i in range(2 * nt)},
        compiler_params=pltpu.CompilerParams(has_side_effects=DATAFLOW),
    )(*[_in_hbm(a) for a in chunks], *lands)
    return out[:2], list(out[2:2 + nt]), list(out[2 + nt:2 + 2 * nt]), out[-1]


def _scatter_finish(sems, chunks, lands, after, name):
    nt = len(chunks)

    def body(*refs):
        ins = refs[:nt]
        lands_in = refs[nt:2 * nt]
        send_sems, recv_sems = refs[2 * nt:2 * nt + 2]
        _, _, c = _my_place()
        for t in range(nt):
            for j, (px, py) in enumerate(_other_chips()):
                cp = pltpu.make_async_remote_copy(src_ref=ins[t].at[2 * px + py], dst_ref=lands_in[t].at[2 * px + py],
                                                  send_sem=send_sems.at[3 * t + j], recv_sem=recv_sems.at[3 * t + j],
                                                  device_id=(px, py, c), device_id_type=MESH)
                cp.wait_send()
                cp.wait_recv()

    out = pl.pallas_call(
        body, name=name, in_specs=[HBM] * (2 * nt) + [SEM, SEM, ANY],
        out_shape=tuple(pltpu.HBM(a.shape, a.dtype) for a in lands), out_specs=tuple([HBM] * nt),
        input_output_aliases={nt + i: i for i in range(nt)},
        compiler_params=pltpu.CompilerParams(has_side_effects=DATAFLOW),
    )(*chunks, *lands, *sems, after)
    return list(out)


def _swap_halves(views, name):
    nt = len(views)

    def body(*refs):
        ins = refs[:nt]
        outs = refs[nt:2 * nt]
        send_sems, recv_sems = refs[2 * nt:]
        x, y, c = _my_place()
        copies = [pltpu.make_async_remote_copy(
            src_ref=ins[t].at[pl.ds(0, views[t].shape[0]), pl.ds(1 - c, 1)], dst_ref=outs[t],
            send_sem=send_sems.at[t], recv_sem=recv_sems.at[t], device_id=(x, y, 1 - c), device_id_type=MESH)
            for t in range(nt)]
        for cp in copies:
            cp.start()
        for cp in copies:
            cp.wait()

    return pl.pallas_call(
        body, name=name, in_specs=[ANY] * nt, out_specs=[ANY] * nt,
        out_shape=[jax.ShapeDtypeStruct((a.shape[0], 1) + a.shape[2:], a.dtype) for a in views],
        scratch_shapes=[pltpu.SemaphoreType.DMA((nt,)), pltpu.SemaphoreType.DMA((nt,))],
    )(*views)


def _pair_add_halves(view, recv, name):
    n, _, rows, cols = view.shape
    tile = _row_tile(rows, cols, 4)

    def body(a0_ref, a1_ref, r_ref, o_ref):
        mine = jnp.where(lax.axis_index("c") == 0, a0_ref[...], a1_ref[...])
        o_ref[...] = (mine.astype(F32) + r_ref[...].astype(F32)).astype(o_ref.dtype)

    half = lambda h: pl.BlockSpec((None, None, tile, cols), lambda p, i: (p, h, i, 0))
    return pl.pallas_call(
        body, name=name, grid=(n, rows // tile), in_specs=[half(0), half(1), half(0)],
        out_specs=pl.BlockSpec((None, tile, cols), lambda p, i: (p, i, 0)),
        out_shape=jax.ShapeDtypeStruct((n, rows, cols), view.dtype))(view, view, recv)


def _sum_chunks(lands, chunks, order, name):
    _, rows, cols = chunks.shape
    tile = _row_tile(rows, cols, 5)

    def body(order_ref, l0_ref, l1_ref, l2_ref, own_ref, o_ref):
        o_ref[...] = ((l0_ref[...].astype(F32) + l1_ref[...].astype(F32)) + l2_ref[...].astype(F32)
                      + own_ref[...].astype(F32))

    slot = lambda j: pl.BlockSpec((None, tile, cols), lambda i, order_ref: (order_ref[j], i, 0))
    grid_spec = pltpu.PrefetchScalarGridSpec(
        num_scalar_prefetch=1, grid=(rows // tile,), in_specs=[slot(0), slot(1), slot(2), slot(3)],
        out_specs=pl.BlockSpec((tile, cols), lambda i, order_ref: (i, 0)))
    return pl.pallas_call(body, name=name, grid_spec=grid_spec,
                          out_shape=jax.ShapeDtypeStruct((rows, cols), F32))(order, lands, lands, lands, chunks)


def _adamw_layer(w, g_mine, g_sibling, m, v, layer, prev, name):
    depth, rows, cols = w.shape
    half = rows // 2
    tile = _row_tile(half, cols, 10)
    tiles = half // tile

    def body(w_ref, gm_ref, gs_ref, m_ref, v_ref, *rest):
        d_ref, nm_ref, nv_ref, go_ref = rest[-4:]
        gv = jnp.where(pl.program_id(0) == lax.axis_index("c"), gm_ref[...], gs_ref[...])
        d_ref[...], nm_ref[...], nv_ref[...] = _adamw_math(w_ref[...], gv, m_ref[...], v_ref[...])
        go_ref[...] = gv

    spec = pl.BlockSpec((None, tile, cols), lambda h, i: (layer, h * tiles + i, 0))
    gspec = pl.BlockSpec((tile, cols), lambda h, i: (i, 0))
    shape = jax.ShapeDtypeStruct((depth, rows, cols), F32)
    extra = list(prev)
    aliases = {5 + j: j for j in range(4)} if len(extra) == 4 else {}
    return pl.pallas_call(
        body, name=name, grid=(2, tiles), in_specs=[spec, gspec, gspec, spec, spec] + [ANY] * len(extra),
        out_specs=[spec] * 4, out_shape=[shape] * 4, input_output_aliases=aliases)(w, g_mine, g_sibling, m, v, *extra)


def _sibling_swap_other(pairs, name):
    nt = len(pairs)

    def body(*refs):
        ins = refs[:2 * nt]
        outs = refs[2 * nt:3 * nt]
        send_sems, recv_sems = refs[3 * nt:]
        x, y, c = _my_place()

        def copy(t, src):
            return pltpu.make_async_remote_copy(src_ref=src, dst_ref=outs[t], send_sem=send_sems.at[t],
                                                recv_sem=recv_sems.at[t], device_id=(x, y, 1 - c), device_id_type=MESH)

        for t in range(nt):
            @pl.when(c == 0)
            def _():
                copy(t, ins[2 * t + 1]).start()

            @pl.when(c == 1)
            def _():
                copy(t, ins[2 * t]).start()
        for t in range(nt):
            copy(t, ins[2 * t]).wait()

    flat = [a for pair in pairs for a in pair]
    return pl.pallas_call(
        body, name=name, in_specs=[ANY] * (2 * nt), out_specs=[ANY] * nt,
        out_shape=[jax.ShapeDtypeStruct(a0.shape, a0.dtype) for a0, _ in pairs],
        scratch_shapes=[pltpu.SemaphoreType.DMA((nt,)), pltpu.SemaphoreType.DMA((nt,))],
    )(*flat)


def _sibling_swap(arrs, name):
    nt = len(arrs)

    def body(*refs):
        ins = refs[:nt]
        outs = refs[nt:2 * nt]
        send_sems, recv_sems = refs[2 * nt:]
        x, y, c = _my_place()
        copies = [pltpu.make_async_remote_copy(src_ref=ins[t], dst_ref=outs[t], send_sem=send_sems.at[t],
                                               recv_sem=recv_sems.at[t], device_id=(x, y, 1 - c), device_id_type=MESH)
                  for t in range(nt)]
        for cp in copies:
            cp.start()
        for cp in copies:
            cp.wait()

    return pl.pallas_call(
        body, name=name, in_specs=[ANY] * nt, out_specs=[ANY] * nt,
        out_shape=[jax.ShapeDtypeStruct(a.shape, a.dtype) for a in arrs],
        scratch_shapes=[pltpu.SemaphoreType.DMA((nt,)), pltpu.SemaphoreType.DMA((nt,))],
    )(*arrs)


def _chip_all_to_all(arrs, name):
    nt = len(arrs)

    def body(*refs):
        ins = refs[:nt]
        outs = refs[nt:2 * nt]
        send_sems, recv_sems, local_sems = refs[2 * nt:]
        x, y, c = _my_place()
        mine = 2 * x + y
        chips = [(1 - x, y), (x, 1 - y), (1 - x, 1 - y)]
        local = [pltpu.make_async_copy(ins[t].at[mine], outs[t].at[mine], local_sems.at[t]) for t in range(nt)]
        for cp in local:
            cp.start()
        sends = []
        for t in range(nt):
            for j, (px, py) in enumerate(chips):
                sends.append(pltpu.make_async_remote_copy(
                    src_ref=ins[t].at[2 * px + py], dst_ref=outs[t].at[mine], send_sem=send_sems.at[t, j],
                    recv_sem=recv_sems.at[t, j], device_id=(px, py, c), device_id_type=MESH))
        for cp in sends:
            cp.start()
        for t in range(nt):
            for j, (px, py) in enumerate(chips):
                pltpu.make_async_remote_copy(
                    src_ref=ins[t].at[mine], dst_ref=outs[t].at[2 * px + py], send_sem=send_sems.at[t, j],
                    recv_sem=recv_sems.at[t, j], device_id=(px, py, c), device_id_type=MESH).wait_recv()
        for cp in sends:
            cp.wait_send()
        for cp in local:
            cp.wait()

    return pl.pallas_call(
        body, name=name, in_specs=[ANY] * nt, out_specs=[ANY] * nt,
        out_shape=[jax.ShapeDtypeStruct(a.shape, a.dtype) for a in arrs],
        scratch_shapes=[pltpu.SemaphoreType.DMA((nt, 3)), pltpu.SemaphoreType.DMA((nt, 3)),
                        pltpu.SemaphoreType.DMA((nt,))],
    )(*arrs)


def _as_rows(a):
    return a.reshape(-1, a.shape[-1])


STREAM_VMEM_BYTES = 32 * 1024 * 1024
SUBLANES = 8


def _row_tile(rows, cols, n_arrays):
    lanes = -(-cols // LANES) * LANES
    for t in range(min(rows, 512), SUBLANES - 1, -1):
        if rows % t == 0 and t % SUBLANES == 0 and 2 * n_arrays * t * lanes * 4 <= STREAM_VMEM_BYTES:
            return t
    return rows


def _pair_add(a0, a1, recv, name):
    rows, cols = a0.shape
    tile = _row_tile(rows, cols, 4)

    def body(a0_ref, a1_ref, r_ref, o_ref):
        mine = jnp.where(lax.axis_index("c") == 0, a0_ref[...], a1_ref[...])
        o_ref[...] = (mine.astype(F32) + r_ref[...].astype(F32)).astype(o_ref.dtype)

    spec = pl.BlockSpec((tile, cols), lambda i: (i, 0))
    return pl.pallas_call(body, name=name, grid=(rows // tile,), in_specs=[spec] * 3, out_specs=spec,
                          out_shape=jax.ShapeDtypeStruct((rows, cols), a0.dtype))(a0, a1, recv)


def _sum_leading(a, name):
    n, rows, cols = a.shape
    tile = _row_tile(rows, cols, n + 1)

    def body(a_ref, o_ref):
        acc = a_ref[0].astype(F32)
        for k in range(1, n):
            acc = acc + a_ref[k].astype(F32)
        o_ref[...] = acc

    return pl.pallas_call(
        body, name=name, grid=(rows // tile,), in_specs=[pl.BlockSpec((n, tile, cols), lambda i: (0, i, 0))],
        out_specs=pl.BlockSpec((tile, cols), lambda i: (i, 0)),
        out_shape=jax.ShapeDtypeStruct((rows, cols), F32))(a)


def _adamw_math(w, g, m, v):
    mn = ADAM_B1 * m + (1.0 - ADAM_B1) * g
    vn = ADAM_B2 * v + (1.0 - ADAM_B2) * jnp.square(g)
    m_hat = mn / (1.0 - ADAM_B1 ** ADAM_STEP)
    v_hat = vn / (1.0 - ADAM_B2 ** ADAM_STEP)
    delta = -ADAM_LR * (m_hat / (jnp.sqrt(v_hat) + ADAM_EPS) + ADAM_WD * w)
    return delta, mn, vn


def _adamw_layers(w, g_mine, g_sibling, m, v, name):
    depth, rows, cols = w.shape
    tile = _row_tile(rows, cols, 10)

    def body(w_ref, gm_ref, gs_ref, m_ref, v_ref, d_ref, nm_ref, nv_ref, go_ref):
        gv = jnp.where(pl.program_id(0) == lax.axis_index("c"), gm_ref[...], gs_ref[...])
        d_ref[...], nm_ref[...], nv_ref[...] = _adamw_math(w_ref[...], gv, m_ref[...], v_ref[...])
        go_ref[...] = gv

    spec = pl.BlockSpec((None, tile, cols), lambda l, i: (l, i, 0))
    gspec = pl.BlockSpec((tile, cols), lambda l, i: (i, 0))
    shape = jax.ShapeDtypeStruct((depth, rows, cols), F32)
    return pl.pallas_call(body, name=name, grid=(depth, rows // tile), in_specs=[spec, gspec, gspec, spec, spec],
                          out_specs=[spec] * 4, out_shape=[shape] * 4)(w, g_mine, g_sibling, m, v)


def _adamw_rows(w, g, m, v, name):
    depth, rows, cols = w.shape
    tile = _row_tile(rows, cols, 7)

    def body(w_ref, g_ref, m_ref, v_ref, d_ref, nm_ref, nv_ref):
        d_ref[...], nm_ref[...], nv_ref[...] = _adamw_math(w_ref[...], g_ref[...], m_ref[...], v_ref[...])

    spec = pl.BlockSpec((None, tile, cols), lambda l, i: (l, i, 0))
    shape = jax.ShapeDtypeStruct((depth, rows, cols), F32)
    return pl.pallas_call(body, name=name, grid=(depth, rows // tile), in_specs=[spec] * 4, out_specs=[spec] * 3,
                          out_shape=[shape] * 3)(w, g, m, v)


def _adamw_many(ws, gs, ms, vs, name):
    nt = len(ws)

    def body(*refs):
        for t in range(nt):
            w_ref, g_ref, m_ref, v_ref = (refs[k * nt + t] for k in range(4))
            d_ref, nm_ref, nv_ref = (refs[(4 + k) * nt + t] for k in range(3))
            d_ref[...], nm_ref[...], nv_ref[...] = _adamw_math(w_ref[...], g_ref[...], m_ref[...], v_ref[...])

    shapes = [jax.ShapeDtypeStruct(a.shape, F32) for a in ws]
    out = pl.pallas_call(body, name=name, out_shape=shapes * 3)(*ws, *gs, *ms, *vs)
    return out[:nt], out[nt:2 * nt], out[2 * nt:]


TINY_ROWS_MULTIPLE = 128


def _flat_pack(arrs):
    flat = jnp.concatenate([a.reshape(-1) for a in arrs])
    pad = (-flat.shape[0]) % (TINY_ROWS_MULTIPLE * LANES)
    return jnp.pad(flat, (0, pad)).reshape(-1, LANES)


def _flat_unpack(buf, shapes):
    flat = buf.reshape(-1)
    out = []
    off = 0
    for shp in shapes:
        n = math.prod(shp)
        out.append(flat[off:off + n].reshape(shp))
        off += n
    return out


def _to_chunks(a, name):
    if name == "w_in":
        a = _unpad_in_proj(a)
    rows, cols = a.shape
    if name in COL_SHARDED:
        return a.reshape(rows, 4, cols // 4).transpose(1, 0, 2)
    return a.reshape(4, rows // 4, cols)


def _from_chunks(a, name):
    _, depth, r, cc = a.shape
    if name in COL_SHARDED:
        return a.transpose(1, 2, 0, 3).reshape(depth, r, 4 * cc)
    return a.transpose(1, 0, 2, 3).reshape(depth, 4 * r, cc)


def kernel(x, norm_mix, w_in, s5_lam_re, s5_lam_im, s5_log_step, s5_b_re, s5_b_im, s5_c_re, s5_c_im, s5_d, s5_w_glu, s5_b_glu, s5_norm, ssd_conv_w, ssd_conv_b, ssd_dt_bias, ssd_a_log, ssd_d, ssd_norm, w_out, norm_ffn, w_gate, w_up, w_down, norm_final, loss_target, m_norm_mix, m_w_in, m_s5_lam_re, m_s5_lam_im, m_s5_log_step, m_s5_b_re, m_s5_b_im, m_s5_c_re, m_s5_c_im, m_s5_d, m_s5_w_glu, m_s5_b_glu, m_s5_norm, m_ssd_conv_w, m_ssd_conv_b, m_ssd_dt_bias, m_ssd_a_log, m_ssd_d, m_ssd_norm, m_w_out, m_norm_ffn, m_w_gate, m_w_up, m_w_down, m_norm_final, v_norm_mix, v_w_in, v_s5_lam_re, v_s5_lam_im, v_s5_log_step, v_s5_b_re, v_s5_b_im, v_s5_c_re, v_s5_c_im, v_s5_d, v_s5_w_glu, v_s5_b_glu, v_s5_norm, v_ssd_conv_w, v_ssd_conv_b, v_ssd_dt_bias, v_ssd_a_log, v_ssd_d, v_ssd_norm, v_w_out, v_norm_ffn, v_w_gate, v_w_up, v_w_down, v_norm_final):
    args = dict(locals())
    w = {k: args[k] for k in WEIGHTS}
    m = {k: args["m_" + k] for k in WEIGHTS}
    v = {k: args["v_" + k] for k in WEIGHTS}
    cx, cy, cc = _my_place()
    chip = 2 * cx + cy

    me = 4 * cx + 2 * cy + cc
    others = _other_chips()
    chunk_order = jnp.stack([2 * px + py for px, py in others] + [chip]).astype(jnp.int32)
    stored = lambda k, a: jnp.swapaxes(a, 1, 2) if k in T_STORED else a

    def my_half(k, layer):
        a = stored(k, w[k])[layer]
        return lax.dynamic_slice_in_dim(a, cc * (a.shape[0] // 2), a.shape[0] // 2, 0).astype(BF16)

    def assemble(names, lands, blocks):
        full = {}
        for k, a, b in zip(names, lands, blocks):
            a = lax.dynamic_update_index_in_dim(a, b, me, 0)
            a = a.reshape(4, 2 * a.shape[1], a.shape[2])
            if k in COL_SHARDED:
                full[k] = _pad_in_proj(a.transpose(1, 0, 2).reshape(a.shape[1], 4 * a.shape[2]))
            else:
                full[k] = a.reshape(4 * a.shape[1], a.shape[2])
        return full

    conv_block = w["ssd_conv_w"].reshape(DEPTH * SSD_CONV, -1)
    first = [my_half(k, 0) for k in MIXER_BIG] + [conv_block]
    ffn0 = [my_half(k, 0) for k in FFN_BIG]
    blocks1 = [my_half(k, 1) for k in BIG]
    sems_a, kept_a, lands_a, token = _gather_start(first, x, "gather0a_start")
    sems_b, kept_b, lands_b, token = _gather_start(ffn0, token, "gather0b_start")
    sems1, kept1, lands1, token = _gather_start(blocks1, token, "gather1_start")
    sems_a, lands_a = _gather_forward(sems_a, kept_a, lands_a, token, "gather0a_forward")
    lands_a = _gather_finish(sems_a, lands_a, token, "gather0a_finish")
    big0 = assemble(MIXER_BIG, lands_a, first)
    conv_rows = lax.dynamic_update_index_in_dim(lands_a[-1], conv_block, me, 0)
    conv_full = conv_rows.reshape(4, 2, DEPTH, SSD_CONV, -1)[:, 0].transpose(1, 2, 0, 3).reshape(
        DEPTH, SSD_CONV, SSD_CONV_DIM)
    small = {k: w[k] for k in LAYER_SMALL}
    small["ssd_conv_w"] = conv_full
    p0 = {k: a[0] for k, a in small.items()}
    p1 = {k: a[1] for k, a in small.items()}

    p0["norm_mix"] = p0["norm_mix"] + token[0, 0]
    pending = {}

    def pass_on_ffn0(u):
        pending["ffn0"] = _gather_forward(sems_b, kept_b, lands_b, u, "gather0b_forward")

    def ffn0_matrices(x1):
        sems, lands = pending["ffn0"]
        lands = _gather_finish(sems, lands, x1, "gather0b_finish")
        pending["layer1"] = _gather_forward(sems1, kept1, lands1, lands[0], "gather1_forward")
        return assemble(FFN_BIG, lands, ffn0)

    h1, saved0 = _layer_forward(x[0], p0, big0, 0, pass_on_ffn0, ffn0_matrices)
    big0 = {**big0, **saved0["ffn_matrices"]}
    sems1, lands1 = pending["layer1"]
    lands1 = _gather_finish(sems1, lands1, h1, "gather1_finish")
    big1 = assemble(BIG, lands1, blocks1)
    h2, saved1 = _layer_forward(h1, p1, big1, 1)
    loss_row, dx, dxb, g_final = _final_loss(h2, w["norm_final"].reshape(1, -1), loss_target[0], "final_loss")
    loss_part, g_final = loss_row[0, 0], g_final[0]

    def halves_view(k, a):
        if k in COL_SHARDED:
            return a.reshape(1, 2, a.shape[0] // 2, a.shape[1])
        return a.reshape(4, 2, a.shape[0] // 8, a.shape[1])

    def to_chunks(k, part):
        if k in COL_SHARDED:
            a = _unpad_in_proj(part[0])
            return a.reshape(a.shape[0], 4, a.shape[1] // 4).transpose(1, 0, 2)
        return part.reshape(4, -1, part.shape[-1])

    def reduce_begin(names, views, tag):
        recv = _swap_halves(views, tag + "swap")
        parts = [_pair_add_halves(a, r, tag + "pair_" + k) for k, a, r in zip(names, views, recv)]
        chunks = [to_chunks(k, p) for k, p in zip(names, parts)]
        return _scatter_start(chunks, tag + "scatter_start")

    def reduce_end(names, handle, after, tag):
        sems, kept, lands, _ = handle
        lands = _scatter_finish(sems, kept, lands, after, tag + "scatter_finish")
        return [_sum_chunks(a, b, chunk_order, tag + "sum_" + k) for k, a, b in zip(names, lands, kept)]

    dx, dxb, g1 = _layer_backward(dx, dxb, p1, big1, saved1, 1)
    round1 = reduce_begin(BIG, [halves_view(k, g1[k]) for k in BIG], "grad1_")
    p0["norm_ffn"] = p0["norm_ffn"] + round1[3][0, 0]

    early = FFN_BIG + ("w_out",)
    late = ("w_in", "s5_w_glu")

    def send_early(g_so_far):
        pending["early"] = reduce_begin(early, [halves_view(k, g_so_far[k]) for k in early], "grad0a_")
        return pending["early"][3]

    grad_x, _, g0 = _layer_backward(dx, dxb, p0, big0, saved0, 0, send_early)
    g = {k: [g0[k], g1[k]] for k in LAYER_SMALL}
    reduced1 = dict(zip(BIG, reduce_end(BIG, round1, grad_x, "grad1_")))
    shared1 = dict(zip(BIG, _sibling_swap([reduced1[k] for k in BIG], "grad1_share")))

    bc_rows = 2 * DEPTH * S5_GROUP * S5_GROUPS
    b_all = jnp.stack([g["s5_b_re"][0], g["s5_b_im"][0], g["s5_b_re"][1], g["s5_b_im"][1]]).reshape(bc_rows, S5_STATE)
    c_all = jnp.stack([g["s5_c_re"][0], g["s5_c_im"][0], g["s5_c_re"][1], g["s5_c_im"][1]]).reshape(bc_rows, S5_STATE)
    late_names = late + ("s5_bc",)
    round0 = reduce_begin(late_names, [halves_view(k, g0[k]) for k in late] + [jnp.stack([b_all, c_all])[None]],
                          "grad0b_")

    delta, new_m, new_v, grads = {}, {}, {}, {}
    adam1 = {}
    for k in BIG:
        adam1[k] = _adamw_layer(stored(k, w[k]), reduced1[k], shared1[k], stored(k, m[k]), stored(k, v[k]), 1,
                                [round0[3]], "adamw1_" + k)
    reduced0 = dict(zip(early, reduce_end(early, pending["early"], adam1[BIG[-1]][0], "grad0a_")))
    reduced0.update(zip(late_names, reduce_end(late_names, round0, adam1[BIG[-1]][0], "grad0b_")))
    shared0 = dict(zip(BIG, _sibling_swap([reduced0[k] for k in BIG], "grad0_share")))
    for k in BIG:
        outs = _adamw_layer(stored(k, w[k]), reduced0[k], shared0[k], stored(k, m[k]), stored(k, v[k]), 0, adam1[k],
                            "adamw0_" + k)
        delta[k], new_m[k], new_v[k], grads[k] = (stored(k, a) for a in outs)
    reduced = [reduced0["s5_bc"]]

    tiny_names = TINY + ("norm_final",)
    parts = [jnp.stack(g[k]) for k in TINY] + [g_final, loss_part.reshape(1)]
    shapes = [p.shape for p in parts]
    allparts, bc_eighths = _all_gather8([_flat_pack(parts), reduced[-1]], "gather_small")
    unpacked = _flat_unpack(_sum_leading(allparts, "sum_small"), shapes)
    loss = unpacked[-1][0]
    grads.update(zip(tiny_names, unpacked[:-1]))
    width = SSD_CONV_DIM // 4
    grads["ssd_conv_w"] = lax.dynamic_slice_in_dim(grads["ssd_conv_w"], chip * width, width, axis=2)
    bc = bc_eighths.reshape(4, 2, bc_rows // 4, S5_STATE)
    b_sum = bc[:, 0].reshape(DEPTH, 2, S5_GROUP, S5_GROUPS, S5_STATE)
    c_sum = bc[:, 1].reshape(DEPTH, 2, S5_GROUPS, S5_GROUP, S5_STATE)
    grads["s5_b_re"] = b_sum[:, 0].transpose(0, 2, 3, 1)
    grads["s5_b_im"] = b_sum[:, 1].transpose(0, 2, 3, 1)
    grads["s5_c_re"] = c_sum[:, 0]
    grads["s5_c_im"] = c_sum[:, 1]

    for k in ("s5_b_re", "s5_b_im"):
        shp = w[k].shape
        rows = lambda a: a.reshape(DEPTH, -1, shp[-1])
        d, nm, nv = _adamw_rows(rows(w[k]), rows(grads[k]), rows(m[k]), rows(v[k]), "adamw_" + k)
        delta[k], new_m[k], new_v[k] = d.reshape(shp), nm.reshape(shp), nv.reshape(shp)
    names = tiny_names + ("s5_c_re", "s5_c_im")
    as2d = lambda a: a.reshape(1, -1) if a.ndim == 1 else a
    ds, nms, nvs = _adamw_many([as2d(w[k]) for k in names], [as2d(grads[k]) for k in names],
                               [as2d(m[k]) for k in names], [as2d(v[k]) for k in names], "adamw_small")
    for k, a, b, c in zip(names, ds, nms, nvs):
        delta[k], new_m[k], new_v[k] = (t.reshape(w[k].shape) for t in (a, b, c))

    return (loss, grad_x[None], *[grads[k] for k in WEIGHTS], *[delta[k] for k in WEIGHTS],
            *[new_m[k] for k in WEIGHTS], *[new_v[k] for k in WEIGHTS])
```

```python
import functools
import math

import jax
import jax.numpy as jnp
from jax import lax
from jax.experimental import pallas as pl
from jax.experimental.pallas import tpu as pltpu

F32 = jnp.float32
BF16 = jnp.bfloat16
MESH = pl.DeviceIdType.MESH
ANY = pl.BlockSpec(memory_space=pl.ANY)

D_MODEL = 1024
DEPTH = 2
S5_GROUPS = 64
S5_GROUP = 16
S5_STATE = 64
S5_COLS = S5_GROUPS * S5_STATE
S5_TILE_GROUPS = 8
S5_TILES = S5_GROUPS // S5_TILE_GROUPS
S5_TILE_IN = S5_TILE_GROUPS * S5_GROUP
S5_TILE_ST = S5_TILE_GROUPS * S5_STATE
SEGS = 8
SSD_HEADS = 16
SSD_HEAD_DIM = 64
SSD_GROUPS = 2
SSD_GROUP_HEADS = SSD_HEADS // SSD_GROUPS
SSD_STATE = 128
SSD_CONV = 4
SSD_CHUNK = 128
SSD_WIDTH = 1024
SSD_CONV_DIM = SSD_WIDTH + 2 * SSD_GROUPS * SSD_STATE
IN_PROJ = 3600
IN_MAIN = 3584
IN_PAD = IN_MAIN + 2 * 128
FFN = 2816
EPS = 1e-6
LANES = 128
ROW_TILE = 256

ADAM_LR = 0.001
ADAM_B1 = 0.9
ADAM_B2 = 0.999
ADAM_EPS = 1e-08
ADAM_WD = 0.01
ADAM_STEP = 10


def _sigmoid(x):
    return 1.0 / (1.0 + jnp.exp(-x))


def _silu(x):
    return x * _sigmoid(x)


def _dsilu(x):
    s = _sigmoid(x)
    return s * (1.0 + x * (1.0 - s))


_GELU_K = math.sqrt(2.0 / math.pi)
_GELU_C = 0.044715


def _gelu(x):
    t = jnp.tanh(_GELU_K * (x + _GELU_C * x * x * x))
    return 0.5 * x * (1.0 + t)


def _dgelu(x):
    t = jnp.tanh(_GELU_K * (x + _GELU_C * x * x * x))
    return 0.5 * (1.0 + t) + 0.5 * x * (1.0 - t * t) * _GELU_K * (1.0 + 3.0 * _GELU_C * x * x)


def _softplus(x):
    e = jnp.exp(-jnp.abs(x))
    u = 1.0 + e
    log1p = jnp.where(u == 1.0, e, jnp.log(u) * e / jnp.where(u == 1.0, 1.0, u - 1.0))
    return jnp.maximum(x, 0.0) + log1p


def _rstd(x):
    return lax.rsqrt(jnp.mean(x * x, axis=-1, keepdims=True) + EPS)


def _rms_bwd(x, r, gain, dy):
    dyg = dy * gain
    dx = r * dyg - x * (r * r * r) * jnp.mean(x * dyg, axis=-1, keepdims=True)
    dgain = jnp.sum(dy * x * r, axis=0, keepdims=True)
    return dx, dgain


def _dot(a, b):
    return jnp.dot(a, b, preferred_element_type=F32)


def _dot_nt(a, b):
    return lax.dot_general(a, b, (((1,), (1,)), ((), ())), preferred_element_type=F32)


def _dot_tn(a, b):
    return lax.dot_general(a, b, (((0,), (0,)), ((), ())), preferred_element_type=F32)


def _row_spec(tile, cols):
    return pl.BlockSpec((tile, cols), lambda i: (i, 0))


def _full_spec(shape):
    nd = len(shape)
    return pl.BlockSpec(shape, lambda *_: (0,) * nd)


def _const_spec(shape):
    nd = len(shape)
    return pl.BlockSpec(shape, lambda *_: (0,) * nd, pipeline_mode=pl.Buffered(1))


def _layer_spec(shape, layer, block=0):
    if layer is None:
        return pl.BlockSpec(tuple(shape), lambda *_: (block, 0), pipeline_mode=pl.Buffered(1))
    return pl.BlockSpec((None,) + tuple(shape), lambda *_: (layer, block, 0), pipeline_mode=pl.Buffered(1))


def _acc_rows(ref, val, first):
    @pl.when(first)
    def _():
        ref[...] = val

    @pl.when(jnp.logical_not(first))
    def _():
        ref[...] += val


def _pick_tile(n, cap):
    best = LANES
    for t in range(LANES, cap + 1, LANES):
        if n % t == 0:
            best = t
    return best


def _mm_tn(a, b, name):
    k, m = a.shape
    _, n = b.shape
    tm = _pick_tile(m, 512)
    tn = _pick_tile(n, 1536)

    def body(a_ref, b_ref, o_ref):
        o_ref[...] = _dot_tn(a_ref[...], b_ref[...]).astype(BF16)

    return pl.pallas_call(
        body, name=name, grid=(n // tn, m // tm),
        in_specs=[pl.BlockSpec((k, tm), lambda j, i: (0, i)), pl.BlockSpec((k, tn), lambda j, i: (0, j))],
        out_specs=pl.BlockSpec((tm, tn), lambda j, i: (i, j)),
        out_shape=jax.ShapeDtypeStruct((m, n), BF16),
    )(a, b)


def _rms_inproj(x, gain, w_pad, layer, name):
    L = x.shape[0]

    def body(x_ref, g_ref, w_ref, u_ref, z_ref, xbc_ref, dt_ref, h_ref):
        xv = x_ref[...]
        h = (xv * _rstd(xv) * g_ref[...]).astype(BF16)
        h_ref[...] = h
        p = _dot(h, w_ref[...])
        u_ref[...] = p[:, :1024]
        z_ref[...] = p[:, 1024:2048]
        xbc_ref[...] = p[:, 2048:IN_MAIN]
        dt_ref[...] = p[:, IN_MAIN:IN_PAD]

    return pl.pallas_call(
        body, name=name, grid=(L // ROW_TILE,),
        in_specs=[_row_spec(ROW_TILE, D_MODEL), _full_spec((1, D_MODEL)), _layer_spec((D_MODEL, IN_PAD), layer)],
        out_specs=[_row_spec(ROW_TILE, 1024), _row_spec(ROW_TILE, 1024), _row_spec(ROW_TILE, SSD_CONV_DIM),
                   _row_spec(ROW_TILE, 256), _row_spec(ROW_TILE, D_MODEL)],
        out_shape=[jax.ShapeDtypeStruct((L, 1024), F32), jax.ShapeDtypeStruct((L, 1024), F32),
                   jax.ShapeDtypeStruct((L, SSD_CONV_DIM), F32), jax.ShapeDtypeStruct((L, 256), F32),
                   jax.ShapeDtypeStruct((L, D_MODEL), BF16)],
    )(x, gain, w_pad)


def _s5_prep_math(lr, li, ls, bre, bim):
    step = jnp.exp(ls)
    mag = jnp.exp(lr * step)
    ang = li * step
    are = mag * jnp.cos(ang)
    aim = mag * jnp.sin(ang)
    den = lr * lr + li * li
    nr = are - 1.0
    ni = aim
    cre = (nr * lr + ni * li) / den
    cim = (ni * lr - nr * li) / den
    bbre = cre[None] * bre - cim[None] * bim
    bbim = cre[None] * bim + cim[None] * bre
    return are, aim, bbre, bbim


def _s5_prep(lr, li, ls, bre, bim, name):
    def body(lr_ref, li_ref, ls_ref, bre_ref, bim_ref, are_ref, aim_ref, bbre_ref, bbim_ref):
        are, aim, bbre, bbim = _s5_prep_math(lr_ref[...], li_ref[...], ls_ref[...], bre_ref[...], bim_ref[...])
        are_ref[...] = are
        aim_ref[...] = aim
        bbre_ref[...] = bbre
        bbim_ref[...] = bbim

    gp = jax.ShapeDtypeStruct((S5_GROUPS, S5_STATE), F32)
    hgp = jax.ShapeDtypeStruct((S5_GROUP, S5_GROUPS, S5_STATE), F32)
    return pl.pallas_call(body, name=name, out_shape=[gp, gp, hgp, hgp])(lr, li, ls, bre, bim)


def _s5_prep_bwd(lr, li, ls, bre, bim, dare, daim, dbbre, dbbim, name):
    def body(lr_ref, li_ref, ls_ref, bre_ref, bim_ref, dare_ref, daim_ref, dbbre_ref, dbbim_ref,
             dlr_ref, dli_ref, dls_ref, dbre_ref, dbim_ref):
        _, vjp = jax.vjp(_s5_prep_math, lr_ref[...], li_ref[...], ls_ref[...], bre_ref[...], bim_ref[...])
        dlr, dli, dls, dbre, dbim = vjp((dare_ref[...], daim_ref[...], dbbre_ref[...], dbbim_ref[...]))
        dlr_ref[...] = dlr
        dli_ref[...] = dli
        dls_ref[...] = dls
        dbre_ref[...] = dbre
        dbim_ref[...] = dbim

    gp = jax.ShapeDtypeStruct((S5_GROUPS, S5_STATE), F32)
    g1 = jax.ShapeDtypeStruct((S5_GROUPS, 1), F32)
    hgp = jax.ShapeDtypeStruct((S5_GROUP, S5_GROUPS, S5_STATE), F32)
    return pl.pallas_call(body, name=name, out_shape=[gp, gp, g1, hgp, hgp])(
        lr, li, ls, bre, bim, dare, daim, dbbre, dbbim)


def _cmul_add(ar, ai, sr, si, br, bi):
    return ar * sr - ai * si + br, ar * si + ai * sr + bi


def _shift_rows_down(v):
    rolled = pltpu.roll(v, 1, 0)
    row = lax.broadcasted_iota(jnp.int32, v.shape, 0)
    return jnp.where(row == 0, 0.0, rolled)


def _shift_rows_up(v):
    rolled = pltpu.roll(v, SEGS - 1, 0)
    row = lax.broadcasted_iota(jnp.int32, v.shape, 0)
    return jnp.where(row == SEGS - 1, 0.0, rolled)


def _segment_power(ar, ai, steps):
    n = 1
    while n < steps:
        ar, ai = ar * ar - ai * ai, 2.0 * ar * ai
        n *= 2
    assert n == steps
    return ar, ai


def _segment_entries(ar, ai, fr, fi, steps, shift):
    pr, pi = _segment_power(ar, ai, steps)
    er = jnp.zeros_like(fr)
    ei = jnp.zeros_like(fi)
    for _ in range(SEGS - 1):
        nr, ni = _cmul_add(pr, pi, er, ei, fr, fi)
        er, ei = shift(nr), shift(ni)
    return er, ei


def _s5_scan(u_perm, bre_bd, bim_bd, cre_bd, cim_bd, are, aim, name):
    L = u_perm.shape[0]
    steps = L // SEGS

    def body(u_ref, bre_ref, bim_ref, cre_ref, cim_ref, are_ref, aim_ref, y_ref, xr_ref, xi_ref):
        u = u_ref[...].astype(BF16)
        xr_ref[...] = _dot(u, bre_ref[0])
        xi_ref[...] = _dot(u, bim_ref[0])
        ar = jnp.broadcast_to(are_ref[0], (SEGS, S5_TILE_ST))
        ai = jnp.broadcast_to(aim_ref[0], (SEGS, S5_TILE_ST))
        zero = jnp.zeros((SEGS, S5_TILE_ST), F32)

        def finals(j, c):
            rows = pl.ds(pl.multiple_of(j * SEGS, SEGS), SEGS)
            return _cmul_add(ar, ai, c[0], c[1], xr_ref[rows, :], xi_ref[rows, :])

        fr, fi = lax.fori_loop(0, steps, finals, (zero, zero), unroll=4)
        er, ei = _segment_entries(ar, ai, fr, fi, steps, _shift_rows_down)

        def scan(j, c):
            rows = pl.ds(pl.multiple_of(j * SEGS, SEGS), SEGS)
            sr, si = _cmul_add(ar, ai, c[0], c[1], xr_ref[rows, :], xi_ref[rows, :])
            xr_ref[rows, :] = sr
            xi_ref[rows, :] = si
            return sr, si

        lax.fori_loop(0, steps, scan, (er, ei), unroll=4)
        y_ref[...] = (_dot(xr_ref[...].astype(BF16), cre_ref[0]) - _dot(xi_ref[...].astype(BF16), cim_ref[0]))

    tile3 = lambda a, b: pl.BlockSpec((1, a, b), lambda k: (k, 0, 0))
    return pl.pallas_call(
        body, name=name, grid=(S5_TILES,),
        in_specs=[pl.BlockSpec((L, S5_TILE_IN), lambda k: (0, k)),
                  tile3(S5_TILE_IN, S5_TILE_ST), tile3(S5_TILE_IN, S5_TILE_ST),
                  tile3(S5_TILE_ST, S5_TILE_IN), tile3(S5_TILE_ST, S5_TILE_IN),
                  tile3(1, S5_TILE_ST), tile3(1, S5_TILE_ST)],
        out_specs=[pl.BlockSpec((L, S5_TILE_IN), lambda k: (0, k)),
                   pl.BlockSpec((L, S5_TILE_ST), lambda k: (0, k)), pl.BlockSpec((L, S5_TILE_ST), lambda k: (0, k))],
        out_shape=[jax.ShapeDtypeStruct((L, 1024), F32), jax.ShapeDtypeStruct((L, S5_COLS), F32),
                   jax.ShapeDtypeStruct((L, S5_COLS), F32)],
    )(u_perm, bre_bd, bim_bd, cre_bd, cim_bd, are, aim)


def _s5_scan_bwd(dy_perm, u_perm, xr, xi, bret_bd, bimt_bd, cret_bd, cimt_bd, are, aim, name):
    L = u_perm.shape[0]
    steps = L // SEGS

    def body(dy_ref, u_ref, xr_ref, xi_ref, bret_ref, bimt_ref, cret_ref, cimt_ref, are_ref, aim_ref,
             du_ref, dar_ref, dai_ref, dcre_ref, dcim_ref, dbre_ref, dbim_ref, gr_ref, gi_ref):
        dy = dy_ref[...].astype(BF16)
        u = u_ref[...].astype(BF16)
        gr_ref[...] = _dot(dy, cret_ref[0])
        gi_ref[...] = -_dot(dy, cimt_ref[0])
        ar = jnp.broadcast_to(are_ref[0], (SEGS, S5_TILE_ST))
        ai = -jnp.broadcast_to(aim_ref[0], (SEGS, S5_TILE_ST))
        zero = jnp.zeros((SEGS, S5_TILE_ST), F32)

        def finals(k, c):
            rows = pl.ds(pl.multiple_of((steps - 1 - k) * SEGS, SEGS), SEGS)
            return _cmul_add(ar, ai, c[0], c[1], gr_ref[rows, :], gi_ref[rows, :])

        fr, fi = lax.fori_loop(0, steps, finals, (zero, zero), unroll=4)
        er, ei = _segment_entries(ar, ai, fr, fi, steps, _shift_rows_up)

        def scan(k, c):
            sr0, si0, accr, acci = c
            j = steps - 1 - k
            rows = pl.ds(pl.multiple_of(j * SEGS, SEGS), SEGS)
            sr, si = _cmul_add(ar, ai, sr0, si0, gr_ref[rows, :], gi_ref[rows, :])
            gr_ref[rows, :] = sr
            gi_ref[rows, :] = si
            prev = pl.ds(pl.multiple_of(jnp.maximum(j - 1, 0) * SEGS, SEGS), SEGS)
            live = (j > 0).astype(F32)
            xpr = xr_ref[prev, :] * live
            xpi = xi_ref[prev, :] * live
            return sr, si, accr + sr * xpr + si * xpi, acci + si * xpr - sr * xpi

        _, _, accr, acci = lax.fori_loop(0, steps, scan, (er, ei, zero, zero), unroll=4)
        first = pl.ds(0, SEGS)
        last = pl.ds((steps - 1) * SEGS, SEGS)
        xpr = _shift_rows_down(xr_ref[last, :])
        xpi = _shift_rows_down(xi_ref[last, :])
        g0r = gr_ref[first, :]
        g0i = gi_ref[first, :]
        accr = accr + g0r * xpr + g0i * xpi
        acci = acci + g0i * xpr - g0r * xpi
        dar_ref[0] = jnp.sum(accr, axis=0, keepdims=True)
        dai_ref[0] = jnp.sum(acci, axis=0, keepdims=True)

        grb = gr_ref[...].astype(BF16)
        gib = gi_ref[...].astype(BF16)
        du_ref[...] = _dot(grb, bret_ref[0]) + _dot(gib, bimt_ref[0])
        dbre_ref[0] = _dot_tn(u, grb)
        dbim_ref[0] = _dot_tn(u, gib)
        dcre_ref[0] = _dot_tn(dy, xr_ref[...].astype(BF16))
        dcim_ref[0] = -_dot_tn(dy, xi_ref[...].astype(BF16))

    tile3 = lambda a, b: pl.BlockSpec((1, a, b), lambda k: (k, 0, 0))
    col_in = pl.BlockSpec((L, S5_TILE_IN), lambda k: (0, k))
    col_st = pl.BlockSpec((L, S5_TILE_ST), lambda k: (0, k))
    dense = jax.ShapeDtypeStruct((S5_TILES, S5_TILE_IN, S5_TILE_ST), F32)
    vec = jax.ShapeDtypeStruct((S5_TILES, 1, S5_TILE_ST), F32)
    return pl.pallas_call(
        body, name=name, grid=(S5_TILES,),
        in_specs=[col_in, col_in, col_st, col_st,
                  tile3(S5_TILE_ST, S5_TILE_IN), tile3(S5_TILE_ST, S5_TILE_IN),
                  tile3(S5_TILE_IN, S5_TILE_ST), tile3(S5_TILE_IN, S5_TILE_ST),
                  tile3(1, S5_TILE_ST), tile3(1, S5_TILE_ST)],
        out_specs=[col_in, tile3(1, S5_TILE_ST), tile3(1, S5_TILE_ST),
                   tile3(S5_TILE_IN, S5_TILE_ST), tile3(S5_TILE_IN, S5_TILE_ST),
                   tile3(S5_TILE_IN, S5_TILE_ST), tile3(S5_TILE_IN, S5_TILE_ST)],
        out_shape=[jax.ShapeDtypeStruct((L, 1024), F32), vec, vec, dense, dense, dense, dense],
        scratch_shapes=[pltpu.VMEM((L, S5_TILE_ST), F32), pltpu.VMEM((L, S5_TILE_ST), F32)],
    )(dy_perm, u_perm, xr, xi, bret_bd, bimt_bd, cret_bd, cimt_bd, are, aim)


def _s5_post(ys, u, d_skip, w_glu, b_glu, gain, layer, name):
    L = ys.shape[0]

    def body(ys_ref, u_ref, d_ref, w_ref, b_ref, g_ref, ya_ref):
        g = _gelu(ys_ref[...] + d_ref[...] * u_ref[...])
        q = _dot(g.astype(BF16), w_ref[...]) + b_ref[...]
        oa = g * _sigmoid(q)
        ya_ref[...] = (oa * _rstd(oa) * g_ref[...]).astype(BF16)

    vec = _full_spec((1, 1024))
    return pl.pallas_call(
        body, name=name, grid=(L // ROW_TILE,),
        in_specs=[_row_spec(ROW_TILE, 1024), _row_spec(ROW_TILE, 1024), vec, _layer_spec((1024, 1024), layer), vec,
                  vec],
        out_specs=_row_spec(ROW_TILE, 1024),
        out_shape=jax.ShapeDtypeStruct((L, 1024), BF16),
    )(ys, u, d_skip, w_glu, b_glu, gain)


def _s5_post_bwd(dx, w_out, ys, u, d_skip, w_glu, b_glu, gain, layer, name):
    L = ys.shape[0]

    def body(dx_ref, wo_ref, ys_ref, u_ref, d_ref, w_ref, b_ref, gn_ref,
             dys_ref, dus_ref, g_ref, dq_ref, dgain_ref, dd_ref, db_ref):
        first = pl.program_id(0) == 0
        uv = u_ref[...]
        yt = ys_ref[...] + d_ref[...] * uv
        g = _gelu(yt)
        gb = g.astype(BF16)
        q = _dot(gb, w_ref[...]) + b_ref[...]
        s = _sigmoid(q)
        oa = g * s
        dya = _dot_nt(dx_ref[...], wo_ref[...])
        doa, dgain = _rms_bwd(oa, _rstd(oa), gn_ref[...], dya)
        dq = doa * g * s * (1.0 - s)
        dqb = dq.astype(BF16)
        dg = doa * s + _dot_nt(dqb, w_ref[...])
        dyt = dg * _dgelu(yt)
        dys_ref[...] = dyt
        dus_ref[...] = dyt * d_ref[...]
        g_ref[...] = gb
        dq_ref[...] = dqb
        _acc_rows(dgain_ref, dgain, first)
        _acc_rows(dd_ref, jnp.sum(dyt * uv, axis=0, keepdims=True), first)
        _acc_rows(db_ref, jnp.sum(dq, axis=0, keepdims=True), first)

    vec = _full_spec((1, 1024))
    row = _row_spec(ROW_TILE, 1024)
    vshape = jax.ShapeDtypeStruct((1, 1024), F32)
    return pl.pallas_call(
        body, name=name, grid=(L // ROW_TILE,),
        in_specs=[row, _layer_spec((1024, 1024), layer, 0), row, row, vec, _layer_spec((1024, 1024), layer), vec,
                  vec],
        out_specs=[row, row, row, row, vec, vec, vec],
        out_shape=[jax.ShapeDtypeStruct((L, 1024), F32), jax.ShapeDtypeStruct((L, 1024), F32),
                   jax.ShapeDtypeStruct((L, 1024), BF16), jax.ShapeDtypeStruct((L, 1024), BF16),
                   vshape, vshape, vshape],
    )(dx, w_out, ys, u, d_skip, w_glu, b_glu, gain)


CONV_TILE = 256


def _shift_time(v, d):
    if d == 0:
        return v
    rolled = pltpu.roll(v, d, 0)
    row = lax.broadcasted_iota(jnp.int32, v.shape, 0)
    return jnp.where(row < d, 0.0, rolled)


def _unshift_time(v, d):
    if d == 0:
        return v
    n = v.shape[0]
    rolled = pltpu.roll(v, n - d, 0)
    row = lax.broadcasted_iota(jnp.int32, v.shape, 0)
    return jnp.where(row >= n - d, 0.0, rolled)


def _ssd_conv(xbc, w, b, name):
    L = xbc.shape[0]

    def body(x_ref, w_ref, b_ref, o_ref):
        xv = x_ref[...]
        pre = jnp.broadcast_to(b_ref[...], xv.shape)
        for k in range(SSD_CONV):
            pre = pre + w_ref[k:k + 1, :] * _shift_time(xv, SSD_CONV - 1 - k)
        o_ref[...] = _silu(pre)

    col = pl.BlockSpec((L, CONV_TILE), lambda j: (0, j))
    return pl.pallas_call(
        body, name=name, grid=(SSD_CONV_DIM // CONV_TILE,),
        in_specs=[col, pl.BlockSpec((8, CONV_TILE), lambda j: (0, j)), pl.BlockSpec((1, CONV_TILE), lambda j: (0, j))],
        out_specs=col, out_shape=jax.ShapeDtypeStruct((L, SSD_CONV_DIM), F32),
    )(xbc, w, b)


def _ssd_conv_bwd(dxc, xbc, w, b, name):
    L = xbc.shape[0]

    def body(d_ref, x_ref, w_ref, b_ref, dx_ref, dw_ref, db_ref):
        xv = x_ref[...]
        shifted = [_shift_time(xv, SSD_CONV - 1 - k) for k in range(SSD_CONV)]
        pre = jnp.broadcast_to(b_ref[...], xv.shape)
        for k in range(SSD_CONV):
            pre = pre + w_ref[k:k + 1, :] * shifted[k]
        dpre = d_ref[...] * _dsilu(pre)
        dx = jnp.zeros_like(xv)
        rows = []
        for k in range(SSD_CONV):
            dx = dx + w_ref[k:k + 1, :] * _unshift_time(dpre, SSD_CONV - 1 - k)
            rows.append(jnp.sum(dpre * shifted[k], axis=0, keepdims=True))
        dx_ref[...] = dx
        dw_ref[...] = jnp.concatenate(rows + [jnp.zeros((8 - SSD_CONV, CONV_TILE), F32)], axis=0)
        db_ref[...] = jnp.sum(dpre, axis=0, keepdims=True)

    col = pl.BlockSpec((L, CONV_TILE), lambda j: (0, j))
    w_spec = pl.BlockSpec((8, CONV_TILE), lambda j: (0, j))
    b_spec = pl.BlockSpec((1, CONV_TILE), lambda j: (0, j))
    return pl.pallas_call(
        body, name=name, grid=(SSD_CONV_DIM // CONV_TILE,),
        in_specs=[col, col, w_spec, b_spec], out_specs=[col, w_spec, b_spec],
        out_shape=[jax.ShapeDtypeStruct((L, SSD_CONV_DIM), F32), jax.ShapeDtypeStruct((8, SSD_CONV_DIM), F32),
                   jax.ShapeDtypeStruct((1, SSD_CONV_DIM), F32)],
    )(dxc, xbc, w, b)


def _tri(lower):
    r = lax.broadcasted_iota(jnp.int32, (SSD_CHUNK, SSD_CHUNK), 0)
    c = lax.broadcasted_iota(jnp.int32, (SSD_CHUNK, SSD_CHUNK), 1)
    return (r >= c) if lower else (r <= c)


def _ssd_chunk_common(dt_ref, bias_ref, alog_ref):
    pre = dt_ref[...] + bias_ref[0]
    dtp = _softplus(pre)
    a_neg = -jnp.exp(alog_ref[0])
    dta = dtp * a_neg
    acum = _select_rows(_tri(True), dta)
    return pre, dtp, a_neg, dta, acum


GROUP_W = SSD_GROUP_HEADS * SSD_HEAD_DIM


def _head_expander():
    r = lax.broadcasted_iota(jnp.int32, (LANES, GROUP_W), 0)
    c = lax.broadcasted_iota(jnp.int32, (LANES, GROUP_W), 1)
    return (c // SSD_HEAD_DIM == r).astype(F32)


def _split_bf16(a, terms):
    parts = []
    rest = a
    for _ in range(terms):
        piece = rest.astype(BF16)
        parts.append(piece)
        rest = rest - piece.astype(F32)
    return parts


def _select_cols(a, sel, terms=3):
    lhs = jnp.concatenate(_split_bf16(a, terms), axis=1)
    rhs = jnp.concatenate([sel.astype(BF16)] * terms, axis=0)
    return _dot(lhs, rhs)


def _select_rows(sel, b, terms=3):
    lhs = jnp.concatenate([sel.astype(BF16)] * terms, axis=1)
    rhs = jnp.concatenate(_split_bf16(b, terms), axis=0)
    return _dot(lhs, rhs)


def _decay_mask(acum_all, acum_t, h, lower):
    seg = acum_all[:, h:h + 1] - acum_t[h:h + 1, :]
    return jnp.where(lower, jnp.exp(jnp.minimum(seg, 0.0)), 0.0)


def _ssd_scan(xc, dt, dt_bias, a_log, d_wide, expand, expand_t, name):
    L = xc.shape[0]
    nc = L // SSD_CHUNK

    def body(x_ref, b_ref, c_ref, dt_ref, bias_ref, alog_ref, d_ref, e_ref, et_ref, y_ref, sp_ref, s_ref, xdt_ref):
        @pl.when(pl.program_id(1) == 0)
        def _():
            s_ref[...] = jnp.zeros_like(s_ref)

        _, dtp_all, _, _, acum_all = _ssd_chunk_common(dt_ref, bias_ref, alog_ref)
        acum_t = acum_all.T
        wide = _select_cols(jnp.concatenate([acum_all, dtp_all], axis=0), e_ref[...])
        acum_e = wide[:SSD_CHUNK]
        alast_e = acum_e[SSD_CHUNK - 1:SSD_CHUNK, :]
        x = x_ref[...]
        xdt = x * wide[SSD_CHUNK:]
        xdt_ref[...] = xdt.astype(BF16)
        bm = b_ref[...].astype(BF16)
        cm = c_ref[...].astype(BF16)
        cb = _dot_nt(cm, bm)
        lower = _tri(True)
        sp = s_ref[...]
        for h in range(SSD_GROUP_HEADS):
            cols = slice(h * SSD_HEAD_DIM, (h + 1) * SSD_HEAD_DIM)
            lm = _decay_mask(acum_all, acum_t, h, lower)
            y_ref[:, cols] = _dot((cb * lm).astype(BF16), xdt_ref[:, cols])
        y_ref[...] += _dot_nt(cm, sp.astype(BF16)) * jnp.exp(acum_e) + d_ref[0] * x
        wgt = xdt * jnp.exp(alast_e - acum_e)
        ealast = jnp.exp(_select_rows(et_ref[...], acum_t)[:, SSD_CHUNK - 1:SSD_CHUNK])
        sp_ref[0, 0] = sp
        s_ref[...] = ealast * sp + _dot_tn(wgt.astype(BF16), bm)

    par = lambda n: pl.BlockSpec((1, 1, n), lambda g, c: (g, 0, 0))
    return pl.pallas_call(
        body, name=name, grid=(SSD_GROUPS, nc),
        in_specs=[pl.BlockSpec((SSD_CHUNK, GROUP_W), lambda g, c: (c, g)),
                  pl.BlockSpec((SSD_CHUNK, SSD_STATE), lambda g, c: (c, 8 + g)),
                  pl.BlockSpec((SSD_CHUNK, SSD_STATE), lambda g, c: (c, 10 + g)),
                  pl.BlockSpec((SSD_CHUNK, LANES), lambda g, c: (c, g)),
                  par(LANES), par(LANES), par(GROUP_W), _full_spec((LANES, GROUP_W)), _full_spec((GROUP_W, LANES))],
        out_specs=[pl.BlockSpec((SSD_CHUNK, GROUP_W), lambda g, c: (c, g)),
                   pl.BlockSpec((1, 1, GROUP_W, SSD_STATE), lambda g, c: (c, g, 0, 0))],
        out_shape=[jax.ShapeDtypeStruct((L, SSD_WIDTH), F32),
                   jax.ShapeDtypeStruct((nc, SSD_GROUPS, GROUP_W, SSD_STATE), F32)],
        scratch_shapes=[pltpu.VMEM((GROUP_W, SSD_STATE), F32), pltpu.VMEM((SSD_CHUNK, GROUP_W), BF16)],
    )(xc, xc, xc, dt, dt_bias, a_log, d_wide, expand, expand_t)


def _ssd_scan_bwd(dy, xc, dt, sprev, dt_bias, a_log, d_wide, expand, expand_t, name):
    L = xc.shape[0]
    nc = L // SSD_CHUNK

    def body(dy_ref, x_ref, b_ref, c_ref, dt_ref, sp_ref, bias_ref, alog_ref, d_ref, e_ref, et_ref,
             dx_ref, db_ref, dc_ref, ddt_ref, dbias_ref, dalog_ref, dd_ref, ds_ref, xdt_ref, dyb_ref):
        first = pl.program_id(1) == 0

        @pl.when(first)
        def _():
            ds_ref[...] = jnp.zeros_like(ds_ref)

        pre, dtp_all, a_neg, _, acum_all = _ssd_chunk_common(dt_ref, bias_ref, alog_ref)
        acum_t = acum_all.T
        e = e_ref[...]
        et = et_ref[...]
        wide = _select_cols(jnp.concatenate([acum_all, dtp_all], axis=0), e)
        acum_e = wide[:SSD_CHUNK]
        dtp_e = wide[SSD_CHUNK:]
        alast_e = acum_e[SSD_CHUNK - 1:SSD_CHUNK, :]
        dstate_e = jnp.exp(alast_e - acum_e)
        x = x_ref[...]
        dy = dy_ref[...]
        xdt = x * dtp_e
        xdt_ref[...] = xdt.astype(BF16)
        dyb_ref[...] = dy.astype(BF16)
        bm = b_ref[...].astype(BF16)
        cm = c_ref[...].astype(BF16)
        cb = _dot_nt(cm, bm)
        sp = sp_ref[0, 0]
        spb = sp.astype(BF16)
        dsn = ds_ref[...]
        dsb = dsn.astype(BF16)
        z = _dot_nt(cm, spb)
        dz = dy * jnp.exp(acum_e)
        dzb = dz.astype(BF16)
        dc_acc = _dot(dzb, spb)
        ealast = jnp.exp(_select_rows(et, acum_t)[:, SSD_CHUNK - 1:SSD_CHUNK])
        ds_ref[...] = _dot_tn(dzb, cm) + ealast * dsn
        dw = _dot_nt(bm, dsb)
        wgt = xdt * dstate_e
        db_acc = _dot(wgt.astype(BF16), dsb)
        lower = _tri(True)
        lane = lax.broadcasted_iota(jnp.int32, (SSD_CHUNK, LANES), 1)
        row = lax.broadcasted_iota(jnp.int32, (SSD_CHUNK, LANES), 0)
        dcb = jnp.zeros((SSD_CHUNK, SSD_CHUNK), F32)
        dacum_all = jnp.zeros((SSD_CHUNK, LANES), F32)
        dacum_cols = jnp.zeros((SSD_CHUNK, LANES), F32)
        for h in range(SSD_GROUP_HEADS):
            cols = slice(h * SSD_HEAD_DIM, (h + 1) * SSD_HEAD_DIM)
            lm = _decay_mask(acum_all, acum_t, h, lower)
            dm = _dot_nt(dyb_ref[:, cols], xdt_ref[:, cols])
            dx_ref[:, cols] = _dot_tn((cb * lm).astype(BF16), dyb_ref[:, cols])
            dm_lm = dm * lm
            dcb = dcb + dm_lm
            q = dm_lm * cb
            dacum_all = jnp.where(lane == h, jnp.sum(q, axis=1, keepdims=True), dacum_all)
            dacum_cols = jnp.where(row == h, jnp.sum(q, axis=0, keepdims=True), dacum_cols)
        dxdt = dx_ref[...] + dw * dstate_e
        sums = _select_cols(jnp.concatenate([dz * z, dw * wgt, dxdt * x, dy * x], axis=0), et, terms=2)
        dacum_off = sums[0:SSD_CHUNK]
        dds_ds = sums[SSD_CHUNK:2 * SSD_CHUNK]
        ddtp_x = sums[2 * SSD_CHUNK:3 * SSD_CHUNK]
        dd_part = sums[3 * SSD_CHUNK:4 * SSD_CHUNK]
        ds_s = jnp.sum(_select_rows(e, dsn * sp, terms=2).T, axis=0, keepdims=True)
        dalast = ds_s * jnp.exp(acum_all[SSD_CHUNK - 1:SSD_CHUNK, :]) + jnp.sum(dds_ds, axis=0, keepdims=True)
        dacum_all = dacum_all - dacum_cols.T + dacum_off - dds_ds + jnp.where(row == SSD_CHUNK - 1, dalast, 0.0)
        dx_ref[...] = d_ref[0] * dy + dxdt * dtp_e
        dcbb = dcb.astype(BF16)
        dc_ref[...] = dc_acc + _dot(dcbb, bm)
        db_ref[...] = db_acc + _dot_tn(dcbb, cm)
        ddta = _select_rows(_tri(False), dacum_all)
        ddt = (ddtp_x + ddta * a_neg) * _sigmoid(pre)
        ddt_ref[...] = ddt
        _acc_rows(dbias_ref, jnp.sum(ddt, axis=0, keepdims=True)[None], first)
        _acc_rows(dalog_ref, (jnp.sum(ddta * dtp_all, axis=0, keepdims=True) * a_neg)[None], first)
        _acc_rows(dd_ref, jnp.sum(dd_part, axis=0, keepdims=True)[None], first)

    rev = lambda c: nc - 1 - c
    par = lambda n: pl.BlockSpec((1, 1, n), lambda g, c: (g, 0, 0))
    pshape = jax.ShapeDtypeStruct((SSD_GROUPS, 1, LANES), F32)
    return pl.pallas_call(
        body, name=name, grid=(SSD_GROUPS, nc),
        in_specs=[pl.BlockSpec((SSD_CHUNK, GROUP_W), lambda g, c: (rev(c), g)),
                  pl.BlockSpec((SSD_CHUNK, GROUP_W), lambda g, c: (rev(c), g)),
                  pl.BlockSpec((SSD_CHUNK, SSD_STATE), lambda g, c: (rev(c), 8 + g)),
                  pl.BlockSpec((SSD_CHUNK, SSD_STATE), lambda g, c: (rev(c), 10 + g)),
                  pl.BlockSpec((SSD_CHUNK, LANES), lambda g, c: (rev(c), g)),
                  pl.BlockSpec((1, 1, GROUP_W, SSD_STATE), lambda g, c: (rev(c), g, 0, 0)),
                  par(LANES), par(LANES), par(GROUP_W), _full_spec((LANES, GROUP_W)), _full_spec((GROUP_W, LANES))],
        out_specs=[pl.BlockSpec((SSD_CHUNK, GROUP_W), lambda g, c: (rev(c), g)),
                   pl.BlockSpec((SSD_CHUNK, SSD_STATE), lambda g, c: (rev(c), g)),
                   pl.BlockSpec((SSD_CHUNK, SSD_STATE), lambda g, c: (rev(c), g)),
                   pl.BlockSpec((SSD_CHUNK, LANES), lambda g, c: (rev(c), g)),
                   par(LANES), par(LANES), par(LANES)],
        out_shape=[jax.ShapeDtypeStruct((L, SSD_WIDTH), F32), jax.ShapeDtypeStruct((L, 256), F32),
                   jax.ShapeDtypeStruct((L, 256), F32), jax.ShapeDtypeStruct((L, 256), F32),
                   pshape, pshape, pshape],
        scratch_shapes=[pltpu.VMEM((GROUP_W, SSD_STATE), F32), pltpu.VMEM((SSD_CHUNK, GROUP_W), BF16),
                        pltpu.VMEM((SSD_CHUNK, GROUP_W), BF16)],
    )(dy, xc, xc, xc, dt, sprev, dt_bias, a_log, d_wide, expand, expand_t)


def _ssd_post(y, z, gain, name):
    L = y.shape[0]

    def body(y_ref, z_ref, g_ref, o_ref):
        ob = y_ref[...] * _silu(z_ref[...])
        o_ref[...] = (ob * _rstd(ob) * g_ref[...]).astype(BF16)

    row = _row_spec(ROW_TILE, 1024)
    return pl.pallas_call(body, name=name, grid=(L // ROW_TILE,), in_specs=[row, row, _full_spec((1, 1024))],
                          out_specs=row, out_shape=jax.ShapeDtypeStruct((L, 1024), BF16))(y, z, gain)


def _ssd_post_bwd(dx, w_out, y, z, gain, layer, name):
    L = y.shape[0]

    def body(dx_ref, wo_ref, y_ref, z_ref, g_ref, dy_ref, dz_ref, dgain_ref):
        first = pl.program_id(0) == 0
        yv = y_ref[...]
        zv = z_ref[...]
        sz = _silu(zv)
        ob = yv * sz
        dyb = _dot_nt(dx_ref[...], wo_ref[...])
        dob, dgain = _rms_bwd(ob, _rstd(ob), g_ref[...], dyb)
        dy_ref[...] = dob * sz
        dz_ref[...] = dob * yv * _dsilu(zv)
        _acc_rows(dgain_ref, dgain, first)

    row = _row_spec(ROW_TILE, 1024)
    vec = _full_spec((1, 1024))
    return pl.pallas_call(
        body, name=name, grid=(L // ROW_TILE,),
        in_specs=[row, _layer_spec((1024, 1024), layer, 1), row, row, vec],
        out_specs=[row, row, vec],
        out_shape=[jax.ShapeDtypeStruct((L, 1024), F32), jax.ShapeDtypeStruct((L, 1024), F32),
                   jax.ShapeDtypeStruct((1, 1024), F32)],
    )(dx, w_out, y, z, gain)


def _out_proj(x, ya, yb, w_out, layer, name):
    L = x.shape[0]

    def body(x_ref, ya_ref, yb_ref, w_ref, o_ref):
        o_ref[...] = x_ref[...] + _dot(ya_ref[...], w_ref[:1024, :]) + _dot(yb_ref[...], w_ref[1024:, :])

    row = _row_spec(ROW_TILE, 1024)
    return pl.pallas_call(body, name=name, grid=(L // ROW_TILE,),
                          in_specs=[row, row, row, _layer_spec((2048, 1024), layer)],
                          out_specs=row, out_shape=jax.ShapeDtypeStruct((L, D_MODEL), F32))(x, ya, yb, w_out)


def _ffn(x, gain, w_gate, w_up, w_down, layer, name):
    L = x.shape[0]

    def body(x_ref, g_ref, wg_ref, wu_ref, wd_ref, o_ref, gt_ref, up_ref):
        xv = x_ref[...]
        h = (xv * _rstd(xv) * g_ref[...]).astype(BF16)
        gt = _dot_nt(h, wg_ref[...])
        up = _dot_nt(h, wu_ref[...])
        gt_ref[...] = gt
        up_ref[...] = up
        o_ref[...] = xv + _dot((_silu(gt) * up).astype(BF16), wd_ref[...])

    row = _row_spec(ROW_TILE, D_MODEL)
    hid = _row_spec(ROW_TILE, FFN)
    return pl.pallas_call(
        body, name=name, grid=(L // ROW_TILE,),
        in_specs=[row, _full_spec((1, D_MODEL)), _layer_spec((FFN, D_MODEL), layer),
                  _layer_spec((FFN, D_MODEL), layer), _layer_spec((FFN, D_MODEL), layer)],
        out_specs=[row, hid, hid],
        out_shape=[jax.ShapeDtypeStruct((L, D_MODEL), F32), jax.ShapeDtypeStruct((L, FFN), F32),
                   jax.ShapeDtypeStruct((L, FFN), F32)],
    )(x, gain, w_gate, w_up, w_down)


def _ffn_bwd(dx2, x1, gt, up, gain, w_gate, w_up, w_down, layer, name):
    L = x1.shape[0]

    def body(d_ref, x_ref, gt_ref, up_ref, g_ref, wg_ref, wu_ref, wd_ref,
             dx_ref, dxb_ref, h_ref, act_ref, dgt_ref, dup_ref, dgain_ref):
        first = pl.program_id(0) == 0
        dv = d_ref[...]
        xv = x_ref[...]
        r = _rstd(xv)
        h_ref[...] = (xv * r * g_ref[...]).astype(BF16)
        gtv = gt_ref[...]
        upv = up_ref[...]
        sg = _silu(gtv)
        act_ref[...] = (sg * upv).astype(BF16)
        dact = _dot_nt(dv.astype(BF16), wd_ref[...])
        dgt = (dact * upv * _dsilu(gtv)).astype(BF16)
        dup = (dact * sg).astype(BF16)
        dgt_ref[...] = dgt
        dup_ref[...] = dup
        dh = _dot(dgt, wg_ref[...]) + _dot(dup, wu_ref[...])
        dxn, dgain = _rms_bwd(xv, r, g_ref[...], dh)
        dx = dv + dxn
        dx_ref[...] = dx
        dxb_ref[...] = dx.astype(BF16)
        _acc_rows(dgain_ref, dgain, first)

    row = _row_spec(ROW_TILE, D_MODEL)
    hid = _row_spec(ROW_TILE, FFN)
    vec = _full_spec((1, D_MODEL))
    return pl.pallas_call(
        body, name=name, grid=(L // ROW_TILE,),
        in_specs=[row, row, hid, hid, vec, _layer_spec((FFN, D_MODEL), layer), _layer_spec((FFN, D_MODEL), layer),
                  _layer_spec((FFN, D_MODEL), layer)],
        out_specs=[row, row, row, hid, hid, hid, vec],
        out_shape=[jax.ShapeDtypeStruct((L, D_MODEL), F32), jax.ShapeDtypeStruct((L, D_MODEL), BF16),
                   jax.ShapeDtypeStruct((L, D_MODEL), BF16),
                   jax.ShapeDtypeStruct((L, FFN), BF16), jax.ShapeDtypeStruct((L, FFN), BF16),
                   jax.ShapeDtypeStruct((L, FFN), BF16), jax.ShapeDtypeStruct((1, D_MODEL), F32)],
    )(dx2, x1, gt, up, gain, w_gate, w_up, w_down)


def _inproj_bwd(dx1, x0, du_skip, du_scan, dz, dxbc, ddt, gain, w_pad, layer, name):
    L = x0.shape[0]

    def body(d_ref, x_ref, dus_ref, duc_ref, dz_ref, dxbc_ref, ddt_ref, g_ref, w_ref,
             dx_ref, dxb_ref, dp_ref, dgain_ref):
        first = pl.program_id(0) == 0
        xv = x_ref[...]
        dp = jnp.concatenate([dus_ref[...] + duc_ref[...], dz_ref[...], dxbc_ref[...], ddt_ref[...]],
                             axis=1).astype(BF16)
        dp_ref[...] = dp
        dh = _dot_nt(dp, w_ref[...])
        dxn, dgain = _rms_bwd(xv, _rstd(xv), g_ref[...], dh)
        dx = d_ref[...] + dxn
        dx_ref[...] = dx
        dxb_ref[...] = dx.astype(BF16)
        _acc_rows(dgain_ref, dgain, first)

    row = _row_spec(ROW_TILE, D_MODEL)
    vec = _full_spec((1, D_MODEL))
    return pl.pallas_call(
        body, name=name, grid=(L // ROW_TILE,),
        in_specs=[row, row, row, row, row, _row_spec(ROW_TILE, SSD_CONV_DIM), _row_spec(ROW_TILE, 256), vec,
                  _layer_spec((D_MODEL, IN_PAD), layer)],
        out_specs=[row, row, _row_spec(ROW_TILE, IN_PAD), vec],
        out_shape=[jax.ShapeDtypeStruct((L, D_MODEL), F32), jax.ShapeDtypeStruct((L, D_MODEL), BF16),
                   jax.ShapeDtypeStruct((L, IN_PAD), BF16), jax.ShapeDtypeStruct((1, D_MODEL), F32)],
    )(dx1, x0, du_skip, du_scan, dz, dxbc, ddt, gain, w_pad)


def _final_loss(x, gain, target, name):
    L = x.shape[0]

    def body(x_ref, g_ref, t_ref, loss_ref, dx_ref, dxb_ref, dgain_ref):
        first = pl.program_id(0) == 0
        xv = x_ref[...]
        r = _rstd(xv)
        err = xv * r * g_ref[...] - t_ref[...]
        part = 0.5 * jnp.sum(jnp.mean(err * err, axis=-1, keepdims=True), axis=0, keepdims=True)
        dx, dgain = _rms_bwd(xv, r, g_ref[...], err * (1.0 / D_MODEL))
        dx_ref[...] = dx
        dxb_ref[...] = dx.astype(BF16)
        _acc_rows(loss_ref, jnp.broadcast_to(part, (1, LANES)), first)
        _acc_rows(dgain_ref, dgain, first)

    row = _row_spec(ROW_TILE, D_MODEL)
    vec = _full_spec((1, D_MODEL))
    return pl.pallas_call(
        body, name=name, grid=(L // ROW_TILE,), in_specs=[row, vec, row],
        out_specs=[_full_spec((1, LANES)), row, row, vec],
        out_shape=[jax.ShapeDtypeStruct((1, LANES), F32), jax.ShapeDtypeStruct((L, D_MODEL), F32),
                   jax.ShapeDtypeStruct((L, D_MODEL), BF16), jax.ShapeDtypeStruct((1, D_MODEL), F32)],
    )(x, gain, target)


def _to_segments(a):
    L, n = a.shape
    return a.reshape(SEGS, L // SEGS, n).transpose(1, 0, 2).reshape(L, n)


def _from_segments(a):
    L, n = a.shape
    return a.reshape(L // SEGS, SEGS, n).transpose(1, 0, 2).reshape(L, n)


def _block_diag_in_to_state(m):
    m = m.reshape(S5_TILES, S5_TILE_GROUPS, S5_GROUP, S5_STATE)
    eye = jnp.eye(S5_TILE_GROUPS, dtype=m.dtype)
    out = m[:, :, :, None, :] * eye[None, :, None, :, None]
    return out.reshape(S5_TILES, S5_TILE_IN, S5_TILE_ST)


def _block_diag_extract(d):
    d = d.reshape(S5_TILES, S5_TILE_GROUPS, S5_GROUP, S5_TILE_GROUPS, S5_STATE)
    d = jnp.stack([d[:, a, :, a, :] for a in range(S5_TILE_GROUPS)], axis=1)
    return d.reshape(S5_GROUPS, S5_GROUP, S5_STATE)


def _pad_in_proj(w):
    z = jnp.zeros(w.shape[:-1] + (LANES - SSD_GROUP_HEADS,), w.dtype)
    return jnp.concatenate([w[..., :IN_MAIN + 8], z, w[..., IN_MAIN + 8:], z], axis=-1)


def _unpad_in_proj(w):
    return jnp.concatenate([w[..., :IN_MAIN + 8], w[..., IN_MAIN + LANES:IN_MAIN + LANES + 8]], axis=-1)


def _pad_heads(v):
    v = v.reshape(SSD_GROUPS, 1, SSD_GROUP_HEADS)
    return jnp.pad(v, ((0, 0), (0, 0), (0, LANES - SSD_GROUP_HEADS)))


def _unpad_heads(v):
    return v[:, 0, :SSD_GROUP_HEADS].reshape(SSD_HEADS)


def _layer_forward(x0, p, big, i, after_inproj=None, before_ffn=None):
    tag = "l%d_" % i
    ls = p["s5_log_step"].reshape(S5_GROUPS, 1)
    b_hgp = (p["s5_b_re"].transpose(2, 0, 1), p["s5_b_im"].transpose(2, 0, 1))
    are, aim, bbre, bbim = _s5_prep(p["s5_lam_re"], p["s5_lam_im"], ls, b_hgp[0], b_hgp[1], tag + "s5_prep")
    bre_ghp = bbre.transpose(1, 0, 2)
    bim_ghp = bbim.transpose(1, 0, 2)
    bre_bd = _block_diag_in_to_state(bre_ghp).astype(BF16)
    bim_bd = _block_diag_in_to_state(bim_ghp).astype(BF16)
    cret_bd = _block_diag_in_to_state(p["s5_c_re"]).astype(BF16)
    cimt_bd = _block_diag_in_to_state(p["s5_c_im"]).astype(BF16)
    s5mats = dict(bre_bd=bre_bd, bim_bd=bim_bd, cret_bd=cret_bd, cimt_bd=cimt_bd,
                  bret_bd=bre_bd.transpose(0, 2, 1), bimt_bd=bim_bd.transpose(0, 2, 1),
                  cre_bd=cret_bd.transpose(0, 2, 1), cim_bd=cimt_bd.transpose(0, 2, 1),
                  are=are.reshape(S5_TILES, 1, S5_TILE_ST), aim=aim.reshape(S5_TILES, 1, S5_TILE_ST))

    u, z, xbc, dt, h1 = _rms_inproj(x0, p["norm_mix"].reshape(1, -1), big["w_in"], None, tag + "rms_inproj")
    if after_inproj is not None:
        after_inproj(u)
    u_perm = _to_segments(u)
    ys_perm, xr, xi = _s5_scan(u_perm, bre_bd, bim_bd, s5mats["cre_bd"], s5mats["cim_bd"],
                               s5mats["are"], s5mats["aim"], tag + "s5_scan")
    ys = _from_segments(ys_perm)
    ya = _s5_post(ys, u, p["s5_d"].reshape(1, -1), big["s5_w_glu"], p["s5_b_glu"].reshape(1, -1),
                  p["s5_norm"].reshape(1, -1), None, tag + "s5_post")

    conv_w = jnp.pad(p["ssd_conv_w"], ((0, 8 - SSD_CONV), (0, 0)))
    conv_b = p["ssd_conv_b"].reshape(1, -1)
    xc = _ssd_conv(xbc, conv_w, conv_b, tag + "ssd_conv")
    expand = _head_expander()
    heads = dict(dt_bias=_pad_heads(p["ssd_dt_bias"]), a_log=_pad_heads(p["ssd_a_log"]),
                 d=jnp.repeat(p["ssd_d"], SSD_HEAD_DIM).reshape(SSD_GROUPS, 1, GROUP_W),
                 expand=expand, expand_t=expand.T)
    y, sprev = _ssd_scan(xc, dt, heads["dt_bias"], heads["a_log"], heads["d"], expand, heads["expand_t"],
                         tag + "ssd_scan")
    yb = _ssd_post(y, z, p["ssd_norm"].reshape(1, -1), tag + "ssd_post")

    x1 = _out_proj(x0, ya, yb, big["w_out"], None, tag + "out_proj")
    ffn_matrices = before_ffn(x1) if before_ffn is not None else {}
    big = {**big, **ffn_matrices}
    x2, gt, up = _ffn(x1, p["norm_ffn"].reshape(1, -1), big["w_gate"], big["w_up"], big["w_down"], None,
                      tag + "ffn")
    saved = dict(x0=x0, h1=h1, u=u, u_perm=u_perm, z=z, xbc=xbc, dt=dt, xr=xr, xi=xi, ys=ys, ya=ya, xc=xc, y=y,
                 sprev=sprev, yb=yb, x1=x1, gt=gt, up=up, s5mats=s5mats, heads=heads, conv_w=conv_w,
                 conv_b=conv_b, ls=ls, b_hgp=b_hgp, ffn_matrices=ffn_matrices)
    return x2, saved


def _layer_backward(dx2, dx2b, p, big, s, i, after_ffn_grads=None):
    tag = "l%d_" % i
    g = {}
    dx1, dx1b, h2, act, dgt, dup, dgain = _ffn_bwd(dx2, s["x1"], s["gt"], s["up"], p["norm_ffn"].reshape(1, -1),
                                                  big["w_gate"], big["w_up"], big["w_down"], None, tag + "ffn_bwd")
    g["norm_ffn"] = dgain[0]
    g["w_down"] = _mm_tn(act, dx2b, tag + "dw_down")
    g["w_gate"] = _mm_tn(dgt, h2, tag + "dw_gate")
    g["w_up"] = _mm_tn(dup, h2, tag + "dw_up")
    g["w_out"] = _mm_tn(jnp.concatenate([s["ya"], s["yb"]], axis=1), dx1b, tag + "dw_out")
    if after_ffn_grads is not None:
        p = {**p, "s5_norm": p["s5_norm"] + after_ffn_grads(g)[0, 0]}

    dys, du_skip, gelu_b, dq_b, dgain, dd, dbg = _s5_post_bwd(
        dx1b, big["w_out"], s["ys"], s["u"], p["s5_d"].reshape(1, -1), big["s5_w_glu"],
        p["s5_b_glu"].reshape(1, -1), p["s5_norm"].reshape(1, -1), None, tag + "s5_post_bwd")
    g["s5_norm"] = dgain[0]
    g["s5_d"] = dd[0]
    g["s5_b_glu"] = dbg[0]
    g["s5_w_glu"] = _mm_tn(gelu_b, dq_b, tag + "dw_glu")
    m = s["s5mats"]
    du_perm, dar, dai, dcre_d, dcim_d, dbre_d, dbim_d = _s5_scan_bwd(
        _to_segments(dys), s["u_perm"], s["xr"], s["xi"], m["bret_bd"], m["bimt_bd"], m["cret_bd"], m["cimt_bd"],
        m["are"], m["aim"], tag + "s5_scan_bwd")
    du_scan = _from_segments(du_perm)
    g["s5_c_re"] = _block_diag_extract(dcre_d)
    g["s5_c_im"] = _block_diag_extract(dcim_d)
    dbbre = _block_diag_extract(dbre_d).transpose(1, 0, 2)
    dbbim = _block_diag_extract(dbim_d).transpose(1, 0, 2)
    dlr, dli, dls, dbre, dbim = _s5_prep_bwd(
        p["s5_lam_re"], p["s5_lam_im"], s["ls"], s["b_hgp"][0], s["b_hgp"][1],
        dar.reshape(S5_GROUPS, S5_STATE), dai.reshape(S5_GROUPS, S5_STATE), dbbre, dbbim, tag + "s5_prep_bwd")
    g["s5_lam_re"] = dlr
    g["s5_lam_im"] = dli
    g["s5_log_step"] = dls[:, 0]
    g["s5_b_re"] = dbre
    g["s5_b_im"] = dbim

    dy, dz, dgain = _ssd_post_bwd(dx1b, big["w_out"], s["y"], s["z"], p["ssd_norm"].reshape(1, -1), None,
                                  tag + "ssd_post_bwd")
    g["ssd_norm"] = dgain[0]
    hd = s["heads"]
    dxs, dbm, dcm, ddt, dbias, dalog, dd = _ssd_scan_bwd(dy, s["xc"], s["dt"], s["sprev"], hd["dt_bias"],
                                                       hd["a_log"], hd["d"], hd["expand"], hd["expand_t"],
                                                       tag + "ssd_scan_bwd")
    g["ssd_dt_bias"] = _unpad_heads(dbias)
    g["ssd_a_log"] = _unpad_heads(dalog)
    g["ssd_d"] = _unpad_heads(dd)
    dxc = jnp.concatenate([dxs, dbm, dcm], axis=1)
    dxbc, dcw, dcb = _ssd_conv_bwd(dxc, s["xbc"], s["conv_w"], s["conv_b"], tag + "ssd_conv_bwd")
    g["ssd_conv_w"] = dcw[:SSD_CONV]
    g["ssd_conv_b"] = dcb[0]

    dx0, dx0b, dproj, dgain = _inproj_bwd(dx1, s["x0"], du_skip, du_scan, dz, dxbc, ddt, p["norm_mix"].reshape(1, -1),
                                          big["w_in"], None, tag + "inproj_bwd")
    g["norm_mix"] = dgain[0]
    g["w_in"] = _mm_tn(s["h1"], dproj, tag + "dw_in")
    return dx0, dx0b, g


MIXER_BIG = ("w_in", "s5_w_glu", "w_out")
FFN_BIG = ("w_gate", "w_up", "w_down")
BIG = MIXER_BIG + FFN_BIG
COL_SHARDED = ("w_in",)
T_STORED = ("w_gate", "w_up")
LAYER_SMALL = ("norm_mix", "s5_lam_re", "s5_lam_im", "s5_log_step", "s5_b_re", "s5_b_im", "s5_c_re", "s5_c_im",
               "s5_d", "s5_b_glu", "s5_norm", "ssd_conv_w", "ssd_conv_b", "ssd_dt_bias", "ssd_a_log", "ssd_d",
               "ssd_norm", "norm_ffn")
WEIGHTS = ("norm_mix", "w_in", "s5_lam_re", "s5_lam_im", "s5_log_step", "s5_b_re", "s5_b_im", "s5_c_re", "s5_c_im",
           "s5_d", "s5_w_glu", "s5_b_glu", "s5_norm", "ssd_conv_w", "ssd_conv_b", "ssd_dt_bias", "ssd_a_log",
           "ssd_d", "ssd_norm", "w_out", "norm_ffn", "w_gate", "w_up", "w_down", "norm_final")


S5_BC = ("s5_b_re", "s5_b_im", "s5_c_re", "s5_c_im")
TINY = tuple(k for k in LAYER_SMALL if k not in S5_BC)


def _local_step(x, target, big, small, norm_final):
    saved = []
    h = x
    for i in range(DEPTH):
        p = {k: v[i] for k, v in small.items()}
        h, s = _layer_forward(h, p, big, i)
        saved.append((p, s))
    loss, dx, dxb, dgf = _final_loss(h, norm_final.reshape(1, -1), target, "final_loss")
    grads = [None] * DEPTH
    for i in reversed(range(DEPTH)):
        p, s = saved[i]
        dx, dxb, grads[i] = _layer_backward(dx, dxb, p, big, s, i)
    by_name = {k: [grads[i][k] for i in range(DEPTH)] for k in BIG + LAYER_SMALL}
    return loss[0, 0], dx, by_name, dgf[0]


def _my_place():
    return lax.axis_index("x"), lax.axis_index("y"), lax.axis_index("c")


def _all_gather8(blocks, name):
    nt = len(blocks)

    def body(*refs):
        ins = refs[:nt]
        outs = refs[nt:2 * nt]
        send_sems, recv_sems, local_sems = refs[2 * nt:]
        x, y, c = _my_place()
        me, sibling = (x, y, c), (x, y, 1 - c)
        chips = [(1 - x, y), (x, 1 - y), (1 - x, 1 - y)]

        def slot(t, place):
            px, py, pc = place
            return outs[t].at[4 * px + 2 * py + pc]

        def copy(t, k, block, to, src=None):
            return pltpu.make_async_remote_copy(
                src_ref=slot(t, block) if src is None else src, dst_ref=slot(t, block),
                send_sem=send_sems.at[t, k], recv_sem=recv_sems.at[t, k], device_id=to, device_id_type=MESH)

        mine = [pltpu.make_async_copy(ins[t], slot(t, me), local_sems.at[t]) for t in range(nt)]
        for cp in mine:
            cp.start()
        first = []
        for t in range(nt):
            first.append(copy(t, 0, me, sibling, src=ins[t]))
            first += [copy(t, 1 + j, me, (*chip, c), src=ins[t]) for j, chip in enumerate(chips)]
        for cp in first:
            cp.start()
        passed = []
        for j, chip in enumerate(chips):
            for t in range(nt):
                copy(t, 1 + j, (*chip, c), me).wait_recv()
                fwd = copy(t, 4 + j, (*chip, c), sibling)
                fwd.start()
                passed.append(fwd)
        for t in range(nt):
            copy(t, 0, sibling, me).wait_recv()
            for j, chip in enumerate(chips):
                copy(t, 4 + j, (*chip, 1 - c), me).wait_recv()
        for cp in first + passed:
            cp.wait_send()
        for cp in mine:
            cp.wait()

    return pl.pallas_call(
        body, name=name, in_specs=[ANY] * nt, out_specs=[ANY] * nt,
        out_shape=[jax.ShapeDtypeStruct((8,) + b.shape, b.dtype) for b in blocks],
        scratch_shapes=[pltpu.SemaphoreType.DMA((nt, 7)), pltpu.SemaphoreType.DMA((nt, 7)),
                        pltpu.SemaphoreType.DMA((nt,))],
    )(*blocks)


HBM = pl.BlockSpec(memory_space=pltpu.HBM)
SEM = pl.BlockSpec(memory_space=pltpu.SEMAPHORE)
DATAFLOW = pltpu.SideEffectType.DATAFLOW_SIDE_EFFECTING


def _in_hbm(a):
    return pltpu.with_memory_space_constraint(a, pltpu.HBM)


TOKEN = jax.ShapeDtypeStruct((8, LANES), F32)
VMEM_SPEC = pl.BlockSpec(memory_space=pltpu.VMEM)


def _gather_start(blocks, after, name):
    nt = len(blocks)

    def body(*refs):
        ins = refs[:nt]
        lands = refs[nt:2 * nt]
        send_sems, recv_sems = refs[2 * nt + 1:2 * nt + 3]
        refs[-1][...] = jnp.zeros(TOKEN.shape, F32)
        x, y, c = _my_place()
        me = 4 * x + 2 * y + c
        peers = [(x, y, 1 - c), (1 - x, y, c), (x, 1 - y, c), (1 - x, 1 - y, c)]
        for t in range(nt):
            for k, peer in enumerate(peers):
                pltpu.make_async_remote_copy(src_ref=ins[t], dst_ref=lands[t].at[me], send_sem=send_sems.at[4 * t + k],
                                             recv_sem=recv_sems.at[4 * t + k], device_id=peer,
                                             device_id_type=MESH).start()

    lands = [_in_hbm(lax.empty((8,) + b.shape, b.dtype)) for b in blocks]
    out = pl.pallas_call(
        body, name=name, in_specs=[HBM] * (2 * nt) + [ANY],
        out_shape=(pltpu.SemaphoreType.DMA((4 * nt,)), pltpu.SemaphoreType.DMA((4 * nt,)),
                   *[pltpu.HBM(b.shape, b.dtype) for b in blocks],
                   *[pltpu.HBM((8,) + b.shape, b.dtype) for b in blocks], TOKEN),
        out_specs=(SEM, SEM, *[HBM] * (2 * nt), VMEM_SPEC),
        input_output_aliases={i: 2 + i for i in range(2 * nt)},
        compiler_params=pltpu.CompilerParams(has_side_effects=DATAFLOW),
    )(*[_in_hbm(b) for b in blocks], *lands, after)
    return out[:2], list(out[2:2 + nt]), list(out[2 + nt:2 + 2 * nt]), out[-1]


def _gather_forward(sems, blocks, lands, after, name):
    nt = len(blocks)

    def body(*refs):
        ins = refs[:nt]
        lands_in = refs[nt:2 * nt]
        send1, recv1 = refs[2 * nt:2 * nt + 2]
        send2, recv2 = refs[2 * nt + 3:2 * nt + 5]
        x, y, c = _my_place()
        me = 4 * x + 2 * y + c
        sibling = (x, y, 1 - c)
        sources = [4 * x + 2 * y + (1 - c), 4 * (1 - x) + 2 * y + c, 4 * x + 2 * (1 - y) + c,
                   4 * (1 - x) + 2 * (1 - y) + c]
        for t in range(nt):
            for k, src in enumerate(sources):
                cp = pltpu.make_async_remote_copy(src_ref=ins[t], dst_ref=lands_in[t].at[src],
                                                  send_sem=send1.at[4 * t + k], recv_sem=recv1.at[4 * t + k],
                                                  device_id=sibling, device_id_type=MESH)
                cp.wait_send()
                cp.wait_recv()
            for k, src in enumerate(sources[1:]):
                pltpu.make_async_remote_copy(src_ref=lands_in[t].at[src], dst_ref=lands_in[t].at[src],
                                             send_sem=send2.at[3 * t + k], recv_sem=recv2.at[3 * t + k],
                                             device_id=sibling, device_id_type=MESH).start()

    out = pl.pallas_call(
        body, name=name, in_specs=[HBM] * (2 * nt) + [SEM, SEM, pl.BlockSpec(memory_space=pl.ANY)],
        out_shape=(pltpu.SemaphoreType.DMA((3 * nt,)), pltpu.SemaphoreType.DMA((3 * nt,)),
                   *[pltpu.HBM(b.shape, b.dtype) for b in blocks],
                   *[pltpu.HBM(a.shape, a.dtype) for a in lands]),
        out_specs=(SEM, SEM, *[HBM] * (2 * nt)),
        input_output_aliases={i: 2 + i for i in range(2 * nt)},
        compiler_params=pltpu.CompilerParams(has_side_effects=DATAFLOW),
    )(*blocks, *lands, *sems, after)
    return out[:2], list(out[2 + nt:])


def _gather_finish(sems, lands, after, name):
    nt = len(lands)

    def body(*refs):
        lands_in = refs[:nt]
        send2, recv2 = refs[nt:nt + 2]
        x, y, c = _my_place()
        sibling = (x, y, 1 - c)
        mine = [4 * (1 - x) + 2 * y + c, 4 * x + 2 * (1 - y) + c, 4 * (1 - x) + 2 * (1 - y) + c]
        theirs = [4 * (1 - x) + 2 * y + 1 - c, 4 * x + 2 * (1 - y) + 1 - c, 4 * (1 - x) + 2 * (1 - y) + 1 - c]
        for t in range(nt):
            for k in range(3):
                cp = pltpu.make_async_remote_copy(src_ref=lands_in[t].at[mine[k]], dst_ref=lands_in[t].at[theirs[k]],
                                                  send_sem=send2.at[3 * t + k], recv_sem=recv2.at[3 * t + k],
                                                  device_id=sibling, device_id_type=MESH)
                cp.wait_send()
                cp.wait_recv()

    out = pl.pallas_call(
        body, name=name, in_specs=[HBM] * nt + [SEM, SEM, pl.BlockSpec(memory_space=pl.ANY)],
        out_shape=tuple(pltpu.HBM(a.shape, a.dtype) for a in lands), out_specs=tuple([HBM] * nt),
        input_output_aliases={i: i for i in range(nt)},
        compiler_params=pltpu.CompilerParams(has_side_effects=DATAFLOW),
    )(*lands, *sems, after)
    return list(out)


def _other_chips():
    x, y, _ = _my_place()
    return [(1 - x, y), (x, 1 - y), (1 - x, 1 - y)]


def _scatter_start(chunks, name):
    nt = len(chunks)

    def body(*refs):
        ins = refs[:nt]
        lands = refs[nt:2 * nt]
        send_sems, recv_sems = refs[2 * nt:2 * nt + 2]
        refs[-1][...] = jnp.zeros(TOKEN.shape, F32)
        x, y, c = _my_place()
        for t in range(nt):
            for j, (px, py) in enumerate(_other_chips()):
                pltpu.make_async_remote_copy(src_ref=ins[t].at[2 * px + py], dst_ref=lands[t].at[2 * x + y],
                                             send_sem=send_sems.at[3 * t + j], recv_sem=recv_sems.at[3 * t + j],
                                             device_id=(px, py, c), device_id_type=MESH).start()

    lands = [_in_hbm(lax.empty(a.shape, a.dtype)) for a in chunks]
    out = pl.pallas_call(
        body, name=name, in_specs=[HBM] * (2 * nt),
        out_shape=(pltpu.SemaphoreType.DMA((3 * nt,)), pltpu.SemaphoreType.DMA((3 * nt,)),
                   *[pltpu.HBM(a.shape, a.dtype) for a in chunks] * 2, TOKEN),
        out_specs=(SEM, SEM, *[HBM] * (2 * nt), VMEM_SPEC),
        input_output_aliases={i: 2 + i for i in range(2 * nt)},
        compiler_params=pltpu.CompilerParams(has_side_effects=DATAFLOW),
    )(*[_in_hbm(a) for a in chunks], *lands)
    return out[:2], list(out[2:2 + nt]), list(out[2 + nt:2 + 2 * nt]), out[-1]


def _scatter_finish(sems, chunks, lands, after, name):
    nt = len(chunks)

    def body(*refs):
        ins = refs[:nt]
        lands_in = refs[nt:2 * nt]
        send_sems, recv_sems = refs[2 * nt:2 * nt + 2]
        _, _, c = _my_place()
        for t in range(nt):
            for j, (px, py) in enumerate(_other_chips()):
                cp = pltpu.make_async_remote_copy(src_ref=ins[t].at[2 * px + py], dst_ref=lands_in[t].at[2 * px + py],
                                                  send_sem=send_sems.at[3 * t + j], recv_sem=recv_sems.at[3 * t + j],
                                                  device_id=(px, py, c), device_id_type=MESH)
                cp.wait_send()
                cp.wait_recv()

    out = pl.pallas_call(
        body, name=name, in_specs=[HBM] * (2 * nt) + [SEM, SEM, ANY],
        out_shape=tuple(pltpu.HBM(a.shape, a.dtype) for a in lands), out_specs=tuple([HBM] * nt),
        input_output_aliases={nt + i: i for i in range(nt)},
        compiler_params=pltpu.CompilerParams(has_side_effects=DATAFLOW),
    )(*chunks, *lands, *sems, after)
    return list(out)


def _swap_halves(views, name):
    nt = len(views)

    def body(*refs):
        ins = refs[:nt]
        outs = refs[nt:2 * nt]
        send_sems, recv_sems = refs[2 * nt:]
        x, y, c = _my_place()
        copies = [pltpu.make_async_remote_copy(
            src_ref=ins[t].at[pl.ds(0, views[t].shape[0]), pl.ds(1 - c, 1)], dst_ref=outs[t],
            send_sem=send_sems.at[t], recv_sem=recv_sems.at[t], device_id=(x, y, 1 - c), device_id_type=MESH)
            for t in range(nt)]
        for cp in copies:
            cp.start()
        for cp in copies:
            cp.wait()

    return pl.pallas_call(
        body, name=name, in_specs=[ANY] * nt, out_specs=[ANY] * nt,
        out_shape=[jax.ShapeDtypeStruct((a.shape[0], 1) + a.shape[2:], a.dtype) for a in views],
        scratch_shapes=[pltpu.SemaphoreType.DMA((nt,)), pltpu.SemaphoreType.DMA((nt,))],
    )(*views)


def _pair_add_halves(view, recv, name):
    n, _, rows, cols = view.shape
    tile = _row_tile(rows, cols, 4)

    def body(a0_ref, a1_ref, r_ref, o_ref):
        mine = jnp.where(lax.axis_index("c") == 0, a0_ref[...], a1_ref[...])
        o_ref[...] = (mine.astype(F32) + r_ref[...].astype(F32)).astype(o_ref.dtype)

    half = lambda h: pl.BlockSpec((None, None, tile, cols), lambda p, i: (p, h, i, 0))
    return pl.pallas_call(
        body, name=name, grid=(n, rows // tile), in_specs=[half(0), half(1), half(0)],
        out_specs=pl.BlockSpec((None, tile, cols), lambda p, i: (p, i, 0)),
        out_shape=jax.ShapeDtypeStruct((n, rows, cols), view.dtype))(view, view, recv)


def _sum_chunks(lands, chunks, order, name):
    _, rows, cols = chunks.shape
    tile = _row_tile(rows, cols, 5)

    def body(order_ref, l0_ref, l1_ref, l2_ref, own_ref, o_ref):
        o_ref[...] = ((l0_ref[...].astype(F32) + l1_ref[...].astype(F32)) + l2_ref[...].astype(F32)
                      + own_ref[...].astype(F32))

    slot = lambda j: pl.BlockSpec((None, tile, cols), lambda i, order_ref: (order_ref[j], i, 0))
    grid_spec = pltpu.PrefetchScalarGridSpec(
        num_scalar_prefetch=1, grid=(rows // tile,), in_specs=[slot(0), slot(1), slot(2), slot(3)],
        out_specs=pl.BlockSpec((tile, cols), lambda i, order_ref: (i, 0)))
    return pl.pallas_call(body, name=name, grid_spec=grid_spec,
                          out_shape=jax.ShapeDtypeStruct((rows, cols), F32))(order, lands, lands, lands, chunks)


def _adamw_layer(w, g_mine, g_sibling, m, v, layer, prev, name):
    depth, rows, cols = w.shape
    half = rows // 2
    tile = _row_tile(half, cols, 10)
    tiles = half // tile

    def body(w_ref, gm_ref, gs_ref, m_ref, v_ref, *rest):
        d_ref, nm_ref, nv_ref, go_ref = rest[-4:]
        gv = jnp.where(pl.program_id(0) == lax.axis_index("c"), gm_ref[...], gs_ref[...])
        d_ref[...], nm_ref[...], nv_ref[...] = _adamw_math(w_ref[...], gv, m_ref[...], v_ref[...])
        go_ref[...] = gv

    spec = pl.BlockSpec((None, tile, cols), lambda h, i: (layer, h * tiles + i, 0))
    gspec = pl.BlockSpec((tile, cols), lambda h, i: (i, 0))
    shape = jax.ShapeDtypeStruct((depth, rows, cols), F32)
    extra = list(prev)
    aliases = {5 + j: j for j in range(4)} if len(extra) == 4 else {}
    return pl.pallas_call(
        body, name=name, grid=(2, tiles), in_specs=[spec, gspec, gspec, spec, spec] + [ANY] * len(extra),
        out_specs=[spec] * 4, out_shape=[shape] * 4, input_output_aliases=aliases)(w, g_mine, g_sibling, m, v, *extra)


def _sibling_swap_other(pairs, name):
    nt = len(pairs)

    def body(*refs):
        ins = refs[:2 * nt]
        outs = refs[2 * nt:3 * nt]
        send_sems, recv_sems = refs[3 * nt:]
        x, y, c = _my_place()

        def copy(t, src):
            return pltpu.make_async_remote_copy(src_ref=src, dst_ref=outs[t], send_sem=send_sems.at[t],
                                                recv_sem=recv_sems.at[t], device_id=(x, y, 1 - c), device_id_type=MESH)

        for t in range(nt):
            @pl.when(c == 0)
            def _():
                copy(t, ins[2 * t + 1]).start()

            @pl.when(c == 1)
            def _():
                copy(t, ins[2 * t]).start()
        for t in range(nt):
            copy(t, ins[2 * t]).wait()

    flat = [a for pair in pairs for a in pair]
    return pl.pallas_call(
        body, name=name, in_specs=[ANY] * (2 * nt), out_specs=[ANY] * nt,
        out_shape=[jax.ShapeDtypeStruct(a0.shape, a0.dtype) for a0, _ in pairs],
        scratch_shapes=[pltpu.SemaphoreType.DMA((nt,)), pltpu.SemaphoreType.DMA((nt,))],
    )(*flat)


def _sibling_swap(arrs, name):
    nt = len(arrs)

    def body(*refs):
        ins = refs[:nt]
        outs = refs[nt:2 * nt]
        send_sems, recv_sems = refs[2 * nt:]
        x, y, c = _my_place()
        copies = [pltpu.make_async_remote_copy(src_ref=ins[t], dst_ref=outs[t], send_sem=send_sems.at[t],
                                               recv_sem=recv_sems.at[t], device_id=(x, y, 1 - c), device_id_type=MESH)
                  for t in range(nt)]
        for cp in copies:
            cp.start()
        for cp in copies:
            cp.wait()

    return pl.pallas_call(
        body, name=name, in_specs=[ANY] * nt, out_specs=[ANY] * nt,
        out_shape=[jax.ShapeDtypeStruct(a.shape, a.dtype) for a in arrs],
        scratch_shapes=[pltpu.SemaphoreType.DMA((nt,)), pltpu.SemaphoreType.DMA((nt,))],
    )(*arrs)


def _chip_all_to_all(arrs, name):
    nt = len(arrs)

    def body(*refs):
        ins = refs[:nt]
        outs = refs[nt:2 * nt]
        send_sems, recv_sems, local_sems = refs[2 * nt:]
        x, y, c = _my_place()
        mine = 2 * x + y
        chips = [(1 - x, y), (x, 1 - y), (1 - x, 1 - y)]
        local = [pltpu.make_async_copy(ins[t].at[mine], outs[t].at[mine], local_sems.at[t]) for t in range(nt)]
        for cp in local:
            cp.start()
        sends = []
        for t in range(nt):
            for j, (px, py) in enumerate(chips):
                sends.append(pltpu.make_async_remote_copy(
                    src_ref=ins[t].at[2 * px + py], dst_ref=outs[t].at[mine], send_sem=send_sems.at[t, j],
                    recv_sem=recv_sems.at[t, j], device_id=(px, py, c), device_id_type=MESH))
        for cp in sends:
            cp.start()
        for t in range(nt):
            for j, (px, py) in enumerate(chips):
                pltpu.make_async_remote_copy(
                    src_ref=ins[t].at[mine], dst_ref=outs[t].at[2 * px + py], send_sem=send_sems.at[t, j],
                    recv_sem=recv_sems.at[t, j], device_id=(px, py, c), device_id_type=MESH).wait_recv()
        for cp in sends:
            cp.wait_send()
        for cp in local:
            cp.wait()

    return pl.pallas_call(
        body, name=name, in_specs=[ANY] * nt, out_specs=[ANY] * nt,
        out_shape=[jax.ShapeDtypeStruct(a.shape, a.dtype) for a in arrs],
        scratch_shapes=[pltpu.SemaphoreType.DMA((nt, 3)), pltpu.SemaphoreType.DMA((nt, 3)),
                        pltpu.SemaphoreType.DMA((nt,))],
    )(*arrs)


def _as_rows(a):
    return a.reshape(-1, a.shape[-1])


STREAM_VMEM_BYTES = 32 * 1024 * 1024
SUBLANES = 8


def _row_tile(rows, cols, n_arrays):
    lanes = -(-cols // LANES) * LANES
    for t in range(min(rows, 512), SUBLANES - 1, -1):
        if rows % t == 0 and t % SUBLANES == 0 and 2 * n_arrays * t * lanes * 4 <= STREAM_VMEM_BYTES:
            return t
    return rows


def _pair_add(a0, a1, recv, name):
    rows, cols = a0.shape
    tile = _row_tile(rows, cols, 4)

    def body(a0_ref, a1_ref, r_ref, o_ref):
        mine = jnp.where(lax.axis_index("c") == 0, a0_ref[...], a1_ref[...])
        o_ref[...] = (mine.astype(F32) + r_ref[...].astype(F32)).astype(o_ref.dtype)

    spec = pl.BlockSpec((tile, cols), lambda i: (i, 0))
    return pl.pallas_call(body, name=name, grid=(rows // tile,), in_specs=[spec] * 3, out_specs=spec,
                          out_shape=jax.ShapeDtypeStruct((rows, cols), a0.dtype))(a0, a1, recv)


def _sum_leading(a, name):
    n, rows, cols = a.shape
    tile = _row_tile(rows, cols, n + 1)

    def body(a_ref, o_ref):
        acc = a_ref[0].astype(F32)
        for k in range(1, n):
            acc = acc + a_ref[k].astype(F32)
        o_ref[...] = acc

    return pl.pallas_call(
        body, name=name, grid=(rows // tile,), in_specs=[pl.BlockSpec((n, tile, cols), lambda i: (0, i, 0))],
        out_specs=pl.BlockSpec((tile, cols), lambda i: (i, 0)),
        out_shape=jax.ShapeDtypeStruct((rows, cols), F32))(a)


def _adamw_math(w, g, m, v):
    mn = ADAM_B1 * m + (1.0 - ADAM_B1) * g
    vn = ADAM_B2 * v + (1.0 - ADAM_B2) * jnp.square(g)
    m_hat = mn / (1.0 - ADAM_B1 ** ADAM_STEP)
    v_hat = vn / (1.0 - ADAM_B2 ** ADAM_STEP)
    delta = -ADAM_LR * (m_hat / (jnp.sqrt(v_hat) + ADAM_EPS) + ADAM_WD * w)
    return delta, mn, vn


def _adamw_layers(w, g_mine, g_sibling, m, v, name):
    depth, rows, cols = w.shape
    tile = _row_tile(rows, cols, 10)

    def body(w_ref, gm_ref, gs_ref, m_ref, v_ref, d_ref, nm_ref, nv_ref, go_ref):
        gv = jnp.where(pl.program_id(0) == lax.axis_index("c"), gm_ref[...], gs_ref[...])
        d_ref[...], nm_ref[...], nv_ref[...] = _adamw_math(w_ref[...], gv, m_ref[...], v_ref[...])
        go_ref[...] = gv

    spec = pl.BlockSpec((None, tile, cols), lambda l, i: (l, i, 0))
    gspec = pl.BlockSpec((tile, cols), lambda l, i: (i, 0))
    shape = jax.ShapeDtypeStruct((depth, rows, cols), F32)
    return pl.pallas_call(body, name=name, grid=(depth, rows // tile), in_specs=[spec, gspec, gspec, spec, spec],
                          out_specs=[spec] * 4, out_shape=[shape] * 4)(w, g_mine, g_sibling, m, v)


def _adamw_rows(w, g, m, v, name):
    depth, rows, cols = w.shape
    tile = _row_tile(rows, cols, 7)

    def body(w_ref, g_ref, m_ref, v_ref, d_ref, nm_ref, nv_ref):
        d_ref[...], nm_ref[...], nv_ref[...] = _adamw_math(w_ref[...], g_ref[...], m_ref[...], v_ref[...])

    spec = pl.BlockSpec((None, tile, cols), lambda l, i: (l, i, 0))
    shape = jax.ShapeDtypeStruct((depth, rows, cols), F32)
    return pl.pallas_call(body, name=name, grid=(depth, rows // tile), in_specs=[spec] * 4, out_specs=[spec] * 3,
                          out_shape=[shape] * 3)(w, g, m, v)


def _adamw_many(ws, gs, ms, vs, name):
    nt = len(ws)

    def body(*refs):
        for t in range(nt):
            w_ref, g_ref, m_ref, v_ref = (refs[k * nt + t] for k in range(4))
            d_ref, nm_ref, nv_ref = (refs[(4 + k) * nt + t] for k in range(3))
            d_ref[...], nm_ref[...], nv_ref[...] = _adamw_math(w_ref[...], g_ref[...], m_ref[...], v_ref[...])

    shapes = [jax.ShapeDtypeStruct(a.shape, F32) for a in ws]
    out = pl.pallas_call(body, name=name, out_shape=shapes * 3)(*ws, *gs, *ms, *vs)
    return out[:nt], out[nt:2 * nt], out[2 * nt:]


TINY_ROWS_MULTIPLE = 128


def _flat_pack(arrs):
    flat = jnp.concatenate([a.reshape(-1) for a in arrs])
    pad = (-flat.shape[0]) % (TINY_ROWS_MULTIPLE * LANES)
    return jnp.pad(flat, (0, pad)).reshape(-1, LANES)


def _flat_unpack(buf, shapes):
    flat = buf.reshape(-1)
    out = []
    off = 0
    for shp in shapes:
        n = math.prod(shp)
        out.append(flat[off:off + n].reshape(shp))
        off += n
    return out


def _to_chunks(a, name):
    if name == "w_in":
        a = _unpad_in_proj(a)
    rows, cols = a.shape
    if name in COL_SHARDED:
        return a.reshape(rows, 4, cols // 4).transpose(1, 0, 2)
    return a.reshape(4, rows // 4, cols)


def _from_chunks(a, name):
    _, depth, r, cc = a.shape
    if name in COL_SHARDED:
        return a.transpose(1, 2, 0, 3).reshape(depth, r, 4 * cc)
    return a.transpose(1, 0, 2, 3).reshape(depth, 4 * r, cc)


def kernel(x, norm_mix, w_in, s5_lam_re, s5_lam_im, s5_log_step, s5_b_re, s5_b_im, s5_c_re, s5_c_im, s5_d, s5_w_glu, s5_b_glu, s5_norm, ssd_conv_w, ssd_conv_b, ssd_dt_bias, ssd_a_log, ssd_d, ssd_norm, w_out, norm_ffn, w_gate, w_up, w_down, norm_final, loss_target, m_norm_mix, m_w_in, m_s5_lam_re, m_s5_lam_im, m_s5_log_step, m_s5_b_re, m_s5_b_im, m_s5_c_re, m_s5_c_im, m_s5_d, m_s5_w_glu, m_s5_b_glu, m_s5_norm, m_ssd_conv_w, m_ssd_conv_b, m_ssd_dt_bias, m_ssd_a_log, m_ssd_d, m_ssd_norm, m_w_out, m_norm_ffn, m_w_gate, m_w_up, m_w_down, m_norm_final, v_norm_mix, v_w_in, v_s5_lam_re, v_s5_lam_im, v_s5_log_step, v_s5_b_re, v_s5_b_im, v_s5_c_re, v_s5_c_im, v_s5_d, v_s5_w_glu, v_s5_b_glu, v_s5_norm, v_ssd_conv_w, v_ssd_conv_b, v_ssd_dt_bias, v_ssd_a_log, v_ssd_d, v_ssd_norm, v_w_out, v_norm_ffn, v_w_gate, v_w_up, v_w_down, v_norm_final):
    args = dict(locals())
    w = {k: args[k] for k in WEIGHTS}
    m = {k: args["m_" + k] for k in WEIGHTS}
    v = {k: args["v_" + k] for k in WEIGHTS}
    cx, cy, cc = _my_place()
    chip = 2 * cx + cy

    me = 4 * cx + 2 * cy + cc
    others = _other_chips()
    chunk_order = jnp.stack([2 * px + py for px, py in others] + [chip]).astype(jnp.int32)
    stored = lambda k, a: jnp.swapaxes(a, 1, 2) if k in T_STORED else a

    def my_half(k, layer):
        a = stored(k, w[k])[layer]
        return lax.dynamic_slice_in_dim(a, cc * (a.shape[0] // 2), a.shape[0] // 2, 0).astype(BF16)

    def assemble(names, lands, blocks):
        full = {}
        for k, a, b in zip(names, lands, blocks):
            a = lax.dynamic_update_index_in_dim(a, b, me, 0)
            a = a.reshape(4, 2 * a.shape[1], a.shape[2])
            if k in COL_SHARDED:
                full[k] = _pad_in_proj(a.transpose(1, 0, 2).reshape(a.shape[1], 4 * a.shape[2]))
            else:
                full[k] = a.reshape(4 * a.shape[1], a.shape[2])
        return full

    conv_block = w["ssd_conv_w"].reshape(DEPTH * SSD_CONV, -1)
    first = [my_half(k, 0) for k in MIXER_BIG] + [conv_block]
    ffn0 = [my_half(k, 0) for k in FFN_BIG]
    blocks1 = [my_half(k, 1) for k in BIG]
    sems_a, kept_a, lands_a, token = _gather_start(first, x, "gather0a_start")
    sems_b, kept_b, lands_b, token = _gather_start(ffn0, token, "gather0b_start")
    sems1, kept1, lands1, token = _gather_start(blocks1, token, "gather1_start")
    sems_a, lands_a = _gather_forward(sems_a, kept_a, lands_a, token, "gather0a_forward")
    lands_a = _gather_finish(sems_a, lands_a, token, "gather0a_finish")
    big0 = assemble(MIXER_BIG, lands_a, first)
    conv_rows = lax.dynamic_update_index_in_dim(lands_a[-1], conv_block, me, 0)
    conv_full = conv_rows.reshape(4, 2, DEPTH, SSD_CONV, -1)[:, 0].transpose(1, 2, 0, 3).reshape(
        DEPTH, SSD_CONV, SSD_CONV_DIM)
    small = {k: w[k] for k in LAYER_SMALL}
    small["ssd_conv_w"] = conv_full
    p0 = {k: a[0] for k, a in small.items()}
    p1 = {k: a[1] for k, a in small.items()}

    p0["norm_mix"] = p0["norm_mix"] + token[0, 0]
    pending = {}

    def pass_on_ffn0(u):
        pending["ffn0"] = _gather_forward(sems_b, kept_b, lands_b, u, "gather0b_forward")

    def ffn0_matrices(x1):
        sems, lands = pending["ffn0"]
        lands = _gather_finish(sems, lands, x1, "gather0b_finish")
        pending["layer1"] = _gather_forward(sems1, kept1, lands1, lands[0], "gather1_forward")
        return assemble(FFN_BIG, lands, ffn0)

    h1, saved0 = _layer_forward(x[0], p0, big0, 0, pass_on_ffn0, ffn0_matrices)
    big0 = {**big0, **saved0["ffn_matrices"]}
    sems1, lands1 = pending["layer1"]
    lands1 = _gather_finish(sems1, lands1, h1, "gather1_finish")
    big1 = assemble(BIG, lands1, blocks1)
    h2, saved1 = _layer_forward(h1, p1, big1, 1)
    loss_row, dx, dxb, g_final = _final_loss(h2, w["norm_final"].reshape(1, -1), loss_target[0], "final_loss")
    loss_part, g_final = loss_row[0, 0], g_final[0]

    def halves_view(k, a):
        if k in COL_SHARDED:
            return a.reshape(1, 2, a.shape[0] // 2, a.shape[1])
        return a.reshape(4, 2, a.shape[0] // 8, a.shape[1])

    def to_chunks(k, part):
        if k in COL_SHARDED:
            a = _unpad_in_proj(part[0])
            return a.reshape(a.shape[0], 4, a.shape[1] // 4).transpose(1, 0, 2)
        return part.reshape(4, -1, part.shape[-1])

    def reduce_begin(names, views, tag):
        recv = _swap_halves(views, tag + "swap")
        parts = [_pair_add_halves(a, r, tag + "pair_" + k) for k, a, r in zip(names, views, recv)]
        chunks = [to_chunks(k, p) for k, p in zip(names, parts)]
        return _scatter_start(chunks, tag + "scatter_start")

    def reduce_end(names, handle, after, tag):
        sems, kept, lands, _ = handle
        lands = _scatter_finish(sems, kept, lands, after, tag + "scatter_finish")
        return [_sum_chunks(a, b, chunk_order, tag + "sum_" + k) for k, a, b in zip(names, lands, kept)]

    dx, dxb, g1 = _layer_backward(dx, dxb, p1, big1, saved1, 1)
    round1 = reduce_begin(BIG, [halves_view(k, g1[k]) for k in BIG], "grad1_")
    p0["norm_ffn"] = p0["norm_ffn"] + round1[3][0, 0]

    early = FFN_BIG + ("w_out",)
    late = ("w_in", "s5_w_glu")

    def send_early(g_so_far):
        pending["early"] = reduce_begin(early, [halves_view(k, g_so_far[k]) for k in early], "grad0a_")
        return pending["early"][3]

    grad_x, _, g0 = _layer_backward(dx, dxb, p0, big0, saved0, 0, send_early)
    g = {k: [g0[k], g1[k]] for k in LAYER_SMALL}
    reduced1 = dict(zip(BIG, reduce_end(BIG, round1, grad_x, "grad1_")))
    shared1 = dict(zip(BIG, _sibling_swap([reduced1[k] for k in BIG], "grad1_share")))

    bc_rows = 2 * DEPTH * S5_GROUP * S5_GROUPS
    b_all = jnp.stack([g["s5_b_re"][0], g["s5_b_im"][0], g["s5_b_re"][1], g["s5_b_im"][1]]).reshape(bc_rows, S5_STATE)
    c_all = jnp.stack([g["s5_c_re"][0], g["s5_c_im"][0], g["s5_c_re"][1], g["s5_c_im"][1]]).reshape(bc_rows, S5_STATE)
    late_names = late + ("s5_bc",)
    round0 = reduce_begin(late_names, [halves_view(k, g0[k]) for k in late] + [jnp.stack([b_all, c_all])[None]],
                          "grad0b_")

    delta, new_m, new_v, grads = {}, {}, {}, {}
    adam1 = {}
    for k in BIG:
        adam1[k] = _adamw_layer(stored(k, w[k]), reduced1[k], shared1[k], stored(k, m[k]), stored(k, v[k]), 1,
                                [round0[3]], "adamw1_" + k)
    reduced0 = dict(zip(early, reduce_end(early, pending["early"], adam1[BIG[-1]][0], "grad0a_")))
    reduced0.update(zip(late_names, reduce_end(late_names, round0, adam1[BIG[-1]][0], "grad0b_")))
    shared0 = dict(zip(BIG, _sibling_swap([reduced0[k] for k in BIG], "grad0_share")))
    for k in BIG:
        outs = _adamw_layer(stored(k, w[k]), reduced0[k], shared0[k], stored(k, m[k]), stored(k, v[k]), 0, adam1[k],
                            "adamw0_" + k)
        delta[k], new_m[k], new_v[k], grads[k] = (stored(k, a) for a in outs)
    reduced = [reduced0["s5_bc"]]

    tiny_names = TINY + ("norm_final",)
    parts = [jnp.stack(g[k]) for k in TINY] + [g_final, loss_part.reshape(1)]
    shapes = [p.shape for p in parts]
    allparts, bc_eighths = _all_gather8([_flat_pack(parts), reduced[-1]], "gather_small")
    unpacked = _flat_unpack(_sum_leading(allparts, "sum_small"), shapes)
    loss = unpacked[-1][0]
    grads.update(zip(tiny_names, unpacked[:-1]))
    width = SSD_CONV_DIM // 4
    grads["ssd_conv_w"] = lax.dynamic_slice_in_dim(grads["ssd_conv_w"], chip * width, width, axis=2)
    bc = bc_eighths.reshape(4, 2, bc_rows // 4, S5_STATE)
    b_sum = bc[:, 0].reshape(DEPTH, 2, S5_GROUP, S5_GROUPS, S5_STATE)
    c_sum = bc[:, 1].reshape(DEPTH, 2, S5_GROUPS, S5_GROUP, S5_STATE)
    grads["s5_b_re"] = b_sum[:, 0].transpose(0, 2, 3, 1)
    grads["s5_b_im"] = b_sum[:, 1].transpose(0, 2, 3, 1)
    grads["s5_c_re"] = c_sum[:, 0]
    grads["s5_c_im"] = c_sum[:, 1]

    for k in ("s5_b_re", "s5_b_im"):
        shp = w[k].shape
        rows = lambda a: a.reshape(DEPTH, -1, shp[-1])
        d, nm, nv = _adamw_rows(rows(w[k]), rows(grads[k]), rows(m[k]), rows(v[k]), "adamw_" + k)
        delta[k], new_m[k], new_v[k] = d.reshape(shp), nm.reshape(shp), nv.reshape(shp)
    names = tiny_names + ("s5_c_re", "s5_c_im")
    as2d = lambda a: a.reshape(1, -1) if a.ndim == 1 else a
    ds, nms, nvs = _adamw_many([as2d(w[k]) for k in names], [as2d(grads[k]) for k in names],
                               [as2d(m[k]) for k in names], [as2d(v[k]) for k in names], "adamw_small")
    for k, a, b, c in zip(names, ds, nms, nvs):
        delta[k], new_m[k], new_v[k] = (t.reshape(w[k].shape) for t in (a, b, c))

    return (loss, grad_x[None], *[grads[k] for k in WEIGHTS], *[delta[k] for k in WEIGHTS],
            *[new_m[k] for k in WEIGHTS], *[new_v[k] for k in WEIGHTS])
```

```python
import functools
import math

import jax
import jax.numpy as jnp
from jax import lax
from jax.experimental import pallas as pl
from jax.experimental.pallas import tpu as pltpu

F32 = jnp.float32
BF16 = jnp.bfloat16
MESH = pl.DeviceIdType.MESH
ANY = pl.BlockSpec(memory_space=pl.ANY)

D_MODEL = 1024
DEPTH = 2
S5_GROUPS = 64
S5_GROUP = 16
S5_STATE = 64
S5_COLS = S5_GROUPS * S5_STATE
S5_TILE_GROUPS = 8
S5_TILES = S5_GROUPS // S5_TILE_GROUPS
S5_TILE_IN = S5_TILE_GROUPS * S5_GROUP
S5_TILE_ST = S5_TILE_GROUPS * S5_STATE
SEGS = 8
SSD_HEADS = 16
SSD_HEAD_DIM = 64
SSD_GROUPS = 2
SSD_GROUP_HEADS = SSD_HEADS // SSD_GROUPS
SSD_STATE = 128
SSD_CONV = 4
SSD_CHUNK = 128
SSD_WIDTH = 1024
SSD_CONV_DIM = SSD_WIDTH + 2 * SSD_GROUPS * SSD_STATE
IN_PROJ = 3600
IN_MAIN = 3584
IN_PAD = IN_MAIN + 2 * 128
FFN = 2816
EPS = 1e-6
LANES = 128
ROW_TILE = 256

ADAM_LR = 0.001
ADAM_B1 = 0.9
ADAM_B2 = 0.999
ADAM_EPS = 1e-08
ADAM_WD = 0.01
ADAM_STEP = 10


def _sigmoid(x):
    return 1.0 / (1.0 + jnp.exp(-x))


def _silu(x):
    return x * _sigmoid(x)


def _dsilu(x):
    s = _sigmoid(x)
    return s * (1.0 + x * (1.0 - s))


_GELU_K = math.sqrt(2.0 / math.pi)
_GELU_C = 0.044715


def _gelu(x):
    t = jnp.tanh(_GELU_K * (x + _GELU_C * x * x * x))
    return 0.5 * x * (1.0 + t)


def _dgelu(x):
    t = jnp.tanh(_GELU_K * (x + _GELU_C * x * x * x))
    return 0.5 * (1.0 + t) + 0.5 * x * (1.0 - t * t) * _GELU_K * (1.0 + 3.0 * _GELU_C * x * x)


def _softplus(x):
    e = jnp.exp(-jnp.abs(x))
    u = 1.0 + e
    log1p = jnp.where(u == 1.0, e, jnp.log(u) * e / jnp.where(u == 1.0, 1.0, u - 1.0))
    return jnp.maximum(x, 0.0) + log1p


def _rstd(x):
    return lax.rsqrt(jnp.mean(x * x, axis=-1, keepdims=True) + EPS)


def _rms_bwd(x, r, gain, dy):
    dyg = dy * gain
    dx = r * dyg - x * (r * r * r) * jnp.mean(x * dyg, axis=-1, keepdims=True)
    dgain = jnp.sum(dy * x * r, axis=0, keepdims=True)
    return dx, dgain


def _dot(a, b):
    return jnp.dot(a, b, preferred_element_type=F32)


def _dot_nt(a, b):
    return lax.dot_general(a, b, (((1,), (1,)), ((), ())), preferred_element_type=F32)


def _dot_tn(a, b):
    return lax.dot_general(a, b, (((0,), (0,)), ((), ())), preferred_element_type=F32)


def _row_spec(tile, cols):
    return pl.BlockSpec((tile, cols), lambda i: (i, 0))


def _full_spec(shape):
    nd = len(shape)
    return pl.BlockSpec(shape, lambda *_: (0,) * nd)


def _const_spec(shape):
    nd = len(shape)
    return pl.BlockSpec(shape, lambda *_: (0,) * nd, pipeline_mode=pl.Buffered(1))


def _layer_spec(shape, layer, block=0):
    if layer is None:
        return pl.BlockSpec(tuple(shape), lambda *_: (block, 0), pipeline_mode=pl.Buffered(1))
    return pl.BlockSpec((None,) + tuple(shape), lambda *_: (layer, block, 0), pipeline_mode=pl.Buffered(1))


def _acc_rows(ref, val, first):
    @pl.when(first)
    def _():
        ref[...] = val

    @pl.when(jnp.logical_not(first))
    def _():
        ref[...] += val


def _pick_tile(n, cap):
    best = LANES
    for t in range(LANES, cap + 1, LANES):
        if n % t == 0:
            best = t
    return best


def _mm_tn(a, b, name):
    k, m = a.shape
    _, n = b.shape
    tm = _pick_tile(m, 512)
    tn = _pick_tile(n, 1536)

    def body(a_ref, b_ref, o_ref):
        o_ref[...] = _dot_tn(a_ref[...], b_ref[...]).astype(BF16)

    return pl.pallas_call(
        body, name=name, grid=(n // tn, m // tm),
        in_specs=[pl.BlockSpec((k, tm), lambda j, i: (0, i)), pl.BlockSpec((k, tn), lambda j, i: (0, j))],
        out_specs=pl.BlockSpec((tm, tn), lambda j, i: (i, j)),
        out_shape=jax.ShapeDtypeStruct((m, n), BF16),
    )(a, b)


def _rms_inproj(x, gain, w_pad, layer, name):
    L = x.shape[0]

    def body(x_ref, g_ref, w_ref, u_ref, z_ref, xbc_ref, dt_ref, h_ref):
        xv = x_ref[...]
        h = (xv * _rstd(xv) * g_ref[...]).astype(BF16)
        h_ref[...] = h
        p = _dot(h, w_ref[...])
        u_ref[...] = p[:, :1024]
        z_ref[...] = p[:, 1024:2048]
        xbc_ref[...] = p[:, 2048:IN_MAIN]
        dt_ref[...] = p[:, IN_MAIN:IN_PAD]

    return pl.pallas_call(
        body, name=name, grid=(L // ROW_TILE,),
        in_specs=[_row_spec(ROW_TILE, D_MODEL), _full_spec((1, D_MODEL)), _layer_spec((D_MODEL, IN_PAD), layer)],
        out_specs=[_row_spec(ROW_TILE, 1024), _row_spec(ROW_TILE, 1024), _row_spec(ROW_TILE, SSD_CONV_DIM),
                   _row_spec(ROW_TILE, 256), _row_spec(ROW_TILE, D_MODEL)],
        out_shape=[jax.ShapeDtypeStruct((L, 1024), F32), jax.ShapeDtypeStruct((L, 1024), F32),
                   jax.ShapeDtypeStruct((L, SSD_CONV_DIM), F32), jax.ShapeDtypeStruct((L, 256), F32),
                   jax.ShapeDtypeStruct((L, D_MODEL), BF16)],
    )(x, gain, w_pad)


def _s5_prep_math(lr, li, ls, bre, bim):
    step = jnp.exp(ls)
    mag = jnp.exp(lr * step)
    ang = li * step
    are = mag * jnp.cos(ang)
    aim = mag * jnp.sin(ang)
    den = lr * lr + li * li
    nr = are - 1.0
    ni = aim
    cre = (nr * lr + ni * li) / den
    cim = (ni * lr - nr * li) / den
    bbre = cre[None] * bre - cim[None] * bim
    bbim = cre[None] * bim + cim[None] * bre
    return are, aim, bbre, bbim


def _s5_prep(lr, li, ls, bre, bim, name):
    def body(lr_ref, li_ref, ls_ref, bre_ref, bim_ref, are_ref, aim_ref, bbre_ref, bbim_ref):
        are, aim, bbre, bbim = _s5_prep_math(lr_ref[...], li_ref[...], ls_ref[...], bre_ref[...], bim_ref[...])
        are_ref[...] = are
        aim_ref[...] = aim
        bbre_ref[...] = bbre
        bbim_ref[...] = bbim

    gp = jax.ShapeDtypeStruct((S5_GROUPS, S5_STATE), F32)
    hgp = jax.ShapeDtypeStruct((S5_GROUP, S5_GROUPS, S5_STATE), F32)
    return pl.pallas_call(body, name=name, out_shape=[gp, gp, hgp, hgp])(lr, li, ls, bre, bim)


def _s5_prep_bwd(lr, li, ls, bre, bim, dare, daim, dbbre, dbbim, name):
    def body(lr_ref, li_ref, ls_ref, bre_ref, bim_ref, dare_ref, daim_ref, dbbre_ref, dbbim_ref,
             dlr_ref, dli_ref, dls_ref, dbre_ref, dbim_ref):
        _, vjp = jax.vjp(_s5_prep_math, lr_ref[...], li_ref[...], ls_ref[...], bre_ref[...], bim_ref[...])
        dlr, dli, dls, dbre, dbim = vjp((dare_ref[...], daim_ref[...], dbbre_ref[...], dbbim_ref[...]))
        dlr_ref[...] = dlr
        dli_ref[...] = dli
        dls_ref[...] = dls
        dbre_ref[...] = dbre
        dbim_ref[...] = dbim

    gp = jax.ShapeDtypeStruct((S5_GROUPS, S5_STATE), F32)
    g1 = jax.ShapeDtypeStruct((S5_GROUPS, 1), F32)
    hgp = jax.ShapeDtypeStruct((S5_GROUP, S5_GROUPS, S5_STATE), F32)
    return pl.pallas_call(body, name=name, out_shape=[gp, gp, g1, hgp, hgp])(
        lr, li, ls, bre, bim, dare, daim, dbbre, dbbim)


def _cmul_add(ar, ai, sr, si, br, bi):
    return ar * sr - ai * si + br, ar * si + ai * sr + bi


def _shift_rows_down(v):
    rolled = pltpu.roll(v, 1, 0)
    row = lax.broadcasted_iota(jnp.int32, v.shape, 0)
    return jnp.where(row == 0, 0.0, rolled)


def _shift_rows_up(v):
    rolled = pltpu.roll(v, SEGS - 1, 0)
    row = lax.broadcasted_iota(jnp.int32, v.shape, 0)
    return jnp.where(row == SEGS - 1, 0.0, rolled)


def _segment_power(ar, ai, steps):
    n = 1
    while n < steps:
        ar, ai = ar * ar - ai * ai, 2.0 * ar * ai
        n *= 2
    assert n == steps
    return ar, ai


def _half_segment_entries(ar, ai, first, second, half_steps, shift):
    pr, pi = _segment_power(ar, ai, half_steps)
    er = jnp.zeros_like(first[0])
    ei = jnp.zeros_like(first[1])
    for _ in range(SEGS - 1):
        mr, mi = _cmul_add(pr, pi, er, ei, *first)
        nr, ni = _cmul_add(pr, pi, mr, mi, *second)
        er, ei = shift(nr), shift(ni)
    mr, mi = _cmul_add(pr, pi, er, ei, *first)
    return (er, ei), (mr, mi)


def _s5_scan(u_perm, bre_bd, bim_bd, cre_bd, cim_bd, are, aim, name):
    L = u_perm.shape[0]
    half = L // SEGS // 2

    def body(u_ref, bre_ref, bim_ref, cre_ref, cim_ref, are_ref, aim_ref, y_ref, xr_ref, xi_ref):
        u = u_ref[...].astype(BF16)
        xr_ref[...] = _dot(u, bre_ref[0])
        xi_ref[...] = _dot(u, bim_ref[0])
        ar = jnp.broadcast_to(are_ref[0], (SEGS, S5_TILE_ST))
        ai = jnp.broadcast_to(aim_ref[0], (SEGS, S5_TILE_ST))
        zero = jnp.zeros((SEGS, S5_TILE_ST), F32)
        block = lambda j: pl.ds(pl.multiple_of(j * SEGS, SEGS), SEGS)

        def finals(j, c):
            lo, hi = block(j), block(j + half)
            return (*_cmul_add(ar, ai, c[0], c[1], xr_ref[lo, :], xi_ref[lo, :]),
                    *_cmul_add(ar, ai, c[2], c[3], xr_ref[hi, :], xi_ref[hi, :]))

        f = lax.fori_loop(0, half, finals, (zero,) * 4, unroll=4)
        e_lo, e_hi = _half_segment_entries(ar, ai, f[:2], f[2:], half, _shift_rows_down)

        def scan(j, c):
            lo, hi = block(j), block(j + half)
            s_lo = _cmul_add(ar, ai, c[0], c[1], xr_ref[lo, :], xi_ref[lo, :])
            s_hi = _cmul_add(ar, ai, c[2], c[3], xr_ref[hi, :], xi_ref[hi, :])
            xr_ref[lo, :], xi_ref[lo, :] = s_lo
            xr_ref[hi, :], xi_ref[hi, :] = s_hi
            return (*s_lo, *s_hi)

        lax.fori_loop(0, half, scan, (*e_lo, *e_hi), unroll=4)
        y_ref[...] = (_dot(xr_ref[...].astype(BF16), cre_ref[0]) - _dot(xi_ref[...].astype(BF16), cim_ref[0]))

    tile3 = lambda a, b: pl.BlockSpec((1, a, b), lambda k: (k, 0, 0))
    return pl.pallas_call(
        body, name=name, grid=(S5_TILES,),
        in_specs=[pl.BlockSpec((L, S5_TILE_IN), lambda k: (0, k)),
                  tile3(S5_TILE_IN, S5_TILE_ST), tile3(S5_TILE_IN, S5_TILE_ST),
                  tile3(S5_TILE_ST, S5_TILE_IN), tile3(S5_TILE_ST, S5_TILE_IN),
                  tile3(1, S5_TILE_ST), tile3(1, S5_TILE_ST)],
        out_specs=[pl.BlockSpec((L, S5_TILE_IN), lambda k: (0, k)),
                   pl.BlockSpec((L, S5_TILE_ST), lambda k: (0, k)), pl.BlockSpec((L, S5_TILE_ST), lambda k: (0, k))],
        out_shape=[jax.ShapeDtypeStruct((L, 1024), F32), jax.ShapeDtypeStruct((L, S5_COLS), F32),
                   jax.ShapeDtypeStruct((L, S5_COLS), F32)],
    )(u_perm, bre_bd, bim_bd, cre_bd, cim_bd, are, aim)


def _s5_scan_bwd(dy_perm, u_perm, xr, xi, bret_bd, bimt_bd, cret_bd, cimt_bd, are, aim, name):
    L = u_perm.shape[0]
    steps = L // SEGS
    half = steps // 2

    def body(dy_ref, u_ref, xr_ref, xi_ref, bret_ref, bimt_ref, cret_ref, cimt_ref, are_ref, aim_ref,
             du_ref, dar_ref, dai_ref, dcre_ref, dcim_ref, dbre_ref, dbim_ref, gr_ref, gi_ref):
        dy = dy_ref[...].astype(BF16)
        u = u_ref[...].astype(BF16)
        gr_ref[...] = _dot(dy, cret_ref[0])
        gi_ref[...] = -_dot(dy, cimt_ref[0])
        ar = jnp.broadcast_to(are_ref[0], (SEGS, S5_TILE_ST))
        ai = -jnp.broadcast_to(aim_ref[0], (SEGS, S5_TILE_ST))
        zero = jnp.zeros((SEGS, S5_TILE_ST), F32)
        block = lambda j: pl.ds(pl.multiple_of(j * SEGS, SEGS), SEGS)

        def finals(k, c):
            hi, lo = block(steps - 1 - k), block(half - 1 - k)
            return (*_cmul_add(ar, ai, c[0], c[1], gr_ref[hi, :], gi_ref[hi, :]),
                    *_cmul_add(ar, ai, c[2], c[3], gr_ref[lo, :], gi_ref[lo, :]))

        f = lax.fori_loop(0, half, finals, (zero,) * 4, unroll=4)
        e_hi, e_lo = _half_segment_entries(ar, ai, f[:2], f[2:], half, _shift_rows_up)

        def scan(k, c):
            accr, acci = c[4], c[5]
            j_hi, j_lo = steps - 1 - k, half - 1 - k
            hi, lo = block(j_hi), block(j_lo)
            hr, hi_im = _cmul_add(ar, ai, c[0], c[1], gr_ref[hi, :], gi_ref[hi, :])
            lr, lo_im = _cmul_add(ar, ai, c[2], c[3], gr_ref[lo, :], gi_ref[lo, :])
            gr_ref[hi, :], gi_ref[hi, :] = hr, hi_im
            gr_ref[lo, :], gi_ref[lo, :] = lr, lo_im
            before_hi = block(j_hi - 1)
            before_lo = block(jnp.maximum(j_lo - 1, 0))
            live = (j_lo > 0).astype(F32)
            xhr, xhi = xr_ref[before_hi, :], xi_ref[before_hi, :]
            xlr, xli = xr_ref[before_lo, :] * live, xi_ref[before_lo, :] * live
            accr = accr + (hr * xhr + hi_im * xhi) + (lr * xlr + lo_im * xli)
            acci = acci + (hi_im * xhr - hr * xhi) + (lo_im * xlr - lr * xli)
            return hr, hi_im, lr, lo_im, accr, acci

        out = lax.fori_loop(0, half, scan, (*e_hi, *e_lo, zero, zero), unroll=2)
        accr, acci = out[4], out[5]
        first = pl.ds(0, SEGS)
        last = pl.ds((steps - 1) * SEGS, SEGS)
        xpr = _shift_rows_down(xr_ref[last, :])
        xpi = _shift_rows_down(xi_ref[last, :])
        g0r = gr_ref[first, :]
        g0i = gi_ref[first, :]
        accr = accr + g0r * xpr + g0i * xpi
        acci = acci + g0i * xpr - g0r * xpi
        dar_ref[0] = jnp.sum(accr, axis=0, keepdims=True)
        dai_ref[0] = jnp.sum(acci, axis=0, keepdims=True)

        grb = gr_ref[...].astype(BF16)
        gib = gi_ref[...].astype(BF16)
        du_ref[...] = _dot(grb, bret_ref[0]) + _dot(gib, bimt_ref[0])
        dbre_ref[0] = _dot_tn(u, grb)
        dbim_ref[0] = _dot_tn(u, gib)
        dcre_ref[0] = _dot_tn(dy, xr_ref[...].astype(BF16))
        dcim_ref[0] = -_dot_tn(dy, xi_ref[...].astype(BF16))

    tile3 = lambda a, b: pl.BlockSpec((1, a, b), lambda k: (k, 0, 0))
    col_in = pl.BlockSpec((L, S5_TILE_IN), lambda k: (0, k))
    col_st = pl.BlockSpec((L, S5_TILE_ST), lambda k: (0, k))
    dense = jax.ShapeDtypeStruct((S5_TILES, S5_TILE_IN, S5_TILE_ST), F32)
    vec = jax.ShapeDtypeStruct((S5_TILES, 1, S5_TILE_ST), F32)
    return pl.pallas_call(
        body, name=name, grid=(S5_TILES,),
        in_specs=[col_in, col_in, col_st, col_st,
                  tile3(S5_TILE_ST, S5_TILE_IN), tile3(S5_TILE_ST, S5_TILE_IN),
                  tile3(S5_TILE_IN, S5_TILE_ST), tile3(S5_TILE_IN, S5_TILE_ST),
                  tile3(1, S5_TILE_ST), tile3(1, S5_TILE_ST)],
        out_specs=[col_in, tile3(1, S5_TILE_ST), tile3(1, S5_TILE_ST),
                   tile3(S5_TILE_IN, S5_TILE_ST), tile3(S5_TILE_IN, S5_TILE_ST),
                   tile3(S5_TILE_IN, S5_TILE_ST), tile3(S5_TILE_IN, S5_TILE_ST)],
        out_shape=[jax.ShapeDtypeStruct((L, 1024), F32), vec, vec, dense, dense, dense, dense],
        scratch_shapes=[pltpu.VMEM((L, S5_TILE_ST), F32), pltpu.VMEM((L, S5_TILE_ST), F32)],
    )(dy_perm, u_perm, xr, xi, bret_bd, bimt_bd, cret_bd, cimt_bd, are, aim)


def _s5_post(ys, u, d_skip, w_glu, b_glu, gain, layer, name):
    L = ys.shape[0]

    def body(ys_ref, u_ref, d_ref, w_ref, b_ref, g_ref, ya_ref):
        g = _gelu(ys_ref[...] + d_ref[...] * u_ref[...])
        q = _dot(g.astype(BF16), w_ref[...]) + b_ref[...]
        oa = g * _sigmoid(q)
        ya_ref[...] = (oa * _rstd(oa) * g_ref[...]).astype(BF16)

    vec = _full_spec((1, 1024))
    return pl.pallas_call(
        body, name=name, grid=(L // ROW_TILE,),
        in_specs=[_row_spec(ROW_TILE, 1024), _row_spec(ROW_TILE, 1024), vec, _layer_spec((1024, 1024), layer), vec,
                  vec],
        out_specs=_row_spec(ROW_TILE, 1024),
        out_shape=jax.ShapeDtypeStruct((L, 1024), BF16),
    )(ys, u, d_skip, w_glu, b_glu, gain)


def _s5_post_bwd(dx, w_out, ys, u, d_skip, w_glu, b_glu, gain, layer, name):
    L = ys.shape[0]

    def body(dx_ref, wo_ref, ys_ref, u_ref, d_ref, w_ref, b_ref, gn_ref,
             dys_ref, dus_ref, g_ref, dq_ref, dgain_ref, dd_ref, db_ref):
        first = pl.program_id(0) == 0
        uv = u_ref[...]
        yt = ys_ref[...] + d_ref[...] * uv
        g = _gelu(yt)
        gb = g.astype(BF16)
        q = _dot(gb, w_ref[...]) + b_ref[...]
        s = _sigmoid(q)
        oa = g * s
        dya = _dot_nt(dx_ref[...], wo_ref[...])
        doa, dgain = _rms_bwd(oa, _rstd(oa), gn_ref[...], dya)
        dq = doa * g * s * (1.0 - s)
        dqb = dq.astype(BF16)
        dg = doa * s + _dot_nt(dqb, w_ref[...])
        dyt = dg * _dgelu(yt)
        dys_ref[...] = dyt
        dus_ref[...] = dyt * d_ref[...]
        g_ref[...] = gb
        dq_ref[...] = dqb
        _acc_rows(dgain_ref, dgain, first)
        _acc_rows(dd_ref, jnp.sum(dyt * uv, axis=0, keepdims=True), first)
        _acc_rows(db_ref, jnp.sum(dq, axis=0, keepdims=True), first)

    vec = _full_spec((1, 1024))
    row = _row_spec(ROW_TILE, 1024)
    vshape = jax.ShapeDtypeStruct((1, 1024), F32)
    return pl.pallas_call(
        body, name=name, grid=(L // ROW_TILE,),
        in_specs=[row, _layer_spec((1024, 1024), layer, 0), row, row, vec, _layer_spec((1024, 1024), layer), vec,
                  vec],
        out_specs=[row, row, row, row, vec, vec, vec],
        out_shape=[jax.ShapeDtypeStruct((L, 1024), F32), jax.ShapeDtypeStruct((L, 1024), F32),
                   jax.ShapeDtypeStruct((L, 1024), BF16), jax.ShapeDtypeStruct((L, 1024), BF16),
                   vshape, vshape, vshape],
    )(dx, w_out, ys, u, d_skip, w_glu, b_glu, gain)


CONV_TILE = 256


def _shift_time(v, d):
    if d == 0:
        return v
    rolled = pltpu.roll(v, d, 0)
    row = lax.broadcasted_iota(jnp.int32, v.shape, 0)
    return jnp.where(row < d, 0.0, rolled)


def _unshift_time(v, d):
    if d == 0:
        return v
    n = v.shape[0]
    rolled = pltpu.roll(v, n - d, 0)
    row = lax.broadcasted_iota(jnp.int32, v.shape, 0)
    return jnp.where(row >= n - d, 0.0, rolled)


def _ssd_conv(xbc, w, b, name):
    L = xbc.shape[0]

    def body(x_ref, w_ref, b_ref, o_ref):
        xv = x_ref[...]
        pre = jnp.broadcast_to(b_ref[...], xv.shape)
        for k in range(SSD_CONV):
            pre = pre + w_ref[k:k + 1, :] * _shift_time(xv, SSD_CONV - 1 - k)
        o_ref[...] = _silu(pre)

    col = pl.BlockSpec((L, CONV_TILE), lambda j: (0, j))
    return pl.pallas_call(
        body, name=name, grid=(SSD_CONV_DIM // CONV_TILE,),
        in_specs=[col, pl.BlockSpec((8, CONV_TILE), lambda j: (0, j)), pl.BlockSpec((1, CONV_TILE), lambda j: (0, j))],
        out_specs=col, out_shape=jax.ShapeDtypeStruct((L, SSD_CONV_DIM), F32),
    )(xbc, w, b)


def _ssd_conv_bwd(dxc, xbc, w, b, name):
    L = xbc.shape[0]

    def body(d_ref, x_ref, w_ref, b_ref, dx_ref, dw_ref, db_ref):
        xv = x_ref[...]
        shifted = [_shift_time(xv, SSD_CONV - 1 - k) for k in range(SSD_CONV)]
        pre = jnp.broadcast_to(b_ref[...], xv.shape)
        for k in range(SSD_CONV):
            pre = pre + w_ref[k:k + 1, :] * shifted[k]
        dpre = d_ref[...] * _dsilu(pre)
        dx = jnp.zeros_like(xv)
        rows = []
        for k in range(SSD_CONV):
            dx = dx + w_ref[k:k + 1, :] * _unshift_time(dpre, SSD_CONV - 1 - k)
            rows.append(jnp.sum(dpre * shifted[k], axis=0, keepdims=True))
        dx_ref[...] = dx
        dw_ref[...] = jnp.concatenate(rows + [jnp.zeros((8 - SSD_CONV, CONV_TILE), F32)], axis=0)
        db_ref[...] = jnp.sum(dpre, axis=0, keepdims=True)

    col = pl.BlockSpec((L, CONV_TILE), lambda j: (0, j))
    w_spec = pl.BlockSpec((8, CONV_TILE), lambda j: (0, j))
    b_spec = pl.BlockSpec((1, CONV_TILE), lambda j: (0, j))
    return pl.pallas_call(
        body, name=name, grid=(SSD_CONV_DIM // CONV_TILE,),
        in_specs=[col, col, w_spec, b_spec], out_specs=[col, w_spec, b_spec],
        out_shape=[jax.ShapeDtypeStruct((L, SSD_CONV_DIM), F32), jax.ShapeDtypeStruct((8, SSD_CONV_DIM), F32),
                   jax.ShapeDtypeStruct((1, SSD_CONV_DIM), F32)],
    )(dxc, xbc, w, b)


def _tri(lower):
    r = lax.broadcasted_iota(jnp.int32, (SSD_CHUNK, SSD_CHUNK), 0)
    c = lax.broadcasted_iota(jnp.int32, (SSD_CHUNK, SSD_CHUNK), 1)
    return (r >= c) if lower else (r <= c)


def _ssd_chunk_common(dt_ref, bias_ref, alog_ref):
    pre = dt_ref[...] + bias_ref[0]
    dtp = _softplus(pre)
    a_neg = -jnp.exp(alog_ref[0])
    dta = dtp * a_neg
    acum = _select_rows(_tri(True), dta)
    return pre, dtp, a_neg, dta, acum


GROUP_W = SSD_GROUP_HEADS * SSD_HEAD_DIM


def _head_expander():
    r = lax.broadcasted_iota(jnp.int32, (LANES, GROUP_W), 0)
    c = lax.broadcasted_iota(jnp.int32, (LANES, GROUP_W), 1)
    return (c // SSD_HEAD_DIM == r).astype(F32)


def _split_bf16(a, terms):
    parts = []
    rest = a
    for _ in range(terms):
        piece = rest.astype(BF16)
        parts.append(piece)
        rest = rest - piece.astype(F32)
    return parts


def _select_cols(a, sel, terms=3):
    lhs = jnp.concatenate(_split_bf16(a, terms), axis=1)
    rhs = jnp.concatenate([sel.astype(BF16)] * terms, axis=0)
    return _dot(lhs, rhs)


def _select_rows(sel, b, terms=3):
    lhs = jnp.concatenate([sel.astype(BF16)] * terms, axis=1)
    rhs = jnp.concatenate(_split_bf16(b, terms), axis=0)
    return _dot(lhs, rhs)


def _decay_mask(acum_all, acum_t, h, lower):
    seg = acum_all[:, h:h + 1] - acum_t[h:h + 1, :]
    return jnp.where(lower, jnp.exp(jnp.minimum(seg, 0.0)), 0.0)


def _ssd_scan(xc, dt, dt_bias, a_log, d_wide, expand, expand_t, name):
    L = xc.shape[0]
    nc = L // SSD_CHUNK

    def body(x_ref, b_ref, c_ref, dt_ref, bias_ref, alog_ref, d_ref, e_ref, et_ref, y_ref, sp_ref, s_ref, xdt_ref):
        @pl.when(pl.program_id(1) == 0)
        def _():
            s_ref[...] = jnp.zeros_like(s_ref)

        _, dtp_all, _, _, acum_all = _ssd_chunk_common(dt_ref, bias_ref, alog_ref)
        acum_t = acum_all.T
        wide = _select_cols(jnp.concatenate([acum_all, dtp_all], axis=0), e_ref[...])
        acum_e = wide[:SSD_CHUNK]
        alast_e = acum_e[SSD_CHUNK - 1:SSD_CHUNK, :]
        x = x_ref[...]
        xdt = x * wide[SSD_CHUNK:]
        xdt_ref[...] = xdt.astype(BF16)
        bm = b_ref[...].astype(BF16)
        cm = c_ref[...].astype(BF16)
        cb = _dot_nt(cm, bm)
        lower = _tri(True)
        sp = s_ref[...]
        for h in range(SSD_GROUP_HEADS):
            cols = slice(h * SSD_HEAD_DIM, (h + 1) * SSD_HEAD_DIM)
            lm = _decay_mask(acum_all, acum_t, h, lower)
            y_ref[:, cols] = _dot((cb * lm).astype(BF16), xdt_ref[:, cols])
        y_ref[...] += _dot_nt(cm, sp.astype(BF16)) * jnp.exp(acum_e) + d_ref[0] * x
        wgt = xdt * jnp.exp(alast_e - acum_e)
        ealast = jnp.exp(_select_rows(et_ref[...], acum_t)[:, SSD_CHUNK - 1:SSD_CHUNK])
        sp_ref[0, 0] = sp
        s_ref[...] = ealast * sp + _dot_tn(wgt.astype(BF16), bm)

    par = lambda n: pl.BlockSpec((1, 1, n), lambda g, c: (g, 0, 0))
    return pl.pallas_call(
        body, name=name, grid=(SSD_GROUPS, nc),
        in_specs=[pl.BlockSpec((SSD_CHUNK, GROUP_W), lambda g, c: (c, g)),
                  pl.BlockSpec((SSD_CHUNK, SSD_STATE), lambda g, c: (c, 8 + g)),
                  pl.BlockSpec((SSD_CHUNK, SSD_STATE), lambda g, c: (c, 10 + g)),
                  pl.BlockSpec((SSD_CHUNK, LANES), lambda g, c: (c, g)),
                  par(LANES), par(LANES), par(GROUP_W), _full_spec((LANES, GROUP_W)), _full_spec((GROUP_W, LANES))],
        out_specs=[pl.BlockSpec((SSD_CHUNK, GROUP_W), lambda g, c: (c, g)),
                   pl.BlockSpec((1, 1, GROUP_W, SSD_STATE), lambda g, c: (c, g, 0, 0))],
        out_shape=[jax.ShapeDtypeStruct((L, SSD_WIDTH), F32),
                   jax.ShapeDtypeStruct((nc, SSD_GROUPS, GROUP_W, SSD_STATE), F32)],
        scratch_shapes=[pltpu.VMEM((GROUP_W, SSD_STATE), F32), pltpu.VMEM((SSD_CHUNK, GROUP_W), BF16)],
    )(xc, xc, xc, dt, dt_bias, a_log, d_wide, expand, expand_t)


def _ssd_scan_bwd(dy, xc, dt, sprev, dt_bias, a_log, d_wide, expand, expand_t, name):
    L = xc.shape[0]
    nc = L // SSD_CHUNK

    def body(dy_ref, x_ref, b_ref, c_ref, dt_ref, sp_ref, bias_ref, alog_ref, d_ref, e_ref, et_ref,
             dx_ref, db_ref, dc_ref, ddt_ref, dbias_ref, dalog_ref, dd_ref, ds_ref, xdt_ref, dyb_ref):
        first = pl.program_id(1) == 0

        @pl.when(first)
        def _():
            ds_ref[...] = jnp.zeros_like(ds_ref)

        pre, dtp_all, a_neg, _, acum_all = _ssd_chunk_common(dt_ref, bias_ref, alog_ref)
        acum_t = acum_all.T
        e = e_ref[...]
        et = et_ref[...]
        wide = _select_cols(jnp.concatenate([acum_all, dtp_all], axis=0), e)
        acum_e = wide[:SSD_CHUNK]
        dtp_e = wide[SSD_CHUNK:]
        alast_e = acum_e[SSD_CHUNK - 1:SSD_CHUNK, :]
        dstate_e = jnp.exp(alast_e - acum_e)
        x = x_ref[...]
        dy = dy_ref[...]
        xdt = x * dtp_e
        xdt_ref[...] = xdt.astype(BF16)
        dyb_ref[...] = dy.astype(BF16)
        bm = b_ref[...].astype(BF16)
        cm = c_ref[...].astype(BF16)
        cb = _dot_nt(cm, bm)
        sp = sp_ref[0, 0]
        spb = sp.astype(BF16)
        dsn = ds_ref[...]
        dsb = dsn.astype(BF16)
        z = _dot_nt(cm, spb)
        dz = dy * jnp.exp(acum_e)
        dzb = dz.astype(BF16)
        dc_acc = _dot(dzb, spb)
        ealast = jnp.exp(_select_rows(et, acum_t)[:, SSD_CHUNK - 1:SSD_CHUNK])
        ds_ref[...] = _dot_tn(dzb, cm) + ealast * dsn
        dw = _dot_nt(bm, dsb)
        wgt = xdt * dstate_e
        db_acc = _dot(wgt.astype(BF16), dsb)
        lower = _tri(True)
        lane = lax.broadcasted_iota(jnp.int32, (SSD_CHUNK, LANES), 1)
        row = lax.broadcasted_iota(jnp.int32, (SSD_CHUNK, LANES), 0)
        dcb = jnp.zeros((SSD_CHUNK, SSD_CHUNK), F32)
        dacum_all = jnp.zeros((SSD_CHUNK, LANES), F32)
        dacum_cols = jnp.zeros((SSD_CHUNK, LANES), F32)
        for h in range(SSD_GROUP_HEADS):
            cols = slice(h * SSD_HEAD_DIM, (h + 1) * SSD_HEAD_DIM)
            lm = _decay_mask(acum_all, acum_t, h, lower)
            dm = _dot_nt(dyb_ref[:, cols], xdt_ref[:, cols])
            dx_ref[:, cols] = _dot_tn((cb * lm).astype(BF16), dyb_ref[:, cols])
            dm_lm = dm * lm
            dcb = dcb + dm_lm
            q = dm_lm * cb
            dacum_all = jnp.where(lane == h, jnp.sum(q, axis=1, keepdims=True), dacum_all)
            dacum_cols = jnp.where(row == h, jnp.sum(q, axis=0, keepdims=True), dacum_cols)
        dxdt = dx_ref[...] + dw * dstate_e
        sums = _select_cols(jnp.concatenate([dz * z, dw * wgt, dxdt * x, dy * x], axis=0), et, terms=2)
        dacum_off = sums[0:SSD_CHUNK]
        dds_ds = sums[SSD_CHUNK:2 * SSD_CHUNK]
        ddtp_x = sums[2 * SSD_CHUNK:3 * SSD_CHUNK]
        dd_part = sums[3 * SSD_CHUNK:4 * SSD_CHUNK]
        ds_s = jnp.sum(_select_rows(e, dsn * sp, terms=2).T, axis=0, keepdims=True)
        dalast = ds_s * jnp.exp(acum_all[SSD_CHUNK - 1:SSD_CHUNK, :]) + jnp.sum(dds_ds, axis=0, keepdims=True)
        dacum_all = dacum_all - dacum_cols.T + dacum_off - dds_ds + jnp.where(row == SSD_CHUNK - 1, dalast, 0.0)
        dx_ref[...] = d_ref[0] * dy + dxdt * dtp_e
        dcbb = dcb.astype(BF16)
        dc_ref[...] = dc_acc + _dot(dcbb, bm)
        db_ref[...] = db_acc + _dot_tn(dcbb, cm)
        ddta = _select_rows(_tri(False), dacum_all)
        ddt = (ddtp_x + ddta * a_neg) * _sigmoid(pre)
        ddt_ref[...] = ddt
        _acc_rows(dbias_ref, jnp.sum(ddt, axis=0, keepdims=True)[None], first)
        _acc_rows(dalog_ref, (jnp.sum(ddta * dtp_all, axis=0, keepdims=True) * a_neg)[None], first)
        _acc_rows(dd_ref, jnp.sum(dd_part, axis=0, keepdims=True)[None], first)

    rev = lambda c: nc - 1 - c
    par = lambda n: pl.BlockSpec((1, 1, n), lambda g, c: (g, 0, 0))
    pshape = jax.ShapeDtypeStruct((SSD_GROUPS, 1, LANES), F32)
    return pl.pallas_call(
        body, name=name, grid=(SSD_GROUPS, nc),
        in_specs=[pl.BlockSpec((SSD_CHUNK, GROUP_W), lambda g, c: (rev(c), g)),
                  pl.BlockSpec((SSD_CHUNK, GROUP_W), lambda g, c: (rev(c), g)),
                  pl.BlockSpec((SSD_CHUNK, SSD_STATE), lambda g, c: (rev(c), 8 + g)),
                  pl.BlockSpec((SSD_CHUNK, SSD_STATE), lambda g, c: (rev(c), 10 + g)),
                  pl.BlockSpec((SSD_CHUNK, LANES), lambda g, c: (rev(c), g)),
                  pl.BlockSpec((1, 1, GROUP_W, SSD_STATE), lambda g, c: (rev(c), g, 0, 0)),
                  par(LANES), par(LANES), par(GROUP_W), _full_spec((LANES, GROUP_W)), _full_spec((GROUP_W, LANES))],
        out_specs=[pl.BlockSpec((SSD_CHUNK, GROUP_W), lambda g, c: (rev(c), g)),
                   pl.BlockSpec((SSD_CHUNK, SSD_STATE), lambda g, c: (rev(c), g)),
                   pl.BlockSpec((SSD_CHUNK, SSD_STATE), lambda g, c: (rev(c), g)),
                   pl.BlockSpec((SSD_CHUNK, LANES), lambda g, c: (rev(c), g)),
                   par(LANES), par(LANES), par(LANES)],
        out_shape=[jax.ShapeDtypeStruct((L, SSD_WIDTH), F32), jax.ShapeDtypeStruct((L, 256), F32),
                   jax.ShapeDtypeStruct((L, 256), F32), jax.ShapeDtypeStruct((L, 256), F32),
                   pshape, pshape, pshape],
        scratch_shapes=[pltpu.VMEM((GROUP_W, SSD_STATE), F32), pltpu.VMEM((SSD_CHUNK, GROUP_W), BF16),
                        pltpu.VMEM((SSD_CHUNK, GROUP_W), BF16)],
    )(dy, xc, xc, xc, dt, sprev, dt_bias, a_log, d_wide, expand, expand_t)


def _ssd_post(y, z, gain, name):
    L = y.shape[0]

    def body(y_ref, z_ref, g_ref, o_ref):
        ob = y_ref[...] * _silu(z_ref[...])
        o_ref[...] = (ob * _rstd(ob) * g_ref[...]).astype(BF16)

    row = _row_spec(ROW_TILE, 1024)
    return pl.pallas_call(body, name=name, grid=(L // ROW_TILE,), in_specs=[row, row, _full_spec((1, 1024))],
                          out_specs=row, out_shape=jax.ShapeDtypeStruct((L, 1024), BF16))(y, z, gain)


def _ssd_post_bwd(dx, w_out, y, z, gain, layer, name):
    L = y.shape[0]

    def body(dx_ref, wo_ref, y_ref, z_ref, g_ref, dy_ref, dz_ref, dgain_ref):
        first = pl.program_id(0) == 0
        yv = y_ref[...]
        zv = z_ref[...]
        sz = _silu(zv)
        ob = yv * sz
        dyb = _dot_nt(dx_ref[...], wo_ref[...])
        dob, dgain = _rms_bwd(ob, _rstd(ob), g_ref[...], dyb)
        dy_ref[...] = dob * sz
        dz_ref[...] = dob * yv * _dsilu(zv)
        _acc_rows(dgain_ref, dgain, first)

    row = _row_spec(ROW_TILE, 1024)
    vec = _full_spec((1, 1024))
    return pl.pallas_call(
        body, name=name, grid=(L // ROW_TILE,),
        in_specs=[row, _layer_spec((1024, 1024), layer, 1), row, row, vec],
        out_specs=[row, row, vec],
        out_shape=[jax.ShapeDtypeStruct((L, 1024), F32), jax.ShapeDtypeStruct((L, 1024), F32),
                   jax.ShapeDtypeStruct((1, 1024), F32)],
    )(dx, w_out, y, z, gain)


def _out_proj(x, ya, yb, w_out, layer, name):
    L = x.shape[0]

    def body(x_ref, ya_ref, yb_ref, w_ref, o_ref):
        o_ref[...] = x_ref[...] + _dot(ya_ref[...], w_ref[:1024, :]) + _dot(yb_ref[...], w_ref[1024:, :])

    row = _row_spec(ROW_TILE, 1024)
    return pl.pallas_call(body, name=name, grid=(L // ROW_TILE,),
                          in_specs=[row, row, row, _layer_spec((2048, 1024), layer)],
                          out_specs=row, out_shape=jax.ShapeDtypeStruct((L, D_MODEL), F32))(x, ya, yb, w_out)


def _ffn(x, gain, w_gate, w_up, w_down, layer, name):
    L = x.shape[0]

    def body(x_ref, g_ref, wg_ref, wu_ref, wd_ref, o_ref, gt_ref, up_ref):
        xv = x_ref[...]
        h = (xv * _rstd(xv) * g_ref[...]).astype(BF16)
        gt = _dot_nt(h, wg_ref[...])
        up = _dot_nt(h, wu_ref[...])
        gt_ref[...] = gt
        up_ref[...] = up
        o_ref[...] = xv + _dot((_silu(gt) * up).astype(BF16), wd_ref[...])

    row = _row_spec(ROW_TILE, D_MODEL)
    hid = _row_spec(ROW_TILE, FFN)
    return pl.pallas_call(
        body, name=name, grid=(L // ROW_TILE,),
        in_specs=[row, _full_spec((1, D_MODEL)), _layer_spec((FFN, D_MODEL), layer),
                  _layer_spec((FFN, D_MODEL), layer), _layer_spec((FFN, D_MODEL), layer)],
        out_specs=[row, hid, hid],
        out_shape=[jax.ShapeDtypeStruct((L, D_MODEL), F32), jax.ShapeDtypeStruct((L, FFN), F32),
                   jax.ShapeDtypeStruct((L, FFN), F32)],
    )(x, gain, w_gate, w_up, w_down)


def _ffn_bwd(dx2, x1, gt, up, gain, w_gate, w_up, w_down, layer, name):
    L = x1.shape[0]

    def body(d_ref, x_ref, gt_ref, up_ref, g_ref, wg_ref, wu_ref, wd_ref,
             dx_ref, dxb_ref, h_ref, act_ref, dgt_ref, dup_ref, dgain_ref):
        first = pl.program_id(0) == 0
        dv = d_ref[...]
        xv = x_ref[...]
        r = _rstd(xv)
        h_ref[...] = (xv * r * g_ref[...]).astype(BF16)
        gtv = gt_ref[...]
        upv = up_ref[...]
        sg = _silu(gtv)
        act_ref[...] = (sg * upv).astype(BF16)
        dact = _dot_nt(dv.astype(BF16), wd_ref[...])
        dgt = (dact * upv * _dsilu(gtv)).astype(BF16)
        dup = (dact * sg).astype(BF16)
        dgt_ref[...] = dgt
        dup_ref[...] = dup
        dh = _dot(dgt, wg_ref[...]) + _dot(dup, wu_ref[...])
        dxn, dgain = _rms_bwd(xv, r, g_ref[...], dh)
        dx = dv + dxn
        dx_ref[...] = dx
        dxb_ref[...] = dx.astype(BF16)
        _acc_rows(dgain_ref, dgain, first)

    row = _row_spec(ROW_TILE, D_MODEL)
    hid = _row_spec(ROW_TILE, FFN)
    vec = _full_spec((1, D_MODEL))
    return pl.pallas_call(
        body, name=name, grid=(L // ROW_TILE,),
        in_specs=[row, row, hid, hid, vec, _layer_spec((FFN, D_MODEL), layer), _layer_spec((FFN, D_MODEL), layer),
                  _layer_spec((FFN, D_MODEL), layer)],
        out_specs=[row, row, row, hid, hid, hid, vec],
        out_shape=[jax.ShapeDtypeStruct((L, D_MODEL), F32), jax.ShapeDtypeStruct((L, D_MODEL), BF16),
                   jax.ShapeDtypeStruct((L, D_MODEL), BF16),
                   jax.ShapeDtypeStruct((L, FFN), BF16), jax.ShapeDtypeStruct((L, FFN), BF16),
                   jax.ShapeDtypeStruct((L, FFN), BF16), jax.ShapeDtypeStruct((1, D_MODEL), F32)],
    )(dx2, x1, gt, up, gain, w_gate, w_up, w_down)


def _inproj_bwd(dx1, x0, du_skip, du_scan, dz, dxbc, ddt, gain, w_pad, layer, name):
    L = x0.shape[0]

    def body(d_ref, x_ref, dus_ref, duc_ref, dz_ref, dxbc_ref, ddt_ref, g_ref, w_ref,
             dx_ref, dxb_ref, dp_ref, dgain_ref):
        first = pl.program_id(0) == 0
        xv = x_ref[...]
        dp = jnp.concatenate([dus_ref[...] + duc_ref[...], dz_ref[...], dxbc_ref[...], ddt_ref[...]],
                             axis=1).astype(BF16)
        dp_ref[...] = dp
        dh = _dot_nt(dp, w_ref[...])
        dxn, dgain = _rms_bwd(xv, _rstd(xv), g_ref[...], dh)
        dx = d_ref[...] + dxn
        dx_ref[...] = dx
        dxb_ref[...] = dx.astype(BF16)
        _acc_rows(dgain_ref, dgain, first)

    row = _row_spec(ROW_TILE, D_MODEL)
    vec = _full_spec((1, D_MODEL))
    return pl.pallas_call(
        body, name=name, grid=(L // ROW_TILE,),
        in_specs=[row, row, row, row, row, _row_spec(ROW_TILE, SSD_CONV_DIM), _row_spec(ROW_TILE, 256), vec,
                  _layer_spec((D_MODEL, IN_PAD), layer)],
        out_specs=[row, row, _row_spec(ROW_TILE, IN_PAD), vec],
        out_shape=[jax.ShapeDtypeStruct((L, D_MODEL), F32), jax.ShapeDtypeStruct((L, D_MODEL), BF16),
                   jax.ShapeDtypeStruct((L, IN_PAD), BF16), jax.ShapeDtypeStruct((1, D_MODEL), F32)],
    )(dx1, x0, du_skip, du_scan, dz, dxbc, ddt, gain, w_pad)


def _final_loss(x, gain, target, name):
    L = x.shape[0]

    def body(x_ref, g_ref, t_ref, loss_ref, dx_ref, dxb_ref, dgain_ref):
        first = pl.program_id(0) == 0
        xv = x_ref[...]
        r = _rstd(xv)
        err = xv * r * g_ref[...] - t_ref[...]
        part = 0.5 * jnp.sum(jnp.mean(err * err, axis=-1, keepdims=True), axis=0, keepdims=True)
        dx, dgain = _rms_bwd(xv, r, g_ref[...], err * (1.0 / D_MODEL))
        dx_ref[...] = dx
        dxb_ref[...] = dx.astype(BF16)
        _acc_rows(loss_ref, jnp.broadcast_to(part, (1, LANES)), first)
        _acc_rows(dgain_ref, dgain, first)

    row = _row_spec(ROW_TILE, D_MODEL)
    vec = _full_spec((1, D_MODEL))
    return pl.pallas_call(
        body, name=name, grid=(L // ROW_TILE,), in_specs=[row, vec, row],
        out_specs=[_full_spec((1, LANES)), row, row, vec],
        out_shape=[jax.ShapeDtypeStruct((1, LANES), F32), jax.ShapeDtypeStruct((L, D_MODEL), F32),
                   jax.ShapeDtypeStruct((L, D_MODEL), BF16), jax.ShapeDtypeStruct((1, D_MODEL), F32)],
    )(x, gain, target)


def _to_segments(a):
    L, n = a.shape
    return a.reshape(SEGS, L // SEGS, n).transpose(1, 0, 2).reshape(L, n)


def _from_segments(a):
    L, n = a.shape
    return a.reshape(L // SEGS, SEGS, n).transpose(1, 0, 2).reshape(L, n)


def _block_diag_in_to_state(m):
    m = m.reshape(S5_TILES, S5_TILE_GROUPS, S5_GROUP, S5_STATE)
    eye = jnp.eye(S5_TILE_GROUPS, dtype=m.dtype)
    out = m[:, :, :, None, :] * eye[None, :, None, :, None]
    return out.reshape(S5_TILES, S5_TILE_IN, S5_TILE_ST)


def _block_diag_extract(d):
    d = d.reshape(S5_TILES, S5_TILE_GROUPS, S5_GROUP, S5_TILE_GROUPS, S5_STATE)
    d = jnp.stack([d[:, a, :, a, :] for a in range(S5_TILE_GROUPS)], axis=1)
    return d.reshape(S5_GROUPS, S5_GROUP, S5_STATE)


def _pad_in_proj(w):
    z = jnp.zeros(w.shape[:-1] + (LANES - SSD_GROUP_HEADS,), w.dtype)
    return jnp.concatenate([w[..., :IN_MAIN + 8], z, w[..., IN_MAIN + 8:], z], axis=-1)


def _unpad_in_proj(w):
    return jnp.concatenate([w[..., :IN_MAIN + 8], w[..., IN_MAIN + LANES:IN_MAIN + LANES + 8]], axis=-1)


def _pad_heads(v):
    v = v.reshape(SSD_GROUPS, 1, SSD_GROUP_HEADS)
    return jnp.pad(v, ((0, 0), (0, 0), (0, LANES - SSD_GROUP_HEADS)))


def _unpad_heads(v):
    return v[:, 0, :SSD_GROUP_HEADS].reshape(SSD_HEADS)


def _layer_forward(x0, p, big, i, after_inproj=None, before_ffn=None):
    tag = "l%d_" % i
    ls = p["s5_log_step"].reshape(S5_GROUPS, 1)
    b_hgp = (p["s5_b_re"].transpose(2, 0, 1), p["s5_b_im"].transpose(2, 0, 1))
    are, aim, bbre, bbim = _s5_prep(p["s5_lam_re"], p["s5_lam_im"], ls, b_hgp[0], b_hgp[1], tag + "s5_prep")
    bre_ghp = bbre.transpose(1, 0, 2)
    bim_ghp = bbim.transpose(1, 0, 2)
    bre_bd = _block_diag_in_to_state(bre_ghp).astype(BF16)
    bim_bd = _block_diag_in_to_state(bim_ghp).astype(BF16)
    cret_bd = _block_diag_in_to_state(p["s5_c_re"]).astype(BF16)
    cimt_bd = _block_diag_in_to_state(p["s5_c_im"]).astype(BF16)
    s5mats = dict(bre_bd=bre_bd, bim_bd=bim_bd, cret_bd=cret_bd, cimt_bd=cimt_bd,
                  bret_bd=bre_bd.transpose(0, 2, 1), bimt_bd=bim_bd.transpose(0, 2, 1),
                  cre_bd=cret_bd.transpose(0, 2, 1), cim_bd=cimt_bd.transpose(0, 2, 1),
                  are=are.reshape(S5_TILES, 1, S5_TILE_ST), aim=aim.reshape(S5_TILES, 1, S5_TILE_ST))

    u, z, xbc, dt, h1 = _rms_inproj(x0, p["norm_mix"].reshape(1, -1), big["w_in"], None, tag + "rms_inproj")
    if after_inproj is not None:
        after_inproj(u)
    u_perm = _to_segments(u)
    ys_perm, xr, xi = _s5_scan(u_perm, bre_bd, bim_bd, s5mats["cre_bd"], s5mats["cim_bd"],
                               s5mats["are"], s5mats["aim"], tag + "s5_scan")
    ys = _from_segments(ys_perm)
    ya = _s5_post(ys, u, p["s5_d"].reshape(1, -1), big["s5_w_glu"], p["s5_b_glu"].reshape(1, -1),
                  p["s5_norm"].reshape(1, -1), None, tag + "s5_post")

    conv_w = jnp.pad(p["ssd_conv_w"], ((0, 8 - SSD_CONV), (0, 0)))
    conv_b = p["ssd_conv_b"].reshape(1, -1)
    xc = _ssd_conv(xbc, conv_w, conv_b, tag + "ssd_conv")
    expand = _head_expander()
    heads = dict(dt_bias=_pad_heads(p["ssd_dt_bias"]), a_log=_pad_heads(p["ssd_a_log"]),
                 d=jnp.repeat(p["ssd_d"], SSD_HEAD_DIM).reshape(SSD_GROUPS, 1, GROUP_W),
                 expand=expand, expand_t=expand.T)
    y, sprev = _ssd_scan(xc, dt, heads["dt_bias"], heads["a_log"], heads["d"], expand, heads["expand_t"],
                         tag + "ssd_scan")
    yb = _ssd_post(y, z, p["ssd_norm"].reshape(1, -1), tag + "ssd_post")

    x1 = _out_proj(x0, ya, yb, big["w_out"], None, tag + "out_proj")
    ffn_matrices = before_ffn(x1) if before_ffn is not None else {}
    big = {**big, **ffn_matrices}
    x2, gt, up = _ffn(x1, p["norm_ffn"].reshape(1, -1), big["w_gate"], big["w_up"], big["w_down"], None,
                      tag + "ffn")
    saved = dict(x0=x0, h1=h1, u=u, u_perm=u_perm, z=z, xbc=xbc, dt=dt, xr=xr, xi=xi, ys=ys, ya=ya, xc=xc, y=y,
                 sprev=sprev, yb=yb, x1=x1, gt=gt, up=up, s5mats=s5mats, heads=heads, conv_w=conv_w,
                 conv_b=conv_b, ls=ls, b_hgp=b_hgp, ffn_matrices=ffn_matrices)
    return x2, saved


def _layer_backward(dx2, dx2b, p, big, s, i, after_ffn_grads=None, after_s5_grads=None):
    tag = "l%d_" % i
    g = {}
    dx1, dx1b, h2, act, dgt, dup, dgain = _ffn_bwd(dx2, s["x1"], s["gt"], s["up"], p["norm_ffn"].reshape(1, -1),
                                                  big["w_gate"], big["w_up"], big["w_down"], None, tag + "ffn_bwd")
    g["norm_ffn"] = dgain[0]
    g["w_down"] = _mm_tn(act, dx2b, tag + "dw_down")
    g["w_gate"] = _mm_tn(dgt, h2, tag + "dw_gate")
    g["w_up"] = _mm_tn(dup, h2, tag + "dw_up")
    g["w_out"] = _mm_tn(jnp.concatenate([s["ya"], s["yb"]], axis=1), dx1b, tag + "dw_out")
    if after_ffn_grads is not None:
        p = {**p, "s5_norm": p["s5_norm"] + after_ffn_grads(g)[0, 0]}

    dys, du_skip, gelu_b, dq_b, dgain, dd, dbg = _s5_post_bwd(
        dx1b, big["w_out"], s["ys"], s["u"], p["s5_d"].reshape(1, -1), big["s5_w_glu"],
        p["s5_b_glu"].reshape(1, -1), p["s5_norm"].reshape(1, -1), None, tag + "s5_post_bwd")
    g["s5_norm"] = dgain[0]
    g["s5_d"] = dd[0]
    g["s5_b_glu"] = dbg[0]
    g["s5_w_glu"] = _mm_tn(gelu_b, dq_b, tag + "dw_glu")
    m = s["s5mats"]
    du_perm, dar, dai, dcre_d, dcim_d, dbre_d, dbim_d = _s5_scan_bwd(
        _to_segments(dys), s["u_perm"], s["xr"], s["xi"], m["bret_bd"], m["bimt_bd"], m["cret_bd"], m["cimt_bd"],
        m["are"], m["aim"], tag + "s5_scan_bwd")
    du_scan = _from_segments(du_perm)
    g["s5_c_re"] = _block_diag_extract(dcre_d)
    g["s5_c_im"] = _block_diag_extract(dcim_d)
    dbbre = _block_diag_extract(dbre_d).transpose(1, 0, 2)
    dbbim = _block_diag_extract(dbim_d).transpose(1, 0, 2)
    dlr, dli, dls, dbre, dbim = _s5_prep_bwd(
        p["s5_lam_re"], p["s5_lam_im"], s["ls"], s["b_hgp"][0], s["b_hgp"][1],
        dar.reshape(S5_GROUPS, S5_STATE), dai.reshape(S5_GROUPS, S5_STATE), dbbre, dbbim, tag + "s5_prep_bwd")
    g["s5_lam_re"] = dlr
    g["s5_lam_im"] = dli
    g["s5_log_step"] = dls[:, 0]
    g["s5_b_re"] = dbre
    g["s5_b_im"] = dbim
    if after_s5_grads is not None:
        p = {**p, "ssd_norm": p["ssd_norm"] + after_s5_grads(g)[0, 0]}

    dy, dz, dgain = _ssd_post_bwd(dx1b, big["w_out"], s["y"], s["z"], p["ssd_norm"].reshape(1, -1), None,
                                  tag + "ssd_post_bwd")
    g["ssd_norm"] = dgain[0]
    hd = s["heads"]
    dxs, dbm, dcm, ddt, dbias, dalog, dd = _ssd_scan_bwd(dy, s["xc"], s["dt"], s["sprev"], hd["dt_bias"],
                                                       hd["a_log"], hd["d"], hd["expand"], hd["expand_t"],
                                                       tag + "ssd_scan_bwd")
    g["ssd_dt_bias"] = _unpad_heads(dbias)
    g["ssd_a_log"] = _unpad_heads(dalog)
    g["ssd_d"] = _unpad_heads(dd)
    dxc = jnp.concatenate([dxs, dbm, dcm], axis=1)
    dxbc, dcw, dcb = _ssd_conv_bwd(dxc, s["xbc"], s["conv_w"], s["conv_b"], tag + "ssd_conv_bwd")
    g["ssd_conv_w"] = dcw[:SSD_CONV]
    g["ssd_conv_b"] = dcb[0]

    dx0, dx0b, dproj, dgain = _inproj_bwd(dx1, s["x0"], du_skip, du_scan, dz, dxbc, ddt, p["norm_mix"].reshape(1, -1),
                                          big["w_in"], None, tag + "inproj_bwd")
    g["norm_mix"] = dgain[0]
    g["w_in"] = _mm_tn(s["h1"], dproj, tag + "dw_in")
    return dx0, dx0b, g


MIXER_BIG = ("w_in", "s5_w_glu", "w_out")
FFN_BIG = ("w_gate", "w_up", "w_down")
BIG = MIXER_BIG + FFN_BIG
COL_SHARDED = ("w_in",)
T_STORED = ("w_gate", "w_up")
LAYER_SMALL = ("norm_mix", "s5_lam_re", "s5_lam_im", "s5_log_step", "s5_b_re", "s5_b_im", "s5_c_re", "s5_c_im",
               "s5_d", "s5_b_glu", "s5_norm", "ssd_conv_w", "ssd_conv_b", "ssd_dt_bias", "ssd_a_log", "ssd_d",
               "ssd_norm", "norm_ffn")
WEIGHTS = ("norm_mix", "w_in", "s5_lam_re", "s5_lam_im", "s5_log_step", "s5_b_re", "s5_b_im", "s5_c_re", "s5_c_im",
           "s5_d", "s5_w_glu", "s5_b_glu", "s5_norm", "ssd_conv_w", "ssd_conv_b", "ssd_dt_bias", "ssd_a_log",
           "ssd_d", "ssd_norm", "w_out", "norm_ffn", "w_gate", "w_up", "w_down", "norm_final")


S5_BC = ("s5_b_re", "s5_b_im", "s5_c_re", "s5_c_im")
TINY = tuple(k for k in LAYER_SMALL if k not in S5_BC)


def _local_step(x, target, big, small, norm_final):
    saved = []
    h = x
    for i in range(DEPTH):
        p = {k: v[i] for k, v in small.items()}
        h, s = _layer_forward(h, p, big, i)
        saved.append((p, s))
    loss, dx, dxb, dgf = _final_loss(h, norm_final.reshape(1, -1), target, "final_loss")
    grads = [None] * DEPTH
    for i in reversed(range(DEPTH)):
        p, s = saved[i]
        dx, dxb, grads[i] = _layer_backward(dx, dxb, p, big, s, i)
    by_name = {k: [grads[i][k] for i in range(DEPTH)] for k in BIG + LAYER_SMALL}
    return loss[0, 0], dx, by_name, dgf[0]


def _my_place():
    return lax.axis_index("x"), lax.axis_index("y"), lax.axis_index("c")


def _all_gather8(blocks, name):
    nt = len(blocks)

    def body(*refs):
        ins = refs[:nt]
        outs = refs[nt:2 * nt]
        send_sems, recv_sems, local_sems = refs[2 * nt:]
        x, y, c = _my_place()
        me, sibling = (x, y, c), (x, y, 1 - c)
        chips = [(1 - x, y), (x, 1 - y), (1 - x, 1 - y)]

        def slot(t, place):
            px, py, pc = place
            return outs[t].at[4 * px + 2 * py + pc]

        def copy(t, k, block, to, src=None):
            return pltpu.make_async_remote_copy(
                src_ref=slot(t, block) if src is None else src, dst_ref=slot(t, block),
                send_sem=send_sems.at[t, k], recv_sem=recv_sems.at[t, k], device_id=to, device_id_type=MESH)

        mine = [pltpu.make_async_copy(ins[t], slot(t, me), local_sems.at[t]) for t in range(nt)]
        for cp in mine:
            cp.start()
        first = []
        for t in range(nt):
            first.append(copy(t, 0, me, sibling, src=ins[t]))
            first += [copy(t, 1 + j, me, (*chip, c), src=ins[t]) for j, chip in enumerate(chips)]
        for cp in first:
            cp.start()
        passed = []
        for j, chip in enumerate(chips):
            for t in range(nt):
                copy(t, 1 + j, (*chip, c), me).wait_recv()
                fwd = copy(t, 4 + j, (*chip, c), sibling)
                fwd.start()
                passed.append(fwd)
        for t in range(nt):
            copy(t, 0, sibling, me).wait_recv()
            for j, chip in enumerate(chips):
                copy(t, 4 + j, (*chip, 1 - c), me).wait_recv()
        for cp in first + passed:
            cp.wait_send()
        for cp in mine:
            cp.wait()

    return pl.pallas_call(
        body, name=name, in_specs=[ANY] * nt, out_specs=[ANY] * nt,
        out_shape=[jax.ShapeDtypeStruct((8,) + b.shape, b.dtype) for b in blocks],
        scratch_shapes=[pltpu.SemaphoreType.DMA((nt, 7)), pltpu.SemaphoreType.DMA((nt, 7)),
                        pltpu.SemaphoreType.DMA((nt,))],
    )(*blocks)


HBM = pl.BlockSpec(memory_space=pltpu.HBM)
SEM = pl.BlockSpec(memory_space=pltpu.SEMAPHORE)
DATAFLOW = pltpu.SideEffectType.DATAFLOW_SIDE_EFFECTING


def _in_hbm(a):
    return pltpu.with_memory_space_constraint(a, pltpu.HBM)


TOKEN = jax.ShapeDtypeStruct((8, LANES), F32)
VMEM_SPEC = pl.BlockSpec(memory_space=pltpu.VMEM)


def _gather_start(blocks, after, name):
    nt = len(blocks)

    def body(*refs):
        ins = refs[:nt]
        lands = refs[nt:2 * nt]
        send_sems, recv_sems = refs[2 * nt + 1:2 * nt + 3]
        refs[-1][...] = jnp.zeros(TOKEN.shape, F32)
        x, y, c = _my_place()
        me = 4 * x + 2 * y + c
        peers = [(x, y, 1 - c), (1 - x, y, c), (x, 1 - y, c), (1 - x, 1 - y, c)]
        for t in range(nt):
            for k, peer in enumerate(peers):
                pltpu.make_async_remote_copy(src_ref=ins[t], dst_ref=lands[t].at[me], send_sem=send_sems.at[4 * t + k],
                                             recv_sem=recv_sems.at[4 * t + k], device_id=peer,
                                             device_id_type=MESH).start()

    lands = [_in_hbm(lax.empty((8,) + b.shape, b.dtype)) for b in blocks]
    out = pl.pallas_call(
        body, name=name, in_specs=[HBM] * (2 * nt) + [ANY],
        out_shape=(pltpu.SemaphoreType.DMA((4 * nt,)), pltpu.SemaphoreType.DMA((4 * nt,)),
                   *[pltpu.HBM(b.shape, b.dtype) for b in blocks],
                   *[pltpu.HBM((8,) + b.shape, b.dtype) for b in blocks], TOKEN),
        out_specs=(SEM, SEM, *[HBM] * (2 * nt), VMEM_SPEC),
        input_output_aliases={i: 2 + i for i in range(2 * nt)},
        compiler_params=pltpu.CompilerParams(has_side_effects=DATAFLOW),
    )(*[_in_hbm(b) for b in blocks], *lands, after)
    return out[:2], list(out[2:2 + nt]), list(out[2 + nt:2 + 2 * nt]), out[-1]


def _gather_forward(sems, blocks, lands, after, name):
    nt = len(blocks)

    def body(*refs):
        ins = refs[:nt]
        lands_in = refs[nt:2 * nt]
        send1, recv1 = refs[2 * nt:2 * nt + 2]
        send2, recv2 = refs[2 * nt + 3:2 * nt + 5]
        x, y, c = _my_place()
        me = 4 * x + 2 * y + c
        sibling = (x, y, 1 - c)
        sources = [4 * x + 2 * y + (1 - c), 4 * (1 - x) + 2 * y + c, 4 * x + 2 * (1 - y) + c,
                   4 * (1 - x) + 2 * (1 - y) + c]
        for t in range(nt):
            for k, src in enumerate(sources):
                cp = pltpu.make_async_remote_copy(src_ref=ins[t], dst_ref=lands_in[t].at[src],
                                                  send_sem=send1.at[4 * t + k], recv_sem=recv1.at[4 * t + k],
                                                  device_id=sibling, device_id_type=MESH)
                cp.wait_send()
                cp.wait_recv()
            for k, src in enumerate(sources[1:]):
                pltpu.make_async_remote_copy(src_ref=lands_in[t].at[src], dst_ref=lands_in[t].at[src],
                                             send_sem=send2.at[3 * t + k], recv_sem=recv2.at[3 * t + k],
                                             device_id=sibling, device_id_type=MESH).start()

    out = pl.pallas_call(
        body, name=name, in_specs=[HBM] * (2 * nt) + [SEM, SEM, pl.BlockSpec(memory_space=pl.ANY)],
        out_shape=(pltpu.SemaphoreType.DMA((3 * nt,)), pltpu.SemaphoreType.DMA((3 * nt,)),
                   *[pltpu.HBM(b.shape, b.dtype) for b in blocks],
                   *[pltpu.HBM(a.shape, a.dtype) for a in lands]),
        out_specs=(SEM, SEM, *[HBM] * (2 * nt)),
        input_output_aliases={i: 2 + i for i in range(2 * nt)},
        compiler_params=pltpu.CompilerParams(has_side_effects=DATAFLOW),
    )(*blocks, *lands, *sems, after)
    return out[:2], list(out[2 + nt:])


def _gather_finish(sems, lands, after, name):
    nt = len(lands)

    def body(*refs):
        lands_in = refs[:nt]
        send2, recv2 = refs[nt:nt + 2]
        x, y, c = _my_place()
        sibling = (x, y, 1 - c)
        mine = [4 * (1 - x) + 2 * y + c, 4 * x + 2 * (1 - y) + c, 4 * (1 - x) + 2 * (1 - y) + c]
        theirs = [4 * (1 - x) + 2 * y + 1 - c, 4 * x + 2 * (1 - y) + 1 - c, 4 * (1 - x) + 2 * (1 - y) + 1 - c]
        for t in range(nt):
            for k in range(3):
                cp = pltpu.make_async_remote_copy(src_ref=lands_in[t].at[mine[k]], dst_ref=lands_in[t].at[theirs[k]],
                                                  send_sem=send2.at[3 * t + k], recv_sem=recv2.at[3 * t + k],
                                                  device_id=sibling, device_id_type=MESH)
                cp.wait_send()
                cp.wait_recv()

    out = pl.pallas_call(
        body, name=name, in_specs=[HBM] * nt + [SEM, SEM, pl.BlockSpec(memory_space=pl.ANY)],
        out_shape=tuple(pltpu.HBM(a.shape, a.dtype) for a in lands), out_specs=tuple([HBM] * nt),
        input_output_aliases={i: i for i in range(nt)},
        compiler_params=pltpu.CompilerParams(has_side_effects=DATAFLOW),
    )(*lands, *sems, after)
    return list(out)


def _other_chips():
    x, y, _ = _my_place()
    return [(1 - x, y), (x, 1 - y), (1 - x, 1 - y)]


def _scatter_start(chunks, name):
    nt = len(chunks)

    def body(*refs):
        ins = refs[:nt]
        lands = refs[nt:2 * nt]
        send_sems, recv_sems = refs[2 * nt:2 * nt + 2]
        refs[-1][...] = jnp.zeros(TOKEN.shape, F32)
        x, y, c = _my_place()
        for t in range(nt):
            for j, (px, py) in enumerate(_other_chips()):
                pltpu.make_async_remote_copy(src_ref=ins[t].at[2 * px + py], dst_ref=lands[t].at[2 * x + y],
                                             send_sem=send_sems.at[3 * t + j], recv_sem=recv_sems.at[3 * t + j],
                                             device_id=(px, py, c), device_id_type=MESH).start()

    lands = [_in_hbm(lax.empty(a.shape, a.dtype)) for a in chunks]
    out = pl.pallas_call(
        body, name=name, in_specs=[HBM] * (2 * nt),
        out_shape=(pltpu.SemaphoreType.DMA((3 * nt,)), pltpu.SemaphoreType.DMA((3 * nt,)),
                   *[pltpu.HBM(a.shape, a.dtype) for a in chunks] * 2, TOKEN),
        out_specs=(SEM, SEM, *[HBM] * (2 * nt), VMEM_SPEC),
        input_output_aliases={i: 2 + i for i in range(2 * nt)},
        compiler_params=pltpu.CompilerParams(has_side_effects=DATAFLOW),
    )(*[_in_hbm(a) for a in chunks], *lands)
    return out[:2], list(out[2:2 + nt]), list(out[2 + nt:2 + 2 * nt]), out[-1]


def _scatter_finish(sems, chunks, lands, after, name):
    nt = len(chunks)

    def body(*refs):
        ins = refs[:nt]
        lands_in = refs[nt:2 * nt]
        send_sems, recv_sems = refs[2 * nt:2 * nt + 2]
        _, _, c = _my_place()
        for t in range(nt):
            for j, (px, py) in enumerate(_other_chips()):
                cp = pltpu.make_async_remote_copy(src_ref=ins[t].at[2 * px + py], dst_ref=lands_in[t].at[2 * px + py],
                                                  send_sem=send_sems.at[3 * t + j], recv_sem=recv_sems.at[3 * t + j],
                                                  device_id=(px, py, c), device_id_type=MESH)
                cp.wait_send()
                cp.wait_recv()

    out = pl.pallas_call(
        body, name=name, in_specs=[HBM] * (2 * nt) + [SEM, SEM, ANY],
        out_shape=tuple(pltpu.HBM(a.shape, a.dtype) for a in lands), out_specs=tuple([HBM] * nt),
        input_output_aliases={nt + i: i for i in range(nt)},
        compiler_params=pltpu.CompilerParams(has_side_effects=DATAFLOW),
    )(*chunks, *lands, *sems, after)
    return list(out)


def _swap_halves(views, name):
    nt = len(views)

    def body(*refs):
        ins = refs[:nt]
        outs = refs[nt:2 * nt]
        send_sems, recv_sems = refs[2 * nt:]
        x, y, c = _my_place()
        copies = [pltpu.make_async_remote_copy(
            src_ref=ins[t].at[pl.ds(0, views[t].shape[0]), pl.ds(1 - c, 1)], dst_ref=outs[t],
            send_sem=send_sems.at[t], recv_sem=recv_sems.at[t], device_id=(x, y, 1 - c), device_id_type=MESH)
            for t in range(nt)]
        for cp in copies:
            cp.start()
        for cp in copies:
            cp.wait()

    return pl.pallas_call(
        body, name=name, in_specs=[ANY] * nt, out_specs=[ANY] * nt,
        out_shape=[jax.ShapeDtypeStruct((a.shape[0], 1) + a.shape[2:], a.dtype) for a in views],
        scratch_shapes=[pltpu.SemaphoreType.DMA((nt,)), pltpu.SemaphoreType.DMA((nt,))],
    )(*views)


def _pair_add_halves(view, recv, name):
    n, _, rows, cols = view.shape
    tile = _row_tile(rows, cols, 4)

    def body(a0_ref, a1_ref, r_ref, o_ref):
        mine = jnp.where(lax.axis_index("c") == 0, a0_ref[...], a1_ref[...])
        o_ref[...] = (mine.astype(F32) + r_ref[...].astype(F32)).astype(o_ref.dtype)

    half = lambda h: pl.BlockSpec((None, None, tile, cols), lambda p, i: (p, h, i, 0))
    return pl.pallas_call(
        body, name=name, grid=(n, rows // tile), in_specs=[half(0), half(1), half(0)],
        out_specs=pl.BlockSpec((None, tile, cols), lambda p, i: (p, i, 0)),
        out_shape=jax.ShapeDtypeStruct((n, rows, cols), view.dtype))(view, view, recv)


def _sum_chunks(lands, chunks, order, name):
    _, rows, cols = chunks.shape
    tile = _row_tile(rows, cols, 5)

    def body(order_ref, l0_ref, l1_ref, l2_ref, own_ref, o_ref):
        o_ref[...] = ((l0_ref[...].astype(F32) + l1_ref[...].astype(F32)) + l2_ref[...].astype(F32)
                      + own_ref[...].astype(F32))

    slot = lambda j: pl.BlockSpec((None, tile, cols), lambda i, order_ref: (order_ref[j], i, 0))
    grid_spec = pltpu.PrefetchScalarGridSpec(
        num_scalar_prefetch=1, grid=(rows // tile,), in_specs=[slot(0), slot(1), slot(2), slot(3)],
        out_specs=pl.BlockSpec((tile, cols), lambda i, order_ref: (i, 0)))
    return pl.pallas_call(body, name=name, grid_spec=grid_spec,
                          out_shape=jax.ShapeDtypeStruct((rows, cols), F32))(order, lands, lands, lands, chunks)


def _adamw_layer(w, g_mine, g_sibling, m, v, layer, prev, name):
    depth, rows, cols = w.shape
    half = rows // 2
    tile = _row_tile(half, cols, 10)
    tiles = half // tile

    def body(w_ref, gm_ref, gs_ref, m_ref, v_ref, *rest):
        d_ref, nm_ref, nv_ref, go_ref = rest[-4:]
        gv = jnp.where(pl.program_id(0) == lax.axis_index("c"), gm_ref[...], gs_ref[...])
        d_ref[...], nm_ref[...], nv_ref[...] = _adamw_math(w_ref[...], gv, m_ref[...], v_ref[...])
        go_ref[...] = gv

    spec = pl.BlockSpec((None, tile, cols), lambda h, i: (layer, h * tiles + i, 0))
    gspec = pl.BlockSpec((tile, cols), lambda h, i: (i, 0))
    shape = jax.ShapeDtypeStruct((depth, rows, cols), F32)
    extra = list(prev)
    aliases = {5 + j: j for j in range(4)} if len(extra) == 4 else {}
    return pl.pallas_call(
        body, name=name, grid=(2, tiles), in_specs=[spec, gspec, gspec, spec, spec] + [ANY] * len(extra),
        out_specs=[spec] * 4, out_shape=[shape] * 4, input_output_aliases=aliases)(w, g_mine, g_sibling, m, v, *extra)


def _sibling_swap_other(pairs, name):
    nt = len(pairs)

    def body(*refs):
        ins = refs[:2 * nt]
        outs = refs[2 * nt:3 * nt]
        send_sems, recv_sems = refs[3 * nt:]
        x, y, c = _my_place()

        def copy(t, src):
            return pltpu.make_async_remote_copy(src_ref=src, dst_ref=outs[t], send_sem=send_sems.at[t],
                                                recv_sem=recv_sems.at[t], device_id=(x, y, 1 - c), device_id_type=MESH)

        for t in range(nt):
            @pl.when(c == 0)
            def _():
                copy(t, ins[2 * t + 1]).start()

            @pl.when(c == 1)
            def _():
                copy(t, ins[2 * t]).start()
        for t in range(nt):
            copy(t, ins[2 * t]).wait()

    flat = [a for pair in pairs for a in pair]
    return pl.pallas_call(
        body, name=name, in_specs=[ANY] * (2 * nt), out_specs=[ANY] * nt,
        out_shape=[jax.ShapeDtypeStruct(a0.shape, a0.dtype) for a0, _ in pairs],
        scratch_shapes=[pltpu.SemaphoreType.DMA((nt,)), pltpu.SemaphoreType.DMA((nt,))],
    )(*flat)


def _sibling_swap(arrs, name):
    nt = len(arrs)

    def body(*refs):
        ins = refs[:nt]
        outs = refs[nt:2 * nt]
        send_sems, recv_sems = refs[2 * nt:]
        x, y, c = _my_place()
        copies = [pltpu.make_async_remote_copy(src_ref=ins[t], dst_ref=outs[t], send_sem=send_sems.at[t],
                                               recv_sem=recv_sems.at[t], device_id=(x, y, 1 - c), device_id_type=MESH)
                  for t in range(nt)]
        for cp in copies:
            cp.start()
        for cp in copies:
            cp.wait()

    return pl.pallas_call(
        body, name=name, in_specs=[ANY] * nt, out_specs=[ANY] * nt,
        out_shape=[jax.ShapeDtypeStruct(a.shape, a.dtype) for a in arrs],
        scratch_shapes=[pltpu.SemaphoreType.DMA((nt,)), pltpu.SemaphoreType.DMA((nt,))],
    )(*arrs)


def _chip_all_to_all(arrs, name):
    nt = len(arrs)

    def body(*refs):
        ins = refs[:nt]
        outs = refs[nt:2 * nt]
        send_sems, recv_sems, local_sems = refs[2 * nt:]
        x, y, c = _my_place()
        mine = 2 * x + y
        chips = [(1 - x, y), (x, 1 - y), (1 - x, 1 - y)]
        local = [pltpu.make_async_copy(ins[t].at[mine], outs[t].at[mine], local_sems.at[t]) for t in range(nt)]
        for cp in local:
            cp.start()
        sends = []
        for t in range(nt):
            for j, (px, py) in enumerate(chips):
                sends.append(pltpu.make_async_remote_copy(
                    src_ref=ins[t].at[2 * px + py], dst_ref=outs[t].at[mine], send_sem=send_sems.at[t, j],
                    recv_sem=recv_sems.at[t, j], device_id=(px, py, c), device_id_type=MESH))
        for cp in sends:
            cp.start()
        for t in range(nt):
            for j, (px, py) in enumerate(chips):
                pltpu.make_async_remote_copy(
                    src_ref=ins[t].at[mine], dst_ref=outs[t].at[2 * px + py], send_sem=send_sems.at[t, j],
                    recv_sem=recv_sems.at[t, j], device_id=(px, py, c), device_id_type=MESH).wait_recv()
        for cp in sends:
            cp.wait_send()
        for cp in local:
            cp.wait()

    return pl.pallas_call(
        body, name=name, in_specs=[ANY] * nt, out_specs=[ANY] * nt,
        out_shape=[jax.ShapeDtypeStruct(a.shape, a.dtype) for a in arrs],
        scratch_shapes=[pltpu.SemaphoreType.DMA((nt, 3)), pltpu.SemaphoreType.DMA((nt, 3)),
                        pltpu.SemaphoreType.DMA((nt,))],
    )(*arrs)


def _as_rows(a):
    return a.reshape(-1, a.shape[-1])


STREAM_VMEM_BYTES = 32 * 1024 * 1024
SUBLANES = 8


def _row_tile(rows, cols, n_arrays):
    lanes = -(-cols // LANES) * LANES
    for t in range(min(rows, 512), SUBLANES - 1, -1):
        if rows % t == 0 and t % SUBLANES == 0 and 2 * n_arrays * t * lanes * 4 <= STREAM_VMEM_BYTES:
            return t
    return rows


def _pair_add(a0, a1, recv, name):
    rows, cols = a0.shape
    tile = _row_tile(rows, cols, 4)

    def body(a0_ref, a1_ref, r_ref, o_ref):
        mine = jnp.where(lax.axis_index("c") == 0, a0_ref[...], a1_ref[...])
        o_ref[...] = (mine.astype(F32) + r_ref[...].astype(F32)).astype(o_ref.dtype)

    spec = pl.BlockSpec((tile, cols), lambda i: (i, 0))
    return pl.pallas_call(body, name=name, grid=(rows // tile,), in_specs=[spec] * 3, out_specs=spec,
                          out_shape=jax.ShapeDtypeStruct((rows, cols), a0.dtype))(a0, a1, recv)


def _sum_leading(a, name):
    n, rows, cols = a.shape
    tile = _row_tile(rows, cols, n + 1)

    def body(a_ref, o_ref):
        acc = a_ref[0].astype(F32)
        for k in range(1, n):
            acc = acc + a_ref[k].astype(F32)
        o_ref[...] = acc

    return pl.pallas_call(
        body, name=name, grid=(rows // tile,), in_specs=[pl.BlockSpec((n, tile, cols), lambda i: (0, i, 0))],
        out_specs=pl.BlockSpec((tile, cols), lambda i: (i, 0)),
        out_shape=jax.ShapeDtypeStruct((rows, cols), F32))(a)


def _adamw_math(w, g, m, v):
    mn = ADAM_B1 * m + (1.0 - ADAM_B1) * g
    vn = ADAM_B2 * v + (1.0 - ADAM_B2) * jnp.square(g)
    m_hat = mn / (1.0 - ADAM_B1 ** ADAM_STEP)
    v_hat = vn / (1.0 - ADAM_B2 ** ADAM_STEP)
    delta = -ADAM_LR * (m_hat / (jnp.sqrt(v_hat) + ADAM_EPS) + ADAM_WD * w)
    return delta, mn, vn


def _adamw_layers(w, g_mine, g_sibling, m, v, name):
    depth, rows, cols = w.shape
    tile = _row_tile(rows, cols, 10)

    def body(w_ref, gm_ref, gs_ref, m_ref, v_ref, d_ref, nm_ref, nv_ref, go_ref):
        gv = jnp.where(pl.program_id(0) == lax.axis_index("c"), gm_ref[...], gs_ref[...])
        d_ref[...], nm_ref[...], nv_ref[...] = _adamw_math(w_ref[...], gv, m_ref[...], v_ref[...])
        go_ref[...] = gv

    spec = pl.BlockSpec((None, tile, cols), lambda l, i: (l, i, 0))
    gspec = pl.BlockSpec((tile, cols), lambda l, i: (i, 0))
    shape = jax.ShapeDtypeStruct((depth, rows, cols), F32)
    return pl.pallas_call(body, name=name, grid=(depth, rows // tile), in_specs=[spec, gspec, gspec, spec, spec],
                          out_specs=[spec] * 4, out_shape=[shape] * 4)(w, g_mine, g_sibling, m, v)


def _adamw_rows(w, g, m, v, name):
    depth, rows, cols = w.shape
    tile = _row_tile(rows, cols, 7)

    def body(w_ref, g_ref, m_ref, v_ref, d_ref, nm_ref, nv_ref):
        d_ref[...], nm_ref[...], nv_ref[...] = _adamw_math(w_ref[...], g_ref[...], m_ref[...], v_ref[...])

    spec = pl.BlockSpec((None, tile, cols), lambda l, i: (l, i, 0))
    shape = jax.ShapeDtypeStruct((depth, rows, cols), F32)
    return pl.pallas_call(body, name=name, grid=(depth, rows // tile), in_specs=[spec] * 4, out_specs=[spec] * 3,
                          out_shape=[shape] * 3)(w, g, m, v)


def _adamw_many(ws, gs, ms, vs, name):
    nt = len(ws)

    def body(*refs):
        for t in range(nt):
            w_ref, g_ref, m_ref, v_ref = (refs[k * nt + t] for k in range(4))
            d_ref, nm_ref, nv_ref = (refs[(4 + k) * nt + t] for k in range(3))
            d_ref[...], nm_ref[...], nv_ref[...] = _adamw_math(w_ref[...], g_ref[...], m_ref[...], v_ref[...])

    shapes = [jax.ShapeDtypeStruct(a.shape, F32) for a in ws]
    out = pl.pallas_call(body, name=name, out_shape=shapes * 3)(*ws, *gs, *ms, *vs)
    return out[:nt], out[nt:2 * nt], out[2 * nt:]


TINY_ROWS_MULTIPLE = 128


def _flat_pack(arrs):
    flat = jnp.concatenate([a.reshape(-1) for a in arrs])
    pad = (-flat.shape[0]) % (TINY_ROWS_MULTIPLE * LANES)
    return jnp.pad(flat, (0, pad)).reshape(-1, LANES)


def _flat_unpack(buf, shapes):
    flat = buf.reshape(-1)
    out = []
    off = 0
    for shp in shapes:
        n = math.prod(shp)
        out.append(flat[off:off + n].reshape(shp))
        off += n
    return out


def _to_chunks(a, name):
    if name == "w_in":
        a = _unpad_in_proj(a)
    rows, cols = a.shape
    if name in COL_SHARDED:
        return a.reshape(rows, 4, cols // 4).transpose(1, 0, 2)
    return a.reshape(4, rows // 4, cols)


def _from_chunks(a, name):
    _, depth, r, cc = a.shape
    if name in COL_SHARDED:
        return a.transpose(1, 2, 0, 3).reshape(depth, r, 4 * cc)
    return a.transpose(1, 0, 2, 3).reshape(depth, 4 * r, cc)


def kernel(x, norm_mix, w_in, s5_lam_re, s5_lam_im, s5_log_step, s5_b_re, s5_b_im, s5_c_re, s5_c_im, s5_d, s5_w_glu, s5_b_glu, s5_norm, ssd_conv_w, ssd_conv_b, ssd_dt_bias, ssd_a_log, ssd_d, ssd_norm, w_out, norm_ffn, w_gate, w_up, w_down, norm_final, loss_target, m_norm_mix, m_w_in, m_s5_lam_re, m_s5_lam_im, m_s5_log_step, m_s5_b_re, m_s5_b_im, m_s5_c_re, m_s5_c_im, m_s5_d, m_s5_w_glu, m_s5_b_glu, m_s5_norm, m_ssd_conv_w, m_ssd_conv_b, m_ssd_dt_bias, m_ssd_a_log, m_ssd_d, m_ssd_norm, m_w_out, m_norm_ffn, m_w_gate, m_w_up, m_w_down, m_norm_final, v_norm_mix, v_w_in, v_s5_lam_re, v_s5_lam_im, v_s5_log_step, v_s5_b_re, v_s5_b_im, v_s5_c_re, v_s5_c_im, v_s5_d, v_s5_w_glu, v_s5_b_glu, v_s5_norm, v_ssd_conv_w, v_ssd_conv_b, v_ssd_dt_bias, v_ssd_a_log, v_ssd_d, v_ssd_norm, v_w_out, v_norm_ffn, v_w_gate, v_w_up, v_w_down, v_norm_final):
    args = dict(locals())
    w = {k: args[k] for k in WEIGHTS}
    m = {k: args["m_" + k] for k in WEIGHTS}
    v = {k: args["v_" + k] for k in WEIGHTS}
    cx, cy, cc = _my_place()
    chip = 2 * cx + cy

    me = 4 * cx + 2 * cy + cc
    others = _other_chips()
    chunk_order = jnp.stack([2 * px + py for px, py in others] + [chip]).astype(jnp.int32)
    stored = lambda k, a: jnp.swapaxes(a, 1, 2) if k in T_STORED else a

    def my_half(k, layer):
        a = stored(k, w[k])[layer]
        return lax.dynamic_slice_in_dim(a, cc * (a.shape[0] // 2), a.shape[0] // 2, 0).astype(BF16)

    def assemble(names, lands, blocks):
        full = {}
        for k, a, b in zip(names, lands, blocks):
            a = lax.dynamic_update_index_in_dim(a, b, me, 0)
            a = a.reshape(4, 2 * a.shape[1], a.shape[2])
            if k in COL_SHARDED:
                full[k] = _pad_in_proj(a.transpose(1, 0, 2).reshape(a.shape[1], 4 * a.shape[2]))
            else:
                full[k] = a.reshape(4 * a.shape[1], a.shape[2])
        return full

    conv_block = w["ssd_conv_w"].reshape(DEPTH * SSD_CONV, -1)
    first = [my_half(k, 0) for k in MIXER_BIG] + [conv_block]
    ffn0 = [my_half(k, 0) for k in FFN_BIG]
    blocks1 = [my_half(k, 1) for k in BIG]
    sems_a, kept_a, lands_a, token = _gather_start(first, x, "gather0a_start")
    sems_b, kept_b, lands_b, token = _gather_start(ffn0, token, "gather0b_start")
    sems1, kept1, lands1, token = _gather_start(blocks1, token, "gather1_start")
    sems_a, lands_a = _gather_forward(sems_a, kept_a, lands_a, token, "gather0a_forward")
    lands_a = _gather_finish(sems_a, lands_a, token, "gather0a_finish")
    big0 = assemble(MIXER_BIG, lands_a, first)
    conv_rows = lax.dynamic_update_index_in_dim(lands_a[-1], conv_block, me, 0)
    conv_full = conv_rows.reshape(4, 2, DEPTH, SSD_CONV, -1)[:, 0].transpose(1, 2, 0, 3).reshape(
        DEPTH, SSD_CONV, SSD_CONV_DIM)
    small = {k: w[k] for k in LAYER_SMALL}
    small["ssd_conv_w"] = conv_full
    p0 = {k: a[0] for k, a in small.items()}
    p1 = {k: a[1] for k, a in small.items()}

    p0["norm_mix"] = p0["norm_mix"] + token[0, 0]
    pending = {}

    def pass_on_ffn0(u):
        pending["ffn0"] = _gather_forward(sems_b, kept_b, lands_b, u, "gather0b_forward")

    def ffn0_matrices(x1):
        sems, lands = pending["ffn0"]
        lands = _gather_finish(sems, lands, x1, "gather0b_finish")
        pending["layer1"] = _gather_forward(sems1, kept1, lands1, lands[0], "gather1_forward")
        return assemble(FFN_BIG, lands, ffn0)

    h1, saved0 = _layer_forward(x[0], p0, big0, 0, pass_on_ffn0, ffn0_matrices)
    big0 = {**big0, **saved0["ffn_matrices"]}
    sems1, lands1 = pending["layer1"]
    lands1 = _gather_finish(sems1, lands1, h1, "gather1_finish")
    big1 = assemble(BIG, lands1, blocks1)
    h2, saved1 = _layer_forward(h1, p1, big1, 1)
    loss_row, dx, dxb, g_final = _final_loss(h2, w["norm_final"].reshape(1, -1), loss_target[0], "final_loss")
    loss_part, g_final = loss_row[0, 0], g_final[0]

    def halves_view(k, a):
        if k in COL_SHARDED:
            return a.reshape(1, 2, a.shape[0] // 2, a.shape[1])
        return a.reshape(4, 2, a.shape[0] // 8, a.shape[1])

    def to_chunks(k, part):
        if k in COL_SHARDED:
            a = _unpad_in_proj(part[0])
            return a.reshape(a.shape[0], 4, a.shape[1] // 4).transpose(1, 0, 2)
        return part.reshape(4, -1, part.shape[-1])

    def reduce_begin(names, views, tag):
        recv = _swap_halves(views, tag + "swap")
        parts = [_pair_add_halves(a, r, tag + "pair_" + k) for k, a, r in zip(names, views, recv)]
        chunks = [to_chunks(k, p) for k, p in zip(names, parts)]
        return _scatter_start(chunks, tag + "scatter_start")

    def reduce_end(names, handle, after, tag):
        sems, kept, lands, _ = handle
        lands = _scatter_finish(sems, kept, lands, after, tag + "scatter_finish")
        return [_sum_chunks(a, b, chunk_order, tag + "sum_" + k) for k, a, b in zip(names, lands, kept)]

    dx, dxb, g1 = _layer_backward(dx, dxb, p1, big1, saved1, 1)
    round1 = reduce_begin(BIG, [halves_view(k, g1[k]) for k in BIG], "grad1_")
    p0["norm_ffn"] = p0["norm_ffn"] + round1[3][0, 0]

    early = FFN_BIG + ("w_out",)
    middle = ("s5_w_glu", "s5_bc")
    bc_rows = 2 * DEPTH * S5_GROUP * S5_GROUPS

    def send_early(g_so_far):
        pending["early"] = reduce_begin(early, [halves_view(k, g_so_far[k]) for k in early], "grad0a_")
        return pending["early"][3]

    def send_middle(g_so_far):
        rows = lambda names: jnp.stack([a for layer in (g_so_far, g1) for a in (layer[names[0]], layer[names[1]])]
                                       ).reshape(bc_rows, S5_STATE)
        bc = jnp.stack([rows(("s5_b_re", "s5_b_im")), rows(("s5_c_re", "s5_c_im"))])[None]
        pending["middle"] = reduce_begin(middle, [halves_view("s5_w_glu", g_so_far["s5_w_glu"]), bc], "grad0b_")
        return pending["middle"][3]

    grad_x, _, g0 = _layer_backward(dx, dxb, p0, big0, saved0, 0, send_early, send_middle)
    g = {k: [g0[k], g1[k]] for k in LAYER_SMALL}
    reduced1 = dict(zip(BIG, reduce_end(BIG, round1, grad_x, "grad1_")))
    shared1 = dict(zip(BIG, _sibling_swap([reduced1[k] for k in BIG], "grad1_share")))
    round0 = reduce_begin(("w_in",), [halves_view("w_in", g0["w_in"])], "grad0c_")

    delta, new_m, new_v, grads = {}, {}, {}, {}
    adam1 = {}
    for k in BIG:
        adam1[k] = _adamw_layer(stored(k, w[k]), reduced1[k], shared1[k], stored(k, m[k]), stored(k, v[k]), 1,
                                [round0[3]], "adamw1_" + k)
    follow = adam1[BIG[-1]][0]
    reduced0 = dict(zip(early, reduce_end(early, pending["early"], follow, "grad0a_")))
    reduced0.update(zip(middle, reduce_end(middle, pending["middle"], follow, "grad0b_")))
    reduced0.update(zip(("w_in",), reduce_end(("w_in",), round0, follow, "grad0c_")))
    shared0 = dict(zip(BIG, _sibling_swap([reduced0[k] for k in BIG], "grad0_share")))
    for k in BIG:
        outs = _adamw_layer(stored(k, w[k]), reduced0[k], shared0[k], stored(k, m[k]), stored(k, v[k]), 0, adam1[k],
                            "adamw0_" + k)
        delta[k], new_m[k], new_v[k], grads[k] = (stored(k, a) for a in outs)
    reduced = [reduced0["s5_bc"]]

    tiny_names = TINY + ("norm_final",)
    parts = [jnp.stack(g[k]) for k in TINY] + [g_final, loss_part.reshape(1)]
    shapes = [p.shape for p in parts]
    allparts, bc_eighths = _all_gather8([_flat_pack(parts), reduced[-1]], "gather_small")
    unpacked = _flat_unpack(_sum_leading(allparts, "sum_small"), shapes)
    loss = unpacked[-1][0]
    grads.update(zip(tiny_names, unpacked[:-1]))
    width = SSD_CONV_DIM // 4
    grads["ssd_conv_w"] = lax.dynamic_slice_in_dim(grads["ssd_conv_w"], chip * width, width, axis=2)
    bc = bc_eighths.reshape(4, 2, bc_rows // 4, S5_STATE)
    b_sum = bc[:, 0].reshape(DEPTH, 2, S5_GROUP, S5_GROUPS, S5_STATE)
    c_sum = bc[:, 1].reshape(DEPTH, 2, S5_GROUPS, S5_GROUP, S5_STATE)
    grads["s5_b_re"] = b_sum[:, 0].transpose(0, 2, 3, 1)
    grads["s5_b_im"] = b_sum[:, 1].transpose(0, 2, 3, 1)
    grads["s5_c_re"] = c_sum[:, 0]
    grads["s5_c_im"] = c_sum[:, 1]

    for k in ("s5_b_re", "s5_b_im"):
        shp = w[k].shape
        rows = lambda a: a.reshape(DEPTH, -1, shp[-1])
        d, nm, nv = _adamw_rows(rows(w[k]), rows(grads[k]), rows(m[k]), rows(v[k]), "adamw_" + k)
        delta[k], new_m[k], new_v[k] = d.reshape(shp), nm.reshape(shp), nv.reshape(shp)
    names = tiny_names + ("s5_c_re", "s5_c_im")
    as2d = lambda a: a.reshape(1, -1) if a.ndim == 1 else a
    ds, nms, nvs = _adamw_many([as2d(w[k]) for k in names], [as2d(grads[k]) for k in names],
                               [as2d(m[k]) for k in names], [as2d(v[k]) for k in names], "adamw_small")
    for k, a, b, c in zip(names, ds, nms, nvs):
        delta[k], new_m[k], new_v[k] = (t.reshape(w[k].shape) for t in (a, b, c))

    return (loss, grad_x[None], *[grads[k] for k in WEIGHTS], *[delta[k] for k in WEIGHTS],
            *[new_m[k] for k in WEIGHTS], *[new_v[k] for k in WEIGHTS])
```

```python
import functools
import math

import jax
import jax.numpy as jnp
from jax import lax
from jax.experimental import pallas as pl
from jax.experimental.pallas import tpu as pltpu

F32 = jnp.float32
BF16 = jnp.bfloat16
MESH = pl.DeviceIdType.MESH
ANY = pl.BlockSpec(memory_space=pl.ANY)

D_MODEL = 1024
DEPTH = 2
S5_GROUPS = 64
S5_GROUP = 16
S5_STATE = 64
S5_COLS = S5_GROUPS * S5_STATE
S5_TILE_GROUPS = 8
S5_TILES = S5_GROUPS // S5_TILE_GROUPS
S5_TILE_IN = S5_TILE_GROUPS * S5_GROUP
S5_TILE_ST = S5_TILE_GROUPS * S5_STATE
SEGS = 8
SSD_HEADS = 16
SSD_HEAD_DIM = 64
SSD_GROUPS = 2
SSD_GROUP_HEADS = SSD_HEADS // SSD_GROUPS
SSD_STATE = 128
SSD_CONV = 4
SSD_CHUNK = 128
SSD_WIDTH = 1024
SSD_CONV_DIM = SSD_WIDTH + 2 * SSD_GROUPS * SSD_STATE
IN_PROJ = 3600
IN_MAIN = 3584
IN_PAD = IN_MAIN + 2 * 128
FFN = 2816
EPS = 1e-6
LANES = 128
ROW_TILE = 256

ADAM_LR = 0.001
ADAM_B1 = 0.9
ADAM_B2 = 0.999
ADAM_EPS = 1e-08
ADAM_WD = 0.01
ADAM_STEP = 10


def _sigmoid(x):
    return 1.0 / (1.0 + jnp.exp(-x))


def _silu(x):
    return x * _sigmoid(x)


def _dsilu(x):
    s = _sigmoid(x)
    return s * (1.0 + x * (1.0 - s))


_GELU_K = math.sqrt(2.0 / math.pi)
_GELU_C = 0.044715


def _gelu(x):
    t = jnp.tanh(_GELU_K * (x + _GELU_C * x * x * x))
    return 0.5 * x * (1.0 + t)


def _dgelu(x):
    t = jnp.tanh(_GELU_K * (x + _GELU_C * x * x * x))
    return 0.5 * (1.0 + t) + 0.5 * x * (1.0 - t * t) * _GELU_K * (1.0 + 3.0 * _GELU_C * x * x)


def _softplus(x):
    e = jnp.exp(-jnp.abs(x))
    u = 1.0 + e
    log1p = jnp.where(u == 1.0, e, jnp.log(u) * e / jnp.where(u == 1.0, 1.0, u - 1.0))
    return jnp.maximum(x, 0.0) + log1p


def _rstd(x):
    return lax.rsqrt(jnp.mean(x * x, axis=-1, keepdims=True) + EPS)


def _rms_bwd(x, r, gain, dy):
    dyg = dy * gain
    dx = r * dyg - x * (r * r * r) * jnp.mean(x * dyg, axis=-1, keepdims=True)
    dgain = jnp.sum(dy * x * r, axis=0, keepdims=True)
    return dx, dgain


def _dot(a, b):
    return jnp.dot(a, b, preferred_element_type=F32)


def _dot_nt(a, b):
    return lax.dot_general(a, b, (((1,), (1,)), ((), ())), preferred_element_type=F32)


def _dot_tn(a, b):
    return lax.dot_general(a, b, (((0,), (0,)), ((), ())), preferred_element_type=F32)


def _row_spec(tile, cols):
    return pl.BlockSpec((tile, cols), lambda i: (i, 0))


def _full_spec(shape):
    nd = len(shape)
    return pl.BlockSpec(shape, lambda *_: (0,) * nd)


def _const_spec(shape):
    nd = len(shape)
    return pl.BlockSpec(shape, lambda *_: (0,) * nd, pipeline_mode=pl.Buffered(1))


def _layer_spec(shape, layer, block=0):
    if layer is None:
        return pl.BlockSpec(tuple(shape), lambda *_: (block, 0), pipeline_mode=pl.Buffered(1))
    return pl.BlockSpec((None,) + tuple(shape), lambda *_: (layer, block, 0), pipeline_mode=pl.Buffered(1))


def _acc_rows(ref, val, first):
    @pl.when(first)
    def _():
        ref[...] = val

    @pl.when(jnp.logical_not(first))
    def _():
        ref[...] += val


def _pick_tile(n, cap):
    best = LANES
    for t in range(LANES, cap + 1, LANES):
        if n % t == 0:
            best = t
    return best


def _mm_tn(a, b, name):
    k, m = a.shape
    _, n = b.shape
    tm = _pick_tile(m, 512)
    tn = _pick_tile(n, 1536)

    def body(a_ref, b_ref, o_ref):
        o_ref[...] = _dot_tn(a_ref[...], b_ref[...]).astype(BF16)

    return pl.pallas_call(
        body, name=name, grid=(n // tn, m // tm),
        in_specs=[pl.BlockSpec((k, tm), lambda j, i: (0, i)), pl.BlockSpec((k, tn), lambda j, i: (0, j))],
        out_specs=pl.BlockSpec((tm, tn), lambda j, i: (i, j)),
        out_shape=jax.ShapeDtypeStruct((m, n), BF16),
    )(a, b)


def _rms_inproj(x, gain, w_pad, layer, name):
    L = x.shape[0]

    def body(x_ref, g_ref, w_ref, u_ref, z_ref, xbc_ref, dt_ref, h_ref):
        xv = x_ref[...]
        h = (xv * _rstd(xv) * g_ref[...]).astype(BF16)
        h_ref[...] = h
        p = _dot(h, w_ref[...])
        u_ref[...] = p[:, :1024]
        z_ref[...] = p[:, 1024:2048]
        xbc_ref[...] = p[:, 2048:IN_MAIN]
        dt_ref[...] = p[:, IN_MAIN:IN_PAD]

    return pl.pallas_call(
        body, name=name, grid=(L // ROW_TILE,),
        in_specs=[_row_spec(ROW_TILE, D_MODEL), _full_spec((1, D_MODEL)), _layer_spec((D_MODEL, IN_PAD), layer)],
        out_specs=[_row_spec(ROW_TILE, 1024), _row_spec(ROW_TILE, 1024), _row_spec(ROW_TILE, SSD_CONV_DIM),
                   _row_spec(ROW_TILE, 256), _row_spec(ROW_TILE, D_MODEL)],
        out_shape=[jax.ShapeDtypeStruct((L, 1024), F32), jax.ShapeDtypeStruct((L, 1024), F32),
                   jax.ShapeDtypeStruct((L, SSD_CONV_DIM), F32), jax.ShapeDtypeStruct((L, 256), F32),
                   jax.ShapeDtypeStruct((L, D_MODEL), BF16)],
    )(x, gain, w_pad)


def _s5_prep_math(lr, li, ls, bre, bim):
    step = jnp.exp(ls)
    mag = jnp.exp(lr * step)
    ang = li * step
    are = mag * jnp.cos(ang)
    aim = mag * jnp.sin(ang)
    den = lr * lr + li * li
    nr = are - 1.0
    ni = aim
    cre = (nr * lr + ni * li) / den
    cim = (ni * lr - nr * li) / den
    bbre = cre[None] * bre - cim[None] * bim
    bbim = cre[None] * bim + cim[None] * bre
    return are, aim, bbre, bbim


def _s5_prep(lr, li, ls, bre, bim, name):
    def body(lr_ref, li_ref, ls_ref, bre_ref, bim_ref, are_ref, aim_ref, bbre_ref, bbim_ref):
        are, aim, bbre, bbim = _s5_prep_math(lr_ref[...], li_ref[...], ls_ref[...], bre_ref[...], bim_ref[...])
        are_ref[...] = are
        aim_ref[...] = aim
        bbre_ref[...] = bbre
        bbim_ref[...] = bbim

    gp = jax.ShapeDtypeStruct((S5_GROUPS, S5_STATE), F32)
    hgp = jax.ShapeDtypeStruct((S5_GROUP, S5_GROUPS, S5_STATE), F32)
    return pl.pallas_call(body, name=name, out_shape=[gp, gp, hgp, hgp])(lr, li, ls, bre, bim)


def _s5_prep_bwd(lr, li, ls, bre, bim, dare, daim, dbbre, dbbim, name):
    def body(lr_ref, li_ref, ls_ref, bre_ref, bim_ref, dare_ref, daim_ref, dbbre_ref, dbbim_ref,
             dlr_ref, dli_ref, dls_ref, dbre_ref, dbim_ref):
        _, vjp = jax.vjp(_s5_prep_math, lr_ref[...], li_ref[...], ls_ref[...], bre_ref[...], bim_ref[...])
        dlr, dli, dls, dbre, dbim = vjp((dare_ref[...], daim_ref[...], dbbre_ref[...], dbbim_ref[...]))
        dlr_ref[...] = dlr
        dli_ref[...] = dli
        dls_ref[...] = dls
        dbre_ref[...] = dbre
        dbim_ref[...] = dbim

    gp = jax.ShapeDtypeStruct((S5_GROUPS, S5_STATE), F32)
    g1 = jax.ShapeDtypeStruct((S5_GROUPS, 1), F32)
    hgp = jax.ShapeDtypeStruct((S5_GROUP, S5_GROUPS, S5_STATE), F32)
    return pl.pallas_call(body, name=name, out_shape=[gp, gp, g1, hgp, hgp])(
        lr, li, ls, bre, bim, dare, daim, dbbre, dbbim)


def _cmul_add(ar, ai, sr, si, br, bi):
    return ar * sr - ai * si + br, ar * si + ai * sr + bi


def _shift_rows_down(v):
    rolled = pltpu.roll(v, 1, 0)
    row = lax.broadcasted_iota(jnp.int32, v.shape, 0)
    return jnp.where(row == 0, 0.0, rolled)


def _shift_rows_up(v):
    rolled = pltpu.roll(v, SEGS - 1, 0)
    row = lax.broadcasted_iota(jnp.int32, v.shape, 0)
    return jnp.where(row == SEGS - 1, 0.0, rolled)


def _segment_power(ar, ai, steps):
    n = 1
    while n < steps:
        ar, ai = ar * ar - ai * ai, 2.0 * ar * ai
        n *= 2
    assert n == steps
    return ar, ai


def _half_segment_entries(ar, ai, first, second, half_steps, shift):
    pr, pi = _segment_power(ar, ai, half_steps)
    er = jnp.zeros_like(first[0])
    ei = jnp.zeros_like(first[1])
    for _ in range(SEGS - 1):
        mr, mi = _cmul_add(pr, pi, er, ei, *first)
        nr, ni = _cmul_add(pr, pi, mr, mi, *second)
        er, ei = shift(nr), shift(ni)
    mr, mi = _cmul_add(pr, pi, er, ei, *first)
    return (er, ei), (mr, mi)


def _s5_scan(u_perm, bre_bd, bim_bd, cre_bd, cim_bd, are, aim, name):
    L = u_perm.shape[0]
    half = L // SEGS // 2

    def body(u_ref, bre_ref, bim_ref, cre_ref, cim_ref, are_ref, aim_ref, y_ref, xr_ref, xi_ref):
        u = u_ref[...].astype(BF16)
        xr_ref[...] = _dot(u, bre_ref[0])
        xi_ref[...] = _dot(u, bim_ref[0])
        ar = jnp.broadcast_to(are_ref[0], (SEGS, S5_TILE_ST))
        ai = jnp.broadcast_to(aim_ref[0], (SEGS, S5_TILE_ST))
        zero = jnp.zeros((SEGS, S5_TILE_ST), F32)
        block = lambda j: pl.ds(pl.multiple_of(j * SEGS, SEGS), SEGS)

        def finals(j, c):
            lo, hi = block(j), block(j + half)
            return (*_cmul_add(ar, ai, c[0], c[1], xr_ref[lo, :], xi_ref[lo, :]),
                    *_cmul_add(ar, ai, c[2], c[3], xr_ref[hi, :], xi_ref[hi, :]))

        f = lax.fori_loop(0, half, finals, (zero,) * 4, unroll=4)
        e_lo, e_hi = _half_segment_entries(ar, ai, f[:2], f[2:], half, _shift_rows_down)

        def scan(j, c):
            lo, hi = block(j), block(j + half)
            s_lo = _cmul_add(ar, ai, c[0], c[1], xr_ref[lo, :], xi_ref[lo, :])
            s_hi = _cmul_add(ar, ai, c[2], c[3], xr_ref[hi, :], xi_ref[hi, :])
            xr_ref[lo, :], xi_ref[lo, :] = s_lo
            xr_ref[hi, :], xi_ref[hi, :] = s_hi
            return (*s_lo, *s_hi)

        lax.fori_loop(0, half, scan, (*e_lo, *e_hi), unroll=4)
        y_ref[...] = (_dot(xr_ref[...].astype(BF16), cre_ref[0]) - _dot(xi_ref[...].astype(BF16), cim_ref[0]))

    tile3 = lambda a, b: pl.BlockSpec((1, a, b), lambda k: (k, 0, 0))
    return pl.pallas_call(
        body, name=name, grid=(S5_TILES,),
        in_specs=[pl.BlockSpec((L, S5_TILE_IN), lambda k: (0, k)),
                  tile3(S5_TILE_IN, S5_TILE_ST), tile3(S5_TILE_IN, S5_TILE_ST),
                  tile3(S5_TILE_ST, S5_TILE_IN), tile3(S5_TILE_ST, S5_TILE_IN),
                  tile3(1, S5_TILE_ST), tile3(1, S5_TILE_ST)],
        out_specs=[pl.BlockSpec((L, S5_TILE_IN), lambda k: (0, k)),
                   pl.BlockSpec((L, S5_TILE_ST), lambda k: (0, k)), pl.BlockSpec((L, S5_TILE_ST), lambda k: (0, k))],
        out_shape=[jax.ShapeDtypeStruct((L, 1024), F32), jax.ShapeDtypeStruct((L, S5_COLS), F32),
                   jax.ShapeDtypeStruct((L, S5_COLS), F32)],
    )(u_perm, bre_bd, bim_bd, cre_bd, cim_bd, are, aim)


def _s5_scan_bwd(dy_perm, u_perm, xr, xi, bret_bd, bimt_bd, cret_bd, cimt_bd, are, aim, name):
    L = u_perm.shape[0]
    steps = L // SEGS
    half = steps // 2

    def body(dy_ref, u_ref, xr_ref, xi_ref, bret_ref, bimt_ref, cret_ref, cimt_ref, are_ref, aim_ref,
             du_ref, dar_ref, dai_ref, dcre_ref, dcim_ref, dbre_ref, dbim_ref, gr_ref, gi_ref):
        dy = dy_ref[...].astype(BF16)
        u = u_ref[...].astype(BF16)
        gr_ref[...] = _dot(dy, cret_ref[0])
        gi_ref[...] = -_dot(dy, cimt_ref[0])
        ar = jnp.broadcast_to(are_ref[0], (SEGS, S5_TILE_ST))
        ai = -jnp.broadcast_to(aim_ref[0], (SEGS, S5_TILE_ST))
        zero = jnp.zeros((SEGS, S5_TILE_ST), F32)
        block = lambda j: pl.ds(pl.multiple_of(j * SEGS, SEGS), SEGS)

        def finals(k, c):
            hi, lo = block(steps - 1 - k), block(half - 1 - k)
            return (*_cmul_add(ar, ai, c[0], c[1], gr_ref[hi, :], gi_ref[hi, :]),
                    *_cmul_add(ar, ai, c[2], c[3], gr_ref[lo, :], gi_ref[lo, :]))

        f = lax.fori_loop(0, half, finals, (zero,) * 4, unroll=4)
        e_hi, e_lo = _half_segment_entries(ar, ai, f[:2], f[2:], half, _shift_rows_up)

        def scan(k, c):
            accr, acci = c[4], c[5]
            j_hi, j_lo = steps - 1 - k, half - 1 - k
            hi, lo = block(j_hi), block(j_lo)
            hr, hi_im = _cmul_add(ar, ai, c[0], c[1], gr_ref[hi, :], gi_ref[hi, :])
            lr, lo_im = _cmul_add(ar, ai, c[2], c[3], gr_ref[lo, :], gi_ref[lo, :])
            gr_ref[hi, :], gi_ref[hi, :] = hr, hi_im
            gr_ref[lo, :], gi_ref[lo, :] = lr, lo_im
            before_hi = block(j_hi - 1)
            before_lo = block(jnp.maximum(j_lo - 1, 0))
            live = (j_lo > 0).astype(F32)
            xhr, xhi = xr_ref[before_hi, :], xi_ref[before_hi, :]
            xlr, xli = xr_ref[before_lo, :] * live, xi_ref[before_lo, :] * live
            accr = accr + (hr * xhr + hi_im * xhi) + (lr * xlr + lo_im * xli)
            acci = acci + (hi_im * xhr - hr * xhi) + (lo_im * xlr - lr * xli)
            return hr, hi_im, lr, lo_im, accr, acci

        out = lax.fori_loop(0, half, scan, (*e_hi, *e_lo, zero, zero), unroll=2)
        accr, acci = out[4], out[5]
        first = pl.ds(0, SEGS)
        last = pl.ds((steps - 1) * SEGS, SEGS)
        xpr = _shift_rows_down(xr_ref[last, :])
        xpi = _shift_rows_down(xi_ref[last, :])
        g0r = gr_ref[first, :]
        g0i = gi_ref[first, :]
        accr = accr + g0r * xpr + g0i * xpi
        acci = acci + g0i * xpr - g0r * xpi
        dar_ref[0] = jnp.sum(accr, axis=0, keepdims=True)
        dai_ref[0] = jnp.sum(acci, axis=0, keepdims=True)

        grb = gr_ref[...].astype(BF16)
        gib = gi_ref[...].astype(BF16)
        du_ref[...] = _dot(grb, bret_ref[0]) + _dot(gib, bimt_ref[0])
        dbre_ref[0] = _dot_tn(u, grb)
        dbim_ref[0] = _dot_tn(u, gib)
        dcre_ref[0] = _dot_tn(dy, xr_ref[...].astype(BF16))
        dcim_ref[0] = -_dot_tn(dy, xi_ref[...].astype(BF16))

    tile3 = lambda a, b: pl.BlockSpec((1, a, b), lambda k: (k, 0, 0))
    col_in = pl.BlockSpec((L, S5_TILE_IN), lambda k: (0, k))
    col_st = pl.BlockSpec((L, S5_TILE_ST), lambda k: (0, k))
    dense = jax.ShapeDtypeStruct((S5_TILES, S5_TILE_IN, S5_TILE_ST), F32)
    vec = jax.ShapeDtypeStruct((S5_TILES, 1, S5_TILE_ST), F32)
    return pl.pallas_call(
        body, name=name, grid=(S5_TILES,),
        in_specs=[col_in, col_in, col_st, col_st,
                  tile3(S5_TILE_ST, S5_TILE_IN), tile3(S5_TILE_ST, S5_TILE_IN),
                  tile3(S5_TILE_IN, S5_TILE_ST), tile3(S5_TILE_IN, S5_TILE_ST),
                  tile3(1, S5_TILE_ST), tile3(1, S5_TILE_ST)],
        out_specs=[col_in, tile3(1, S5_TILE_ST), tile3(1, S5_TILE_ST),
                   tile3(S5_TILE_IN, S5_TILE_ST), tile3(S5_TILE_IN, S5_TILE_ST),
                   tile3(S5_TILE_IN, S5_TILE_ST), tile3(S5_TILE_IN, S5_TILE_ST)],
        out_shape=[jax.ShapeDtypeStruct((L, 1024), F32), vec, vec, dense, dense, dense, dense],
        scratch_shapes=[pltpu.VMEM((L, S5_TILE_ST), F32), pltpu.VMEM((L, S5_TILE_ST), F32)],
    )(dy_perm, u_perm, xr, xi, bret_bd, bimt_bd, cret_bd, cimt_bd, are, aim)


def _s5_post(ys, u, d_skip, w_glu, b_glu, gain, layer, name):
    L = ys.shape[0]

    def body(ys_ref, u_ref, d_ref, w_ref, b_ref, g_ref, ya_ref):
        g = _gelu(ys_ref[...] + d_ref[...] * u_ref[...])
        q = _dot(g.astype(BF16), w_ref[...]) + b_ref[...]
        oa = g * _sigmoid(q)
        ya_ref[...] = (oa * _rstd(oa) * g_ref[...]).astype(BF16)

    vec = _full_spec((1, 1024))
    return pl.pallas_call(
        body, name=name, grid=(L // ROW_TILE,),
        in_specs=[_row_spec(ROW_TILE, 1024), _row_spec(ROW_TILE, 1024), vec, _layer_spec((1024, 1024), layer), vec,
                  vec],
        out_specs=_row_spec(ROW_TILE, 1024),
        out_shape=jax.ShapeDtypeStruct((L, 1024), BF16),
    )(ys, u, d_skip, w_glu, b_glu, gain)


def _s5_post_bwd(dx, w_out, ys, u, d_skip, w_glu, b_glu, gain, layer, name):
    L = ys.shape[0]

    def body(dx_ref, wo_ref, ys_ref, u_ref, d_ref, w_ref, b_ref, gn_ref,
             dys_ref, dus_ref, g_ref, dq_ref, dgain_ref, dd_ref, db_ref):
        first = pl.program_id(0) == 0
        uv = u_ref[...]
        yt = ys_ref[...] + d_ref[...] * uv
        g = _gelu(yt)
        gb = g.astype(BF16)
        q = _dot(gb, w_ref[...]) + b_ref[...]
        s = _sigmoid(q)
        oa = g * s
        dya = _dot_nt(dx_ref[...], wo_ref[...])
        doa, dgain = _rms_bwd(oa, _rstd(oa), gn_ref[...], dya)
        dq = doa * g * s * (1.0 - s)
        dqb = dq.astype(BF16)
        dg = doa * s + _dot_nt(dqb, w_ref[...])
        dyt = dg * _dgelu(yt)
        dys_ref[...] = dyt
        dus_ref[...] = dyt * d_ref[...]
        g_ref[...] = gb
        dq_ref[...] = dqb
        _acc_rows(dgain_ref, dgain, first)
        _acc_rows(dd_ref, jnp.sum(dyt * uv, axis=0, keepdims=True), first)
        _acc_rows(db_ref, jnp.sum(dq, axis=0, keepdims=True), first)

    vec = _full_spec((1, 1024))
    row = _row_spec(ROW_TILE, 1024)
    vshape = jax.ShapeDtypeStruct((1, 1024), F32)
    return pl.pallas_call(
        body, name=name, grid=(L // ROW_TILE,),
        in_specs=[row, _layer_spec((1024, 1024), layer, 0), row, row, vec, _layer_spec((1024, 1024), layer), vec,
                  vec],
        out_specs=[row, row, row, row, vec, vec, vec],
        out_shape=[jax.ShapeDtypeStruct((L, 1024), F32), jax.ShapeDtypeStruct((L, 1024), F32),
                   jax.ShapeDtypeStruct((L, 1024), BF16), jax.ShapeDtypeStruct((L, 1024), BF16),
                   vshape, vshape, vshape],
    )(dx, w_out, ys, u, d_skip, w_glu, b_glu, gain)


CONV_TILE = 256


def _shift_time(v, d):
    if d == 0:
        return v
    rolled = pltpu.roll(v, d, 0)
    row = lax.broadcasted_iota(jnp.int32, v.shape, 0)
    return jnp.where(row < d, 0.0, rolled)


def _unshift_time(v, d):
    if d == 0:
        return v
    n = v.shape[0]
    rolled = pltpu.roll(v, n - d, 0)
    row = lax.broadcasted_iota(jnp.int32, v.shape, 0)
    return jnp.where(row >= n - d, 0.0, rolled)


def _ssd_conv(xbc, w, b, name):
    L = xbc.shape[0]

    def body(x_ref, w_ref, b_ref, o_ref):
        xv = x_ref[...]
        pre = jnp.broadcast_to(b_ref[...], xv.shape)
        for k in range(SSD_CONV):
            pre = pre + w_ref[k:k + 1, :] * _shift_time(xv, SSD_CONV - 1 - k)
        o_ref[...] = _silu(pre)

    col = pl.BlockSpec((L, CONV_TILE), lambda j: (0, j))
    return pl.pallas_call(
        body, name=name, grid=(SSD_CONV_DIM // CONV_TILE,),
        in_specs=[col, pl.BlockSpec((8, CONV_TILE), lambda j: (0, j)), pl.BlockSpec((1, CONV_TILE), lambda j: (0, j))],
        out_specs=col, out_shape=jax.ShapeDtypeStruct((L, SSD_CONV_DIM), F32),
    )(xbc, w, b)


def _ssd_conv_bwd(dxc, xbc, w, b, name):
    L = xbc.shape[0]

    def body(d_ref, x_ref, w_ref, b_ref, dx_ref, dw_ref, db_ref):
        xv = x_ref[...]
        shifted = [_shift_time(xv, SSD_CONV - 1 - k) for k in range(SSD_CONV)]
        pre = jnp.broadcast_to(b_ref[...], xv.shape)
        for k in range(SSD_CONV):
            pre = pre + w_ref[k:k + 1, :] * shifted[k]
        dpre = d_ref[...] * _dsilu(pre)
        dx = jnp.zeros_like(xv)
        rows = []
        for k in range(SSD_CONV):
            dx = dx + w_ref[k:k + 1, :] * _unshift_time(dpre, SSD_CONV - 1 - k)
            rows.append(jnp.sum(dpre * shifted[k], axis=0, keepdims=True))
        dx_ref[...] = dx
        dw_ref[...] = jnp.concatenate(rows + [jnp.zeros((8 - SSD_CONV, CONV_TILE), F32)], axis=0)
        db_ref[...] = jnp.sum(dpre, axis=0, keepdims=True)

    col = pl.BlockSpec((L, CONV_TILE), lambda j: (0, j))
    w_spec = pl.BlockSpec((8, CONV_TILE), lambda j: (0, j))
    b_spec = pl.BlockSpec((1, CONV_TILE), lambda j: (0, j))
    return pl.pallas_call(
        body, name=name, grid=(SSD_CONV_DIM // CONV_TILE,),
        in_specs=[col, col, w_spec, b_spec], out_specs=[col, w_spec, b_spec],
        out_shape=[jax.ShapeDtypeStruct((L, SSD_CONV_DIM), F32), jax.ShapeDtypeStruct((8, SSD_CONV_DIM), F32),
                   jax.ShapeDtypeStruct((1, SSD_CONV_DIM), F32)],
    )(dxc, xbc, w, b)


def _tri(lower):
    r = lax.broadcasted_iota(jnp.int32, (SSD_CHUNK, SSD_CHUNK), 0)
    c = lax.broadcasted_iota(jnp.int32, (SSD_CHUNK, SSD_CHUNK), 1)
    return (r >= c) if lower else (r <= c)


def _ssd_chunk_common(dt_ref, bias_ref, alog_ref):
    pre = dt_ref[...] + bias_ref[0]
    dtp = _softplus(pre)
    a_neg = -jnp.exp(alog_ref[0])
    dta = dtp * a_neg
    acum = _select_rows(_tri(True), dta)
    return pre, dtp, a_neg, dta, acum


GROUP_W = SSD_GROUP_HEADS * SSD_HEAD_DIM


def _head_expander():
    r = lax.broadcasted_iota(jnp.int32, (LANES, GROUP_W), 0)
    c = lax.broadcasted_iota(jnp.int32, (LANES, GROUP_W), 1)
    return (c // SSD_HEAD_DIM == r).astype(F32)


def _split_bf16(a, terms):
    parts = []
    rest = a
    for _ in range(terms):
        piece = rest.astype(BF16)
        parts.append(piece)
        rest = rest - piece.astype(F32)
    return parts


def _select_cols(a, sel, terms=3):
    lhs = jnp.concatenate(_split_bf16(a, terms), axis=1)
    rhs = jnp.concatenate([sel.astype(BF16)] * terms, axis=0)
    return _dot(lhs, rhs)


def _select_rows(sel, b, terms=3):
    lhs = jnp.concatenate([sel.astype(BF16)] * terms, axis=1)
    rhs = jnp.concatenate(_split_bf16(b, terms), axis=0)
    return _dot(lhs, rhs)


def _decay_mask(acum_all, acum_t, h, lower):
    seg = acum_all[:, h:h + 1] - acum_t[h:h + 1, :]
    return jnp.where(lower, jnp.exp(jnp.minimum(seg, 0.0)), 0.0)


def _ssd_scan(xc, dt, dt_bias, a_log, d_wide, expand, expand_t, name):
    L = xc.shape[0]
    nc = L // SSD_CHUNK

    def body(x_ref, b_ref, c_ref, dt_ref, bias_ref, alog_ref, d_ref, e_ref, et_ref, y_ref, sp_ref, s_ref, xdt_ref):
        @pl.when(pl.program_id(1) == 0)
        def _():
            s_ref[...] = jnp.zeros_like(s_ref)

        _, dtp_all, _, _, acum_all = _ssd_chunk_common(dt_ref, bias_ref, alog_ref)
        acum_t = acum_all.T
        wide = _select_cols(jnp.concatenate([acum_all, dtp_all], axis=0), e_ref[...])
        acum_e = wide[:SSD_CHUNK]
        alast_e = acum_e[SSD_CHUNK - 1:SSD_CHUNK, :]
        x = x_ref[...]
        xdt = x * wide[SSD_CHUNK:]
        xdt_ref[...] = xdt.astype(BF16)
        bm = b_ref[...].astype(BF16)
        cm = c_ref[...].astype(BF16)
        cb = _dot_nt(cm, bm)
        lower = _tri(True)
        sp = s_ref[...]
        for h in range(SSD_GROUP_HEADS):
            cols = slice(h * SSD_HEAD_DIM, (h + 1) * SSD_HEAD_DIM)
            lm = _decay_mask(acum_all, acum_t, h, lower)
            y_ref[:, cols] = _dot((cb * lm).astype(BF16), xdt_ref[:, cols])
        y_ref[...] += _dot_nt(cm, sp.astype(BF16)) * jnp.exp(acum_e) + d_ref[0] * x
        wgt = xdt * jnp.exp(alast_e - acum_e)
        ealast = jnp.exp(_select_rows(et_ref[...], acum_t)[:, SSD_CHUNK - 1:SSD_CHUNK])
        sp_ref[0, 0] = sp
        s_ref[...] = ealast * sp + _dot_tn(wgt.astype(BF16), bm)

    par = lambda n: pl.BlockSpec((1, 1, n), lambda g, c: (g, 0, 0))
    return pl.pallas_call(
        body, name=name, grid=(SSD_GROUPS, nc),
        in_specs=[pl.BlockSpec((SSD_CHUNK, GROUP_W), lambda g, c: (c, g)),
                  pl.BlockSpec((SSD_CHUNK, SSD_STATE), lambda g, c: (c, 8 + g)),
                  pl.BlockSpec((SSD_CHUNK, SSD_STATE), lambda g, c: (c, 10 + g)),
                  pl.BlockSpec((SSD_CHUNK, LANES), lambda g, c: (c, g)),
                  par(LANES), par(LANES), par(GROUP_W), _full_spec((LANES, GROUP_W)), _full_spec((GROUP_W, LANES))],
        out_specs=[pl.BlockSpec((SSD_CHUNK, GROUP_W), lambda g, c: (c, g)),
                   pl.BlockSpec((1, 1, GROUP_W, SSD_STATE), lambda g, c: (c, g, 0, 0))],
        out_shape=[jax.ShapeDtypeStruct((L, SSD_WIDTH), F32),
                   jax.ShapeDtypeStruct((nc, SSD_GROUPS, GROUP_W, SSD_STATE), F32)],
        scratch_shapes=[pltpu.VMEM((GROUP_W, SSD_STATE), F32), pltpu.VMEM((SSD_CHUNK, GROUP_W), BF16)],
    )(xc, xc, xc, dt, dt_bias, a_log, d_wide, expand, expand_t)


def _ssd_scan_bwd(dy, xc, dt, sprev, dt_bias, a_log, d_wide, expand, expand_t, name):
    L = xc.shape[0]
    nc = L // SSD_CHUNK

    def body(dy_ref, x_ref, b_ref, c_ref, dt_ref, sp_ref, bias_ref, alog_ref, d_ref, e_ref, et_ref,
             dx_ref, db_ref, dc_ref, ddt_ref, dbias_ref, dalog_ref, dd_ref, ds_ref, xdt_ref, dyb_ref):
        first = pl.program_id(1) == 0

        @pl.when(first)
        def _():
            ds_ref[...] = jnp.zeros_like(ds_ref)

        pre, dtp_all, a_neg, _, acum_all = _ssd_chunk_common(dt_ref, bias_ref, alog_ref)
        acum_t = acum_all.T
        e = e_ref[...]
        et = et_ref[...]
        wide = _select_cols(jnp.concatenate([acum_all, dtp_all], axis=0), e)
        acum_e = wide[:SSD_CHUNK]
        dtp_e = wide[SSD_CHUNK:]
        alast_e = acum_e[SSD_CHUNK - 1:SSD_CHUNK, :]
        dstate_e = jnp.exp(alast_e - acum_e)
        x = x_ref[...]
        dy = dy_ref[...]
        xdt = x * dtp_e
        xdt_ref[...] = xdt.astype(BF16)
        dyb_ref[...] = dy.astype(BF16)
        bm = b_ref[...].astype(BF16)
        cm = c_ref[...].astype(BF16)
        cb = _dot_nt(cm, bm)
        sp = sp_ref[0, 0]
        spb = sp.astype(BF16)
        dsn = ds_ref[...]
        dsb = dsn.astype(BF16)
        z = _dot_nt(cm, spb)
        dz = dy * jnp.exp(acum_e)
        dzb = dz.astype(BF16)
        dc_acc = _dot(dzb, spb)
        ealast = jnp.exp(_select_rows(et, acum_t)[:, SSD_CHUNK - 1:SSD_CHUNK])
        ds_ref[...] = _dot_tn(dzb, cm) + ealast * dsn
        dw = _dot_nt(bm, dsb)
        wgt = xdt * dstate_e
        db_acc = _dot(wgt.astype(BF16), dsb)
        lower = _tri(True)
        lane = lax.broadcasted_iota(jnp.int32, (SSD_CHUNK, LANES), 1)
        row = lax.broadcasted_iota(jnp.int32, (SSD_CHUNK, LANES), 0)
        dcb = jnp.zeros((SSD_CHUNK, SSD_CHUNK), F32)
        dacum_all = jnp.zeros((SSD_CHUNK, LANES), F32)
        dacum_cols = jnp.zeros((SSD_CHUNK, LANES), F32)
        for h in range(SSD_GROUP_HEADS):
            cols = slice(h * SSD_HEAD_DIM, (h + 1) * SSD_HEAD_DIM)
            lm = _decay_mask(acum_all, acum_t, h, lower)
            dm = _dot_nt(dyb_ref[:, cols], xdt_ref[:, cols])
            dx_ref[:, cols] = _dot_tn((cb * lm).astype(BF16), dyb_ref[:, cols])
            dm_lm = dm * lm
            dcb = dcb + dm_lm
            q = dm_lm * cb
            dacum_all = jnp.where(lane == h, jnp.sum(q, axis=1, keepdims=True), dacum_all)
            dacum_cols = jnp.where(row == h, jnp.sum(q, axis=0, keepdims=True), dacum_cols)
        dxdt = dx_ref[...] + dw * dstate_e
        sums = _select_cols(jnp.concatenate([dz * z, dw * wgt, dxdt * x, dy * x], axis=0), et, terms=2)
        dacum_off = sums[0:SSD_CHUNK]
        dds_ds = sums[SSD_CHUNK:2 * SSD_CHUNK]
        ddtp_x = sums[2 * SSD_CHUNK:3 * SSD_CHUNK]
        dd_part = sums[3 * SSD_CHUNK:4 * SSD_CHUNK]
        ds_s = jnp.sum(_select_rows(e, dsn * sp, terms=2).T, axis=0, keepdims=True)
        dalast = ds_s * jnp.exp(acum_all[SSD_CHUNK - 1:SSD_CHUNK, :]) + jnp.sum(dds_ds, axis=0, keepdims=True)
        dacum_all = dacum_all - dacum_cols.T + dacum_off - dds_ds + jnp.where(row == SSD_CHUNK - 1, dalast, 0.0)
        dx_ref[...] = d_ref[0] * dy + dxdt * dtp_e
        dcbb = dcb.astype(BF16)
        dc_ref[...] = dc_acc + _dot(dcbb, bm)
        db_ref[...] = db_acc + _dot_tn(dcbb, cm)
        ddta = _select_rows(_tri(False), dacum_all)
        ddt = (ddtp_x + ddta * a_neg) * _sigmoid(pre)
        ddt_ref[...] = ddt
        _acc_rows(dbias_ref, jnp.sum(ddt, axis=0, keepdims=True)[None], first)
        _acc_rows(dalog_ref, (jnp.sum(ddta * dtp_all, axis=0, keepdims=True) * a_neg)[None], first)
        _acc_rows(dd_ref, jnp.sum(dd_part, axis=0, keepdims=True)[None], first)

    rev = lambda c: nc - 1 - c
    par = lambda n: pl.BlockSpec((1, 1, n), lambda g, c: (g, 0, 0))
    pshape = jax.ShapeDtypeStruct((SSD_GROUPS, 1, LANES), F32)
    return pl.pallas_call(
        body, name=name, grid=(SSD_GROUPS, nc),
        in_specs=[pl.BlockSpec((SSD_CHUNK, GROUP_W), lambda g, c: (rev(c), g)),
                  pl.BlockSpec((SSD_CHUNK, GROUP_W), lambda g, c: (rev(c), g)),
                  pl.BlockSpec((SSD_CHUNK, SSD_STATE), lambda g, c: (rev(c), 8 + g)),
                  pl.BlockSpec((SSD_CHUNK, SSD_STATE), lambda g, c: (rev(c), 10 + g)),
                  pl.BlockSpec((SSD_CHUNK, LANES), lambda g, c: (rev(c), g)),
                  pl.BlockSpec((1, 1, GROUP_W, SSD_STATE), lambda g, c: (rev(c), g, 0, 0)),
                  par(LANES), par(LANES), par(GROUP_W), _full_spec((LANES, GROUP_W)), _full_spec((GROUP_W, LANES))],
        out_specs=[pl.BlockSpec((SSD_CHUNK, GROUP_W), lambda g, c: (rev(c), g)),
                   pl.BlockSpec((SSD_CHUNK, SSD_STATE), lambda g, c: (rev(c), g)),
                   pl.BlockSpec((SSD_CHUNK, SSD_STATE), lambda g, c: (rev(c), g)),
                   pl.BlockSpec((SSD_CHUNK, LANES), lambda g, c: (rev(c), g)),
                   par(LANES), par(LANES), par(LANES)],
        out_shape=[jax.ShapeDtypeStruct((L, SSD_WIDTH), F32), jax.ShapeDtypeStruct((L, 256), F32),
                   jax.ShapeDtypeStruct((L, 256), F32), jax.ShapeDtypeStruct((L, 256), F32),
                   pshape, pshape, pshape],
        scratch_shapes=[pltpu.VMEM((GROUP_W, SSD_STATE), F32), pltpu.VMEM((SSD_CHUNK, GROUP_W), BF16),
                        pltpu.VMEM((SSD_CHUNK, GROUP_W), BF16)],
    )(dy, xc, xc, xc, dt, sprev, dt_bias, a_log, d_wide, expand, expand_t)


def _ssd_post(y, z, gain, name):
    L = y.shape[0]

    def body(y_ref, z_ref, g_ref, o_ref):
        ob = y_ref[...] * _silu(z_ref[...])
        o_ref[...] = (ob * _rstd(ob) * g_ref[...]).astype(BF16)

    row = _row_spec(ROW_TILE, 1024)
    return pl.pallas_call(body, name=name, grid=(L // ROW_TILE,), in_specs=[row, row, _full_spec((1, 1024))],
                          out_specs=row, out_shape=jax.ShapeDtypeStruct((L, 1024), BF16))(y, z, gain)


def _ssd_post_bwd(dx, w_out, y, z, gain, layer, name):
    L = y.shape[0]

    def body(dx_ref, wo_ref, y_ref, z_ref, g_ref, dy_ref, dz_ref, dgain_ref):
        first = pl.program_id(0) == 0
        yv = y_ref[...]
        zv = z_ref[...]
        sz = _silu(zv)
        ob = yv * sz
        dyb = _dot_nt(dx_ref[...], wo_ref[...])
        dob, dgain = _rms_bwd(ob, _rstd(ob), g_ref[...], dyb)
        dy_ref[...] = dob * sz
        dz_ref[...] = dob * yv * _dsilu(zv)
        _acc_rows(dgain_ref, dgain, first)

    row = _row_spec(ROW_TILE, 1024)
    vec = _full_spec((1, 1024))
    return pl.pallas_call(
        body, name=name, grid=(L // ROW_TILE,),
        in_specs=[row, _layer_spec((1024, 1024), layer, 1), row, row, vec],
        out_specs=[row, row, vec],
        out_shape=[jax.ShapeDtypeStruct((L, 1024), F32), jax.ShapeDtypeStruct((L, 1024), F32),
                   jax.ShapeDtypeStruct((1, 1024), F32)],
    )(dx, w_out, y, z, gain)


def _out_proj(x, ya, yb, w_out, layer, name):
    L = x.shape[0]

    def body(x_ref, ya_ref, yb_ref, w_ref, o_ref):
        o_ref[...] = x_ref[...] + _dot(ya_ref[...], w_ref[:1024, :]) + _dot(yb_ref[...], w_ref[1024:, :])

    row = _row_spec(ROW_TILE, 1024)
    return pl.pallas_call(body, name=name, grid=(L // ROW_TILE,),
                          in_specs=[row, row, row, _layer_spec((2048, 1024), layer)],
                          out_specs=row, out_shape=jax.ShapeDtypeStruct((L, D_MODEL), F32))(x, ya, yb, w_out)


def _ffn(x, gain, w_gate, w_up, w_down, layer, name):
    L = x.shape[0]

    def body(x_ref, g_ref, wg_ref, wu_ref, wd_ref, o_ref, gt_ref, up_ref):
        xv = x_ref[...]
        h = (xv * _rstd(xv) * g_ref[...]).astype(BF16)
        gt = _dot_nt(h, wg_ref[...])
        up = _dot_nt(h, wu_ref[...])
        gt_ref[...] = gt
        up_ref[...] = up
        o_ref[...] = xv + _dot((_silu(gt) * up).astype(BF16), wd_ref[...])

    row = _row_spec(ROW_TILE, D_MODEL)
    hid = _row_spec(ROW_TILE, FFN)
    return pl.pallas_call(
        body, name=name, grid=(L // ROW_TILE,),
        in_specs=[row, _full_spec((1, D_MODEL)), _layer_spec((FFN, D_MODEL), layer),
                  _layer_spec((FFN, D_MODEL), layer), _layer_spec((FFN, D_MODEL), layer)],
        out_specs=[row, hid, hid],
        out_shape=[jax.ShapeDtypeStruct((L, D_MODEL), F32), jax.ShapeDtypeStruct((L, FFN), F32),
                   jax.ShapeDtypeStruct((L, FFN), F32)],
    )(x, gain, w_gate, w_up, w_down)


def _ffn_bwd(dx2, x1, gt, up, gain, w_gate, w_up, w_down, layer, name):
    L = x1.shape[0]

    def body(d_ref, x_ref, gt_ref, up_ref, g_ref, wg_ref, wu_ref, wd_ref,
             dx_ref, dxb_ref, h_ref, act_ref, dgt_ref, dup_ref, dgain_ref):
        first = pl.program_id(0) == 0
        dv = d_ref[...]
        xv = x_ref[...]
        r = _rstd(xv)
        h_ref[...] = (xv * r * g_ref[...]).astype(BF16)
        gtv = gt_ref[...]
        upv = up_ref[...]
        sg = _silu(gtv)
        act_ref[...] = (sg * upv).astype(BF16)
        dact = _dot_nt(dv.astype(BF16), wd_ref[...])
        dgt = (dact * upv * _dsilu(gtv)).astype(BF16)
        dup = (dact * sg).astype(BF16)
        dgt_ref[...] = dgt
        dup_ref[...] = dup
        dh = _dot(dgt, wg_ref[...]) + _dot(dup, wu_ref[...])
        dxn, dgain = _rms_bwd(xv, r, g_ref[...], dh)
        dx = dv + dxn
        dx_ref[...] = dx
        dxb_ref[...] = dx.astype(BF16)
        _acc_rows(dgain_ref, dgain, first)

    row = _row_spec(ROW_TILE, D_MODEL)
    hid = _row_spec(ROW_TILE, FFN)
    vec = _full_spec((1, D_MODEL))
    return pl.pallas_call(
        body, name=name, grid=(L // ROW_TILE,),
        in_specs=[row, row, hid, hid, vec, _layer_spec((FFN, D_MODEL), layer), _layer_spec((FFN, D_MODEL), layer),
                  _layer_spec((FFN, D_MODEL), layer)],
        out_specs=[row, row, row, hid, hid, hid, vec],
        out_shape=[jax.ShapeDtypeStruct((L, D_MODEL), F32), jax.ShapeDtypeStruct((L, D_MODEL), BF16),
                   jax.ShapeDtypeStruct((L, D_MODEL), BF16),
                   jax.ShapeDtypeStruct((L, FFN), BF16), jax.ShapeDtypeStruct((L, FFN), BF16),
                   jax.ShapeDtypeStruct((L, FFN), BF16), jax.ShapeDtypeStruct((1, D_MODEL), F32)],
    )(dx2, x1, gt, up, gain, w_gate, w_up, w_down)


def _inproj_bwd(dx1, x0, du_skip, du_scan, dz, dxbc, ddt, gain, w_pad, layer, name):
    L = x0.shape[0]

    def body(d_ref, x_ref, dus_ref, duc_ref, dz_ref, dxbc_ref, ddt_ref, g_ref, w_ref,
             dx_ref, dxb_ref, dp_ref, dgain_ref):
        first = pl.program_id(0) == 0
        xv = x_ref[...]
        dp = jnp.concatenate([dus_ref[...] + duc_ref[...], dz_ref[...], dxbc_ref[...], ddt_ref[...]],
                             axis=1).astype(BF16)
        dp_ref[...] = dp
        dh = _dot_nt(dp, w_ref[...])
        dxn, dgain = _rms_bwd(xv, _rstd(xv), g_ref[...], dh)
        dx = d_ref[...] + dxn
        dx_ref[...] = dx
        dxb_ref[...] = dx.astype(BF16)
        _acc_rows(dgain_ref, dgain, first)

    row = _row_spec(ROW_TILE, D_MODEL)
    vec = _full_spec((1, D_MODEL))
    return pl.pallas_call(
        body, name=name, grid=(L // ROW_TILE,),
        in_specs=[row, row, row, row, row, _row_spec(ROW_TILE, SSD_CONV_DIM), _row_spec(ROW_TILE, 256), vec,
                  _layer_spec((D_MODEL, IN_PAD), layer)],
        out_specs=[row, row, _row_spec(ROW_TILE, IN_PAD), vec],
        out_shape=[jax.ShapeDtypeStruct((L, D_MODEL), F32), jax.ShapeDtypeStruct((L, D_MODEL), BF16),
                   jax.ShapeDtypeStruct((L, IN_PAD), BF16), jax.ShapeDtypeStruct((1, D_MODEL), F32)],
    )(dx1, x0, du_skip, du_scan, dz, dxbc, ddt, gain, w_pad)


def _final_loss(x, gain, target, name):
    L = x.shape[0]

    def body(x_ref, g_ref, t_ref, loss_ref, dx_ref, dxb_ref, dgain_ref):
        first = pl.program_id(0) == 0
        xv = x_ref[...]
        r = _rstd(xv)
        err = xv * r * g_ref[...] - t_ref[...]
        part = 0.5 * jnp.sum(jnp.mean(err * err, axis=-1, keepdims=True), axis=0, keepdims=True)
        dx, dgain = _rms_bwd(xv, r, g_ref[...], err * (1.0 / D_MODEL))
        dx_ref[...] = dx
        dxb_ref[...] = dx.astype(BF16)
        _acc_rows(loss_ref, jnp.broadcast_to(part, (1, LANES)), first)
        _acc_rows(dgain_ref, dgain, first)

    row = _row_spec(ROW_TILE, D_MODEL)
    vec = _full_spec((1, D_MODEL))
    return pl.pallas_call(
        body, name=name, grid=(L // ROW_TILE,), in_specs=[row, vec, row],
        out_specs=[_full_spec((1, LANES)), row, row, vec],
        out_shape=[jax.ShapeDtypeStruct((1, LANES), F32), jax.ShapeDtypeStruct((L, D_MODEL), F32),
                   jax.ShapeDtypeStruct((L, D_MODEL), BF16), jax.ShapeDtypeStruct((1, D_MODEL), F32)],
    )(x, gain, target)


def _to_segments(a):
    L, n = a.shape
    return a.reshape(SEGS, L // SEGS, n).transpose(1, 0, 2).reshape(L, n)


def _from_segments(a):
    L, n = a.shape
    return a.reshape(L // SEGS, SEGS, n).transpose(1, 0, 2).reshape(L, n)


def _diag_block(g):
    k, a = divmod(g, S5_TILE_GROUPS)
    return k, slice(a * S5_GROUP, (a + 1) * S5_GROUP), slice(a * S5_STATE, (a + 1) * S5_STATE)


def _block_diag_build(mats, name):
    n = mats.shape[0]

    def body(m_ref, o_ref):
        o_ref[...] = jnp.zeros(o_ref.shape, BF16)
        for q in range(n):
            for g in range(S5_GROUPS):
                k, rows, cols = _diag_block(g)
                o_ref[q, k, rows, cols] = m_ref[q, g].astype(BF16)

    return pl.pallas_call(body, name=name,
                          out_shape=jax.ShapeDtypeStruct((n, S5_TILES, S5_TILE_IN, S5_TILE_ST), BF16))(mats)


def _block_diag_extract(dense, name):
    n = len(dense)

    def body(*refs):
        o_ref = refs[n]
        for q in range(n):
            for g in range(S5_GROUPS):
                k, rows, cols = _diag_block(g)
                o_ref[q, g] = refs[q][k, rows, cols]

    return pl.pallas_call(body, name=name,
                          out_shape=jax.ShapeDtypeStruct((n, S5_GROUPS, S5_GROUP, S5_STATE), F32))(*dense)


def _pad_in_proj(w):
    z = jnp.zeros(w.shape[:-1] + (LANES - SSD_GROUP_HEADS,), w.dtype)
    return jnp.concatenate([w[..., :IN_MAIN + 8], z, w[..., IN_MAIN + 8:], z], axis=-1)


def _unpad_in_proj(w):
    return jnp.concatenate([w[..., :IN_MAIN + 8], w[..., IN_MAIN + LANES:IN_MAIN + LANES + 8]], axis=-1)


def _lane_dense(a):
    return a.reshape(DEPTH, D_MODEL // LANES, LANES, -1).transpose(3, 1, 0, 2).reshape(-1, LANES)


def _from_lane_dense(a):
    return a.reshape(-1, D_MODEL // LANES, DEPTH, LANES).transpose(2, 1, 3, 0).reshape(DEPTH, D_MODEL, -1)


def _pad_heads(v):
    v = v.reshape(SSD_GROUPS, 1, SSD_GROUP_HEADS)
    return jnp.pad(v, ((0, 0), (0, 0), (0, LANES - SSD_GROUP_HEADS)))


def _unpad_heads(v):
    return v[:, 0, :SSD_GROUP_HEADS].reshape(SSD_HEADS)


def _layer_forward(x0, p, big, i, after_inproj=None, before_ffn=None):
    tag = "l%d_" % i
    ls = p["s5_log_step"].reshape(S5_GROUPS, 1)
    b_hgp = (p["s5_b_re"].transpose(2, 0, 1), p["s5_b_im"].transpose(2, 0, 1))
    are, aim, bbre, bbim = _s5_prep(p["s5_lam_re"], p["s5_lam_im"], ls, b_hgp[0], b_hgp[1], tag + "s5_prep")
    mats = jnp.stack([bbre.transpose(1, 0, 2), bbim.transpose(1, 0, 2), p["s5_c_re"], p["s5_c_im"]])
    bre_bd, bim_bd, cret_bd, cimt_bd = _block_diag_build(mats, tag + "s5_blockdiag")
    s5mats = dict(bre_bd=bre_bd, bim_bd=bim_bd, cret_bd=cret_bd, cimt_bd=cimt_bd,
                  bret_bd=bre_bd.transpose(0, 2, 1), bimt_bd=bim_bd.transpose(0, 2, 1),
                  cre_bd=cret_bd.transpose(0, 2, 1), cim_bd=cimt_bd.transpose(0, 2, 1),
                  are=are.reshape(S5_TILES, 1, S5_TILE_ST), aim=aim.reshape(S5_TILES, 1, S5_TILE_ST))

    u, z, xbc, dt, h1 = _rms_inproj(x0, p["norm_mix"].reshape(1, -1), big["w_in"], None, tag + "rms_inproj")
    if after_inproj is not None:
        after_inproj(u)
    u_perm = _to_segments(u)
    ys_perm, xr, xi = _s5_scan(u_perm, bre_bd, bim_bd, s5mats["cre_bd"], s5mats["cim_bd"],
                               s5mats["are"], s5mats["aim"], tag + "s5_scan")
    ys = _from_segments(ys_perm)
    ya = _s5_post(ys, u, p["s5_d"].reshape(1, -1), big["s5_w_glu"], p["s5_b_glu"].reshape(1, -1),
                  p["s5_norm"].reshape(1, -1), None, tag + "s5_post")

    conv_w = jnp.pad(p["ssd_conv_w"], ((0, 8 - SSD_CONV), (0, 0)))
    conv_b = p["ssd_conv_b"].reshape(1, -1)
    xc = _ssd_conv(xbc, conv_w, conv_b, tag + "ssd_conv")
    expand = _head_expander()
    heads = dict(dt_bias=_pad_heads(p["ssd_dt_bias"]), a_log=_pad_heads(p["ssd_a_log"]),
                 d=jnp.repeat(p["ssd_d"], SSD_HEAD_DIM).reshape(SSD_GROUPS, 1, GROUP_W),
                 expand=expand, expand_t=expand.T)
    y, sprev = _ssd_scan(xc, dt, heads["dt_bias"], heads["a_log"], heads["d"], expand, heads["expand_t"],
                         tag + "ssd_scan")
    yb = _ssd_post(y, z, p["ssd_norm"].reshape(1, -1), tag + "ssd_post")

    x1 = _out_proj(x0, ya, yb, big["w_out"], None, tag + "out_proj")
    ffn_matrices = before_ffn(x1) if before_ffn is not None else {}
    big = {**big, **ffn_matrices}
    x2, gt, up = _ffn(x1, p["norm_ffn"].reshape(1, -1), big["w_gate"], big["w_up"], big["w_down"], None,
                      tag + "ffn")
    saved = dict(x0=x0, h1=h1, u=u, u_perm=u_perm, z=z, xbc=xbc, dt=dt, xr=xr, xi=xi, ys=ys, ya=ya, xc=xc, y=y,
                 sprev=sprev, yb=yb, x1=x1, gt=gt, up=up, s5mats=s5mats, heads=heads, conv_w=conv_w,
                 conv_b=conv_b, ls=ls, b_hgp=b_hgp, ffn_matrices=ffn_matrices)
    return x2, saved


def _layer_backward(dx2, dx2b, p, big, s, i, after_ffn_grads=None, after_s5_grads=None):
    tag = "l%d_" % i
    g = {}
    dx1, dx1b, h2, act, dgt, dup, dgain = _ffn_bwd(dx2, s["x1"], s["gt"], s["up"], p["norm_ffn"].reshape(1, -1),
                                                  big["w_gate"], big["w_up"], big["w_down"], None, tag + "ffn_bwd")
    g["norm_ffn"] = dgain[0]
    g["w_down"] = _mm_tn(act, dx2b, tag + "dw_down")
    g["w_gate"] = _mm_tn(dgt, h2, tag + "dw_gate")
    g["w_up"] = _mm_tn(dup, h2, tag + "dw_up")
    g["w_out"] = _mm_tn(jnp.concatenate([s["ya"], s["yb"]], axis=1), dx1b, tag + "dw_out")
    if after_ffn_grads is not None:
        p = {**p, "s5_norm": p["s5_norm"] + after_ffn_grads(g)[0, 0]}

    dys, du_skip, gelu_b, dq_b, dgain, dd, dbg = _s5_post_bwd(
        dx1b, big["w_out"], s["ys"], s["u"], p["s5_d"].reshape(1, -1), big["s5_w_glu"],
        p["s5_b_glu"].reshape(1, -1), p["s5_norm"].reshape(1, -1), None, tag + "s5_post_bwd")
    g["s5_norm"] = dgain[0]
    g["s5_d"] = dd[0]
    g["s5_b_glu"] = dbg[0]
    g["s5_w_glu"] = _mm_tn(gelu_b, dq_b, tag + "dw_glu")
    m = s["s5mats"]
    du_perm, dar, dai, dcre_d, dcim_d, dbre_d, dbim_d = _s5_scan_bwd(
        _to_segments(dys), s["u_perm"], s["xr"], s["xi"], m["bret_bd"], m["bimt_bd"], m["cret_bd"], m["cimt_bd"],
        m["are"], m["aim"], tag + "s5_scan_bwd")
    du_scan = _from_segments(du_perm)
    diag = _block_diag_extract([dcre_d, dcim_d, dbre_d, dbim_d], tag + "s5_blockdiag_bwd")
    g["s5_c_re"], g["s5_c_im"] = diag[0], diag[1]
    dbbre = diag[2].transpose(1, 0, 2)
    dbbim = diag[3].transpose(1, 0, 2)
    dlr, dli, dls, dbre, dbim = _s5_prep_bwd(
        p["s5_lam_re"], p["s5_lam_im"], s["ls"], s["b_hgp"][0], s["b_hgp"][1],
        dar.reshape(S5_GROUPS, S5_STATE), dai.reshape(S5_GROUPS, S5_STATE), dbbre, dbbim, tag + "s5_prep_bwd")
    g["s5_lam_re"] = dlr
    g["s5_lam_im"] = dli
    g["s5_log_step"] = dls[:, 0]
    g["s5_b_re"] = dbre
    g["s5_b_im"] = dbim
    if after_s5_grads is not None:
        p = {**p, "ssd_norm": p["ssd_norm"] + after_s5_grads(g)[0, 0]}

    dy, dz, dgain = _ssd_post_bwd(dx1b, big["w_out"], s["y"], s["z"], p["ssd_norm"].reshape(1, -1), None,
                                  tag + "ssd_post_bwd")
    g["ssd_norm"] = dgain[0]
    hd = s["heads"]
    dxs, dbm, dcm, ddt, dbias, dalog, dd = _ssd_scan_bwd(dy, s["xc"], s["dt"], s["sprev"], hd["dt_bias"],
                                                       hd["a_log"], hd["d"], hd["expand"], hd["expand_t"],
                                                       tag + "ssd_scan_bwd")
    g["ssd_dt_bias"] = _unpad_heads(dbias)
    g["ssd_a_log"] = _unpad_heads(dalog)
    g["ssd_d"] = _unpad_heads(dd)
    dxc = jnp.concatenate([dxs, dbm, dcm], axis=1)
    dxbc, dcw, dcb = _ssd_conv_bwd(dxc, s["xbc"], s["conv_w"], s["conv_b"], tag + "ssd_conv_bwd")
    g["ssd_conv_w"] = dcw[:SSD_CONV]
    g["ssd_conv_b"] = dcb[0]

    dx0, dx0b, dproj, dgain = _inproj_bwd(dx1, s["x0"], du_skip, du_scan, dz, dxbc, ddt, p["norm_mix"].reshape(1, -1),
                                          big["w_in"], None, tag + "inproj_bwd")
    g["norm_mix"] = dgain[0]
    g["w_in"] = _mm_tn(s["h1"], dproj, tag + "dw_in")
    return dx0, dx0b, g


MIXER_BIG = ("w_in", "s5_w_glu", "w_out")
FFN_BIG = ("w_gate", "w_up", "w_down")
BIG = MIXER_BIG + FFN_BIG
COL_SHARDED = ("w_in",)
T_STORED = ("w_gate", "w_up")
LAYER_SMALL = ("norm_mix", "s5_lam_re", "s5_lam_im", "s5_log_step", "s5_b_re", "s5_b_im", "s5_c_re", "s5_c_im",
               "s5_d", "s5_b_glu", "s5_norm", "ssd_conv_w", "ssd_conv_b", "ssd_dt_bias", "ssd_a_log", "ssd_d",
               "ssd_norm", "norm_ffn")
WEIGHTS = ("norm_mix", "w_in", "s5_lam_re", "s5_lam_im", "s5_log_step", "s5_b_re", "s5_b_im", "s5_c_re", "s5_c_im",
           "s5_d", "s5_w_glu", "s5_b_glu", "s5_norm", "ssd_conv_w", "ssd_conv_b", "ssd_dt_bias", "ssd_a_log",
           "ssd_d", "ssd_norm", "w_out", "norm_ffn", "w_gate", "w_up", "w_down", "norm_final")


S5_BC = ("s5_b_re", "s5_b_im", "s5_c_re", "s5_c_im")
TINY = tuple(k for k in LAYER_SMALL if k not in S5_BC)


def _local_step(x, target, big, small, norm_final):
    saved = []
    h = x
    for i in range(DEPTH):
        p = {k: v[i] for k, v in small.items()}
        h, s = _layer_forward(h, p, big, i)
        saved.append((p, s))
    loss, dx, dxb, dgf = _final_loss(h, norm_final.reshape(1, -1), target, "final_loss")
    grads = [None] * DEPTH
    for i in reversed(range(DEPTH)):
        p, s = saved[i]
        dx, dxb, grads[i] = _layer_backward(dx, dxb, p, big, s, i)
    by_name = {k: [grads[i][k] for i in range(DEPTH)] for k in BIG + LAYER_SMALL}
    return loss[0, 0], dx, by_name, dgf[0]


def _my_place():
    return lax.axis_index("x"), lax.axis_index("y"), lax.axis_index("c")


def _all_gather8(blocks, name):
    nt = len(blocks)

    def body(*refs):
        ins = refs[:nt]
        outs = refs[nt:2 * nt]
        send_sems, recv_sems, local_sems = refs[2 * nt:]
        x, y, c = _my_place()
        me, sibling = (x, y, c), (x, y, 1 - c)
        chips = [(1 - x, y), (x, 1 - y), (1 - x, 1 - y)]

        def slot(t, place):
            px, py, pc = place
            return outs[t].at[4 * px + 2 * py + pc]

        def copy(t, k, block, to, src=None):
            return pltpu.make_async_remote_copy(
                src_ref=slot(t, block) if src is None else src, dst_ref=slot(t, block),
                send_sem=send_sems.at[t, k], recv_sem=recv_sems.at[t, k], device_id=to, device_id_type=MESH)

        mine = [pltpu.make_async_copy(ins[t], slot(t, me), local_sems.at[t]) for t in range(nt)]
        for cp in mine:
            cp.start()
        first = []
        for t in range(nt):
            first.append(copy(t, 0, me, sibling, src=ins[t]))
            first += [copy(t, 1 + j, me, (*chip, c), src=ins[t]) for j, chip in enumerate(chips)]
        for cp in first:
            cp.start()
        passed = []
        for j, chip in enumerate(chips):
            for t in range(nt):
                copy(t, 1 + j, (*chip, c), me).wait_recv()
                fwd = copy(t, 4 + j, (*chip, c), sibling)
                fwd.start()
                passed.append(fwd)
        for t in range(nt):
            copy(t, 0, sibling, me).wait_recv()
            for j, chip in enumerate(chips):
                copy(t, 4 + j, (*chip, 1 - c), me).wait_recv()
        for cp in first + passed:
            cp.wait_send()
        for cp in mine:
            cp.wait()

    return pl.pallas_call(
        body, name=name, in_specs=[ANY] * nt, out_specs=[ANY] * nt,
        out_shape=[jax.ShapeDtypeStruct((8,) + b.shape, b.dtype) for b in blocks],
        scratch_shapes=[pltpu.SemaphoreType.DMA((nt, 7)), pltpu.SemaphoreType.DMA((nt, 7)),
                        pltpu.SemaphoreType.DMA((nt,))],
    )(*blocks)


HBM = pl.BlockSpec(memory_space=pltpu.HBM)
SEM = pl.BlockSpec(memory_space=pltpu.SEMAPHORE)
DATAFLOW = pltpu.SideEffectType.DATAFLOW_SIDE_EFFECTING


def _in_hbm(a):
    return pltpu.with_memory_space_constraint(a, pltpu.HBM)


TOKEN = jax.ShapeDtypeStruct((8, LANES), F32)
VMEM_SPEC = pl.BlockSpec(memory_space=pltpu.VMEM)


def _gather_start(blocks, after, name):
    nt = len(blocks)

    def body(*refs):
        ins = refs[:nt]
        lands = refs[nt:2 * nt]
        send_sems, recv_sems = refs[2 * nt + 1:2 * nt + 3]
        refs[-1][...] = jnp.zeros(TOKEN.shape, F32)
        x, y, c = _my_place()
        me = 4 * x + 2 * y + c
        peers = [(x, y, 1 - c), (1 - x, y, c), (x, 1 - y, c), (1 - x, 1 - y, c)]
        for t in range(nt):
            for k, peer in enumerate(peers):
                pltpu.make_async_remote_copy(src_ref=ins[t], dst_ref=lands[t].at[me], send_sem=send_sems.at[4 * t + k],
                                             recv_sem=recv_sems.at[4 * t + k], device_id=peer,
                                             device_id_type=MESH).start()

    lands = [_in_hbm(lax.empty((8,) + b.shape, b.dtype)) for b in blocks]
    out = pl.pallas_call(
        body, name=name, in_specs=[HBM] * (2 * nt) + [ANY],
        out_shape=(pltpu.SemaphoreType.DMA((4 * nt,)), pltpu.SemaphoreType.DMA((4 * nt,)),
                   *[pltpu.HBM(b.shape, b.dtype) for b in blocks],
                   *[pltpu.HBM((8,) + b.shape, b.dtype) for b in blocks], TOKEN),
        out_specs=(SEM, SEM, *[HBM] * (2 * nt), VMEM_SPEC),
        input_output_aliases={i: 2 + i for i in range(2 * nt)},
        compiler_params=pltpu.CompilerParams(has_side_effects=DATAFLOW),
    )(*[_in_hbm(b) for b in blocks], *lands, after)
    return out[:2], list(out[2:2 + nt]), list(out[2 + nt:2 + 2 * nt]), out[-1]


def _gather_forward(sems, blocks, lands, after, name):
    nt = len(blocks)

    def body(*refs):
        ins = refs[:nt]
        lands_in = refs[nt:2 * nt]
        send1, recv1 = refs[2 * nt:2 * nt + 2]
        send2, recv2 = refs[2 * nt + 3:2 * nt + 5]
        x, y, c = _my_place()
        me = 4 * x + 2 * y + c
        sibling = (x, y, 1 - c)
        sources = [4 * x + 2 * y + (1 - c), 4 * (1 - x) + 2 * y + c, 4 * x + 2 * (1 - y) + c,
                   4 * (1 - x) + 2 * (1 - y) + c]
        for t in range(nt):
            for k, src in enumerate(sources):
                cp = pltpu.make_async_remote_copy(src_ref=ins[t], dst_ref=lands_in[t].at[src],
                                                  send_sem=send1.at[4 * t + k], recv_sem=recv1.at[4 * t + k],
                                                  device_id=sibling, device_id_type=MESH)
                cp.wait_send()
                cp.wait_recv()
            for k, src in enumerate(sources[1:]):
                pltpu.make_async_remote_copy(src_ref=lands_in[t].at[src], dst_ref=lands_in[t].at[src],
                                             send_sem=send2.at[3 * t + k], recv_sem=recv2.at[3 * t + k],
                                             device_id=sibling, device_id_type=MESH).start()

    out = pl.pallas_call(
        body, name=name, in_specs=[HBM] * (2 * nt) + [SEM, SEM, pl.BlockSpec(memory_space=pl.ANY)],
        out_shape=(pltpu.SemaphoreType.DMA((3 * nt,)), pltpu.SemaphoreType.DMA((3 * nt,)),
                   *[pltpu.HBM(b.shape, b.dtype) for b in blocks],
                   *[pltpu.HBM(a.shape, a.dtype) for a in lands]),
        out_specs=(SEM, SEM, *[HBM] * (2 * nt)),
        input_output_aliases={i: 2 + i for i in range(2 * nt)},
        compiler_params=pltpu.CompilerParams(has_side_effects=DATAFLOW),
    )(*blocks, *lands, *sems, after)
    return out[:2], list(out[2 + nt:])


def _gather_finish(sems, lands, after, name):
    nt = len(lands)

    def body(*refs):
        lands_in = refs[:nt]
        send2, recv2 = refs[nt:nt + 2]
        x, y, c = _my_place()
        sibling = (x, y, 1 - c)
        mine = [4 * (1 - x) + 2 * y + c, 4 * x + 2 * (1 - y) + c, 4 * (1 - x) + 2 * (1 - y) + c]
        theirs = [4 * (1 - x) + 2 * y + 1 - c, 4 * x + 2 * (1 - y) + 1 - c, 4 * (1 - x) + 2 * (1 - y) + 1 - c]
        for t in range(nt):
            for k in range(3):
                cp = pltpu.make_async_remote_copy(src_ref=lands_in[t].at[mine[k]], dst_ref=lands_in[t].at[theirs[k]],
                                                  send_sem=send2.at[3 * t + k], recv_sem=recv2.at[3 * t + k],
                                                  device_id=sibling, device_id_type=MESH)
                cp.wait_send()
                cp.wait_recv()

    out = pl.pallas_call(
        body, name=name, in_specs=[HBM] * nt + [SEM, SEM, pl.BlockSpec(memory_space=pl.ANY)],
        out_shape=tuple(pltpu.HBM(a.shape, a.dtype) for a in lands), out_specs=tuple([HBM] * nt),
        input_output_aliases={i: i for i in range(nt)},
        compiler_params=pltpu.CompilerParams(has_side_effects=DATAFLOW),
    )(*lands, *sems, after)
    return list(out)


def _other_chips():
    x, y, _ = _my_place()
    return [(1 - x, y), (x, 1 - y), (1 - x, 1 - y)]


def _scatter_start(chunks, name):
    nt = len(chunks)

    def body(*refs):
        ins = refs[:nt]
        lands = refs[nt:2 * nt]
        send_sems, recv_sems = refs[2 * nt:2 * nt + 2]
        refs[-1][...] = jnp.zeros(TOKEN.shape, F32)
        x, y, c = _my_place()
        for t in range(nt):
            for j, (px, py) in enumerate(_other_chips()):
                pltpu.make_async_remote_copy(src_ref=ins[t].at[2 * px + py], dst_ref=lands[t].at[2 * x + y],
                                             send_sem=send_sems.at[3 * t + j], recv_sem=recv_sems.at[3 * t + j],
                                             device_id=(px, py, c), device_id_type=MESH).start()

    lands = [_in_hbm(lax.empty(a.shape, a.dtype)) for a in chunks]
    out = pl.pallas_call(
        body, name=name, in_specs=[HBM] * (2 * nt),
        out_shape=(pltpu.SemaphoreType.DMA((3 * nt,)), pltpu.SemaphoreType.DMA((3 * nt,)),
                   *[pltpu.HBM(a.shape, a.dtype) for a in chunks] * 2, TOKEN),
        out_specs=(SEM, SEM, *[HBM] * (2 * nt), VMEM_SPEC),
        input_output_aliases={i: 2 + i for i in range(2 * nt)},
        compiler_params=pltpu.CompilerParams(has_side_effects=DATAFLOW),
    )(*[_in_hbm(a) for a in chunks], *lands)
    return out[:2], list(out[2:2 + nt]), list(out[2 + nt:2 + 2 * nt]), out[-1]


def _scatter_finish(sems, chunks, lands, after, name):
    nt = len(chunks)

    def body(*refs):
        ins = refs[:nt]
        lands_in = refs[nt:2 * nt]
        send_sems, recv_sems = refs[2 * nt:2 * nt + 2]
        _, _, c = _my_place()
        for t in range(nt):
            for j, (px, py) in enumerate(_other_chips()):
                cp = pltpu.make_async_remote_copy(src_ref=ins[t].at[2 * px + py], dst_ref=lands_in[t].at[2 * px + py],
                                                  send_sem=send_sems.at[3 * t + j], recv_sem=recv_sems.at[3 * t + j],
                                                  device_id=(px, py, c), device_id_type=MESH)
                cp.wait_send()
                cp.wait_recv()

    out = pl.pallas_call(
        body, name=name, in_specs=[HBM] * (2 * nt) + [SEM, SEM, ANY],
        out_shape=tuple(pltpu.HBM(a.shape, a.dtype) for a in lands), out_specs=tuple([HBM] * nt),
        input_output_aliases={nt + i: i for i in range(nt)},
        compiler_params=pltpu.CompilerParams(has_side_effects=DATAFLOW),
    )(*chunks, *lands, *sems, after)
    return list(out)


def _swap_halves(views, name):
    nt = len(views)

    def body(*refs):
        ins = refs[:nt]
        outs = refs[nt:2 * nt]
        send_sems, recv_sems = refs[2 * nt:]
        x, y, c = _my_place()
        copies = [pltpu.make_async_remote_copy(
            src_ref=ins[t].at[pl.ds(0, views[t].shape[0]), pl.ds(1 - c, 1)], dst_ref=outs[t],
            send_sem=send_sems.at[t], recv_sem=recv_sems.at[t], device_id=(x, y, 1 - c), device_id_type=MESH)
            for t in range(nt)]
        for cp in copies:
            cp.start()
        for cp in copies:
            cp.wait()

    return pl.pallas_call(
        body, name=name, in_specs=[ANY] * nt, out_specs=[ANY] * nt,
        out_shape=[jax.ShapeDtypeStruct((a.shape[0], 1) + a.shape[2:], a.dtype) for a in views],
        scratch_shapes=[pltpu.SemaphoreType.DMA((nt,)), pltpu.SemaphoreType.DMA((nt,))],
    )(*views)


def _pair_add_halves(view, recv, name):
    n, _, rows, cols = view.shape
    tile = _row_tile(rows, cols, 4)

    def body(a0_ref, a1_ref, r_ref, o_ref):
        mine = jnp.where(lax.axis_index("c") == 0, a0_ref[...], a1_ref[...])
        o_ref[...] = (mine.astype(F32) + r_ref[...].astype(F32)).astype(o_ref.dtype)

    half = lambda h: pl.BlockSpec((None, None, tile, cols), lambda p, i: (p, h, i, 0))
    return pl.pallas_call(
        body, name=name, grid=(n, rows // tile), in_specs=[half(0), half(1), half(0)],
        out_specs=pl.BlockSpec((None, tile, cols), lambda p, i: (p, i, 0)),
        out_shape=jax.ShapeDtypeStruct((n, rows, cols), view.dtype))(view, view, recv)


def _sum_chunks(lands, chunks, order, name):
    _, rows, cols = chunks.shape
    tile = _row_tile(rows, cols, 5)

    def body(order_ref, l0_ref, l1_ref, l2_ref, own_ref, o_ref):
        o_ref[...] = ((l0_ref[...].astype(F32) + l1_ref[...].astype(F32)) + l2_ref[...].astype(F32)
                      + own_ref[...].astype(F32))

    slot = lambda j: pl.BlockSpec((None, tile, cols), lambda i, order_ref: (order_ref[j], i, 0))
    grid_spec = pltpu.PrefetchScalarGridSpec(
        num_scalar_prefetch=1, grid=(rows // tile,), in_specs=[slot(0), slot(1), slot(2), slot(3)],
        out_specs=pl.BlockSpec((tile, cols), lambda i, order_ref: (i, 0)))
    return pl.pallas_call(body, name=name, grid_spec=grid_spec,
                          out_shape=jax.ShapeDtypeStruct((rows, cols), F32))(order, lands, lands, lands, chunks)


def _adamw_layer(w, g_mine, g_sibling, m, v, layer, prev, name):
    depth, rows, cols = w.shape
    half = rows // 2
    tile = _row_tile(half, cols, 10)
    tiles = half // tile

    def body(w_ref, gm_ref, gs_ref, m_ref, v_ref, *rest):
        d_ref, nm_ref, nv_ref, go_ref = rest[-4:]
        gv = jnp.where(pl.program_id(0) == lax.axis_index("c"), gm_ref[...], gs_ref[...])
        d_ref[...], nm_ref[...], nv_ref[...] = _adamw_math(w_ref[...], gv, m_ref[...], v_ref[...])
        go_ref[...] = gv

    spec = pl.BlockSpec((None, tile, cols), lambda h, i: (layer, h * tiles + i, 0))
    gspec = pl.BlockSpec((tile, cols), lambda h, i: (i, 0))
    shape = jax.ShapeDtypeStruct((depth, rows, cols), F32)
    extra = list(prev)
    aliases = {5 + j: j for j in range(4)} if len(extra) == 4 else {}
    return pl.pallas_call(
        body, name=name, grid=(2, tiles), in_specs=[spec, gspec, gspec, spec, spec] + [ANY] * len(extra),
        out_specs=[spec] * 4, out_shape=[shape] * 4, input_output_aliases=aliases)(w, g_mine, g_sibling, m, v, *extra)


def _sibling_swap_other(pairs, name):
    nt = len(pairs)

    def body(*refs):
        ins = refs[:2 * nt]
        outs = refs[2 * nt:3 * nt]
        send_sems, recv_sems = refs[3 * nt:]
        x, y, c = _my_place()

        def copy(t, src):
            return pltpu.make_async_remote_copy(src_ref=src, dst_ref=outs[t], send_sem=send_sems.at[t],
                                                recv_sem=recv_sems.at[t], device_id=(x, y, 1 - c), device_id_type=MESH)

        for t in range(nt):
            @pl.when(c == 0)
            def _():
                copy(t, ins[2 * t + 1]).start()

            @pl.when(c == 1)
            def _():
                copy(t, ins[2 * t]).start()
        for t in range(nt):
            copy(t, ins[2 * t]).wait()

    flat = [a for pair in pairs for a in pair]
    return pl.pallas_call(
        body, name=name, in_specs=[ANY] * (2 * nt), out_specs=[ANY] * nt,
        out_shape=[jax.ShapeDtypeStruct(a0.shape, a0.dtype) for a0, _ in pairs],
        scratch_shapes=[pltpu.SemaphoreType.DMA((nt,)), pltpu.SemaphoreType.DMA((nt,))],
    )(*flat)


def _sibling_swap(arrs, name):
    nt = len(arrs)

    def body(*refs):
        ins = refs[:nt]
        outs = refs[nt:2 * nt]
        send_sems, recv_sems = refs[2 * nt:]
        x, y, c = _my_place()
        copies = [pltpu.make_async_remote_copy(src_ref=ins[t], dst_ref=outs[t], send_sem=send_sems.at[t],
                                               recv_sem=recv_sems.at[t], device_id=(x, y, 1 - c), device_id_type=MESH)
                  for t in range(nt)]
        for cp in copies:
            cp.start()
        for cp in copies:
            cp.wait()

    return pl.pallas_call(
        body, name=name, in_specs=[ANY] * nt, out_specs=[ANY] * nt,
        out_shape=[jax.ShapeDtypeStruct(a.shape, a.dtype) for a in arrs],
        scratch_shapes=[pltpu.SemaphoreType.DMA((nt,)), pltpu.SemaphoreType.DMA((nt,))],
    )(*arrs)


def _chip_all_to_all(arrs, name):
    nt = len(arrs)

    def body(*refs):
        ins = refs[:nt]
        outs = refs[nt:2 * nt]
        send_sems, recv_sems, local_sems = refs[2 * nt:]
        x, y, c = _my_place()
        mine = 2 * x + y
        chips = [(1 - x, y), (x, 1 - y), (1 - x, 1 - y)]
        local = [pltpu.make_async_copy(ins[t].at[mine], outs[t].at[mine], local_sems.at[t]) for t in range(nt)]
        for cp in local:
            cp.start()
        sends = []
        for t in range(nt):
            for j, (px, py) in enumerate(chips):
                sends.append(pltpu.make_async_remote_copy(
                    src_ref=ins[t].at[2 * px + py], dst_ref=outs[t].at[mine], send_sem=send_sems.at[t, j],
                    recv_sem=recv_sems.at[t, j], device_id=(px, py, c), device_id_type=MESH))
        for cp in sends:
            cp.start()
        for t in range(nt):
            for j, (px, py) in enumerate(chips):
                pltpu.make_async_remote_copy(
                    src_ref=ins[t].at[mine], dst_ref=outs[t].at[2 * px + py], send_sem=send_sems.at[t, j],
                    recv_sem=recv_sems.at[t, j], device_id=(px, py, c), device_id_type=MESH).wait_recv()
        for cp in sends:
            cp.wait_send()
        for cp in local:
            cp.wait()

    return pl.pallas_call(
        body, name=name, in_specs=[ANY] * nt, out_specs=[ANY] * nt,
        out_shape=[jax.ShapeDtypeStruct(a.shape, a.dtype) for a in arrs],
        scratch_shapes=[pltpu.SemaphoreType.DMA((nt, 3)), pltpu.SemaphoreType.DMA((nt, 3)),
                        pltpu.SemaphoreType.DMA((nt,))],
    )(*arrs)


def _as_rows(a):
    return a.reshape(-1, a.shape[-1])


STREAM_VMEM_BYTES = 32 * 1024 * 1024
SUBLANES = 8


def _row_tile(rows, cols, n_arrays):
    lanes = -(-cols // LANES) * LANES
    for t in range(min(rows, 512), SUBLANES - 1, -1):
        if rows % t == 0 and t % SUBLANES == 0 and 2 * n_arrays * t * lanes * 4 <= STREAM_VMEM_BYTES:
            return t
    return rows


def _pair_add(a0, a1, recv, name):
    rows, cols = a0.shape
    tile = _row_tile(rows, cols, 4)

    def body(a0_ref, a1_ref, r_ref, o_ref):
        mine = jnp.where(lax.axis_index("c") == 0, a0_ref[...], a1_ref[...])
        o_ref[...] = (mine.astype(F32) + r_ref[...].astype(F32)).astype(o_ref.dtype)

    spec = pl.BlockSpec((tile, cols), lambda i: (i, 0))
    return pl.pallas_call(body, name=name, grid=(rows // tile,), in_specs=[spec] * 3, out_specs=spec,
                          out_shape=jax.ShapeDtypeStruct((rows, cols), a0.dtype))(a0, a1, recv)


def _sum_leading(a, name):
    n, rows, cols = a.shape
    tile = _row_tile(rows, cols, n + 1)

    def body(a_ref, o_ref):
        acc = a_ref[0].astype(F32)
        for k in range(1, n):
            acc = acc + a_ref[k].astype(F32)
        o_ref[...] = acc

    return pl.pallas_call(
        body, name=name, grid=(rows // tile,), in_specs=[pl.BlockSpec((n, tile, cols), lambda i: (0, i, 0))],
        out_specs=pl.BlockSpec((tile, cols), lambda i: (i, 0)),
        out_shape=jax.ShapeDtypeStruct((rows, cols), F32))(a)


def _adamw_math(w, g, m, v):
    mn = ADAM_B1 * m + (1.0 - ADAM_B1) * g
    vn = ADAM_B2 * v + (1.0 - ADAM_B2) * jnp.square(g)
    m_hat = mn / (1.0 - ADAM_B1 ** ADAM_STEP)
    v_hat = vn / (1.0 - ADAM_B2 ** ADAM_STEP)
    delta = -ADAM_LR * (m_hat / (jnp.sqrt(v_hat) + ADAM_EPS) + ADAM_WD * w)
    return delta, mn, vn


def _adamw_layers(w, g_mine, g_sibling, m, v, name):
    depth, rows, cols = w.shape
    tile = _row_tile(rows, cols, 10)

    def body(w_ref, gm_ref, gs_ref, m_ref, v_ref, d_ref, nm_ref, nv_ref, go_ref):
        gv = jnp.where(pl.program_id(0) == lax.axis_index("c"), gm_ref[...], gs_ref[...])
        d_ref[...], nm_ref[...], nv_ref[...] = _adamw_math(w_ref[...], gv, m_ref[...], v_ref[...])
        go_ref[...] = gv

    spec = pl.BlockSpec((None, tile, cols), lambda l, i: (l, i, 0))
    gspec = pl.BlockSpec((tile, cols), lambda l, i: (i, 0))
    shape = jax.ShapeDtypeStruct((depth, rows, cols), F32)
    return pl.pallas_call(body, name=name, grid=(depth, rows // tile), in_specs=[spec, gspec, gspec, spec, spec],
                          out_specs=[spec] * 4, out_shape=[shape] * 4)(w, g_mine, g_sibling, m, v)


def _adamw_rows(w, g, m, v, name):
    depth, rows, cols = w.shape
    tile = _row_tile(rows, cols, 7)

    def body(w_ref, g_ref, m_ref, v_ref, d_ref, nm_ref, nv_ref):
        d_ref[...], nm_ref[...], nv_ref[...] = _adamw_math(w_ref[...], g_ref[...], m_ref[...], v_ref[...])

    spec = pl.BlockSpec((None, tile, cols), lambda l, i: (l, i, 0))
    shape = jax.ShapeDtypeStruct((depth, rows, cols), F32)
    return pl.pallas_call(body, name=name, grid=(depth, rows // tile), in_specs=[spec] * 4, out_specs=[spec] * 3,
                          out_shape=[shape] * 3)(w, g, m, v)


def _adamw_many(ws, gs, ms, vs, name):
    nt = len(ws)

    def body(*refs):
        for t in range(nt):
            w_ref, g_ref, m_ref, v_ref = (refs[k * nt + t] for k in range(4))
            d_ref, nm_ref, nv_ref = (refs[(4 + k) * nt + t] for k in range(3))
            d_ref[...], nm_ref[...], nv_ref[...] = _adamw_math(w_ref[...], g_ref[...], m_ref[...], v_ref[...])

    shapes = [jax.ShapeDtypeStruct(a.shape, F32) for a in ws]
    out = pl.pallas_call(body, name=name, out_shape=shapes * 3)(*ws, *gs, *ms, *vs)
    return out[:nt], out[nt:2 * nt], out[2 * nt:]


TINY_ROWS_MULTIPLE = 128


def _flat_pack(arrs):
    flat = jnp.concatenate([a.reshape(-1) for a in arrs])
    pad = (-flat.shape[0]) % (TINY_ROWS_MULTIPLE * LANES)
    return jnp.pad(flat, (0, pad)).reshape(-1, LANES)


def _flat_unpack(buf, shapes):
    flat = buf.reshape(-1)
    out = []
    off = 0
    for shp in shapes:
        n = math.prod(shp)
        out.append(flat[off:off + n].reshape(shp))
        off += n
    return out


def _to_chunks(a, name):
    if name == "w_in":
        a = _unpad_in_proj(a)
    rows, cols = a.shape
    if name in COL_SHARDED:
        return a.reshape(rows, 4, cols // 4).transpose(1, 0, 2)
    return a.reshape(4, rows // 4, cols)


def _from_chunks(a, name):
    _, depth, r, cc = a.shape
    if name in COL_SHARDED:
        return a.transpose(1, 2, 0, 3).reshape(depth, r, 4 * cc)
    return a.transpose(1, 0, 2, 3).reshape(depth, 4 * r, cc)


def kernel(x, norm_mix, w_in, s5_lam_re, s5_lam_im, s5_log_step, s5_b_re, s5_b_im, s5_c_re, s5_c_im, s5_d, s5_w_glu, s5_b_glu, s5_norm, ssd_conv_w, ssd_conv_b, ssd_dt_bias, ssd_a_log, ssd_d, ssd_norm, w_out, norm_ffn, w_gate, w_up, w_down, norm_final, loss_target, m_norm_mix, m_w_in, m_s5_lam_re, m_s5_lam_im, m_s5_log_step, m_s5_b_re, m_s5_b_im, m_s5_c_re, m_s5_c_im, m_s5_d, m_s5_w_glu, m_s5_b_glu, m_s5_norm, m_ssd_conv_w, m_ssd_conv_b, m_ssd_dt_bias, m_ssd_a_log, m_ssd_d, m_ssd_norm, m_w_out, m_norm_ffn, m_w_gate, m_w_up, m_w_down, m_norm_final, v_norm_mix, v_w_in, v_s5_lam_re, v_s5_lam_im, v_s5_log_step, v_s5_b_re, v_s5_b_im, v_s5_c_re, v_s5_c_im, v_s5_d, v_s5_w_glu, v_s5_b_glu, v_s5_norm, v_ssd_conv_w, v_ssd_conv_b, v_ssd_dt_bias, v_ssd_a_log, v_ssd_d, v_ssd_norm, v_w_out, v_norm_ffn, v_w_gate, v_w_up, v_w_down, v_norm_final):
    args = dict(locals())
    w = {k: args[k] for k in WEIGHTS}
    m = {k: args["m_" + k] for k in WEIGHTS}
    v = {k: args["v_" + k] for k in WEIGHTS}
    cx, cy, cc = _my_place()
    chip = 2 * cx + cy

    me = 4 * cx + 2 * cy + cc
    others = _other_chips()
    chunk_order = jnp.stack([2 * px + py for px, py in others] + [chip]).astype(jnp.int32)
    stored = lambda k, a: jnp.swapaxes(a, 1, 2) if k in T_STORED else a

    def my_half(k, layer):
        a = stored(k, w[k])[layer]
        return lax.dynamic_slice_in_dim(a, cc * (a.shape[0] // 2), a.shape[0] // 2, 0).astype(BF16)

    def assemble(names, lands, blocks):
        full = {}
        for k, a, b in zip(names, lands, blocks):
            a = lax.dynamic_update_index_in_dim(a, b, me, 0)
            a = a.reshape(4, 2 * a.shape[1], a.shape[2])
            if k in COL_SHARDED:
                full[k] = _pad_in_proj(a.transpose(1, 0, 2).reshape(a.shape[1], 4 * a.shape[2]))
            else:
                full[k] = a.reshape(4 * a.shape[1], a.shape[2])
        return full

    conv_block = w["ssd_conv_w"].reshape(DEPTH * SSD_CONV, -1)
    first = [my_half(k, 0) for k in MIXER_BIG] + [conv_block]
    ffn0 = [my_half(k, 0) for k in FFN_BIG]
    blocks1 = [my_half(k, 1) for k in BIG]
    sems_a, kept_a, lands_a, token = _gather_start(first, x, "gather0a_start")
    sems_b, kept_b, lands_b, token = _gather_start(ffn0, token, "gather0b_start")
    sems1, kept1, lands1, token = _gather_start(blocks1, token, "gather1_start")
    sems_a, lands_a = _gather_forward(sems_a, kept_a, lands_a, token, "gather0a_forward")
    lands_a = _gather_finish(sems_a, lands_a, token, "gather0a_finish")
    big0 = assemble(MIXER_BIG, lands_a, first)
    conv_rows = lax.dynamic_update_index_in_dim(lands_a[-1], conv_block, me, 0)
    conv_full = conv_rows.reshape(4, 2, DEPTH, SSD_CONV, -1)[:, 0].transpose(1, 2, 0, 3).reshape(
        DEPTH, SSD_CONV, SSD_CONV_DIM)
    small = {k: w[k] for k in LAYER_SMALL}
    small["ssd_conv_w"] = conv_full
    p0 = {k: a[0] for k, a in small.items()}
    p1 = {k: a[1] for k, a in small.items()}

    p0["norm_mix"] = p0["norm_mix"] + token[0, 0]
    pending = {}

    def pass_on_ffn0(u):
        pending["ffn0"] = _gather_forward(sems_b, kept_b, lands_b, u, "gather0b_forward")

    def ffn0_matrices(x1):
        sems, lands = pending["ffn0"]
        lands = _gather_finish(sems, lands, x1, "gather0b_finish")
        pending["layer1"] = _gather_forward(sems1, kept1, lands1, lands[0], "gather1_forward")
        return assemble(FFN_BIG, lands, ffn0)

    h1, saved0 = _layer_forward(x[0], p0, big0, 0, pass_on_ffn0, ffn0_matrices)
    big0 = {**big0, **saved0["ffn_matrices"]}
    sems1, lands1 = pending["layer1"]
    lands1 = _gather_finish(sems1, lands1, h1, "gather1_finish")
    big1 = assemble(BIG, lands1, blocks1)
    h2, saved1 = _layer_forward(h1, p1, big1, 1)
    loss_row, dx, dxb, g_final = _final_loss(h2, w["norm_final"].reshape(1, -1), loss_target[0], "final_loss")
    loss_part, g_final = loss_row[0, 0], g_final[0]

    def halves_view(k, a):
        if k in COL_SHARDED:
            return a.reshape(1, 2, a.shape[0] // 2, a.shape[1])
        return a.reshape(4, 2, a.shape[0] // 8, a.shape[1])

    def to_chunks(k, part):
        if k in COL_SHARDED:
            a = _unpad_in_proj(part[0])
            return a.reshape(a.shape[0], 4, a.shape[1] // 4).transpose(1, 0, 2)
        return part.reshape(4, -1, part.shape[-1])

    def reduce_begin(names, views, tag):
        recv = _swap_halves(views, tag + "swap")
        parts = [_pair_add_halves(a, r, tag + "pair_" + k) for k, a, r in zip(names, views, recv)]
        chunks = [to_chunks(k, p) for k, p in zip(names, parts)]
        return _scatter_start(chunks, tag + "scatter_start")

    def reduce_end(names, handle, after, tag):
        sems, kept, lands, _ = handle
        lands = _scatter_finish(sems, kept, lands, after, tag + "scatter_finish")
        return [_sum_chunks(a, b, chunk_order, tag + "sum_" + k) for k, a, b in zip(names, lands, kept)]

    dx, dxb, g1 = _layer_backward(dx, dxb, p1, big1, saved1, 1)
    round1 = reduce_begin(BIG, [halves_view(k, g1[k]) for k in BIG], "grad1_")
    p0["norm_ffn"] = p0["norm_ffn"] + round1[3][0, 0]

    early = FFN_BIG + ("w_out",)
    middle = ("s5_w_glu", "s5_bc")
    bc_rows = 2 * DEPTH * S5_GROUP * S5_GROUPS

    def send_early(g_so_far):
        pending["early"] = reduce_begin(early, [halves_view(k, g_so_far[k]) for k in early], "grad0a_")
        return pending["early"][3]

    def send_middle(g_so_far):
        rows = lambda names: jnp.stack([a for layer in (g_so_far, g1) for a in (layer[names[0]], layer[names[1]])]
                                       ).reshape(bc_rows, S5_STATE)
        bc = jnp.stack([rows(("s5_b_re", "s5_b_im")), rows(("s5_c_re", "s5_c_im"))])[None]
        pending["middle"] = reduce_begin(middle, [halves_view("s5_w_glu", g_so_far["s5_w_glu"]), bc], "grad0b_")
        return pending["middle"][3]

    grad_x, _, g0 = _layer_backward(dx, dxb, p0, big0, saved0, 0, send_early, send_middle)
    g = {k: [g0[k], g1[k]] for k in LAYER_SMALL}
    reduced1 = dict(zip(BIG, reduce_end(BIG, round1, grad_x, "grad1_")))
    shared1 = dict(zip(BIG, _sibling_swap([reduced1[k] for k in BIG], "grad1_share")))
    round0 = reduce_begin(("w_in",), [halves_view("w_in", g0["w_in"])], "grad0c_")

    delta, new_m, new_v, grads = {}, {}, {}, {}
    adam1 = {}
    layered = tuple(k for k in BIG if k not in COL_SHARDED)
    for k in layered:
        adam1[k] = _adamw_layer(stored(k, w[k]), reduced1[k], shared1[k], stored(k, m[k]), stored(k, v[k]), 1,
                                [round0[3]], "adamw1_" + k)
    follow = adam1[layered[-1]][0]
    reduced0 = dict(zip(early, reduce_end(early, pending["early"], follow, "grad0a_")))
    reduced0.update(zip(middle, reduce_end(middle, pending["middle"], follow, "grad0b_")))
    reduced0.update(zip(("w_in",), reduce_end(("w_in",), round0, follow, "grad0c_")))
    shared0 = dict(zip(BIG, _sibling_swap([reduced0[k] for k in BIG], "grad0_share")))
    for k in layered:
        outs = _adamw_layer(stored(k, w[k]), reduced0[k], shared0[k], stored(k, m[k]), stored(k, v[k]), 0, adam1[k],
                            "adamw0_" + k)
        delta[k], new_m[k], new_v[k], grads[k] = (stored(k, a) for a in outs)
    both = lambda mine, sib: jnp.where(cc == 0, jnp.concatenate([mine, sib]), jnp.concatenate([sib, mine]))
    grads["w_in"] = jnp.stack([both(reduced0["w_in"], shared0["w_in"]), both(reduced1["w_in"], shared1["w_in"])])
    outs = _adamw_rows(*[_lane_dense(a)[None] for a in (w["w_in"], grads["w_in"], m["w_in"], v["w_in"])],
                       "adamw_w_in")
    delta["w_in"], new_m["w_in"], new_v["w_in"] = (_from_lane_dense(a[0]) for a in outs)
    reduced = [reduced0["s5_bc"]]

    tiny_names = TINY + ("norm_final",)
    parts = [jnp.stack(g[k]) for k in TINY] + [g_final, loss_part.reshape(1)]
    shapes = [p.shape for p in parts]
    allparts, bc_eighths = _all_gather8([_flat_pack(parts), reduced[-1]], "gather_small")
    unpacked = _flat_unpack(_sum_leading(allparts, "sum_small"), shapes)
    loss = unpacked[-1][0]
    grads.update(zip(tiny_names, unpacked[:-1]))
    width = SSD_CONV_DIM // 4
    grads["ssd_conv_w"] = lax.dynamic_slice_in_dim(grads["ssd_conv_w"], chip * width, width, axis=2)
    bc = bc_eighths.reshape(4, 2, bc_rows // 4, S5_STATE)
    b_sum = bc[:, 0].reshape(DEPTH, 2, S5_GROUP, S5_GROUPS, S5_STATE)
    c_sum = bc[:, 1].reshape(DEPTH, 2, S5_GROUPS, S5_GROUP, S5_STATE)
    grads["s5_c_re"] = c_sum[:, 0]
    grads["s5_c_im"] = c_sum[:, 1]

    b_names = ("s5_b_re", "s5_b_im")
    hp = lambda a: a.transpose(0, 1, 3, 2)
    names = tiny_names + ("s5_c_re", "s5_c_im") + b_names
    view = lambda k, a: hp(a) if k in b_names else (a.reshape(1, -1) if a.ndim == 1 else a)
    g_view = {k: view(k, grads[k]) for k in names if k not in b_names}
    g_view.update({k: b_sum[:, j].transpose(0, 2, 1, 3) for j, k in enumerate(b_names)})
    ds, nms, nvs = _adamw_many([view(k, w[k]) for k in names], [g_view[k] for k in names],
                               [view(k, m[k]) for k in names], [view(k, v[k]) for k in names], "adamw_small")
    for k, a, b, c in zip(names, ds, nms, nvs):
        if k in b_names:
            delta[k], new_m[k], new_v[k], grads[k] = hp(a), hp(b), hp(c), hp(g_view[k])
        else:
            delta[k], new_m[k], new_v[k] = (t.reshape(w[k].shape) for t in (a, b, c))

    return (loss, grad_x[None], *[grads[k] for k in WEIGHTS], *[delta[k] for k in WEIGHTS],
            *[new_m[k] for k in WEIGHTS], *[new_v[k] for k in WEIGHTS])
```

```python
import functools
import math

import jax
import jax.numpy as jnp
from jax import lax
from jax.experimental import pallas as pl
from jax.experimental.pallas import tpu as pltpu

F32 = jnp.float32
BF16 = jnp.bfloat16
MESH = pl.DeviceIdType.MESH
ANY = pl.BlockSpec(memory_space=pl.ANY)

D_MODEL = 1024
DEPTH = 2
S5_GROUPS = 64
S5_GROUP = 16
S5_STATE = 64
S5_COLS = S5_GROUPS * S5_STATE
S5_TILE_GROUPS = 8
S5_TILES = S5_GROUPS // S5_TILE_GROUPS
S5_TILE_IN = S5_TILE_GROUPS * S5_GROUP
S5_TILE_ST = S5_TILE_GROUPS * S5_STATE
SEGS = 8
SSD_HEADS = 16
SSD_HEAD_DIM = 64
SSD_GROUPS = 2
SSD_GROUP_HEADS = SSD_HEADS // SSD_GROUPS
SSD_STATE = 128
SSD_CONV = 4
SSD_CHUNK = 128
SSD_WIDTH = 1024
SSD_CONV_DIM = SSD_WIDTH + 2 * SSD_GROUPS * SSD_STATE
IN_PROJ = 3600
IN_MAIN = 3584
IN_PAD = IN_MAIN + 2 * 128
FFN = 2816
EPS = 1e-6
LANES = 128
ROW_TILE = 256
WIDE_ROW_TILE = 512

ADAM_LR = 0.001
ADAM_B1 = 0.9
ADAM_B2 = 0.999
ADAM_EPS = 1e-08
ADAM_WD = 0.01
ADAM_STEP = 10


def _sigmoid(x):
    return 1.0 / (1.0 + jnp.exp(-x))


def _silu(x):
    return x * _sigmoid(x)


def _dsilu(x):
    s = _sigmoid(x)
    return s * (1.0 + x * (1.0 - s))


_GELU_K = math.sqrt(2.0 / math.pi)
_GELU_C = 0.044715


def _gelu(x):
    t = jnp.tanh(_GELU_K * (x + _GELU_C * x * x * x))
    return 0.5 * x * (1.0 + t)


def _dgelu(x):
    t = jnp.tanh(_GELU_K * (x + _GELU_C * x * x * x))
    return 0.5 * (1.0 + t) + 0.5 * x * (1.0 - t * t) * _GELU_K * (1.0 + 3.0 * _GELU_C * x * x)


def _softplus(x):
    e = jnp.exp(-jnp.abs(x))
    u = 1.0 + e
    log1p = jnp.where(u == 1.0, e, jnp.log(u) * e / jnp.where(u == 1.0, 1.0, u - 1.0))
    return jnp.maximum(x, 0.0) + log1p


def _rstd(x):
    return lax.rsqrt(jnp.mean(x * x, axis=-1, keepdims=True) + EPS)


def _rms_bwd(x, r, gain, dy):
    dyg = dy * gain
    dx = r * dyg - x * (r * r * r) * jnp.mean(x * dyg, axis=-1, keepdims=True)
    dgain = jnp.sum(dy * x * r, axis=0, keepdims=True)
    return dx, dgain


def _dot(a, b):
    return jnp.dot(a, b, preferred_element_type=F32)


def _dot_nt(a, b):
    return lax.dot_general(a, b, (((1,), (1,)), ((), ())), preferred_element_type=F32)


def _dot_tn(a, b):
    return lax.dot_general(a, b, (((0,), (0,)), ((), ())), preferred_element_type=F32)


def _row_spec(tile, cols):
    return pl.BlockSpec((tile, cols), lambda i: (i, 0))


def _full_spec(shape):
    nd = len(shape)
    return pl.BlockSpec(shape, lambda *_: (0,) * nd)


def _const_spec(shape):
    nd = len(shape)
    return pl.BlockSpec(shape, lambda *_: (0,) * nd, pipeline_mode=pl.Buffered(1))


def _layer_spec(shape, layer, block=0):
    if layer is None:
        return pl.BlockSpec(tuple(shape), lambda *_: (block, 0), pipeline_mode=pl.Buffered(1))
    return pl.BlockSpec((None,) + tuple(shape), lambda *_: (layer, block, 0), pipeline_mode=pl.Buffered(1))


def _acc_rows(ref, val, first):
    @pl.when(first)
    def _():
        ref[...] = val

    @pl.when(jnp.logical_not(first))
    def _():
        ref[...] += val


def _pick_tile(n, cap):
    best = LANES
    for t in range(LANES, cap + 1, LANES):
        if n % t == 0:
            best = t
    return best


def _mm_tn(a, b, name):
    k, m = a.shape
    _, n = b.shape
    tm = _pick_tile(m, 512)
    tn = _pick_tile(n, 1536)

    def body(a_ref, b_ref, o_ref):
        o_ref[...] = _dot_tn(a_ref[...], b_ref[...]).astype(BF16)

    return pl.pallas_call(
        body, name=name, grid=(n // tn, m // tm),
        in_specs=[pl.BlockSpec((k, tm), lambda j, i: (0, i)), pl.BlockSpec((k, tn), lambda j, i: (0, j))],
        out_specs=pl.BlockSpec((tm, tn), lambda j, i: (i, j)),
        out_shape=jax.ShapeDtypeStruct((m, n), BF16),
    )(a, b)


def _rms_inproj(x, gain, w_pad, layer, name):
    L = x.shape[0]

    def body(x_ref, g_ref, w_ref, u_ref, z_ref, xbc_ref, dt_ref, h_ref):
        xv = x_ref[...]
        h = (xv * _rstd(xv) * g_ref[...]).astype(BF16)
        h_ref[...] = h
        p = _dot(h, w_ref[...])
        u_ref[...] = p[:, :1024]
        z_ref[...] = p[:, 1024:2048]
        xbc_ref[...] = p[:, 2048:IN_MAIN]
        dt_ref[...] = p[:, IN_MAIN:IN_PAD]

    tile = min(L, WIDE_ROW_TILE)
    return pl.pallas_call(
        body, name=name, grid=(L // tile,),
        in_specs=[_row_spec(tile, D_MODEL), _full_spec((1, D_MODEL)), _layer_spec((D_MODEL, IN_PAD), layer)],
        out_specs=[_row_spec(tile, 1024), _row_spec(tile, 1024), _row_spec(tile, SSD_CONV_DIM),
                   _row_spec(tile, 256), _row_spec(tile, D_MODEL)],
        out_shape=[jax.ShapeDtypeStruct((L, 1024), F32), jax.ShapeDtypeStruct((L, 1024), F32),
                   jax.ShapeDtypeStruct((L, SSD_CONV_DIM), F32), jax.ShapeDtypeStruct((L, 256), F32),
                   jax.ShapeDtypeStruct((L, D_MODEL), BF16)],
    )(x, gain, w_pad)


def _s5_prep_math(lr, li, ls, bre, bim):
    step = jnp.exp(ls)
    mag = jnp.exp(lr * step)
    ang = li * step
    are = mag * jnp.cos(ang)
    aim = mag * jnp.sin(ang)
    den = lr * lr + li * li
    nr = are - 1.0
    ni = aim
    cre = (nr * lr + ni * li) / den
    cim = (ni * lr - nr * li) / den
    bbre = cre[None] * bre - cim[None] * bim
    bbim = cre[None] * bim + cim[None] * bre
    return are, aim, bbre, bbim


def _s5_prep(lr, li, ls, bre, bim, name):
    def body(lr_ref, li_ref, ls_ref, bre_ref, bim_ref, are_ref, aim_ref, bbre_ref, bbim_ref):
        are, aim, bbre, bbim = _s5_prep_math(lr_ref[...], li_ref[...], ls_ref[...], bre_ref[...], bim_ref[...])
        are_ref[...] = are
        aim_ref[...] = aim
        bbre_ref[...] = bbre
        bbim_ref[...] = bbim

    gp = jax.ShapeDtypeStruct((S5_GROUPS, S5_STATE), F32)
    hgp = jax.ShapeDtypeStruct((S5_GROUP, S5_GROUPS, S5_STATE), F32)
    return pl.pallas_call(body, name=name, out_shape=[gp, gp, hgp, hgp])(lr, li, ls, bre, bim)


def _s5_prep_bwd(lr, li, ls, bre, bim, dare, daim, dbbre, dbbim, name):
    def body(lr_ref, li_ref, ls_ref, bre_ref, bim_ref, dare_ref, daim_ref, dbbre_ref, dbbim_ref,
             dlr_ref, dli_ref, dls_ref, dbre_ref, dbim_ref):
        _, vjp = jax.vjp(_s5_prep_math, lr_ref[...], li_ref[...], ls_ref[...], bre_ref[...], bim_ref[...])
        dlr, dli, dls, dbre, dbim = vjp((dare_ref[...], daim_ref[...], dbbre_ref[...], dbbim_ref[...]))
        dlr_ref[...] = dlr
        dli_ref[...] = dli
        dls_ref[...] = dls
        dbre_ref[...] = dbre
        dbim_ref[...] = dbim

    gp = jax.ShapeDtypeStruct((S5_GROUPS, S5_STATE), F32)
    g1 = jax.ShapeDtypeStruct((S5_GROUPS, 1), F32)
    hgp = jax.ShapeDtypeStruct((S5_GROUP, S5_GROUPS, S5_STATE), F32)
    return pl.pallas_call(body, name=name, out_shape=[gp, gp, g1, hgp, hgp])(
        lr, li, ls, bre, bim, dare, daim, dbbre, dbbim)


def _cmul_add(ar, ai, sr, si, br, bi):
    return ar * sr - ai * si + br, ar * si + ai * sr + bi


def _shift_rows_down(v):
    rolled = pltpu.roll(v, 1, 0)
    row = lax.broadcasted_iota(jnp.int32, v.shape, 0)
    return jnp.where(row == 0, 0.0, rolled)


def _shift_rows_up(v):
    rolled = pltpu.roll(v, SEGS - 1, 0)
    row = lax.broadcasted_iota(jnp.int32, v.shape, 0)
    return jnp.where(row == SEGS - 1, 0.0, rolled)


def _segment_power(ar, ai, steps):
    n = 1
    while n < steps:
        ar, ai = ar * ar - ai * ai, 2.0 * ar * ai
        n *= 2
    assert n == steps
    return ar, ai


def _half_segment_entries(ar, ai, first, second, half_steps, shift):
    pr, pi = _segment_power(ar, ai, half_steps)
    er = jnp.zeros_like(first[0])
    ei = jnp.zeros_like(first[1])
    for _ in range(SEGS - 1):
        mr, mi = _cmul_add(pr, pi, er, ei, *first)
        nr, ni = _cmul_add(pr, pi, mr, mi, *second)
        er, ei = shift(nr), shift(ni)
    mr, mi = _cmul_add(pr, pi, er, ei, *first)
    return (er, ei), (mr, mi)


def _s5_scan(u_perm, bre_bd, bim_bd, cre_bd, cim_bd, are, aim, name):
    L = u_perm.shape[0]
    half = L // SEGS // 2

    def body(u_ref, bre_ref, bim_ref, cre_ref, cim_ref, are_ref, aim_ref, y_ref, xr_ref, xi_ref):
        u = u_ref[...].astype(BF16)
        xr_ref[...] = _dot(u, bre_ref[0])
        xi_ref[...] = _dot(u, bim_ref[0])
        ar = jnp.broadcast_to(are_ref[0], (SEGS, S5_TILE_ST))
        ai = jnp.broadcast_to(aim_ref[0], (SEGS, S5_TILE_ST))
        zero = jnp.zeros((SEGS, S5_TILE_ST), F32)
        block = lambda j: pl.ds(pl.multiple_of(j * SEGS, SEGS), SEGS)

        def finals(j, c):
            lo, hi = block(j), block(j + half)
            return (*_cmul_add(ar, ai, c[0], c[1], xr_ref[lo, :], xi_ref[lo, :]),
                    *_cmul_add(ar, ai, c[2], c[3], xr_ref[hi, :], xi_ref[hi, :]))

        f = lax.fori_loop(0, half, finals, (zero,) * 4, unroll=4)
        e_lo, e_hi = _half_segment_entries(ar, ai, f[:2], f[2:], half, _shift_rows_down)

        def scan(j, c):
            lo, hi = block(j), block(j + half)
            s_lo = _cmul_add(ar, ai, c[0], c[1], xr_ref[lo, :], xi_ref[lo, :])
            s_hi = _cmul_add(ar, ai, c[2], c[3], xr_ref[hi, :], xi_ref[hi, :])
            xr_ref[lo, :], xi_ref[lo, :] = s_lo
            xr_ref[hi, :], xi_ref[hi, :] = s_hi
            return (*s_lo, *s_hi)

        lax.fori_loop(0, half, scan, (*e_lo, *e_hi), unroll=4)
        y_ref[...] = (_dot(xr_ref[...].astype(BF16), cre_ref[0]) - _dot(xi_ref[...].astype(BF16), cim_ref[0]))

    tile3 = lambda a, b: pl.BlockSpec((1, a, b), lambda k: (k, 0, 0))
    return pl.pallas_call(
        body, name=name, grid=(S5_TILES,),
        in_specs=[pl.BlockSpec((L, S5_TILE_IN), lambda k: (0, k)),
                  tile3(S5_TILE_IN, S5_TILE_ST), tile3(S5_TILE_IN, S5_TILE_ST),
                  tile3(S5_TILE_ST, S5_TILE_IN), tile3(S5_TILE_ST, S5_TILE_IN),
                  tile3(1, S5_TILE_ST), tile3(1, S5_TILE_ST)],
        out_specs=[pl.BlockSpec((L, S5_TILE_IN), lambda k: (0, k)),
                   pl.BlockSpec((L, S5_TILE_ST), lambda k: (0, k)), pl.BlockSpec((L, S5_TILE_ST), lambda k: (0, k))],
        out_shape=[jax.ShapeDtypeStruct((L, 1024), F32), jax.ShapeDtypeStruct((L, S5_COLS), F32),
                   jax.ShapeDtypeStruct((L, S5_COLS), F32)],
    )(u_perm, bre_bd, bim_bd, cre_bd, cim_bd, are, aim)


def _s5_scan_bwd(dy_perm, u_perm, xr, xi, bret_bd, bimt_bd, cret_bd, cimt_bd, are, aim, name):
    L = u_perm.shape[0]
    steps = L // SEGS
    half = steps // 2

    def body(dy_ref, u_ref, xr_ref, xi_ref, bret_ref, bimt_ref, cret_ref, cimt_ref, are_ref, aim_ref,
             du_ref, dar_ref, dai_ref, dcre_ref, dcim_ref, dbre_ref, dbim_ref, gr_ref, gi_ref):
        dy = dy_ref[...].astype(BF16)
        u = u_ref[...].astype(BF16)
        gr_ref[...] = _dot(dy, cret_ref[0])
        gi_ref[...] = -_dot(dy, cimt_ref[0])
        ar = jnp.broadcast_to(are_ref[0], (SEGS, S5_TILE_ST))
        ai = -jnp.broadcast_to(aim_ref[0], (SEGS, S5_TILE_ST))
        zero = jnp.zeros((SEGS, S5_TILE_ST), F32)
        block = lambda j: pl.ds(pl.multiple_of(j * SEGS, SEGS), SEGS)

        def finals(k, c):
            hi, lo = block(steps - 1 - k), block(half - 1 - k)
            return (*_cmul_add(ar, ai, c[0], c[1], gr_ref[hi, :], gi_ref[hi, :]),
                    *_cmul_add(ar, ai, c[2], c[3], gr_ref[lo, :], gi_ref[lo, :]))

        f = lax.fori_loop(0, half, finals, (zero,) * 4, unroll=4)
        e_hi, e_lo = _half_segment_entries(ar, ai, f[:2], f[2:], half, _shift_rows_up)

        def scan(k, c):
            accr, acci = c[4], c[5]
            j_hi, j_lo = steps - 1 - k, half - 1 - k
            hi, lo = block(j_hi), block(j_lo)
            hr, hi_im = _cmul_add(ar, ai, c[0], c[1], gr_ref[hi, :], gi_ref[hi, :])
            lr, lo_im = _cmul_add(ar, ai, c[2], c[3], gr_ref[lo, :], gi_ref[lo, :])
            gr_ref[hi, :], gi_ref[hi, :] = hr, hi_im
            gr_ref[lo, :], gi_ref[lo, :] = lr, lo_im
            before_hi = block(j_hi - 1)
            before_lo = block(jnp.maximum(j_lo - 1, 0))
            live = (j_lo > 0).astype(F32)
            xhr, xhi = xr_ref[before_hi, :], xi_ref[before_hi, :]
            xlr, xli = xr_ref[before_lo, :] * live, xi_ref[before_lo, :] * live
            accr = accr + (hr * xhr + hi_im * xhi) + (lr * xlr + lo_im * xli)
            acci = acci + (hi_im * xhr - hr * xhi) + (lo_im * xlr - lr * xli)
            return hr, hi_im, lr, lo_im, accr, acci

        out = lax.fori_loop(0, half, scan, (*e_hi, *e_lo, zero, zero), unroll=2)
        accr, acci = out[4], out[5]
        first = pl.ds(0, SEGS)
        last = pl.ds((steps - 1) * SEGS, SEGS)
        xpr = _shift_rows_down(xr_ref[last, :])
        xpi = _shift_rows_down(xi_ref[last, :])
        g0r = gr_ref[first, :]
        g0i = gi_ref[first, :]
        accr = accr + g0r * xpr + g0i * xpi
        acci = acci + g0i * xpr - g0r * xpi
        dar_ref[0] = jnp.sum(accr, axis=0, keepdims=True)
        dai_ref[0] = jnp.sum(acci, axis=0, keepdims=True)

        grb = gr_ref[...].astype(BF16)
        gib = gi_ref[...].astype(BF16)
        du_ref[...] = _dot(grb, bret_ref[0]) + _dot(gib, bimt_ref[0])
        dbre_ref[0] = _dot_tn(u, grb)
        dbim_ref[0] = _dot_tn(u, gib)
        dcre_ref[0] = _dot_tn(dy, xr_ref[...].astype(BF16))
        dcim_ref[0] = -_dot_tn(dy, xi_ref[...].astype(BF16))

    tile3 = lambda a, b: pl.BlockSpec((1, a, b), lambda k: (k, 0, 0))
    col_in = pl.BlockSpec((L, S5_TILE_IN), lambda k: (0, k))
    col_st = pl.BlockSpec((L, S5_TILE_ST), lambda k: (0, k))
    dense = jax.ShapeDtypeStruct((S5_TILES, S5_TILE_IN, S5_TILE_ST), F32)
    vec = jax.ShapeDtypeStruct((S5_TILES, 1, S5_TILE_ST), F32)
    return pl.pallas_call(
        body, name=name, grid=(S5_TILES,),
        in_specs=[col_in, col_in, col_st, col_st,
                  tile3(S5_TILE_ST, S5_TILE_IN), tile3(S5_TILE_ST, S5_TILE_IN),
                  tile3(S5_TILE_IN, S5_TILE_ST), tile3(S5_TILE_IN, S5_TILE_ST),
                  tile3(1, S5_TILE_ST), tile3(1, S5_TILE_ST)],
        out_specs=[col_in, tile3(1, S5_TILE_ST), tile3(1, S5_TILE_ST),
                   tile3(S5_TILE_IN, S5_TILE_ST), tile3(S5_TILE_IN, S5_TILE_ST),
                   tile3(S5_TILE_IN, S5_TILE_ST), tile3(S5_TILE_IN, S5_TILE_ST)],
        out_shape=[jax.ShapeDtypeStruct((L, 1024), F32), vec, vec, dense, dense, dense, dense],
        scratch_shapes=[pltpu.VMEM((L, S5_TILE_ST), F32), pltpu.VMEM((L, S5_TILE_ST), F32)],
    )(dy_perm, u_perm, xr, xi, bret_bd, bimt_bd, cret_bd, cimt_bd, are, aim)


def _s5_post(ys, u, d_skip, w_glu, b_glu, gain, layer, name):
    L = ys.shape[0]

    def body(ys_ref, u_ref, d_ref, w_ref, b_ref, g_ref, ya_ref):
        g = _gelu(ys_ref[...] + d_ref[...] * u_ref[...])
        q = _dot(g.astype(BF16), w_ref[...]) + b_ref[...]
        oa = g * _sigmoid(q)
        ya_ref[...] = (oa * _rstd(oa) * g_ref[...]).astype(BF16)

    vec = _full_spec((1, 1024))
    tile = min(L, WIDE_ROW_TILE)
    return pl.pallas_call(
        body, name=name, grid=(L // tile,),
        in_specs=[_row_spec(tile, 1024), _row_spec(tile, 1024), vec, _layer_spec((1024, 1024), layer), vec, vec],
        out_specs=_row_spec(tile, 1024),
        out_shape=jax.ShapeDtypeStruct((L, 1024), BF16),
    )(ys, u, d_skip, w_glu, b_glu, gain)


def _s5_post_bwd(dx, w_out, ys, u, d_skip, w_glu, b_glu, gain, layer, name):
    L = ys.shape[0]

    def body(dx_ref, wo_ref, ys_ref, u_ref, d_ref, w_ref, b_ref, gn_ref,
             dys_ref, dus_ref, g_ref, dq_ref, dgain_ref, dd_ref, db_ref):
        first = pl.program_id(0) == 0
        uv = u_ref[...]
        yt = ys_ref[...] + d_ref[...] * uv
        g = _gelu(yt)
        gb = g.astype(BF16)
        q = _dot(gb, w_ref[...]) + b_ref[...]
        s = _sigmoid(q)
        oa = g * s
        dya = _dot_nt(dx_ref[...], wo_ref[...])
        doa, dgain = _rms_bwd(oa, _rstd(oa), gn_ref[...], dya)
        dq = doa * g * s * (1.0 - s)
        dqb = dq.astype(BF16)
        dg = doa * s + _dot_nt(dqb, w_ref[...])
        dyt = dg * _dgelu(yt)
        dys_ref[...] = dyt
        dus_ref[...] = dyt * d_ref[...]
        g_ref[...] = gb
        dq_ref[...] = dqb
        _acc_rows(dgain_ref, dgain, first)
        _acc_rows(dd_ref, jnp.sum(dyt * uv, axis=0, keepdims=True), first)
        _acc_rows(db_ref, jnp.sum(dq, axis=0, keepdims=True), first)

    vec = _full_spec((1, 1024))
    row = _row_spec(ROW_TILE, 1024)
    vshape = jax.ShapeDtypeStruct((1, 1024), F32)
    return pl.pallas_call(
        body, name=name, grid=(L // ROW_TILE,),
        in_specs=[row, _layer_spec((1024, 1024), layer, 0), row, row, vec, _layer_spec((1024, 1024), layer), vec,
                  vec],
        out_specs=[row, row, row, row, vec, vec, vec],
        out_shape=[jax.ShapeDtypeStruct((L, 1024), F32), jax.ShapeDtypeStruct((L, 1024), F32),
                   jax.ShapeDtypeStruct((L, 1024), BF16), jax.ShapeDtypeStruct((L, 1024), BF16),
                   vshape, vshape, vshape],
    )(dx, w_out, ys, u, d_skip, w_glu, b_glu, gain)


CONV_TILE = 256


def _shift_time(v, d):
    if d == 0:
        return v
    rolled = pltpu.roll(v, d, 0)
    row = lax.broadcasted_iota(jnp.int32, v.shape, 0)
    return jnp.where(row < d, 0.0, rolled)


def _unshift_time(v, d):
    if d == 0:
        return v
    n = v.shape[0]
    rolled = pltpu.roll(v, n - d, 0)
    row = lax.broadcasted_iota(jnp.int32, v.shape, 0)
    return jnp.where(row >= n - d, 0.0, rolled)


def _ssd_conv(xbc, w, b, name):
    L = xbc.shape[0]

    def body(x_ref, w_ref, b_ref, o_ref):
        xv = x_ref[...]
        pre = jnp.broadcast_to(b_ref[...], xv.shape)
        for k in range(SSD_CONV):
            pre = pre + w_ref[k:k + 1, :] * _shift_time(xv, SSD_CONV - 1 - k)
        o_ref[...] = _silu(pre)

    col = pl.BlockSpec((L, CONV_TILE), lambda j: (0, j))
    return pl.pallas_call(
        body, name=name, grid=(SSD_CONV_DIM // CONV_TILE,),
        in_specs=[col, pl.BlockSpec((8, CONV_TILE), lambda j: (0, j)), pl.BlockSpec((1, CONV_TILE), lambda j: (0, j))],
        out_specs=col, out_shape=jax.ShapeDtypeStruct((L, SSD_CONV_DIM), F32),
    )(xbc, w, b)


def _ssd_conv_bwd(dxc, xbc, w, b, name):
    L = xbc.shape[0]

    def body(d_ref, x_ref, w_ref, b_ref, dx_ref, dw_ref, db_ref):
        xv = x_ref[...]
        shifted = [_shift_time(xv, SSD_CONV - 1 - k) for k in range(SSD_CONV)]
        pre = jnp.broadcast_to(b_ref[...], xv.shape)
        for k in range(SSD_CONV):
            pre = pre + w_ref[k:k + 1, :] * shifted[k]
        dpre = d_ref[...] * _dsilu(pre)
        dx = jnp.zeros_like(xv)
        rows = []
        for k in range(SSD_CONV):
            dx = dx + w_ref[k:k + 1, :] * _unshift_time(dpre, SSD_CONV - 1 - k)
            rows.append(jnp.sum(dpre * shifted[k], axis=0, keepdims=True))
        dx_ref[...] = dx
        dw_ref[...] = jnp.concatenate(rows + [jnp.zeros((8 - SSD_CONV, CONV_TILE), F32)], axis=0)
        db_ref[...] = jnp.sum(dpre, axis=0, keepdims=True)

    col = pl.BlockSpec((L, CONV_TILE), lambda j: (0, j))
    w_spec = pl.BlockSpec((8, CONV_TILE), lambda j: (0, j))
    b_spec = pl.BlockSpec((1, CONV_TILE), lambda j: (0, j))
    return pl.pallas_call(
        body, name=name, grid=(SSD_CONV_DIM // CONV_TILE,),
        in_specs=[col, col, w_spec, b_spec], out_specs=[col, w_spec, b_spec],
        out_shape=[jax.ShapeDtypeStruct((L, SSD_CONV_DIM), F32), jax.ShapeDtypeStruct((8, SSD_CONV_DIM), F32),
                   jax.ShapeDtypeStruct((1, SSD_CONV_DIM), F32)],
    )(dxc, xbc, w, b)


def _tri(lower):
    r = lax.broadcasted_iota(jnp.int32, (SSD_CHUNK, SSD_CHUNK), 0)
    c = lax.broadcasted_iota(jnp.int32, (SSD_CHUNK, SSD_CHUNK), 1)
    return (r >= c) if lower else (r <= c)


def _ssd_chunk_common(dt_ref, bias_ref, alog_ref):
    pre = dt_ref[...] + bias_ref[0]
    dtp = _softplus(pre)
    a_neg = -jnp.exp(alog_ref[0])
    dta = dtp * a_neg
    acum = _select_rows(_tri(True), dta)
    return pre, dtp, a_neg, dta, acum


GROUP_W = SSD_GROUP_HEADS * SSD_HEAD_DIM


def _head_expander():
    r = lax.broadcasted_iota(jnp.int32, (LANES, GROUP_W), 0)
    c = lax.broadcasted_iota(jnp.int32, (LANES, GROUP_W), 1)
    return (c // SSD_HEAD_DIM == r).astype(F32)


def _split_bf16(a, terms):
    parts = []
    rest = a
    for _ in range(terms):
        piece = rest.astype(BF16)
        parts.append(piece)
        rest = rest - piece.astype(F32)
    return parts


def _select_cols(a, sel, terms=3):
    lhs = jnp.concatenate(_split_bf16(a, terms), axis=1)
    rhs = jnp.concatenate([sel.astype(BF16)] * terms, axis=0)
    return _dot(lhs, rhs)


def _select_rows(sel, b, terms=3):
    lhs = jnp.concatenate([sel.astype(BF16)] * terms, axis=1)
    rhs = jnp.concatenate(_split_bf16(b, terms), axis=0)
    return _dot(lhs, rhs)


def _decay_mask(acum_all, acum_t, h, lower):
    seg = acum_all[:, h:h + 1] - acum_t[h:h + 1, :]
    return jnp.where(lower, jnp.exp(jnp.minimum(seg, 0.0)), 0.0)


def _ssd_scan(xc, dt, dt_bias, a_log, d_wide, expand, expand_t, name):
    L = xc.shape[0]
    nc = L // SSD_CHUNK

    def body(x_ref, b_ref, c_ref, dt_ref, bias_ref, alog_ref, d_ref, e_ref, et_ref, y_ref, sp_ref, s_ref, xdt_ref):
        @pl.when(pl.program_id(1) == 0)
        def _():
            s_ref[...] = jnp.zeros_like(s_ref)

        _, dtp_all, _, _, acum_all = _ssd_chunk_common(dt_ref, bias_ref, alog_ref)
        acum_t = acum_all.T
        wide = _select_cols(jnp.concatenate([acum_all, dtp_all], axis=0), e_ref[...])
        acum_e = wide[:SSD_CHUNK]
        alast_e = acum_e[SSD_CHUNK - 1:SSD_CHUNK, :]
        x = x_ref[...]
        xdt = x * wide[SSD_CHUNK:]
        xdt_ref[...] = xdt.astype(BF16)
        bm = b_ref[...].astype(BF16)
        cm = c_ref[...].astype(BF16)
        cb = _dot_nt(cm, bm)
        lower = _tri(True)
        sp = s_ref[...]
        for h in range(SSD_GROUP_HEADS):
            cols = slice(h * SSD_HEAD_DIM, (h + 1) * SSD_HEAD_DIM)
            lm = _decay_mask(acum_all, acum_t, h, lower)
            y_ref[:, cols] = _dot((cb * lm).astype(BF16), xdt_ref[:, cols])
        y_ref[...] += _dot_nt(cm, sp.astype(BF16)) * jnp.exp(acum_e) + d_ref[0] * x
        wgt = xdt * jnp.exp(alast_e - acum_e)
        ealast = jnp.exp(_select_rows(et_ref[...], acum_t)[:, SSD_CHUNK - 1:SSD_CHUNK])
        sp_ref[0, 0] = sp
        s_ref[...] = ealast * sp + _dot_tn(wgt.astype(BF16), bm)

    par = lambda n: pl.BlockSpec((1, 1, n), lambda g, c: (g, 0, 0))
    return pl.pallas_call(
        body, name=name, grid=(SSD_GROUPS, nc),
        in_specs=[pl.BlockSpec((SSD_CHUNK, GROUP_W), lambda g, c: (c, g)),
                  pl.BlockSpec((SSD_CHUNK, SSD_STATE), lambda g, c: (c, 8 + g)),
                  pl.BlockSpec((SSD_CHUNK, SSD_STATE), lambda g, c: (c, 10 + g)),
                  pl.BlockSpec((SSD_CHUNK, LANES), lambda g, c: (c, g)),
                  par(LANES), par(LANES), par(GROUP_W), _full_spec((LANES, GROUP_W)), _full_spec((GROUP_W, LANES))],
        out_specs=[pl.BlockSpec((SSD_CHUNK, GROUP_W), lambda g, c: (c, g)),
                   pl.BlockSpec((1, 1, GROUP_W, SSD_STATE), lambda g, c: (c, g, 0, 0))],
        out_shape=[jax.ShapeDtypeStruct((L, SSD_WIDTH), F32),
                   jax.ShapeDtypeStruct((nc, SSD_GROUPS, GROUP_W, SSD_STATE), F32)],
        scratch_shapes=[pltpu.VMEM((GROUP_W, SSD_STATE), F32), pltpu.VMEM((SSD_CHUNK, GROUP_W), BF16)],
    )(xc, xc, xc, dt, dt_bias, a_log, d_wide, expand, expand_t)


def _ssd_scan_bwd(dy, xc, dt, sprev, dt_bias, a_log, d_wide, expand, expand_t, name):
    L = xc.shape[0]
    nc = L // SSD_CHUNK

    def body(dy_ref, x_ref, b_ref, c_ref, dt_ref, sp_ref, bias_ref, alog_ref, d_ref, e_ref, et_ref,
             dx_ref, db_ref, dc_ref, ddt_ref, dbias_ref, dalog_ref, dd_ref, ds_ref, xdt_ref, dyb_ref):
        first = pl.program_id(1) == 0

        @pl.when(first)
        def _():
            ds_ref[...] = jnp.zeros_like(ds_ref)

        pre, dtp_all, a_neg, _, acum_all = _ssd_chunk_common(dt_ref, bias_ref, alog_ref)
        acum_t = acum_all.T
        e = e_ref[...]
        et = et_ref[...]
        wide = _select_cols(jnp.concatenate([acum_all, dtp_all], axis=0), e)
        acum_e = wide[:SSD_CHUNK]
        dtp_e = wide[SSD_CHUNK:]
        alast_e = acum_e[SSD_CHUNK - 1:SSD_CHUNK, :]
        dstate_e = jnp.exp(alast_e - acum_e)
        x = x_ref[...]
        dy = dy_ref[...]
        xdt = x * dtp_e
        xdt_ref[...] = xdt.astype(BF16)
        dyb_ref[...] = dy.astype(BF16)
        bm = b_ref[...].astype(BF16)
        cm = c_ref[...].astype(BF16)
        cb = _dot_nt(cm, bm)
        sp = sp_ref[0, 0]
        spb = sp.astype(BF16)
        dsn = ds_ref[...]
        dsb = dsn.astype(BF16)
        z = _dot_nt(cm, spb)
        dz = dy * jnp.exp(acum_e)
        dzb = dz.astype(BF16)
        dc_acc = _dot(dzb, spb)
        ealast = jnp.exp(_select_rows(et, acum_t)[:, SSD_CHUNK - 1:SSD_CHUNK])
        ds_ref[...] = _dot_tn(dzb, cm) + ealast * dsn
        dw = _dot_nt(bm, dsb)
        wgt = xdt * dstate_e
        db_acc = _dot(wgt.astype(BF16), dsb)
        lower = _tri(True)
        lane = lax.broadcasted_iota(jnp.int32, (SSD_CHUNK, LANES), 1)
        row = lax.broadcasted_iota(jnp.int32, (SSD_CHUNK, LANES), 0)
        dcb = jnp.zeros((SSD_CHUNK, SSD_CHUNK), F32)
        dacum_all = jnp.zeros((SSD_CHUNK, LANES), F32)
        dacum_cols = jnp.zeros((SSD_CHUNK, LANES), F32)
        for h in range(SSD_GROUP_HEADS):
            cols = slice(h * SSD_HEAD_DIM, (h + 1) * SSD_HEAD_DIM)
            lm = _decay_mask(acum_all, acum_t, h, lower)
            dm = _dot_nt(dyb_ref[:, cols], xdt_ref[:, cols])
            dx_ref[:, cols] = _dot_tn((cb * lm).astype(BF16), dyb_ref[:, cols])
            dm_lm = dm * lm
            dcb = dcb + dm_lm
            q = dm_lm * cb
            dacum_all = jnp.where(lane == h, jnp.sum(q, axis=1, keepdims=True), dacum_all)
            dacum_cols = jnp.where(row == h, jnp.sum(q, axis=0, keepdims=True), dacum_cols)
        dxdt = dx_ref[...] + dw * dstate_e
        sums = _select_cols(jnp.concatenate([dz * z, dw * wgt, dxdt * x, dy * x], axis=0), et, terms=2)
        dacum_off = sums[0:SSD_CHUNK]
        dds_ds = sums[SSD_CHUNK:2 * SSD_CHUNK]
        ddtp_x = sums[2 * SSD_CHUNK:3 * SSD_CHUNK]
        dd_part = sums[3 * SSD_CHUNK:4 * SSD_CHUNK]
        ds_s = jnp.sum(_select_rows(e, dsn * sp, terms=2).T, axis=0, keepdims=True)
        dalast = ds_s * jnp.exp(acum_all[SSD_CHUNK - 1:SSD_CHUNK, :]) + jnp.sum(dds_ds, axis=0, keepdims=True)
        dacum_all = dacum_all - dacum_cols.T + dacum_off - dds_ds + jnp.where(row == SSD_CHUNK - 1, dalast, 0.0)
        dx_ref[...] = d_ref[0] * dy + dxdt * dtp_e
        dcbb = dcb.astype(BF16)
        dc_ref[...] = dc_acc + _dot(dcbb, bm)
        db_ref[...] = db_acc + _dot_tn(dcbb, cm)
        ddta = _select_rows(_tri(False), dacum_all)
        ddt = (ddtp_x + ddta * a_neg) * _sigmoid(pre)
        ddt_ref[...] = ddt
        _acc_rows(dbias_ref, jnp.sum(ddt, axis=0, keepdims=True)[None], first)
        _acc_rows(dalog_ref, (jnp.sum(ddta * dtp_all, axis=0, keepdims=True) * a_neg)[None], first)
        _acc_rows(dd_ref, jnp.sum(dd_part, axis=0, keepdims=True)[None], first)

    rev = lambda c: nc - 1 - c
    par = lambda n: pl.BlockSpec((1, 1, n), lambda g, c: (g, 0, 0))
    pshape = jax.ShapeDtypeStruct((SSD_GROUPS, 1, LANES), F32)
    return pl.pallas_call(
        body, name=name, grid=(SSD_GROUPS, nc),
        in_specs=[pl.BlockSpec((SSD_CHUNK, GROUP_W), lambda g, c: (rev(c), g)),
                  pl.BlockSpec((SSD_CHUNK, GROUP_W), lambda g, c: (rev(c), g)),
                  pl.BlockSpec((SSD_CHUNK, SSD_STATE), lambda g, c: (rev(c), 8 + g)),
                  pl.BlockSpec((SSD_CHUNK, SSD_STATE), lambda g, c: (rev(c), 10 + g)),
                  pl.BlockSpec((SSD_CHUNK, LANES), lambda g, c: (rev(c), g)),
                  pl.BlockSpec((1, 1, GROUP_W, SSD_STATE), lambda g, c: (rev(c), g, 0, 0)),
                  par(LANES), par(LANES), par(GROUP_W), _full_spec((LANES, GROUP_W)), _full_spec((GROUP_W, LANES))],
        out_specs=[pl.BlockSpec((SSD_CHUNK, GROUP_W), lambda g, c: (rev(c), g)),
                   pl.BlockSpec((SSD_CHUNK, SSD_STATE), lambda g, c: (rev(c), g)),
                   pl.BlockSpec((SSD_CHUNK, SSD_STATE), lambda g, c: (rev(c), g)),
                   pl.BlockSpec((SSD_CHUNK, LANES), lambda g, c: (rev(c), g)),
                   par(LANES), par(LANES), par(LANES)],
        out_shape=[jax.ShapeDtypeStruct((L, SSD_WIDTH), F32), jax.ShapeDtypeStruct((L, 256), F32),
                   jax.ShapeDtypeStruct((L, 256), F32), jax.ShapeDtypeStruct((L, 256), F32),
                   pshape, pshape, pshape],
        scratch_shapes=[pltpu.VMEM((GROUP_W, SSD_STATE), F32), pltpu.VMEM((SSD_CHUNK, GROUP_W), BF16),
                        pltpu.VMEM((SSD_CHUNK, GROUP_W), BF16)],
    )(dy, xc, xc, xc, dt, sprev, dt_bias, a_log, d_wide, expand, expand_t)


def _ssd_post(y, z, gain, name):
    L = y.shape[0]

    def body(y_ref, z_ref, g_ref, o_ref):
        ob = y_ref[...] * _silu(z_ref[...])
        o_ref[...] = (ob * _rstd(ob) * g_ref[...]).astype(BF16)

    tile = min(L, WIDE_ROW_TILE)
    row = _row_spec(tile, 1024)
    return pl.pallas_call(body, name=name, grid=(L // tile,), in_specs=[row, row, _full_spec((1, 1024))],
                          out_specs=row, out_shape=jax.ShapeDtypeStruct((L, 1024), BF16))(y, z, gain)


def _ssd_post_bwd(dx, w_out, y, z, gain, layer, name):
    L = y.shape[0]

    def body(dx_ref, wo_ref, y_ref, z_ref, g_ref, dy_ref, dz_ref, dgain_ref):
        first = pl.program_id(0) == 0
        yv = y_ref[...]
        zv = z_ref[...]
        sz = _silu(zv)
        ob = yv * sz
        dyb = _dot_nt(dx_ref[...], wo_ref[...])
        dob, dgain = _rms_bwd(ob, _rstd(ob), g_ref[...], dyb)
        dy_ref[...] = dob * sz
        dz_ref[...] = dob * yv * _dsilu(zv)
        _acc_rows(dgain_ref, dgain, first)

    row = _row_spec(ROW_TILE, 1024)
    vec = _full_spec((1, 1024))
    return pl.pallas_call(
        body, name=name, grid=(L // ROW_TILE,),
        in_specs=[row, _layer_spec((1024, 1024), layer, 1), row, row, vec],
        out_specs=[row, row, vec],
        out_shape=[jax.ShapeDtypeStruct((L, 1024), F32), jax.ShapeDtypeStruct((L, 1024), F32),
                   jax.ShapeDtypeStruct((1, 1024), F32)],
    )(dx, w_out, y, z, gain)


def _out_proj(x, ya, yb, w_out, layer, name):
    L = x.shape[0]

    def body(x_ref, ya_ref, yb_ref, w_ref, o_ref):
        o_ref[...] = x_ref[...] + _dot(ya_ref[...], w_ref[:1024, :]) + _dot(yb_ref[...], w_ref[1024:, :])

    tile = min(L, WIDE_ROW_TILE)
    row = _row_spec(tile, 1024)
    return pl.pallas_call(body, name=name, grid=(L // tile,),
                          in_specs=[row, row, row, _layer_spec((2048, 1024), layer)],
                          out_specs=row, out_shape=jax.ShapeDtypeStruct((L, D_MODEL), F32))(x, ya, yb, w_out)


def _ffn(x, gain, w_gate, w_up, w_down, layer, name):
    L = x.shape[0]

    def body(x_ref, g_ref, wg_ref, wu_ref, wd_ref, o_ref, gt_ref, up_ref):
        xv = x_ref[...]
        h = (xv * _rstd(xv) * g_ref[...]).astype(BF16)
        gt = _dot_nt(h, wg_ref[...])
        up = _dot_nt(h, wu_ref[...])
        gt_ref[...] = gt
        up_ref[...] = up
        o_ref[...] = xv + _dot((_silu(gt) * up).astype(BF16), wd_ref[...])

    row = _row_spec(ROW_TILE, D_MODEL)
    hid = _row_spec(ROW_TILE, FFN)
    return pl.pallas_call(
        body, name=name, grid=(L // ROW_TILE,),
        in_specs=[row, _full_spec((1, D_MODEL)), _layer_spec((FFN, D_MODEL), layer),
                  _layer_spec((FFN, D_MODEL), layer), _layer_spec((FFN, D_MODEL), layer)],
        out_specs=[row, hid, hid],
        out_shape=[jax.ShapeDtypeStruct((L, D_MODEL), F32), jax.ShapeDtypeStruct((L, FFN), F32),
                   jax.ShapeDtypeStruct((L, FFN), F32)],
    )(x, gain, w_gate, w_up, w_down)


def _ffn_bwd(dx2, x1, gt, up, gain, w_gate, w_up, w_down, layer, name):
    L = x1.shape[0]

    def body(d_ref, x_ref, gt_ref, up_ref, g_ref, wg_ref, wu_ref, wd_ref,
             dx_ref, dxb_ref, h_ref, act_ref, dgt_ref, dup_ref, dgain_ref):
        first = pl.program_id(0) == 0
        dv = d_ref[...]
        xv = x_ref[...]
        r = _rstd(xv)
        h_ref[...] = (xv * r * g_ref[...]).astype(BF16)
        gtv = gt_ref[...]
        upv = up_ref[...]
        sg = _silu(gtv)
        act_ref[...] = (sg * upv).astype(BF16)
        dact = _dot_nt(dv.astype(BF16), wd_ref[...])
        dgt = (dact * upv * _dsilu(gtv)).astype(BF16)
        dup = (dact * sg).astype(BF16)
        dgt_ref[...] = dgt
        dup_ref[...] = dup
        dh = _dot(dgt, wg_ref[...]) + _dot(dup, wu_ref[...])
        dxn, dgain = _rms_bwd(xv, r, g_ref[...], dh)
        dx = dv + dxn
        dx_ref[...] = dx
        dxb_ref[...] = dx.astype(BF16)
        _acc_rows(dgain_ref, dgain, first)

    row = _row_spec(ROW_TILE, D_MODEL)
    hid = _row_spec(ROW_TILE, FFN)
    vec = _full_spec((1, D_MODEL))
    return pl.pallas_call(
        body, name=name, grid=(L // ROW_TILE,),
        in_specs=[row, row, hid, hid, vec, _layer_spec((FFN, D_MODEL), layer), _layer_spec((FFN, D_MODEL), layer),
                  _layer_spec((FFN, D_MODEL), layer)],
        out_specs=[row, row, row, hid, hid, hid, vec],
        out_shape=[jax.ShapeDtypeStruct((L, D_MODEL), F32), jax.ShapeDtypeStruct((L, D_MODEL), BF16),
                   jax.ShapeDtypeStruct((L, D_MODEL), BF16),
                   jax.ShapeDtypeStruct((L, FFN), BF16), jax.ShapeDtypeStruct((L, FFN), BF16),
                   jax.ShapeDtypeStruct((L, FFN), BF16), jax.ShapeDtypeStruct((1, D_MODEL), F32)],
    )(dx2, x1, gt, up, gain, w_gate, w_up, w_down)


def _inproj_bwd(dx1, x0, du_skip, du_scan, dz, dxbc, ddt, gain, w_pad, layer, name):
    L = x0.shape[0]

    def body(d_ref, x_ref, dus_ref, duc_ref, dz_ref, dxbc_ref, ddt_ref, g_ref, w_ref,
             dx_ref, dxb_ref, dp_ref, dgain_ref):
        first = pl.program_id(0) == 0
        xv = x_ref[...]
        dp = jnp.concatenate([dus_ref[...] + duc_ref[...], dz_ref[...], dxbc_ref[...], ddt_ref[...]],
                             axis=1).astype(BF16)
        dp_ref[...] = dp
        dh = _dot_nt(dp, w_ref[...])
        dxn, dgain = _rms_bwd(xv, _rstd(xv), g_ref[...], dh)
        dx = d_ref[...] + dxn
        dx_ref[...] = dx
        dxb_ref[...] = dx.astype(BF16)
        _acc_rows(dgain_ref, dgain, first)

    row = _row_spec(ROW_TILE, D_MODEL)
    vec = _full_spec((1, D_MODEL))
    return pl.pallas_call(
        body, name=name, grid=(L // ROW_TILE,),
        in_specs=[row, row, row, row, row, _row_spec(ROW_TILE, SSD_CONV_DIM), _row_spec(ROW_TILE, 256), vec,
                  _layer_spec((D_MODEL, IN_PAD), layer)],
        out_specs=[row, row, _row_spec(ROW_TILE, IN_PAD), vec],
        out_shape=[jax.ShapeDtypeStruct((L, D_MODEL), F32), jax.ShapeDtypeStruct((L, D_MODEL), BF16),
                   jax.ShapeDtypeStruct((L, IN_PAD), BF16), jax.ShapeDtypeStruct((1, D_MODEL), F32)],
    )(dx1, x0, du_skip, du_scan, dz, dxbc, ddt, gain, w_pad)


def _final_loss(x, gain, target, name):
    L = x.shape[0]

    def body(x_ref, g_ref, t_ref, loss_ref, dx_ref, dxb_ref, dgain_ref):
        first = pl.program_id(0) == 0
        xv = x_ref[...]
        r = _rstd(xv)
        err = xv * r * g_ref[...] - t_ref[...]
        part = 0.5 * jnp.sum(jnp.mean(err * err, axis=-1, keepdims=True), axis=0, keepdims=True)
        dx, dgain = _rms_bwd(xv, r, g_ref[...], err * (1.0 / D_MODEL))
        dx_ref[...] = dx
        dxb_ref[...] = dx.astype(BF16)
        _acc_rows(loss_ref, jnp.broadcast_to(part, (1, LANES)), first)
        _acc_rows(dgain_ref, dgain, first)

    row = _row_spec(ROW_TILE, D_MODEL)
    vec = _full_spec((1, D_MODEL))
    return pl.pallas_call(
        body, name=name, grid=(L // ROW_TILE,), in_specs=[row, vec, row],
        out_specs=[_full_spec((1, LANES)), row, row, vec],
        out_shape=[jax.ShapeDtypeStruct((1, LANES), F32), jax.ShapeDtypeStruct((L, D_MODEL), F32),
                   jax.ShapeDtypeStruct((L, D_MODEL), BF16), jax.ShapeDtypeStruct((1, D_MODEL), F32)],
    )(x, gain, target)


def _to_segments(a):
    L, n = a.shape
    return a.reshape(SEGS, L // SEGS, n).transpose(1, 0, 2).reshape(L, n)


def _from_segments(a):
    L, n = a.shape
    return a.reshape(L // SEGS, SEGS, n).transpose(1, 0, 2).reshape(L, n)


def _diag_block(g):
    k, a = divmod(g, S5_TILE_GROUPS)
    return k, slice(a * S5_GROUP, (a + 1) * S5_GROUP), slice(a * S5_STATE, (a + 1) * S5_STATE)


def _block_diag_build(mats, name):
    n = mats.shape[0]

    def body(m_ref, o_ref):
        o_ref[...] = jnp.zeros(o_ref.shape, BF16)
        for q in range(n):
            for g in range(S5_GROUPS):
                k, rows, cols = _diag_block(g)
                o_ref[q, k, rows, cols] = m_ref[q, g].astype(BF16)

    return pl.pallas_call(body, name=name,
                          out_shape=jax.ShapeDtypeStruct((n, S5_TILES, S5_TILE_IN, S5_TILE_ST), BF16))(mats)


def _block_diag_extract(dense, name):
    n = len(dense)

    def body(*refs):
        o_ref = refs[n]
        for q in range(n):
            for g in range(S5_GROUPS):
                k, rows, cols = _diag_block(g)
                o_ref[q, g] = refs[q][k, rows, cols]

    return pl.pallas_call(body, name=name,
                          out_shape=jax.ShapeDtypeStruct((n, S5_GROUPS, S5_GROUP, S5_STATE), F32))(*dense)


def _pad_in_proj(w):
    z = jnp.zeros(w.shape[:-1] + (LANES - SSD_GROUP_HEADS,), w.dtype)
    return jnp.concatenate([w[..., :IN_MAIN + 8], z, w[..., IN_MAIN + 8:], z], axis=-1)


def _unpad_in_proj(w):
    return jnp.concatenate([w[..., :IN_MAIN + 8], w[..., IN_MAIN + LANES:IN_MAIN + LANES + 8]], axis=-1)


def _lane_dense(a):
    return a.reshape(DEPTH, D_MODEL // LANES, LANES, -1).transpose(3, 1, 0, 2).reshape(-1, LANES)


def _from_lane_dense(a):
    return a.reshape(-1, D_MODEL // LANES, DEPTH, LANES).transpose(2, 1, 3, 0).reshape(DEPTH, D_MODEL, -1)


def _pad_heads(v):
    v = v.reshape(SSD_GROUPS, 1, SSD_GROUP_HEADS)
    return jnp.pad(v, ((0, 0), (0, 0), (0, LANES - SSD_GROUP_HEADS)))


def _unpad_heads(v):
    return v[:, 0, :SSD_GROUP_HEADS].reshape(SSD_HEADS)


def _layer_forward(x0, p, big, i, after_inproj=None, before_s5_post=None, before_ffn=None):
    tag = "l%d_" % i
    ls = p["s5_log_step"].reshape(S5_GROUPS, 1)
    b_hgp = (p["s5_b_re"].transpose(2, 0, 1), p["s5_b_im"].transpose(2, 0, 1))
    are, aim, bbre, bbim = _s5_prep(p["s5_lam_re"], p["s5_lam_im"], ls, b_hgp[0], b_hgp[1], tag + "s5_prep")
    mats = jnp.stack([bbre.transpose(1, 0, 2), bbim.transpose(1, 0, 2), p["s5_c_re"], p["s5_c_im"]])
    bre_bd, bim_bd, cret_bd, cimt_bd = _block_diag_build(mats, tag + "s5_blockdiag")
    s5mats = dict(bre_bd=bre_bd, bim_bd=bim_bd, cret_bd=cret_bd, cimt_bd=cimt_bd,
                  bret_bd=bre_bd.transpose(0, 2, 1), bimt_bd=bim_bd.transpose(0, 2, 1),
                  cre_bd=cret_bd.transpose(0, 2, 1), cim_bd=cimt_bd.transpose(0, 2, 1),
                  are=are.reshape(S5_TILES, 1, S5_TILE_ST), aim=aim.reshape(S5_TILES, 1, S5_TILE_ST))

    u, z, xbc, dt, h1 = _rms_inproj(x0, p["norm_mix"].reshape(1, -1), big["w_in"], None, tag + "rms_inproj")
    if after_inproj is not None:
        after_inproj(u)
    u_perm = _to_segments(u)
    ys_perm, xr, xi = _s5_scan(u_perm, bre_bd, bim_bd, s5mats["cre_bd"], s5mats["cim_bd"],
                               s5mats["are"], s5mats["aim"], tag + "s5_scan")
    ys = _from_segments(ys_perm)
    late_matrices = before_s5_post(ys) if before_s5_post is not None else {}
    big = {**big, **late_matrices}
    ya = _s5_post(ys, u, p["s5_d"].reshape(1, -1), big["s5_w_glu"], p["s5_b_glu"].reshape(1, -1),
                  p["s5_norm"].reshape(1, -1), None, tag + "s5_post")

    conv_w = jnp.pad(p["ssd_conv_w"], ((0, 8 - SSD_CONV), (0, 0)))
    conv_b = p["ssd_conv_b"].reshape(1, -1)
    xc = _ssd_conv(xbc, conv_w, conv_b, tag + "ssd_conv")
    expand = _head_expander()
    heads = dict(dt_bias=_pad_heads(p["ssd_dt_bias"]), a_log=_pad_heads(p["ssd_a_log"]),
                 d=jnp.repeat(p["ssd_d"], SSD_HEAD_DIM).reshape(SSD_GROUPS, 1, GROUP_W),
                 expand=expand, expand_t=expand.T)
    y, sprev = _ssd_scan(xc, dt, heads["dt_bias"], heads["a_log"], heads["d"], expand, heads["expand_t"],
                         tag + "ssd_scan")
    yb = _ssd_post(y, z, p["ssd_norm"].reshape(1, -1), tag + "ssd_post")

    x1 = _out_proj(x0, ya, yb, big["w_out"], None, tag + "out_proj")
    ffn_matrices = before_ffn(x1) if before_ffn is not None else {}
    big = {**big, **ffn_matrices}
    late_matrices = {**late_matrices, **ffn_matrices}
    x2, gt, up = _ffn(x1, p["norm_ffn"].reshape(1, -1), big["w_gate"], big["w_up"], big["w_down"], None,
                      tag + "ffn")
    saved = dict(x0=x0, h1=h1, u=u, u_perm=u_perm, z=z, xbc=xbc, dt=dt, xr=xr, xi=xi, ys=ys, ya=ya, xc=xc, y=y,
                 sprev=sprev, yb=yb, x1=x1, gt=gt, up=up, s5mats=s5mats, heads=heads, conv_w=conv_w,
                 conv_b=conv_b, ls=ls, b_hgp=b_hgp, late_matrices=late_matrices)
    return x2, saved


def _layer_backward(dx2, dx2b, p, big, s, i, after_ffn_grads=None, after_s5_grads=None):
    tag = "l%d_" % i
    g = {}
    dx1, dx1b, h2, act, dgt, dup, dgain = _ffn_bwd(dx2, s["x1"], s["gt"], s["up"], p["norm_ffn"].reshape(1, -1),
                                                  big["w_gate"], big["w_up"], big["w_down"], None, tag + "ffn_bwd")
    g["norm_ffn"] = dgain[0]
    g["w_down"] = _mm_tn(act, dx2b, tag + "dw_down")
    g["w_gate"] = _mm_tn(dgt, h2, tag + "dw_gate")
    g["w_up"] = _mm_tn(dup, h2, tag + "dw_up")
    g["w_out"] = _mm_tn(jnp.concatenate([s["ya"], s["yb"]], axis=1), dx1b, tag + "dw_out")
    if after_ffn_grads is not None:
        p = {**p, "s5_norm": p["s5_norm"] + after_ffn_grads(g)[0, 0]}

    dys, du_skip, gelu_b, dq_b, dgain, dd, dbg = _s5_post_bwd(
        dx1b, big["w_out"], s["ys"], s["u"], p["s5_d"].reshape(1, -1), big["s5_w_glu"],
        p["s5_b_glu"].reshape(1, -1), p["s5_norm"].reshape(1, -1), None, tag + "s5_post_bwd")
    g["s5_norm"] = dgain[0]
    g["s5_d"] = dd[0]
    g["s5_b_glu"] = dbg[0]
    g["s5_w_glu"] = _mm_tn(gelu_b, dq_b, tag + "dw_glu")
    m = s["s5mats"]
    du_perm, dar, dai, dcre_d, dcim_d, dbre_d, dbim_d = _s5_scan_bwd(
        _to_segments(dys), s["u_perm"], s["xr"], s["xi"], m["bret_bd"], m["bimt_bd"], m["cret_bd"], m["cimt_bd"],
        m["are"], m["aim"], tag + "s5_scan_bwd")
    du_scan = _from_segments(du_perm)
    diag = _block_diag_extract([dcre_d, dcim_d, dbre_d, dbim_d], tag + "s5_blockdiag_bwd")
    g["s5_c_re"], g["s5_c_im"] = diag[0], diag[1]
    dbbre = diag[2].transpose(1, 0, 2)
    dbbim = diag[3].transpose(1, 0, 2)
    dlr, dli, dls, dbre, dbim = _s5_prep_bwd(
        p["s5_lam_re"], p["s5_lam_im"], s["ls"], s["b_hgp"][0], s["b_hgp"][1],
        dar.reshape(S5_GROUPS, S5_STATE), dai.reshape(S5_GROUPS, S5_STATE), dbbre, dbbim, tag + "s5_prep_bwd")
    g["s5_lam_re"] = dlr
    g["s5_lam_im"] = dli
    g["s5_log_step"] = dls[:, 0]
    g["s5_b_re"] = dbre
    g["s5_b_im"] = dbim
    if after_s5_grads is not None:
        p = {**p, "ssd_norm": p["ssd_norm"] + after_s5_grads(g)[0, 0]}

    dy, dz, dgain = _ssd_post_bwd(dx1b, big["w_out"], s["y"], s["z"], p["ssd_norm"].reshape(1, -1), None,
                                  tag + "ssd_post_bwd")
    g["ssd_norm"] = dgain[0]
    hd = s["heads"]
    dxs, dbm, dcm, ddt, dbias, dalog, dd = _ssd_scan_bwd(dy, s["xc"], s["dt"], s["sprev"], hd["dt_bias"],
                                                       hd["a_log"], hd["d"], hd["expand"], hd["expand_t"],
                                                       tag + "ssd_scan_bwd")
    g["ssd_dt_bias"] = _unpad_heads(dbias)
    g["ssd_a_log"] = _unpad_heads(dalog)
    g["ssd_d"] = _unpad_heads(dd)
    dxc = jnp.concatenate([dxs, dbm, dcm], axis=1)
    dxbc, dcw, dcb = _ssd_conv_bwd(dxc, s["xbc"], s["conv_w"], s["conv_b"], tag + "ssd_conv_bwd")
    g["ssd_conv_w"] = dcw[:SSD_CONV]
    g["ssd_conv_b"] = dcb[0]

    dx0, dx0b, dproj, dgain = _inproj_bwd(dx1, s["x0"], du_skip, du_scan, dz, dxbc, ddt, p["norm_mix"].reshape(1, -1),
                                          big["w_in"], None, tag + "inproj_bwd")
    g["norm_mix"] = dgain[0]
    g["w_in"] = _mm_tn(s["h1"], dproj, tag + "dw_in")
    return dx0, dx0b, g


MIXER_BIG = ("w_in", "s5_w_glu", "w_out")
FFN_BIG = ("w_gate", "w_up", "w_down")
BIG = MIXER_BIG + FFN_BIG
COL_SHARDED = ("w_in",)
T_STORED = ("w_gate", "w_up")
LAYER_SMALL = ("norm_mix", "s5_lam_re", "s5_lam_im", "s5_log_step", "s5_b_re", "s5_b_im", "s5_c_re", "s5_c_im",
               "s5_d", "s5_b_glu", "s5_norm", "ssd_conv_w", "ssd_conv_b", "ssd_dt_bias", "ssd_a_log", "ssd_d",
               "ssd_norm", "norm_ffn")
WEIGHTS = ("norm_mix", "w_in", "s5_lam_re", "s5_lam_im", "s5_log_step", "s5_b_re", "s5_b_im", "s5_c_re", "s5_c_im",
           "s5_d", "s5_w_glu", "s5_b_glu", "s5_norm", "ssd_conv_w", "ssd_conv_b", "ssd_dt_bias", "ssd_a_log",
           "ssd_d", "ssd_norm", "w_out", "norm_ffn", "w_gate", "w_up", "w_down", "norm_final")


S5_BC = ("s5_b_re", "s5_b_im", "s5_c_re", "s5_c_im")
TINY = tuple(k for k in LAYER_SMALL if k not in S5_BC)


def _local_step(x, target, big, small, norm_final):
    saved = []
    h = x
    for i in range(DEPTH):
        p = {k: v[i] for k, v in small.items()}
        h, s = _layer_forward(h, p, big, i)
        saved.append((p, s))
    loss, dx, dxb, dgf = _final_loss(h, norm_final.reshape(1, -1), target, "final_loss")
    grads = [None] * DEPTH
    for i in reversed(range(DEPTH)):
        p, s = saved[i]
        dx, dxb, grads[i] = _layer_backward(dx, dxb, p, big, s, i)
    by_name = {k: [grads[i][k] for i in range(DEPTH)] for k in BIG + LAYER_SMALL}
    return loss[0, 0], dx, by_name, dgf[0]


def _my_place():
    return lax.axis_index("x"), lax.axis_index("y"), lax.axis_index("c")


def _all_gather8(blocks, name):
    nt = len(blocks)

    def body(*refs):
        ins = refs[:nt]
        outs = refs[nt:2 * nt]
        send_sems, recv_sems, local_sems = refs[2 * nt:]
        x, y, c = _my_place()
        me, sibling = (x, y, c), (x, y, 1 - c)
        chips = [(1 - x, y), (x, 1 - y), (1 - x, 1 - y)]

        def slot(t, place):
            px, py, pc = place
            return outs[t].at[4 * px + 2 * py + pc]

        def copy(t, k, block, to, src=None):
            return pltpu.make_async_remote_copy(
                src_ref=slot(t, block) if src is None else src, dst_ref=slot(t, block),
                send_sem=send_sems.at[t, k], recv_sem=recv_sems.at[t, k], device_id=to, device_id_type=MESH)

        mine = [pltpu.make_async_copy(ins[t], slot(t, me), local_sems.at[t]) for t in range(nt)]
        for cp in mine:
            cp.start()
        first = []
        for t in range(nt):
            first.append(copy(t, 0, me, sibling, src=ins[t]))
            first += [copy(t, 1 + j, me, (*chip, c), src=ins[t]) for j, chip in enumerate(chips)]
        for cp in first:
            cp.start()
        passed = []
        for j, chip in enumerate(chips):
            for t in range(nt):
                copy(t, 1 + j, (*chip, c), me).wait_recv()
                fwd = copy(t, 4 + j, (*chip, c), sibling)
                fwd.start()
                passed.append(fwd)
        for t in range(nt):
            copy(t, 0, sibling, me).wait_recv()
            for j, chip in enumerate(chips):
                copy(t, 4 + j, (*chip, 1 - c), me).wait_recv()
        for cp in first + passed:
            cp.wait_send()
        for cp in mine:
            cp.wait()

    return pl.pallas_call(
        body, name=name, in_specs=[ANY] * nt, out_specs=[ANY] * nt,
        out_shape=[jax.ShapeDtypeStruct((8,) + b.shape, b.dtype) for b in blocks],
        scratch_shapes=[pltpu.SemaphoreType.DMA((nt, 7)), pltpu.SemaphoreType.DMA((nt, 7)),
                        pltpu.SemaphoreType.DMA((nt,))],
    )(*blocks)


HBM = pl.BlockSpec(memory_space=pltpu.HBM)
SEM = pl.BlockSpec(memory_space=pltpu.SEMAPHORE)
DATAFLOW = pltpu.SideEffectType.DATAFLOW_SIDE_EFFECTING


def _in_hbm(a):
    return pltpu.with_memory_space_constraint(a, pltpu.HBM)


TOKEN = jax.ShapeDtypeStruct((8, LANES), F32)
VMEM_SPEC = pl.BlockSpec(memory_space=pltpu.VMEM)


def _gather_start(blocks, after, name):
    nt = len(blocks)

    def body(*refs):
        ins = refs[:nt]
        lands = refs[nt:2 * nt]
        send_sems, recv_sems = refs[2 * nt + 1:2 * nt + 3]
        refs[-1][...] = jnp.zeros(TOKEN.shape, F32)
        x, y, c = _my_place()
        me = 4 * x + 2 * y + c
        peers = [(x, y, 1 - c), (1 - x, y, c), (x, 1 - y, c), (1 - x, 1 - y, c)]
        for t in range(nt):
            for k, peer in enumerate(peers):
                pltpu.make_async_remote_copy(src_ref=ins[t], dst_ref=lands[t].at[me], send_sem=send_sems.at[4 * t + k],
                                             recv_sem=recv_sems.at[4 * t + k], device_id=peer,
                                             device_id_type=MESH).start()

    lands = [_in_hbm(lax.empty((8,) + b.shape, b.dtype)) for b in blocks]
    out = pl.pallas_call(
        body, name=name, in_specs=[HBM] * (2 * nt) + [ANY],
        out_shape=(pltpu.SemaphoreType.DMA((4 * nt,)), pltpu.SemaphoreType.DMA((4 * nt,)),
                   *[pltpu.HBM(b.shape, b.dtype) for b in blocks],
                   *[pltpu.HBM((8,) + b.shape, b.dtype) for b in blocks], TOKEN),
        out_specs=(SEM, SEM, *[HBM] * (2 * nt), VMEM_SPEC),
        input_output_aliases={i: 2 + i for i in range(2 * nt)},
        compiler_params=pltpu.CompilerParams(has_side_effects=DATAFLOW),
    )(*[_in_hbm(b) for b in blocks], *lands, after)
    return out[:2], list(out[2:2 + nt]), list(out[2 + nt:2 + 2 * nt]), out[-1]


def _gather_forward(sems, blocks, lands, after, name):
    nt = len(blocks)

    def body(*refs):
        ins = refs[:nt]
        lands_in = refs[nt:2 * nt]
        send1, recv1 = refs[2 * nt:2 * nt + 2]
        send2, recv2 = refs[2 * nt + 3:2 * nt + 5]
        x, y, c = _my_place()
        me = 4 * x + 2 * y + c
        sibling = (x, y, 1 - c)
        sources = [4 * x + 2 * y + (1 - c), 4 * (1 - x) + 2 * y + c, 4 * x + 2 * (1 - y) + c,
                   4 * (1 - x) + 2 * (1 - y) + c]
        for t in range(nt):
            for k, src in enumerate(sources):
                cp = pltpu.make_async_remote_copy(src_ref=ins[t], dst_ref=lands_in[t].at[src],
                                                  send_sem=send1.at[4 * t + k], recv_sem=recv1.at[4 * t + k],
                                                  device_id=sibling, device_id_type=MESH)
                cp.wait_send()
                cp.wait_recv()
            for k, src in enumerate(sources[1:]):
                pltpu.make_async_remote_copy(src_ref=lands_in[t].at[src], dst_ref=lands_in[t].at[src],
                                             send_sem=send2.at[3 * t + k], recv_sem=recv2.at[3 * t + k],
                                             device_id=sibling, device_id_type=MESH).start()

    out = pl.pallas_call(
        body, name=name, in_specs=[HBM] * (2 * nt) + [SEM, SEM, pl.BlockSpec(memory_space=pl.ANY)],
        out_shape=(pltpu.SemaphoreType.DMA((3 * nt,)), pltpu.SemaphoreType.DMA((3 * nt,)),
                   *[pltpu.HBM(b.shape, b.dtype) for b in blocks],
                   *[pltpu.HBM(a.shape, a.dtype) for a in lands]),
        out_specs=(SEM, SEM, *[HBM] * (2 * nt)),
        input_output_aliases={i: 2 + i for i in range(2 * nt)},
        compiler_params=pltpu.CompilerParams(has_side_effects=DATAFLOW),
    )(*blocks, *lands, *sems, after)
    return out[:2], list(out[2 + nt:])


def _gather_finish(sems, lands, after, name):
    nt = len(lands)

    def body(*refs):
        lands_in = refs[:nt]
        send2, recv2 = refs[nt:nt + 2]
        x, y, c = _my_place()
        sibling = (x, y, 1 - c)
        mine = [4 * (1 - x) + 2 * y + c, 4 * x + 2 * (1 - y) + c, 4 * (1 - x) + 2 * (1 - y) + c]
        theirs = [4 * (1 - x) + 2 * y + 1 - c, 4 * x + 2 * (1 - y) + 1 - c, 4 * (1 - x) + 2 * (1 - y) + 1 - c]
        for t in range(nt):
            for k in range(3):
                cp = pltpu.make_async_remote_copy(src_ref=lands_in[t].at[mine[k]], dst_ref=lands_in[t].at[theirs[k]],
                                                  send_sem=send2.at[3 * t + k], recv_sem=recv2.at[3 * t + k],
                                                  device_id=sibling, device_id_type=MESH)
                cp.wait_send()
                cp.wait_recv()

    out = pl.pallas_call(
        body, name=name, in_specs=[HBM] * nt + [SEM, SEM, pl.BlockSpec(memory_space=pl.ANY)],
        out_shape=tuple(pltpu.HBM(a.shape, a.dtype) for a in lands), out_specs=tuple([HBM] * nt),
        input_output_aliases={i: i for i in range(nt)},
        compiler_params=pltpu.CompilerParams(has_side_effects=DATAFLOW),
    )(*lands, *sems, after)
    return list(out)


def _other_chips():
    x, y, _ = _my_place()
    return [(1 - x, y), (x, 1 - y), (1 - x, 1 - y)]


def _scatter_start(chunks, name):
    nt = len(chunks)

    def body(*refs):
        ins = refs[:nt]
        lands = refs[nt:2 * nt]
        send_sems, recv_sems = refs[2 * nt:2 * nt + 2]
        refs[-1][...] = jnp.zeros(TOKEN.shape, F32)
        x, y, c = _my_place()
        for t in range(nt):
            for j, (px, py) in enumerate(_other_chips()):
                pltpu.make_async_remote_copy(src_ref=ins[t].at[2 * px + py], dst_ref=lands[t].at[2 * x + y],
                                             send_sem=send_sems.at[3 * t + j], recv_sem=recv_sems.at[3 * t + j],
                                             device_id=(px, py, c), device_id_type=MESH).start()

    lands = [_in_hbm(lax.empty(a.shape, a.dtype)) for a in chunks]
    out = pl.pallas_call(
        body, name=name, in_specs=[HBM] * (2 * nt),
        out_shape=(pltpu.SemaphoreType.DMA((3 * nt,)), pltpu.SemaphoreType.DMA((3 * nt,)),
                   *[pltpu.HBM(a.shape, a.dtype) for a in chunks] * 2, TOKEN),
        out_specs=(SEM, SEM, *[HBM] * (2 * nt), VMEM_SPEC),
        input_output_aliases={i: 2 + i for i in range(2 * nt)},
        compiler_params=pltpu.CompilerParams(has_side_effects=DATAFLOW),
    )(*[_in_hbm(a) for a in chunks], *lands)
    return out[:2], list(out[2:2 + nt]), list(out[2 + nt:2 + 2 * nt]), out[-1]


def _scatter_finish(sems, chunks, lands, after, name):
    nt = len(chunks)

    def body(*refs):
        ins = refs[:nt]
        lands_in = refs[nt:2 * nt]
        send_sems, recv_sems = refs[2 * nt:2 * nt + 2]
        _, _, c = _my_place()
        for t in range(nt):
            for j, (px, py) in enumerate(_other_chips()):
                cp = pltpu.make_async_remote_copy(src_ref=ins[t].at[2 * px + py], dst_ref=lands_in[t].at[2 * px + py],
                                                  send_sem=send_sems.at[3 * t + j], recv_sem=recv_sems.at[3 * t + j],
                                                  device_id=(px, py, c), device_id_type=MESH)
                cp.wait_send()
                cp.wait_recv()

    out = pl.pallas_call(
        body, name=name, in_specs=[HBM] * (2 * nt) + [SEM, SEM, ANY],
        out_shape=tuple(pltpu.HBM(a.shape, a.dtype) for a in lands), out_specs=tuple([HBM] * nt),
        input_output_aliases={nt + i: i for i in range(nt)},
        compiler_params=pltpu.CompilerParams(has_side_effects=DATAFLOW),
    )(*chunks, *lands, *sems, after)
    return list(out)


def _swap_halves(views, name):
    nt = len(views)

    def body(*refs):
        ins = refs[:nt]
        outs = refs[nt:2 * nt]
        send_sems, recv_sems = refs[2 * nt:]
        x, y, c = _my_place()
        copies = [pltpu.make_async_remote_copy(
            src_ref=ins[t].at[pl.ds(0, views[t].shape[0]), pl.ds(1 - c, 1)], dst_ref=outs[t],
            send_sem=send_sems.at[t], recv_sem=recv_sems.at[t], device_id=(x, y, 1 - c), device_id_type=MESH)
            for t in range(nt)]
        for cp in copies:
            cp.start()
        for cp in copies:
            cp.wait()

    return pl.pallas_call(
        body, name=name, in_specs=[ANY] * nt, out_specs=[ANY] * nt,
        out_shape=[jax.ShapeDtypeStruct((a.shape[0], 1) + a.shape[2:], a.dtype) for a in views],
        scratch_shapes=[pltpu.SemaphoreType.DMA((nt,)), pltpu.SemaphoreType.DMA((nt,))],
    )(*views)


def _pair_add_halves(view, recv, name):
    n, _, rows, cols = view.shape
    tile = _row_tile(rows, cols, 4)

    def body(a0_ref, a1_ref, r_ref, o_ref):
        mine = jnp.where(lax.axis_index("c") == 0, a0_ref[...], a1_ref[...])
        o_ref[...] = (mine.astype(F32) + r_ref[...].astype(F32)).astype(o_ref.dtype)

    half = lambda h: pl.BlockSpec((None, None, tile, cols), lambda p, i: (p, h, i, 0))
    return pl.pallas_call(
        body, name=name, grid=(n, rows // tile), in_specs=[half(0), half(1), half(0)],
        out_specs=pl.BlockSpec((None, tile, cols), lambda p, i: (p, i, 0)),
        out_shape=jax.ShapeDtypeStruct((n, rows, cols), view.dtype))(view, view, recv)


def _sum_chunks(lands, chunks, order, name):
    _, rows, cols = chunks.shape
    tile = _row_tile(rows, cols, 5)

    def body(order_ref, l0_ref, l1_ref, l2_ref, own_ref, o_ref):
        o_ref[...] = ((l0_ref[...].astype(F32) + l1_ref[...].astype(F32)) + l2_ref[...].astype(F32)
                      + own_ref[...].astype(F32))

    slot = lambda j: pl.BlockSpec((None, tile, cols), lambda i, order_ref: (order_ref[j], i, 0))
    grid_spec = pltpu.PrefetchScalarGridSpec(
        num_scalar_prefetch=1, grid=(rows // tile,), in_specs=[slot(0), slot(1), slot(2), slot(3)],
        out_specs=pl.BlockSpec((tile, cols), lambda i, order_ref: (i, 0)))
    return pl.pallas_call(body, name=name, grid_spec=grid_spec,
                          out_shape=jax.ShapeDtypeStruct((rows, cols), F32))(order, lands, lands, lands, chunks)


def _adamw_layer(w, g_mine, g_sibling, m, v, layer, prev, name):
    depth, rows, cols = w.shape
    half = rows // 2
    tile = _row_tile(half, cols, 10)
    tiles = half // tile

    def body(w_ref, gm_ref, gs_ref, m_ref, v_ref, *rest):
        d_ref, nm_ref, nv_ref, go_ref = rest[-4:]
        gv = jnp.where(pl.program_id(0) == lax.axis_index("c"), gm_ref[...], gs_ref[...])
        d_ref[...], nm_ref[...], nv_ref[...] = _adamw_math(w_ref[...], gv, m_ref[...], v_ref[...])
        go_ref[...] = gv

    spec = pl.BlockSpec((None, tile, cols), lambda h, i: (layer, h * tiles + i, 0))
    gspec = pl.BlockSpec((tile, cols), lambda h, i: (i, 0))
    shape = jax.ShapeDtypeStruct((depth, rows, cols), F32)
    extra = list(prev)
    aliases = {5 + j: j for j in range(4)} if len(extra) == 4 else {}
    return pl.pallas_call(
        body, name=name, grid=(2, tiles), in_specs=[spec, gspec, gspec, spec, spec] + [ANY] * len(extra),
        out_specs=[spec] * 4, out_shape=[shape] * 4, input_output_aliases=aliases)(w, g_mine, g_sibling, m, v, *extra)


def _sibling_swap_other(pairs, name):
    nt = len(pairs)

    def body(*refs):
        ins = refs[:2 * nt]
        outs = refs[2 * nt:3 * nt]
        send_sems, recv_sems = refs[3 * nt:]
        x, y, c = _my_place()

        def copy(t, src):
            return pltpu.make_async_remote_copy(src_ref=src, dst_ref=outs[t], send_sem=send_sems.at[t],
                                                recv_sem=recv_sems.at[t], device_id=(x, y, 1 - c), device_id_type=MESH)

        for t in range(nt):
            @pl.when(c == 0)
            def _():
                copy(t, ins[2 * t + 1]).start()

            @pl.when(c == 1)
            def _():
                copy(t, ins[2 * t]).start()
        for t in range(nt):
            copy(t, ins[2 * t]).wait()

    flat = [a for pair in pairs for a in pair]
    return pl.pallas_call(
        body, name=name, in_specs=[ANY] * (2 * nt), out_specs=[ANY] * nt,
        out_shape=[jax.ShapeDtypeStruct(a0.shape, a0.dtype) for a0, _ in pairs],
        scratch_shapes=[pltpu.SemaphoreType.DMA((nt,)), pltpu.SemaphoreType.DMA((nt,))],
    )(*flat)


def _sibling_swap(arrs, name):
    nt = len(arrs)

    def body(*refs):
        ins = refs[:nt]
        outs = refs[nt:2 * nt]
        send_sems, recv_sems = refs[2 * nt:]
        x, y, c = _my_place()
        copies = [pltpu.make_async_remote_copy(src_ref=ins[t], dst_ref=outs[t], send_sem=send_sems.at[t],
                                               recv_sem=recv_sems.at[t], device_id=(x, y, 1 - c), device_id_type=MESH)
                  for t in range(nt)]
        for cp in copies:
            cp.start()
        for cp in copies:
            cp.wait()

    return pl.pallas_call(
        body, name=name, in_specs=[ANY] * nt, out_specs=[ANY] * nt,
        out_shape=[jax.ShapeDtypeStruct(a.shape, a.dtype) for a in arrs],
        scratch_shapes=[pltpu.SemaphoreType.DMA((nt,)), pltpu.SemaphoreType.DMA((nt,))],
    )(*arrs)


def _chip_all_to_all(arrs, name):
    nt = len(arrs)

    def body(*refs):
        ins = refs[:nt]
        outs = refs[nt:2 * nt]
        send_sems, recv_sems, local_sems = refs[2 * nt:]
        x, y, c = _my_place()
        mine = 2 * x + y
        chips = [(1 - x, y), (x, 1 - y), (1 - x, 1 - y)]
        local = [pltpu.make_async_copy(ins[t].at[mine], outs[t].at[mine], local_sems.at[t]) for t in range(nt)]
        for cp in local:
            cp.start()
        sends = []
        for t in range(nt):
            for j, (px, py) in enumerate(chips):
                sends.append(pltpu.make_async_remote_copy(
                    src_ref=ins[t].at[2 * px + py], dst_ref=outs[t].at[mine], send_sem=send_sems.at[t, j],
                    recv_sem=recv_sems.at[t, j], device_id=(px, py, c), device_id_type=MESH))
        for cp in sends:
            cp.start()
        for t in range(nt):
            for j, (px, py) in enumerate(chips):
                pltpu.make_async_remote_copy(
                    src_ref=ins[t].at[mine], dst_ref=outs[t].at[2 * px + py], send_sem=send_sems.at[t, j],
                    recv_sem=recv_sems.at[t, j], device_id=(px, py, c), device_id_type=MESH).wait_recv()
        for cp in sends:
            cp.wait_send()
        for cp in local:
            cp.wait()

    return pl.pallas_call(
        body, name=name, in_specs=[ANY] * nt, out_specs=[ANY] * nt,
        out_shape=[jax.ShapeDtypeStruct(a.shape, a.dtype) for a in arrs],
        scratch_shapes=[pltpu.SemaphoreType.DMA((nt, 3)), pltpu.SemaphoreType.DMA((nt, 3)),
                        pltpu.SemaphoreType.DMA((nt,))],
    )(*arrs)


def _as_rows(a):
    return a.reshape(-1, a.shape[-1])


STREAM_VMEM_BYTES = 32 * 1024 * 1024
SUBLANES = 8


def _row_tile(rows, cols, n_arrays):
    lanes = -(-cols // LANES) * LANES
    for t in range(min(rows, 512), SUBLANES - 1, -1):
        if rows % t == 0 and t % SUBLANES == 0 and 2 * n_arrays * t * lanes * 4 <= STREAM_VMEM_BYTES:
            return t
    return rows


def _pair_add(a0, a1, recv, name):
    rows, cols = a0.shape
    tile = _row_tile(rows, cols, 4)

    def body(a0_ref, a1_ref, r_ref, o_ref):
        mine = jnp.where(lax.axis_index("c") == 0, a0_ref[...], a1_ref[...])
        o_ref[...] = (mine.astype(F32) + r_ref[...].astype(F32)).astype(o_ref.dtype)

    spec = pl.BlockSpec((tile, cols), lambda i: (i, 0))
    return pl.pallas_call(body, name=name, grid=(rows // tile,), in_specs=[spec] * 3, out_specs=spec,
                          out_shape=jax.ShapeDtypeStruct((rows, cols), a0.dtype))(a0, a1, recv)


def _sum_leading(a, name):
    n, rows, cols = a.shape
    tile = _row_tile(rows, cols, n + 1)

    def body(a_ref, o_ref):
        acc = a_ref[0].astype(F32)
        for k in range(1, n):
            acc = acc + a_ref[k].astype(F32)
        o_ref[...] = acc

    return pl.pallas_call(
        body, name=name, grid=(rows // tile,), in_specs=[pl.BlockSpec((n, tile, cols), lambda i: (0, i, 0))],
        out_specs=pl.BlockSpec((tile, cols), lambda i: (i, 0)),
        out_shape=jax.ShapeDtypeStruct((rows, cols), F32))(a)


def _adamw_math(w, g, m, v):
    mn = ADAM_B1 * m + (1.0 - ADAM_B1) * g
    vn = ADAM_B2 * v + (1.0 - ADAM_B2) * jnp.square(g)
    m_hat = mn / (1.0 - ADAM_B1 ** ADAM_STEP)
    v_hat = vn / (1.0 - ADAM_B2 ** ADAM_STEP)
    delta = -ADAM_LR * (m_hat / (jnp.sqrt(v_hat) + ADAM_EPS) + ADAM_WD * w)
    return delta, mn, vn


def _adamw_layers(w, g_mine, g_sibling, m, v, name):
    depth, rows, cols = w.shape
    tile = _row_tile(rows, cols, 10)

    def body(w_ref, gm_ref, gs_ref, m_ref, v_ref, d_ref, nm_ref, nv_ref, go_ref):
        gv = jnp.where(pl.program_id(0) == lax.axis_index("c"), gm_ref[...], gs_ref[...])
        d_ref[...], nm_ref[...], nv_ref[...] = _adamw_math(w_ref[...], gv, m_ref[...], v_ref[...])
        go_ref[...] = gv

    spec = pl.BlockSpec((None, tile, cols), lambda l, i: (l, i, 0))
    gspec = pl.BlockSpec((tile, cols), lambda l, i: (i, 0))
    shape = jax.ShapeDtypeStruct((depth, rows, cols), F32)
    return pl.pallas_call(body, name=name, grid=(depth, rows // tile), in_specs=[spec, gspec, gspec, spec, spec],
                          out_specs=[spec] * 4, out_shape=[shape] * 4)(w, g_mine, g_sibling, m, v)


def _adamw_rows(w, g, m, v, name):
    depth, rows, cols = w.shape
    tile = _row_tile(rows, cols, 7)

    def body(w_ref, g_ref, m_ref, v_ref, d_ref, nm_ref, nv_ref):
        d_ref[...], nm_ref[...], nv_ref[...] = _adamw_math(w_ref[...], g_ref[...], m_ref[...], v_ref[...])

    spec = pl.BlockSpec((None, tile, cols), lambda l, i: (l, i, 0))
    shape = jax.ShapeDtypeStruct((depth, rows, cols), F32)
    return pl.pallas_call(body, name=name, grid=(depth, rows // tile), in_specs=[spec] * 4, out_specs=[spec] * 3,
                          out_shape=[shape] * 3)(w, g, m, v)


def _adamw_many(ws, gs, ms, vs, name):
    nt = len(ws)

    def body(*refs):
        for t in range(nt):
            w_ref, g_ref, m_ref, v_ref = (refs[k * nt + t] for k in range(4))
            d_ref, nm_ref, nv_ref = (refs[(4 + k) * nt + t] for k in range(3))
            d_ref[...], nm_ref[...], nv_ref[...] = _adamw_math(w_ref[...], g_ref[...], m_ref[...], v_ref[...])

    shapes = [jax.ShapeDtypeStruct(a.shape, F32) for a in ws]
    out = pl.pallas_call(body, name=name, out_shape=shapes * 3)(*ws, *gs, *ms, *vs)
    return out[:nt], out[nt:2 * nt], out[2 * nt:]


TINY_ROWS_MULTIPLE = 128


def _flat_pack(arrs):
    flat = jnp.concatenate([a.reshape(-1) for a in arrs])
    pad = (-flat.shape[0]) % (TINY_ROWS_MULTIPLE * LANES)
    return jnp.pad(flat, (0, pad)).reshape(-1, LANES)


def _flat_unpack(buf, shapes):
    flat = buf.reshape(-1)
    out = []
    off = 0
    for shp in shapes:
        n = math.prod(shp)
        out.append(flat[off:off + n].reshape(shp))
        off += n
    return out


def _to_chunks(a, name):
    if name == "w_in":
        a = _unpad_in_proj(a)
    rows, cols = a.shape
    if name in COL_SHARDED:
        return a.reshape(rows, 4, cols // 4).transpose(1, 0, 2)
    return a.reshape(4, rows // 4, cols)


def _from_chunks(a, name):
    _, depth, r, cc = a.shape
    if name in COL_SHARDED:
        return a.transpose(1, 2, 0, 3).reshape(depth, r, 4 * cc)
    return a.transpose(1, 0, 2, 3).reshape(depth, 4 * r, cc)


def kernel(x, norm_mix, w_in, s5_lam_re, s5_lam_im, s5_log_step, s5_b_re, s5_b_im, s5_c_re, s5_c_im, s5_d, s5_w_glu, s5_b_glu, s5_norm, ssd_conv_w, ssd_conv_b, ssd_dt_bias, ssd_a_log, ssd_d, ssd_norm, w_out, norm_ffn, w_gate, w_up, w_down, norm_final, loss_target, m_norm_mix, m_w_in, m_s5_lam_re, m_s5_lam_im, m_s5_log_step, m_s5_b_re, m_s5_b_im, m_s5_c_re, m_s5_c_im, m_s5_d, m_s5_w_glu, m_s5_b_glu, m_s5_norm, m_ssd_conv_w, m_ssd_conv_b, m_ssd_dt_bias, m_ssd_a_log, m_ssd_d, m_ssd_norm, m_w_out, m_norm_ffn, m_w_gate, m_w_up, m_w_down, m_norm_final, v_norm_mix, v_w_in, v_s5_lam_re, v_s5_lam_im, v_s5_log_step, v_s5_b_re, v_s5_b_im, v_s5_c_re, v_s5_c_im, v_s5_d, v_s5_w_glu, v_s5_b_glu, v_s5_norm, v_ssd_conv_w, v_ssd_conv_b, v_ssd_dt_bias, v_ssd_a_log, v_ssd_d, v_ssd_norm, v_w_out, v_norm_ffn, v_w_gate, v_w_up, v_w_down, v_norm_final):
    args = dict(locals())
    w = {k: args[k] for k in WEIGHTS}
    m = {k: args["m_" + k] for k in WEIGHTS}
    v = {k: args["v_" + k] for k in WEIGHTS}
    cx, cy, cc = _my_place()
    chip = 2 * cx + cy

    me = 4 * cx + 2 * cy + cc
    others = _other_chips()
    chunk_order = jnp.stack([2 * px + py for px, py in others] + [chip]).astype(jnp.int32)
    stored = lambda k, a: jnp.swapaxes(a, 1, 2) if k in T_STORED else a

    def my_half(k, layer):
        a = stored(k, w[k])[layer]
        return lax.dynamic_slice_in_dim(a, cc * (a.shape[0] // 2), a.shape[0] // 2, 0).astype(BF16)

    def assemble(names, lands, blocks):
        full = {}
        for k, a, b in zip(names, lands, blocks):
            a = lax.dynamic_update_index_in_dim(a, b, me, 0)
            a = a.reshape(4, 2 * a.shape[1], a.shape[2])
            if k in COL_SHARDED:
                full[k] = _pad_in_proj(a.transpose(1, 0, 2).reshape(a.shape[1], 4 * a.shape[2]))
            else:
                full[k] = a.reshape(4 * a.shape[1], a.shape[2])
        return full

    conv_block = w["ssd_conv_w"].reshape(DEPTH * SSD_CONV, -1)
    first = [my_half("w_in", 0), conv_block]
    second_names = ("s5_w_glu", "w_out")
    second = [my_half(k, 0) for k in second_names]
    ffn0 = [my_half(k, 0) for k in FFN_BIG]
    blocks1 = [my_half(k, 1) for k in BIG]
    sems_a, kept_a, lands_a, token = _gather_start(first, x, "gather0a_start")
    sems_c, kept_c, lands_c, token = _gather_start(second, token, "gather0c_start")
    sems_b, kept_b, lands_b, token = _gather_start(ffn0, token, "gather0b_start")
    sems1, kept1, lands1, token = _gather_start(blocks1, token, "gather1_start")
    sems_a, lands_a = _gather_forward(sems_a, kept_a, lands_a, token, "gather0a_forward")
    lands_a = _gather_finish(sems_a, lands_a, token, "gather0a_finish")
    big0 = assemble(("w_in",), lands_a, first)
    conv_rows = lax.dynamic_update_index_in_dim(lands_a[-1], conv_block, me, 0)
    conv_full = conv_rows.reshape(4, 2, DEPTH, SSD_CONV, -1)[:, 0].transpose(1, 2, 0, 3).reshape(
        DEPTH, SSD_CONV, SSD_CONV_DIM)
    small = {k: w[k] for k in LAYER_SMALL}
    small["ssd_conv_w"] = conv_full
    p0 = {k: a[0] for k, a in small.items()}
    p1 = {k: a[1] for k, a in small.items()}

    p0["norm_mix"] = p0["norm_mix"] + token[0, 0]
    pending = {}

    def pass_on_ffn0(u):
        pending["second"] = _gather_forward(sems_c, kept_c, lands_c, u, "gather0c_forward")
        pending["ffn0"] = _gather_forward(sems_b, kept_b, lands_b, pending["second"][1][0], "gather0b_forward")

    def second_matrices(ys):
        sems, lands = pending["second"]
        return assemble(second_names, _gather_finish(sems, lands, ys, "gather0c_finish"), second)

    def ffn0_matrices(x1):
        sems, lands = pending["ffn0"]
        lands = _gather_finish(sems, lands, x1, "gather0b_finish")
        pending["layer1"] = _gather_forward(sems1, kept1, lands1, lands[0], "gather1_forward")
        return assemble(FFN_BIG, lands, ffn0)

    h1, saved0 = _layer_forward(x[0], p0, big0, 0, pass_on_ffn0, second_matrices, ffn0_matrices)
    big0 = {**big0, **saved0["late_matrices"]}
    sems1, lands1 = pending["layer1"]
    lands1 = _gather_finish(sems1, lands1, h1, "gather1_finish")
    big1 = assemble(BIG, lands1, blocks1)
    h2, saved1 = _layer_forward(h1, p1, big1, 1)
    loss_row, dx, dxb, g_final = _final_loss(h2, w["norm_final"].reshape(1, -1), loss_target[0], "final_loss")
    loss_part, g_final = loss_row[0, 0], g_final[0]

    def halves_view(k, a):
        if k in COL_SHARDED:
            return a.reshape(1, 2, a.shape[0] // 2, a.shape[1])
        return a.reshape(4, 2, a.shape[0] // 8, a.shape[1])

    def to_chunks(k, part):
        if k in COL_SHARDED:
            a = _unpad_in_proj(part[0])
            return a.reshape(a.shape[0], 4, a.shape[1] // 4).transpose(1, 0, 2)
        return part.reshape(4, -1, part.shape[-1])

    def reduce_begin(names, views, tag):
        recv = _swap_halves(views, tag + "swap")
        parts = [_pair_add_halves(a, r, tag + "pair_" + k) for k, a, r in zip(names, views, recv)]
        chunks = [to_chunks(k, p) for k, p in zip(names, parts)]
        return _scatter_start(chunks, tag + "scatter_start")

    def reduce_end(names, handle, after, tag):
        sems, kept, lands, _ = handle
        lands = _scatter_finish(sems, kept, lands, after, tag + "scatter_finish")
        return [_sum_chunks(a, b, chunk_order, tag + "sum_" + k) for k, a, b in zip(names, lands, kept)]

    dx, dxb, g1 = _layer_backward(dx, dxb, p1, big1, saved1, 1)
    round1 = reduce_begin(BIG, [halves_view(k, g1[k]) for k in BIG], "grad1_")
    p0["norm_ffn"] = p0["norm_ffn"] + round1[3][0, 0]

    early = FFN_BIG + ("w_out",)
    middle = ("s5_w_glu", "s5_bc")
    bc_rows = 2 * DEPTH * S5_GROUP * S5_GROUPS

    def send_early(g_so_far):
        pending["early"] = reduce_begin(early, [halves_view(k, g_so_far[k]) for k in early], "grad0a_")
        return pending["early"][3]

    def send_middle(g_so_far):
        rows = lambda names: jnp.stack([a for layer in (g_so_far, g1) for a in (layer[names[0]], layer[names[1]])]
                                       ).reshape(bc_rows, S5_STATE)
        bc = jnp.stack([rows(("s5_b_re", "s5_b_im")), rows(("s5_c_re", "s5_c_im"))])[None]
        pending["middle"] = reduce_begin(middle, [halves_view("s5_w_glu", g_so_far["s5_w_glu"]), bc], "grad0b_")
        return pending["middle"][3]

    grad_x, _, g0 = _layer_backward(dx, dxb, p0, big0, saved0, 0, send_early, send_middle)
    g = {k: [g0[k], g1[k]] for k in LAYER_SMALL}
    reduced1 = dict(zip(BIG, reduce_end(BIG, round1, grad_x, "grad1_")))
    shared1 = dict(zip(BIG, _sibling_swap([reduced1[k] for k in BIG], "grad1_share")))
    round0 = reduce_begin(("w_in",), [halves_view("w_in", g0["w_in"])], "grad0c_")

    delta, new_m, new_v, grads = {}, {}, {}, {}
    adam1 = {}
    layered = tuple(k for k in BIG if k not in COL_SHARDED)
    for k in layered:
        adam1[k] = _adamw_layer(stored(k, w[k]), reduced1[k], shared1[k], stored(k, m[k]), stored(k, v[k]), 1,
                                [round0[3]], "adamw1_" + k)
    follow = adam1[layered[-1]][0]
    reduced0 = dict(zip(early, reduce_end(early, pending["early"], follow, "grad0a_")))
    reduced0.update(zip(middle, reduce_end(middle, pending["middle"], follow, "grad0b_")))
    tiny_names = TINY + ("norm_final",)
    parts = [jnp.stack(g[k]) for k in TINY] + [g_final, loss_part.reshape(1)]
    shapes = [p.shape for p in parts]
    small_blocks = [_flat_pack(parts), reduced0["s5_bc"]]
    small_sems, small_kept, small_lands, small_token = _gather_start(small_blocks, follow, "gather_small_start")
    reduced0.update(zip(("w_in",), reduce_end(("w_in",), round0, small_token, "grad0c_")))
    shared0 = dict(zip(BIG, _sibling_swap([reduced0[k] for k in BIG], "grad0_share")))
    for k in layered:
        outs = _adamw_layer(stored(k, w[k]), reduced0[k], shared0[k], stored(k, m[k]), stored(k, v[k]), 0, adam1[k],
                            "adamw0_" + k)
        delta[k], new_m[k], new_v[k], grads[k] = (stored(k, a) for a in outs)
    both = lambda mine, sib: jnp.where(cc == 0, jnp.concatenate([mine, sib]), jnp.concatenate([sib, mine]))
    grads["w_in"] = jnp.stack([both(reduced0["w_in"], shared0["w_in"]), both(reduced1["w_in"], shared1["w_in"])])
    outs = _adamw_rows(*[_lane_dense(a)[None] for a in (w["w_in"], grads["w_in"], m["w_in"], v["w_in"])],
                       "adamw_w_in")
    delta["w_in"], new_m["w_in"], new_v["w_in"] = (_from_lane_dense(a[0]) for a in outs)

    last = delta["w_in"]
    small_sems, small_lands = _gather_forward(small_sems, small_kept, small_lands, last, "gather_small_forward")
    small_lands = _gather_finish(small_sems, small_lands, last, "gather_small_finish")
    allparts, bc_eighths = (lax.dynamic_update_index_in_dim(a, b, me, 0) for a, b in zip(small_lands, small_blocks))
    unpacked = _flat_unpack(_sum_leading(allparts, "sum_small"), shapes)
    loss = unpacked[-1][0]
    grads.update(zip(tiny_names, unpacked[:-1]))
    width = SSD_CONV_DIM // 4
    grads["ssd_conv_w"] = lax.dynamic_slice_in_dim(grads["ssd_conv_w"], chip * width, width, axis=2)
    bc = bc_eighths.reshape(4, 2, bc_rows // 4, S5_STATE)
    b_sum = bc[:, 0].reshape(DEPTH, 2, S5_GROUP, S5_GROUPS, S5_STATE)
    c_sum = bc[:, 1].reshape(DEPTH, 2, S5_GROUPS, S5_GROUP, S5_STATE)
    grads["s5_c_re"] = c_sum[:, 0]
    grads["s5_c_im"] = c_sum[:, 1]

    b_names = ("s5_b_re", "s5_b_im")
    hp = lambda a: a.transpose(0, 1, 3, 2)
    names = tiny_names + ("s5_c_re", "s5_c_im") + b_names
    view = lambda k, a: hp(a) if k in b_names else (a.reshape(1, -1) if a.ndim == 1 else a)
    g_view = {k: view(k, grads[k]) for k in names if k not in b_names}
    g_view.update({k: b_sum[:, j].transpose(0, 2, 1, 3) for j, k in enumerate(b_names)})
    ds, nms, nvs = _adamw_many([view(k, w[k]) for k in names], [g_view[k] for k in names],
                               [view(k, m[k]) for k in names], [view(k, v[k]) for k in names], "adamw_small")
    for k, a, b, c in zip(names, ds, nms, nvs):
        if k in b_names:
            delta[k], new_m[k], new_v[k], grads[k] = hp(a), hp(b), hp(c), hp(g_view[k])
        else:
            delta[k], new_m[k], new_v[k] = (t.reshape(w[k].shape) for t in (a, b, c))

    return (loss, grad_x[None], *[grads[k] for k in WEIGHTS], *[delta[k] for k in WEIGHTS],
            *[new_m[k] for k in WEIGHTS], *[new_v[k] for k in WEIGHTS])
```

```python
import functools
import math

import jax
import jax.numpy as jnp
from jax import lax
from jax.experimental import pallas as pl
from jax.experimental.pallas import tpu as pltpu

F32 = jnp.float32
BF16 = jnp.bfloat16
MESH = pl.DeviceIdType.MESH
ANY = pl.BlockSpec(memory_space=pl.ANY)

D_MODEL = 1024
DEPTH = 2
S5_GROUPS = 64
S5_GROUP = 16
S5_STATE = 64
S5_COLS = S5_GROUPS * S5_STATE
S5_TILE_GROUPS = 8
S5_TILES = S5_GROUPS // S5_TILE_GROUPS
S5_TILE_IN = S5_TILE_GROUPS * S5_GROUP
S5_TILE_ST = S5_TILE_GROUPS * S5_STATE
SEGS = 8
SSD_HEADS = 16
SSD_HEAD_DIM = 64
SSD_GROUPS = 2
SSD_GROUP_HEADS = SSD_HEADS // SSD_GROUPS
SSD_STATE = 128
SSD_CONV = 4
SSD_CHUNK = 128
SSD_WIDTH = 1024
SSD_CONV_DIM = SSD_WIDTH + 2 * SSD_GROUPS * SSD_STATE
IN_PROJ = 3600
IN_MAIN = 3584
IN_PAD = IN_MAIN + 2 * 128
FFN = 2816
EPS = 1e-6
LANES = 128
ROW_TILE = 256
WIDE_ROW_TILE = 512

ADAM_LR = 0.001
ADAM_B1 = 0.9
ADAM_B2 = 0.999
ADAM_EPS = 1e-08
ADAM_WD = 0.01
ADAM_STEP = 10


def _sigmoid(x):
    return 1.0 / (1.0 + jnp.exp(-x))


def _silu(x):
    return x * _sigmoid(x)


def _dsilu(x):
    s = _sigmoid(x)
    return s * (1.0 + x * (1.0 - s))


_GELU_K = math.sqrt(2.0 / math.pi)
_GELU_C = 0.044715


def _gelu(x):
    t = jnp.tanh(_GELU_K * (x + _GELU_C * x * x * x))
    return 0.5 * x * (1.0 + t)


def _dgelu(x):
    t = jnp.tanh(_GELU_K * (x + _GELU_C * x * x * x))
    return 0.5 * (1.0 + t) + 0.5 * x * (1.0 - t * t) * _GELU_K * (1.0 + 3.0 * _GELU_C * x * x)


def _softplus(x):
    e = jnp.exp(-jnp.abs(x))
    u = 1.0 + e
    log1p = jnp.where(u == 1.0, e, jnp.log(u) * e / jnp.where(u == 1.0, 1.0, u - 1.0))
    return jnp.maximum(x, 0.0) + log1p


def _rstd(x):
    return lax.rsqrt(jnp.mean(x * x, axis=-1, keepdims=True) + EPS)


def _rms_bwd(x, r, gain, dy):
    dyg = dy * gain
    dx = r * dyg - x * (r * r * r) * jnp.mean(x * dyg, axis=-1, keepdims=True)
    dgain = jnp.sum(dy * x * r, axis=0, keepdims=True)
    return dx, dgain


def _dot(a, b):
    return jnp.dot(a, b, preferred_element_type=F32)


def _dot_nt(a, b):
    return lax.dot_general(a, b, (((1,), (1,)), ((), ())), preferred_element_type=F32)


def _dot_tn(a, b):
    return lax.dot_general(a, b, (((0,), (0,)), ((), ())), preferred_element_type=F32)


def _row_spec(tile, cols):
    return pl.BlockSpec((tile, cols), lambda i: (i, 0))


def _full_spec(shape):
    nd = len(shape)
    return pl.BlockSpec(shape, lambda *_: (0,) * nd)


def _const_spec(shape):
    nd = len(shape)
    return pl.BlockSpec(shape, lambda *_: (0,) * nd, pipeline_mode=pl.Buffered(1))


def _layer_spec(shape, layer, block=0):
    if layer is None:
        return pl.BlockSpec(tuple(shape), lambda *_: (block, 0), pipeline_mode=pl.Buffered(1))
    return pl.BlockSpec((None,) + tuple(shape), lambda *_: (layer, block, 0), pipeline_mode=pl.Buffered(1))


def _acc_rows(ref, val, first):
    @pl.when(first)
    def _():
        ref[...] = val

    @pl.when(jnp.logical_not(first))
    def _():
        ref[...] += val


def _pick_tile(n, cap):
    best = LANES
    for t in range(LANES, cap + 1, LANES):
        if n % t == 0:
            best = t
    return best


def _mm_tn(a, b, name):
    k, m = a.shape
    _, n = b.shape
    tm = _pick_tile(m, 1536)
    tn = _pick_tile(n, 1536)

    def body(a_ref, b_ref, o_ref):
        o_ref[...] = _dot_tn(a_ref[...], b_ref[...]).astype(BF16)

    return pl.pallas_call(
        body, name=name, grid=(n // tn, m // tm),
        in_specs=[pl.BlockSpec((k, tm), lambda j, i: (0, i)), pl.BlockSpec((k, tn), lambda j, i: (0, j))],
        out_specs=pl.BlockSpec((tm, tn), lambda j, i: (i, j)),
        out_shape=jax.ShapeDtypeStruct((m, n), BF16),
    )(a, b)


def _rms_inproj(x, gain, w_pad, layer, name):
    L = x.shape[0]

    def body(x_ref, g_ref, w_ref, u_ref, z_ref, xbc_ref, dt_ref, h_ref):
        xv = x_ref[...]
        h = (xv * _rstd(xv) * g_ref[...]).astype(BF16)
        h_ref[...] = h
        p = _dot(h, w_ref[...])
        u_ref[...] = p[:, :1024]
        z_ref[...] = p[:, 1024:2048]
        xbc_ref[...] = p[:, 2048:IN_MAIN]
        dt_ref[...] = p[:, IN_MAIN:IN_PAD]

    tile = min(L, WIDE_ROW_TILE)
    return pl.pallas_call(
        body, name=name, grid=(L // tile,),
        in_specs=[_row_spec(tile, D_MODEL), _full_spec((1, D_MODEL)), _layer_spec((D_MODEL, IN_PAD), layer)],
        out_specs=[_row_spec(tile, 1024), _row_spec(tile, 1024), _row_spec(tile, SSD_CONV_DIM),
                   _row_spec(tile, 256), _row_spec(tile, D_MODEL)],
        out_shape=[jax.ShapeDtypeStruct((L, 1024), F32), jax.ShapeDtypeStruct((L, 1024), F32),
                   jax.ShapeDtypeStruct((L, SSD_CONV_DIM), F32), jax.ShapeDtypeStruct((L, 256), F32),
                   jax.ShapeDtypeStruct((L, D_MODEL), BF16)],
    )(x, gain, w_pad)


def _s5_prep_math(lr, li, ls, bre, bim):
    step = jnp.exp(ls)
    mag = jnp.exp(lr * step)
    ang = li * step
    are = mag * jnp.cos(ang)
    aim = mag * jnp.sin(ang)
    den = lr * lr + li * li
    nr = are - 1.0
    ni = aim
    cre = (nr * lr + ni * li) / den
    cim = (ni * lr - nr * li) / den
    bbre = cre[None] * bre - cim[None] * bim
    bbim = cre[None] * bim + cim[None] * bre
    return are, aim, bbre, bbim


def _s5_prep(lr, li, ls, bre, bim, name):
    def body(lr_ref, li_ref, ls_ref, bre_ref, bim_ref, are_ref, aim_ref, bbre_ref, bbim_ref):
        are, aim, bbre, bbim = _s5_prep_math(lr_ref[...], li_ref[...], ls_ref[...], bre_ref[...], bim_ref[...])
        are_ref[...] = are
        aim_ref[...] = aim
        bbre_ref[...] = bbre
        bbim_ref[...] = bbim

    gp = jax.ShapeDtypeStruct((S5_GROUPS, S5_STATE), F32)
    hgp = jax.ShapeDtypeStruct((S5_GROUP, S5_GROUPS, S5_STATE), F32)
    return pl.pallas_call(body, name=name, out_shape=[gp, gp, hgp, hgp])(lr, li, ls, bre, bim)


def _s5_prep_bwd(lr, li, ls, bre, bim, dare, daim, dbbre, dbbim, name):
    def body(lr_ref, li_ref, ls_ref, bre_ref, bim_ref, dare_ref, daim_ref, dbbre_ref, dbbim_ref,
             dlr_ref, dli_ref, dls_ref, dbre_ref, dbim_ref):
        _, vjp = jax.vjp(_s5_prep_math, lr_ref[...], li_ref[...], ls_ref[...], bre_ref[...], bim_ref[...])
        dlr, dli, dls, dbre, dbim = vjp((dare_ref[...], daim_ref[...], dbbre_ref[...], dbbim_ref[...]))
        dlr_ref[...] = dlr
        dli_ref[...] = dli
        dls_ref[...] = dls
        dbre_ref[...] = dbre
        dbim_ref[...] = dbim

    gp = jax.ShapeDtypeStruct((S5_GROUPS, S5_STATE), F32)
    g1 = jax.ShapeDtypeStruct((S5_GROUPS, 1), F32)
    hgp = jax.ShapeDtypeStruct((S5_GROUP, S5_GROUPS, S5_STATE), F32)
    return pl.pallas_call(body, name=name, out_shape=[gp, gp, g1, hgp, hgp])(
        lr, li, ls, bre, bim, dare, daim, dbbre, dbbim)


def _cmul_add(ar, ai, sr, si, br, bi):
    return ar * sr - ai * si + br, ar * si + ai * sr + bi


def _shift_rows_down(v):
    rolled = pltpu.roll(v, 1, 0)
    row = lax.broadcasted_iota(jnp.int32, v.shape, 0)
    return jnp.where(row == 0, 0.0, rolled)


def _shift_rows_up(v):
    rolled = pltpu.roll(v, SEGS - 1, 0)
    row = lax.broadcasted_iota(jnp.int32, v.shape, 0)
    return jnp.where(row == SEGS - 1, 0.0, rolled)


def _segment_power(ar, ai, steps):
    n = 1
    while n < steps:
        ar, ai = ar * ar - ai * ai, 2.0 * ar * ai
        n *= 2
    assert n == steps
    return ar, ai


def _half_segment_entries(ar, ai, first, second, half_steps, shift):
    pr, pi = _segment_power(ar, ai, half_steps)
    er = jnp.zeros_like(first[0])
    ei = jnp.zeros_like(first[1])
    for _ in range(SEGS - 1):
        mr, mi = _cmul_add(pr, pi, er, ei, *first)
        nr, ni = _cmul_add(pr, pi, mr, mi, *second)
        er, ei = shift(nr), shift(ni)
    mr, mi = _cmul_add(pr, pi, er, ei, *first)
    return (er, ei), (mr, mi)


def _s5_scan(u_perm, bre_bd, bim_bd, cre_bd, cim_bd, are, aim, name):
    L = u_perm.shape[0]
    half = L // SEGS // 2

    def body(u_ref, bre_ref, bim_ref, cre_ref, cim_ref, are_ref, aim_ref, y_ref, xr_ref, xi_ref):
        u = u_ref[...].astype(BF16)
        xr_ref[...] = _dot(u, bre_ref[0])
        xi_ref[...] = _dot(u, bim_ref[0])
        ar = jnp.broadcast_to(are_ref[0], (SEGS, S5_TILE_ST))
        ai = jnp.broadcast_to(aim_ref[0], (SEGS, S5_TILE_ST))
        zero = jnp.zeros((SEGS, S5_TILE_ST), F32)
        block = lambda j: pl.ds(pl.multiple_of(j * SEGS, SEGS), SEGS)

        def finals(j, c):
            lo, hi = block(j), block(j + half)
            return (*_cmul_add(ar, ai, c[0], c[1], xr_ref[lo, :], xi_ref[lo, :]),
                    *_cmul_add(ar, ai, c[2], c[3], xr_ref[hi, :], xi_ref[hi, :]))

        f = lax.fori_loop(0, half, finals, (zero,) * 4, unroll=4)
        e_lo, e_hi = _half_segment_entries(ar, ai, f[:2], f[2:], half, _shift_rows_down)

        def scan(j, c):
            lo, hi = block(j), block(j + half)
            s_lo = _cmul_add(ar, ai, c[0], c[1], xr_ref[lo, :], xi_ref[lo, :])
            s_hi = _cmul_add(ar, ai, c[2], c[3], xr_ref[hi, :], xi_ref[hi, :])
            xr_ref[lo, :], xi_ref[lo, :] = s_lo
            xr_ref[hi, :], xi_ref[hi, :] = s_hi
            return (*s_lo, *s_hi)

        lax.fori_loop(0, half, scan, (*e_lo, *e_hi), unroll=4)
        y_ref[...] = (_dot(xr_ref[...].astype(BF16), cre_ref[0]) - _dot(xi_ref[...].astype(BF16), cim_ref[0]))

    tile3 = lambda a, b: pl.BlockSpec((1, a, b), lambda k: (k, 0, 0))
    return pl.pallas_call(
        body, name=name, grid=(S5_TILES,),
        in_specs=[pl.BlockSpec((L, S5_TILE_IN), lambda k: (0, k)),
                  tile3(S5_TILE_IN, S5_TILE_ST), tile3(S5_TILE_IN, S5_TILE_ST),
                  tile3(S5_TILE_ST, S5_TILE_IN), tile3(S5_TILE_ST, S5_TILE_IN),
                  tile3(1, S5_TILE_ST), tile3(1, S5_TILE_ST)],
        out_specs=[pl.BlockSpec((L, S5_TILE_IN), lambda k: (0, k)),
                   pl.BlockSpec((L, S5_TILE_ST), lambda k: (0, k)), pl.BlockSpec((L, S5_TILE_ST), lambda k: (0, k))],
        out_shape=[jax.ShapeDtypeStruct((L, 1024), F32), jax.ShapeDtypeStruct((L, S5_COLS), F32),
                   jax.ShapeDtypeStruct((L, S5_COLS), F32)],
    )(u_perm, bre_bd, bim_bd, cre_bd, cim_bd, are, aim)


def _s5_scan_bwd(dy_perm, u_perm, xr, xi, bret_bd, bimt_bd, cret_bd, cimt_bd, are, aim, name):
    L = u_perm.shape[0]
    steps = L // SEGS
    half = steps // 2

    def body(dy_ref, u_ref, xr_ref, xi_ref, bret_ref, bimt_ref, cret_ref, cimt_ref, are_ref, aim_ref,
             du_ref, dar_ref, dai_ref, dcre_ref, dcim_ref, dbre_ref, dbim_ref, gr_ref, gi_ref):
        dy = dy_ref[...].astype(BF16)
        u = u_ref[...].astype(BF16)
        gr_ref[...] = _dot(dy, cret_ref[0])
        gi_ref[...] = -_dot(dy, cimt_ref[0])
        ar = jnp.broadcast_to(are_ref[0], (SEGS, S5_TILE_ST))
        ai = -jnp.broadcast_to(aim_ref[0], (SEGS, S5_TILE_ST))
        zero = jnp.zeros((SEGS, S5_TILE_ST), F32)
        block = lambda j: pl.ds(pl.multiple_of(j * SEGS, SEGS), SEGS)

        def finals(k, c):
            hi, lo = block(steps - 1 - k), block(half - 1 - k)
            return (*_cmul_add(ar, ai, c[0], c[1], gr_ref[hi, :], gi_ref[hi, :]),
                    *_cmul_add(ar, ai, c[2], c[3], gr_ref[lo, :], gi_ref[lo, :]))

        f = lax.fori_loop(0, half, finals, (zero,) * 4, unroll=4)
        e_hi, e_lo = _half_segment_entries(ar, ai, f[:2], f[2:], half, _shift_rows_up)

        def scan(k, c):
            accr, acci = c[4], c[5]
            j_hi, j_lo = steps - 1 - k, half - 1 - k
            hi, lo = block(j_hi), block(j_lo)
            hr, hi_im = _cmul_add(ar, ai, c[0], c[1], gr_ref[hi, :], gi_ref[hi, :])
            lr, lo_im = _cmul_add(ar, ai, c[2], c[3], gr_ref[lo, :], gi_ref[lo, :])
            gr_ref[hi, :], gi_ref[hi, :] = hr, hi_im
            gr_ref[lo, :], gi_ref[lo, :] = lr, lo_im
            before_hi = block(j_hi - 1)
            before_lo = block(jnp.maximum(j_lo - 1, 0))
            live = (j_lo > 0).astype(F32)
            xhr, xhi = xr_ref[before_hi, :], xi_ref[before_hi, :]
            xlr, xli = xr_ref[before_lo, :] * live, xi_ref[before_lo, :] * live
            accr = accr + (hr * xhr + hi_im * xhi) + (lr * xlr + lo_im * xli)
            acci = acci + (hi_im * xhr - hr * xhi) + (lo_im * xlr - lr * xli)
            return hr, hi_im, lr, lo_im, accr, acci

        out = lax.fori_loop(0, half, scan, (*e_hi, *e_lo, zero, zero), unroll=2)
        accr, acci = out[4], out[5]
        first = pl.ds(0, SEGS)
        last = pl.ds((steps - 1) * SEGS, SEGS)
        xpr = _shift_rows_down(xr_ref[last, :])
        xpi = _shift_rows_down(xi_ref[last, :])
        g0r = gr_ref[first, :]
        g0i = gi_ref[first, :]
        accr = accr + g0r * xpr + g0i * xpi
        acci = acci + g0i * xpr - g0r * xpi
        dar_ref[0] = jnp.sum(accr, axis=0, keepdims=True)
        dai_ref[0] = jnp.sum(acci, axis=0, keepdims=True)

        grb = gr_ref[...].astype(BF16)
        gib = gi_ref[...].astype(BF16)
        du_ref[...] = _dot(grb, bret_ref[0]) + _dot(gib, bimt_ref[0])
        dbre_ref[0] = _dot_tn(u, grb)
        dbim_ref[0] = _dot_tn(u, gib)
        dcre_ref[0] = _dot_tn(dy, xr_ref[...].astype(BF16))
        dcim_ref[0] = -_dot_tn(dy, xi_ref[...].astype(BF16))

    tile3 = lambda a, b: pl.BlockSpec((1, a, b), lambda k: (k, 0, 0))
    col_in = pl.BlockSpec((L, S5_TILE_IN), lambda k: (0, k))
    col_st = pl.BlockSpec((L, S5_TILE_ST), lambda k: (0, k))
    dense = jax.ShapeDtypeStruct((S5_TILES, S5_TILE_IN, S5_TILE_ST), F32)
    vec = jax.ShapeDtypeStruct((S5_TILES, 1, S5_TILE_ST), F32)
    return pl.pallas_call(
        body, name=name, grid=(S5_TILES,),
        in_specs=[col_in, col_in, col_st, col_st,
                  tile3(S5_TILE_ST, S5_TILE_IN), tile3(S5_TILE_ST, S5_TILE_IN),
                  tile3(S5_TILE_IN, S5_TILE_ST), tile3(S5_TILE_IN, S5_TILE_ST),
                  tile3(1, S5_TILE_ST), tile3(1, S5_TILE_ST)],
        out_specs=[col_in, tile3(1, S5_TILE_ST), tile3(1, S5_TILE_ST),
                   tile3(S5_TILE_IN, S5_TILE_ST), tile3(S5_TILE_IN, S5_TILE_ST),
                   tile3(S5_TILE_IN, S5_TILE_ST), tile3(S5_TILE_IN, S5_TILE_ST)],
        out_shape=[jax.ShapeDtypeStruct((L, 1024), F32), vec, vec, dense, dense, dense, dense],
        scratch_shapes=[pltpu.VMEM((L, S5_TILE_ST), F32), pltpu.VMEM((L, S5_TILE_ST), F32)],
    )(dy_perm, u_perm, xr, xi, bret_bd, bimt_bd, cret_bd, cimt_bd, are, aim)


def _s5_post(ys, u, d_skip, w_glu, b_glu, gain, layer, name):
    L = ys.shape[0]

    def body(ys_ref, u_ref, d_ref, w_ref, b_ref, g_ref, ya_ref):
        g = _gelu(ys_ref[...] + d_ref[...] * u_ref[...])
        q = _dot(g.astype(BF16), w_ref[...]) + b_ref[...]
        oa = g * _sigmoid(q)
        ya_ref[...] = (oa * _rstd(oa) * g_ref[...]).astype(BF16)

    vec = _full_spec((1, 1024))
    tile = min(L, WIDE_ROW_TILE)
    return pl.pallas_call(
        body, name=name, grid=(L // tile,),
        in_specs=[_row_spec(tile, 1024), _row_spec(tile, 1024), vec, _layer_spec((1024, 1024), layer), vec, vec],
        out_specs=_row_spec(tile, 1024),
        out_shape=jax.ShapeDtypeStruct((L, 1024), BF16),
    )(ys, u, d_skip, w_glu, b_glu, gain)


def _s5_post_bwd(dx, w_out, ys, u, d_skip, w_glu, b_glu, gain, layer, name):
    L = ys.shape[0]

    def body(dx_ref, wo_ref, ys_ref, u_ref, d_ref, w_ref, b_ref, gn_ref,
             dys_ref, dus_ref, g_ref, dq_ref, dgain_ref, dd_ref, db_ref):
        first = pl.program_id(0) == 0
        uv = u_ref[...]
        yt = ys_ref[...] + d_ref[...] * uv
        g = _gelu(yt)
        gb = g.astype(BF16)
        q = _dot(gb, w_ref[...]) + b_ref[...]
        s = _sigmoid(q)
        oa = g * s
        dya = _dot_nt(dx_ref[...], wo_ref[...])
        doa, dgain = _rms_bwd(oa, _rstd(oa), gn_ref[...], dya)
        dq = doa * g * s * (1.0 - s)
        dqb = dq.astype(BF16)
        dg = doa * s + _dot_nt(dqb, w_ref[...])
        dyt = dg * _dgelu(yt)
        dys_ref[...] = dyt
        dus_ref[...] = dyt * d_ref[...]
        g_ref[...] = gb
        dq_ref[...] = dqb
        _acc_rows(dgain_ref, dgain, first)
        _acc_rows(dd_ref, jnp.sum(dyt * uv, axis=0, keepdims=True), first)
        _acc_rows(db_ref, jnp.sum(dq, axis=0, keepdims=True), first)

    vec = _full_spec((1, 1024))
    row = _row_spec(ROW_TILE, 1024)
    vshape = jax.ShapeDtypeStruct((1, 1024), F32)
    return pl.pallas_call(
        body, name=name, grid=(L // ROW_TILE,),
        in_specs=[row, _layer_spec((1024, 1024), layer, 0), row, row, vec, _layer_spec((1024, 1024), layer), vec,
                  vec],
        out_specs=[row, row, row, row, vec, vec, vec],
        out_shape=[jax.ShapeDtypeStruct((L, 1024), F32), jax.ShapeDtypeStruct((L, 1024), F32),
                   jax.ShapeDtypeStruct((L, 1024), BF16), jax.ShapeDtypeStruct((L, 1024), BF16),
                   vshape, vshape, vshape],
    )(dx, w_out, ys, u, d_skip, w_glu, b_glu, gain)


CONV_TILE = 256


def _shift_time(v, d):
    if d == 0:
        return v
    rolled = pltpu.roll(v, d, 0)
    row = lax.broadcasted_iota(jnp.int32, v.shape, 0)
    return jnp.where(row < d, 0.0, rolled)


def _unshift_time(v, d):
    if d == 0:
        return v
    n = v.shape[0]
    rolled = pltpu.roll(v, n - d, 0)
    row = lax.broadcasted_iota(jnp.int32, v.shape, 0)
    return jnp.where(row >= n - d, 0.0, rolled)


def _ssd_conv(xbc, w, b, name):
    L = xbc.shape[0]

    def body(x_ref, w_ref, b_ref, o_ref):
        xv = x_ref[...]
        pre = jnp.broadcast_to(b_ref[...], xv.shape)
        for k in range(SSD_CONV):
            pre = pre + w_ref[k:k + 1, :] * _shift_time(xv, SSD_CONV - 1 - k)
        o_ref[...] = _silu(pre)

    col = pl.BlockSpec((L, CONV_TILE), lambda j: (0, j))
    return pl.pallas_call(
        body, name=name, grid=(SSD_CONV_DIM // CONV_TILE,),
        in_specs=[col, pl.BlockSpec((8, CONV_TILE), lambda j: (0, j)), pl.BlockSpec((1, CONV_TILE), lambda j: (0, j))],
        out_specs=col, out_shape=jax.ShapeDtypeStruct((L, SSD_CONV_DIM), F32),
    )(xbc, w, b)


def _ssd_conv_bwd(dxc, xbc, w, b, name):
    L = xbc.shape[0]

    def body(d_ref, x_ref, w_ref, b_ref, dx_ref, dw_ref, db_ref):
        xv = x_ref[...]
        shifted = [_shift_time(xv, SSD_CONV - 1 - k) for k in range(SSD_CONV)]
        pre = jnp.broadcast_to(b_ref[...], xv.shape)
        for k in range(SSD_CONV):
            pre = pre + w_ref[k:k + 1, :] * shifted[k]
        dpre = d_ref[...] * _dsilu(pre)
        dx = jnp.zeros_like(xv)
        rows = []
        for k in range(SSD_CONV):
            dx = dx + w_ref[k:k + 1, :] * _unshift_time(dpre, SSD_CONV - 1 - k)
            rows.append(jnp.sum(dpre * shifted[k], axis=0, keepdims=True))
        dx_ref[...] = dx
        dw_ref[...] = jnp.concatenate(rows + [jnp.zeros((8 - SSD_CONV, CONV_TILE), F32)], axis=0)
        db_ref[...] = jnp.sum(dpre, axis=0, keepdims=True)

    col = pl.BlockSpec((L, CONV_TILE), lambda j: (0, j))
    w_spec = pl.BlockSpec((8, CONV_TILE), lambda j: (0, j))
    b_spec = pl.BlockSpec((1, CONV_TILE), lambda j: (0, j))
    return pl.pallas_call(
        body, name=name, grid=(SSD_CONV_DIM // CONV_TILE,),
        in_specs=[col, col, w_spec, b_spec], out_specs=[col, w_spec, b_spec],
        out_shape=[jax.ShapeDtypeStruct((L, SSD_CONV_DIM), F32), jax.ShapeDtypeStruct((8, SSD_CONV_DIM), F32),
                   jax.ShapeDtypeStruct((1, SSD_CONV_DIM), F32)],
    )(dxc, xbc, w, b)


def _tri(lower):
    r = lax.broadcasted_iota(jnp.int32, (SSD_CHUNK, SSD_CHUNK), 0)
    c = lax.broadcasted_iota(jnp.int32, (SSD_CHUNK, SSD_CHUNK), 1)
    return (r >= c) if lower else (r <= c)


def _ssd_chunk_common(dt_ref, bias_ref, alog_ref):
    pre = dt_ref[...] + bias_ref[0]
    dtp = _softplus(pre)
    a_neg = -jnp.exp(alog_ref[0])
    dta = dtp * a_neg
    acum = _select_rows(_tri(True), dta)
    return pre, dtp, a_neg, dta, acum


GROUP_W = SSD_GROUP_HEADS * SSD_HEAD_DIM


def _head_expander():
    r = lax.broadcasted_iota(jnp.int32, (LANES, GROUP_W), 0)
    c = lax.broadcasted_iota(jnp.int32, (LANES, GROUP_W), 1)
    return (c // SSD_HEAD_DIM == r).astype(F32)


def _split_bf16(a, terms):
    parts = []
    rest = a
    for _ in range(terms):
        piece = rest.astype(BF16)
        parts.append(piece)
        rest = rest - piece.astype(F32)
    return parts


def _select_cols(a, sel, terms=3):
    lhs = jnp.concatenate(_split_bf16(a, terms), axis=1)
    rhs = jnp.concatenate([sel.astype(BF16)] * terms, axis=0)
    return _dot(lhs, rhs)


def _select_rows(sel, b, terms=3):
    lhs = jnp.concatenate([sel.astype(BF16)] * terms, axis=1)
    rhs = jnp.concatenate(_split_bf16(b, terms), axis=0)
    return _dot(lhs, rhs)


def _decay_mask(acum_all, acum_t, h, lower):
    seg = acum_all[:, h:h + 1] - acum_t[h:h + 1, :]
    return jnp.where(lower, jnp.exp(jnp.minimum(seg, 0.0)), 0.0)


def _ssd_scan(xc, dt, dt_bias, a_log, d_wide, expand, expand_t, name):
    L = xc.shape[0]
    nc = L // SSD_CHUNK

    def body(x_ref, b_ref, c_ref, dt_ref, bias_ref, alog_ref, d_ref, e_ref, et_ref, y_ref, sp_ref, s_ref, xdt_ref):
        @pl.when(pl.program_id(1) == 0)
        def _():
            s_ref[...] = jnp.zeros_like(s_ref)

        _, dtp_all, _, _, acum_all = _ssd_chunk_common(dt_ref, bias_ref, alog_ref)
        acum_t = acum_all.T
        wide = _select_cols(jnp.concatenate([acum_all, dtp_all], axis=0), e_ref[...])
        acum_e = wide[:SSD_CHUNK]
        alast_e = acum_e[SSD_CHUNK - 1:SSD_CHUNK, :]
        x = x_ref[...]
        xdt = x * wide[SSD_CHUNK:]
        xdt_ref[...] = xdt.astype(BF16)
        bm = b_ref[...].astype(BF16)
        cm = c_ref[...].astype(BF16)
        cb = _dot_nt(cm, bm)
        lower = _tri(True)
        sp = s_ref[...]
        for h in range(SSD_GROUP_HEADS):
            cols = slice(h * SSD_HEAD_DIM, (h + 1) * SSD_HEAD_DIM)
            lm = _decay_mask(acum_all, acum_t, h, lower)
            y_ref[:, cols] = _dot((cb * lm).astype(BF16), xdt_ref[:, cols])
        y_ref[...] += _dot_nt(cm, sp.astype(BF16)) * jnp.exp(acum_e) + d_ref[0] * x
        wgt = xdt * jnp.exp(alast_e - acum_e)
        ealast = jnp.exp(_select_rows(et_ref[...], acum_t)[:, SSD_CHUNK - 1:SSD_CHUNK])
        sp_ref[0, 0] = sp
        s_ref[...] = ealast * sp + _dot_tn(wgt.astype(BF16), bm)

    par = lambda n: pl.BlockSpec((1, 1, n), lambda g, c: (g, 0, 0))
    return pl.pallas_call(
        body, name=name, grid=(SSD_GROUPS, nc),
        in_specs=[pl.BlockSpec((SSD_CHUNK, GROUP_W), lambda g, c: (c, g)),
                  pl.BlockSpec((SSD_CHUNK, SSD_STATE), lambda g, c: (c, 8 + g)),
                  pl.BlockSpec((SSD_CHUNK, SSD_STATE), lambda g, c: (c, 10 + g)),
                  pl.BlockSpec((SSD_CHUNK, LANES), lambda g, c: (c, g)),
                  par(LANES), par(LANES), par(GROUP_W), _full_spec((LANES, GROUP_W)), _full_spec((GROUP_W, LANES))],
        out_specs=[pl.BlockSpec((SSD_CHUNK, GROUP_W), lambda g, c: (c, g)),
                   pl.BlockSpec((1, 1, GROUP_W, SSD_STATE), lambda g, c: (c, g, 0, 0))],
        out_shape=[jax.ShapeDtypeStruct((L, SSD_WIDTH), F32),
                   jax.ShapeDtypeStruct((nc, SSD_GROUPS, GROUP_W, SSD_STATE), F32)],
        scratch_shapes=[pltpu.VMEM((GROUP_W, SSD_STATE), F32), pltpu.VMEM((SSD_CHUNK, GROUP_W), BF16)],
    )(xc, xc, xc, dt, dt_bias, a_log, d_wide, expand, expand_t)


def _ssd_scan_bwd(dy, xc, dt, sprev, dt_bias, a_log, d_wide, expand, expand_t, name):
    L = xc.shape[0]
    nc = L // SSD_CHUNK

    def body(dy_ref, x_ref, b_ref, c_ref, dt_ref, sp_ref, bias_ref, alog_ref, d_ref, e_ref, et_ref,
             dx_ref, db_ref, dc_ref, ddt_ref, dbias_ref, dalog_ref, dd_ref, ds_ref, xdt_ref, dyb_ref):
        first = pl.program_id(1) == 0

        @pl.when(first)
        def _():
            ds_ref[...] = jnp.zeros_like(ds_ref)

        pre, dtp_all, a_neg, _, acum_all = _ssd_chunk_common(dt_ref, bias_ref, alog_ref)
        acum_t = acum_all.T
        e = e_ref[...]
        et = et_ref[...]
        wide = _select_cols(jnp.concatenate([acum_all, dtp_all], axis=0), e)
        acum_e = wide[:SSD_CHUNK]
        dtp_e = wide[SSD_CHUNK:]
        alast_e = acum_e[SSD_CHUNK - 1:SSD_CHUNK, :]
        dstate_e = jnp.exp(alast_e - acum_e)
        x = x_ref[...]
        dy = dy_ref[...]
        xdt = x * dtp_e
        xdt_ref[...] = xdt.astype(BF16)
        dyb_ref[...] = dy.astype(BF16)
        bm = b_ref[...].astype(BF16)
        cm = c_ref[...].astype(BF16)
        cb = _dot_nt(cm, bm)
        sp = sp_ref[0, 0]
        spb = sp.astype(BF16)
        dsn = ds_ref[...]
        dsb = dsn.astype(BF16)
        z = _dot_nt(cm, spb)
        dz = dy * jnp.exp(acum_e)
        dzb = dz.astype(BF16)
        dc_acc = _dot(dzb, spb)
        ealast = jnp.exp(_select_rows(et, acum_t)[:, SSD_CHUNK - 1:SSD_CHUNK])
        ds_ref[...] = _dot_tn(dzb, cm) + ealast * dsn
        dw = _dot_nt(bm, dsb)
        wgt = xdt * dstate_e
        db_acc = _dot(wgt.astype(BF16), dsb)
        lower = _tri(True)
        lane = lax.broadcasted_iota(jnp.int32, (SSD_CHUNK, LANES), 1)
        row = lax.broadcasted_iota(jnp.int32, (SSD_CHUNK, LANES), 0)
        dcb = jnp.zeros((SSD_CHUNK, SSD_CHUNK), F32)
        dacum_all = jnp.zeros((SSD_CHUNK, LANES), F32)
        dacum_cols = jnp.zeros((SSD_CHUNK, LANES), F32)
        for h in range(SSD_GROUP_HEADS):
            cols = slice(h * SSD_HEAD_DIM, (h + 1) * SSD_HEAD_DIM)
            lm = _decay_mask(acum_all, acum_t, h, lower)
            dm = _dot_nt(dyb_ref[:, cols], xdt_ref[:, cols])
            dx_ref[:, cols] = _dot_tn((cb * lm).astype(BF16), dyb_ref[:, cols])
            dm_lm = dm * lm
            dcb = dcb + dm_lm
            q = dm_lm * cb
            dacum_all = jnp.where(lane == h, jnp.sum(q, axis=1, keepdims=True), dacum_all)
            dacum_cols = jnp.where(row == h, jnp.sum(q, axis=0, keepdims=True), dacum_cols)
        dxdt = dx_ref[...] + dw * dstate_e
        sums = _select_cols(jnp.concatenate([dz * z, dw * wgt, dxdt * x, dy * x], axis=0), et, terms=2)
        dacum_off = sums[0:SSD_CHUNK]
        dds_ds = sums[SSD_CHUNK:2 * SSD_CHUNK]
        ddtp_x = sums[2 * SSD_CHUNK:3 * SSD_CHUNK]
        dd_part = sums[3 * SSD_CHUNK:4 * SSD_CHUNK]
        ds_s = jnp.sum(_select_rows(e, dsn * sp, terms=2).T, axis=0, keepdims=True)
        dalast = ds_s * jnp.exp(acum_all[SSD_CHUNK - 1:SSD_CHUNK, :]) + jnp.sum(dds_ds, axis=0, keepdims=True)
        dacum_all = dacum_all - dacum_cols.T + dacum_off - dds_ds + jnp.where(row == SSD_CHUNK - 1, dalast, 0.0)
        dx_ref[...] = d_ref[0] * dy + dxdt * dtp_e
        dcbb = dcb.astype(BF16)
        dc_ref[...] = dc_acc + _dot(dcbb, bm)
        db_ref[...] = db_acc + _dot_tn(dcbb, cm)
        ddta = _select_rows(_tri(False), dacum_all)
        ddt = (ddtp_x + ddta * a_neg) * _sigmoid(pre)
        ddt_ref[...] = ddt
        _acc_rows(dbias_ref, jnp.sum(ddt, axis=0, keepdims=True)[None], first)
        _acc_rows(dalog_ref, (jnp.sum(ddta * dtp_all, axis=0, keepdims=True) * a_neg)[None], first)
        _acc_rows(dd_ref, jnp.sum(dd_part, axis=0, keepdims=True)[None], first)

    rev = lambda c: nc - 1 - c
    par = lambda n: pl.BlockSpec((1, 1, n), lambda g, c: (g, 0, 0))
    pshape = jax.ShapeDtypeStruct((SSD_GROUPS, 1, LANES), F32)
    return pl.pallas_call(
        body, name=name, grid=(SSD_GROUPS, nc),
        in_specs=[pl.BlockSpec((SSD_CHUNK, GROUP_W), lambda g, c: (rev(c), g)),
                  pl.BlockSpec((SSD_CHUNK, GROUP_W), lambda g, c: (rev(c), g)),
                  pl.BlockSpec((SSD_CHUNK, SSD_STATE), lambda g, c: (rev(c), 8 + g)),
                  pl.BlockSpec((SSD_CHUNK, SSD_STATE), lambda g, c: (rev(c), 10 + g)),
                  pl.BlockSpec((SSD_CHUNK, LANES), lambda g, c: (rev(c), g)),
                  pl.BlockSpec((1, 1, GROUP_W, SSD_STATE), lambda g, c: (rev(c), g, 0, 0)),
                  par(LANES), par(LANES), par(GROUP_W), _full_spec((LANES, GROUP_W)), _full_spec((GROUP_W, LANES))],
        out_specs=[pl.BlockSpec((SSD_CHUNK, GROUP_W), lambda g, c: (rev(c), g)),
                   pl.BlockSpec((SSD_CHUNK, SSD_STATE), lambda g, c: (rev(c), g)),
                   pl.BlockSpec((SSD_CHUNK, SSD_STATE), lambda g, c: (rev(c), g)),
                   pl.BlockSpec((SSD_CHUNK, LANES), lambda g, c: (rev(c), g)),
                   par(LANES), par(LANES), par(LANES)],
        out_shape=[jax.ShapeDtypeStruct((L, SSD_WIDTH), F32), jax.ShapeDtypeStruct((L, 256), F32),
                   jax.ShapeDtypeStruct((L, 256), F32), jax.ShapeDtypeStruct((L, 256), F32),
                   pshape, pshape, pshape],
        scratch_shapes=[pltpu.VMEM((GROUP_W, SSD_STATE), F32), pltpu.VMEM((SSD_CHUNK, GROUP_W), BF16),
                        pltpu.VMEM((SSD_CHUNK, GROUP_W), BF16)],
    )(dy, xc, xc, xc, dt, sprev, dt_bias, a_log, d_wide, expand, expand_t)


def _ssd_post(y, z, gain, name):
    L = y.shape[0]

    def body(y_ref, z_ref, g_ref, o_ref):
        ob = y_ref[...] * _silu(z_ref[...])
        o_ref[...] = (ob * _rstd(ob) * g_ref[...]).astype(BF16)

    tile = min(L, WIDE_ROW_TILE)
    row = _row_spec(tile, 1024)
    return pl.pallas_call(body, name=name, grid=(L // tile,), in_specs=[row, row, _full_spec((1, 1024))],
                          out_specs=row, out_shape=jax.ShapeDtypeStruct((L, 1024), BF16))(y, z, gain)


def _ssd_post_bwd(dx, w_out, y, z, gain, layer, name):
    L = y.shape[0]

    def body(dx_ref, wo_ref, y_ref, z_ref, g_ref, dy_ref, dz_ref, dgain_ref):
        first = pl.program_id(0) == 0
        yv = y_ref[...]
        zv = z_ref[...]
        sz = _silu(zv)
        ob = yv * sz
        dyb = _dot_nt(dx_ref[...], wo_ref[...])
        dob, dgain = _rms_bwd(ob, _rstd(ob), g_ref[...], dyb)
        dy_ref[...] = dob * sz
        dz_ref[...] = dob * yv * _dsilu(zv)
        _acc_rows(dgain_ref, dgain, first)

    row = _row_spec(ROW_TILE, 1024)
    vec = _full_spec((1, 1024))
    return pl.pallas_call(
        body, name=name, grid=(L // ROW_TILE,),
        in_specs=[row, _layer_spec((1024, 1024), layer, 1), row, row, vec],
        out_specs=[row, row, vec],
        out_shape=[jax.ShapeDtypeStruct((L, 1024), F32), jax.ShapeDtypeStruct((L, 1024), F32),
                   jax.ShapeDtypeStruct((1, 1024), F32)],
    )(dx, w_out, y, z, gain)


def _out_proj(x, ya, yb, w_out, layer, name):
    L = x.shape[0]

    def body(x_ref, ya_ref, yb_ref, w_ref, o_ref):
        o_ref[...] = x_ref[...] + _dot(ya_ref[...], w_ref[:1024, :]) + _dot(yb_ref[...], w_ref[1024:, :])

    tile = min(L, WIDE_ROW_TILE)
    row = _row_spec(tile, 1024)
    return pl.pallas_call(body, name=name, grid=(L // tile,),
                          in_specs=[row, row, row, _layer_spec((2048, 1024), layer)],
                          out_specs=row, out_shape=jax.ShapeDtypeStruct((L, D_MODEL), F32))(x, ya, yb, w_out)


def _ffn(x, gain, w_gate, w_up, w_down, layer, name):
    L = x.shape[0]

    def body(x_ref, g_ref, wg_ref, wu_ref, wd_ref, o_ref, gt_ref, up_ref):
        xv = x_ref[...]
        h = (xv * _rstd(xv) * g_ref[...]).astype(BF16)
        gt = _dot_nt(h, wg_ref[...])
        up = _dot_nt(h, wu_ref[...])
        gt_ref[...] = gt
        up_ref[...] = up
        o_ref[...] = xv + _dot((_silu(gt) * up).astype(BF16), wd_ref[...])

    row = _row_spec(ROW_TILE, D_MODEL)
    hid = _row_spec(ROW_TILE, FFN)
    return pl.pallas_call(
        body, name=name, grid=(L // ROW_TILE,),
        in_specs=[row, _full_spec((1, D_MODEL)), _layer_spec((FFN, D_MODEL), layer),
                  _layer_spec((FFN, D_MODEL), layer), _layer_spec((FFN, D_MODEL), layer)],
        out_specs=[row, hid, hid],
        out_shape=[jax.ShapeDtypeStruct((L, D_MODEL), F32), jax.ShapeDtypeStruct((L, FFN), F32),
                   jax.ShapeDtypeStruct((L, FFN), F32)],
    )(x, gain, w_gate, w_up, w_down)


def _ffn_bwd(dx2, x1, gt, up, gain, w_gate, w_up, w_down, layer, name):
    L = x1.shape[0]

    def body(d_ref, x_ref, gt_ref, up_ref, g_ref, wg_ref, wu_ref, wd_ref,
             dx_ref, dxb_ref, h_ref, act_ref, dgt_ref, dup_ref, dgain_ref):
        first = pl.program_id(0) == 0
        dv = d_ref[...]
        xv = x_ref[...]
        r = _rstd(xv)
        h_ref[...] = (xv * r * g_ref[...]).astype(BF16)
        gtv = gt_ref[...]
        upv = up_ref[...]
        sg = _silu(gtv)
        act_ref[...] = (sg * upv).astype(BF16)
        dact = _dot_nt(dv.astype(BF16), wd_ref[...])
        dgt = (dact * upv * _dsilu(gtv)).astype(BF16)
        dup = (dact * sg).astype(BF16)
        dgt_ref[...] = dgt
        dup_ref[...] = dup
        dh = _dot(dgt, wg_ref[...]) + _dot(dup, wu_ref[...])
        dxn, dgain = _rms_bwd(xv, r, g_ref[...], dh)
        dx = dv + dxn
        dx_ref[...] = dx
        dxb_ref[...] = dx.astype(BF16)
        _acc_rows(dgain_ref, dgain, first)

    row = _row_spec(ROW_TILE, D_MODEL)
    hid = _row_spec(ROW_TILE, FFN)
    vec = _full_spec((1, D_MODEL))
    return pl.pallas_call(
        body, name=name, grid=(L // ROW_TILE,),
        in_specs=[row, row, hid, hid, vec, _layer_spec((FFN, D_MODEL), layer), _layer_spec((FFN, D_MODEL), layer),
                  _layer_spec((FFN, D_MODEL), layer)],
        out_specs=[row, row, row, hid, hid, hid, vec],
        out_shape=[jax.ShapeDtypeStruct((L, D_MODEL), F32), jax.ShapeDtypeStruct((L, D_MODEL), BF16),
                   jax.ShapeDtypeStruct((L, D_MODEL), BF16),
                   jax.ShapeDtypeStruct((L, FFN), BF16), jax.ShapeDtypeStruct((L, FFN), BF16),
                   jax.ShapeDtypeStruct((L, FFN), BF16), jax.ShapeDtypeStruct((1, D_MODEL), F32)],
    )(dx2, x1, gt, up, gain, w_gate, w_up, w_down)


def _inproj_bwd(dx1, x0, du_skip, du_scan, dz, dxbc, ddt, gain, w_pad, layer, name):
    L = x0.shape[0]

    def body(d_ref, x_ref, dus_ref, duc_ref, dz_ref, dxbc_ref, ddt_ref, g_ref, w_ref,
             dx_ref, dxb_ref, dp_ref, dgain_ref):
        first = pl.program_id(0) == 0
        xv = x_ref[...]
        dp = jnp.concatenate([dus_ref[...] + duc_ref[...], dz_ref[...], dxbc_ref[...], ddt_ref[...]],
                             axis=1).astype(BF16)
        dp_ref[...] = dp
        dh = _dot_nt(dp, w_ref[...])
        dxn, dgain = _rms_bwd(xv, _rstd(xv), g_ref[...], dh)
        dx = d_ref[...] + dxn
        dx_ref[...] = dx
        dxb_ref[...] = dx.astype(BF16)
        _acc_rows(dgain_ref, dgain, first)

    row = _row_spec(ROW_TILE, D_MODEL)
    vec = _full_spec((1, D_MODEL))
    return pl.pallas_call(
        body, name=name, grid=(L // ROW_TILE,),
        in_specs=[row, row, row, row, row, _row_spec(ROW_TILE, SSD_CONV_DIM), _row_spec(ROW_TILE, 256), vec,
                  _layer_spec((D_MODEL, IN_PAD), layer)],
        out_specs=[row, row, _row_spec(ROW_TILE, IN_PAD), vec],
        out_shape=[jax.ShapeDtypeStruct((L, D_MODEL), F32), jax.ShapeDtypeStruct((L, D_MODEL), BF16),
                   jax.ShapeDtypeStruct((L, IN_PAD), BF16), jax.ShapeDtypeStruct((1, D_MODEL), F32)],
    )(dx1, x0, du_skip, du_scan, dz, dxbc, ddt, gain, w_pad)


def _final_loss(x, gain, target, name):
    L = x.shape[0]

    def body(x_ref, g_ref, t_ref, loss_ref, dx_ref, dxb_ref, dgain_ref):
        first = pl.program_id(0) == 0
        xv = x_ref[...]
        r = _rstd(xv)
        err = xv * r * g_ref[...] - t_ref[...]
        part = 0.5 * jnp.sum(jnp.mean(err * err, axis=-1, keepdims=True), axis=0, keepdims=True)
        dx, dgain = _rms_bwd(xv, r, g_ref[...], err * (1.0 / D_MODEL))
        dx_ref[...] = dx
        dxb_ref[...] = dx.astype(BF16)
        _acc_rows(loss_ref, jnp.broadcast_to(part, (1, LANES)), first)
        _acc_rows(dgain_ref, dgain, first)

    row = _row_spec(ROW_TILE, D_MODEL)
    vec = _full_spec((1, D_MODEL))
    return pl.pallas_call(
        body, name=name, grid=(L // ROW_TILE,), in_specs=[row, vec, row],
        out_specs=[_full_spec((1, LANES)), row, row, vec],
        out_shape=[jax.ShapeDtypeStruct((1, LANES), F32), jax.ShapeDtypeStruct((L, D_MODEL), F32),
                   jax.ShapeDtypeStruct((L, D_MODEL), BF16), jax.ShapeDtypeStruct((1, D_MODEL), F32)],
    )(x, gain, target)


def _to_segments(a):
    L, n = a.shape
    return a.reshape(SEGS, L // SEGS, n).transpose(1, 0, 2).reshape(L, n)


def _from_segments(a):
    L, n = a.shape
    return a.reshape(L // SEGS, SEGS, n).transpose(1, 0, 2).reshape(L, n)


def _diag_block(g):
    k, a = divmod(g, S5_TILE_GROUPS)
    return k, slice(a * S5_GROUP, (a + 1) * S5_GROUP), slice(a * S5_STATE, (a + 1) * S5_STATE)


def _block_diag_build(mats, name):
    n = mats.shape[0]

    def body(m_ref, o_ref):
        o_ref[...] = jnp.zeros(o_ref.shape, BF16)
        for q in range(n):
            for g in range(S5_GROUPS):
                k, rows, cols = _diag_block(g)
                o_ref[q, k, rows, cols] = m_ref[q, g].astype(BF16)

    return pl.pallas_call(body, name=name,
                          out_shape=jax.ShapeDtypeStruct((n, S5_TILES, S5_TILE_IN, S5_TILE_ST), BF16))(mats)


def _block_diag_extract(dense, name):
    n = len(dense)

    def body(*refs):
        o_ref = refs[n]
        for q in range(n):
            for g in range(S5_GROUPS):
                k, rows, cols = _diag_block(g)
                o_ref[q, g] = refs[q][k, rows, cols]

    return pl.pallas_call(body, name=name,
                          out_shape=jax.ShapeDtypeStruct((n, S5_GROUPS, S5_GROUP, S5_STATE), F32))(*dense)


def _pad_in_proj(w):
    z = jnp.zeros(w.shape[:-1] + (LANES - SSD_GROUP_HEADS,), w.dtype)
    return jnp.concatenate([w[..., :IN_MAIN + 8], z, w[..., IN_MAIN + 8:], z], axis=-1)


def _unpad_in_proj(w):
    return jnp.concatenate([w[..., :IN_MAIN + 8], w[..., IN_MAIN + LANES:IN_MAIN + LANES + 8]], axis=-1)


def _lane_dense(a):
    return a.reshape(DEPTH, D_MODEL // LANES, LANES, -1).transpose(3, 1, 0, 2).reshape(-1, LANES)


def _from_lane_dense(a):
    return a.reshape(-1, D_MODEL // LANES, DEPTH, LANES).transpose(2, 1, 3, 0).reshape(DEPTH, D_MODEL, -1)


def _pad_heads(v):
    v = v.reshape(SSD_GROUPS, 1, SSD_GROUP_HEADS)
    return jnp.pad(v, ((0, 0), (0, 0), (0, LANES - SSD_GROUP_HEADS)))


def _unpad_heads(v):
    return v[:, 0, :SSD_GROUP_HEADS].reshape(SSD_HEADS)


def _layer_forward(x0, p, big, i, after_inproj=None, before_s5_post=None, before_ffn=None):
    tag = "l%d_" % i
    ls = p["s5_log_step"].reshape(S5_GROUPS, 1)
    b_hgp = (p["s5_b_re"].transpose(2, 0, 1), p["s5_b_im"].transpose(2, 0, 1))
    are, aim, bbre, bbim = _s5_prep(p["s5_lam_re"], p["s5_lam_im"], ls, b_hgp[0], b_hgp[1], tag + "s5_prep")
    mats = jnp.stack([bbre.transpose(1, 0, 2), bbim.transpose(1, 0, 2), p["s5_c_re"], p["s5_c_im"]])
    bre_bd, bim_bd, cret_bd, cimt_bd = _block_diag_build(mats, tag + "s5_blockdiag")
    s5mats = dict(bre_bd=bre_bd, bim_bd=bim_bd, cret_bd=cret_bd, cimt_bd=cimt_bd,
                  bret_bd=bre_bd.transpose(0, 2, 1), bimt_bd=bim_bd.transpose(0, 2, 1),
                  cre_bd=cret_bd.transpose(0, 2, 1), cim_bd=cimt_bd.transpose(0, 2, 1),
                  are=are.reshape(S5_TILES, 1, S5_TILE_ST), aim=aim.reshape(S5_TILES, 1, S5_TILE_ST))

    u, z, xbc, dt, h1 = _rms_inproj(x0, p["norm_mix"].reshape(1, -1), big["w_in"], None, tag + "rms_inproj")
    if after_inproj is not None:
        after_inproj(u)
    u_perm = _to_segments(u)
    ys_perm, xr, xi = _s5_scan(u_perm, bre_bd, bim_bd, s5mats["cre_bd"], s5mats["cim_bd"],
                               s5mats["are"], s5mats["aim"], tag + "s5_scan")
    ys = _from_segments(ys_perm)
    late_matrices = before_s5_post(ys) if before_s5_post is not None else {}
    big = {**big, **late_matrices}
    ya = _s5_post(ys, u, p["s5_d"].reshape(1, -1), big["s5_w_glu"], p["s5_b_glu"].reshape(1, -1),
                  p["s5_norm"].reshape(1, -1), None, tag + "s5_post")

    conv_w = jnp.pad(p["ssd_conv_w"], ((0, 8 - SSD_CONV), (0, 0)))
    conv_b = p["ssd_conv_b"].reshape(1, -1)
    xc = _ssd_conv(xbc, conv_w, conv_b, tag + "ssd_conv")
    expand = _head_expander()
    heads = dict(dt_bias=_pad_heads(p["ssd_dt_bias"]), a_log=_pad_heads(p["ssd_a_log"]),
                 d=jnp.repeat(p["ssd_d"], SSD_HEAD_DIM).reshape(SSD_GROUPS, 1, GROUP_W),
                 expand=expand, expand_t=expand.T)
    y, sprev = _ssd_scan(xc, dt, heads["dt_bias"], heads["a_log"], heads["d"], expand, heads["expand_t"],
                         tag + "ssd_scan")
    yb = _ssd_post(y, z, p["ssd_norm"].reshape(1, -1), tag + "ssd_post")

    x1 = _out_proj(x0, ya, yb, big["w_out"], None, tag + "out_proj")
    ffn_matrices = before_ffn(x1) if before_ffn is not None else {}
    big = {**big, **ffn_matrices}
    late_matrices = {**late_matrices, **ffn_matrices}
    x2, gt, up = _ffn(x1, p["norm_ffn"].reshape(1, -1), big["w_gate"], big["w_up"], big["w_down"], None,
                      tag + "ffn")
    saved = dict(x0=x0, h1=h1, u=u, u_perm=u_perm, z=z, xbc=xbc, dt=dt, xr=xr, xi=xi, ys=ys, ya=ya, xc=xc, y=y,
                 sprev=sprev, yb=yb, x1=x1, gt=gt, up=up, s5mats=s5mats, heads=heads, conv_w=conv_w,
                 conv_b=conv_b, ls=ls, b_hgp=b_hgp, late_matrices=late_matrices)
    return x2, saved


def _layer_backward(dx2, dx2b, p, big, s, i, after_ffn_grads=None, after_s5_grads=None):
    tag = "l%d_" % i
    g = {}
    dx1, dx1b, h2, act, dgt, dup, dgain = _ffn_bwd(dx2, s["x1"], s["gt"], s["up"], p["norm_ffn"].reshape(1, -1),
                                                  big["w_gate"], big["w_up"], big["w_down"], None, tag + "ffn_bwd")
    g["norm_ffn"] = dgain[0]
    g["w_down"] = _mm_tn(act, dx2b, tag + "dw_down")
    g["w_gate"] = _mm_tn(dgt, h2, tag + "dw_gate")
    g["w_up"] = _mm_tn(dup, h2, tag + "dw_up")
    g["w_out"] = _mm_tn(jnp.concatenate([s["ya"], s["yb"]], axis=1), dx1b, tag + "dw_out")
    if after_ffn_grads is not None:
        p = {**p, "s5_norm": p["s5_norm"] + after_ffn_grads(g)[0, 0]}

    dys, du_skip, gelu_b, dq_b, dgain, dd, dbg = _s5_post_bwd(
        dx1b, big["w_out"], s["ys"], s["u"], p["s5_d"].reshape(1, -1), big["s5_w_glu"],
        p["s5_b_glu"].reshape(1, -1), p["s5_norm"].reshape(1, -1), None, tag + "s5_post_bwd")
    g["s5_norm"] = dgain[0]
    g["s5_d"] = dd[0]
    g["s5_b_glu"] = dbg[0]
    g["s5_w_glu"] = _mm_tn(gelu_b, dq_b, tag + "dw_glu")
    m = s["s5mats"]
    du_perm, dar, dai, dcre_d, dcim_d, dbre_d, dbim_d = _s5_scan_bwd(
        _to_segments(dys), s["u_perm"], s["xr"], s["xi"], m["bret_bd"], m["bimt_bd"], m["cret_bd"], m["cimt_bd"],
        m["are"], m["aim"], tag + "s5_scan_bwd")
    du_scan = _from_segments(du_perm)
    diag = _block_diag_extract([dcre_d, dcim_d, dbre_d, dbim_d], tag + "s5_blockdiag_bwd")
    g["s5_c_re"], g["s5_c_im"] = diag[0], diag[1]
    dbbre = diag[2].transpose(1, 0, 2)
    dbbim = diag[3].transpose(1, 0, 2)
    dlr, dli, dls, dbre, dbim = _s5_prep_bwd(
        p["s5_lam_re"], p["s5_lam_im"], s["ls"], s["b_hgp"][0], s["b_hgp"][1],
        dar.reshape(S5_GROUPS, S5_STATE), dai.reshape(S5_GROUPS, S5_STATE), dbbre, dbbim, tag + "s5_prep_bwd")
    g["s5_lam_re"] = dlr
    g["s5_lam_im"] = dli
    g["s5_log_step"] = dls[:, 0]
    g["s5_b_re"] = dbre
    g["s5_b_im"] = dbim
    if after_s5_grads is not None:
        p = {**p, "ssd_norm": p["ssd_norm"] + after_s5_grads(g)[0, 0]}

    dy, dz, dgain = _ssd_post_bwd(dx1b, big["w_out"], s["y"], s["z"], p["ssd_norm"].reshape(1, -1), None,
                                  tag + "ssd_post_bwd")
    g["ssd_norm"] = dgain[0]
    hd = s["heads"]
    dxs, dbm, dcm, ddt, dbias, dalog, dd = _ssd_scan_bwd(dy, s["xc"], s["dt"], s["sprev"], hd["dt_bias"],
                                                       hd["a_log"], hd["d"], hd["expand"], hd["expand_t"],
                                                       tag + "ssd_scan_bwd")
    g["ssd_dt_bias"] = _unpad_heads(dbias)
    g["ssd_a_log"] = _unpad_heads(dalog)
    g["ssd_d"] = _unpad_heads(dd)
    dxc = jnp.concatenate([dxs, dbm, dcm], axis=1)
    dxbc, dcw, dcb = _ssd_conv_bwd(dxc, s["xbc"], s["conv_w"], s["conv_b"], tag + "ssd_conv_bwd")
    g["ssd_conv_w"] = dcw[:SSD_CONV]
    g["ssd_conv_b"] = dcb[0]

    dx0, dx0b, dproj, dgain = _inproj_bwd(dx1, s["x0"], du_skip, du_scan, dz, dxbc, ddt, p["norm_mix"].reshape(1, -1),
                                          big["w_in"], None, tag + "inproj_bwd")
    g["norm_mix"] = dgain[0]
    g["w_in"] = _mm_tn(s["h1"], dproj, tag + "dw_in")
    return dx0, dx0b, g


MIXER_BIG = ("w_in", "s5_w_glu", "w_out")
FFN_BIG = ("w_gate", "w_up", "w_down")
BIG = MIXER_BIG + FFN_BIG
COL_SHARDED = ("w_in",)
T_STORED = ("w_gate", "w_up")
LAYER_SMALL = ("norm_mix", "s5_lam_re", "s5_lam_im", "s5_log_step", "s5_b_re", "s5_b_im", "s5_c_re", "s5_c_im",
               "s5_d", "s5_b_glu", "s5_norm", "ssd_conv_w", "ssd_conv_b", "ssd_dt_bias", "ssd_a_log", "ssd_d",
               "ssd_norm", "norm_ffn")
WEIGHTS = ("norm_mix", "w_in", "s5_lam_re", "s5_lam_im", "s5_log_step", "s5_b_re", "s5_b_im", "s5_c_re", "s5_c_im",
           "s5_d", "s5_w_glu", "s5_b_glu", "s5_norm", "ssd_conv_w", "ssd_conv_b", "ssd_dt_bias", "ssd_a_log",
           "ssd_d", "ssd_norm", "w_out", "norm_ffn", "w_gate", "w_up", "w_down", "norm_final")


S5_BC = ("s5_b_re", "s5_b_im", "s5_c_re", "s5_c_im")
TINY = tuple(k for k in LAYER_SMALL if k not in S5_BC)


def _local_step(x, target, big, small, norm_final):
    saved = []
    h = x
    for i in range(DEPTH):
        p = {k: v[i] for k, v in small.items()}
        h, s = _layer_forward(h, p, big, i)
        saved.append((p, s))
    loss, dx, dxb, dgf = _final_loss(h, norm_final.reshape(1, -1), target, "final_loss")
    grads = [None] * DEPTH
    for i in reversed(range(DEPTH)):
        p, s = saved[i]
        dx, dxb, grads[i] = _layer_backward(dx, dxb, p, big, s, i)
    by_name = {k: [grads[i][k] for i in range(DEPTH)] for k in BIG + LAYER_SMALL}
    return loss[0, 0], dx, by_name, dgf[0]


def _my_place():
    return lax.axis_index("x"), lax.axis_index("y"), lax.axis_index("c")


def _all_gather8(blocks, name):
    nt = len(blocks)

    def body(*refs):
        ins = refs[:nt]
        outs = refs[nt:2 * nt]
        send_sems, recv_sems, local_sems = refs[2 * nt:]
        x, y, c = _my_place()
        me, sibling = (x, y, c), (x, y, 1 - c)
        chips = [(1 - x, y), (x, 1 - y), (1 - x, 1 - y)]

        def slot(t, place):
            px, py, pc = place
            return outs[t].at[4 * px + 2 * py + pc]

        def copy(t, k, block, to, src=None):
            return pltpu.make_async_remote_copy(
                src_ref=slot(t, block) if src is None else src, dst_ref=slot(t, block),
                send_sem=send_sems.at[t, k], recv_sem=recv_sems.at[t, k], device_id=to, device_id_type=MESH)

        mine = [pltpu.make_async_copy(ins[t], slot(t, me), local_sems.at[t]) for t in range(nt)]
        for cp in mine:
            cp.start()
        first = []
        for t in range(nt):
            first.append(copy(t, 0, me, sibling, src=ins[t]))
            first += [copy(t, 1 + j, me, (*chip, c), src=ins[t]) for j, chip in enumerate(chips)]
        for cp in first:
            cp.start()
        passed = []
        for j, chip in enumerate(chips):
            for t in range(nt):
                copy(t, 1 + j, (*chip, c), me).wait_recv()
                fwd = copy(t, 4 + j, (*chip, c), sibling)
                fwd.start()
                passed.append(fwd)
        for t in range(nt):
            copy(t, 0, sibling, me).wait_recv()
            for j, chip in enumerate(chips):
                copy(t, 4 + j, (*chip, 1 - c), me).wait_recv()
        for cp in first + passed:
            cp.wait_send()
        for cp in mine:
            cp.wait()

    return pl.pallas_call(
        body, name=name, in_specs=[ANY] * nt, out_specs=[ANY] * nt,
        out_shape=[jax.ShapeDtypeStruct((8,) + b.shape, b.dtype) for b in blocks],
        scratch_shapes=[pltpu.SemaphoreType.DMA((nt, 7)), pltpu.SemaphoreType.DMA((nt, 7)),
                        pltpu.SemaphoreType.DMA((nt,))],
    )(*blocks)


HBM = pl.BlockSpec(memory_space=pltpu.HBM)
SEM = pl.BlockSpec(memory_space=pltpu.SEMAPHORE)
DATAFLOW = pltpu.SideEffectType.DATAFLOW_SIDE_EFFECTING


def _in_hbm(a):
    return pltpu.with_memory_space_constraint(a, pltpu.HBM)


TOKEN = jax.ShapeDtypeStruct((8, LANES), F32)
VMEM_SPEC = pl.BlockSpec(memory_space=pltpu.VMEM)


def _gather_start(blocks, after, name):
    nt = len(blocks)

    def body(*refs):
        ins = refs[:nt]
        lands = refs[nt:2 * nt]
        send_sems, recv_sems = refs[2 * nt + 1:2 * nt + 3]
        refs[-1][...] = jnp.zeros(TOKEN.shape, F32)
        x, y, c = _my_place()
        me = 4 * x + 2 * y + c
        peers = [(x, y, 1 - c), (1 - x, y, c), (x, 1 - y, c), (1 - x, 1 - y, c)]
        for t in range(nt):
            for k, peer in enumerate(peers):
                pltpu.make_async_remote_copy(src_ref=ins[t], dst_ref=lands[t].at[me], send_sem=send_sems.at[4 * t + k],
                                             recv_sem=recv_sems.at[4 * t + k], device_id=peer,
                                             device_id_type=MESH).start()

    lands = [_in_hbm(lax.empty((8,) + b.shape, b.dtype)) for b in blocks]
    out = pl.pallas_call(
        body, name=name, in_specs=[HBM] * (2 * nt) + [ANY],
        out_shape=(pltpu.SemaphoreType.DMA((4 * nt,)), pltpu.SemaphoreType.DMA((4 * nt,)),
                   *[pltpu.HBM(b.shape, b.dtype) for b in blocks],
                   *[pltpu.HBM((8,) + b.shape, b.dtype) for b in blocks], TOKEN),
        out_specs=(SEM, SEM, *[HBM] * (2 * nt), VMEM_SPEC),
        input_output_aliases={i: 2 + i for i in range(2 * nt)},
        compiler_params=pltpu.CompilerParams(has_side_effects=DATAFLOW),
    )(*[_in_hbm(b) for b in blocks], *lands, after)
    return out[:2], list(out[2:2 + nt]), list(out[2 + nt:2 + 2 * nt]), out[-1]


def _gather_forward(sems, blocks, lands, after, name):
    nt = len(blocks)

    def body(*refs):
        ins = refs[:nt]
        lands_in = refs[nt:2 * nt]
        send1, recv1 = refs[2 * nt:2 * nt + 2]
        send2, recv2 = refs[2 * nt + 3:2 * nt + 5]
        x, y, c = _my_place()
        me = 4 * x + 2 * y + c
        sibling = (x, y, 1 - c)
        sources = [4 * x + 2 * y + (1 - c), 4 * (1 - x) + 2 * y + c, 4 * x + 2 * (1 - y) + c,
                   4 * (1 - x) + 2 * (1 - y) + c]
        for t in range(nt):
            for k, src in enumerate(sources):
                cp = pltpu.make_async_remote_copy(src_ref=ins[t], dst_ref=lands_in[t].at[src],
                                                  send_sem=send1.at[4 * t + k], recv_sem=recv1.at[4 * t + k],
                                                  device_id=sibling, device_id_type=MESH)
                cp.wait_send()
                cp.wait_recv()
            for k, src in enumerate(sources[1:]):
                pltpu.make_async_remote_copy(src_ref=lands_in[t].at[src], dst_ref=lands_in[t].at[src],
                                             send_sem=send2.at[3 * t + k], recv_sem=recv2.at[3 * t + k],
                                             device_id=sibling, device_id_type=MESH).start()

    out = pl.pallas_call(
        body, name=name, in_specs=[HBM] * (2 * nt) + [SEM, SEM, pl.BlockSpec(memory_space=pl.ANY)],
        out_shape=(pltpu.SemaphoreType.DMA((3 * nt,)), pltpu.SemaphoreType.DMA((3 * nt,)),
                   *[pltpu.HBM(b.shape, b.dtype) for b in blocks],
                   *[pltpu.HBM(a.shape, a.dtype) for a in lands]),
        out_specs=(SEM, SEM, *[HBM] * (2 * nt)),
        input_output_aliases={i: 2 + i for i in range(2 * nt)},
        compiler_params=pltpu.CompilerParams(has_side_effects=DATAFLOW),
    )(*blocks, *lands, *sems, after)
    return out[:2], list(out[2 + nt:])


def _gather_finish(sems, lands, after, name):
    nt = len(lands)

    def body(*refs):
        lands_in = refs[:nt]
        send2, recv2 = refs[nt:nt + 2]
        x, y, c = _my_place()
        sibling = (x, y, 1 - c)
        mine = [4 * (1 - x) + 2 * y + c, 4 * x + 2 * (1 - y) + c, 4 * (1 - x) + 2 * (1 - y) + c]
        theirs = [4 * (1 - x) + 2 * y + 1 - c, 4 * x + 2 * (1 - y) + 1 - c, 4 * (1 - x) + 2 * (1 - y) + 1 - c]
        for t in range(nt):
            for k in range(3):
                cp = pltpu.make_async_remote_copy(src_ref=lands_in[t].at[mine[k]], dst_ref=lands_in[t].at[theirs[k]],
                                                  send_sem=send2.at[3 * t + k], recv_sem=recv2.at[3 * t + k],
                                                  device_id=sibling, device_id_type=MESH)
                cp.wait_send()
                cp.wait_recv()

    out = pl.pallas_call(
        body, name=name, in_specs=[HBM] * nt + [SEM, SEM, pl.BlockSpec(memory_space=pl.ANY)],
        out_shape=tuple(pltpu.HBM(a.shape, a.dtype) for a in lands), out_specs=tuple([HBM] * nt),
        input_output_aliases={i: i for i in range(nt)},
        compiler_params=pltpu.CompilerParams(has_side_effects=DATAFLOW),
    )(*lands, *sems, after)
    return list(out)


def _other_chips():
    x, y, _ = _my_place()
    return [(1 - x, y), (x, 1 - y), (1 - x, 1 - y)]


def _scatter_start(chunks, name):
    nt = len(chunks)

    def body(*refs):
        ins = refs[:nt]
        lands = refs[nt:2 * nt]
        send_sems, recv_sems = refs[2 * nt:2 * nt + 2]
        refs[-1][...] = jnp.zeros(TOKEN.shape, F32)
        x, y, c = _my_place()
        for t in range(nt):
            for j, (px, py) in enumerate(_other_chips()):
                pltpu.make_async_remote_copy(src_ref=ins[t].at[2 * px + py], dst_ref=lands[t].at[2 * x + y],
                                             send_sem=send_sems.at[3 * t + j], recv_sem=recv_sems.at[3 * t + j],
                                             device_id=(px, py, c), device_id_type=MESH).start()

    lands = [_in_hbm(lax.empty(a.shape, a.dtype)) for a in chunks]
    out = pl.pallas_call(
        body, name=name, in_specs=[HBM] * (2 * nt),
        out_shape=(pltpu.SemaphoreType.DMA((3 * nt,)), pltpu.SemaphoreType.DMA((3 * nt,)),
                   *[pltpu.HBM(a.shape, a.dtype) for a in chunks] * 2, TOKEN),
        out_specs=(SEM, SEM, *[HBM] * (2 * nt), VMEM_SPEC),
        input_output_aliases={i: 2 + i for i in range(2 * nt)},
        compiler_params=pltpu.CompilerParams(has_side_effects=DATAFLOW),
    )(*[_in_hbm(a) for a in chunks], *lands)
    return out[:2], list(out[2:2 + nt]), list(out[2 + nt:2 + 2 * nt]), out[-1]


def _scatter_finish(sems, chunks, lands, after, name):
    nt = len(chunks)

    def body(*refs):
        ins = refs[:nt]
        lands_in = refs[nt:2 * nt]
        send_sems, recv_sems = refs[2 * nt:2 * nt + 2]
        _, _, c = _my_place()
        for t in range(nt):
            for j, (px, py) in enumerate(_other_chips()):
                cp = pltpu.make_async_remote_copy(src_ref=ins[t].at[2 * px + py], dst_ref=lands_in[t].at[2 * px + py],
                                                  send_sem=send_sems.at[3 * t + j], recv_sem=recv_sems.at[3 * t + j],
                                                  device_id=(px, py, c), device_id_type=MESH)
                cp.wait_send()
                cp.wait_recv()

    out = pl.pallas_call(
        body, name=name, in_specs=[HBM] * (2 * nt) + [SEM, SEM, ANY],
        out_shape=tuple(pltpu.HBM(a.shape, a.dtype) for a in lands), out_specs=tuple([HBM] * nt),
        input_output_aliases={nt + i: i for i in range(nt)},
        compiler_params=pltpu.CompilerParams(has_side_effects=DATAFLOW),
    )(*chunks, *lands, *sems, after)
    return list(out)


def _swap_halves(views, name):
    nt = len(views)

    def body(*refs):
        ins = refs[:nt]
        outs = refs[nt:2 * nt]
        send_sems, recv_sems = refs[2 * nt:]
        x, y, c = _my_place()
        copies = [pltpu.make_async_remote_copy(
            src_ref=ins[t].at[pl.ds(0, views[t].shape[0]), pl.ds(1 - c, 1)], dst_ref=outs[t],
            send_sem=send_sems.at[t], recv_sem=recv_sems.at[t], device_id=(x, y, 1 - c), device_id_type=MESH)
            for t in range(nt)]
        for cp in copies:
            cp.start()
        for cp in copies:
            cp.wait()

    return pl.pallas_call(
        body, name=name, in_specs=[ANY] * nt, out_specs=[ANY] * nt,
        out_shape=[jax.ShapeDtypeStruct((a.shape[0], 1) + a.shape[2:], a.dtype) for a in views],
        scratch_shapes=[pltpu.SemaphoreType.DMA((nt,)), pltpu.SemaphoreType.DMA((nt,))],
    )(*views)


def _pair_add_halves(view, recv, name):
    n, _, rows, cols = view.shape
    tile = _row_tile(rows, cols, 4)

    def body(a0_ref, a1_ref, r_ref, o_ref):
        mine = jnp.where(lax.axis_index("c") == 0, a0_ref[...], a1_ref[...])
        o_ref[...] = (mine.astype(F32) + r_ref[...].astype(F32)).astype(o_ref.dtype)

    half = lambda h: pl.BlockSpec((None, None, tile, cols), lambda p, i: (p, h, i, 0))
    return pl.pallas_call(
        body, name=name, grid=(n, rows // tile), in_specs=[half(0), half(1), half(0)],
        out_specs=pl.BlockSpec((None, tile, cols), lambda p, i: (p, i, 0)),
        out_shape=jax.ShapeDtypeStruct((n, rows, cols), view.dtype))(view, view, recv)


def _sum_chunks(lands, chunks, order, name):
    _, rows, cols = chunks.shape
    tile = _row_tile(rows, cols, 5)

    def body(order_ref, l0_ref, l1_ref, l2_ref, own_ref, o_ref):
        o_ref[...] = ((l0_ref[...].astype(F32) + l1_ref[...].astype(F32)) + l2_ref[...].astype(F32)
                      + own_ref[...].astype(F32))

    slot = lambda j: pl.BlockSpec((None, tile, cols), lambda i, order_ref: (order_ref[j], i, 0))
    grid_spec = pltpu.PrefetchScalarGridSpec(
        num_scalar_prefetch=1, grid=(rows // tile,), in_specs=[slot(0), slot(1), slot(2), slot(3)],
        out_specs=pl.BlockSpec((tile, cols), lambda i, order_ref: (i, 0)))
    return pl.pallas_call(body, name=name, grid_spec=grid_spec,
                          out_shape=jax.ShapeDtypeStruct((rows, cols), F32))(order, lands, lands, lands, chunks)


def _adamw_layer(w, g_mine, g_sibling, m, v, layer, prev, name):
    depth, rows, cols = w.shape
    half = rows // 2
    tile = _row_tile(half, cols, 10)
    tiles = half // tile

    def body(w_ref, gm_ref, gs_ref, m_ref, v_ref, *rest):
        d_ref, nm_ref, nv_ref, go_ref = rest[-4:]
        gv = jnp.where(pl.program_id(0) == lax.axis_index("c"), gm_ref[...], gs_ref[...])
        d_ref[...], nm_ref[...], nv_ref[...] = _adamw_math(w_ref[...], gv, m_ref[...], v_ref[...])
        go_ref[...] = gv

    spec = pl.BlockSpec((None, tile, cols), lambda h, i: (layer, h * tiles + i, 0))
    gspec = pl.BlockSpec((tile, cols), lambda h, i: (i, 0))
    shape = jax.ShapeDtypeStruct((depth, rows, cols), F32)
    extra = list(prev)
    aliases = {5 + j: j for j in range(4)} if len(extra) == 4 else {}
    return pl.pallas_call(
        body, name=name, grid=(2, tiles), in_specs=[spec, gspec, gspec, spec, spec] + [ANY] * len(extra),
        out_specs=[spec] * 4, out_shape=[shape] * 4, input_output_aliases=aliases)(w, g_mine, g_sibling, m, v, *extra)


def _sibling_swap_other(pairs, name):
    nt = len(pairs)

    def body(*refs):
        ins = refs[:2 * nt]
        outs = refs[2 * nt:3 * nt]
        send_sems, recv_sems = refs[3 * nt:]
        x, y, c = _my_place()

        def copy(t, src):
            return pltpu.make_async_remote_copy(src_ref=src, dst_ref=outs[t], send_sem=send_sems.at[t],
                                                recv_sem=recv_sems.at[t], device_id=(x, y, 1 - c), device_id_type=MESH)

        for t in range(nt):
            @pl.when(c == 0)
            def _():
                copy(t, ins[2 * t + 1]).start()

            @pl.when(c == 1)
            def _():
                copy(t, ins[2 * t]).start()
        for t in range(nt):
            copy(t, ins[2 * t]).wait()

    flat = [a for pair in pairs for a in pair]
    return pl.pallas_call(
        body, name=name, in_specs=[ANY] * (2 * nt), out_specs=[ANY] * nt,
        out_shape=[jax.ShapeDtypeStruct(a0.shape, a0.dtype) for a0, _ in pairs],
        scratch_shapes=[pltpu.SemaphoreType.DMA((nt,)), pltpu.SemaphoreType.DMA((nt,))],
    )(*flat)


def _sibling_swap(arrs, name):
    nt = len(arrs)

    def body(*refs):
        ins = refs[:nt]
        outs = refs[nt:2 * nt]
        send_sems, recv_sems = refs[2 * nt:]
        x, y, c = _my_place()
        copies = [pltpu.make_async_remote_copy(src_ref=ins[t], dst_ref=outs[t], send_sem=send_sems.at[t],
                                               recv_sem=recv_sems.at[t], device_id=(x, y, 1 - c), device_id_type=MESH)
                  for t in range(nt)]
        for cp in copies:
            cp.start()
        for cp in copies:
            cp.wait()

    return pl.pallas_call(
        body, name=name, in_specs=[ANY] * nt, out_specs=[ANY] * nt,
        out_shape=[jax.ShapeDtypeStruct(a.shape, a.dtype) for a in arrs],
        scratch_shapes=[pltpu.SemaphoreType.DMA((nt,)), pltpu.SemaphoreType.DMA((nt,))],
    )(*arrs)


def _chip_all_to_all(arrs, name):
    nt = len(arrs)

    def body(*refs):
        ins = refs[:nt]
        outs = refs[nt:2 * nt]
        send_sems, recv_sems, local_sems = refs[2 * nt:]
        x, y, c = _my_place()
        mine = 2 * x + y
        chips = [(1 - x, y), (x, 1 - y), (1 - x, 1 - y)]
        local = [pltpu.make_async_copy(ins[t].at[mine], outs[t].at[mine], local_sems.at[t]) for t in range(nt)]
        for cp in local:
            cp.start()
        sends = []
        for t in range(nt):
            for j, (px, py) in enumerate(chips):
                sends.append(pltpu.make_async_remote_copy(
                    src_ref=ins[t].at[2 * px + py], dst_ref=outs[t].at[mine], send_sem=send_sems.at[t, j],
                    recv_sem=recv_sems.at[t, j], device_id=(px, py, c), device_id_type=MESH))
        for cp in sends:
            cp.start()
        for t in range(nt):
            for j, (px, py) in enumerate(chips):
                pltpu.make_async_remote_copy(
                    src_ref=ins[t].at[mine], dst_ref=outs[t].at[2 * px + py], send_sem=send_sems.at[t, j],
                    recv_sem=recv_sems.at[t, j], device_id=(px, py, c), device_id_type=MESH).wait_recv()
        for cp in sends:
            cp.wait_send()
        for cp in local:
            cp.wait()

    return pl.pallas_call(
        body, name=name, in_specs=[ANY] * nt, out_specs=[ANY] * nt,
        out_shape=[jax.ShapeDtypeStruct(a.shape, a.dtype) for a in arrs],
        scratch_shapes=[pltpu.SemaphoreType.DMA((nt, 3)), pltpu.SemaphoreType.DMA((nt, 3)),
                        pltpu.SemaphoreType.DMA((nt,))],
    )(*arrs)


def _as_rows(a):
    return a.reshape(-1, a.shape[-1])


STREAM_VMEM_BYTES = 32 * 1024 * 1024
SUBLANES = 8


def _row_tile(rows, cols, n_arrays):
    lanes = -(-cols // LANES) * LANES
    for t in range(min(rows, 512), SUBLANES - 1, -1):
        if rows % t == 0 and t % SUBLANES == 0 and 2 * n_arrays * t * lanes * 4 <= STREAM_VMEM_BYTES:
            return t
    return rows


def _pair_add(a0, a1, recv, name):
    rows, cols = a0.shape
    tile = _row_tile(rows, cols, 4)

    def body(a0_ref, a1_ref, r_ref, o_ref):
        mine = jnp.where(lax.axis_index("c") == 0, a0_ref[...], a1_ref[...])
        o_ref[...] = (mine.astype(F32) + r_ref[...].astype(F32)).astype(o_ref.dtype)

    spec = pl.BlockSpec((tile, cols), lambda i: (i, 0))
    return pl.pallas_call(body, name=name, grid=(rows // tile,), in_specs=[spec] * 3, out_specs=spec,
                          out_shape=jax.ShapeDtypeStruct((rows, cols), a0.dtype))(a0, a1, recv)


def _sum_leading(a, name):
    n, rows, cols = a.shape
    tile = _row_tile(rows, cols, n + 1)

    def body(a_ref, o_ref):
        acc = a_ref[0].astype(F32)
        for k in range(1, n):
            acc = acc + a_ref[k].astype(F32)
        o_ref[...] = acc

    return pl.pallas_call(
        body, name=name, grid=(rows // tile,), in_specs=[pl.BlockSpec((n, tile, cols), lambda i: (0, i, 0))],
        out_specs=pl.BlockSpec((tile, cols), lambda i: (i, 0)),
        out_shape=jax.ShapeDtypeStruct((rows, cols), F32))(a)


def _adamw_math(w, g, m, v):
    mn = ADAM_B1 * m + (1.0 - ADAM_B1) * g
    vn = ADAM_B2 * v + (1.0 - ADAM_B2) * jnp.square(g)
    m_hat = mn / (1.0 - ADAM_B1 ** ADAM_STEP)
    v_hat = vn / (1.0 - ADAM_B2 ** ADAM_STEP)
    delta = -ADAM_LR * (m_hat / (jnp.sqrt(v_hat) + ADAM_EPS) + ADAM_WD * w)
    return delta, mn, vn


def _adamw_layers(w, g_mine, g_sibling, m, v, name):
    depth, rows, cols = w.shape
    tile = _row_tile(rows, cols, 10)

    def body(w_ref, gm_ref, gs_ref, m_ref, v_ref, d_ref, nm_ref, nv_ref, go_ref):
        gv = jnp.where(pl.program_id(0) == lax.axis_index("c"), gm_ref[...], gs_ref[...])
        d_ref[...], nm_ref[...], nv_ref[...] = _adamw_math(w_ref[...], gv, m_ref[...], v_ref[...])
        go_ref[...] = gv

    spec = pl.BlockSpec((None, tile, cols), lambda l, i: (l, i, 0))
    gspec = pl.BlockSpec((tile, cols), lambda l, i: (i, 0))
    shape = jax.ShapeDtypeStruct((depth, rows, cols), F32)
    return pl.pallas_call(body, name=name, grid=(depth, rows // tile), in_specs=[spec, gspec, gspec, spec, spec],
                          out_specs=[spec] * 4, out_shape=[shape] * 4)(w, g_mine, g_sibling, m, v)


def _adamw_rows(w, g, m, v, name):
    depth, rows, cols = w.shape
    tile = _row_tile(rows, cols, 7)

    def body(w_ref, g_ref, m_ref, v_ref, d_ref, nm_ref, nv_ref):
        d_ref[...], nm_ref[...], nv_ref[...] = _adamw_math(w_ref[...], g_ref[...], m_ref[...], v_ref[...])

    spec = pl.BlockSpec((None, tile, cols), lambda l, i: (l, i, 0))
    shape = jax.ShapeDtypeStruct((depth, rows, cols), F32)
    return pl.pallas_call(body, name=name, grid=(depth, rows // tile), in_specs=[spec] * 4, out_specs=[spec] * 3,
                          out_shape=[shape] * 3)(w, g, m, v)


def _adamw_many(ws, gs, ms, vs, name):
    nt = len(ws)

    def body(*refs):
        for t in range(nt):
            w_ref, g_ref, m_ref, v_ref = (refs[k * nt + t] for k in range(4))
            d_ref, nm_ref, nv_ref = (refs[(4 + k) * nt + t] for k in range(3))
            d_ref[...], nm_ref[...], nv_ref[...] = _adamw_math(w_ref[...], g_ref[...], m_ref[...], v_ref[...])

    shapes = [jax.ShapeDtypeStruct(a.shape, F32) for a in ws]
    out = pl.pallas_call(body, name=name, out_shape=shapes * 3)(*ws, *gs, *ms, *vs)
    return out[:nt], out[nt:2 * nt], out[2 * nt:]


TINY_ROWS_MULTIPLE = 128


def _flat_pack(arrs):
    flat = jnp.concatenate([a.reshape(-1) for a in arrs])
    pad = (-flat.shape[0]) % (TINY_ROWS_MULTIPLE * LANES)
    return jnp.pad(flat, (0, pad)).reshape(-1, LANES)


def _flat_unpack(buf, shapes):
    flat = buf.reshape(-1)
    out = []
    off = 0
    for shp in shapes:
        n = math.prod(shp)
        out.append(flat[off:off + n].reshape(shp))
        off += n
    return out


def _to_chunks(a, name):
    if name == "w_in":
        a = _unpad_in_proj(a)
    rows, cols = a.shape
    if name in COL_SHARDED:
        return a.reshape(rows, 4, cols // 4).transpose(1, 0, 2)
    return a.reshape(4, rows // 4, cols)


def _from_chunks(a, name):
    _, depth, r, cc = a.shape
    if name in COL_SHARDED:
        return a.transpose(1, 2, 0, 3).reshape(depth, r, 4 * cc)
    return a.transpose(1, 0, 2, 3).reshape(depth, 4 * r, cc)


def kernel(x, norm_mix, w_in, s5_lam_re, s5_lam_im, s5_log_step, s5_b_re, s5_b_im, s5_c_re, s5_c_im, s5_d, s5_w_glu, s5_b_glu, s5_norm, ssd_conv_w, ssd_conv_b, ssd_dt_bias, ssd_a_log, ssd_d, ssd_norm, w_out, norm_ffn, w_gate, w_up, w_down, norm_final, loss_target, m_norm_mix, m_w_in, m_s5_lam_re, m_s5_lam_im, m_s5_log_step, m_s5_b_re, m_s5_b_im, m_s5_c_re, m_s5_c_im, m_s5_d, m_s5_w_glu, m_s5_b_glu, m_s5_norm, m_ssd_conv_w, m_ssd_conv_b, m_ssd_dt_bias, m_ssd_a_log, m_ssd_d, m_ssd_norm, m_w_out, m_norm_ffn, m_w_gate, m_w_up, m_w_down, m_norm_final, v_norm_mix, v_w_in, v_s5_lam_re, v_s5_lam_im, v_s5_log_step, v_s5_b_re, v_s5_b_im, v_s5_c_re, v_s5_c_im, v_s5_d, v_s5_w_glu, v_s5_b_glu, v_s5_norm, v_ssd_conv_w, v_ssd_conv_b, v_ssd_dt_bias, v_ssd_a_log, v_ssd_d, v_ssd_norm, v_w_out, v_norm_ffn, v_w_gate, v_w_up, v_w_down, v_norm_final):
    args = dict(locals())
    w = {k: args[k] for k in WEIGHTS}
    m = {k: args["m_" + k] for k in WEIGHTS}
    v = {k: args["v_" + k] for k in WEIGHTS}
    cx, cy, cc = _my_place()
    chip = 2 * cx + cy

    me = 4 * cx + 2 * cy + cc
    others = _other_chips()
    chunk_order = jnp.stack([2 * px + py for px, py in others] + [chip]).astype(jnp.int32)
    stored = lambda k, a: jnp.swapaxes(a, 1, 2) if k in T_STORED else a

    def my_half(k, layer):
        a = stored(k, w[k])[layer]
        return lax.dynamic_slice_in_dim(a, cc * (a.shape[0] // 2), a.shape[0] // 2, 0).astype(BF16)

    def assemble(names, lands, blocks):
        full = {}
        for k, a, b in zip(names, lands, blocks):
            a = lax.dynamic_update_index_in_dim(a, b, me, 0)
            a = a.reshape(4, 2 * a.shape[1], a.shape[2])
            if k in COL_SHARDED:
                full[k] = _pad_in_proj(a.transpose(1, 0, 2).reshape(a.shape[1], 4 * a.shape[2]))
            else:
                full[k] = a.reshape(4 * a.shape[1], a.shape[2])
        return full

    conv_block = w["ssd_conv_w"].reshape(DEPTH * SSD_CONV, -1)
    first = [my_half("w_in", 0), conv_block]
    second_names = ("s5_w_glu", "w_out")
    second = [my_half(k, 0) for k in second_names]
    ffn0 = [my_half(k, 0) for k in FFN_BIG]
    blocks1 = [my_half(k, 1) for k in BIG]
    sems_a, kept_a, lands_a, token = _gather_start(first, x, "gather0a_start")
    sems_c, kept_c, lands_c, token = _gather_start(second, token, "gather0c_start")
    sems_b, kept_b, lands_b, token = _gather_start(ffn0, token, "gather0b_start")
    sems1, kept1, lands1, token = _gather_start(blocks1, token, "gather1_start")
    sems_a, lands_a = _gather_forward(sems_a, kept_a, lands_a, token, "gather0a_forward")
    lands_a = _gather_finish(sems_a, lands_a, token, "gather0a_finish")
    big0 = assemble(("w_in",), lands_a, first)
    conv_rows = lax.dynamic_update_index_in_dim(lands_a[-1], conv_block, me, 0)
    conv_full = conv_rows.reshape(4, 2, DEPTH, SSD_CONV, -1)[:, 0].transpose(1, 2, 0, 3).reshape(
        DEPTH, SSD_CONV, SSD_CONV_DIM)
    small = {k: w[k] for k in LAYER_SMALL}
    small["ssd_conv_w"] = conv_full
    p0 = {k: a[0] for k, a in small.items()}
    p1 = {k: a[1] for k, a in small.items()}

    p0["norm_mix"] = p0["norm_mix"] + token[0, 0]
    pending = {}

    def pass_on_ffn0(u):
        pending["second"] = _gather_forward(sems_c, kept_c, lands_c, u, "gather0c_forward")
        pending["ffn0"] = _gather_forward(sems_b, kept_b, lands_b, pending["second"][1][0], "gather0b_forward")

    def second_matrices(ys):
        sems, lands = pending["second"]
        return assemble(second_names, _gather_finish(sems, lands, ys, "gather0c_finish"), second)

    def ffn0_matrices(x1):
        sems, lands = pending["ffn0"]
        lands = _gather_finish(sems, lands, x1, "gather0b_finish")
        pending["layer1"] = _gather_forward(sems1, kept1, lands1, lands[0], "gather1_forward")
        return assemble(FFN_BIG, lands, ffn0)

    h1, saved0 = _layer_forward(x[0], p0, big0, 0, pass_on_ffn0, second_matrices, ffn0_matrices)
    big0 = {**big0, **saved0["late_matrices"]}
    sems1, lands1 = pending["layer1"]
    lands1 = _gather_finish(sems1, lands1, h1, "gather1_finish")
    big1 = assemble(BIG, lands1, blocks1)
    h2, saved1 = _layer_forward(h1, p1, big1, 1)
    loss_row, dx, dxb, g_final = _final_loss(h2, w["norm_final"].reshape(1, -1), loss_target[0], "final_loss")
    loss_part, g_final = loss_row[0, 0], g_final[0]

    def halves_view(k, a):
        if k in COL_SHARDED:
            return a.reshape(1, 2, a.shape[0] // 2, a.shape[1])
        return a.reshape(4, 2, a.shape[0] // 8, a.shape[1])

    def to_chunks(k, part):
        if k in COL_SHARDED:
            a = _unpad_in_proj(part[0])
            return a.reshape(a.shape[0], 4, a.shape[1] // 4).transpose(1, 0, 2)
        return part.reshape(4, -1, part.shape[-1])

    def reduce_begin(names, views, tag):
        recv = _swap_halves(views, tag + "swap")
        parts = [_pair_add_halves(a, r, tag + "pair_" + k) for k, a, r in zip(names, views, recv)]
        chunks = [to_chunks(k, p) for k, p in zip(names, parts)]
        return _scatter_start(chunks, tag + "scatter_start")

    def reduce_end(names, handle, after, tag):
        sems, kept, lands, _ = handle
        lands = _scatter_finish(sems, kept, lands, after, tag + "scatter_finish")
        return [_sum_chunks(a, b, chunk_order, tag + "sum_" + k) for k, a, b in zip(names, lands, kept)]

    dx, dxb, g1 = _layer_backward(dx, dxb, p1, big1, saved1, 1)
    round1 = reduce_begin(BIG, [halves_view(k, g1[k]) for k in BIG], "grad1_")
    p0["norm_ffn"] = p0["norm_ffn"] + round1[3][0, 0]

    early = FFN_BIG + ("w_out",)
    middle = ("s5_w_glu", "s5_bc")
    bc_rows = 2 * DEPTH * S5_GROUP * S5_GROUPS

    def send_early(g_so_far):
        pending["early"] = reduce_begin(early, [halves_view(k, g_so_far[k]) for k in early], "grad0a_")
        return pending["early"][3]

    def send_middle(g_so_far):
        rows = lambda names: jnp.stack([a for layer in (g_so_far, g1) for a in (layer[names[0]], layer[names[1]])]
                                       ).reshape(bc_rows, S5_STATE)
        bc = jnp.stack([rows(("s5_b_re", "s5_b_im")), rows(("s5_c_re", "s5_c_im"))])[None]
        pending["middle"] = reduce_begin(middle, [halves_view("s5_w_glu", g_so_far["s5_w_glu"]), bc], "grad0b_")
        return pending["middle"][3]

    grad_x, _, g0 = _layer_backward(dx, dxb, p0, big0, saved0, 0, send_early, send_middle)
    g = {k: [g0[k], g1[k]] for k in LAYER_SMALL}
    reduced1 = dict(zip(BIG, reduce_end(BIG, round1, grad_x, "grad1_")))
    shared1 = dict(zip(BIG, _sibling_swap([reduced1[k] for k in BIG], "grad1_share")))
    round0 = reduce_begin(("w_in",), [halves_view("w_in", g0["w_in"])], "grad0c_")

    delta, new_m, new_v, grads = {}, {}, {}, {}
    adam1 = {}
    layered = tuple(k for k in BIG if k not in COL_SHARDED)
    for k in layered:
        adam1[k] = _adamw_layer(stored(k, w[k]), reduced1[k], shared1[k], stored(k, m[k]), stored(k, v[k]), 1,
                                [round0[3]], "adamw1_" + k)
    follow = adam1[layered[-1]][0]
    reduced0 = dict(zip(early, reduce_end(early, pending["early"], follow, "grad0a_")))
    reduced0.update(zip(middle, reduce_end(middle, pending["middle"], follow, "grad0b_")))
    tiny_names = TINY + ("norm_final",)
    parts = [jnp.stack(g[k]) for k in TINY] + [g_final, loss_part.reshape(1)]
    shapes = [p.shape for p in parts]
    small_blocks = [_flat_pack(parts), reduced0["s5_bc"]]
    small_sems, small_kept, small_lands, small_token = _gather_start(small_blocks, follow, "gather_small_start")
    reduced0.update(zip(("w_in",), reduce_end(("w_in",), round0, small_token, "grad0c_")))
    shared0 = dict(zip(BIG, _sibling_swap([reduced0[k] for k in BIG], "grad0_share")))
    for k in layered:
        outs = _adamw_layer(stored(k, w[k]), reduced0[k], shared0[k], stored(k, m[k]), stored(k, v[k]), 0, adam1[k],
                            "adamw0_" + k)
        delta[k], new_m[k], new_v[k], grads[k] = (stored(k, a) for a in outs)
    both = lambda mine, sib: jnp.where(cc == 0, jnp.concatenate([mine, sib]), jnp.concatenate([sib, mine]))
    grads["w_in"] = jnp.stack([both(reduced0["w_in"], shared0["w_in"]), both(reduced1["w_in"], shared1["w_in"])])
    outs = _adamw_rows(*[_lane_dense(a)[None] for a in (w["w_in"], grads["w_in"], m["w_in"], v["w_in"])],
                       "adamw_w_in")
    delta["w_in"], new_m["w_in"], new_v["w_in"] = (_from_lane_dense(a[0]) for a in outs)

    last = delta["w_in"]
    small_sems, small_lands = _gather_forward(small_sems, small_kept, small_lands, last, "gather_small_forward")
    small_lands = _gather_finish(small_sems, small_lands, last, "gather_small_finish")
    allparts, bc_eighths = (lax.dynamic_update_index_in_dim(a, b, me, 0) for a, b in zip(small_lands, small_blocks))
    unpacked = _flat_unpack(_sum_leading(allparts, "sum_small"), shapes)
    loss = unpacked[-1][0]
    grads.update(zip(tiny_names, unpacked[:-1]))
    width = SSD_CONV_DIM // 4
    grads["ssd_conv_w"] = lax.dynamic_slice_in_dim(grads["ssd_conv_w"], chip * width, width, axis=2)
    bc = bc_eighths.reshape(4, 2, bc_rows // 4, S5_STATE)
    b_sum = bc[:, 0].reshape(DEPTH, 2, S5_GROUP, S5_GROUPS, S5_STATE)
    c_sum = bc[:, 1].reshape(DEPTH, 2, S5_GROUPS, S5_GROUP, S5_STATE)
    grads["s5_c_re"] = c_sum[:, 0]
    grads["s5_c_im"] = c_sum[:, 1]

    b_names = ("s5_b_re", "s5_b_im")
    hp = lambda a: a.transpose(0, 1, 3, 2)
    names = tiny_names + ("s5_c_re", "s5_c_im") + b_names
    view = lambda k, a: hp(a) if k in b_names else (a.reshape(1, -1) if a.ndim == 1 else a)
    g_view = {k: view(k, grads[k]) for k in names if k not in b_names}
    g_view.update({k: b_sum[:, j].transpose(0, 2, 1, 3) for j, k in enumerate(b_names)})
    ds, nms, nvs = _adamw_many([view(k, w[k]) for k in names], [g_view[k] for k in names],
                               [view(k, m[k]) for k in names], [view(k, v[k]) for k in names], "adamw_small")
    for k, a, b, c in zip(names, ds, nms, nvs):
        if k in b_names:
            delta[k], new_m[k], new_v[k], grads[k] = hp(a), hp(b), hp(c), hp(g_view[k])
        else:
            delta[k], new_m[k], new_v[k] = (t.reshape(w[k].shape) for t in (a, b, c))

    return (loss, grad_x[None], *[grads[k] for k in WEIGHTS], *[delta[k] for k in WEIGHTS],
            *[new_m[k] for k in WEIGHTS], *[new_v[k] for k in WEIGHTS])
```

```python
import functools
import math

import jax
import jax.numpy as jnp
from jax import lax
from jax.experimental import pallas as pl
from jax.experimental.pallas import tpu as pltpu

F32 = jnp.float32
BF16 = jnp.bfloat16
MESH = pl.DeviceIdType.MESH
ANY = pl.BlockSpec(memory_space=pl.ANY)

D_MODEL = 1024
DEPTH = 2
S5_GROUPS = 64
S5_GROUP = 16
S5_STATE = 64
S5_COLS = S5_GROUPS * S5_STATE
S5_TILE_GROUPS = 8
S5_TILES = S5_GROUPS // S5_TILE_GROUPS
S5_TILE_IN = S5_TILE_GROUPS * S5_GROUP
S5_TILE_ST = S5_TILE_GROUPS * S5_STATE
SEGS = 8
SSD_HEADS = 16
SSD_HEAD_DIM = 64
SSD_GROUPS = 2
SSD_GROUP_HEADS = SSD_HEADS // SSD_GROUPS
SSD_STATE = 128
SSD_CONV = 4
SSD_CHUNK = 128
SSD_WIDTH = 1024
SSD_CONV_DIM = SSD_WIDTH + 2 * SSD_GROUPS * SSD_STATE
IN_PROJ = 3600
IN_MAIN = 3584
IN_PAD = IN_MAIN + 2 * 128
FFN = 2816
EPS = 1e-6
LANES = 128
ROW_TILE = 256
WIDE_ROW_TILE = 512

ADAM_LR = 0.001
ADAM_B1 = 0.9
ADAM_B2 = 0.999
ADAM_EPS = 1e-08
ADAM_WD = 0.01
ADAM_STEP = 10


def _sigmoid(x):
    return 1.0 / (1.0 + jnp.exp(-x))


def _silu(x):
    return x * _sigmoid(x)


def _dsilu(x):
    s = _sigmoid(x)
    return s * (1.0 + x * (1.0 - s))


_GELU_K = math.sqrt(2.0 / math.pi)
_GELU_C = 0.044715


def _gelu(x):
    t = jnp.tanh(_GELU_K * (x + _GELU_C * x * x * x))
    return 0.5 * x * (1.0 + t)


def _dgelu(x):
    t = jnp.tanh(_GELU_K * (x + _GELU_C * x * x * x))
    return 0.5 * (1.0 + t) + 0.5 * x * (1.0 - t * t) * _GELU_K * (1.0 + 3.0 * _GELU_C * x * x)


def _softplus(x):
    e = jnp.exp(-jnp.abs(x))
    u = 1.0 + e
    log1p = jnp.where(u == 1.0, e, jnp.log(u) * e / jnp.where(u == 1.0, 1.0, u - 1.0))
    return jnp.maximum(x, 0.0) + log1p


def _rstd(x):
    return lax.rsqrt(jnp.mean(x * x, axis=-1, keepdims=True) + EPS)


def _rms_bwd(x, r, gain, dy):
    dyg = dy * gain
    dx = r * dyg - x * (r * r * r) * jnp.mean(x * dyg, axis=-1, keepdims=True)
    dgain = jnp.sum(dy * x * r, axis=0, keepdims=True)
    return dx, dgain


def _dot(a, b):
    return jnp.dot(a, b, preferred_element_type=F32)


def _dot_nt(a, b):
    return lax.dot_general(a, b, (((1,), (1,)), ((), ())), preferred_element_type=F32)


def _dot_tn(a, b):
    return lax.dot_general(a, b, (((0,), (0,)), ((), ())), preferred_element_type=F32)


def _row_spec(tile, cols):
    return pl.BlockSpec((tile, cols), lambda i: (i, 0))


def _full_spec(shape):
    nd = len(shape)
    return pl.BlockSpec(shape, lambda *_: (0,) * nd)


def _const_spec(shape):
    nd = len(shape)
    return pl.BlockSpec(shape, lambda *_: (0,) * nd, pipeline_mode=pl.Buffered(1))


def _layer_spec(shape, layer, block=0):
    if layer is None:
        return pl.BlockSpec(tuple(shape), lambda *_: (block, 0), pipeline_mode=pl.Buffered(1))
    return pl.BlockSpec((None,) + tuple(shape), lambda *_: (layer, block, 0), pipeline_mode=pl.Buffered(1))


def _acc_rows(ref, val, first):
    @pl.when(first)
    def _():
        ref[...] = val

    @pl.when(jnp.logical_not(first))
    def _():
        ref[...] += val


def _pick_tile(n, cap):
    best = LANES
    for t in range(LANES, cap + 1, LANES):
        if n % t == 0:
            best = t
    return best


def _mm_tn(a, b, name):
    k, m = a.shape
    _, n = b.shape
    tm = _pick_tile(m, 1536)
    tn = _pick_tile(n, 1536)

    def body(a_ref, b_ref, o_ref):
        o_ref[...] = _dot_tn(a_ref[...], b_ref[...]).astype(BF16)

    return pl.pallas_call(
        body, name=name, grid=(n // tn, m // tm),
        in_specs=[pl.BlockSpec((k, tm), lambda j, i: (0, i)), pl.BlockSpec((k, tn), lambda j, i: (0, j))],
        out_specs=pl.BlockSpec((tm, tn), lambda j, i: (i, j)),
        out_shape=jax.ShapeDtypeStruct((m, n), BF16),
    )(a, b)


def _rms_inproj(x, gain, w_pad, layer, name):
    L = x.shape[0]

    def body(x_ref, g_ref, w_ref, u_ref, z_ref, xbc_ref, dt_ref, h_ref):
        xv = x_ref[...]
        h = (xv * _rstd(xv) * g_ref[...]).astype(BF16)
        h_ref[...] = h
        p = _dot(h, w_ref[...])
        u_ref[...] = p[:, :1024]
        z_ref[...] = p[:, 1024:2048]
        xbc_ref[...] = p[:, 2048:IN_MAIN]
        dt_ref[...] = p[:, IN_MAIN:IN_PAD]

    tile = min(L, WIDE_ROW_TILE)
    return pl.pallas_call(
        body, name=name, grid=(L // tile,),
        in_specs=[_row_spec(tile, D_MODEL), _full_spec((1, D_MODEL)), _layer_spec((D_MODEL, IN_PAD), layer)],
        out_specs=[_row_spec(tile, 1024), _row_spec(tile, 1024), _row_spec(tile, SSD_CONV_DIM),
                   _row_spec(tile, 256), _row_spec(tile, D_MODEL)],
        out_shape=[jax.ShapeDtypeStruct((L, 1024), F32), jax.ShapeDtypeStruct((L, 1024), F32),
                   jax.ShapeDtypeStruct((L, SSD_CONV_DIM), F32), jax.ShapeDtypeStruct((L, 256), F32),
                   jax.ShapeDtypeStruct((L, D_MODEL), BF16)],
    )(x, gain, w_pad)


def _s5_prep_math(lr, li, ls, bre, bim):
    step = jnp.exp(ls)
    mag = jnp.exp(lr * step)
    ang = li * step
    are = mag * jnp.cos(ang)
    aim = mag * jnp.sin(ang)
    den = lr * lr + li * li
    nr = are - 1.0
    ni = aim
    cre = (nr * lr + ni * li) / den
    cim = (ni * lr - nr * li) / den
    bbre = cre[None] * bre - cim[None] * bim
    bbim = cre[None] * bim + cim[None] * bre
    return are, aim, bbre, bbim


def _s5_prep(lr, li, ls, bre, bim, name):
    def body(lr_ref, li_ref, ls_ref, bre_ref, bim_ref, are_ref, aim_ref, bbre_ref, bbim_ref):
        are, aim, bbre, bbim = _s5_prep_math(lr_ref[...], li_ref[...], ls_ref[...], bre_ref[...], bim_ref[...])
        are_ref[...] = are
        aim_ref[...] = aim
        bbre_ref[...] = bbre
        bbim_ref[...] = bbim

    gp = jax.ShapeDtypeStruct((S5_GROUPS, S5_STATE), F32)
    hgp = jax.ShapeDtypeStruct((S5_GROUP, S5_GROUPS, S5_STATE), F32)
    return pl.pallas_call(body, name=name, out_shape=[gp, gp, hgp, hgp])(lr, li, ls, bre, bim)


def _s5_prep_bwd(lr, li, ls, bre, bim, dare, daim, dbbre, dbbim, name):
    def body(lr_ref, li_ref, ls_ref, bre_ref, bim_ref, dare_ref, daim_ref, dbbre_ref, dbbim_ref,
             dlr_ref, dli_ref, dls_ref, dbre_ref, dbim_ref):
        _, vjp = jax.vjp(_s5_prep_math, lr_ref[...], li_ref[...], ls_ref[...], bre_ref[...], bim_ref[...])
        dlr, dli, dls, dbre, dbim = vjp((dare_ref[...], daim_ref[...], dbbre_ref[...], dbbim_ref[...]))
        dlr_ref[...] = dlr
        dli_ref[...] = dli
        dls_ref[...] = dls
        dbre_ref[...] = dbre
        dbim_ref[...] = dbim

    gp = jax.ShapeDtypeStruct((S5_GROUPS, S5_STATE), F32)
    g1 = jax.ShapeDtypeStruct((S5_GROUPS, 1), F32)
    hgp = jax.ShapeDtypeStruct((S5_GROUP, S5_GROUPS, S5_STATE), F32)
    return pl.pallas_call(body, name=name, out_shape=[gp, gp, g1, hgp, hgp])(
        lr, li, ls, bre, bim, dare, daim, dbbre, dbbim)


def _cmul_add(ar, ai, sr, si, br, bi):
    return ar * sr - ai * si + br, ar * si + ai * sr + bi


def _shift_rows_down(v):
    rolled = pltpu.roll(v, 1, 0)
    row = lax.broadcasted_iota(jnp.int32, v.shape, 0)
    return jnp.where(row == 0, 0.0, rolled)


def _shift_rows_up(v):
    rolled = pltpu.roll(v, SEGS - 1, 0)
    row = lax.broadcasted_iota(jnp.int32, v.shape, 0)
    return jnp.where(row == SEGS - 1, 0.0, rolled)


def _segment_power(ar, ai, steps):
    n = 1
    while n < steps:
        ar, ai = ar * ar - ai * ai, 2.0 * ar * ai
        n *= 2
    assert n == steps
    return ar, ai


def _half_segment_entries(ar, ai, first, second, half_steps, shift):
    pr, pi = _segment_power(ar, ai, half_steps)
    er = jnp.zeros_like(first[0])
    ei = jnp.zeros_like(first[1])
    for _ in range(SEGS - 1):
        mr, mi = _cmul_add(pr, pi, er, ei, *first)
        nr, ni = _cmul_add(pr, pi, mr, mi, *second)
        er, ei = shift(nr), shift(ni)
    mr, mi = _cmul_add(pr, pi, er, ei, *first)
    return (er, ei), (mr, mi)


def _s5_scan(u_perm, bre_bd, bim_bd, cre_bd, cim_bd, are, aim, name):
    L = u_perm.shape[0]
    half = L // SEGS // 2

    def body(u_ref, bre_ref, bim_ref, cre_ref, cim_ref, are_ref, aim_ref, y_ref, xr_ref, xi_ref):
        u = u_ref[...].astype(BF16)
        xr_ref[...] = _dot(u, bre_ref[0])
        xi_ref[...] = _dot(u, bim_ref[0])
        ar = jnp.broadcast_to(are_ref[0], (SEGS, S5_TILE_ST))
        ai = jnp.broadcast_to(aim_ref[0], (SEGS, S5_TILE_ST))
        zero = jnp.zeros((SEGS, S5_TILE_ST), F32)
        block = lambda j: pl.ds(pl.multiple_of(j * SEGS, SEGS), SEGS)

        def finals(j, c):
            lo, hi = block(j), block(j + half)
            return (*_cmul_add(ar, ai, c[0], c[1], xr_ref[lo, :], xi_ref[lo, :]),
                    *_cmul_add(ar, ai, c[2], c[3], xr_ref[hi, :], xi_ref[hi, :]))

        f = lax.fori_loop(0, half, finals, (zero,) * 4, unroll=4)
        e_lo, e_hi = _half_segment_entries(ar, ai, f[:2], f[2:], half, _shift_rows_down)

        def scan(j, c):
            lo, hi = block(j), block(j + half)
            s_lo = _cmul_add(ar, ai, c[0], c[1], xr_ref[lo, :], xi_ref[lo, :])
            s_hi = _cmul_add(ar, ai, c[2], c[3], xr_ref[hi, :], xi_ref[hi, :])
            xr_ref[lo, :], xi_ref[lo, :] = s_lo
            xr_ref[hi, :], xi_ref[hi, :] = s_hi
            return (*s_lo, *s_hi)

        lax.fori_loop(0, half, scan, (*e_lo, *e_hi), unroll=4)
        y_ref[...] = (_dot(xr_ref[...].astype(BF16), cre_ref[0]) - _dot(xi_ref[...].astype(BF16), cim_ref[0]))

    tile3 = lambda a, b: pl.BlockSpec((1, a, b), lambda k: (k, 0, 0))
    return pl.pallas_call(
        body, name=name, grid=(S5_TILES,),
        in_specs=[pl.BlockSpec((L, S5_TILE_IN), lambda k: (0, k)),
                  tile3(S5_TILE_IN, S5_TILE_ST), tile3(S5_TILE_IN, S5_TILE_ST),
                  tile3(S5_TILE_ST, S5_TILE_IN), tile3(S5_TILE_ST, S5_TILE_IN),
                  tile3(1, S5_TILE_ST), tile3(1, S5_TILE_ST)],
        out_specs=[pl.BlockSpec((L, S5_TILE_IN), lambda k: (0, k)),
                   pl.BlockSpec((L, S5_TILE_ST), lambda k: (0, k)), pl.BlockSpec((L, S5_TILE_ST), lambda k: (0, k))],
        out_shape=[jax.ShapeDtypeStruct((L, 1024), F32), jax.ShapeDtypeStruct((L, S5_COLS), F32),
                   jax.ShapeDtypeStruct((L, S5_COLS), F32)],
    )(u_perm, bre_bd, bim_bd, cre_bd, cim_bd, are, aim)


def _s5_scan_bwd(dy_perm, u_perm, xr, xi, bret_bd, bimt_bd, cret_bd, cimt_bd, are, aim, name):
    L = u_perm.shape[0]
    steps = L // SEGS
    half = steps // 2

    def body(dy_ref, u_ref, xr_ref, xi_ref, bret_ref, bimt_ref, cret_ref, cimt_ref, are_ref, aim_ref,
             du_ref, dar_ref, dai_ref, dcre_ref, dcim_ref, dbre_ref, dbim_ref, gr_ref, gi_ref):
        dy = dy_ref[...].astype(BF16)
        u = u_ref[...].astype(BF16)
        gr_ref[...] = _dot(dy, cret_ref[0])
        gi_ref[...] = -_dot(dy, cimt_ref[0])
        ar = jnp.broadcast_to(are_ref[0], (SEGS, S5_TILE_ST))
        ai = -jnp.broadcast_to(aim_ref[0], (SEGS, S5_TILE_ST))
        zero = jnp.zeros((SEGS, S5_TILE_ST), F32)
        block = lambda j: pl.ds(pl.multiple_of(j * SEGS, SEGS), SEGS)

        def finals(k, c):
            hi, lo = block(steps - 1 - k), block(half - 1 - k)
            return (*_cmul_add(ar, ai, c[0], c[1], gr_ref[hi, :], gi_ref[hi, :]),
                    *_cmul_add(ar, ai, c[2], c[3], gr_ref[lo, :], gi_ref[lo, :]))

        f = lax.fori_loop(0, half, finals, (zero,) * 4, unroll=4)
        e_hi, e_lo = _half_segment_entries(ar, ai, f[:2], f[2:], half, _shift_rows_up)

        def scan(k, c):
            accr, acci = c[4], c[5]
            j_hi, j_lo = steps - 1 - k, half - 1 - k
            hi, lo = block(j_hi), block(j_lo)
            hr, hi_im = _cmul_add(ar, ai, c[0], c[1], gr_ref[hi, :], gi_ref[hi, :])
            lr, lo_im = _cmul_add(ar, ai, c[2], c[3], gr_ref[lo, :], gi_ref[lo, :])
            gr_ref[hi, :], gi_ref[hi, :] = hr, hi_im
            gr_ref[lo, :], gi_ref[lo, :] = lr, lo_im
            before_hi = block(j_hi - 1)
            before_lo = block(jnp.maximum(j_lo - 1, 0))
            live = (j_lo > 0).astype(F32)
            xhr, xhi = xr_ref[before_hi, :], xi_ref[before_hi, :]
            xlr, xli = xr_ref[before_lo, :] * live, xi_ref[before_lo, :] * live
            accr = accr + (hr * xhr + hi_im * xhi) + (lr * xlr + lo_im * xli)
            acci = acci + (hi_im * xhr - hr * xhi) + (lo_im * xlr - lr * xli)
            return hr, hi_im, lr, lo_im, accr, acci

        out = lax.fori_loop(0, half, scan, (*e_hi, *e_lo, zero, zero), unroll=2)
        accr, acci = out[4], out[5]
        first = pl.ds(0, SEGS)
        last = pl.ds((steps - 1) * SEGS, SEGS)
        xpr = _shift_rows_down(xr_ref[last, :])
        xpi = _shift_rows_down(xi_ref[last, :])
        g0r = gr_ref[first, :]
        g0i = gi_ref[first, :]
        accr = accr + g0r * xpr + g0i * xpi
        acci = acci + g0i * xpr - g0r * xpi
        dar_ref[0] = jnp.sum(accr, axis=0, keepdims=True)
        dai_ref[0] = jnp.sum(acci, axis=0, keepdims=True)

        grb = gr_ref[...].astype(BF16)
        gib = gi_ref[...].astype(BF16)
        du_ref[...] = _dot(grb, bret_ref[0]) + _dot(gib, bimt_ref[0])
        dbre_ref[0] = _dot_tn(u, grb)
        dbim_ref[0] = _dot_tn(u, gib)
        dcre_ref[0] = _dot_tn(dy, xr_ref[...].astype(BF16))
        dcim_ref[0] = -_dot_tn(dy, xi_ref[...].astype(BF16))

    tile3 = lambda a, b: pl.BlockSpec((1, a, b), lambda k: (k, 0, 0))
    col_in = pl.BlockSpec((L, S5_TILE_IN), lambda k: (0, k))
    col_st = pl.BlockSpec((L, S5_TILE_ST), lambda k: (0, k))
    dense = jax.ShapeDtypeStruct((S5_TILES, S5_TILE_IN, S5_TILE_ST), F32)
    vec = jax.ShapeDtypeStruct((S5_TILES, 1, S5_TILE_ST), F32)
    return pl.pallas_call(
        body, name=name, grid=(S5_TILES,),
        in_specs=[col_in, col_in, col_st, col_st,
                  tile3(S5_TILE_ST, S5_TILE_IN), tile3(S5_TILE_ST, S5_TILE_IN),
                  tile3(S5_TILE_IN, S5_TILE_ST), tile3(S5_TILE_IN, S5_TILE_ST),
                  tile3(1, S5_TILE_ST), tile3(1, S5_TILE_ST)],
        out_specs=[col_in, tile3(1, S5_TILE_ST), tile3(1, S5_TILE_ST),
                   tile3(S5_TILE_IN, S5_TILE_ST), tile3(S5_TILE_IN, S5_TILE_ST),
                   tile3(S5_TILE_IN, S5_TILE_ST), tile3(S5_TILE_IN, S5_TILE_ST)],
        out_shape=[jax.ShapeDtypeStruct((L, 1024), F32), vec, vec, dense, dense, dense, dense],
        scratch_shapes=[pltpu.VMEM((L, S5_TILE_ST), F32), pltpu.VMEM((L, S5_TILE_ST), F32)],
    )(dy_perm, u_perm, xr, xi, bret_bd, bimt_bd, cret_bd, cimt_bd, are, aim)


def _s5_post(ys, u, d_skip, w_glu, b_glu, gain, layer, name):
    L = ys.shape[0]

    def body(ys_ref, u_ref, d_ref, w_ref, b_ref, g_ref, ya_ref):
        g = _gelu(ys_ref[...] + d_ref[...] * u_ref[...])
        q = _dot(g.astype(BF16), w_ref[...]) + b_ref[...]
        oa = g * _sigmoid(q)
        ya_ref[...] = (oa * _rstd(oa) * g_ref[...]).astype(BF16)

    vec = _full_spec((1, 1024))
    tile = min(L, WIDE_ROW_TILE)
    return pl.pallas_call(
        body, name=name, grid=(L // tile,),
        in_specs=[_row_spec(tile, 1024), _row_spec(tile, 1024), vec, _layer_spec((1024, 1024), layer), vec, vec],
        out_specs=_row_spec(tile, 1024),
        out_shape=jax.ShapeDtypeStruct((L, 1024), BF16),
    )(ys, u, d_skip, w_glu, b_glu, gain)


def _s5_post_bwd(dx, w_out, ys, u, d_skip, w_glu, b_glu, gain, layer, name):
    L = ys.shape[0]

    def body(dx_ref, wo_ref, ys_ref, u_ref, d_ref, w_ref, b_ref, gn_ref,
             dys_ref, dus_ref, g_ref, dq_ref, dgain_ref, dd_ref, db_ref):
        first = pl.program_id(0) == 0
        uv = u_ref[...]
        yt = ys_ref[...] + d_ref[...] * uv
        g = _gelu(yt)
        gb = g.astype(BF16)
        q = _dot(gb, w_ref[...]) + b_ref[...]
        s = _sigmoid(q)
        oa = g * s
        dya = _dot_nt(dx_ref[...], wo_ref[...])
        doa, dgain = _rms_bwd(oa, _rstd(oa), gn_ref[...], dya)
        dq = doa * g * s * (1.0 - s)
        dqb = dq.astype(BF16)
        dg = doa * s + _dot_nt(dqb, w_ref[...])
        dyt = dg * _dgelu(yt)
        dys_ref[...] = dyt
        dus_ref[...] = dyt * d_ref[...]
        g_ref[...] = gb
        dq_ref[...] = dqb
        _acc_rows(dgain_ref, dgain, first)
        _acc_rows(dd_ref, jnp.sum(dyt * uv, axis=0, keepdims=True), first)
        _acc_rows(db_ref, jnp.sum(dq, axis=0, keepdims=True), first)

    vec = _full_spec((1, 1024))
    row = _row_spec(ROW_TILE, 1024)
    vshape = jax.ShapeDtypeStruct((1, 1024), F32)
    return pl.pallas_call(
        body, name=name, grid=(L // ROW_TILE,),
        in_specs=[row, _layer_spec((1024, 1024), layer, 0), row, row, vec, _layer_spec((1024, 1024), layer), vec,
                  vec],
        out_specs=[row, row, row, row, vec, vec, vec],
        out_shape=[jax.ShapeDtypeStruct((L, 1024), F32), jax.ShapeDtypeStruct((L, 1024), F32),
                   jax.ShapeDtypeStruct((L, 1024), BF16), jax.ShapeDtypeStruct((L, 1024), BF16),
                   vshape, vshape, vshape],
    )(dx, w_out, ys, u, d_skip, w_glu, b_glu, gain)


CONV_TILE = 256


def _shift_time(v, d):
    if d == 0:
        return v
    rolled = pltpu.roll(v, d, 0)
    row = lax.broadcasted_iota(jnp.int32, v.shape, 0)
    return jnp.where(row < d, 0.0, rolled)


def _unshift_time(v, d):
    if d == 0:
        return v
    n = v.shape[0]
    rolled = pltpu.roll(v, n - d, 0)
    row = lax.broadcasted_iota(jnp.int32, v.shape, 0)
    return jnp.where(row >= n - d, 0.0, rolled)


def _ssd_conv(xbc, w, b, name):
    L = xbc.shape[0]

    def body(x_ref, w_ref, b_ref, o_ref):
        xv = x_ref[...]
        pre = jnp.broadcast_to(b_ref[...], xv.shape)
        for k in range(SSD_CONV):
            pre = pre + w_ref[k:k + 1, :] * _shift_time(xv, SSD_CONV - 1 - k)
        o_ref[...] = _silu(pre)

    col = pl.BlockSpec((L, CONV_TILE), lambda j: (0, j))
    return pl.pallas_call(
        body, name=name, grid=(SSD_CONV_DIM // CONV_TILE,),
        in_specs=[col, pl.BlockSpec((8, CONV_TILE), lambda j: (0, j)), pl.BlockSpec((1, CONV_TILE), lambda j: (0, j))],
        out_specs=col, out_shape=jax.ShapeDtypeStruct((L, SSD_CONV_DIM), F32),
    )(xbc, w, b)


def _ssd_conv_bwd(dxc, xbc, w, b, name):
    L = xbc.shape[0]

    def body(d_ref, x_ref, w_ref, b_ref, dx_ref, dw_ref, db_ref):
        xv = x_ref[...]
        shifted = [_shift_time(xv, SSD_CONV - 1 - k) for k in range(SSD_CONV)]
        pre = jnp.broadcast_to(b_ref[...], xv.shape)
        for k in range(SSD_CONV):
            pre = pre + w_ref[k:k + 1, :] * shifted[k]
        dpre = d_ref[...] * _dsilu(pre)
        dx = jnp.zeros_like(xv)
        rows = []
        for k in range(SSD_CONV):
            dx = dx + w_ref[k:k + 1, :] * _unshift_time(dpre, SSD_CONV - 1 - k)
            rows.append(jnp.sum(dpre * shifted[k], axis=0, keepdims=True))
        dx_ref[...] = dx
        dw_ref[...] = jnp.concatenate(rows + [jnp.zeros((8 - SSD_CONV, CONV_TILE), F32)], axis=0)
        db_ref[...] = jnp.sum(dpre, axis=0, keepdims=True)

    col = pl.BlockSpec((L, CONV_TILE), lambda j: (0, j))
    w_spec = pl.BlockSpec((8, CONV_TILE), lambda j: (0, j))
    b_spec = pl.BlockSpec((1, CONV_TILE), lambda j: (0, j))
    return pl.pallas_call(
        body, name=name, grid=(SSD_CONV_DIM // CONV_TILE,),
        in_specs=[col, col, w_spec, b_spec], out_specs=[col, w_spec, b_spec],
        out_shape=[jax.ShapeDtypeStruct((L, SSD_CONV_DIM), F32), jax.ShapeDtypeStruct((8, SSD_CONV_DIM), F32),
                   jax.ShapeDtypeStruct((1, SSD_CONV_DIM), F32)],
    )(dxc, xbc, w, b)


def _tri(lower):
    r = lax.broadcasted_iota(jnp.int32, (SSD_CHUNK, SSD_CHUNK), 0)
    c = lax.broadcasted_iota(jnp.int32, (SSD_CHUNK, SSD_CHUNK), 1)
    return (r >= c) if lower else (r <= c)


def _ssd_chunk_common(dt_ref, bias_ref, alog_ref):
    pre = dt_ref[...] + bias_ref[0]
    dtp = _softplus(pre)
    a_neg = -jnp.exp(alog_ref[0])
    dta = dtp * a_neg
    acum = _select_rows(_tri(True), dta)
    return pre, dtp, a_neg, dta, acum


GROUP_W = SSD_GROUP_HEADS * SSD_HEAD_DIM


def _head_expander():
    r = lax.broadcasted_iota(jnp.int32, (LANES, GROUP_W), 0)
    c = lax.broadcasted_iota(jnp.int32, (LANES, GROUP_W), 1)
    return (c // SSD_HEAD_DIM == r).astype(F32)


def _split_bf16(a, terms):
    parts = []
    rest = a
    for _ in range(terms):
        piece = rest.astype(BF16)
        parts.append(piece)
        rest = rest - piece.astype(F32)
    return parts


def _select_cols(a, sel, terms=3):
    lhs = jnp.concatenate(_split_bf16(a, terms), axis=1)
    rhs = jnp.concatenate([sel.astype(BF16)] * terms, axis=0)
    return _dot(lhs, rhs)


def _select_rows(sel, b, terms=3):
    lhs = jnp.concatenate([sel.astype(BF16)] * terms, axis=1)
    rhs = jnp.concatenate(_split_bf16(b, terms), axis=0)
    return _dot(lhs, rhs)


def _decay_mask(acum_all, acum_t, h, lower):
    seg = acum_all[:, h:h + 1] - acum_t[h:h + 1, :]
    return jnp.where(lower, jnp.exp(jnp.minimum(seg, 0.0)), 0.0)


def _ssd_scan(xc, dt, dt_bias, a_log, d_wide, expand, expand_t, name):
    L = xc.shape[0]
    nc = L // SSD_CHUNK

    def body(x_ref, b_ref, c_ref, dt_ref, bias_ref, alog_ref, d_ref, e_ref, et_ref, y_ref, sp_ref, s_ref, xdt_ref):
        @pl.when(pl.program_id(1) == 0)
        def _():
            s_ref[...] = jnp.zeros_like(s_ref)

        _, dtp_all, _, _, acum_all = _ssd_chunk_common(dt_ref, bias_ref, alog_ref)
        acum_t = acum_all.T
        wide = _select_cols(jnp.concatenate([acum_all, dtp_all], axis=0), e_ref[...])
        acum_e = wide[:SSD_CHUNK]
        alast_e = acum_e[SSD_CHUNK - 1:SSD_CHUNK, :]
        x = x_ref[...]
        xdt = x * wide[SSD_CHUNK:]
        xdt_ref[...] = xdt.astype(BF16)
        bm = b_ref[...].astype(BF16)
        cm = c_ref[...].astype(BF16)
        cb = _dot_nt(cm, bm)
        lower = _tri(True)
        sp = s_ref[...]
        for h in range(SSD_GROUP_HEADS):
            cols = slice(h * SSD_HEAD_DIM, (h + 1) * SSD_HEAD_DIM)
            lm = _decay_mask(acum_all, acum_t, h, lower)
            y_ref[:, cols] = _dot((cb * lm).astype(BF16), xdt_ref[:, cols])
        y_ref[...] += _dot_nt(cm, sp.astype(BF16)) * jnp.exp(acum_e) + d_ref[0] * x
        wgt = xdt * jnp.exp(alast_e - acum_e)
        ealast = jnp.exp(_select_rows(et_ref[...], acum_t)[:, SSD_CHUNK - 1:SSD_CHUNK])
        sp_ref[0, 0] = sp
        s_ref[...] = ealast * sp + _dot_tn(wgt.astype(BF16), bm)

    par = lambda n: pl.BlockSpec((1, 1, n), lambda g, c: (g, 0, 0))
    return pl.pallas_call(
        body, name=name, grid=(SSD_GROUPS, nc),
        in_specs=[pl.BlockSpec((SSD_CHUNK, GROUP_W), lambda g, c: (c, g)),
                  pl.BlockSpec((SSD_CHUNK, SSD_STATE), lambda g, c: (c, 8 + g)),
                  pl.BlockSpec((SSD_CHUNK, SSD_STATE), lambda g, c: (c, 10 + g)),
                  pl.BlockSpec((SSD_CHUNK, LANES), lambda g, c: (c, g)),
                  par(LANES), par(LANES), par(GROUP_W), _full_spec((LANES, GROUP_W)), _full_spec((GROUP_W, LANES))],
        out_specs=[pl.BlockSpec((SSD_CHUNK, GROUP_W), lambda g, c: (c, g)),
                   pl.BlockSpec((1, 1, GROUP_W, SSD_STATE), lambda g, c: (c, g, 0, 0))],
        out_shape=[jax.ShapeDtypeStruct((L, SSD_WIDTH), F32),
                   jax.ShapeDtypeStruct((nc, SSD_GROUPS, GROUP_W, SSD_STATE), F32)],
        scratch_shapes=[pltpu.VMEM((GROUP_W, SSD_STATE), F32), pltpu.VMEM((SSD_CHUNK, GROUP_W), BF16)],
    )(xc, xc, xc, dt, dt_bias, a_log, d_wide, expand, expand_t)


def _ssd_scan_bwd(dy, xc, dt, sprev, dt_bias, a_log, d_wide, expand, expand_t, name):
    L = xc.shape[0]
    nc = L // SSD_CHUNK

    def body(dy_ref, x_ref, b_ref, c_ref, dt_ref, sp_ref, bias_ref, alog_ref, d_ref, e_ref, et_ref,
             dx_ref, db_ref, dc_ref, ddt_ref, dbias_ref, dalog_ref, dd_ref, ds_ref, xdt_ref, dyb_ref):
        first = pl.program_id(1) == 0

        @pl.when(first)
        def _():
            ds_ref[...] = jnp.zeros_like(ds_ref)

        pre, dtp_all, a_neg, _, acum_all = _ssd_chunk_common(dt_ref, bias_ref, alog_ref)
        acum_t = acum_all.T
        e = e_ref[...]
        et = et_ref[...]
        wide = _select_cols(jnp.concatenate([acum_all, dtp_all], axis=0), e)
        acum_e = wide[:SSD_CHUNK]
        dtp_e = wide[SSD_CHUNK:]
        alast_e = acum_e[SSD_CHUNK - 1:SSD_CHUNK, :]
        dstate_e = jnp.exp(alast_e - acum_e)
        x = x_ref[...]
        dy = dy_ref[...]
        xdt = x * dtp_e
        xdt_ref[...] = xdt.astype(BF16)
        dyb_ref[...] = dy.astype(BF16)
        bm = b_ref[...].astype(BF16)
        cm = c_ref[...].astype(BF16)
        cb = _dot_nt(cm, bm)
        sp = sp_ref[0, 0]
        spb = sp.astype(BF16)
        dsn = ds_ref[...]
        dsb = dsn.astype(BF16)
        z = _dot_nt(cm, spb)
        dz = dy * jnp.exp(acum_e)
        dzb = dz.astype(BF16)
        dc_acc = _dot(dzb, spb)
        ealast = jnp.exp(_select_rows(et, acum_t)[:, SSD_CHUNK - 1:SSD_CHUNK])
        ds_ref[...] = _dot_tn(dzb, cm) + ealast * dsn
        dw = _dot_nt(bm, dsb)
        wgt = xdt * dstate_e
        db_acc = _dot(wgt.astype(BF16), dsb)
        lower = _tri(True)
        lane = lax.broadcasted_iota(jnp.int32, (SSD_CHUNK, LANES), 1)
        row = lax.broadcasted_iota(jnp.int32, (SSD_CHUNK, LANES), 0)
        dcb = jnp.zeros((SSD_CHUNK, SSD_CHUNK), F32)
        dacum_all = jnp.zeros((SSD_CHUNK, LANES), F32)
        dacum_cols = jnp.zeros((SSD_CHUNK, LANES), F32)
        for h in range(SSD_GROUP_HEADS):
            cols = slice(h * SSD_HEAD_DIM, (h + 1) * SSD_HEAD_DIM)
            lm = _decay_mask(acum_all, acum_t, h, lower)
            dm = _dot_nt(dyb_ref[:, cols], xdt_ref[:, cols])
            dx_ref[:, cols] = _dot_tn((cb * lm).astype(BF16), dyb_ref[:, cols])
            dm_lm = dm * lm
            dcb = dcb + dm_lm
            q = dm_lm * cb
            dacum_all = jnp.where(lane == h, jnp.sum(q, axis=1, keepdims=True), dacum_all)
            dacum_cols = jnp.where(row == h, jnp.sum(q, axis=0, keepdims=True), dacum_cols)
        dxdt = dx_ref[...] + dw * dstate_e
        sums = _select_cols(jnp.concatenate([dz * z, dw * wgt, dxdt * x, dy * x], axis=0), et, terms=2)
        dacum_off = sums[0:SSD_CHUNK]
        dds_ds = sums[SSD_CHUNK:2 * SSD_CHUNK]
        ddtp_x = sums[2 * SSD_CHUNK:3 * SSD_CHUNK]
        dd_part = sums[3 * SSD_CHUNK:4 * SSD_CHUNK]
        ds_s = jnp.sum(_select_rows(e, dsn * sp, terms=2).T, axis=0, keepdims=True)
        dalast = ds_s * jnp.exp(acum_all[SSD_CHUNK - 1:SSD_CHUNK, :]) + jnp.sum(dds_ds, axis=0, keepdims=True)
        dacum_all = dacum_all - dacum_cols.T + dacum_off - dds_ds + jnp.where(row == SSD_CHUNK - 1, dalast, 0.0)
        dx_ref[...] = d_ref[0] * dy + dxdt * dtp_e
        dcbb = dcb.astype(BF16)
        dc_ref[...] = dc_acc + _dot(dcbb, bm)
        db_ref[...] = db_acc + _dot_tn(dcbb, cm)
        ddta = _select_rows(_tri(False), dacum_all)
        ddt = (ddtp_x + ddta * a_neg) * _sigmoid(pre)
        ddt_ref[...] = ddt
        _acc_rows(dbias_ref, jnp.sum(ddt, axis=0, keepdims=True)[None], first)
        _acc_rows(dalog_ref, (jnp.sum(ddta * dtp_all, axis=0, keepdims=True) * a_neg)[None], first)
        _acc_rows(dd_ref, jnp.sum(dd_part, axis=0, keepdims=True)[None], first)

    rev = lambda c: nc - 1 - c
    par = lambda n: pl.BlockSpec((1, 1, n), lambda g, c: (g, 0, 0))
    pshape = jax.ShapeDtypeStruct((SSD_GROUPS, 1, LANES), F32)
    return pl.pallas_call(
        body, name=name, grid=(SSD_GROUPS, nc),
        in_specs=[pl.BlockSpec((SSD_CHUNK, GROUP_W), lambda g, c: (rev(c), g)),
                  pl.BlockSpec((SSD_CHUNK, GROUP_W), lambda g, c: (rev(c), g)),
                  pl.BlockSpec((SSD_CHUNK, SSD_STATE), lambda g, c: (rev(c), 8 + g)),
                  pl.BlockSpec((SSD_CHUNK, SSD_STATE), lambda g, c: (rev(c), 10 + g)),
                  pl.BlockSpec((SSD_CHUNK, LANES), lambda g, c: (rev(c), g)),
                  pl.BlockSpec((1, 1, GROUP_W, SSD_STATE), lambda g, c: (rev(c), g, 0, 0)),
                  par(LANES), par(LANES), par(GROUP_W), _full_spec((LANES, GROUP_W)), _full_spec((GROUP_W, LANES))],
        out_specs=[pl.BlockSpec((SSD_CHUNK, GROUP_W), lambda g, c: (rev(c), g)),
                   pl.BlockSpec((SSD_CHUNK, SSD_STATE), lambda g, c: (rev(c), g)),
                   pl.BlockSpec((SSD_CHUNK, SSD_STATE), lambda g, c: (rev(c), g)),
                   pl.BlockSpec((SSD_CHUNK, LANES), lambda g, c: (rev(c), g)),
                   par(LANES), par(LANES), par(LANES)],
        out_shape=[jax.ShapeDtypeStruct((L, SSD_WIDTH), F32), jax.ShapeDtypeStruct((L, 256), F32),
                   jax.ShapeDtypeStruct((L, 256), F32), jax.ShapeDtypeStruct((L, 256), F32),
                   pshape, pshape, pshape],
        scratch_shapes=[pltpu.VMEM((GROUP_W, SSD_STATE), F32), pltpu.VMEM((SSD_CHUNK, GROUP_W), BF16),
                        pltpu.VMEM((SSD_CHUNK, GROUP_W), BF16)],
    )(dy, xc, xc, xc, dt, sprev, dt_bias, a_log, d_wide, expand, expand_t)


def _ssd_post(y, z, gain, name):
    L = y.shape[0]

    def body(y_ref, z_ref, g_ref, o_ref):
        ob = y_ref[...] * _silu(z_ref[...])
        o_ref[...] = (ob * _rstd(ob) * g_ref[...]).astype(BF16)

    tile = min(L, WIDE_ROW_TILE)
    row = _row_spec(tile, 1024)
    return pl.pallas_call(body, name=name, grid=(L // tile,), in_specs=[row, row, _full_spec((1, 1024))],
                          out_specs=row, out_shape=jax.ShapeDtypeStruct((L, 1024), BF16))(y, z, gain)


def _ssd_post_bwd(dx, w_out, y, z, gain, layer, name):
    L = y.shape[0]

    def body(dx_ref, wo_ref, y_ref, z_ref, g_ref, dy_ref, dz_ref, dgain_ref):
        first = pl.program_id(0) == 0
        yv = y_ref[...]
        zv = z_ref[...]
        sz = _silu(zv)
        ob = yv * sz
        dyb = _dot_nt(dx_ref[...], wo_ref[...])
        dob, dgain = _rms_bwd(ob, _rstd(ob), g_ref[...], dyb)
        dy_ref[...] = dob * sz
        dz_ref[...] = dob * yv * _dsilu(zv)
        _acc_rows(dgain_ref, dgain, first)

    row = _row_spec(ROW_TILE, 1024)
    vec = _full_spec((1, 1024))
    return pl.pallas_call(
        body, name=name, grid=(L // ROW_TILE,),
        in_specs=[row, _layer_spec((1024, 1024), layer, 1), row, row, vec],
        out_specs=[row, row, vec],
        out_shape=[jax.ShapeDtypeStruct((L, 1024), F32), jax.ShapeDtypeStruct((L, 1024), F32),
                   jax.ShapeDtypeStruct((1, 1024), F32)],
    )(dx, w_out, y, z, gain)


def _out_proj(x, ya, yb, w_out, layer, name):
    L = x.shape[0]

    def body(x_ref, ya_ref, yb_ref, w_ref, o_ref):
        o_ref[...] = x_ref[...] + _dot(ya_ref[...], w_ref[:1024, :]) + _dot(yb_ref[...], w_ref[1024:, :])

    tile = min(L, WIDE_ROW_TILE)
    row = _row_spec(tile, 1024)
    return pl.pallas_call(body, name=name, grid=(L // tile,),
                          in_specs=[row, row, row, _layer_spec((2048, 1024), layer)],
                          out_specs=row, out_shape=jax.ShapeDtypeStruct((L, D_MODEL), F32))(x, ya, yb, w_out)


def _ffn(x, gain, w_gate, w_up, w_down, layer, name):
    L = x.shape[0]

    def body(x_ref, g_ref, wg_ref, wu_ref, wd_ref, o_ref, gt_ref, up_ref):
        xv = x_ref[...]
        h = (xv * _rstd(xv) * g_ref[...]).astype(BF16)
        gt = _dot_nt(h, wg_ref[...])
        up = _dot_nt(h, wu_ref[...])
        gt_ref[...] = gt
        up_ref[...] = up
        o_ref[...] = xv + _dot((_silu(gt) * up).astype(BF16), wd_ref[...])

    row = _row_spec(ROW_TILE, D_MODEL)
    hid = _row_spec(ROW_TILE, FFN)
    return pl.pallas_call(
        body, name=name, grid=(L // ROW_TILE,),
        in_specs=[row, _full_spec((1, D_MODEL)), _layer_spec((FFN, D_MODEL), layer),
                  _layer_spec((FFN, D_MODEL), layer), _layer_spec((FFN, D_MODEL), layer)],
        out_specs=[row, hid, hid],
        out_shape=[jax.ShapeDtypeStruct((L, D_MODEL), F32), jax.ShapeDtypeStruct((L, FFN), F32),
                   jax.ShapeDtypeStruct((L, FFN), F32)],
    )(x, gain, w_gate, w_up, w_down)


def _ffn_bwd(dx2, x1, gt, up, gain, w_gate, w_up, w_down, layer, name):
    L = x1.shape[0]

    def body(d_ref, x_ref, gt_ref, up_ref, g_ref, wg_ref, wu_ref, wd_ref,
             dx_ref, dxb_ref, h_ref, act_ref, dgt_ref, dup_ref, dgain_ref):
        first = pl.program_id(0) == 0
        dv = d_ref[...]
        xv = x_ref[...]
        r = _rstd(xv)
        h_ref[...] = (xv * r * g_ref[...]).astype(BF16)
        gtv = gt_ref[...]
        upv = up_ref[...]
        sg = _silu(gtv)
        act_ref[...] = (sg * upv).astype(BF16)
        dact = _dot_nt(dv.astype(BF16), wd_ref[...])
        dgt = (dact * upv * _dsilu(gtv)).astype(BF16)
        dup = (dact * sg).astype(BF16)
        dgt_ref[...] = dgt
        dup_ref[...] = dup
        dh = _dot(dgt, wg_ref[...]) + _dot(dup, wu_ref[...])
        dxn, dgain = _rms_bwd(xv, r, g_ref[...], dh)
        dx = dv + dxn
        dx_ref[...] = dx
        dxb_ref[...] = dx.astype(BF16)
        _acc_rows(dgain_ref, dgain, first)

    row = _row_spec(ROW_TILE, D_MODEL)
    hid = _row_spec(ROW_TILE, FFN)
    vec = _full_spec((1, D_MODEL))
    return pl.pallas_call(
        body, name=name, grid=(L // ROW_TILE,),
        in_specs=[row, row, hid, hid, vec, _layer_spec((FFN, D_MODEL), layer), _layer_spec((FFN, D_MODEL), layer),
                  _layer_spec((FFN, D_MODEL), layer)],
        out_specs=[row, row, row, hid, hid, hid, vec],
        out_shape=[jax.ShapeDtypeStruct((L, D_MODEL), F32), jax.ShapeDtypeStruct((L, D_MODEL), BF16),
                   jax.ShapeDtypeStruct((L, D_MODEL), BF16),
                   jax.ShapeDtypeStruct((L, FFN), BF16), jax.ShapeDtypeStruct((L, FFN), BF16),
                   jax.ShapeDtypeStruct((L, FFN), BF16), jax.ShapeDtypeStruct((1, D_MODEL), F32)],
    )(dx2, x1, gt, up, gain, w_gate, w_up, w_down)


def _inproj_bwd(dx1, x0, du_skip, du_scan, dz, dxbc, ddt, gain, w_pad, layer, name):
    L = x0.shape[0]

    def body(d_ref, x_ref, dus_ref, duc_ref, dz_ref, dxbc_ref, ddt_ref, g_ref, w_ref,
             dx_ref, dxb_ref, dp_ref, dgain_ref):
        first = pl.program_id(0) == 0
        xv = x_ref[...]
        dp = jnp.concatenate([dus_ref[...] + duc_ref[...], dz_ref[...], dxbc_ref[...], ddt_ref[...]],
                             axis=1).astype(BF16)
        dp_ref[...] = dp
        dh = _dot_nt(dp, w_ref[...])
        dxn, dgain = _rms_bwd(xv, _rstd(xv), g_ref[...], dh)
        dx = d_ref[...] + dxn
        dx_ref[...] = dx
        dxb_ref[...] = dx.astype(BF16)
        _acc_rows(dgain_ref, dgain, first)

    row = _row_spec(ROW_TILE, D_MODEL)
    vec = _full_spec((1, D_MODEL))
    return pl.pallas_call(
        body, name=name, grid=(L // ROW_TILE,),
        in_specs=[row, row, row, row, row, _row_spec(ROW_TILE, SSD_CONV_DIM), _row_spec(ROW_TILE, 256), vec,
                  _layer_spec((D_MODEL, IN_PAD), layer)],
        out_specs=[row, row, _row_spec(ROW_TILE, IN_PAD), vec],
        out_shape=[jax.ShapeDtypeStruct((L, D_MODEL), F32), jax.ShapeDtypeStruct((L, D_MODEL), BF16),
                   jax.ShapeDtypeStruct((L, IN_PAD), BF16), jax.ShapeDtypeStruct((1, D_MODEL), F32)],
    )(dx1, x0, du_skip, du_scan, dz, dxbc, ddt, gain, w_pad)


def _final_loss(x, gain, target, name):
    L = x.shape[0]

    def body(x_ref, g_ref, t_ref, loss_ref, dx_ref, dxb_ref, dgain_ref):
        first = pl.program_id(0) == 0
        xv = x_ref[...]
        r = _rstd(xv)
        err = xv * r * g_ref[...] - t_ref[...]
        part = 0.5 * jnp.sum(jnp.mean(err * err, axis=-1, keepdims=True), axis=0, keepdims=True)
        dx, dgain = _rms_bwd(xv, r, g_ref[...], err * (1.0 / D_MODEL))
        dx_ref[...] = dx
        dxb_ref[...] = dx.astype(BF16)
        _acc_rows(loss_ref, jnp.broadcast_to(part, (1, LANES)), first)
        _acc_rows(dgain_ref, dgain, first)

    row = _row_spec(ROW_TILE, D_MODEL)
    vec = _full_spec((1, D_MODEL))
    return pl.pallas_call(
        body, name=name, grid=(L // ROW_TILE,), in_specs=[row, vec, row],
        out_specs=[_full_spec((1, LANES)), row, row, vec],
        out_shape=[jax.ShapeDtypeStruct((1, LANES), F32), jax.ShapeDtypeStruct((L, D_MODEL), F32),
                   jax.ShapeDtypeStruct((L, D_MODEL), BF16), jax.ShapeDtypeStruct((1, D_MODEL), F32)],
    )(x, gain, target)


def _to_segments(a):
    L, n = a.shape
    return a.reshape(SEGS, L // SEGS, n).transpose(1, 0, 2).reshape(L, n)


def _from_segments(a):
    L, n = a.shape
    return a.reshape(L // SEGS, SEGS, n).transpose(1, 0, 2).reshape(L, n)


def _diag_block(g):
    k, a = divmod(g, S5_TILE_GROUPS)
    return k, slice(a * S5_GROUP, (a + 1) * S5_GROUP), slice(a * S5_STATE, (a + 1) * S5_STATE)


def _block_diag_build(mats, name):
    n = mats.shape[0]

    def body(m_ref, o_ref):
        o_ref[...] = jnp.zeros(o_ref.shape, BF16)
        for q in range(n):
            for g in range(S5_GROUPS):
                k, rows, cols = _diag_block(g)
                o_ref[q, k, rows, cols] = m_ref[q, g].astype(BF16)

    return pl.pallas_call(body, name=name,
                          out_shape=jax.ShapeDtypeStruct((n, S5_TILES, S5_TILE_IN, S5_TILE_ST), BF16))(mats)


def _block_diag_extract(dense, name):
    n = len(dense)

    def body(*refs):
        o_ref = refs[n]
        for q in range(n):
            for g in range(S5_GROUPS):
                k, rows, cols = _diag_block(g)
                o_ref[q, g] = refs[q][k, rows, cols]

    return pl.pallas_call(body, name=name,
                          out_shape=jax.ShapeDtypeStruct((n, S5_GROUPS, S5_GROUP, S5_STATE), F32))(*dense)


def _pad_in_proj(w):
    z = jnp.zeros(w.shape[:-1] + (LANES - SSD_GROUP_HEADS,), w.dtype)
    return jnp.concatenate([w[..., :IN_MAIN + 8], z, w[..., IN_MAIN + 8:], z], axis=-1)


def _unpad_in_proj(w):
    return jnp.concatenate([w[..., :IN_MAIN + 8], w[..., IN_MAIN + LANES:IN_MAIN + LANES + 8]], axis=-1)


def _lane_dense(a):
    return a.reshape(DEPTH, D_MODEL // LANES, LANES, -1).transpose(3, 1, 0, 2).reshape(-1, LANES)


def _from_lane_dense(a):
    return a.reshape(-1, D_MODEL // LANES, DEPTH, LANES).transpose(2, 1, 3, 0).reshape(DEPTH, D_MODEL, -1)


def _pad_heads(v):
    v = v.reshape(SSD_GROUPS, 1, SSD_GROUP_HEADS)
    return jnp.pad(v, ((0, 0), (0, 0), (0, LANES - SSD_GROUP_HEADS)))


def _unpad_heads(v):
    return v[:, 0, :SSD_GROUP_HEADS].reshape(SSD_HEADS)


def _layer_forward(x0, p, big, i, after_inproj=None, before_s5_post=None, before_ffn=None):
    tag = "l%d_" % i
    ls = p["s5_log_step"].reshape(S5_GROUPS, 1)
    b_hgp = (p["s5_b_re"].transpose(2, 0, 1), p["s5_b_im"].transpose(2, 0, 1))
    are, aim, bbre, bbim = _s5_prep(p["s5_lam_re"], p["s5_lam_im"], ls, b_hgp[0], b_hgp[1], tag + "s5_prep")
    mats = jnp.stack([bbre.transpose(1, 0, 2), bbim.transpose(1, 0, 2), p["s5_c_re"], p["s5_c_im"]])
    bre_bd, bim_bd, cret_bd, cimt_bd = _block_diag_build(mats, tag + "s5_blockdiag")
    s5mats = dict(bre_bd=bre_bd, bim_bd=bim_bd, cret_bd=cret_bd, cimt_bd=cimt_bd,
                  bret_bd=bre_bd.transpose(0, 2, 1), bimt_bd=bim_bd.transpose(0, 2, 1),
                  cre_bd=cret_bd.transpose(0, 2, 1), cim_bd=cimt_bd.transpose(0, 2, 1),
                  are=are.reshape(S5_TILES, 1, S5_TILE_ST), aim=aim.reshape(S5_TILES, 1, S5_TILE_ST))

    u, z, xbc, dt, h1 = _rms_inproj(x0, p["norm_mix"].reshape(1, -1), big["w_in"], None, tag + "rms_inproj")
    if after_inproj is not None:
        after_inproj(u)
    u_perm = _to_segments(u)
    ys_perm, xr, xi = _s5_scan(u_perm, bre_bd, bim_bd, s5mats["cre_bd"], s5mats["cim_bd"],
                               s5mats["are"], s5mats["aim"], tag + "s5_scan")
    ys = _from_segments(ys_perm)
    late_matrices = before_s5_post(ys) if before_s5_post is not None else {}
    big = {**big, **late_matrices}
    ya = _s5_post(ys, u, p["s5_d"].reshape(1, -1), big["s5_w_glu"], p["s5_b_glu"].reshape(1, -1),
                  p["s5_norm"].reshape(1, -1), None, tag + "s5_post")

    conv_w = jnp.pad(p["ssd_conv_w"], ((0, 8 - SSD_CONV), (0, 0)))
    conv_b = p["ssd_conv_b"].reshape(1, -1)
    xc = _ssd_conv(xbc, conv_w, conv_b, tag + "ssd_conv")
    expand = _head_expander()
    heads = dict(dt_bias=_pad_heads(p["ssd_dt_bias"]), a_log=_pad_heads(p["ssd_a_log"]),
                 d=jnp.repeat(p["ssd_d"], SSD_HEAD_DIM).reshape(SSD_GROUPS, 1, GROUP_W),
                 expand=expand, expand_t=expand.T)
    y, sprev = _ssd_scan(xc, dt, heads["dt_bias"], heads["a_log"], heads["d"], expand, heads["expand_t"],
                         tag + "ssd_scan")
    yb = _ssd_post(y, z, p["ssd_norm"].reshape(1, -1), tag + "ssd_post")

    x1 = _out_proj(x0, ya, yb, big["w_out"], None, tag + "out_proj")
    ffn_matrices = before_ffn(x1) if before_ffn is not None else {}
    big = {**big, **ffn_matrices}
    late_matrices = {**late_matrices, **ffn_matrices}
    x2, gt, up = _ffn(x1, p["norm_ffn"].reshape(1, -1), big["w_gate"], big["w_up"], big["w_down"], None,
                      tag + "ffn")
    saved = dict(x0=x0, h1=h1, u=u, u_perm=u_perm, z=z, xbc=xbc, dt=dt, xr=xr, xi=xi, ys=ys, ya=ya, xc=xc, y=y,
                 sprev=sprev, yb=yb, x1=x1, gt=gt, up=up, s5mats=s5mats, heads=heads, conv_w=conv_w,
                 conv_b=conv_b, ls=ls, b_hgp=b_hgp, late_matrices=late_matrices)
    return x2, saved


def _layer_backward(dx2, dx2b, p, big, s, i, after_ffn_grads=None, after_s5_grads=None):
    tag = "l%d_" % i
    g = {}
    dx1, dx1b, h2, act, dgt, dup, dgain = _ffn_bwd(dx2, s["x1"], s["gt"], s["up"], p["norm_ffn"].reshape(1, -1),
                                                  big["w_gate"], big["w_up"], big["w_down"], None, tag + "ffn_bwd")
    g["norm_ffn"] = dgain[0]
    g["w_down"] = _mm_tn(act, dx2b, tag + "dw_down")
    g["w_gate"] = _mm_tn(dgt, h2, tag + "dw_gate")
    g["w_up"] = _mm_tn(dup, h2, tag + "dw_up")
    g["w_out"] = _mm_tn(jnp.concatenate([s["ya"], s["yb"]], axis=1), dx1b, tag + "dw_out")
    if after_ffn_grads is not None:
        p = {**p, "s5_norm": p["s5_norm"] + after_ffn_grads(g)[0, 0]}

    dys, du_skip, gelu_b, dq_b, dgain, dd, dbg = _s5_post_bwd(
        dx1b, big["w_out"], s["ys"], s["u"], p["s5_d"].reshape(1, -1), big["s5_w_glu"],
        p["s5_b_glu"].reshape(1, -1), p["s5_norm"].reshape(1, -1), None, tag + "s5_post_bwd")
    g["s5_norm"] = dgain[0]
    g["s5_d"] = dd[0]
    g["s5_b_glu"] = dbg[0]
    g["s5_w_glu"] = _mm_tn(gelu_b, dq_b, tag + "dw_glu")
    m = s["s5mats"]
    du_perm, dar, dai, dcre_d, dcim_d, dbre_d, dbim_d = _s5_scan_bwd(
        _to_segments(dys), s["u_perm"], s["xr"], s["xi"], m["bret_bd"], m["bimt_bd"], m["cret_bd"], m["cimt_bd"],
        m["are"], m["aim"], tag + "s5_scan_bwd")
    du_scan = _from_segments(du_perm)
    diag = _block_diag_extract([dcre_d, dcim_d, dbre_d, dbim_d], tag + "s5_blockdiag_bwd")
    g["s5_c_re"], g["s5_c_im"] = diag[0], diag[1]
    dbbre = diag[2].transpose(1, 0, 2)
    dbbim = diag[3].transpose(1, 0, 2)
    dlr, dli, dls, dbre, dbim = _s5_prep_bwd(
        p["s5_lam_re"], p["s5_lam_im"], s["ls"], s["b_hgp"][0], s["b_hgp"][1],
        dar.reshape(S5_GROUPS, S5_STATE), dai.reshape(S5_GROUPS, S5_STATE), dbbre, dbbim, tag + "s5_prep_bwd")
    g["s5_lam_re"] = dlr
    g["s5_lam_im"] = dli
    g["s5_log_step"] = dls[:, 0]
    g["s5_b_re"] = dbre
    g["s5_b_im"] = dbim
    if after_s5_grads is not None:
        p = {**p, "ssd_norm": p["ssd_norm"] + after_s5_grads(g)[0, 0]}

    dy, dz, dgain = _ssd_post_bwd(dx1b, big["w_out"], s["y"], s["z"], p["ssd_norm"].reshape(1, -1), None,
                                  tag + "ssd_post_bwd")
    g["ssd_norm"] = dgain[0]
    hd = s["heads"]
    dxs, dbm, dcm, ddt, dbias, dalog, dd = _ssd_scan_bwd(dy, s["xc"], s["dt"], s["sprev"], hd["dt_bias"],
                                                       hd["a_log"], hd["d"], hd["expand"], hd["expand_t"],
                                                       tag + "ssd_scan_bwd")
    g["ssd_dt_bias"] = _unpad_heads(dbias)
    g["ssd_a_log"] = _unpad_heads(dalog)
    g["ssd_d"] = _unpad_heads(dd)
    dxc = jnp.concatenate([dxs, dbm, dcm], axis=1)
    dxbc, dcw, dcb = _ssd_conv_bwd(dxc, s["xbc"], s["conv_w"], s["conv_b"], tag + "ssd_conv_bwd")
    g["ssd_conv_w"] = dcw[:SSD_CONV]
    g["ssd_conv_b"] = dcb[0]

    dx0, dx0b, dproj, dgain = _inproj_bwd(dx1, s["x0"], du_skip, du_scan, dz, dxbc, ddt, p["norm_mix"].reshape(1, -1),
                                          big["w_in"], None, tag + "inproj_bwd")
    g["norm_mix"] = dgain[0]
    g["w_in"] = _mm_tn(s["h1"], dproj, tag + "dw_in")
    return dx0, dx0b, g


MIXER_BIG = ("w_in", "s5_w_glu", "w_out")
FFN_BIG = ("w_gate", "w_up", "w_down")
BIG = MIXER_BIG + FFN_BIG
COL_SHARDED = ("w_in",)
T_STORED = ("w_gate", "w_up")
LAYER_SMALL = ("norm_mix", "s5_lam_re", "s5_lam_im", "s5_log_step", "s5_b_re", "s5_b_im", "s5_c_re", "s5_c_im",
               "s5_d", "s5_b_glu", "s5_norm", "ssd_conv_w", "ssd_conv_b", "ssd_dt_bias", "ssd_a_log", "ssd_d",
               "ssd_norm", "norm_ffn")
WEIGHTS = ("norm_mix", "w_in", "s5_lam_re", "s5_lam_im", "s5_log_step", "s5_b_re", "s5_b_im", "s5_c_re", "s5_c_im",
           "s5_d", "s5_w_glu", "s5_b_glu", "s5_norm", "ssd_conv_w", "ssd_conv_b", "ssd_dt_bias", "ssd_a_log",
           "ssd_d", "ssd_norm", "w_out", "norm_ffn", "w_gate", "w_up", "w_down", "norm_final")


S5_BC = ("s5_b_re", "s5_b_im", "s5_c_re", "s5_c_im")
TINY = tuple(k for k in LAYER_SMALL if k not in S5_BC)


def _local_step(x, target, big, small, norm_final):
    saved = []
    h = x
    for i in range(DEPTH):
        p = {k: v[i] for k, v in small.items()}
        h, s = _layer_forward(h, p, big, i)
        saved.append((p, s))
    loss, dx, dxb, dgf = _final_loss(h, norm_final.reshape(1, -1), target, "final_loss")
    grads = [None] * DEPTH
    for i in reversed(range(DEPTH)):
        p, s = saved[i]
        dx, dxb, grads[i] = _layer_backward(dx, dxb, p, big, s, i)
    by_name = {k: [grads[i][k] for i in range(DEPTH)] for k in BIG + LAYER_SMALL}
    return loss[0, 0], dx, by_name, dgf[0]


def _my_place():
    return lax.axis_index("x"), lax.axis_index("y"), lax.axis_index("c")


def _all_gather8(blocks, name):
    nt = len(blocks)

    def body(*refs):
        ins = refs[:nt]
        outs = refs[nt:2 * nt]
        send_sems, recv_sems, local_sems = refs[2 * nt:]
        x, y, c = _my_place()
        me, sibling = (x, y, c), (x, y, 1 - c)
        chips = [(1 - x, y), (x, 1 - y), (1 - x, 1 - y)]

        def slot(t, place):
            px, py, pc = place
            return outs[t].at[4 * px + 2 * py + pc]

        def copy(t, k, block, to, src=None):
            return pltpu.make_async_remote_copy(
                src_ref=slot(t, block) if src is None else src, dst_ref=slot(t, block),
                send_sem=send_sems.at[t, k], recv_sem=recv_sems.at[t, k], device_id=to, device_id_type=MESH)

        mine = [pltpu.make_async_copy(ins[t], slot(t, me), local_sems.at[t]) for t in range(nt)]
        for cp in mine:
            cp.start()
        first = []
        for t in range(nt):
            first.append(copy(t, 0, me, sibling, src=ins[t]))
            first += [copy(t, 1 + j, me, (*chip, c), src=ins[t]) for j, chip in enumerate(chips)]
        for cp in first:
            cp.start()
        passed = []
        for j, chip in enumerate(chips):
            for t in range(nt):
                copy(t, 1 + j, (*chip, c), me).wait_recv()
                fwd = copy(t, 4 + j, (*chip, c), sibling)
                fwd.start()
                passed.append(fwd)
        for t in range(nt):
            copy(t, 0, sibling, me).wait_recv()
            for j, chip in enumerate(chips):
                copy(t, 4 + j, (*chip, 1 - c), me).wait_recv()
        for cp in first + passed:
            cp.wait_send()
        for cp in mine:
            cp.wait()

    return pl.pallas_call(
        body, name=name, in_specs=[ANY] * nt, out_specs=[ANY] * nt,
        out_shape=[jax.ShapeDtypeStruct((8,) + b.shape, b.dtype) for b in blocks],
        scratch_shapes=[pltpu.SemaphoreType.DMA((nt, 7)), pltpu.SemaphoreType.DMA((nt, 7)),
                        pltpu.SemaphoreType.DMA((nt,))],
    )(*blocks)


HBM = pl.BlockSpec(memory_space=pltpu.HBM)
SEM = pl.BlockSpec(memory_space=pltpu.SEMAPHORE)
DATAFLOW = pltpu.SideEffectType.DATAFLOW_SIDE_EFFECTING


def _in_hbm(a):
    return pltpu.with_memory_space_constraint(a, pltpu.HBM)


TOKEN = jax.ShapeDtypeStruct((8, LANES), F32)
VMEM_SPEC = pl.BlockSpec(memory_space=pltpu.VMEM)


def _gather_start(blocks, after, name):
    nt = len(blocks)

    def body(*refs):
        ins = refs[:nt]
        lands = refs[nt:2 * nt]
        send_sems, recv_sems = refs[2 * nt + 1:2 * nt + 3]
        refs[-1][...] = jnp.zeros(TOKEN.shape, F32)
        x, y, c = _my_place()
        me = 4 * x + 2 * y + c
        peers = [(x, y, 1 - c), (1 - x, y, c), (x, 1 - y, c), (1 - x, 1 - y, c)]
        for t in range(nt):
            for k, peer in enumerate(peers):
                pltpu.make_async_remote_copy(src_ref=ins[t], dst_ref=lands[t].at[me], send_sem=send_sems.at[4 * t + k],
                                             recv_sem=recv_sems.at[4 * t + k], device_id=peer,
                                             device_id_type=MESH).start()

    lands = [_in_hbm(lax.empty((8,) + b.shape, b.dtype)) for b in blocks]
    out = pl.pallas_call(
        body, name=name, in_specs=[HBM] * (2 * nt) + [ANY],
        out_shape=(pltpu.SemaphoreType.DMA((4 * nt,)), pltpu.SemaphoreType.DMA((4 * nt,)),
                   *[pltpu.HBM(b.shape, b.dtype) for b in blocks],
                   *[pltpu.HBM((8,) + b.shape, b.dtype) for b in blocks], TOKEN),
        out_specs=(SEM, SEM, *[HBM] * (2 * nt), VMEM_SPEC),
        input_output_aliases={i: 2 + i for i in range(2 * nt)},
        compiler_params=pltpu.CompilerParams(has_side_effects=DATAFLOW),
    )(*[_in_hbm(b) for b in blocks], *lands, after)
    return out[:2], list(out[2:2 + nt]), list(out[2 + nt:2 + 2 * nt]), out[-1]


def _gather_forward(sems, blocks, lands, after, name):
    nt = len(blocks)

    def body(*refs):
        ins = refs[:nt]
        lands_in = refs[nt:2 * nt]
        send1, recv1 = refs[2 * nt:2 * nt + 2]
        send2, recv2 = refs[2 * nt + 3:2 * nt + 5]
        x, y, c = _my_place()
        me = 4 * x + 2 * y + c
        sibling = (x, y, 1 - c)
        sources = [4 * x + 2 * y + (1 - c), 4 * (1 - x) + 2 * y + c, 4 * x + 2 * (1 - y) + c,
                   4 * (1 - x) + 2 * (1 - y) + c]
        for t in range(nt):
            for k, src in enumerate(sources):
                cp = pltpu.make_async_remote_copy(src_ref=ins[t], dst_ref=lands_in[t].at[src],
                                                  send_sem=send1.at[4 * t + k], recv_sem=recv1.at[4 * t + k],
                                                  device_id=sibling, device_id_type=MESH)
                cp.wait_send()
                cp.wait_recv()
            for k, src in enumerate(sources[1:]):
                pltpu.make_async_remote_copy(src_ref=lands_in[t].at[src], dst_ref=lands_in[t].at[src],
                                             send_sem=send2.at[3 * t + k], recv_sem=recv2.at[3 * t + k],
                                             device_id=sibling, device_id_type=MESH).start()

    out = pl.pallas_call(
        body, name=name, in_specs=[HBM] * (2 * nt) + [SEM, SEM, pl.BlockSpec(memory_space=pl.ANY)],
        out_shape=(pltpu.SemaphoreType.DMA((3 * nt,)), pltpu.SemaphoreType.DMA((3 * nt,)),
                   *[pltpu.HBM(b.shape, b.dtype) for b in blocks],
                   *[pltpu.HBM(a.shape, a.dtype) for a in lands]),
        out_specs=(SEM, SEM, *[HBM] * (2 * nt)),
        input_output_aliases={i: 2 + i for i in range(2 * nt)},
        compiler_params=pltpu.CompilerParams(has_side_effects=DATAFLOW),
    )(*blocks, *lands, *sems, after)
    return out[:2], list(out[2 + nt:])


def _gather_finish(sems, lands, after, name):
    nt = len(lands)

    def body(*refs):
        lands_in = refs[:nt]
        send2, recv2 = refs[nt:nt + 2]
        x, y, c = _my_place()
        sibling = (x, y, 1 - c)
        mine = [4 * (1 - x) + 2 * y + c, 4 * x + 2 * (1 - y) + c, 4 * (1 - x) + 2 * (1 - y) + c]
        theirs = [4 * (1 - x) + 2 * y + 1 - c, 4 * x + 2 * (1 - y) + 1 - c, 4 * (1 - x) + 2 * (1 - y) + 1 - c]
        for t in range(nt):
            for k in range(3):
                cp = pltpu.make_async_remote_copy(src_ref=lands_in[t].at[mine[k]], dst_ref=lands_in[t].at[theirs[k]],
                                                  send_sem=send2.at[3 * t + k], recv_sem=recv2.at[3 * t + k],
                                                  device_id=sibling, device_id_type=MESH)
                cp.wait_send()
                cp.wait_recv()

    out = pl.pallas_call(
        body, name=name, in_specs=[HBM] * nt + [SEM, SEM, pl.BlockSpec(memory_space=pl.ANY)],
        out_shape=tuple(pltpu.HBM(a.shape, a.dtype) for a in lands), out_specs=tuple([HBM] * nt),
        input_output_aliases={i: i for i in range(nt)},
        compiler_params=pltpu.CompilerParams(has_side_effects=DATAFLOW),
    )(*lands, *sems, after)
    return list(out)


def _other_chips():
    x, y, _ = _my_place()
    return [(1 - x, y), (x, 1 - y), (1 - x, 1 - y)]


def _scatter_start(chunks, name):
    nt = len(chunks)

    def body(*refs):
        ins = refs[:nt]
        lands = refs[nt:2 * nt]
        send_sems, recv_sems = refs[2 * nt:2 * nt + 2]
        refs[-1][...] = jnp.zeros(TOKEN.shape, F32)
        x, y, c = _my_place()
        for t in range(nt):
            for j, (px, py) in enumerate(_other_chips()):
                pltpu.make_async_remote_copy(src_ref=ins[t].at[2 * px + py], dst_ref=lands[t].at[2 * x + y],
                                             send_sem=send_sems.at[3 * t + j], recv_sem=recv_sems.at[3 * t + j],
                                             device_id=(px, py, c), device_id_type=MESH).start()

    lands = [_in_hbm(lax.empty(a.shape, a.dtype)) for a in chunks]
    out = pl.pallas_call(
        body, name=name, in_specs=[HBM] * (2 * nt),
        out_shape=(pltpu.SemaphoreType.DMA((3 * nt,)), pltpu.SemaphoreType.DMA((3 * nt,)),
                   *[pltpu.HBM(a.shape, a.dtype) for a in chunks] * 2, TOKEN),
        out_specs=(SEM, SEM, *[HBM] * (2 * nt), VMEM_SPEC),
        input_output_aliases={i: 2 + i for i in range(2 * nt)},
        compiler_params=pltpu.CompilerParams(has_side_effects=DATAFLOW),
    )(*[_in_hbm(a) for a in chunks], *lands)
    return out[:2], list(out[2:2 + nt]), list(out[2 + nt:2 + 2 * nt]), out[-1]


def _scatter_finish(sems, chunks, lands, after, name):
    nt = len(chunks)

    def body(*refs):
        ins = refs[:nt]
        lands_in = refs[nt:2 * nt]
        send_sems, recv_sems = refs[2 * nt:2 * nt + 2]
        _, _, c = _my_place()
        for t in range(nt):
            for j, (px, py) in enumerate(_other_chips()):
                cp = pltpu.make_async_remote_copy(src_ref=ins[t].at[2 * px + py], dst_ref=lands_in[t].at[2 * px + py],
                                                  send_sem=send_sems.at[3 * t + j], recv_sem=recv_sems.at[3 * t + j],
                                                  device_id=(px, py, c), device_id_type=MESH)
                cp.wait_send()
                cp.wait_recv()

    out = pl.pallas_call(
        body, name=name, in_specs=[HBM] * (2 * nt) + [SEM, SEM, ANY],
        out_shape=tuple(pltpu.HBM(a.shape, a.dtype) for a in lands), out_specs=tuple([HBM] * nt),
        input_output_aliases={nt + i: i for i in range(nt)},
        compiler_params=pltpu.CompilerParams(has_side_effects=DATAFLOW),
    )(*chunks, *lands, *sems, after)
    return list(out)


def _swap_start(views, name):
    nt = len(views)

    def body(*refs):
        ins = refs[:nt]
        lands = refs[nt:2 * nt]
        send_sems, recv_sems = refs[2 * nt:2 * nt + 2]
        refs[-1][...] = jnp.zeros(TOKEN.shape, F32)
        x, y, c = _my_place()
        for t in range(nt):
            pltpu.make_async_remote_copy(
                src_ref=ins[t].at[pl.ds(0, views[t].shape[0]), pl.ds(1 - c, 1)], dst_ref=lands[t],
                send_sem=send_sems.at[t], recv_sem=recv_sems.at[t], device_id=(x, y, 1 - c),
                device_id_type=MESH).start()

    shapes = [(a.shape[0], 1) + a.shape[2:] for a in views]
    lands = [_in_hbm(lax.empty(s, a.dtype)) for s, a in zip(shapes, views)]
    out = pl.pallas_call(
        body, name=name, in_specs=[HBM] * (2 * nt),
        out_shape=(pltpu.SemaphoreType.DMA((nt,)), pltpu.SemaphoreType.DMA((nt,)),
                   *[pltpu.HBM(a.shape, a.dtype) for a in views],
                   *[pltpu.HBM(s, a.dtype) for s, a in zip(shapes, views)], TOKEN),
        out_specs=(SEM, SEM, *[HBM] * (2 * nt), VMEM_SPEC),
        input_output_aliases={i: 2 + i for i in range(2 * nt)},
        compiler_params=pltpu.CompilerParams(has_side_effects=DATAFLOW),
    )(*[_in_hbm(a) for a in views], *lands)
    return out[:2], list(out[2:2 + nt]), list(out[2 + nt:2 + 2 * nt]), out[-1]


def _swap_finish(sems, views, lands, after, name):
    nt = len(views)

    def body(*refs):
        ins = refs[:nt]
        lands_in = refs[nt:2 * nt]
        send_sems, recv_sems = refs[2 * nt:2 * nt + 2]
        x, y, c = _my_place()
        for t in range(nt):
            cp = pltpu.make_async_remote_copy(
                src_ref=ins[t].at[pl.ds(0, views[t].shape[0]), pl.ds(1 - c, 1)], dst_ref=lands_in[t],
                send_sem=send_sems.at[t], recv_sem=recv_sems.at[t], device_id=(x, y, 1 - c), device_id_type=MESH)
            cp.wait_send()
            cp.wait_recv()

    out = pl.pallas_call(
        body, name=name, in_specs=[HBM] * (2 * nt) + [SEM, SEM, ANY],
        out_shape=tuple(pltpu.HBM(a.shape, a.dtype) for a in lands), out_specs=tuple([HBM] * nt),
        input_output_aliases={nt + i: i for i in range(nt)},
        compiler_params=pltpu.CompilerParams(has_side_effects=DATAFLOW),
    )(*views, *lands, *sems, after)
    return list(out)


def _swap_halves(views, name):
    nt = len(views)

    def body(*refs):
        ins = refs[:nt]
        outs = refs[nt:2 * nt]
        send_sems, recv_sems = refs[2 * nt:]
        x, y, c = _my_place()
        copies = [pltpu.make_async_remote_copy(
            src_ref=ins[t].at[pl.ds(0, views[t].shape[0]), pl.ds(1 - c, 1)], dst_ref=outs[t],
            send_sem=send_sems.at[t], recv_sem=recv_sems.at[t], device_id=(x, y, 1 - c), device_id_type=MESH)
            for t in range(nt)]
        for cp in copies:
            cp.start()
        for cp in copies:
            cp.wait()

    return pl.pallas_call(
        body, name=name, in_specs=[ANY] * nt, out_specs=[ANY] * nt,
        out_shape=[jax.ShapeDtypeStruct((a.shape[0], 1) + a.shape[2:], a.dtype) for a in views],
        scratch_shapes=[pltpu.SemaphoreType.DMA((nt,)), pltpu.SemaphoreType.DMA((nt,))],
    )(*views)


def _pair_add_halves(view, recv, name):
    n, _, rows, cols = view.shape
    tile = _row_tile(rows, cols, 4)

    def body(a0_ref, a1_ref, r_ref, o_ref):
        mine = jnp.where(lax.axis_index("c") == 0, a0_ref[...], a1_ref[...])
        o_ref[...] = (mine.astype(F32) + r_ref[...].astype(F32)).astype(o_ref.dtype)

    half = lambda h: pl.BlockSpec((None, None, tile, cols), lambda p, i: (p, h, i, 0))
    return pl.pallas_call(
        body, name=name, grid=(n, rows // tile), in_specs=[half(0), half(1), half(0)],
        out_specs=pl.BlockSpec((None, tile, cols), lambda p, i: (p, i, 0)),
        out_shape=jax.ShapeDtypeStruct((n, rows, cols), view.dtype))(view, view, recv)


def _sum_chunks(lands, chunks, order, name):
    _, rows, cols = chunks.shape
    tile = _row_tile(rows, cols, 5)

    def body(order_ref, l0_ref, l1_ref, l2_ref, own_ref, o_ref):
        o_ref[...] = ((l0_ref[...].astype(F32) + l1_ref[...].astype(F32)) + l2_ref[...].astype(F32)
                      + own_ref[...].astype(F32))

    slot = lambda j: pl.BlockSpec((None, tile, cols), lambda i, order_ref: (order_ref[j], i, 0))
    grid_spec = pltpu.PrefetchScalarGridSpec(
        num_scalar_prefetch=1, grid=(rows // tile,), in_specs=[slot(0), slot(1), slot(2), slot(3)],
        out_specs=pl.BlockSpec((tile, cols), lambda i, order_ref: (i, 0)))
    return pl.pallas_call(body, name=name, grid_spec=grid_spec,
                          out_shape=jax.ShapeDtypeStruct((rows, cols), F32))(order, lands, lands, lands, chunks)


def _adamw_layer(w, g_mine, g_sibling, m, v, layer, prev, name):
    depth, rows, cols = w.shape
    half = rows // 2
    tile = _row_tile(half, cols, 10)
    tiles = half // tile

    def body(w_ref, gm_ref, gs_ref, m_ref, v_ref, *rest):
        d_ref, nm_ref, nv_ref, go_ref = rest[-4:]
        gv = jnp.where(pl.program_id(0) == lax.axis_index("c"), gm_ref[...], gs_ref[...])
        d_ref[...], nm_ref[...], nv_ref[...] = _adamw_math(w_ref[...], gv, m_ref[...], v_ref[...])
        go_ref[...] = gv

    spec = pl.BlockSpec((None, tile, cols), lambda h, i: (layer, h * tiles + i, 0))
    gspec = pl.BlockSpec((tile, cols), lambda h, i: (i, 0))
    shape = jax.ShapeDtypeStruct((depth, rows, cols), F32)
    extra = list(prev)
    aliases = {5 + j: j for j in range(4)} if len(extra) == 4 else {}
    return pl.pallas_call(
        body, name=name, grid=(2, tiles), in_specs=[spec, gspec, gspec, spec, spec] + [ANY] * len(extra),
        out_specs=[spec] * 4, out_shape=[shape] * 4, input_output_aliases=aliases)(w, g_mine, g_sibling, m, v, *extra)


def _sibling_swap_other(pairs, name):
    nt = len(pairs)

    def body(*refs):
        ins = refs[:2 * nt]
        outs = refs[2 * nt:3 * nt]
        send_sems, recv_sems = refs[3 * nt:]
        x, y, c = _my_place()

        def copy(t, src):
            return pltpu.make_async_remote_copy(src_ref=src, dst_ref=outs[t], send_sem=send_sems.at[t],
                                                recv_sem=recv_sems.at[t], device_id=(x, y, 1 - c), device_id_type=MESH)

        for t in range(nt):
            @pl.when(c == 0)
            def _():
                copy(t, ins[2 * t + 1]).start()

            @pl.when(c == 1)
            def _():
                copy(t, ins[2 * t]).start()
        for t in range(nt):
            copy(t, ins[2 * t]).wait()

    flat = [a for pair in pairs for a in pair]
    return pl.pallas_call(
        body, name=name, in_specs=[ANY] * (2 * nt), out_specs=[ANY] * nt,
        out_shape=[jax.ShapeDtypeStruct(a0.shape, a0.dtype) for a0, _ in pairs],
        scratch_shapes=[pltpu.SemaphoreType.DMA((nt,)), pltpu.SemaphoreType.DMA((nt,))],
    )(*flat)


def _sibling_swap(arrs, name):
    nt = len(arrs)

    def body(*refs):
        ins = refs[:nt]
        outs = refs[nt:2 * nt]
        send_sems, recv_sems = refs[2 * nt:]
        x, y, c = _my_place()
        copies = [pltpu.make_async_remote_copy(src_ref=ins[t], dst_ref=outs[t], send_sem=send_sems.at[t],
                                               recv_sem=recv_sems.at[t], device_id=(x, y, 1 - c), device_id_type=MESH)
                  for t in range(nt)]
        for cp in copies:
            cp.start()
        for cp in copies:
            cp.wait()

    return pl.pallas_call(
        body, name=name, in_specs=[ANY] * nt, out_specs=[ANY] * nt,
        out_shape=[jax.ShapeDtypeStruct(a.shape, a.dtype) for a in arrs],
        scratch_shapes=[pltpu.SemaphoreType.DMA((nt,)), pltpu.SemaphoreType.DMA((nt,))],
    )(*arrs)


def _chip_all_to_all(arrs, name):
    nt = len(arrs)

    def body(*refs):
        ins = refs[:nt]
        outs = refs[nt:2 * nt]
        send_sems, recv_sems, local_sems = refs[2 * nt:]
        x, y, c = _my_place()
        mine = 2 * x + y
        chips = [(1 - x, y), (x, 1 - y), (1 - x, 1 - y)]
        local = [pltpu.make_async_copy(ins[t].at[mine], outs[t].at[mine], local_sems.at[t]) for t in range(nt)]
        for cp in local:
            cp.start()
        sends = []
        for t in range(nt):
            for j, (px, py) in enumerate(chips):
                sends.append(pltpu.make_async_remote_copy(
                    src_ref=ins[t].at[2 * px + py], dst_ref=outs[t].at[mine], send_sem=send_sems.at[t, j],
                    recv_sem=recv_sems.at[t, j], device_id=(px, py, c), device_id_type=MESH))
        for cp in sends:
            cp.start()
        for t in range(nt):
            for j, (px, py) in enumerate(chips):
                pltpu.make_async_remote_copy(
                    src_ref=ins[t].at[mine], dst_ref=outs[t].at[2 * px + py], send_sem=send_sems.at[t, j],
                    recv_sem=recv_sems.at[t, j], device_id=(px, py, c), device_id_type=MESH).wait_recv()
        for cp in sends:
            cp.wait_send()
        for cp in local:
            cp.wait()

    return pl.pallas_call(
        body, name=name, in_specs=[ANY] * nt, out_specs=[ANY] * nt,
        out_shape=[jax.ShapeDtypeStruct(a.shape, a.dtype) for a in arrs],
        scratch_shapes=[pltpu.SemaphoreType.DMA((nt, 3)), pltpu.SemaphoreType.DMA((nt, 3)),
                        pltpu.SemaphoreType.DMA((nt,))],
    )(*arrs)


def _as_rows(a):
    return a.reshape(-1, a.shape[-1])


STREAM_VMEM_BYTES = 32 * 1024 * 1024
SUBLANES = 8


def _row_tile(rows, cols, n_arrays):
    lanes = -(-cols // LANES) * LANES
    for t in range(min(rows, 512), SUBLANES - 1, -1):
        if rows % t == 0 and t % SUBLANES == 0 and 2 * n_arrays * t * lanes * 4 <= STREAM_VMEM_BYTES:
            return t
    return rows


def _pair_add(a0, a1, recv, name):
    rows, cols = a0.shape
    tile = _row_tile(rows, cols, 4)

    def body(a0_ref, a1_ref, r_ref, o_ref):
        mine = jnp.where(lax.axis_index("c") == 0, a0_ref[...], a1_ref[...])
        o_ref[...] = (mine.astype(F32) + r_ref[...].astype(F32)).astype(o_ref.dtype)

    spec = pl.BlockSpec((tile, cols), lambda i: (i, 0))
    return pl.pallas_call(body, name=name, grid=(rows // tile,), in_specs=[spec] * 3, out_specs=spec,
                          out_shape=jax.ShapeDtypeStruct((rows, cols), a0.dtype))(a0, a1, recv)


def _sum_leading(a, name):
    n, rows, cols = a.shape
    tile = _row_tile(rows, cols, n + 1)

    def body(a_ref, o_ref):
        acc = a_ref[0].astype(F32)
        for k in range(1, n):
            acc = acc + a_ref[k].astype(F32)
        o_ref[...] = acc

    return pl.pallas_call(
        body, name=name, grid=(rows // tile,), in_specs=[pl.BlockSpec((n, tile, cols), lambda i: (0, i, 0))],
        out_specs=pl.BlockSpec((tile, cols), lambda i: (i, 0)),
        out_shape=jax.ShapeDtypeStruct((rows, cols), F32))(a)


def _adamw_math(w, g, m, v):
    mn = ADAM_B1 * m + (1.0 - ADAM_B1) * g
    vn = ADAM_B2 * v + (1.0 - ADAM_B2) * jnp.square(g)
    m_hat = mn / (1.0 - ADAM_B1 ** ADAM_STEP)
    v_hat = vn / (1.0 - ADAM_B2 ** ADAM_STEP)
    delta = -ADAM_LR * (m_hat / (jnp.sqrt(v_hat) + ADAM_EPS) + ADAM_WD * w)
    return delta, mn, vn


def _adamw_layers(w, g_mine, g_sibling, m, v, name):
    depth, rows, cols = w.shape
    tile = _row_tile(rows, cols, 10)

    def body(w_ref, gm_ref, gs_ref, m_ref, v_ref, d_ref, nm_ref, nv_ref, go_ref):
        gv = jnp.where(pl.program_id(0) == lax.axis_index("c"), gm_ref[...], gs_ref[...])
        d_ref[...], nm_ref[...], nv_ref[...] = _adamw_math(w_ref[...], gv, m_ref[...], v_ref[...])
        go_ref[...] = gv

    spec = pl.BlockSpec((None, tile, cols), lambda l, i: (l, i, 0))
    gspec = pl.BlockSpec((tile, cols), lambda l, i: (i, 0))
    shape = jax.ShapeDtypeStruct((depth, rows, cols), F32)
    return pl.pallas_call(body, name=name, grid=(depth, rows // tile), in_specs=[spec, gspec, gspec, spec, spec],
                          out_specs=[spec] * 4, out_shape=[shape] * 4)(w, g_mine, g_sibling, m, v)


def _adamw_rows(w, g, m, v, name):
    depth, rows, cols = w.shape
    tile = _row_tile(rows, cols, 7)

    def body(w_ref, g_ref, m_ref, v_ref, d_ref, nm_ref, nv_ref):
        d_ref[...], nm_ref[...], nv_ref[...] = _adamw_math(w_ref[...], g_ref[...], m_ref[...], v_ref[...])

    spec = pl.BlockSpec((None, tile, cols), lambda l, i: (l, i, 0))
    shape = jax.ShapeDtypeStruct((depth, rows, cols), F32)
    return pl.pallas_call(body, name=name, grid=(depth, rows // tile), in_specs=[spec] * 4, out_specs=[spec] * 3,
                          out_shape=[shape] * 3)(w, g, m, v)


def _adamw_many(ws, gs, ms, vs, name):
    nt = len(ws)

    def body(*refs):
        for t in range(nt):
            w_ref, g_ref, m_ref, v_ref = (refs[k * nt + t] for k in range(4))
            d_ref, nm_ref, nv_ref = (refs[(4 + k) * nt + t] for k in range(3))
            d_ref[...], nm_ref[...], nv_ref[...] = _adamw_math(w_ref[...], g_ref[...], m_ref[...], v_ref[...])

    shapes = [jax.ShapeDtypeStruct(a.shape, F32) for a in ws]
    out = pl.pallas_call(body, name=name, out_shape=shapes * 3)(*ws, *gs, *ms, *vs)
    return out[:nt], out[nt:2 * nt], out[2 * nt:]


TINY_ROWS_MULTIPLE = 128


def _flat_pack(arrs):
    flat = jnp.concatenate([a.reshape(-1) for a in arrs])
    pad = (-flat.shape[0]) % (TINY_ROWS_MULTIPLE * LANES)
    return jnp.pad(flat, (0, pad)).reshape(-1, LANES)


def _flat_unpack(buf, shapes):
    flat = buf.reshape(-1)
    out = []
    off = 0
    for shp in shapes:
        n = math.prod(shp)
        out.append(flat[off:off + n].reshape(shp))
        off += n
    return out


def _to_chunks(a, name):
    if name == "w_in":
        a = _unpad_in_proj(a)
    rows, cols = a.shape
    if name in COL_SHARDED:
        return a.reshape(rows, 4, cols // 4).transpose(1, 0, 2)
    return a.reshape(4, rows // 4, cols)


def _from_chunks(a, name):
    _, depth, r, cc = a.shape
    if name in COL_SHARDED:
        return a.transpose(1, 2, 0, 3).reshape(depth, r, 4 * cc)
    return a.transpose(1, 0, 2, 3).reshape(depth, 4 * r, cc)


def kernel(x, norm_mix, w_in, s5_lam_re, s5_lam_im, s5_log_step, s5_b_re, s5_b_im, s5_c_re, s5_c_im, s5_d, s5_w_glu, s5_b_glu, s5_norm, ssd_conv_w, ssd_conv_b, ssd_dt_bias, ssd_a_log, ssd_d, ssd_norm, w_out, norm_ffn, w_gate, w_up, w_down, norm_final, loss_target, m_norm_mix, m_w_in, m_s5_lam_re, m_s5_lam_im, m_s5_log_step, m_s5_b_re, m_s5_b_im, m_s5_c_re, m_s5_c_im, m_s5_d, m_s5_w_glu, m_s5_b_glu, m_s5_norm, m_ssd_conv_w, m_ssd_conv_b, m_ssd_dt_bias, m_ssd_a_log, m_ssd_d, m_ssd_norm, m_w_out, m_norm_ffn, m_w_gate, m_w_up, m_w_down, m_norm_final, v_norm_mix, v_w_in, v_s5_lam_re, v_s5_lam_im, v_s5_log_step, v_s5_b_re, v_s5_b_im, v_s5_c_re, v_s5_c_im, v_s5_d, v_s5_w_glu, v_s5_b_glu, v_s5_norm, v_ssd_conv_w, v_ssd_conv_b, v_ssd_dt_bias, v_ssd_a_log, v_ssd_d, v_ssd_norm, v_w_out, v_norm_ffn, v_w_gate, v_w_up, v_w_down, v_norm_final):
    args = dict(locals())
    w = {k: args[k] for k in WEIGHTS}
    m = {k: args["m_" + k] for k in WEIGHTS}
    v = {k: args["v_" + k] for k in WEIGHTS}
    cx, cy, cc = _my_place()
    chip = 2 * cx + cy

    me = 4 * cx + 2 * cy + cc
    others = _other_chips()
    chunk_order = jnp.stack([2 * px + py for px, py in others] + [chip]).astype(jnp.int32)
    stored = lambda k, a: jnp.swapaxes(a, 1, 2) if k in T_STORED else a

    def my_half(k, layer):
        a = stored(k, w[k])[layer]
        return lax.dynamic_slice_in_dim(a, cc * (a.shape[0] // 2), a.shape[0] // 2, 0).astype(BF16)

    def assemble(names, lands, blocks):
        full = {}
        for k, a, b in zip(names, lands, blocks):
            a = lax.dynamic_update_index_in_dim(a, b, me, 0)
            a = a.reshape(4, 2 * a.shape[1], a.shape[2])
            if k in COL_SHARDED:
                full[k] = _pad_in_proj(a.transpose(1, 0, 2).reshape(a.shape[1], 4 * a.shape[2]))
            else:
                full[k] = a.reshape(4 * a.shape[1], a.shape[2])
        return full

    conv_block = w["ssd_conv_w"].reshape(DEPTH * SSD_CONV, -1)
    first = [my_half("w_in", 0), conv_block]
    second_names = ("s5_w_glu", "w_out")
    second = [my_half(k, 0) for k in second_names]
    ffn0 = [my_half(k, 0) for k in FFN_BIG]
    blocks1 = [my_half(k, 1) for k in BIG]
    sems_a, kept_a, lands_a, token = _gather_start(first, x, "gather0a_start")
    sems_c, kept_c, lands_c, token = _gather_start(second, token, "gather0c_start")
    sems_b, kept_b, lands_b, token = _gather_start(ffn0, token, "gather0b_start")
    sems1, kept1, lands1, token = _gather_start(blocks1, token, "gather1_start")
    sems_a, lands_a = _gather_forward(sems_a, kept_a, lands_a, token, "gather0a_forward")
    lands_a = _gather_finish(sems_a, lands_a, token, "gather0a_finish")
    big0 = assemble(("w_in",), lands_a, first)
    conv_rows = lax.dynamic_update_index_in_dim(lands_a[-1], conv_block, me, 0)
    conv_full = conv_rows.reshape(4, 2, DEPTH, SSD_CONV, -1)[:, 0].transpose(1, 2, 0, 3).reshape(
        DEPTH, SSD_CONV, SSD_CONV_DIM)
    small = {k: w[k] for k in LAYER_SMALL}
    small["ssd_conv_w"] = conv_full
    p0 = {k: a[0] for k, a in small.items()}
    p1 = {k: a[1] for k, a in small.items()}

    p0["norm_mix"] = p0["norm_mix"] + token[0, 0]
    pending = {}

    def pass_on_ffn0(u):
        pending["second"] = _gather_forward(sems_c, kept_c, lands_c, u, "gather0c_forward")
        pending["ffn0"] = _gather_forward(sems_b, kept_b, lands_b, pending["second"][1][0], "gather0b_forward")

    def second_matrices(ys):
        sems, lands = pending["second"]
        return assemble(second_names, _gather_finish(sems, lands, ys, "gather0c_finish"), second)

    def ffn0_matrices(x1):
        sems, lands = pending["ffn0"]
        lands = _gather_finish(sems, lands, x1, "gather0b_finish")
        pending["layer1"] = _gather_forward(sems1, kept1, lands1, lands[0], "gather1_forward")
        return assemble(FFN_BIG, lands, ffn0)

    h1, saved0 = _layer_forward(x[0], p0, big0, 0, pass_on_ffn0, second_matrices, ffn0_matrices)
    big0 = {**big0, **saved0["late_matrices"]}
    sems1, lands1 = pending["layer1"]
    lands1 = _gather_finish(sems1, lands1, h1, "gather1_finish")
    big1 = assemble(BIG, lands1, blocks1)
    h2, saved1 = _layer_forward(h1, p1, big1, 1)
    loss_row, dx, dxb, g_final = _final_loss(h2, w["norm_final"].reshape(1, -1), loss_target[0], "final_loss")
    loss_part, g_final = loss_row[0, 0], g_final[0]

    def halves_view(k, a):
        if k in COL_SHARDED:
            return a.reshape(1, 2, a.shape[0] // 2, a.shape[1])
        return a.reshape(4, 2, a.shape[0] // 8, a.shape[1])

    def to_chunks(k, part):
        if k in COL_SHARDED:
            a = _unpad_in_proj(part[0])
            return a.reshape(a.shape[0], 4, a.shape[1] // 4).transpose(1, 0, 2)
        return part.reshape(4, -1, part.shape[-1])

    def reduce_begin(names, views, tag):
        recv = _swap_halves(views, tag + "swap")
        parts = [_pair_add_halves(a, r, tag + "pair_" + k) for k, a, r in zip(names, views, recv)]
        chunks = [to_chunks(k, p) for k, p in zip(names, parts)]
        return _scatter_start(chunks, tag + "scatter_start")

    def reduce_end(names, handle, after, tag):
        sems, kept, lands, _ = handle
        lands = _scatter_finish(sems, kept, lands, after, tag + "scatter_finish")
        return [_sum_chunks(a, b, chunk_order, tag + "sum_" + k) for k, a, b in zip(names, lands, kept)]

    def swap_begin(names, views, tag):
        return (names, *_swap_start(views, tag + "swap_start"))

    def scatter_begin(handle, after, tag):
        names, sems, views, lands, _ = handle
        recv = _swap_finish(sems, views, lands, after, tag + "swap_finish")
        parts = [_pair_add_halves(a, r, tag + "pair_" + k) for k, a, r in zip(names, views, recv)]
        return _scatter_start([to_chunks(k, p) for k, p in zip(names, parts)], tag + "scatter_start")

    dx, dxb, g1 = _layer_backward(dx, dxb, p1, big1, saved1, 1)
    swap1 = swap_begin(BIG, [halves_view(k, g1[k]) for k in BIG], "grad1_")
    p0["norm_ffn"] = p0["norm_ffn"] + swap1[4][0, 0]

    early = FFN_BIG + ("w_out",)
    middle = ("s5_w_glu", "s5_bc")
    bc_rows = 2 * DEPTH * S5_GROUP * S5_GROUPS

    def send_early(g_so_far):
        pending["round1"] = scatter_begin(swap1, g_so_far["w_out"], "grad1_")
        pending["swap_early"] = swap_begin(early, [halves_view(k, g_so_far[k]) for k in early], "grad0a_")
        return pending["swap_early"][4]

    def send_middle(g_so_far):
        pending["early"] = scatter_begin(pending["swap_early"], g_so_far["s5_w_glu"], "grad0a_")
        rows = lambda names: jnp.stack([a for layer in (g_so_far, g1) for a in (layer[names[0]], layer[names[1]])]
                                       ).reshape(bc_rows, S5_STATE)
        bc = jnp.stack([rows(("s5_b_re", "s5_b_im")), rows(("s5_c_re", "s5_c_im"))])[None]
        pending["swap_middle"] = swap_begin(middle, [halves_view("s5_w_glu", g_so_far["s5_w_glu"]), bc], "grad0b_")
        return pending["swap_middle"][4]

    grad_x, _, g0 = _layer_backward(dx, dxb, p0, big0, saved0, 0, send_early, send_middle)
    g = {k: [g0[k], g1[k]] for k in LAYER_SMALL}
    pending["middle"] = scatter_begin(pending["swap_middle"], grad_x, "grad0b_")
    reduced1 = dict(zip(BIG, reduce_end(BIG, pending["round1"], grad_x, "grad1_")))
    shared1 = dict(zip(BIG, _sibling_swap([reduced1[k] for k in BIG], "grad1_share")))
    round0 = reduce_begin(("w_in",), [halves_view("w_in", g0["w_in"])], "grad0c_")

    delta, new_m, new_v, grads = {}, {}, {}, {}
    adam1 = {}
    layered = tuple(k for k in BIG if k not in COL_SHARDED)
    for k in layered:
        adam1[k] = _adamw_layer(stored(k, w[k]), reduced1[k], shared1[k], stored(k, m[k]), stored(k, v[k]), 1,
                                [round0[3]], "adamw1_" + k)
    follow = adam1[layered[-1]][0]
    reduced0 = dict(zip(early, reduce_end(early, pending["early"], follow, "grad0a_")))
    reduced0.update(zip(middle, reduce_end(middle, pending["middle"], follow, "grad0b_")))
    tiny_names = TINY + ("norm_final",)
    parts = [jnp.stack(g[k]) for k in TINY] + [g_final, loss_part.reshape(1)]
    shapes = [p.shape for p in parts]
    small_blocks = [_flat_pack(parts), reduced0["s5_bc"]]
    small_sems, small_kept, small_lands, small_token = _gather_start(small_blocks, follow, "gather_small_start")
    reduced0.update(zip(("w_in",), reduce_end(("w_in",), round0, small_token, "grad0c_")))
    shared0 = dict(zip(BIG, _sibling_swap([reduced0[k] for k in BIG], "grad0_share")))
    for k in layered:
        outs = _adamw_layer(stored(k, w[k]), reduced0[k], shared0[k], stored(k, m[k]), stored(k, v[k]), 0, adam1[k],
                            "adamw0_" + k)
        delta[k], new_m[k], new_v[k], grads[k] = (stored(k, a) for a in outs)
    both = lambda mine, sib: jnp.where(cc == 0, jnp.concatenate([mine, sib]), jnp.concatenate([sib, mine]))
    grads["w_in"] = jnp.stack([both(reduced0["w_in"], shared0["w_in"]), both(reduced1["w_in"], shared1["w_in"])])
    outs = _adamw_rows(*[_lane_dense(a)[None] for a in (w["w_in"], grads["w_in"], m["w_in"], v["w_in"])],
                       "adamw_w_in")
    delta["w_in"], new_m["w_in"], new_v["w_in"] = (_from_lane_dense(a[0]) for a in outs)

    last = delta["w_in"]
    small_sems, small_lands = _gather_forward(small_sems, small_kept, small_lands, last, "gather_small_forward")
    small_lands = _gather_finish(small_sems, small_lands, last, "gather_small_finish")
    allparts, bc_eighths = (lax.dynamic_update_index_in_dim(a, b, me, 0) for a, b in zip(small_lands, small_blocks))
    unpacked = _flat_unpack(_sum_leading(allparts, "sum_small"), shapes)
    loss = unpacked[-1][0]
    grads.update(zip(tiny_names, unpacked[:-1]))
    width = SSD_CONV_DIM // 4
    grads["ssd_conv_w"] = lax.dynamic_slice_in_dim(grads["ssd_conv_w"], chip * width, width, axis=2)
    bc = bc_eighths.reshape(4, 2, bc_rows // 4, S5_STATE)
    b_sum = bc[:, 0].reshape(DEPTH, 2, S5_GROUP, S5_GROUPS, S5_STATE)
    c_sum = bc[:, 1].reshape(DEPTH, 2, S5_GROUPS, S5_GROUP, S5_STATE)
    grads["s5_c_re"] = c_sum[:, 0]
    grads["s5_c_im"] = c_sum[:, 1]

    b_names = ("s5_b_re", "s5_b_im")
    hp = lambda a: a.transpose(0, 1, 3, 2)
    names = tiny_names + ("s5_c_re", "s5_c_im") + b_names
    view = lambda k, a: hp(a) if k in b_names else (a.reshape(1, -1) if a.ndim == 1 else a)
    g_view = {k: view(k, grads[k]) for k in names if k not in b_names}
    g_view.update({k: b_sum[:, j].transpose(0, 2, 1, 3) for j, k in enumerate(b_names)})
    ds, nms, nvs = _adamw_many([view(k, w[k]) for k in names], [g_view[k] for k in names],
                               [view(k, m[k]) for k in names], [view(k, v[k]) for k in names], "adamw_small")
    for k, a, b, c in zip(names, ds, nms, nvs):
        if k in b_names:
            delta[k], new_m[k], new_v[k], grads[k] = hp(a), hp(b), hp(c), hp(g_view[k])
        else:
            delta[k], new_m[k], new_v[k] = (t.reshape(w[k].shape) for t in (a, b, c))

    return (loss, grad_x[None], *[grads[k] for k in WEIGHTS], *[delta[k] for k in WEIGHTS],
            *[new_m[k] for k in WEIGHTS], *[new_v[k] for k in WEIGHTS])
```

```python
import math

import jax
import jax.numpy as jnp
from jax import lax
from jax.experimental import pallas as pl
from jax.experimental.pallas import tpu as pltpu

F32 = jnp.float32
BF16 = jnp.bfloat16
MESH = pl.DeviceIdType.MESH
ANY = pl.BlockSpec(memory_space=pl.ANY)

D_MODEL = 1024
DEPTH = 2
S5_GROUPS = 64
S5_GROUP = 16
S5_STATE = 64
S5_COLS = S5_GROUPS * S5_STATE
S5_TILE_GROUPS = 8
S5_TILES = S5_GROUPS // S5_TILE_GROUPS
S5_TILE_IN = S5_TILE_GROUPS * S5_GROUP
S5_TILE_ST = S5_TILE_GROUPS * S5_STATE
SEGS = 8
SSD_HEADS = 16
SSD_HEAD_DIM = 64
SSD_GROUPS = 2
SSD_GROUP_HEADS = SSD_HEADS // SSD_GROUPS
SSD_STATE = 128
SSD_CONV = 4
SSD_CHUNK = 128
SSD_WIDTH = 1024
SSD_CONV_DIM = SSD_WIDTH + 2 * SSD_GROUPS * SSD_STATE
IN_MAIN = 3584
IN_PAD = IN_MAIN + 2 * 128
FFN = 2816
EPS = 1e-6
LANES = 128
ROW_TILE = 256
WIDE_ROW_TILE = 512

ADAM_LR = 0.001
ADAM_B1 = 0.9
ADAM_B2 = 0.999
ADAM_EPS = 1e-08
ADAM_WD = 0.01
ADAM_STEP = 10


def _sigmoid(x):
    return 1.0 / (1.0 + jnp.exp(-x))


def _silu(x):
    return x * _sigmoid(x)


def _dsilu(x):
    s = _sigmoid(x)
    return s * (1.0 + x * (1.0 - s))


_GELU_K = math.sqrt(2.0 / math.pi)
_GELU_C = 0.044715


def _gelu(x):
    t = jnp.tanh(_GELU_K * (x + _GELU_C * x * x * x))
    return 0.5 * x * (1.0 + t)


def _dgelu(x):
    t = jnp.tanh(_GELU_K * (x + _GELU_C * x * x * x))
    return 0.5 * (1.0 + t) + 0.5 * x * (1.0 - t * t) * _GELU_K * (1.0 + 3.0 * _GELU_C * x * x)


def _softplus(x):
    e = jnp.exp(-jnp.abs(x))
    u = 1.0 + e
    log1p = jnp.where(u == 1.0, e, jnp.log(u) * e / jnp.where(u == 1.0, 1.0, u - 1.0))
    return jnp.maximum(x, 0.0) + log1p


def _rstd(x):
    return lax.rsqrt(jnp.mean(x * x, axis=-1, keepdims=True) + EPS)


def _rms_bwd(x, r, gain, dy):
    dyg = dy * gain
    dx = r * dyg - x * (r * r * r) * jnp.mean(x * dyg, axis=-1, keepdims=True)
    dgain = jnp.sum(dy * x * r, axis=0, keepdims=True)
    return dx, dgain


def _dot(a, b):
    return jnp.dot(a, b, preferred_element_type=F32)


def _dot_nt(a, b):
    return lax.dot_general(a, b, (((1,), (1,)), ((), ())), preferred_element_type=F32)


def _dot_tn(a, b):
    return lax.dot_general(a, b, (((0,), (0,)), ((), ())), preferred_element_type=F32)


def _row_spec(tile, cols):
    return pl.BlockSpec((tile, cols), lambda i: (i, 0))


def _full_spec(shape):
    nd = len(shape)
    return pl.BlockSpec(shape, lambda *_: (0,) * nd)


def _layer_spec(shape, layer, block=0):
    if layer is None:
        return pl.BlockSpec(tuple(shape), lambda *_: (block, 0), pipeline_mode=pl.Buffered(1))
    return pl.BlockSpec((None,) + tuple(shape), lambda *_: (layer, block, 0), pipeline_mode=pl.Buffered(1))


def _acc_rows(ref, val, first):
    @pl.when(first)
    def _():
        ref[...] = val

    @pl.when(jnp.logical_not(first))
    def _():
        ref[...] += val


def _pick_tile(n, cap):
    best = LANES
    for t in range(LANES, cap + 1, LANES):
        if n % t == 0:
            best = t
    return best


def _mm_tn(a, b, name):
    k, m = a.shape
    _, n = b.shape
    tm = _pick_tile(m, 1536)
    tn = _pick_tile(n, 1536)

    def body(a_ref, b_ref, o_ref):
        o_ref[...] = _dot_tn(a_ref[...], b_ref[...]).astype(BF16)

    return pl.pallas_call(
        body, name=name, grid=(n // tn, m // tm),
        in_specs=[pl.BlockSpec((k, tm), lambda j, i: (0, i)), pl.BlockSpec((k, tn), lambda j, i: (0, j))],
        out_specs=pl.BlockSpec((tm, tn), lambda j, i: (i, j)),
        out_shape=jax.ShapeDtypeStruct((m, n), BF16),
    )(a, b)


def _rms_inproj(x, gain, w_pad, layer, name):
    L = x.shape[0]

    def body(x_ref, g_ref, w_ref, u_ref, z_ref, xbc_ref, dt_ref, h_ref):
        xv = x_ref[...]
        h = (xv * _rstd(xv) * g_ref[...]).astype(BF16)
        h_ref[...] = h
        p = _dot(h, w_ref[...])
        u_ref[...] = p[:, :1024]
        z_ref[...] = p[:, 1024:2048]
        xbc_ref[...] = p[:, 2048:IN_MAIN]
        dt_ref[...] = p[:, IN_MAIN:IN_PAD]

    tile = min(L, WIDE_ROW_TILE)
    return pl.pallas_call(
        body, name=name, grid=(L // tile,),
        in_specs=[_row_spec(tile, D_MODEL), _full_spec((1, D_MODEL)), _layer_spec((D_MODEL, IN_PAD), layer)],
        out_specs=[_row_spec(tile, 1024), _row_spec(tile, 1024), _row_spec(tile, SSD_CONV_DIM),
                   _row_spec(tile, 256), _row_spec(tile, D_MODEL)],
        out_shape=[jax.ShapeDtypeStruct((L, 1024), F32), jax.ShapeDtypeStruct((L, 1024), F32),
                   jax.ShapeDtypeStruct((L, SSD_CONV_DIM), F32), jax.ShapeDtypeStruct((L, 256), F32),
                   jax.ShapeDtypeStruct((L, D_MODEL), BF16)],
    )(x, gain, w_pad)


def _s5_prep_math(lr, li, ls, bre, bim):
    step = jnp.exp(ls)
    mag = jnp.exp(lr * step)
    ang = li * step
    are = mag * jnp.cos(ang)
    aim = mag * jnp.sin(ang)
    den = lr * lr + li * li
    nr = are - 1.0
    ni = aim
    cre = (nr * lr + ni * li) / den
    cim = (ni * lr - nr * li) / den
    bbre = cre[None] * bre - cim[None] * bim
    bbim = cre[None] * bim + cim[None] * bre
    return are, aim, bbre, bbim


def _s5_prep(lr, li, ls, bre, bim, name):
    def body(lr_ref, li_ref, ls_ref, bre_ref, bim_ref, are_ref, aim_ref, bbre_ref, bbim_ref):
        are, aim, bbre, bbim = _s5_prep_math(lr_ref[...], li_ref[...], ls_ref[...], bre_ref[...], bim_ref[...])
        are_ref[...] = are
        aim_ref[...] = aim
        bbre_ref[...] = bbre
        bbim_ref[...] = bbim

    gp = jax.ShapeDtypeStruct((S5_GROUPS, S5_STATE), F32)
    hgp = jax.ShapeDtypeStruct((S5_GROUP, S5_GROUPS, S5_STATE), F32)
    return pl.pallas_call(body, name=name, out_shape=[gp, gp, hgp, hgp])(lr, li, ls, bre, bim)


def _s5_prep_bwd(lr, li, ls, bre, bim, dare, daim, dbbre, dbbim, name):
    def body(lr_ref, li_ref, ls_ref, bre_ref, bim_ref, dare_ref, daim_ref, dbbre_ref, dbbim_ref,
             dlr_ref, dli_ref, dls_ref, dbre_ref, dbim_ref):
        _, vjp = jax.vjp(_s5_prep_math, lr_ref[...], li_ref[...], ls_ref[...], bre_ref[...], bim_ref[...])
        dlr, dli, dls, dbre, dbim = vjp((dare_ref[...], daim_ref[...], dbbre_ref[...], dbbim_ref[...]))
        dlr_ref[...] = dlr
        dli_ref[...] = dli
        dls_ref[...] = dls
        dbre_ref[...] = dbre
        dbim_ref[...] = dbim

    gp = jax.ShapeDtypeStruct((S5_GROUPS, S5_STATE), F32)
    g1 = jax.ShapeDtypeStruct((S5_GROUPS, 1), F32)
    hgp = jax.ShapeDtypeStruct((S5_GROUP, S5_GROUPS, S5_STATE), F32)
    return pl.pallas_call(body, name=name, out_shape=[gp, gp, g1, hgp, hgp])(
        lr, li, ls, bre, bim, dare, daim, dbbre, dbbim)


def _cmul_add(ar, ai, sr, si, br, bi):
    return ar * sr - ai * si + br, ar * si + ai * sr + bi


def _shift_rows_down(v):
    rolled = pltpu.roll(v, 1, 0)
    row = lax.broadcasted_iota(jnp.int32, v.shape, 0)
    return jnp.where(row == 0, 0.0, rolled)


def _shift_rows_up(v):
    rolled = pltpu.roll(v, SEGS - 1, 0)
    row = lax.broadcasted_iota(jnp.int32, v.shape, 0)
    return jnp.where(row == SEGS - 1, 0.0, rolled)


def _segment_power(ar, ai, steps):
    n = 1
    while n < steps:
        ar, ai = ar * ar - ai * ai, 2.0 * ar * ai
        n *= 2
    assert n == steps
    return ar, ai


def _half_segment_entries(ar, ai, first, second, half_steps, shift):
    pr, pi = _segment_power(ar, ai, half_steps)
    er = jnp.zeros_like(first[0])
    ei = jnp.zeros_like(first[1])
    for _ in range(SEGS - 1):
        mr, mi = _cmul_add(pr, pi, er, ei, *first)
        nr, ni = _cmul_add(pr, pi, mr, mi, *second)
        er, ei = shift(nr), shift(ni)
    mr, mi = _cmul_add(pr, pi, er, ei, *first)
    return (er, ei), (mr, mi)


def _s5_scan(u_perm, bre_bd, bim_bd, cre_bd, cim_bd, are, aim, name):
    L = u_perm.shape[0]
    half = L // SEGS // 2

    def body(u_ref, bre_ref, bim_ref, cre_ref, cim_ref, are_ref, aim_ref, y_ref, xr_ref, xi_ref):
        u = u_ref[...].astype(BF16)
        xr_ref[...] = _dot(u, bre_ref[0])
        xi_ref[...] = _dot(u, bim_ref[0])
        ar = jnp.broadcast_to(are_ref[0], (SEGS, S5_TILE_ST))
        ai = jnp.broadcast_to(aim_ref[0], (SEGS, S5_TILE_ST))
        zero = jnp.zeros((SEGS, S5_TILE_ST), F32)
        block = lambda j: pl.ds(pl.multiple_of(j * SEGS, SEGS), SEGS)

        def finals(j, c):
            lo, hi = block(j), block(j + half)
            return (*_cmul_add(ar, ai, c[0], c[1], xr_ref[lo, :], xi_ref[lo, :]),
                    *_cmul_add(ar, ai, c[2], c[3], xr_ref[hi, :], xi_ref[hi, :]))

        f = lax.fori_loop(0, half, finals, (zero,) * 4, unroll=4)
        e_lo, e_hi = _half_segment_entries(ar, ai, f[:2], f[2:], half, _shift_rows_down)

        def scan(j, c):
            lo, hi = block(j), block(j + half)
            s_lo = _cmul_add(ar, ai, c[0], c[1], xr_ref[lo, :], xi_ref[lo, :])
            s_hi = _cmul_add(ar, ai, c[2], c[3], xr_ref[hi, :], xi_ref[hi, :])
            xr_ref[lo, :], xi_ref[lo, :] = s_lo
            xr_ref[hi, :], xi_ref[hi, :] = s_hi
            return (*s_lo, *s_hi)

        lax.fori_loop(0, half, scan, (*e_lo, *e_hi), unroll=8)
        y_ref[...] = (_dot(xr_ref[...].astype(BF16), cre_ref[0]) - _dot(xi_ref[...].astype(BF16), cim_ref[0]))

    tile3 = lambda a, b: pl.BlockSpec((1, a, b), lambda k: (k, 0, 0))
    return pl.pallas_call(
        body, name=name, grid=(S5_TILES,),
        in_specs=[pl.BlockSpec((L, S5_TILE_IN), lambda k: (0, k)),
                  tile3(S5_TILE_IN, S5_TILE_ST), tile3(S5_TILE_IN, S5_TILE_ST),
                  tile3(S5_TILE_ST, S5_TILE_IN), tile3(S5_TILE_ST, S5_TILE_IN),
                  tile3(1, S5_TILE_ST), tile3(1, S5_TILE_ST)],
        out_specs=[pl.BlockSpec((L, S5_TILE_IN), lambda k: (0, k)),
                   pl.BlockSpec((L, S5_TILE_ST), lambda k: (0, k)), pl.BlockSpec((L, S5_TILE_ST), lambda k: (0, k))],
        out_shape=[jax.ShapeDtypeStruct((L, 1024), F32), jax.ShapeDtypeStruct((L, S5_COLS), F32),
                   jax.ShapeDtypeStruct((L, S5_COLS), F32)],
    )(u_perm, bre_bd, bim_bd, cre_bd, cim_bd, are, aim)


def _s5_scan_bwd(dy_perm, u_perm, xr, xi, bret_bd, bimt_bd, cret_bd, cimt_bd, are, aim, name):
    L = u_perm.shape[0]
    steps = L // SEGS
    half = steps // 2

    def body(dy_ref, u_ref, xr_ref, xi_ref, bret_ref, bimt_ref, cret_ref, cimt_ref, are_ref, aim_ref,
             du_ref, dar_ref, dai_ref, dcre_ref, dcim_ref, dbre_ref, dbim_ref, gr_ref, gi_ref):
        dy = dy_ref[...].astype(BF16)
        u = u_ref[...].astype(BF16)
        gr_ref[...] = _dot(dy, cret_ref[0])
        gi_ref[...] = -_dot(dy, cimt_ref[0])
        ar = jnp.broadcast_to(are_ref[0], (SEGS, S5_TILE_ST))
        ai = -jnp.broadcast_to(aim_ref[0], (SEGS, S5_TILE_ST))
        zero = jnp.zeros((SEGS, S5_TILE_ST), F32)
        block = lambda j: pl.ds(pl.multiple_of(j * SEGS, SEGS), SEGS)

        def finals(k, c):
            hi, lo = block(steps - 1 - k), block(half - 1 - k)
            return (*_cmul_add(ar, ai, c[0], c[1], gr_ref[hi, :], gi_ref[hi, :]),
                    *_cmul_add(ar, ai, c[2], c[3], gr_ref[lo, :], gi_ref[lo, :]))

        f = lax.fori_loop(0, half, finals, (zero,) * 4, unroll=4)
        e_hi, e_lo = _half_segment_entries(ar, ai, f[:2], f[2:], half, _shift_rows_up)

        def scan(k, c):
            accr, acci = c[4], c[5]
            j_hi, j_lo = steps - 1 - k, half - 1 - k
            hi, lo = block(j_hi), block(j_lo)
            hr, hi_im = _cmul_add(ar, ai, c[0], c[1], gr_ref[hi, :], gi_ref[hi, :])
            lr, lo_im = _cmul_add(ar, ai, c[2], c[3], gr_ref[lo, :], gi_ref[lo, :])
            gr_ref[hi, :], gi_ref[hi, :] = hr, hi_im
            gr_ref[lo, :], gi_ref[lo, :] = lr, lo_im
            before_hi = block(j_hi - 1)
            before_lo = block(jnp.maximum(j_lo - 1, 0))
            live = (j_lo > 0).astype(F32)
            xhr, xhi = xr_ref[before_hi, :], xi_ref[before_hi, :]
            xlr, xli = xr_ref[before_lo, :] * live, xi_ref[before_lo, :] * live
            accr = accr + (hr * xhr + hi_im * xhi) + (lr * xlr + lo_im * xli)
            acci = acci + (hi_im * xhr - hr * xhi) + (lo_im * xlr - lr * xli)
            return hr, hi_im, lr, lo_im, accr, acci

        out = lax.fori_loop(0, half, scan, (*e_hi, *e_lo, zero, zero), unroll=4)
        accr, acci = out[4], out[5]
        first = pl.ds(0, SEGS)
        last = pl.ds((steps - 1) * SEGS, SEGS)
        xpr = _shift_rows_down(xr_ref[last, :])
        xpi = _shift_rows_down(xi_ref[last, :])
        g0r = gr_ref[first, :]
        g0i = gi_ref[first, :]
        accr = accr + g0r * xpr + g0i * xpi
        acci = acci + g0i * xpr - g0r * xpi
        dar_ref[0] = jnp.sum(accr, axis=0, keepdims=True)
        dai_ref[0] = jnp.sum(acci, axis=0, keepdims=True)

        grb = gr_ref[...].astype(BF16)
        gib = gi_ref[...].astype(BF16)
        du_ref[...] = _dot(grb, bret_ref[0]) + _dot(gib, bimt_ref[0])
        dbre_ref[0] = _dot_tn(u, grb)
        dbim_ref[0] = _dot_tn(u, gib)
        dcre_ref[0] = _dot_tn(dy, xr_ref[...].astype(BF16))
        dcim_ref[0] = -_dot_tn(dy, xi_ref[...].astype(BF16))

    tile3 = lambda a, b: pl.BlockSpec((1, a, b), lambda k: (k, 0, 0))
    col_in = pl.BlockSpec((L, S5_TILE_IN), lambda k: (0, k))
    col_st = pl.BlockSpec((L, S5_TILE_ST), lambda k: (0, k))
    dense = jax.ShapeDtypeStruct((S5_TILES, S5_TILE_IN, S5_TILE_ST), F32)
    vec = jax.ShapeDtypeStruct((S5_TILES, 1, S5_TILE_ST), F32)
    return pl.pallas_call(
        body, name=name, grid=(S5_TILES,),
        in_specs=[col_in, col_in, col_st, col_st,
                  tile3(S5_TILE_ST, S5_TILE_IN), tile3(S5_TILE_ST, S5_TILE_IN),
                  tile3(S5_TILE_IN, S5_TILE_ST), tile3(S5_TILE_IN, S5_TILE_ST),
                  tile3(1, S5_TILE_ST), tile3(1, S5_TILE_ST)],
        out_specs=[col_in, tile3(1, S5_TILE_ST), tile3(1, S5_TILE_ST),
                   tile3(S5_TILE_IN, S5_TILE_ST), tile3(S5_TILE_IN, S5_TILE_ST),
                   tile3(S5_TILE_IN, S5_TILE_ST), tile3(S5_TILE_IN, S5_TILE_ST)],
        out_shape=[jax.ShapeDtypeStruct((L, 1024), F32), vec, vec, dense, dense, dense, dense],
        scratch_shapes=[pltpu.VMEM((L, S5_TILE_ST), F32), pltpu.VMEM((L, S5_TILE_ST), F32)],
    )(dy_perm, u_perm, xr, xi, bret_bd, bimt_bd, cret_bd, cimt_bd, are, aim)


def _s5_post(ys, u, d_skip, w_glu, b_glu, gain, layer, name):
    L = ys.shape[0]

    def body(ys_ref, u_ref, d_ref, w_ref, b_ref, g_ref, ya_ref):
        g = _gelu(ys_ref[...] + d_ref[...] * u_ref[...])
        q = _dot(g.astype(BF16), w_ref[...]) + b_ref[...]
        oa = g * _sigmoid(q)
        ya_ref[...] = (oa * _rstd(oa) * g_ref[...]).astype(BF16)

    vec = _full_spec((1, 1024))
    tile = min(L, WIDE_ROW_TILE)
    return pl.pallas_call(
        body, name=name, grid=(L // tile,),
        in_specs=[_row_spec(tile, 1024), _row_spec(tile, 1024), vec, _layer_spec((1024, 1024), layer), vec, vec],
        out_specs=_row_spec(tile, 1024),
        out_shape=jax.ShapeDtypeStruct((L, 1024), BF16),
    )(ys, u, d_skip, w_glu, b_glu, gain)


def _s5_post_bwd(dx, w_out, ys, u, d_skip, w_glu, b_glu, gain, layer, name):
    L = ys.shape[0]

    def body(dx_ref, wo_ref, ys_ref, u_ref, d_ref, w_ref, b_ref, gn_ref,
             dys_ref, dus_ref, g_ref, dq_ref, dgain_ref, dd_ref, db_ref):
        first = pl.program_id(0) == 0
        uv = u_ref[...]
        yt = ys_ref[...] + d_ref[...] * uv
        g = _gelu(yt)
        gb = g.astype(BF16)
        q = _dot(gb, w_ref[...]) + b_ref[...]
        s = _sigmoid(q)
        oa = g * s
        dya = _dot_nt(dx_ref[...], wo_ref[...])
        doa, dgain = _rms_bwd(oa, _rstd(oa), gn_ref[...], dya)
        dq = doa * g * s * (1.0 - s)
        dqb = dq.astype(BF16)
        dg = doa * s + _dot_nt(dqb, w_ref[...])
        dyt = dg * _dgelu(yt)
        dys_ref[...] = dyt
        dus_ref[...] = dyt * d_ref[...]
        g_ref[...] = gb
        dq_ref[...] = dqb
        _acc_rows(dgain_ref, dgain, first)
        _acc_rows(dd_ref, jnp.sum(dyt * uv, axis=0, keepdims=True), first)
        _acc_rows(db_ref, jnp.sum(dq, axis=0, keepdims=True), first)

    vec = _full_spec((1, 1024))
    row = _row_spec(ROW_TILE, 1024)
    vshape = jax.ShapeDtypeStruct((1, 1024), F32)
    return pl.pallas_call(
        body, name=name, grid=(L // ROW_TILE,),
        in_specs=[row, _layer_spec((1024, 1024), layer, 0), row, row, vec, _layer_spec((1024, 1024), layer), vec,
                  vec],
        out_specs=[row, row, row, row, vec, vec, vec],
        out_shape=[jax.ShapeDtypeStruct((L, 1024), F32), jax.ShapeDtypeStruct((L, 1024), F32),
                   jax.ShapeDtypeStruct((L, 1024), BF16), jax.ShapeDtypeStruct((L, 1024), BF16),
                   vshape, vshape, vshape],
    )(dx, w_out, ys, u, d_skip, w_glu, b_glu, gain)


CONV_TILE = 256


def _shift_time(v, d):
    if d == 0:
        return v
    rolled = pltpu.roll(v, d, 0)
    row = lax.broadcasted_iota(jnp.int32, v.shape, 0)
    return jnp.where(row < d, 0.0, rolled)


def _unshift_time(v, d):
    if d == 0:
        return v
    n = v.shape[0]
    rolled = pltpu.roll(v, n - d, 0)
    row = lax.broadcasted_iota(jnp.int32, v.shape, 0)
    return jnp.where(row >= n - d, 0.0, rolled)


def _ssd_conv(xbc, w, b, name):
    L = xbc.shape[0]

    def body(x_ref, w_ref, b_ref, o_ref):
        xv = x_ref[...]
        pre = jnp.broadcast_to(b_ref[...], xv.shape)
        for k in range(SSD_CONV):
            pre = pre + w_ref[k:k + 1, :] * _shift_time(xv, SSD_CONV - 1 - k)
        o_ref[...] = _silu(pre)

    col = pl.BlockSpec((L, CONV_TILE), lambda j: (0, j))
    return pl.pallas_call(
        body, name=name, grid=(SSD_CONV_DIM // CONV_TILE,),
        in_specs=[col, pl.BlockSpec((8, CONV_TILE), lambda j: (0, j)), pl.BlockSpec((1, CONV_TILE), lambda j: (0, j))],
        out_specs=col, out_shape=jax.ShapeDtypeStruct((L, SSD_CONV_DIM), F32),
    )(xbc, w, b)


def _ssd_conv_bwd(dxc, xbc, w, b, name):
    L = xbc.shape[0]

    def body(d_ref, x_ref, w_ref, b_ref, dx_ref, dw_ref, db_ref):
        xv = x_ref[...]
        shifted = [_shift_time(xv, SSD_CONV - 1 - k) for k in range(SSD_CONV)]
        pre = jnp.broadcast_to(b_ref[...], xv.shape)
        for k in range(SSD_CONV):
            pre = pre + w_ref[k:k + 1, :] * shifted[k]
        dpre = d_ref[...] * _dsilu(pre)
        dx = jnp.zeros_like(xv)
        rows = []
        for k in range(SSD_CONV):
            dx = dx + w_ref[k:k + 1, :] * _unshift_time(dpre, SSD_CONV - 1 - k)
            rows.append(jnp.sum(dpre * shifted[k], axis=0, keepdims=True))
        dx_ref[...] = dx
        dw_ref[...] = jnp.concatenate(rows + [jnp.zeros((8 - SSD_CONV, CONV_TILE), F32)], axis=0)
        db_ref[...] = jnp.sum(dpre, axis=0, keepdims=True)

    col = pl.BlockSpec((L, CONV_TILE), lambda j: (0, j))
    w_spec = pl.BlockSpec((8, CONV_TILE), lambda j: (0, j))
    b_spec = pl.BlockSpec((1, CONV_TILE), lambda j: (0, j))
    return pl.pallas_call(
        body, name=name, grid=(SSD_CONV_DIM // CONV_TILE,),
        in_specs=[col, col, w_spec, b_spec], out_specs=[col, w_spec, b_spec],
        out_shape=[jax.ShapeDtypeStruct((L, SSD_CONV_DIM), F32), jax.ShapeDtypeStruct((8, SSD_CONV_DIM), F32),
                   jax.ShapeDtypeStruct((1, SSD_CONV_DIM), F32)],
    )(dxc, xbc, w, b)


def _tri(lower):
    r = lax.broadcasted_iota(jnp.int32, (SSD_CHUNK, SSD_CHUNK), 0)
    c = lax.broadcasted_iota(jnp.int32, (SSD_CHUNK, SSD_CHUNK), 1)
    return (r >= c) if lower else (r <= c)


def _ssd_chunk_common(dt_ref, bias_ref, alog_ref):
    pre = dt_ref[...] + bias_ref[0]
    dtp = _softplus(pre)
    a_neg = -jnp.exp(alog_ref[0])
    dta = dtp * a_neg
    acum = _select_rows(_tri(True), dta)
    return pre, dtp, a_neg, dta, acum


GROUP_W = SSD_GROUP_HEADS * SSD_HEAD_DIM


def _head_expander():
    r = lax.broadcasted_iota(jnp.int32, (LANES, GROUP_W), 0)
    c = lax.broadcasted_iota(jnp.int32, (LANES, GROUP_W), 1)
    return (c // SSD_HEAD_DIM == r).astype(F32)


def _split_bf16(a, terms):
    parts = []
    rest = a
    for _ in range(terms):
        piece = rest.astype(BF16)
        parts.append(piece)
        rest = rest - piece.astype(F32)
    return parts


def _select_cols(a, sel, terms=3):
    lhs = jnp.concatenate(_split_bf16(a, terms), axis=1)
    rhs = jnp.concatenate([sel.astype(BF16)] * terms, axis=0)
    return _dot(lhs, rhs)


def _select_rows(sel, b, terms=3):
    lhs = jnp.concatenate([sel.astype(BF16)] * terms, axis=1)
    rhs = jnp.concatenate(_split_bf16(b, terms), axis=0)
    return _dot(lhs, rhs)


def _decay_mask(acum_all, acum_t, h, lower):
    seg = acum_all[:, h:h + 1] - acum_t[h:h + 1, :]
    return jnp.where(lower, jnp.exp(jnp.minimum(seg, 0.0)), 0.0)


def _ssd_scan(xc, dt, dt_bias, a_log, d_wide, expand, expand_t, name):
    L = xc.shape[0]
    nc = L // SSD_CHUNK

    def body(x_ref, b_ref, c_ref, dt_ref, bias_ref, alog_ref, d_ref, e_ref, et_ref, y_ref, sp_ref, s_ref, xdt_ref):
        @pl.when(pl.program_id(1) == 0)
        def _():
            s_ref[...] = jnp.zeros_like(s_ref)

        _, dtp_all, _, _, acum_all = _ssd_chunk_common(dt_ref, bias_ref, alog_ref)
        acum_t = acum_all.T
        wide = _select_cols(jnp.concatenate([acum_all, dtp_all], axis=0), e_ref[...])
        acum_e = wide[:SSD_CHUNK]
        alast_e = acum_e[SSD_CHUNK - 1:SSD_CHUNK, :]
        x = x_ref[...]
        xdt = x * wide[SSD_CHUNK:]
        xdt_ref[...] = xdt.astype(BF16)
        bm = b_ref[...].astype(BF16)
        cm = c_ref[...].astype(BF16)
        cb = _dot_nt(cm, bm)
        lower = _tri(True)
        sp = s_ref[...]
        for h in range(SSD_GROUP_HEADS):
            cols = slice(h * SSD_HEAD_DIM, (h + 1) * SSD_HEAD_DIM)
            lm = _decay_mask(acum_all, acum_t, h, lower)
            y_ref[:, cols] = _dot((cb * lm).astype(BF16), xdt_ref[:, cols])
        y_ref[...] += _dot_nt(cm, sp.astype(BF16)) * jnp.exp(acum_e) + d_ref[0] * x
        wgt = xdt * jnp.exp(alast_e - acum_e)
        ealast = jnp.exp(_select_rows(et_ref[...], acum_t)[:, SSD_CHUNK - 1:SSD_CHUNK])
        sp_ref[0, 0] = sp
        s_ref[...] = ealast * sp + _dot_tn(wgt.astype(BF16), bm)

    par = lambda n: pl.BlockSpec((1, 1, n), lambda g, c: (g, 0, 0))
    return pl.pallas_call(
        body, name=name, grid=(SSD_GROUPS, nc),
        in_specs=[pl.BlockSpec((SSD_CHUNK, GROUP_W), lambda g, c: (c, g)),
                  pl.BlockSpec((SSD_CHUNK, SSD_STATE), lambda g, c: (c, 8 + g)),
                  pl.BlockSpec((SSD_CHUNK, SSD_STATE), lambda g, c: (c, 10 + g)),
                  pl.BlockSpec((SSD_CHUNK, LANES), lambda g, c: (c, g)),
                  par(LANES), par(LANES), par(GROUP_W), _full_spec((LANES, GROUP_W)), _full_spec((GROUP_W, LANES))],
        out_specs=[pl.BlockSpec((SSD_CHUNK, GROUP_W), lambda g, c: (c, g)),
                   pl.BlockSpec((1, 1, GROUP_W, SSD_STATE), lambda g, c: (c, g, 0, 0))],
        out_shape=[jax.ShapeDtypeStruct((L, SSD_WIDTH), F32),
                   jax.ShapeDtypeStruct((nc, SSD_GROUPS, GROUP_W, SSD_STATE), F32)],
        scratch_shapes=[pltpu.VMEM((GROUP_W, SSD_STATE), F32), pltpu.VMEM((SSD_CHUNK, GROUP_W), BF16)],
    )(xc, xc, xc, dt, dt_bias, a_log, d_wide, expand, expand_t)


def _ssd_scan_bwd(dy, xc, dt, sprev, dt_bias, a_log, d_wide, expand, expand_t, name):
    L = xc.shape[0]
    nc = L // SSD_CHUNK

    def body(dy_ref, x_ref, b_ref, c_ref, dt_ref, sp_ref, bias_ref, alog_ref, d_ref, e_ref, et_ref,
             dx_ref, db_ref, dc_ref, ddt_ref, dbias_ref, dalog_ref, dd_ref, ds_ref, xdt_ref, dyb_ref):
        first = pl.program_id(1) == 0

        @pl.when(first)
        def _():
            ds_ref[...] = jnp.zeros_like(ds_ref)

        pre, dtp_all, a_neg, _, acum_all = _ssd_chunk_common(dt_ref, bias_ref, alog_ref)
        acum_t = acum_all.T
        e = e_ref[...]
        et = et_ref[...]
        wide = _select_cols(jnp.concatenate([acum_all, dtp_all], axis=0), e)
        acum_e = wide[:SSD_CHUNK]
        dtp_e = wide[SSD_CHUNK:]
        alast_e = acum_e[SSD_CHUNK - 1:SSD_CHUNK, :]
        dstate_e = jnp.exp(alast_e - acum_e)
        x = x_ref[...]
        dy = dy_ref[...]
        xdt = x * dtp_e
        xdt_ref[...] = xdt.astype(BF16)
        dyb_ref[...] = dy.astype(BF16)
        bm = b_ref[...].astype(BF16)
        cm = c_ref[...].astype(BF16)
        cb = _dot_nt(cm, bm)
        sp = sp_ref[0, 0]
        spb = sp.astype(BF16)
        dsn = ds_ref[...]
        dsb = dsn.astype(BF16)
        z = _dot_nt(cm, spb)
        dz = dy * jnp.exp(acum_e)
        dzb = dz.astype(BF16)
        dc_acc = _dot(dzb, spb)
        ealast = jnp.exp(_select_rows(et, acum_t)[:, SSD_CHUNK - 1:SSD_CHUNK])
        ds_ref[...] = _dot_tn(dzb, cm) + ealast * dsn
        dw = _dot_nt(bm, dsb)
        wgt = xdt * dstate_e
        db_acc = _dot(wgt.astype(BF16), dsb)
        lower = _tri(True)
        lane = lax.broadcasted_iota(jnp.int32, (SSD_CHUNK, LANES), 1)
        row = lax.broadcasted_iota(jnp.int32, (SSD_CHUNK, LANES), 0)
        dcb = jnp.zeros((SSD_CHUNK, SSD_CHUNK), F32)
        dacum_all = jnp.zeros((SSD_CHUNK, LANES), F32)
        dacum_cols = jnp.zeros((SSD_CHUNK, LANES), F32)
        for h in range(SSD_GROUP_HEADS):
            cols = slice(h * SSD_HEAD_DIM, (h + 1) * SSD_HEAD_DIM)
            lm = _decay_mask(acum_all, acum_t, h, lower)
            dm = _dot_nt(dyb_ref[:, cols], xdt_ref[:, cols])
            dx_ref[:, cols] = _dot_tn((cb * lm).astype(BF16), dyb_ref[:, cols])
            dm_lm = dm * lm
            dcb = dcb + dm_lm
            q = dm_lm * cb
            dacum_all = jnp.where(lane == h, jnp.sum(q, axis=1, keepdims=True), dacum_all)
            dacum_cols = jnp.where(row == h, jnp.sum(q, axis=0, keepdims=True), dacum_cols)
        dxdt = dx_ref[...] + dw * dstate_e
        sums = _select_cols(jnp.concatenate([dz * z, dw * wgt, dxdt * x, dy * x], axis=0), et, terms=2)
        dacum_off = sums[0:SSD_CHUNK]
        dds_ds = sums[SSD_CHUNK:2 * SSD_CHUNK]
        ddtp_x = sums[2 * SSD_CHUNK:3 * SSD_CHUNK]
        dd_part = sums[3 * SSD_CHUNK:4 * SSD_CHUNK]
        ds_s = jnp.sum(_select_rows(e, dsn * sp, terms=2).T, axis=0, keepdims=True)
        dalast = ds_s * jnp.exp(acum_all[SSD_CHUNK - 1:SSD_CHUNK, :]) + jnp.sum(dds_ds, axis=0, keepdims=True)
        dacum_all = dacum_all - dacum_cols.T + dacum_off - dds_ds + jnp.where(row == SSD_CHUNK - 1, dalast, 0.0)
        dx_ref[...] = d_ref[0] * dy + dxdt * dtp_e
        dcbb = dcb.astype(BF16)
        dc_ref[...] = dc_acc + _dot(dcbb, bm)
        db_ref[...] = db_acc + _dot_tn(dcbb, cm)
        ddta = _select_rows(_tri(False), dacum_all)
        ddt = (ddtp_x + ddta * a_neg) * _sigmoid(pre)
        ddt_ref[...] = ddt
        _acc_rows(dbias_ref, jnp.sum(ddt, axis=0, keepdims=True)[None], first)
        _acc_rows(dalog_ref, (jnp.sum(ddta * dtp_all, axis=0, keepdims=True) * a_neg)[None], first)
        _acc_rows(dd_ref, jnp.sum(dd_part, axis=0, keepdims=True)[None], first)

    rev = lambda c: nc - 1 - c
    par = lambda n: pl.BlockSpec((1, 1, n), lambda g, c: (g, 0, 0))
    pshape = jax.ShapeDtypeStruct((SSD_GROUPS, 1, LANES), F32)
    return pl.pallas_call(
        body, name=name, grid=(SSD_GROUPS, nc),
        in_specs=[pl.BlockSpec((SSD_CHUNK, GROUP_W), lambda g, c: (rev(c), g)),
                  pl.BlockSpec((SSD_CHUNK, GROUP_W), lambda g, c: (rev(c), g)),
                  pl.BlockSpec((SSD_CHUNK, SSD_STATE), lambda g, c: (rev(c), 8 + g)),
                  pl.BlockSpec((SSD_CHUNK, SSD_STATE), lambda g, c: (rev(c), 10 + g)),
                  pl.BlockSpec((SSD_CHUNK, LANES), lambda g, c: (rev(c), g)),
                  pl.BlockSpec((1, 1, GROUP_W, SSD_STATE), lambda g, c: (rev(c), g, 0, 0)),
                  par(LANES), par(LANES), par(GROUP_W), _full_spec((LANES, GROUP_W)), _full_spec((GROUP_W, LANES))],
        out_specs=[pl.BlockSpec((SSD_CHUNK, GROUP_W), lambda g, c: (rev(c), g)),
                   pl.BlockSpec((SSD_CHUNK, SSD_STATE), lambda g, c: (rev(c), g)),
                   pl.BlockSpec((SSD_CHUNK, SSD_STATE), lambda g, c: (rev(c), g)),
                   pl.BlockSpec((SSD_CHUNK, LANES), lambda g, c: (rev(c), g)),
                   par(LANES), par(LANES), par(LANES)],
        out_shape=[jax.ShapeDtypeStruct((L, SSD_WIDTH), F32), jax.ShapeDtypeStruct((L, 256), F32),
                   jax.ShapeDtypeStruct((L, 256), F32), jax.ShapeDtypeStruct((L, 256), F32),
                   pshape, pshape, pshape],
        scratch_shapes=[pltpu.VMEM((GROUP_W, SSD_STATE), F32), pltpu.VMEM((SSD_CHUNK, GROUP_W), BF16),
                        pltpu.VMEM((SSD_CHUNK, GROUP_W), BF16)],
    )(dy, xc, xc, xc, dt, sprev, dt_bias, a_log, d_wide, expand, expand_t)


def _ssd_post(y, z, gain, name):
    L = y.shape[0]

    def body(y_ref, z_ref, g_ref, o_ref):
        ob = y_ref[...] * _silu(z_ref[...])
        o_ref[...] = (ob * _rstd(ob) * g_ref[...]).astype(BF16)

    tile = min(L, WIDE_ROW_TILE)
    row = _row_spec(tile, 1024)
    return pl.pallas_call(body, name=name, grid=(L // tile,), in_specs=[row, row, _full_spec((1, 1024))],
                          out_specs=row, out_shape=jax.ShapeDtypeStruct((L, 1024), BF16))(y, z, gain)


def _ssd_post_bwd(dx, w_out, y, z, gain, layer, name):
    L = y.shape[0]

    def body(dx_ref, wo_ref, y_ref, z_ref, g_ref, dy_ref, dz_ref, dgain_ref):
        first = pl.program_id(0) == 0
        yv = y_ref[...]
        zv = z_ref[...]
        sz = _silu(zv)
        ob = yv * sz
        dyb = _dot_nt(dx_ref[...], wo_ref[...])
        dob, dgain = _rms_bwd(ob, _rstd(ob), g_ref[...], dyb)
        dy_ref[...] = dob * sz
        dz_ref[...] = dob * yv * _dsilu(zv)
        _acc_rows(dgain_ref, dgain, first)

    row = _row_spec(ROW_TILE, 1024)
    vec = _full_spec((1, 1024))
    return pl.pallas_call(
        body, name=name, grid=(L // ROW_TILE,),
        in_specs=[row, _layer_spec((1024, 1024), layer, 1), row, row, vec],
        out_specs=[row, row, vec],
        out_shape=[jax.ShapeDtypeStruct((L, 1024), F32), jax.ShapeDtypeStruct((L, 1024), F32),
                   jax.ShapeDtypeStruct((1, 1024), F32)],
    )(dx, w_out, y, z, gain)


def _out_proj(x, ya, yb, w_out, layer, name):
    L = x.shape[0]

    def body(x_ref, ya_ref, yb_ref, w_ref, o_ref):
        o_ref[...] = x_ref[...] + _dot(ya_ref[...], w_ref[:1024, :]) + _dot(yb_ref[...], w_ref[1024:, :])

    tile = min(L, WIDE_ROW_TILE)
    row = _row_spec(tile, 1024)
    return pl.pallas_call(body, name=name, grid=(L // tile,),
                          in_specs=[row, row, row, _layer_spec((2048, 1024), layer)],
                          out_specs=row, out_shape=jax.ShapeDtypeStruct((L, D_MODEL), F32))(x, ya, yb, w_out)


def _ffn(x, gain, w_gate, w_up, w_down, layer, name):
    L = x.shape[0]

    def body(x_ref, g_ref, wg_ref, wu_ref, wd_ref, o_ref, gt_ref, up_ref):
        xv = x_ref[...]
        h = (xv * _rstd(xv) * g_ref[...]).astype(BF16)
        gt = _dot_nt(h, wg_ref[...])
        up = _dot_nt(h, wu_ref[...])
        gt_ref[...] = gt
        up_ref[...] = up
        o_ref[...] = xv + _dot((_silu(gt) * up).astype(BF16), wd_ref[...])

    row = _row_spec(ROW_TILE, D_MODEL)
    hid = _row_spec(ROW_TILE, FFN)
    return pl.pallas_call(
        body, name=name, grid=(L // ROW_TILE,),
        in_specs=[row, _full_spec((1, D_MODEL)), _layer_spec((FFN, D_MODEL), layer),
                  _layer_spec((FFN, D_MODEL), layer), _layer_spec((FFN, D_MODEL), layer)],
        out_specs=[row, hid, hid],
        out_shape=[jax.ShapeDtypeStruct((L, D_MODEL), F32), jax.ShapeDtypeStruct((L, FFN), F32),
                   jax.ShapeDtypeStruct((L, FFN), F32)],
    )(x, gain, w_gate, w_up, w_down)


def _ffn_bwd(dx2, x1, gt, up, gain, w_gate, w_up, w_down, layer, name):
    L = x1.shape[0]

    def body(d_ref, x_ref, gt_ref, up_ref, g_ref, wg_ref, wu_ref, wd_ref,
             dx_ref, dxb_ref, h_ref, act_ref, dgt_ref, dup_ref, dgain_ref):
        first = pl.program_id(0) == 0
        dv = d_ref[...]
        xv = x_ref[...]
        r = _rstd(xv)
        h_ref[...] = (xv * r * g_ref[...]).astype(BF16)
        gtv = gt_ref[...]
        upv = up_ref[...]
        sg = _silu(gtv)
        act_ref[...] = (sg * upv).astype(BF16)
        dact = _dot_nt(dv.astype(BF16), wd_ref[...])
        dgt = (dact * upv * _dsilu(gtv)).astype(BF16)
        dup = (dact * sg).astype(BF16)
        dgt_ref[...] = dgt
        dup_ref[...] = dup
        dh = _dot(dgt, wg_ref[...]) + _dot(dup, wu_ref[...])
        dxn, dgain = _rms_bwd(xv, r, g_ref[...], dh)
        dx = dv + dxn
        dx_ref[...] = dx
        dxb_ref[...] = dx.astype(BF16)
        _acc_rows(dgain_ref, dgain, first)

    row = _row_spec(ROW_TILE, D_MODEL)
    hid = _row_spec(ROW_TILE, FFN)
    vec = _full_spec((1, D_MODEL))
    return pl.pallas_call(
        body, name=name, grid=(L // ROW_TILE,),
        in_specs=[row, row, hid, hid, vec, _layer_spec((FFN, D_MODEL), layer), _layer_spec((FFN, D_MODEL), layer),
                  _layer_spec((FFN, D_MODEL), layer)],
        out_specs=[row, row, row, hid, hid, hid, vec],
        out_shape=[jax.ShapeDtypeStruct((L, D_MODEL), F32), jax.ShapeDtypeStruct((L, D_MODEL), BF16),
                   jax.ShapeDtypeStruct((L, D_MODEL), BF16),
                   jax.ShapeDtypeStruct((L, FFN), BF16), jax.ShapeDtypeStruct((L, FFN), BF16),
                   jax.ShapeDtypeStruct((L, FFN), BF16), jax.ShapeDtypeStruct((1, D_MODEL), F32)],
    )(dx2, x1, gt, up, gain, w_gate, w_up, w_down)


def _inproj_bwd(dx1, x0, du_skip, du_scan, dz, dxbc, ddt, gain, w_pad, layer, name):
    L = x0.shape[0]

    def body(d_ref, x_ref, dus_ref, duc_ref, dz_ref, dxbc_ref, ddt_ref, g_ref, w_ref,
             dx_ref, dxb_ref, dp_ref, dgain_ref):
        first = pl.program_id(0) == 0
        xv = x_ref[...]
        dp = jnp.concatenate([dus_ref[...] + duc_ref[...], dz_ref[...], dxbc_ref[...], ddt_ref[...]],
                             axis=1).astype(BF16)
        dp_ref[...] = dp
        dh = _dot_nt(dp, w_ref[...])
        dxn, dgain = _rms_bwd(xv, _rstd(xv), g_ref[...], dh)
        dx = d_ref[...] + dxn
        dx_ref[...] = dx
        dxb_ref[...] = dx.astype(BF16)
        _acc_rows(dgain_ref, dgain, first)

    row = _row_spec(ROW_TILE, D_MODEL)
    vec = _full_spec((1, D_MODEL))
    return pl.pallas_call(
        body, name=name, grid=(L // ROW_TILE,),
        in_specs=[row, row, row, row, row, _row_spec(ROW_TILE, SSD_CONV_DIM), _row_spec(ROW_TILE, 256), vec,
                  _layer_spec((D_MODEL, IN_PAD), layer)],
        out_specs=[row, row, _row_spec(ROW_TILE, IN_PAD), vec],
        out_shape=[jax.ShapeDtypeStruct((L, D_MODEL), F32), jax.ShapeDtypeStruct((L, D_MODEL), BF16),
                   jax.ShapeDtypeStruct((L, IN_PAD), BF16), jax.ShapeDtypeStruct((1, D_MODEL), F32)],
    )(dx1, x0, du_skip, du_scan, dz, dxbc, ddt, gain, w_pad)


def _final_loss(x, gain, target, name):
    L = x.shape[0]

    def body(x_ref, g_ref, t_ref, loss_ref, dx_ref, dxb_ref, dgain_ref):
        first = pl.program_id(0) == 0
        xv = x_ref[...]
        r = _rstd(xv)
        err = xv * r * g_ref[...] - t_ref[...]
        part = 0.5 * jnp.sum(jnp.mean(err * err, axis=-1, keepdims=True), axis=0, keepdims=True)
        dx, dgain = _rms_bwd(xv, r, g_ref[...], err * (1.0 / D_MODEL))
        dx_ref[...] = dx
        dxb_ref[...] = dx.astype(BF16)
        _acc_rows(loss_ref, jnp.broadcast_to(part, (1, LANES)), first)
        _acc_rows(dgain_ref, dgain, first)

    row = _row_spec(ROW_TILE, D_MODEL)
    vec = _full_spec((1, D_MODEL))
    return pl.pallas_call(
        body, name=name, grid=(L // ROW_TILE,), in_specs=[row, vec, row],
        out_specs=[_full_spec((1, LANES)), row, row, vec],
        out_shape=[jax.ShapeDtypeStruct((1, LANES), F32), jax.ShapeDtypeStruct((L, D_MODEL), F32),
                   jax.ShapeDtypeStruct((L, D_MODEL), BF16), jax.ShapeDtypeStruct((1, D_MODEL), F32)],
    )(x, gain, target)


def _to_segments(a):
    L, n = a.shape
    return a.reshape(SEGS, L // SEGS, n).transpose(1, 0, 2).reshape(L, n)


def _from_segments(a):
    L, n = a.shape
    return a.reshape(L // SEGS, SEGS, n).transpose(1, 0, 2).reshape(L, n)


def _diag_block(g):
    k, a = divmod(g, S5_TILE_GROUPS)
    return k, slice(a * S5_GROUP, (a + 1) * S5_GROUP), slice(a * S5_STATE, (a + 1) * S5_STATE)


def _block_diag_build(mats, name):
    n = mats.shape[0]

    def body(m_ref, o_ref):
        o_ref[...] = jnp.zeros(o_ref.shape, BF16)
        for q in range(n):
            for g in range(S5_GROUPS):
                k, rows, cols = _diag_block(g)
                o_ref[q, k, rows, cols] = m_ref[q, g].astype(BF16)

    return pl.pallas_call(body, name=name,
                          out_shape=jax.ShapeDtypeStruct((n, S5_TILES, S5_TILE_IN, S5_TILE_ST), BF16))(mats)


def _block_diag_extract(dense, name):
    n = len(dense)

    def body(*refs):
        o_ref = refs[n]
        for q in range(n):
            for g in range(S5_GROUPS):
                k, rows, cols = _diag_block(g)
                o_ref[q, g] = refs[q][k, rows, cols]

    return pl.pallas_call(body, name=name,
                          out_shape=jax.ShapeDtypeStruct((n, S5_GROUPS, S5_GROUP, S5_STATE), F32))(*dense)


def _pad_in_proj(w):
    z = jnp.zeros(w.shape[:-1] + (LANES - SSD_GROUP_HEADS,), w.dtype)
    return jnp.concatenate([w[..., :IN_MAIN + 8], z, w[..., IN_MAIN + 8:], z], axis=-1)


def _unpad_in_proj(w):
    return jnp.concatenate([w[..., :IN_MAIN + 8], w[..., IN_MAIN + LANES:IN_MAIN + LANES + 8]], axis=-1)


def _lane_dense(a):
    return a.reshape(DEPTH, D_MODEL // LANES, LANES, -1).transpose(3, 1, 0, 2).reshape(-1, LANES)


def _from_lane_dense(a):
    return a.reshape(-1, D_MODEL // LANES, DEPTH, LANES).transpose(2, 1, 3, 0).reshape(DEPTH, D_MODEL, -1)


def _pad_heads(v):
    v = v.reshape(SSD_GROUPS, 1, SSD_GROUP_HEADS)
    return jnp.pad(v, ((0, 0), (0, 0), (0, LANES - SSD_GROUP_HEADS)))


def _unpad_heads(v):
    return v[:, 0, :SSD_GROUP_HEADS].reshape(SSD_HEADS)


def _layer_forward(x0, p, big, i, after_inproj=None, before_s5_post=None, before_ffn=None):
    tag = "l%d_" % i
    ls = p["s5_log_step"].reshape(S5_GROUPS, 1)
    b_hgp = (p["s5_b_re"].transpose(2, 0, 1), p["s5_b_im"].transpose(2, 0, 1))
    are, aim, bbre, bbim = _s5_prep(p["s5_lam_re"], p["s5_lam_im"], ls, b_hgp[0], b_hgp[1], tag + "s5_prep")
    mats = jnp.stack([bbre.transpose(1, 0, 2), bbim.transpose(1, 0, 2), p["s5_c_re"], p["s5_c_im"]])
    bre_bd, bim_bd, cret_bd, cimt_bd = _block_diag_build(mats, tag + "s5_blockdiag")
    s5mats = dict(bre_bd=bre_bd, bim_bd=bim_bd, cret_bd=cret_bd, cimt_bd=cimt_bd,
                  bret_bd=bre_bd.transpose(0, 2, 1), bimt_bd=bim_bd.transpose(0, 2, 1),
                  cre_bd=cret_bd.transpose(0, 2, 1), cim_bd=cimt_bd.transpose(0, 2, 1),
                  are=are.reshape(S5_TILES, 1, S5_TILE_ST), aim=aim.reshape(S5_TILES, 1, S5_TILE_ST))

    u, z, xbc, dt, h1 = _rms_inproj(x0, p["norm_mix"].reshape(1, -1), big["w_in"], None, tag + "rms_inproj")
    if after_inproj is not None:
        after_inproj(u)
    u_perm = _to_segments(u)
    ys_perm, xr, xi = _s5_scan(u_perm, bre_bd, bim_bd, s5mats["cre_bd"], s5mats["cim_bd"],
                               s5mats["are"], s5mats["aim"], tag + "s5_scan")
    ys = _from_segments(ys_perm)
    late_matrices = before_s5_post(ys) if before_s5_post is not None else {}
    big = {**big, **late_matrices}
    ya = _s5_post(ys, u, p["s5_d"].reshape(1, -1), big["s5_w_glu"], p["s5_b_glu"].reshape(1, -1),
                  p["s5_norm"].reshape(1, -1), None, tag + "s5_post")

    conv_w = jnp.pad(p["ssd_conv_w"], ((0, 8 - SSD_CONV), (0, 0)))
    conv_b = p["ssd_conv_b"].reshape(1, -1)
    xc = _ssd_conv(xbc, conv_w, conv_b, tag + "ssd_conv")
    expand = _head_expander()
    heads = dict(dt_bias=_pad_heads(p["ssd_dt_bias"]), a_log=_pad_heads(p["ssd_a_log"]),
                 d=jnp.repeat(p["ssd_d"], SSD_HEAD_DIM).reshape(SSD_GROUPS, 1, GROUP_W),
                 expand=expand, expand_t=expand.T)
    y, sprev = _ssd_scan(xc, dt, heads["dt_bias"], heads["a_log"], heads["d"], expand, heads["expand_t"],
                         tag + "ssd_scan")
    yb = _ssd_post(y, z, p["ssd_norm"].reshape(1, -1), tag + "ssd_post")

    x1 = _out_proj(x0, ya, yb, big["w_out"], None, tag + "out_proj")
    ffn_matrices = before_ffn(x1) if before_ffn is not None else {}
    big = {**big, **ffn_matrices}
    late_matrices = {**late_matrices, **ffn_matrices}
    x2, gt, up = _ffn(x1, p["norm_ffn"].reshape(1, -1), big["w_gate"], big["w_up"], big["w_down"], None,
                      tag + "ffn")
    saved = dict(x0=x0, h1=h1, u=u, u_perm=u_perm, z=z, xbc=xbc, dt=dt, xr=xr, xi=xi, ys=ys, ya=ya, xc=xc, y=y,
                 sprev=sprev, yb=yb, x1=x1, gt=gt, up=up, s5mats=s5mats, heads=heads, conv_w=conv_w,
                 conv_b=conv_b, ls=ls, b_hgp=b_hgp, late_matrices=late_matrices)
    return x2, saved


def _layer_backward(dx2, dx2b, p, big, s, i, after_ffn_grads=None, after_s5_grads=None):
    tag = "l%d_" % i
    g = {}
    dx1, dx1b, h2, act, dgt, dup, dgain = _ffn_bwd(dx2, s["x1"], s["gt"], s["up"], p["norm_ffn"].reshape(1, -1),
                                                  big["w_gate"], big["w_up"], big["w_down"], None, tag + "ffn_bwd")
    g["norm_ffn"] = dgain[0]
    g["w_down"] = _mm_tn(act, dx2b, tag + "dw_down")
    g["w_gate"] = _mm_tn(dgt, h2, tag + "dw_gate")
    g["w_up"] = _mm_tn(dup, h2, tag + "dw_up")
    g["w_out"] = _mm_tn(jnp.concatenate([s["ya"], s["yb"]], axis=1), dx1b, tag + "dw_out")
    if after_ffn_grads is not None:
        p = {**p, "s5_norm": p["s5_norm"] + after_ffn_grads(g)[0, 0]}

    dys, du_skip, gelu_b, dq_b, dgain, dd, dbg = _s5_post_bwd(
        dx1b, big["w_out"], s["ys"], s["u"], p["s5_d"].reshape(1, -1), big["s5_w_glu"],
        p["s5_b_glu"].reshape(1, -1), p["s5_norm"].reshape(1, -1), None, tag + "s5_post_bwd")
    g["s5_norm"] = dgain[0]
    g["s5_d"] = dd[0]
    g["s5_b_glu"] = dbg[0]
    g["s5_w_glu"] = _mm_tn(gelu_b, dq_b, tag + "dw_glu")
    m = s["s5mats"]
    du_perm, dar, dai, dcre_d, dcim_d, dbre_d, dbim_d = _s5_scan_bwd(
        _to_segments(dys), s["u_perm"], s["xr"], s["xi"], m["bret_bd"], m["bimt_bd"], m["cret_bd"], m["cimt_bd"],
        m["are"], m["aim"], tag + "s5_scan_bwd")
    du_scan = _from_segments(du_perm)
    diag = _block_diag_extract([dcre_d, dcim_d, dbre_d, dbim_d], tag + "s5_blockdiag_bwd")
    g["s5_c_re"], g["s5_c_im"] = diag[0], diag[1]
    dbbre = diag[2].transpose(1, 0, 2)
    dbbim = diag[3].transpose(1, 0, 2)
    dlr, dli, dls, dbre, dbim = _s5_prep_bwd(
        p["s5_lam_re"], p["s5_lam_im"], s["ls"], s["b_hgp"][0], s["b_hgp"][1],
        dar.reshape(S5_GROUPS, S5_STATE), dai.reshape(S5_GROUPS, S5_STATE), dbbre, dbbim, tag + "s5_prep_bwd")
    g["s5_lam_re"] = dlr
    g["s5_lam_im"] = dli
    g["s5_log_step"] = dls[:, 0]
    g["s5_b_re"] = dbre
    g["s5_b_im"] = dbim
    if after_s5_grads is not None:
        p = {**p, "ssd_norm": p["ssd_norm"] + after_s5_grads(g)[0, 0]}

    dy, dz, dgain = _ssd_post_bwd(dx1b, big["w_out"], s["y"], s["z"], p["ssd_norm"].reshape(1, -1), None,
                                  tag + "ssd_post_bwd")
    g["ssd_norm"] = dgain[0]
    hd = s["heads"]
    dxs, dbm, dcm, ddt, dbias, dalog, dd = _ssd_scan_bwd(dy, s["xc"], s["dt"], s["sprev"], hd["dt_bias"],
                                                       hd["a_log"], hd["d"], hd["expand"], hd["expand_t"],
                                                       tag + "ssd_scan_bwd")
    g["ssd_dt_bias"] = _unpad_heads(dbias)
    g["ssd_a_log"] = _unpad_heads(dalog)
    g["ssd_d"] = _unpad_heads(dd)
    dxc = jnp.concatenate([dxs, dbm, dcm], axis=1)
    dxbc, dcw, dcb = _ssd_conv_bwd(dxc, s["xbc"], s["conv_w"], s["conv_b"], tag + "ssd_conv_bwd")
    g["ssd_conv_w"] = dcw[:SSD_CONV]
    g["ssd_conv_b"] = dcb[0]

    dx0, dx0b, dproj, dgain = _inproj_bwd(dx1, s["x0"], du_skip, du_scan, dz, dxbc, ddt, p["norm_mix"].reshape(1, -1),
                                          big["w_in"], None, tag + "inproj_bwd")
    g["norm_mix"] = dgain[0]
    g["w_in"] = _mm_tn(s["h1"], dproj, tag + "dw_in")
    return dx0, dx0b, g


MIXER_BIG = ("w_in", "s5_w_glu", "w_out")
FFN_BIG = ("w_gate", "w_up", "w_down")
BIG = MIXER_BIG + FFN_BIG
COL_SHARDED = ("w_in",)
T_STORED = ("w_gate", "w_up")
LAYER_SMALL = ("norm_mix", "s5_lam_re", "s5_lam_im", "s5_log_step", "s5_b_re", "s5_b_im", "s5_c_re", "s5_c_im",
               "s5_d", "s5_b_glu", "s5_norm", "ssd_conv_w", "ssd_conv_b", "ssd_dt_bias", "ssd_a_log", "ssd_d",
               "ssd_norm", "norm_ffn")
WEIGHTS = ("norm_mix", "w_in", "s5_lam_re", "s5_lam_im", "s5_log_step", "s5_b_re", "s5_b_im", "s5_c_re", "s5_c_im",
           "s5_d", "s5_w_glu", "s5_b_glu", "s5_norm", "ssd_conv_w", "ssd_conv_b", "ssd_dt_bias", "ssd_a_log",
           "ssd_d", "ssd_norm", "w_out", "norm_ffn", "w_gate", "w_up", "w_down", "norm_final")


S5_BC = ("s5_b_re", "s5_b_im", "s5_c_re", "s5_c_im")
TINY = tuple(k for k in LAYER_SMALL if k not in S5_BC)


def _my_place():
    return lax.axis_index("x"), lax.axis_index("y"), lax.axis_index("c")


HBM = pl.BlockSpec(memory_space=pltpu.HBM)
SEM = pl.BlockSpec(memory_space=pltpu.SEMAPHORE)
DATAFLOW = pltpu.SideEffectType.DATAFLOW_SIDE_EFFECTING


def _in_hbm(a):
    return pltpu.with_memory_space_constraint(a, pltpu.HBM)


TOKEN = jax.ShapeDtypeStruct((8, LANES), F32)
VMEM_SPEC = pl.BlockSpec(memory_space=pltpu.VMEM)


def _gather_start(blocks, after, name):
    nt = len(blocks)

    def body(*refs):
        ins = refs[:nt]
        lands = refs[nt:2 * nt]
        send_sems, recv_sems = refs[2 * nt + 1:2 * nt + 3]
        refs[-1][...] = jnp.zeros(TOKEN.shape, F32)
        x, y, c = _my_place()
        me = 4 * x + 2 * y + c
        peers = [(x, y, 1 - c), (1 - x, y, c), (x, 1 - y, c), (1 - x, 1 - y, c)]
        for t in range(nt):
            for k, peer in enumerate(peers):
                pltpu.make_async_remote_copy(src_ref=ins[t], dst_ref=lands[t].at[me], send_sem=send_sems.at[4 * t + k],
                                             recv_sem=recv_sems.at[4 * t + k], device_id=peer,
                                             device_id_type=MESH).start()

    lands = [_in_hbm(lax.empty((8,) + b.shape, b.dtype)) for b in blocks]
    out = pl.pallas_call(
        body, name=name, in_specs=[HBM] * (2 * nt) + [ANY],
        out_shape=(pltpu.SemaphoreType.DMA((4 * nt,)), pltpu.SemaphoreType.DMA((4 * nt,)),
                   *[pltpu.HBM(b.shape, b.dtype) for b in blocks],
                   *[pltpu.HBM((8,) + b.shape, b.dtype) for b in blocks], TOKEN),
        out_specs=(SEM, SEM, *[HBM] * (2 * nt), VMEM_SPEC),
        input_output_aliases={i: 2 + i for i in range(2 * nt)},
        compiler_params=pltpu.CompilerParams(has_side_effects=DATAFLOW),
    )(*[_in_hbm(b) for b in blocks], *lands, after)
    return out[:2], list(out[2:2 + nt]), list(out[2 + nt:2 + 2 * nt]), out[-1]


def _gather_forward(sems, blocks, lands, after, name):
    nt = len(blocks)

    def body(*refs):
        ins = refs[:nt]
        lands_in = refs[nt:2 * nt]
        send1, recv1 = refs[2 * nt:2 * nt + 2]
        send2, recv2 = refs[2 * nt + 3:2 * nt + 5]
        x, y, c = _my_place()
        me = 4 * x + 2 * y + c
        sibling = (x, y, 1 - c)
        sources = [4 * x + 2 * y + (1 - c), 4 * (1 - x) + 2 * y + c, 4 * x + 2 * (1 - y) + c,
                   4 * (1 - x) + 2 * (1 - y) + c]
        for t in range(nt):
            for k, src in enumerate(sources):
                cp = pltpu.make_async_remote_copy(src_ref=ins[t], dst_ref=lands_in[t].at[src],
                                                  send_sem=send1.at[4 * t + k], recv_sem=recv1.at[4 * t + k],
                                                  device_id=sibling, device_id_type=MESH)
                cp.wait_send()
                cp.wait_recv()
            for k, src in enumerate(sources[1:]):
                pltpu.make_async_remote_copy(src_ref=lands_in[t].at[src], dst_ref=lands_in[t].at[src],
                                             send_sem=send2.at[3 * t + k], recv_sem=recv2.at[3 * t + k],
                                             device_id=sibling, device_id_type=MESH).start()

    out = pl.pallas_call(
        body, name=name, in_specs=[HBM] * (2 * nt) + [SEM, SEM, pl.BlockSpec(memory_space=pl.ANY)],
        out_shape=(pltpu.SemaphoreType.DMA((3 * nt,)), pltpu.SemaphoreType.DMA((3 * nt,)),
                   *[pltpu.HBM(b.shape, b.dtype) for b in blocks],
                   *[pltpu.HBM(a.shape, a.dtype) for a in lands]),
        out_specs=(SEM, SEM, *[HBM] * (2 * nt)),
        input_output_aliases={i: 2 + i for i in range(2 * nt)},
        compiler_params=pltpu.CompilerParams(has_side_effects=DATAFLOW),
    )(*blocks, *lands, *sems, after)
    return out[:2], list(out[2 + nt:])


def _gather_finish(sems, lands, after, name):
    nt = len(lands)

    def body(*refs):
        lands_in = refs[:nt]
        send2, recv2 = refs[nt:nt + 2]
        x, y, c = _my_place()
        sibling = (x, y, 1 - c)
        mine = [4 * (1 - x) + 2 * y + c, 4 * x + 2 * (1 - y) + c, 4 * (1 - x) + 2 * (1 - y) + c]
        theirs = [4 * (1 - x) + 2 * y + 1 - c, 4 * x + 2 * (1 - y) + 1 - c, 4 * (1 - x) + 2 * (1 - y) + 1 - c]
        for t in range(nt):
            for k in range(3):
                cp = pltpu.make_async_remote_copy(src_ref=lands_in[t].at[mine[k]], dst_ref=lands_in[t].at[theirs[k]],
                                                  send_sem=send2.at[3 * t + k], recv_sem=recv2.at[3 * t + k],
                                                  device_id=sibling, device_id_type=MESH)
                cp.wait_send()
                cp.wait_recv()

    out = pl.pallas_call(
        body, name=name, in_specs=[HBM] * nt + [SEM, SEM, pl.BlockSpec(memory_space=pl.ANY)],
        out_shape=tuple(pltpu.HBM(a.shape, a.dtype) for a in lands), out_specs=tuple([HBM] * nt),
        input_output_aliases={i: i for i in range(nt)},
        compiler_params=pltpu.CompilerParams(has_side_effects=DATAFLOW),
    )(*lands, *sems, after)
    return list(out)


def _other_chips():
    x, y, _ = _my_place()
    return [(1 - x, y), (x, 1 - y), (1 - x, 1 - y)]


def _scatter_start(chunks, name):
    nt = len(chunks)

    def body(*refs):
        ins = refs[:nt]
        lands = refs[nt:2 * nt]
        send_sems, recv_sems = refs[2 * nt:2 * nt + 2]
        refs[-1][...] = jnp.zeros(TOKEN.shape, F32)
        x, y, c = _my_place()
        for t in range(nt):
            for j, (px, py) in enumerate(_other_chips()):
                pltpu.make_async_remote_copy(src_ref=ins[t].at[2 * px + py], dst_ref=lands[t].at[2 * x + y],
                                             send_sem=send_sems.at[3 * t + j], recv_sem=recv_sems.at[3 * t + j],
                                             device_id=(px, py, c), device_id_type=MESH).start()

    lands = [_in_hbm(lax.empty(a.shape, a.dtype)) for a in chunks]
    out = pl.pallas_call(
        body, name=name, in_specs=[HBM] * (2 * nt),
        out_shape=(pltpu.SemaphoreType.DMA((3 * nt,)), pltpu.SemaphoreType.DMA((3 * nt,)),
                   *[pltpu.HBM(a.shape, a.dtype) for a in chunks] * 2, TOKEN),
        out_specs=(SEM, SEM, *[HBM] * (2 * nt), VMEM_SPEC),
        input_output_aliases={i: 2 + i for i in range(2 * nt)},
        compiler_params=pltpu.CompilerParams(has_side_effects=DATAFLOW),
    )(*[_in_hbm(a) for a in chunks], *lands)
    return out[:2], list(out[2:2 + nt]), list(out[2 + nt:2 + 2 * nt]), out[-1]


def _scatter_finish(sems, chunks, lands, after, name):
    nt = len(chunks)

    def body(*refs):
        ins = refs[:nt]
        lands_in = refs[nt:2 * nt]
        send_sems, recv_sems = refs[2 * nt:2 * nt + 2]
        _, _, c = _my_place()
        for t in range(nt):
            for j, (px, py) in enumerate(_other_chips()):
                cp = pltpu.make_async_remote_copy(src_ref=ins[t].at[2 * px + py], dst_ref=lands_in[t].at[2 * px + py],
                                                  send_sem=send_sems.at[3 * t + j], recv_sem=recv_sems.at[3 * t + j],
                                                  device_id=(px, py, c), device_id_type=MESH)
                cp.wait_send()
                cp.wait_recv()

    out = pl.pallas_call(
        body, name=name, in_specs=[HBM] * (2 * nt) + [SEM, SEM, ANY],
        out_shape=tuple(pltpu.HBM(a.shape, a.dtype) for a in lands), out_specs=tuple([HBM] * nt),
        input_output_aliases={nt + i: i for i in range(nt)},
        compiler_params=pltpu.CompilerParams(has_side_effects=DATAFLOW),
    )(*chunks, *lands, *sems, after)
    return list(out)


def _swap_start(views, name):
    nt = len(views)

    def body(*refs):
        ins = refs[:nt]
        lands = refs[nt:2 * nt]
        send_sems, recv_sems = refs[2 * nt:2 * nt + 2]
        refs[-1][...] = jnp.zeros(TOKEN.shape, F32)
        x, y, c = _my_place()
        for t in range(nt):
            pltpu.make_async_remote_copy(
                src_ref=ins[t].at[pl.ds(0, views[t].shape[0]), pl.ds(1 - c, 1)], dst_ref=lands[t],
                send_sem=send_sems.at[t], recv_sem=recv_sems.at[t], device_id=(x, y, 1 - c),
                device_id_type=MESH).start()

    shapes = [(a.shape[0], 1) + a.shape[2:] for a in views]
    lands = [_in_hbm(lax.empty(s, a.dtype)) for s, a in zip(shapes, views)]
    out = pl.pallas_call(
        body, name=name, in_specs=[HBM] * (2 * nt),
        out_shape=(pltpu.SemaphoreType.DMA((nt,)), pltpu.SemaphoreType.DMA((nt,)),
                   *[pltpu.HBM(a.shape, a.dtype) for a in views],
                   *[pltpu.HBM(s, a.dtype) for s, a in zip(shapes, views)], TOKEN),
        out_specs=(SEM, SEM, *[HBM] * (2 * nt), VMEM_SPEC),
        input_output_aliases={i: 2 + i for i in range(2 * nt)},
        compiler_params=pltpu.CompilerParams(has_side_effects=DATAFLOW),
    )(*[_in_hbm(a) for a in views], *lands)
    return out[:2], list(out[2:2 + nt]), list(out[2 + nt:2 + 2 * nt]), out[-1]


def _swap_finish(sems, views, lands, after, name):
    nt = len(views)

    def body(*refs):
        ins = refs[:nt]
        lands_in = refs[nt:2 * nt]
        send_sems, recv_sems = refs[2 * nt:2 * nt + 2]
        x, y, c = _my_place()
        for t in range(nt):
            cp = pltpu.make_async_remote_copy(
                src_ref=ins[t].at[pl.ds(0, views[t].shape[0]), pl.ds(1 - c, 1)], dst_ref=lands_in[t],
                send_sem=send_sems.at[t], recv_sem=recv_sems.at[t], device_id=(x, y, 1 - c), device_id_type=MESH)
            cp.wait_send()
            cp.wait_recv()

    out = pl.pallas_call(
        body, name=name, in_specs=[HBM] * (2 * nt) + [SEM, SEM, ANY],
        out_shape=tuple(pltpu.HBM(a.shape, a.dtype) for a in lands), out_specs=tuple([HBM] * nt),
        input_output_aliases={nt + i: i for i in range(nt)},
        compiler_params=pltpu.CompilerParams(has_side_effects=DATAFLOW),
    )(*views, *lands, *sems, after)
    return list(out)


def _swap_halves(views, name):
    nt = len(views)

    def body(*refs):
        ins = refs[:nt]
        outs = refs[nt:2 * nt]
        send_sems, recv_sems = refs[2 * nt:]
        x, y, c = _my_place()
        copies = [pltpu.make_async_remote_copy(
            src_ref=ins[t].at[pl.ds(0, views[t].shape[0]), pl.ds(1 - c, 1)], dst_ref=outs[t],
            send_sem=send_sems.at[t], recv_sem=recv_sems.at[t], device_id=(x, y, 1 - c), device_id_type=MESH)
            for t in range(nt)]
        for cp in copies:
            cp.start()
        for cp in copies:
            cp.wait()

    return pl.pallas_call(
        body, name=name, in_specs=[ANY] * nt, out_specs=[ANY] * nt,
        out_shape=[jax.ShapeDtypeStruct((a.shape[0], 1) + a.shape[2:], a.dtype) for a in views],
        scratch_shapes=[pltpu.SemaphoreType.DMA((nt,)), pltpu.SemaphoreType.DMA((nt,))],
    )(*views)


def _pair_add_halves(view, recv, name):
    n, _, rows, cols = view.shape
    tile = _row_tile(rows, cols, 4)

    def body(a0_ref, a1_ref, r_ref, o_ref):
        mine = jnp.where(lax.axis_index("c") == 0, a0_ref[...], a1_ref[...])
        o_ref[...] = (mine.astype(F32) + r_ref[...].astype(F32)).astype(o_ref.dtype)

    half = lambda h: pl.BlockSpec((None, None, tile, cols), lambda p, i: (p, h, i, 0))
    return pl.pallas_call(
        body, name=name, grid=(n, rows // tile), in_specs=[half(0), half(1), half(0)],
        out_specs=pl.BlockSpec((None, tile, cols), lambda p, i: (p, i, 0)),
        out_shape=jax.ShapeDtypeStruct((n, rows, cols), view.dtype))(view, view, recv)


def _sum_chunks(lands, chunks, order, name):
    _, rows, cols = chunks.shape
    tile = _row_tile(rows, cols, 5)

    def body(order_ref, l0_ref, l1_ref, l2_ref, own_ref, o_ref):
        o_ref[...] = ((l0_ref[...].astype(F32) + l1_ref[...].astype(F32)) + l2_ref[...].astype(F32)
                      + own_ref[...].astype(F32))

    slot = lambda j: pl.BlockSpec((None, tile, cols), lambda i, order_ref: (order_ref[j], i, 0))
    grid_spec = pltpu.PrefetchScalarGridSpec(
        num_scalar_prefetch=1, grid=(rows // tile,), in_specs=[slot(0), slot(1), slot(2), slot(3)],
        out_specs=pl.BlockSpec((tile, cols), lambda i, order_ref: (i, 0)))
    return pl.pallas_call(body, name=name, grid_spec=grid_spec,
                          out_shape=jax.ShapeDtypeStruct((rows, cols), F32))(order, lands, lands, lands, chunks)


def _adamw_layer(w, g_mine, g_sibling, m, v, layer, prev, name):
    depth, rows, cols = w.shape
    half = rows // 2
    tile = _row_tile(half, cols, 10)
    tiles = half // tile

    def body(w_ref, gm_ref, gs_ref, m_ref, v_ref, *rest):
        d_ref, nm_ref, nv_ref, go_ref = rest[-4:]
        gv = jnp.where(pl.program_id(0) == lax.axis_index("c"), gm_ref[...], gs_ref[...])
        d_ref[...], nm_ref[...], nv_ref[...] = _adamw_math(w_ref[...], gv, m_ref[...], v_ref[...])
        go_ref[...] = gv

    spec = pl.BlockSpec((None, tile, cols), lambda h, i: (layer, h * tiles + i, 0))
    gspec = pl.BlockSpec((tile, cols), lambda h, i: (i, 0))
    shape = jax.ShapeDtypeStruct((depth, rows, cols), F32)
    extra = list(prev)
    aliases = {5 + j: j for j in range(4)} if len(extra) == 4 else {}
    return pl.pallas_call(
        body, name=name, grid=(2, tiles), in_specs=[spec, gspec, gspec, spec, spec] + [ANY] * len(extra),
        out_specs=[spec] * 4, out_shape=[shape] * 4, input_output_aliases=aliases)(w, g_mine, g_sibling, m, v, *extra)


def _sibling_swap(arrs, name):
    nt = len(arrs)

    def body(*refs):
        ins = refs[:nt]
        outs = refs[nt:2 * nt]
        send_sems, recv_sems = refs[2 * nt:]
        x, y, c = _my_place()
        copies = [pltpu.make_async_remote_copy(src_ref=ins[t], dst_ref=outs[t], send_sem=send_sems.at[t],
                                               recv_sem=recv_sems.at[t], device_id=(x, y, 1 - c), device_id_type=MESH)
                  for t in range(nt)]
        for cp in copies:
            cp.start()
        for cp in copies:
            cp.wait()

    return pl.pallas_call(
        body, name=name, in_specs=[ANY] * nt, out_specs=[ANY] * nt,
        out_shape=[jax.ShapeDtypeStruct(a.shape, a.dtype) for a in arrs],
        scratch_shapes=[pltpu.SemaphoreType.DMA((nt,)), pltpu.SemaphoreType.DMA((nt,))],
    )(*arrs)


STREAM_VMEM_BYTES = 32 * 1024 * 1024
SUBLANES = 8


def _row_tile(rows, cols, n_arrays):
    lanes = -(-cols // LANES) * LANES
    for t in range(min(rows, 512), SUBLANES - 1, -1):
        if rows % t == 0 and t % SUBLANES == 0 and 2 * n_arrays * t * lanes * 4 <= STREAM_VMEM_BYTES:
            return t
    return rows


def _sum_leading(a, name):
    n, rows, cols = a.shape
    tile = _row_tile(rows, cols, n + 1)

    def body(a_ref, o_ref):
        acc = a_ref[0].astype(F32)
        for k in range(1, n):
            acc = acc + a_ref[k].astype(F32)
        o_ref[...] = acc

    return pl.pallas_call(
        body, name=name, grid=(rows // tile,), in_specs=[pl.BlockSpec((n, tile, cols), lambda i: (0, i, 0))],
        out_specs=pl.BlockSpec((tile, cols), lambda i: (i, 0)),
        out_shape=jax.ShapeDtypeStruct((rows, cols), F32))(a)


def _adamw_math(w, g, m, v):
    mn = ADAM_B1 * m + (1.0 - ADAM_B1) * g
    vn = ADAM_B2 * v + (1.0 - ADAM_B2) * jnp.square(g)
    m_hat = mn / (1.0 - ADAM_B1 ** ADAM_STEP)
    v_hat = vn / (1.0 - ADAM_B2 ** ADAM_STEP)
    delta = -ADAM_LR * (m_hat / (jnp.sqrt(v_hat) + ADAM_EPS) + ADAM_WD * w)
    return delta, mn, vn


def _adamw_rows(w, g, m, v, name):
    depth, rows, cols = w.shape
    tile = _row_tile(rows, cols, 7)

    def body(w_ref, g_ref, m_ref, v_ref, d_ref, nm_ref, nv_ref):
        d_ref[...], nm_ref[...], nv_ref[...] = _adamw_math(w_ref[...], g_ref[...], m_ref[...], v_ref[...])

    spec = pl.BlockSpec((None, tile, cols), lambda l, i: (l, i, 0))
    shape = jax.ShapeDtypeStruct((depth, rows, cols), F32)
    return pl.pallas_call(body, name=name, grid=(depth, rows // tile), in_specs=[spec] * 4, out_specs=[spec] * 3,
                          out_shape=[shape] * 3)(w, g, m, v)


def _adamw_many(ws, gs, ms, vs, name):
    nt = len(ws)

    def body(*refs):
        for t in range(nt):
            w_ref, g_ref, m_ref, v_ref = (refs[k * nt + t] for k in range(4))
            d_ref, nm_ref, nv_ref = (refs[(4 + k) * nt + t] for k in range(3))
            d_ref[...], nm_ref[...], nv_ref[...] = _adamw_math(w_ref[...], g_ref[...], m_ref[...], v_ref[...])

    shapes = [jax.ShapeDtypeStruct(a.shape, F32) for a in ws]
    out = pl.pallas_call(body, name=name, out_shape=shapes * 3)(*ws, *gs, *ms, *vs)
    return out[:nt], out[nt:2 * nt], out[2 * nt:]


TINY_ROWS_MULTIPLE = 128


def _flat_pack(arrs):
    flat = jnp.concatenate([a.reshape(-1) for a in arrs])
    pad = (-flat.shape[0]) % (TINY_ROWS_MULTIPLE * LANES)
    return jnp.pad(flat, (0, pad)).reshape(-1, LANES)


def _flat_unpack(buf, shapes):
    flat = buf.reshape(-1)
    out = []
    off = 0
    for shp in shapes:
        n = math.prod(shp)
        out.append(flat[off:off + n].reshape(shp))
        off += n
    return out


def kernel(x, norm_mix, w_in, s5_lam_re, s5_lam_im, s5_log_step, s5_b_re, s5_b_im, s5_c_re, s5_c_im, s5_d, s5_w_glu, s5_b_glu, s5_norm, ssd_conv_w, ssd_conv_b, ssd_dt_bias, ssd_a_log, ssd_d, ssd_norm, w_out, norm_ffn, w_gate, w_up, w_down, norm_final, loss_target, m_norm_mix, m_w_in, m_s5_lam_re, m_s5_lam_im, m_s5_log_step, m_s5_b_re, m_s5_b_im, m_s5_c_re, m_s5_c_im, m_s5_d, m_s5_w_glu, m_s5_b_glu, m_s5_norm, m_ssd_conv_w, m_ssd_conv_b, m_ssd_dt_bias, m_ssd_a_log, m_ssd_d, m_ssd_norm, m_w_out, m_norm_ffn, m_w_gate, m_w_up, m_w_down, m_norm_final, v_norm_mix, v_w_in, v_s5_lam_re, v_s5_lam_im, v_s5_log_step, v_s5_b_re, v_s5_b_im, v_s5_c_re, v_s5_c_im, v_s5_d, v_s5_w_glu, v_s5_b_glu, v_s5_norm, v_ssd_conv_w, v_ssd_conv_b, v_ssd_dt_bias, v_ssd_a_log, v_ssd_d, v_ssd_norm, v_w_out, v_norm_ffn, v_w_gate, v_w_up, v_w_down, v_norm_final):
    args = dict(locals())
    w = {k: args[k] for k in WEIGHTS}
    m = {k: args["m_" + k] for k in WEIGHTS}
    v = {k: args["v_" + k] for k in WEIGHTS}
    cx, cy, cc = _my_place()
    chip = 2 * cx + cy

    me = 4 * cx + 2 * cy + cc
    others = _other_chips()
    chunk_order = jnp.stack([2 * px + py for px, py in others] + [chip]).astype(jnp.int32)
    stored = lambda k, a: jnp.swapaxes(a, 1, 2) if k in T_STORED else a

    def my_half(k, layer):
        a = stored(k, w[k])[layer]
        return lax.dynamic_slice_in_dim(a, cc * (a.shape[0] // 2), a.shape[0] // 2, 0).astype(BF16)

    def assemble(names, lands, blocks):
        full = {}
        for k, a, b in zip(names, lands, blocks):
            a = lax.dynamic_update_index_in_dim(a, b, me, 0)
            a = a.reshape(4, 2 * a.shape[1], a.shape[2])
            if k in COL_SHARDED:
                full[k] = _pad_in_proj(a.transpose(1, 0, 2).reshape(a.shape[1], 4 * a.shape[2]))
            else:
                full[k] = a.reshape(4 * a.shape[1], a.shape[2])
        return full

    conv_block = w["ssd_conv_w"].reshape(DEPTH * SSD_CONV, -1)
    first = [my_half("w_in", 0), conv_block]
    second_names = ("s5_w_glu", "w_out")
    second = [my_half(k, 0) for k in second_names]
    ffn0 = [my_half(k, 0) for k in FFN_BIG]
    blocks1 = [my_half(k, 1) for k in BIG]
    sems_a, kept_a, lands_a, token = _gather_start(first, x, "gather0a_start")
    sems_c, kept_c, lands_c, token = _gather_start(second, token, "gather0c_start")
    sems_b, kept_b, lands_b, token = _gather_start(ffn0, token, "gather0b_start")
    sems1, kept1, lands1, token = _gather_start(blocks1, token, "gather1_start")
    sems_a, lands_a = _gather_forward(sems_a, kept_a, lands_a, token, "gather0a_forward")
    lands_a = _gather_finish(sems_a, lands_a, token, "gather0a_finish")
    big0 = assemble(("w_in",), lands_a, first)
    conv_rows = lax.dynamic_update_index_in_dim(lands_a[-1], conv_block, me, 0)
    conv_full = conv_rows.reshape(4, 2, DEPTH, SSD_CONV, -1)[:, 0].transpose(1, 2, 0, 3).reshape(
        DEPTH, SSD_CONV, SSD_CONV_DIM)
    small = {k: w[k] for k in LAYER_SMALL}
    small["ssd_conv_w"] = conv_full
    p0 = {k: a[0] for k, a in small.items()}
    p1 = {k: a[1] for k, a in small.items()}

    p0["norm_mix"] = p0["norm_mix"] + token[0, 0]
    pending = {}

    def pass_on_ffn0(u):
        pending["second"] = _gather_forward(sems_c, kept_c, lands_c, u, "gather0c_forward")
        pending["ffn0"] = _gather_forward(sems_b, kept_b, lands_b, pending["second"][1][0], "gather0b_forward")

    def second_matrices(ys):
        sems, lands = pending["second"]
        return assemble(second_names, _gather_finish(sems, lands, ys, "gather0c_finish"), second)

    def ffn0_matrices(x1):
        sems, lands = pending["ffn0"]
        lands = _gather_finish(sems, lands, x1, "gather0b_finish")
        pending["layer1"] = _gather_forward(sems1, kept1, lands1, lands[0], "gather1_forward")
        return assemble(FFN_BIG, lands, ffn0)

    h1, saved0 = _layer_forward(x[0], p0, big0, 0, pass_on_ffn0, second_matrices, ffn0_matrices)
    big0 = {**big0, **saved0["late_matrices"]}
    sems1, lands1 = pending["layer1"]
    lands1 = _gather_finish(sems1, lands1, h1, "gather1_finish")
    big1 = assemble(BIG, lands1, blocks1)
    h2, saved1 = _layer_forward(h1, p1, big1, 1)
    loss_row, dx, dxb, g_final = _final_loss(h2, w["norm_final"].reshape(1, -1), loss_target[0], "final_loss")
    loss_part, g_final = loss_row[0, 0], g_final[0]

    def halves_view(k, a):
        if k in COL_SHARDED:
            return a.reshape(1, 2, a.shape[0] // 2, a.shape[1])
        return a.reshape(4, 2, a.shape[0] // 8, a.shape[1])

    def to_chunks(k, part):
        if k in COL_SHARDED:
            a = _unpad_in_proj(part[0])
            return a.reshape(a.shape[0], 4, a.shape[1] // 4).transpose(1, 0, 2)
        return part.reshape(4, -1, part.shape[-1])

    def reduce_begin(names, views, tag):
        recv = _swap_halves(views, tag + "swap")
        parts = [_pair_add_halves(a, r, tag + "pair_" + k) for k, a, r in zip(names, views, recv)]
        chunks = [to_chunks(k, p) for k, p in zip(names, parts)]
        return _scatter_start(chunks, tag + "scatter_start")

    def reduce_end(names, handle, after, tag):
        sems, kept, lands, _ = handle
        lands = _scatter_finish(sems, kept, lands, after, tag + "scatter_finish")
        return [_sum_chunks(a, b, chunk_order, tag + "sum_" + k) for k, a, b in zip(names, lands, kept)]

    def swap_begin(names, views, tag):
        return (names, *_swap_start(views, tag + "swap_start"))

    def scatter_begin(handle, after, tag):
        names, sems, views, lands, _ = handle
        recv = _swap_finish(sems, views, lands, after, tag + "swap_finish")
        parts = [_pair_add_halves(a, r, tag + "pair_" + k) for k, a, r in zip(names, views, recv)]
        return _scatter_start([to_chunks(k, p) for k, p in zip(names, parts)], tag + "scatter_start")

    dx, dxb, g1 = _layer_backward(dx, dxb, p1, big1, saved1, 1)
    swap1 = swap_begin(BIG, [halves_view(k, g1[k]) for k in BIG], "grad1_")
    p0["norm_ffn"] = p0["norm_ffn"] + swap1[4][0, 0]

    early = FFN_BIG + ("w_out",)
    middle = ("s5_w_glu", "s5_bc")
    bc_rows = 2 * DEPTH * S5_GROUP * S5_GROUPS

    def send_early(g_so_far):
        pending["round1"] = scatter_begin(swap1, g_so_far["w_out"], "grad1_")
        pending["swap_early"] = swap_begin(early, [halves_view(k, g_so_far[k]) for k in early], "grad0a_")
        return pending["swap_early"][4]

    def send_middle(g_so_far):
        pending["early"] = scatter_begin(pending["swap_early"], g_so_far["s5_w_glu"], "grad0a_")
        rows = lambda names: jnp.stack([a for layer in (g_so_far, g1) for a in (layer[names[0]], layer[names[1]])]
                                       ).reshape(bc_rows, S5_STATE)
        bc = jnp.stack([rows(("s5_b_re", "s5_b_im")), rows(("s5_c_re", "s5_c_im"))])[None]
        pending["swap_middle"] = swap_begin(middle, [halves_view("s5_w_glu", g_so_far["s5_w_glu"]), bc], "grad0b_")
        return pending["swap_middle"][4]

    grad_x, _, g0 = _layer_backward(dx, dxb, p0, big0, saved0, 0, send_early, send_middle)
    g = {k: [g0[k], g1[k]] for k in LAYER_SMALL}
    pending["middle"] = scatter_begin(pending["swap_middle"], grad_x, "grad0b_")
    reduced1 = dict(zip(BIG, reduce_end(BIG, pending["round1"], grad_x, "grad1_")))
    shared1 = dict(zip(BIG, _sibling_swap([reduced1[k] for k in BIG], "grad1_share")))
    round0 = reduce_begin(("w_in",), [halves_view("w_in", g0["w_in"])], "grad0c_")

    delta, new_m, new_v, grads = {}, {}, {}, {}
    adam1 = {}
    layered = tuple(k for k in BIG if k not in COL_SHARDED)
    for k in layered:
        adam1[k] = _adamw_layer(stored(k, w[k]), reduced1[k], shared1[k], stored(k, m[k]), stored(k, v[k]), 1,
                                [round0[3]], "adamw1_" + k)
    follow = adam1[layered[-1]][0]
    reduced0 = dict(zip(early, reduce_end(early, pending["early"], follow, "grad0a_")))
    reduced0.update(zip(middle, reduce_end(middle, pending["middle"], follow, "grad0b_")))
    tiny_names = TINY + ("norm_final",)
    parts = [jnp.stack(g[k]) for k in TINY] + [g_final, loss_part.reshape(1)]
    shapes = [p.shape for p in parts]
    small_blocks = [_flat_pack(parts), reduced0["s5_bc"]]
    small_sems, small_kept, small_lands, small_token = _gather_start(small_blocks, follow, "gather_small_start")
    reduced0.update(zip(("w_in",), reduce_end(("w_in",), round0, small_token, "grad0c_")))
    shared0 = dict(zip(BIG, _sibling_swap([reduced0[k] for k in BIG], "grad0_share")))
    for k in layered:
        outs = _adamw_layer(stored(k, w[k]), reduced0[k], shared0[k], stored(k, m[k]), stored(k, v[k]), 0, adam1[k],
                            "adamw0_" + k)
        delta[k], new_m[k], new_v[k], grads[k] = (stored(k, a) for a in outs)
    both = lambda mine, sib: jnp.where(cc == 0, jnp.concatenate([mine, sib]), jnp.concatenate([sib, mine]))
    grads["w_in"] = jnp.stack([both(reduced0["w_in"], shared0["w_in"]), both(reduced1["w_in"], shared1["w_in"])])
    outs = _adamw_rows(*[_lane_dense(a)[None] for a in (w["w_in"], grads["w_in"], m["w_in"], v["w_in"])],
                       "adamw_w_in")
    delta["w_in"], new_m["w_in"], new_v["w_in"] = (_from_lane_dense(a[0]) for a in outs)

    last = delta["w_in"]
    small_sems, small_lands = _gather_forward(small_sems, small_kept, small_lands, last, "gather_small_forward")
    small_lands = _gather_finish(small_sems, small_lands, last, "gather_small_finish")
    allparts, bc_eighths = (lax.dynamic_update_index_in_dim(a, b, me, 0) for a, b in zip(small_lands, small_blocks))
    unpacked = _flat_unpack(_sum_leading(allparts, "sum_small"), shapes)
    loss = unpacked[-1][0]
    grads.update(zip(tiny_names, unpacked[:-1]))
    width = SSD_CONV_DIM // 4
    grads["ssd_conv_w"] = lax.dynamic_slice_in_dim(grads["ssd_conv_w"], chip * width, width, axis=2)
    bc = bc_eighths.reshape(4, 2, bc_rows // 4, S5_STATE)
    b_sum = bc[:, 0].reshape(DEPTH, 2, S5_GROUP, S5_GROUPS, S5_STATE)
    c_sum = bc[:, 1].reshape(DEPTH, 2, S5_GROUPS, S5_GROUP, S5_STATE)
    grads["s5_c_re"] = c_sum[:, 0]
    grads["s5_c_im"] = c_sum[:, 1]

    b_names = ("s5_b_re", "s5_b_im")
    hp = lambda a: a.transpose(0, 1, 3, 2)
    names = tiny_names + ("s5_c_re", "s5_c_im") + b_names
    view = lambda k, a: hp(a) if k in b_names else (a.reshape(1, -1) if a.ndim == 1 else a)
    g_view = {k: view(k, grads[k]) for k in names if k not in b_names}
    g_view.update({k: b_sum[:, j].transpose(0, 2, 1, 3) for j, k in enumerate(b_names)})
    ds, nms, nvs = _adamw_many([view(k, w[k]) for k in names], [g_view[k] for k in names],
                               [view(k, m[k]) for k in names], [view(k, v[k]) for k in names], "adamw_small")
    for k, a, b, c in zip(names, ds, nms, nvs):
        if k in b_names:
            delta[k], new_m[k], new_v[k], grads[k] = hp(a), hp(b), hp(c), hp(g_view[k])
        else:
            delta[k], new_m[k], new_v[k] = (t.reshape(w[k].shape) for t in (a, b, c))

    return (loss, grad_x[None], *[grads[k] for k in WEIGHTS], *[delta[k] for k in WEIGHTS],
            *[new_m[k] for k in WEIGHTS], *[new_v[k] for k in WEIGHTS])
```

```python
import math

import jax
import jax.numpy as jnp
from jax import lax
from jax.experimental import pallas as pl
from jax.experimental.pallas import tpu as pltpu

F32 = jnp.float32
BF16 = jnp.bfloat16
MESH = pl.DeviceIdType.MESH
ANY = pl.BlockSpec(memory_space=pl.ANY)

D_MODEL = 1024
DEPTH = 2
S5_GROUPS = 64
S5_GROUP = 16
S5_STATE = 64
S5_COLS = S5_GROUPS * S5_STATE
S5_TILE_GROUPS = 8
S5_TILES = S5_GROUPS // S5_TILE_GROUPS
S5_TILE_IN = S5_TILE_GROUPS * S5_GROUP
S5_TILE_ST = S5_TILE_GROUPS * S5_STATE
SEGS = 8
SSD_HEADS = 16
SSD_HEAD_DIM = 64
SSD_GROUPS = 2
SSD_GROUP_HEADS = SSD_HEADS // SSD_GROUPS
SSD_STATE = 128
SSD_CONV = 4
SSD_CHUNK = 128
SSD_WIDTH = 1024
SSD_CONV_DIM = SSD_WIDTH + 2 * SSD_GROUPS * SSD_STATE
IN_MAIN = 3584
IN_PAD = IN_MAIN + 2 * 128
FFN = 2816
EPS = 1e-6
LANES = 128
ROW_TILE = 256
WIDE_ROW_TILE = 512

ADAM_LR = 0.001
ADAM_B1 = 0.9
ADAM_B2 = 0.999
ADAM_EPS = 1e-08
ADAM_WD = 0.01
ADAM_STEP = 10


def _sigmoid(x):
    return 1.0 / (1.0 + jnp.exp(-x))


def _silu(x):
    return x * _sigmoid(x)


def _dsilu(x):
    s = _sigmoid(x)
    return s * (1.0 + x * (1.0 - s))


_GELU_K = math.sqrt(2.0 / math.pi)
_GELU_C = 0.044715


def _gelu(x):
    t = jnp.tanh(_GELU_K * (x + _GELU_C * x * x * x))
    return 0.5 * x * (1.0 + t)


def _dgelu(x):
    t = jnp.tanh(_GELU_K * (x + _GELU_C * x * x * x))
    return 0.5 * (1.0 + t) + 0.5 * x * (1.0 - t * t) * _GELU_K * (1.0 + 3.0 * _GELU_C * x * x)


def _softplus(x):
    e = jnp.exp(-jnp.abs(x))
    u = 1.0 + e
    log1p = jnp.where(u == 1.0, e, jnp.log(u) * e / jnp.where(u == 1.0, 1.0, u - 1.0))
    return jnp.maximum(x, 0.0) + log1p


def _rstd(x):
    return lax.rsqrt(jnp.mean(x * x, axis=-1, keepdims=True) + EPS)


def _rms_bwd(x, r, gain, dy):
    dyg = dy * gain
    dx = r * dyg - x * (r * r * r) * jnp.mean(x * dyg, axis=-1, keepdims=True)
    dgain = jnp.sum(dy * x * r, axis=0, keepdims=True)
    return dx, dgain


def _dot(a, b):
    return jnp.dot(a, b, preferred_element_type=F32)


def _dot_nt(a, b):
    return lax.dot_general(a, b, (((1,), (1,)), ((), ())), preferred_element_type=F32)


def _dot_tn(a, b):
    return lax.dot_general(a, b, (((0,), (0,)), ((), ())), preferred_element_type=F32)


def _row_spec(tile, cols):
    return pl.BlockSpec((tile, cols), lambda i: (i, 0))


def _full_spec(shape):
    nd = len(shape)
    return pl.BlockSpec(shape, lambda *_: (0,) * nd)


def _layer_spec(shape, layer, block=0):
    if layer is None:
        return pl.BlockSpec(tuple(shape), lambda *_: (block, 0), pipeline_mode=pl.Buffered(1))
    return pl.BlockSpec((None,) + tuple(shape), lambda *_: (layer, block, 0), pipeline_mode=pl.Buffered(1))


def _acc_rows(ref, val, first):
    @pl.when(first)
    def _():
        ref[...] = val

    @pl.when(jnp.logical_not(first))
    def _():
        ref[...] += val


def _pick_tile(n, cap):
    best = LANES
    for t in range(LANES, cap + 1, LANES):
        if n % t == 0:
            best = t
    return best


def _mm_tn(a, b, name):
    k, m = a.shape
    _, n = b.shape
    tm = _pick_tile(m, 1536)
    tn = _pick_tile(n, 1536)

    def body(a_ref, b_ref, o_ref):
        o_ref[...] = _dot_tn(a_ref[...], b_ref[...]).astype(BF16)

    return pl.pallas_call(
        body, name=name, grid=(n // tn, m // tm),
        in_specs=[pl.BlockSpec((k, tm), lambda j, i: (0, i)), pl.BlockSpec((k, tn), lambda j, i: (0, j))],
        out_specs=pl.BlockSpec((tm, tn), lambda j, i: (i, j)),
        out_shape=jax.ShapeDtypeStruct((m, n), BF16),
    )(a, b)


def _rms_inproj(x, gain, w_pad, layer, name):
    L = x.shape[0]

    def body(x_ref, g_ref, w_ref, u_ref, z_ref, xbc_ref, dt_ref, h_ref):
        xv = x_ref[...]
        h = (xv * _rstd(xv) * g_ref[...]).astype(BF16)
        h_ref[...] = h
        p = _dot(h, w_ref[...])
        u_ref[...] = p[:, :1024]
        z_ref[...] = p[:, 1024:2048]
        xbc_ref[...] = p[:, 2048:IN_MAIN]
        dt_ref[...] = p[:, IN_MAIN:IN_PAD]

    tile = min(L, WIDE_ROW_TILE)
    return pl.pallas_call(
        body, name=name, grid=(L // tile,),
        in_specs=[_row_spec(tile, D_MODEL), _full_spec((1, D_MODEL)), _layer_spec((D_MODEL, IN_PAD), layer)],
        out_specs=[_row_spec(tile, 1024), _row_spec(tile, 1024), _row_spec(tile, SSD_CONV_DIM),
                   _row_spec(tile, 256), _row_spec(tile, D_MODEL)],
        out_shape=[jax.ShapeDtypeStruct((L, 1024), F32), jax.ShapeDtypeStruct((L, 1024), F32),
                   jax.ShapeDtypeStruct((L, SSD_CONV_DIM), F32), jax.ShapeDtypeStruct((L, 256), F32),
                   jax.ShapeDtypeStruct((L, D_MODEL), BF16)],
    )(x, gain, w_pad)


def _s5_prep_math(lr, li, ls, bre, bim):
    step = jnp.exp(ls)
    mag = jnp.exp(lr * step)
    ang = li * step
    are = mag * jnp.cos(ang)
    aim = mag * jnp.sin(ang)
    den = lr * lr + li * li
    nr = are - 1.0
    ni = aim
    cre = (nr * lr + ni * li) / den
    cim = (ni * lr - nr * li) / den
    bbre = cre[None] * bre - cim[None] * bim
    bbim = cre[None] * bim + cim[None] * bre
    return are, aim, bbre, bbim


def _s5_prep(lr, li, ls, bre, bim, name):
    def body(lr_ref, li_ref, ls_ref, bre_ref, bim_ref, are_ref, aim_ref, bbre_ref, bbim_ref):
        are, aim, bbre, bbim = _s5_prep_math(lr_ref[...], li_ref[...], ls_ref[...], bre_ref[...], bim_ref[...])
        are_ref[...] = are
        aim_ref[...] = aim
        bbre_ref[...] = bbre
        bbim_ref[...] = bbim

    gp = jax.ShapeDtypeStruct((S5_GROUPS, S5_STATE), F32)
    hgp = jax.ShapeDtypeStruct((S5_GROUP, S5_GROUPS, S5_STATE), F32)
    return pl.pallas_call(body, name=name, out_shape=[gp, gp, hgp, hgp])(lr, li, ls, bre, bim)


def _s5_prep_bwd(lr, li, ls, bre, bim, dare, daim, dbbre, dbbim, name):
    def body(lr_ref, li_ref, ls_ref, bre_ref, bim_ref, dare_ref, daim_ref, dbbre_ref, dbbim_ref,
             dlr_ref, dli_ref, dls_ref, dbre_ref, dbim_ref):
        _, vjp = jax.vjp(_s5_prep_math, lr_ref[...], li_ref[...], ls_ref[...], bre_ref[...], bim_ref[...])
        dlr, dli, dls, dbre, dbim = vjp((dare_ref[...], daim_ref[...], dbbre_ref[...], dbbim_ref[...]))
        dlr_ref[...] = dlr
        dli_ref[...] = dli
        dls_ref[...] = dls
        dbre_ref[...] = dbre
        dbim_ref[...] = dbim

    gp = jax.ShapeDtypeStruct((S5_GROUPS, S5_STATE), F32)
    g1 = jax.ShapeDtypeStruct((S5_GROUPS, 1), F32)
    hgp = jax.ShapeDtypeStruct((S5_GROUP, S5_GROUPS, S5_STATE), F32)
    return pl.pallas_call(body, name=name, out_shape=[gp, gp, g1, hgp, hgp])(
        lr, li, ls, bre, bim, dare, daim, dbbre, dbbim)


def _cmul_add(ar, ai, sr, si, br, bi):
    return ar * sr - ai * si + br, ar * si + ai * sr + bi


def _shift_rows_down(v):
    rolled = pltpu.roll(v, 1, 0)
    row = lax.broadcasted_iota(jnp.int32, v.shape, 0)
    return jnp.where(row == 0, 0.0, rolled)


def _shift_rows_up(v):
    rolled = pltpu.roll(v, SEGS - 1, 0)
    row = lax.broadcasted_iota(jnp.int32, v.shape, 0)
    return jnp.where(row == SEGS - 1, 0.0, rolled)


def _segment_power(ar, ai, steps):
    n = 1
    while n < steps:
        ar, ai = ar * ar - ai * ai, 2.0 * ar * ai
        n *= 2
    assert n == steps
    return ar, ai


def _half_segment_entries(ar, ai, first, second, half_steps, shift):
    pr, pi = _segment_power(ar, ai, half_steps)
    er = jnp.zeros_like(first[0])
    ei = jnp.zeros_like(first[1])
    for _ in range(SEGS - 1):
        mr, mi = _cmul_add(pr, pi, er, ei, *first)
        nr, ni = _cmul_add(pr, pi, mr, mi, *second)
        er, ei = shift(nr), shift(ni)
    mr, mi = _cmul_add(pr, pi, er, ei, *first)
    return (er, ei), (mr, mi)


def _s5_scan(u_perm, bre_bd, bim_bd, cre_bd, cim_bd, are, aim, name):
    L = u_perm.shape[0]
    half = L // SEGS // 2

    def body(u_ref, bre_ref, bim_ref, cre_ref, cim_ref, are_ref, aim_ref, y_ref, xr_ref, xi_ref):
        u = u_ref[...].astype(BF16)
        xr_ref[...] = _dot(u, bre_ref[0])
        xi_ref[...] = _dot(u, bim_ref[0])
        ar = jnp.broadcast_to(are_ref[0], (SEGS, S5_TILE_ST))
        ai = jnp.broadcast_to(aim_ref[0], (SEGS, S5_TILE_ST))
        zero = jnp.zeros((SEGS, S5_TILE_ST), F32)
        block = lambda j: pl.ds(pl.multiple_of(j * SEGS, SEGS), SEGS)

        def finals(j, c):
            lo, hi = block(j), block(j + half)
            return (*_cmul_add(ar, ai, c[0], c[1], xr_ref[lo, :], xi_ref[lo, :]),
                    *_cmul_add(ar, ai, c[2], c[3], xr_ref[hi, :], xi_ref[hi, :]))

        f = lax.fori_loop(0, half, finals, (zero,) * 4, unroll=4)
        e_lo, e_hi = _half_segment_entries(ar, ai, f[:2], f[2:], half, _shift_rows_down)

        def scan(j, c):
            lo, hi = block(j), block(j + half)
            s_lo = _cmul_add(ar, ai, c[0], c[1], xr_ref[lo, :], xi_ref[lo, :])
            s_hi = _cmul_add(ar, ai, c[2], c[3], xr_ref[hi, :], xi_ref[hi, :])
            xr_ref[lo, :], xi_ref[lo, :] = s_lo
            xr_ref[hi, :], xi_ref[hi, :] = s_hi
            return (*s_lo, *s_hi)

        lax.fori_loop(0, half, scan, (*e_lo, *e_hi), unroll=8)
        y_ref[...] = (_dot(xr_ref[...].astype(BF16), cre_ref[0]) - _dot(xi_ref[...].astype(BF16), cim_ref[0]))

    tile3 = lambda a, b: pl.BlockSpec((1, a, b), lambda k: (k, 0, 0))
    return pl.pallas_call(
        body, name=name, grid=(S5_TILES,),
        in_specs=[pl.BlockSpec((L, S5_TILE_IN), lambda k: (0, k)),
                  tile3(S5_TILE_IN, S5_TILE_ST), tile3(S5_TILE_IN, S5_TILE_ST),
                  tile3(S5_TILE_ST, S5_TILE_IN), tile3(S5_TILE_ST, S5_TILE_IN),
                  tile3(1, S5_TILE_ST), tile3(1, S5_TILE_ST)],
        out_specs=[pl.BlockSpec((L, S5_TILE_IN), lambda k: (0, k)),
                   pl.BlockSpec((L, S5_TILE_ST), lambda k: (0, k)), pl.BlockSpec((L, S5_TILE_ST), lambda k: (0, k))],
        out_shape=[jax.ShapeDtypeStruct((L, 1024), F32), jax.ShapeDtypeStruct((L, S5_COLS), F32),
                   jax.ShapeDtypeStruct((L, S5_COLS), F32)],
    )(u_perm, bre_bd, bim_bd, cre_bd, cim_bd, are, aim)


def _s5_scan_bwd(dy_perm, u_perm, xr, xi, bret_bd, bimt_bd, cret_bd, cimt_bd, are, aim, name):
    L = u_perm.shape[0]
    steps = L // SEGS
    half = steps // 2

    def body(dy_ref, u_ref, xr_ref, xi_ref, bret_ref, bimt_ref, cret_ref, cimt_ref, are_ref, aim_ref,
             du_ref, dar_ref, dai_ref, dcre_ref, dcim_ref, dbre_ref, dbim_ref, gr_ref, gi_ref):
        dy = dy_ref[...].astype(BF16)
        u = u_ref[...].astype(BF16)
        gr_ref[...] = _dot(dy, cret_ref[0])
        gi_ref[...] = -_dot(dy, cimt_ref[0])
        ar = jnp.broadcast_to(are_ref[0], (SEGS, S5_TILE_ST))
        ai = -jnp.broadcast_to(aim_ref[0], (SEGS, S5_TILE_ST))
        zero = jnp.zeros((SEGS, S5_TILE_ST), F32)
        block = lambda j: pl.ds(pl.multiple_of(j * SEGS, SEGS), SEGS)

        def finals(k, c):
            hi, lo = block(steps - 1 - k), block(half - 1 - k)
            return (*_cmul_add(ar, ai, c[0], c[1], gr_ref[hi, :], gi_ref[hi, :]),
                    *_cmul_add(ar, ai, c[2], c[3], gr_ref[lo, :], gi_ref[lo, :]))

        f = lax.fori_loop(0, half, finals, (zero,) * 4, unroll=4)
        e_hi, e_lo = _half_segment_entries(ar, ai, f[:2], f[2:], half, _shift_rows_up)

        def scan(k, c):
            accr, acci = c[4], c[5]
            j_hi, j_lo = steps - 1 - k, half - 1 - k
            hi, lo = block(j_hi), block(j_lo)
            hr, hi_im = _cmul_add(ar, ai, c[0], c[1], gr_ref[hi, :], gi_ref[hi, :])
            lr, lo_im = _cmul_add(ar, ai, c[2], c[3], gr_ref[lo, :], gi_ref[lo, :])
            gr_ref[hi, :], gi_ref[hi, :] = hr, hi_im
            gr_ref[lo, :], gi_ref[lo, :] = lr, lo_im
            before_hi = block(j_hi - 1)
            before_lo = block(jnp.maximum(j_lo - 1, 0))
            live = (j_lo > 0).astype(F32)
            xhr, xhi = xr_ref[before_hi, :], xi_ref[before_hi, :]
            xlr, xli = xr_ref[before_lo, :] * live, xi_ref[before_lo, :] * live
            accr = accr + (hr * xhr + hi_im * xhi) + (lr * xlr + lo_im * xli)
            acci = acci + (hi_im * xhr - hr * xhi) + (lo_im * xlr - lr * xli)
            return hr, hi_im, lr, lo_im, accr, acci

        out = lax.fori_loop(0, half, scan, (*e_hi, *e_lo, zero, zero), unroll=4)
        accr, acci = out[4], out[5]
        first = pl.ds(0, SEGS)
        last = pl.ds((steps - 1) * SEGS, SEGS)
        xpr = _shift_rows_down(xr_ref[last, :])
        xpi = _shift_rows_down(xi_ref[last, :])
        g0r = gr_ref[first, :]
        g0i = gi_ref[first, :]
        accr = accr + g0r * xpr + g0i * xpi
        acci = acci + g0i * xpr - g0r * xpi
        dar_ref[0] = jnp.sum(accr, axis=0, keepdims=True)
        dai_ref[0] = jnp.sum(acci, axis=0, keepdims=True)

        grb = gr_ref[...].astype(BF16)
        gib = gi_ref[...].astype(BF16)
        du_ref[...] = _dot(grb, bret_ref[0]) + _dot(gib, bimt_ref[0])
        dbre_ref[0] = _dot_tn(u, grb)
        dbim_ref[0] = _dot_tn(u, gib)
        dcre_ref[0] = _dot_tn(dy, xr_ref[...].astype(BF16))
        dcim_ref[0] = -_dot_tn(dy, xi_ref[...].astype(BF16))

    tile3 = lambda a, b: pl.BlockSpec((1, a, b), lambda k: (k, 0, 0))
    col_in = pl.BlockSpec((L, S5_TILE_IN), lambda k: (0, k))
    col_st = pl.BlockSpec((L, S5_TILE_ST), lambda k: (0, k))
    dense = jax.ShapeDtypeStruct((S5_TILES, S5_TILE_IN, S5_TILE_ST), F32)
    vec = jax.ShapeDtypeStruct((S5_TILES, 1, S5_TILE_ST), F32)
    return pl.pallas_call(
        body, name=name, grid=(S5_TILES,),
        in_specs=[col_in, col_in, col_st, col_st,
                  tile3(S5_TILE_ST, S5_TILE_IN), tile3(S5_TILE_ST, S5_TILE_IN),
                  tile3(S5_TILE_IN, S5_TILE_ST), tile3(S5_TILE_IN, S5_TILE_ST),
                  tile3(1, S5_TILE_ST), tile3(1, S5_TILE_ST)],
        out_specs=[col_in, tile3(1, S5_TILE_ST), tile3(1, S5_TILE_ST),
                   tile3(S5_TILE_IN, S5_TILE_ST), tile3(S5_TILE_IN, S5_TILE_ST),
                   tile3(S5_TILE_IN, S5_TILE_ST), tile3(S5_TILE_IN, S5_TILE_ST)],
        out_shape=[jax.ShapeDtypeStruct((L, 1024), F32), vec, vec, dense, dense, dense, dense],
        scratch_shapes=[pltpu.VMEM((L, S5_TILE_ST), F32), pltpu.VMEM((L, S5_TILE_ST), F32)],
    )(dy_perm, u_perm, xr, xi, bret_bd, bimt_bd, cret_bd, cimt_bd, are, aim)


def _s5_post(ys, u, d_skip, w_glu, b_glu, gain, layer, name):
    L = ys.shape[0]

    def body(ys_ref, u_ref, d_ref, w_ref, b_ref, g_ref, ya_ref):
        g = _gelu(ys_ref[...] + d_ref[...] * u_ref[...])
        q = _dot(g.astype(BF16), w_ref[...]) + b_ref[...]
        oa = g * _sigmoid(q)
        ya_ref[...] = (oa * _rstd(oa) * g_ref[...]).astype(BF16)

    vec = _full_spec((1, 1024))
    tile = min(L, WIDE_ROW_TILE)
    return pl.pallas_call(
        body, name=name, grid=(L // tile,),
        in_specs=[_row_spec(tile, 1024), _row_spec(tile, 1024), vec, _layer_spec((1024, 1024), layer), vec, vec],
        out_specs=_row_spec(tile, 1024),
        out_shape=jax.ShapeDtypeStruct((L, 1024), BF16),
    )(ys, u, d_skip, w_glu, b_glu, gain)


def _s5_post_bwd(dx, w_out, ys, u, d_skip, w_glu, b_glu, gain, layer, name):
    L = ys.shape[0]

    def body(dx_ref, wo_ref, ys_ref, u_ref, d_ref, w_ref, b_ref, gn_ref,
             dys_ref, dus_ref, g_ref, dq_ref, dgain_ref, dd_ref, db_ref):
        first = pl.program_id(0) == 0
        uv = u_ref[...]
        yt = ys_ref[...] + d_ref[...] * uv
        g = _gelu(yt)
        gb = g.astype(BF16)
        q = _dot(gb, w_ref[...]) + b_ref[...]
        s = _sigmoid(q)
        oa = g * s
        dya = _dot_nt(dx_ref[...], wo_ref[...])
        doa, dgain = _rms_bwd(oa, _rstd(oa), gn_ref[...], dya)
        dq = doa * g * s * (1.0 - s)
        dqb = dq.astype(BF16)
        dg = doa * s + _dot_nt(dqb, w_ref[...])
        dyt = dg * _dgelu(yt)
        dys_ref[...] = dyt
        dus_ref[...] = dyt * d_ref[...]
        g_ref[...] = gb
        dq_ref[...] = dqb
        _acc_rows(dgain_ref, dgain, first)
        _acc_rows(dd_ref, jnp.sum(dyt * uv, axis=0, keepdims=True), first)
        _acc_rows(db_ref, jnp.sum(dq, axis=0, keepdims=True), first)

    vec = _full_spec((1, 1024))
    row = _row_spec(ROW_TILE, 1024)
    vshape = jax.ShapeDtypeStruct((1, 1024), F32)
    return pl.pallas_call(
        body, name=name, grid=(L // ROW_TILE,),
        in_specs=[row, _layer_spec((1024, 1024), layer, 0), row, row, vec, _layer_spec((1024, 1024), layer), vec,
                  vec],
        out_specs=[row, row, row, row, vec, vec, vec],
        out_shape=[jax.ShapeDtypeStruct((L, 1024), F32), jax.ShapeDtypeStruct((L, 1024), F32),
                   jax.ShapeDtypeStruct((L, 1024), BF16), jax.ShapeDtypeStruct((L, 1024), BF16),
                   vshape, vshape, vshape],
    )(dx, w_out, ys, u, d_skip, w_glu, b_glu, gain)


CONV_TILE = 256


def _shift_time(v, d):
    if d == 0:
        return v
    rolled = pltpu.roll(v, d, 0)
    row = lax.broadcasted_iota(jnp.int32, v.shape, 0)
    return jnp.where(row < d, 0.0, rolled)


def _unshift_time(v, d):
    if d == 0:
        return v
    n = v.shape[0]
    rolled = pltpu.roll(v, n - d, 0)
    row = lax.broadcasted_iota(jnp.int32, v.shape, 0)
    return jnp.where(row >= n - d, 0.0, rolled)


def _ssd_conv(xbc, w, b, name):
    L = xbc.shape[0]

    def body(x_ref, w_ref, b_ref, o_ref):
        xv = x_ref[...]
        pre = jnp.broadcast_to(b_ref[...], xv.shape)
        for k in range(SSD_CONV):
            pre = pre + w_ref[k:k + 1, :] * _shift_time(xv, SSD_CONV - 1 - k)
        o_ref[...] = _silu(pre)

    col = pl.BlockSpec((L, CONV_TILE), lambda j: (0, j))
    return pl.pallas_call(
        body, name=name, grid=(SSD_CONV_DIM // CONV_TILE,),
        in_specs=[col, pl.BlockSpec((8, CONV_TILE), lambda j: (0, j)), pl.BlockSpec((1, CONV_TILE), lambda j: (0, j))],
        out_specs=col, out_shape=jax.ShapeDtypeStruct((L, SSD_CONV_DIM), F32),
    )(xbc, w, b)


def _ssd_conv_bwd(dxc, xbc, w, b, name):
    L = xbc.shape[0]

    def body(d_ref, x_ref, w_ref, b_ref, dx_ref, dw_ref, db_ref):
        xv = x_ref[...]
        shifted = [_shift_time(xv, SSD_CONV - 1 - k) for k in range(SSD_CONV)]
        pre = jnp.broadcast_to(b_ref[...], xv.shape)
        for k in range(SSD_CONV):
            pre = pre + w_ref[k:k + 1, :] * shifted[k]
        dpre = d_ref[...] * _dsilu(pre)
        dx = jnp.zeros_like(xv)
        rows = []
        for k in range(SSD_CONV):
            dx = dx + w_ref[k:k + 1, :] * _unshift_time(dpre, SSD_CONV - 1 - k)
            rows.append(jnp.sum(dpre * shifted[k], axis=0, keepdims=True))
        dx_ref[...] = dx
        dw_ref[...] = jnp.concatenate(rows + [jnp.zeros((8 - SSD_CONV, CONV_TILE), F32)], axis=0)
        db_ref[...] = jnp.sum(dpre, axis=0, keepdims=True)

    col = pl.BlockSpec((L, CONV_TILE), lambda j: (0, j))
    w_spec = pl.BlockSpec((8, CONV_TILE), lambda j: (0, j))
    b_spec = pl.BlockSpec((1, CONV_TILE), lambda j: (0, j))
    return pl.pallas_call(
        body, name=name, grid=(SSD_CONV_DIM // CONV_TILE,),
        in_specs=[col, col, w_spec, b_spec], out_specs=[col, w_spec, b_spec],
        out_shape=[jax.ShapeDtypeStruct((L, SSD_CONV_DIM), F32), jax.ShapeDtypeStruct((8, SSD_CONV_DIM), F32),
                   jax.ShapeDtypeStruct((1, SSD_CONV_DIM), F32)],
    )(dxc, xbc, w, b)


def _tri(lower):
    r = lax.broadcasted_iota(jnp.int32, (SSD_CHUNK, SSD_CHUNK), 0)
    c = lax.broadcasted_iota(jnp.int32, (SSD_CHUNK, SSD_CHUNK), 1)
    return (r >= c) if lower else (r <= c)


def _ssd_chunk_common(dt_ref, bias_ref, alog_ref):
    pre = dt_ref[...] + bias_ref[0]
    dtp = _softplus(pre)
    a_neg = -jnp.exp(alog_ref[0])
    dta = dtp * a_neg
    acum = _select_rows(_tri(True), dta)
    return pre, dtp, a_neg, dta, acum


GROUP_W = SSD_GROUP_HEADS * SSD_HEAD_DIM


def _head_expander():
    r = lax.broadcasted_iota(jnp.int32, (LANES, GROUP_W), 0)
    c = lax.broadcasted_iota(jnp.int32, (LANES, GROUP_W), 1)
    return (c // SSD_HEAD_DIM == r).astype(F32)


def _split_bf16(a, terms):
    parts = []
    rest = a
    for _ in range(terms):
        piece = rest.astype(BF16)
        parts.append(piece)
        rest = rest - piece.astype(F32)
    return parts


def _select_cols(a, sel, terms=3):
    lhs = jnp.concatenate(_split_bf16(a, terms), axis=1)
    rhs = jnp.concatenate([sel.astype(BF16)] * terms, axis=0)
    return _dot(lhs, rhs)


def _select_rows(sel, b, terms=3):
    lhs = jnp.concatenate([sel.astype(BF16)] * terms, axis=1)
    rhs = jnp.concatenate(_split_bf16(b, terms), axis=0)
    return _dot(lhs, rhs)


def _decay_mask(acum_all, acum_t, h, lower):
    seg = acum_all[:, h:h + 1] - acum_t[h:h + 1, :]
    return jnp.where(lower, jnp.exp(jnp.minimum(seg, 0.0)), 0.0)


def _ssd_scan(xc, dt, dt_bias, a_log, d_wide, expand, expand_t, name):
    L = xc.shape[0]
    nc = L // SSD_CHUNK

    def body(x_ref, b_ref, c_ref, dt_ref, bias_ref, alog_ref, d_ref, e_ref, et_ref, y_ref, sp_ref, s_ref, xdt_ref):
        @pl.when(pl.program_id(1) == 0)
        def _():
            s_ref[...] = jnp.zeros_like(s_ref)

        _, dtp_all, _, _, acum_all = _ssd_chunk_common(dt_ref, bias_ref, alog_ref)
        acum_t = acum_all.T
        wide = _select_cols(jnp.concatenate([acum_all, dtp_all], axis=0), e_ref[...])
        acum_e = wide[:SSD_CHUNK]
        alast_e = acum_e[SSD_CHUNK - 1:SSD_CHUNK, :]
        x = x_ref[...]
        xdt = x * wide[SSD_CHUNK:]
        xdt_ref[...] = xdt.astype(BF16)
        bm = b_ref[...].astype(BF16)
        cm = c_ref[...].astype(BF16)
        cb = _dot_nt(cm, bm)
        lower = _tri(True)
        sp = s_ref[...]
        for h in range(SSD_GROUP_HEADS):
            cols = slice(h * SSD_HEAD_DIM, (h + 1) * SSD_HEAD_DIM)
            lm = _decay_mask(acum_all, acum_t, h, lower)
            y_ref[:, cols] = _dot((cb * lm).astype(BF16), xdt_ref[:, cols])
        y_ref[...] += _dot_nt(cm, sp.astype(BF16)) * jnp.exp(acum_e) + d_ref[0] * x
        wgt = xdt * jnp.exp(alast_e - acum_e)
        ealast = jnp.exp(_select_rows(et_ref[...], acum_t)[:, SSD_CHUNK - 1:SSD_CHUNK])
        sp_ref[0, 0] = sp
        s_ref[...] = ealast * sp + _dot_tn(wgt.astype(BF16), bm)

    par = lambda n: pl.BlockSpec((1, 1, n), lambda g, c: (g, 0, 0))
    return pl.pallas_call(
        body, name=name, grid=(SSD_GROUPS, nc),
        in_specs=[pl.BlockSpec((SSD_CHUNK, GROUP_W), lambda g, c: (c, g)),
                  pl.BlockSpec((SSD_CHUNK, SSD_STATE), lambda g, c: (c, 8 + g)),
                  pl.BlockSpec((SSD_CHUNK, SSD_STATE), lambda g, c: (c, 10 + g)),
                  pl.BlockSpec((SSD_CHUNK, LANES), lambda g, c: (c, g)),
                  par(LANES), par(LANES), par(GROUP_W), _full_spec((LANES, GROUP_W)), _full_spec((GROUP_W, LANES))],
        out_specs=[pl.BlockSpec((SSD_CHUNK, GROUP_W), lambda g, c: (c, g)),
                   pl.BlockSpec((1, 1, GROUP_W, SSD_STATE), lambda g, c: (c, g, 0, 0))],
        out_shape=[jax.ShapeDtypeStruct((L, SSD_WIDTH), F32),
                   jax.ShapeDtypeStruct((nc, SSD_GROUPS, GROUP_W, SSD_STATE), F32)],
        scratch_shapes=[pltpu.VMEM((GROUP_W, SSD_STATE), F32), pltpu.VMEM((SSD_CHUNK, GROUP_W), BF16)],
    )(xc, xc, xc, dt, dt_bias, a_log, d_wide, expand, expand_t)


def _ssd_scan_bwd(dy, xc, dt, sprev, dt_bias, a_log, d_wide, expand, expand_t, name):
    L = xc.shape[0]
    nc = L // SSD_CHUNK

    def body(dy_ref, x_ref, b_ref, c_ref, dt_ref, sp_ref, bias_ref, alog_ref, d_ref, e_ref, et_ref,
             dx_ref, db_ref, dc_ref, ddt_ref, dbias_ref, dalog_ref, dd_ref, ds_ref, xdt_ref, dyb_ref):
        first = pl.program_id(1) == 0

        @pl.when(first)
        def _():
            ds_ref[...] = jnp.zeros_like(ds_ref)

        pre, dtp_all, a_neg, _, acum_all = _ssd_chunk_common(dt_ref, bias_ref, alog_ref)
        acum_t = acum_all.T
        e = e_ref[...]
        et = et_ref[...]
        wide = _select_cols(jnp.concatenate([acum_all, dtp_all], axis=0), e)
        acum_e = wide[:SSD_CHUNK]
        dtp_e = wide[SSD_CHUNK:]
        alast_e = acum_e[SSD_CHUNK - 1:SSD_CHUNK, :]
        dstate_e = jnp.exp(alast_e - acum_e)
        x = x_ref[...]
        dy = dy_ref[...]
        xdt = x * dtp_e
        xdt_ref[...] = xdt.astype(BF16)
        dyb_ref[...] = dy.astype(BF16)
        bm = b_ref[...].astype(BF16)
        cm = c_ref[...].astype(BF16)
        cb = _dot_nt(cm, bm)
        sp = sp_ref[0, 0]
        spb = sp.astype(BF16)
        dsn = ds_ref[...]
        dsb = dsn.astype(BF16)
        z = _dot_nt(cm, spb)
        dz = dy * jnp.exp(acum_e)
        dzb = dz.astype(BF16)
        dc_acc = _dot(dzb, spb)
        ealast = jnp.exp(_select_rows(et, acum_t)[:, SSD_CHUNK - 1:SSD_CHUNK])
        ds_ref[...] = _dot_tn(dzb, cm) + ealast * dsn
        dw = _dot_nt(bm, dsb)
        wgt = xdt * dstate_e
        db_acc = _dot(wgt.astype(BF16), dsb)
        lower = _tri(True)
        lane = lax.broadcasted_iota(jnp.int32, (SSD_CHUNK, LANES), 1)
        row = lax.broadcasted_iota(jnp.int32, (SSD_CHUNK, LANES), 0)
        dcb = jnp.zeros((SSD_CHUNK, SSD_CHUNK), F32)
        dacum_all = jnp.zeros((SSD_CHUNK, LANES), F32)
        dacum_cols = jnp.zeros((SSD_CHUNK, LANES), F32)
        for h in range(SSD_GROUP_HEADS):
            cols = slice(h * SSD_HEAD_DIM, (h + 1) * SSD_HEAD_DIM)
            lm = _decay_mask(acum_all, acum_t, h, lower)
            dm = _dot_nt(dyb_ref[:, cols], xdt_ref[:, cols])
            dx_ref[:, cols] = _dot_tn((cb * lm).astype(BF16), dyb_ref[:, cols])
            dm_lm = dm * lm
            dcb = dcb + dm_lm
            q = dm_lm * cb
            dacum_all = jnp.where(lane == h, jnp.sum(q, axis=1, keepdims=True), dacum_all)
            dacum_cols = jnp.where(row == h, jnp.sum(q, axis=0, keepdims=True), dacum_cols)
        dxdt = dx_ref[...] + dw * dstate_e
        sums = _select_cols(jnp.concatenate([dz * z, dw * wgt, dxdt * x, dy * x], axis=0), et, terms=2)
        dacum_off = sums[0:SSD_CHUNK]
        dds_ds = sums[SSD_CHUNK:2 * SSD_CHUNK]
        ddtp_x = sums[2 * SSD_CHUNK:3 * SSD_CHUNK]
        dd_part = sums[3 * SSD_CHUNK:4 * SSD_CHUNK]
        ds_s = jnp.sum(_select_rows(e, dsn * sp, terms=2).T, axis=0, keepdims=True)
        dalast = ds_s * jnp.exp(acum_all[SSD_CHUNK - 1:SSD_CHUNK, :]) + jnp.sum(dds_ds, axis=0, keepdims=True)
        dacum_all = dacum_all - dacum_cols.T + dacum_off - dds_ds + jnp.where(row == SSD_CHUNK - 1, dalast, 0.0)
        dx_ref[...] = d_ref[0] * dy + dxdt * dtp_e
        dcbb = dcb.astype(BF16)
        dc_ref[...] = dc_acc + _dot(dcbb, bm)
        db_ref[...] = db_acc + _dot_tn(dcbb, cm)
        ddta = _select_rows(_tri(False), dacum_all)
        ddt = (ddtp_x + ddta * a_neg) * _sigmoid(pre)
        ddt_ref[...] = ddt
        _acc_rows(dbias_ref, jnp.sum(ddt, axis=0, keepdims=True)[None], first)
        _acc_rows(dalog_ref, (jnp.sum(ddta * dtp_all, axis=0, keepdims=True) * a_neg)[None], first)
        _acc_rows(dd_ref, jnp.sum(dd_part, axis=0, keepdims=True)[None], first)

    rev = lambda c: nc - 1 - c
    par = lambda n: pl.BlockSpec((1, 1, n), lambda g, c: (g, 0, 0))
    pshape = jax.ShapeDtypeStruct((SSD_GROUPS, 1, LANES), F32)
    return pl.pallas_call(
        body, name=name, grid=(SSD_GROUPS, nc),
        in_specs=[pl.BlockSpec((SSD_CHUNK, GROUP_W), lambda g, c: (rev(c), g)),
                  pl.BlockSpec((SSD_CHUNK, GROUP_W), lambda g, c: (rev(c), g)),
                  pl.BlockSpec((SSD_CHUNK, SSD_STATE), lambda g, c: (rev(c), 8 + g)),
                  pl.BlockSpec((SSD_CHUNK, SSD_STATE), lambda g, c: (rev(c), 10 + g)),
                  pl.BlockSpec((SSD_CHUNK, LANES), lambda g, c: (rev(c), g)),
                  pl.BlockSpec((1, 1, GROUP_W, SSD_STATE), lambda g, c: (rev(c), g, 0, 0)),
                  par(LANES), par(LANES), par(GROUP_W), _full_spec((LANES, GROUP_W)), _full_spec((GROUP_W, LANES))],
        out_specs=[pl.BlockSpec((SSD_CHUNK, GROUP_W), lambda g, c: (rev(c), g)),
                   pl.BlockSpec((SSD_CHUNK, SSD_STATE), lambda g, c: (rev(c), g)),
                   pl.BlockSpec((SSD_CHUNK, SSD_STATE), lambda g, c: (rev(c), g)),
                   pl.BlockSpec((SSD_CHUNK, LANES), lambda g, c: (rev(c), g)),
                   par(LANES), par(LANES), par(LANES)],
        out_shape=[jax.ShapeDtypeStruct((L, SSD_WIDTH), F32), jax.ShapeDtypeStruct((L, 256), F32),
                   jax.ShapeDtypeStruct((L, 256), F32), jax.ShapeDtypeStruct((L, 256), F32),
                   pshape, pshape, pshape],
        scratch_shapes=[pltpu.VMEM((GROUP_W, SSD_STATE), F32), pltpu.VMEM((SSD_CHUNK, GROUP_W), BF16),
                        pltpu.VMEM((SSD_CHUNK, GROUP_W), BF16)],
    )(dy, xc, xc, xc, dt, sprev, dt_bias, a_log, d_wide, expand, expand_t)


def _ssd_post(y, z, gain, name):
    L = y.shape[0]

    def body(y_ref, z_ref, g_ref, o_ref):
        ob = y_ref[...] * _silu(z_ref[...])
        o_ref[...] = (ob * _rstd(ob) * g_ref[...]).astype(BF16)

    tile = min(L, WIDE_ROW_TILE)
    row = _row_spec(tile, 1024)
    return pl.pallas_call(body, name=name, grid=(L // tile,), in_specs=[row, row, _full_spec((1, 1024))],
                          out_specs=row, out_shape=jax.ShapeDtypeStruct((L, 1024), BF16))(y, z, gain)


def _ssd_post_bwd(dx, w_out, y, z, gain, layer, name):
    L = y.shape[0]

    def body(dx_ref, wo_ref, y_ref, z_ref, g_ref, dy_ref, dz_ref, dgain_ref):
        first = pl.program_id(0) == 0
        yv = y_ref[...]
        zv = z_ref[...]
        sz = _silu(zv)
        ob = yv * sz
        dyb = _dot_nt(dx_ref[...], wo_ref[...])
        dob, dgain = _rms_bwd(ob, _rstd(ob), g_ref[...], dyb)
        dy_ref[...] = dob * sz
        dz_ref[...] = dob * yv * _dsilu(zv)
        _acc_rows(dgain_ref, dgain, first)

    row = _row_spec(ROW_TILE, 1024)
    vec = _full_spec((1, 1024))
    return pl.pallas_call(
        body, name=name, grid=(L // ROW_TILE,),
        in_specs=[row, _layer_spec((1024, 1024), layer, 1), row, row, vec],
        out_specs=[row, row, vec],
        out_shape=[jax.ShapeDtypeStruct((L, 1024), F32), jax.ShapeDtypeStruct((L, 1024), F32),
                   jax.ShapeDtypeStruct((1, 1024), F32)],
    )(dx, w_out, y, z, gain)


def _out_proj(x, ya, yb, w_out, layer, name):
    L = x.shape[0]

    def body(x_ref, ya_ref, yb_ref, w_ref, o_ref):
        o_ref[...] = x_ref[...] + _dot(ya_ref[...], w_ref[:1024, :]) + _dot(yb_ref[...], w_ref[1024:, :])

    tile = min(L, WIDE_ROW_TILE)
    row = _row_spec(tile, 1024)
    return pl.pallas_call(body, name=name, grid=(L // tile,),
                          in_specs=[row, row, row, _layer_spec((2048, 1024), layer)],
                          out_specs=row, out_shape=jax.ShapeDtypeStruct((L, D_MODEL), F32))(x, ya, yb, w_out)


def _ffn(x, gain, w_gate, w_up, w_down, layer, name):
    L = x.shape[0]

    def body(x_ref, g_ref, wg_ref, wu_ref, wd_ref, o_ref, gt_ref, up_ref):
        xv = x_ref[...]
        h = (xv * _rstd(xv) * g_ref[...]).astype(BF16)
        gt = _dot_nt(h, wg_ref[...])
        up = _dot_nt(h, wu_ref[...])
        gt_ref[...] = gt.astype(BF16)
        up_ref[...] = up.astype(BF16)
        o_ref[...] = xv + _dot((_silu(gt) * up).astype(BF16), wd_ref[...])

    tile = min(L, WIDE_ROW_TILE)
    row = _row_spec(tile, D_MODEL)
    hid = _row_spec(tile, FFN)
    return pl.pallas_call(
        body, name=name, grid=(L // tile,),
        in_specs=[row, _full_spec((1, D_MODEL)), _layer_spec((FFN, D_MODEL), layer),
                  _layer_spec((FFN, D_MODEL), layer), _layer_spec((FFN, D_MODEL), layer)],
        out_specs=[row, hid, hid],
        out_shape=[jax.ShapeDtypeStruct((L, D_MODEL), F32), jax.ShapeDtypeStruct((L, FFN), BF16),
                   jax.ShapeDtypeStruct((L, FFN), BF16)],
    )(x, gain, w_gate, w_up, w_down)


def _ffn_bwd(dx2, x1, gt, up, gain, w_gate, w_up, w_down, layer, name):
    L = x1.shape[0]

    def body(d_ref, x_ref, gt_ref, up_ref, g_ref, wg_ref, wu_ref, wd_ref,
             dx_ref, dxb_ref, h_ref, act_ref, dgt_ref, dup_ref, dgain_ref):
        first = pl.program_id(0) == 0
        dv = d_ref[...]
        xv = x_ref[...]
        r = _rstd(xv)
        h_ref[...] = (xv * r * g_ref[...]).astype(BF16)
        gtv = gt_ref[...].astype(F32)
        upv = up_ref[...].astype(F32)
        sg = _silu(gtv)
        act_ref[...] = (sg * upv).astype(BF16)
        dact = _dot_nt(dv.astype(BF16), wd_ref[...])
        dgt = (dact * upv * _dsilu(gtv)).astype(BF16)
        dup = (dact * sg).astype(BF16)
        dgt_ref[...] = dgt
        dup_ref[...] = dup
        dh = _dot(dgt, wg_ref[...]) + _dot(dup, wu_ref[...])
        dxn, dgain = _rms_bwd(xv, r, g_ref[...], dh)
        dx = dv + dxn
        dx_ref[...] = dx
        dxb_ref[...] = dx.astype(BF16)
        _acc_rows(dgain_ref, dgain, first)

    row = _row_spec(ROW_TILE, D_MODEL)
    hid = _row_spec(ROW_TILE, FFN)
    vec = _full_spec((1, D_MODEL))
    return pl.pallas_call(
        body, name=name, grid=(L // ROW_TILE,),
        in_specs=[row, row, hid, hid, vec, _layer_spec((FFN, D_MODEL), layer), _layer_spec((FFN, D_MODEL), layer),
                  _layer_spec((FFN, D_MODEL), layer)],
        out_specs=[row, row, row, hid, hid, hid, vec],
        out_shape=[jax.ShapeDtypeStruct((L, D_MODEL), F32), jax.ShapeDtypeStruct((L, D_MODEL), BF16),
                   jax.ShapeDtypeStruct((L, D_MODEL), BF16),
                   jax.ShapeDtypeStruct((L, FFN), BF16), jax.ShapeDtypeStruct((L, FFN), BF16),
                   jax.ShapeDtypeStruct((L, FFN), BF16), jax.ShapeDtypeStruct((1, D_MODEL), F32)],
    )(dx2, x1, gt, up, gain, w_gate, w_up, w_down)


def _inproj_bwd(dx1, x0, du_skip, du_scan, dz, dxbc, ddt, gain, w_pad, layer, name):
    L = x0.shape[0]

    def body(d_ref, x_ref, dus_ref, duc_ref, dz_ref, dxbc_ref, ddt_ref, g_ref, w_ref,
             dx_ref, dxb_ref, dp_ref, dgain_ref):
        first = pl.program_id(0) == 0
        xv = x_ref[...]
        dp = jnp.concatenate([dus_ref[...] + duc_ref[...], dz_ref[...], dxbc_ref[...], ddt_ref[...]],
                             axis=1).astype(BF16)
        dp_ref[...] = dp
        dh = _dot_nt(dp, w_ref[...])
        dxn, dgain = _rms_bwd(xv, _rstd(xv), g_ref[...], dh)
        dx = d_ref[...] + dxn
        dx_ref[...] = dx
        dxb_ref[...] = dx.astype(BF16)
        _acc_rows(dgain_ref, dgain, first)

    row = _row_spec(ROW_TILE, D_MODEL)
    vec = _full_spec((1, D_MODEL))
    return pl.pallas_call(
        body, name=name, grid=(L // ROW_TILE,),
        in_specs=[row, row, row, row, row, _row_spec(ROW_TILE, SSD_CONV_DIM), _row_spec(ROW_TILE, 256), vec,
                  _layer_spec((D_MODEL, IN_PAD), layer)],
        out_specs=[row, row, _row_spec(ROW_TILE, IN_PAD), vec],
        out_shape=[jax.ShapeDtypeStruct((L, D_MODEL), F32), jax.ShapeDtypeStruct((L, D_MODEL), BF16),
                   jax.ShapeDtypeStruct((L, IN_PAD), BF16), jax.ShapeDtypeStruct((1, D_MODEL), F32)],
    )(dx1, x0, du_skip, du_scan, dz, dxbc, ddt, gain, w_pad)


def _final_loss(x, gain, target, name):
    L = x.shape[0]

    def body(x_ref, g_ref, t_ref, loss_ref, dx_ref, dxb_ref, dgain_ref):
        first = pl.program_id(0) == 0
        xv = x_ref[...]
        r = _rstd(xv)
        err = xv * r * g_ref[...] - t_ref[...]
        part = 0.5 * jnp.sum(jnp.mean(err * err, axis=-1, keepdims=True), axis=0, keepdims=True)
        dx, dgain = _rms_bwd(xv, r, g_ref[...], err * (1.0 / D_MODEL))
        dx_ref[...] = dx
        dxb_ref[...] = dx.astype(BF16)
        _acc_rows(loss_ref, jnp.broadcast_to(part, (1, LANES)), first)
        _acc_rows(dgain_ref, dgain, first)

    row = _row_spec(ROW_TILE, D_MODEL)
    vec = _full_spec((1, D_MODEL))
    return pl.pallas_call(
        body, name=name, grid=(L // ROW_TILE,), in_specs=[row, vec, row],
        out_specs=[_full_spec((1, LANES)), row, row, vec],
        out_shape=[jax.ShapeDtypeStruct((1, LANES), F32), jax.ShapeDtypeStruct((L, D_MODEL), F32),
                   jax.ShapeDtypeStruct((L, D_MODEL), BF16), jax.ShapeDtypeStruct((1, D_MODEL), F32)],
    )(x, gain, target)


def _to_segments(a):
    L, n = a.shape
    return a.reshape(SEGS, L // SEGS, n).transpose(1, 0, 2).reshape(L, n)


def _from_segments(a):
    L, n = a.shape
    return a.reshape(L // SEGS, SEGS, n).transpose(1, 0, 2).reshape(L, n)


def _diag_block(g):
    k, a = divmod(g, S5_TILE_GROUPS)
    return k, slice(a * S5_GROUP, (a + 1) * S5_GROUP), slice(a * S5_STATE, (a + 1) * S5_STATE)


def _block_diag_build(mats, name):
    n = mats.shape[0]

    def body(m_ref, o_ref):
        o_ref[...] = jnp.zeros(o_ref.shape, BF16)
        for q in range(n):
            for g in range(S5_GROUPS):
                k, rows, cols = _diag_block(g)
                o_ref[q, k, rows, cols] = m_ref[q, g].astype(BF16)

    return pl.pallas_call(body, name=name,
                          out_shape=jax.ShapeDtypeStruct((n, S5_TILES, S5_TILE_IN, S5_TILE_ST), BF16))(mats)


def _block_diag_extract(dense, name):
    n = len(dense)

    def body(*refs):
        o_ref = refs[n]
        for q in range(n):
            for g in range(S5_GROUPS):
                k, rows, cols = _diag_block(g)
                o_ref[q, g] = refs[q][k, rows, cols]

    return pl.pallas_call(body, name=name,
                          out_shape=jax.ShapeDtypeStruct((n, S5_GROUPS, S5_GROUP, S5_STATE), F32))(*dense)


def _pad_in_proj(w):
    z = jnp.zeros(w.shape[:-1] + (LANES - SSD_GROUP_HEADS,), w.dtype)
    return jnp.concatenate([w[..., :IN_MAIN + 8], z, w[..., IN_MAIN + 8:], z], axis=-1)


def _unpad_in_proj(w):
    return jnp.concatenate([w[..., :IN_MAIN + 8], w[..., IN_MAIN + LANES:IN_MAIN + LANES + 8]], axis=-1)


def _lane_dense(a):
    return a.reshape(DEPTH, D_MODEL // LANES, LANES, -1).transpose(3, 1, 0, 2).reshape(-1, LANES)


def _from_lane_dense(a):
    return a.reshape(-1, D_MODEL // LANES, DEPTH, LANES).transpose(2, 1, 3, 0).reshape(DEPTH, D_MODEL, -1)


def _pad_heads(v):
    v = v.reshape(SSD_GROUPS, 1, SSD_GROUP_HEADS)
    return jnp.pad(v, ((0, 0), (0, 0), (0, LANES - SSD_GROUP_HEADS)))


def _unpad_heads(v):
    return v[:, 0, :SSD_GROUP_HEADS].reshape(SSD_HEADS)


def _layer_forward(x0, p, big, i, after_inproj=None, before_s5_post=None, before_ffn=None):
    tag = "l%d_" % i
    ls = p["s5_log_step"].reshape(S5_GROUPS, 1)
    b_hgp = (p["s5_b_re"].transpose(2, 0, 1), p["s5_b_im"].transpose(2, 0, 1))
    are, aim, bbre, bbim = _s5_prep(p["s5_lam_re"], p["s5_lam_im"], ls, b_hgp[0], b_hgp[1], tag + "s5_prep")
    mats = jnp.stack([bbre.transpose(1, 0, 2), bbim.transpose(1, 0, 2), p["s5_c_re"], p["s5_c_im"]])
    bre_bd, bim_bd, cret_bd, cimt_bd = _block_diag_build(mats, tag + "s5_blockdiag")
    s5mats = dict(bre_bd=bre_bd, bim_bd=bim_bd, cret_bd=cret_bd, cimt_bd=cimt_bd,
                  bret_bd=bre_bd.transpose(0, 2, 1), bimt_bd=bim_bd.transpose(0, 2, 1),
                  cre_bd=cret_bd.transpose(0, 2, 1), cim_bd=cimt_bd.transpose(0, 2, 1),
                  are=are.reshape(S5_TILES, 1, S5_TILE_ST), aim=aim.reshape(S5_TILES, 1, S5_TILE_ST))

    u, z, xbc, dt, h1 = _rms_inproj(x0, p["norm_mix"].reshape(1, -1), big["w_in"], None, tag + "rms_inproj")
    if after_inproj is not None:
        after_inproj(u)
    u_perm = _to_segments(u)
    ys_perm, xr, xi = _s5_scan(u_perm, bre_bd, bim_bd, s5mats["cre_bd"], s5mats["cim_bd"],
                               s5mats["are"], s5mats["aim"], tag + "s5_scan")
    ys = _from_segments(ys_perm)
    late_matrices = before_s5_post(ys) if before_s5_post is not None else {}
    big = {**big, **late_matrices}
    ya = _s5_post(ys, u, p["s5_d"].reshape(1, -1), big["s5_w_glu"], p["s5_b_glu"].reshape(1, -1),
                  p["s5_norm"].reshape(1, -1), None, tag + "s5_post")

    conv_w = jnp.pad(p["ssd_conv_w"], ((0, 8 - SSD_CONV), (0, 0)))
    conv_b = p["ssd_conv_b"].reshape(1, -1)
    xc = _ssd_conv(xbc, conv_w, conv_b, tag + "ssd_conv")
    expand = _head_expander()
    heads = dict(dt_bias=_pad_heads(p["ssd_dt_bias"]), a_log=_pad_heads(p["ssd_a_log"]),
                 d=jnp.repeat(p["ssd_d"], SSD_HEAD_DIM).reshape(SSD_GROUPS, 1, GROUP_W),
                 expand=expand, expand_t=expand.T)
    y, sprev = _ssd_scan(xc, dt, heads["dt_bias"], heads["a_log"], heads["d"], expand, heads["expand_t"],
                         tag + "ssd_scan")
    yb = _ssd_post(y, z, p["ssd_norm"].reshape(1, -1), tag + "ssd_post")

    x1 = _out_proj(x0, ya, yb, big["w_out"], None, tag + "out_proj")
    ffn_matrices = before_ffn(x1) if before_ffn is not None else {}
    big = {**big, **ffn_matrices}
    late_matrices = {**late_matrices, **ffn_matrices}
    x2, gt, up = _ffn(x1, p["norm_ffn"].reshape(1, -1), big["w_gate"], big["w_up"], big["w_down"], None,
                      tag + "ffn")
    saved = dict(x0=x0, h1=h1, u=u, u_perm=u_perm, z=z, xbc=xbc, dt=dt, xr=xr, xi=xi, ys=ys, ya=ya, xc=xc, y=y,
                 sprev=sprev, yb=yb, x1=x1, gt=gt, up=up, s5mats=s5mats, heads=heads, conv_w=conv_w,
                 conv_b=conv_b, ls=ls, b_hgp=b_hgp, late_matrices=late_matrices)
    return x2, saved


def _layer_backward(dx2, dx2b, p, big, s, i, after_ffn_grads=None, after_s5_grads=None):
    tag = "l%d_" % i
    g = {}
    dx1, dx1b, h2, act, dgt, dup, dgain = _ffn_bwd(dx2, s["x1"], s["gt"], s["up"], p["norm_ffn"].reshape(1, -1),
                                                  big["w_gate"], big["w_up"], big["w_down"], None, tag + "ffn_bwd")
    g["norm_ffn"] = dgain[0]
    g["w_down"] = _mm_tn(act, dx2b, tag + "dw_down")
    g["w_gate"] = _mm_tn(dgt, h2, tag + "dw_gate")
    g["w_up"] = _mm_tn(dup, h2, tag + "dw_up")
    g["w_out"] = _mm_tn(jnp.concatenate([s["ya"], s["yb"]], axis=1), dx1b, tag + "dw_out")
    if after_ffn_grads is not None:
        p = {**p, "s5_norm": p["s5_norm"] + after_ffn_grads(g)[0, 0]}

    dys, du_skip, gelu_b, dq_b, dgain, dd, dbg = _s5_post_bwd(
        dx1b, big["w_out"], s["ys"], s["u"], p["s5_d"].reshape(1, -1), big["s5_w_glu"],
        p["s5_b_glu"].reshape(1, -1), p["s5_norm"].reshape(1, -1), None, tag + "s5_post_bwd")
    g["s5_norm"] = dgain[0]
    g["s5_d"] = dd[0]
    g["s5_b_glu"] = dbg[0]
    g["s5_w_glu"] = _mm_tn(gelu_b, dq_b, tag + "dw_glu")
    m = s["s5mats"]
    du_perm, dar, dai, dcre_d, dcim_d, dbre_d, dbim_d = _s5_scan_bwd(
        _to_segments(dys), s["u_perm"], s["xr"], s["xi"], m["bret_bd"], m["bimt_bd"], m["cret_bd"], m["cimt_bd"],
        m["are"], m["aim"], tag + "s5_scan_bwd")
    du_scan = _from_segments(du_perm)
    diag = _block_diag_extract([dcre_d, dcim_d, dbre_d, dbim_d], tag + "s5_blockdiag_bwd")
    g["s5_c_re"], g["s5_c_im"] = diag[0], diag[1]
    dbbre = diag[2].transpose(1, 0, 2)
    dbbim = diag[3].transpose(1, 0, 2)
    dlr, dli, dls, dbre, dbim = _s5_prep_bwd(
        p["s5_lam_re"], p["s5_lam_im"], s["ls"], s["b_hgp"][0], s["b_hgp"][1],
        dar.reshape(S5_GROUPS, S5_STATE), dai.reshape(S5_GROUPS, S5_STATE), dbbre, dbbim, tag + "s5_prep_bwd")
    g["s5_lam_re"] = dlr
    g["s5_lam_im"] = dli
    g["s5_log_step"] = dls[:, 0]
    g["s5_b_re"] = dbre
    g["s5_b_im"] = dbim
    if after_s5_grads is not None:
        p = {**p, "ssd_norm": p["ssd_norm"] + after_s5_grads(g)[0, 0]}

    dy, dz, dgain = _ssd_post_bwd(dx1b, big["w_out"], s["y"], s["z"], p["ssd_norm"].reshape(1, -1), None,
                                  tag + "ssd_post_bwd")
    g["ssd_norm"] = dgain[0]
    hd = s["heads"]
    dxs, dbm, dcm, ddt, dbias, dalog, dd = _ssd_scan_bwd(dy, s["xc"], s["dt"], s["sprev"], hd["dt_bias"],
                                                       hd["a_log"], hd["d"], hd["expand"], hd["expand_t"],
                                                       tag + "ssd_scan_bwd")
    g["ssd_dt_bias"] = _unpad_heads(dbias)
    g["ssd_a_log"] = _unpad_heads(dalog)
    g["ssd_d"] = _unpad_heads(dd)
    dxc = jnp.concatenate([dxs, dbm, dcm], axis=1)
    dxbc, dcw, dcb = _ssd_conv_bwd(dxc, s["xbc"], s["conv_w"], s["conv_b"], tag + "ssd_conv_bwd")
    g["ssd_conv_w"] = dcw[:SSD_CONV]
    g["ssd_conv_b"] = dcb[0]

    dx0, dx0b, dproj, dgain = _inproj_bwd(dx1, s["x0"], du_skip, du_scan, dz, dxbc, ddt, p["norm_mix"].reshape(1, -1),
                                          big["w_in"], None, tag + "inproj_bwd")
    g["norm_mix"] = dgain[0]
    g["w_in"] = _mm_tn(s["h1"], dproj, tag + "dw_in")
    return dx0, dx0b, g


MIXER_BIG = ("w_in", "s5_w_glu", "w_out")
FFN_BIG = ("w_gate", "w_up", "w_down")
BIG = MIXER_BIG + FFN_BIG
COL_SHARDED = ("w_in",)
T_STORED = ("w_gate", "w_up")
LAYER_SMALL = ("norm_mix", "s5_lam_re", "s5_lam_im", "s5_log_step", "s5_b_re", "s5_b_im", "s5_c_re", "s5_c_im",
               "s5_d", "s5_b_glu", "s5_norm", "ssd_conv_w", "ssd_conv_b", "ssd_dt_bias", "ssd_a_log", "ssd_d",
               "ssd_norm", "norm_ffn")
WEIGHTS = ("norm_mix", "w_in", "s5_lam_re", "s5_lam_im", "s5_log_step", "s5_b_re", "s5_b_im", "s5_c_re", "s5_c_im",
           "s5_d", "s5_w_glu", "s5_b_glu", "s5_norm", "ssd_conv_w", "ssd_conv_b", "ssd_dt_bias", "ssd_a_log",
           "ssd_d", "ssd_norm", "w_out", "norm_ffn", "w_gate", "w_up", "w_down", "norm_final")


S5_BC = ("s5_b_re", "s5_b_im", "s5_c_re", "s5_c_im")
TINY = tuple(k for k in LAYER_SMALL if k not in S5_BC)


def _my_place():
    return lax.axis_index("x"), lax.axis_index("y"), lax.axis_index("c")


HBM = pl.BlockSpec(memory_space=pltpu.HBM)
SEM = pl.BlockSpec(memory_space=pltpu.SEMAPHORE)
DATAFLOW = pltpu.SideEffectType.DATAFLOW_SIDE_EFFECTING


def _in_hbm(a):
    return pltpu.with_memory_space_constraint(a, pltpu.HBM)


TOKEN = jax.ShapeDtypeStruct((8, LANES), F32)
VMEM_SPEC = pl.BlockSpec(memory_space=pltpu.VMEM)


def _gather_start(blocks, after, name):
    nt = len(blocks)

    def body(*refs):
        ins = refs[:nt]
        lands = refs[nt:2 * nt]
        send_sems, recv_sems = refs[2 * nt + 1:2 * nt + 3]
        refs[-1][...] = jnp.zeros(TOKEN.shape, F32)
        x, y, c = _my_place()
        me = 4 * x + 2 * y + c
        peers = [(x, y, 1 - c), (1 - x, y, c), (x, 1 - y, c), (1 - x, 1 - y, c)]
        for t in range(nt):
            for k, peer in enumerate(peers):
                pltpu.make_async_remote_copy(src_ref=ins[t], dst_ref=lands[t].at[me], send_sem=send_sems.at[4 * t + k],
                                             recv_sem=recv_sems.at[4 * t + k], device_id=peer,
                                             device_id_type=MESH).start()

    lands = [_in_hbm(lax.empty((8,) + b.shape, b.dtype)) for b in blocks]
    out = pl.pallas_call(
        body, name=name, in_specs=[HBM] * (2 * nt) + [ANY],
        out_shape=(pltpu.SemaphoreType.DMA((4 * nt,)), pltpu.SemaphoreType.DMA((4 * nt,)),
                   *[pltpu.HBM(b.shape, b.dtype) for b in blocks],
                   *[pltpu.HBM((8,) + b.shape, b.dtype) for b in blocks], TOKEN),
        out_specs=(SEM, SEM, *[HBM] * (2 * nt), VMEM_SPEC),
        input_output_aliases={i: 2 + i for i in range(2 * nt)},
        compiler_params=pltpu.CompilerParams(has_side_effects=DATAFLOW),
    )(*[_in_hbm(b) for b in blocks], *lands, after)
    return out[:2], list(out[2:2 + nt]), list(out[2 + nt:2 + 2 * nt]), out[-1]


def _gather_forward(sems, blocks, lands, after, name):
    nt = len(blocks)

    def body(*refs):
        ins = refs[:nt]
        lands_in = refs[nt:2 * nt]
        send1, recv1 = refs[2 * nt:2 * nt + 2]
        send2, recv2 = refs[2 * nt + 3:2 * nt + 5]
        x, y, c = _my_place()
        me = 4 * x + 2 * y + c
        sibling = (x, y, 1 - c)
        sources = [4 * x + 2 * y + (1 - c), 4 * (1 - x) + 2 * y + c, 4 * x + 2 * (1 - y) + c,
                   4 * (1 - x) + 2 * (1 - y) + c]
        for t in range(nt):
            for k, src in enumerate(sources):
                cp = pltpu.make_async_remote_copy(src_ref=ins[t], dst_ref=lands_in[t].at[src],
                                                  send_sem=send1.at[4 * t + k], recv_sem=recv1.at[4 * t + k],
                                                  device_id=sibling, device_id_type=MESH)
                cp.wait_send()
                cp.wait_recv()
            for k, src in enumerate(sources[1:]):
                pltpu.make_async_remote_copy(src_ref=lands_in[t].at[src], dst_ref=lands_in[t].at[src],
                                             send_sem=send2.at[3 * t + k], recv_sem=recv2.at[3 * t + k],
                                             device_id=sibling, device_id_type=MESH).start()

    out = pl.pallas_call(
        body, name=name, in_specs=[HBM] * (2 * nt) + [SEM, SEM, pl.BlockSpec(memory_space=pl.ANY)],
        out_shape=(pltpu.SemaphoreType.DMA((3 * nt,)), pltpu.SemaphoreType.DMA((3 * nt,)),
                   *[pltpu.HBM(b.shape, b.dtype) for b in blocks],
                   *[pltpu.HBM(a.shape, a.dtype) for a in lands]),
        out_specs=(SEM, SEM, *[HBM] * (2 * nt)),
        input_output_aliases={i: 2 + i for i in range(2 * nt)},
        compiler_params=pltpu.CompilerParams(has_side_effects=DATAFLOW),
    )(*blocks, *lands, *sems, after)
    return out[:2], list(out[2 + nt:])


def _gather_finish(sems, lands, after, name):
    nt = len(lands)

    def body(*refs):
        lands_in = refs[:nt]
        send2, recv2 = refs[nt:nt + 2]
        x, y, c = _my_place()
        sibling = (x, y, 1 - c)
        mine = [4 * (1 - x) + 2 * y + c, 4 * x + 2 * (1 - y) + c, 4 * (1 - x) + 2 * (1 - y) + c]
        theirs = [4 * (1 - x) + 2 * y + 1 - c, 4 * x + 2 * (1 - y) + 1 - c, 4 * (1 - x) + 2 * (1 - y) + 1 - c]
        for t in range(nt):
            for k in range(3):
                cp = pltpu.make_async_remote_copy(src_ref=lands_in[t].at[mine[k]], dst_ref=lands_in[t].at[theirs[k]],
                                                  send_sem=send2.at[3 * t + k], recv_sem=recv2.at[3 * t + k],
                                                  device_id=sibling, device_id_type=MESH)
                cp.wait_send()
                cp.wait_recv()

    out = pl.pallas_call(
        body, name=name, in_specs=[HBM] * nt + [SEM, SEM, pl.BlockSpec(memory_space=pl.ANY)],
        out_shape=tuple(pltpu.HBM(a.shape, a.dtype) for a in lands), out_specs=tuple([HBM] * nt),
        input_output_aliases={i: i for i in range(nt)},
        compiler_params=pltpu.CompilerParams(has_side_effects=DATAFLOW),
    )(*lands, *sems, after)
    return list(out)


def _other_chips():
    x, y, _ = _my_place()
    return [(1 - x, y), (x, 1 - y), (1 - x, 1 - y)]


def _scatter_start(chunks, name):
    nt = len(chunks)

    def body(*refs):
        ins = refs[:nt]
        lands = refs[nt:2 * nt]
        send_sems, recv_sems = refs[2 * nt:2 * nt + 2]
        refs[-1][...] = jnp.zeros(TOKEN.shape, F32)
        x, y, c = _my_place()
        for t in range(nt):
            for j, (px, py) in enumerate(_other_chips()):
                pltpu.make_async_remote_copy(src_ref=ins[t].at[2 * px + py], dst_ref=lands[t].at[2 * x + y],
                                             send_sem=send_sems.at[3 * t + j], recv_sem=recv_sems.at[3 * t + j],
                                             device_id=(px, py, c), device_id_type=MESH).start()

    lands = [_in_hbm(lax.empty(a.shape, a.dtype)) for a in chunks]
    out = pl.pallas_call(
        body, name=name, in_specs=[HBM] * (2 * nt),
        out_shape=(pltpu.SemaphoreType.DMA((3 * nt,)), pltpu.SemaphoreType.DMA((3 * nt,)),
                   *[pltpu.HBM(a.shape, a.dtype) for a in chunks] * 2, TOKEN),
        out_specs=(SEM, SEM, *[HBM] * (2 * nt), VMEM_SPEC),
        input_output_aliases={i: 2 + i for i in range(2 * nt)},
        compiler_params=pltpu.CompilerParams(has_side_effects=DATAFLOW),
    )(*[_in_hbm(a) for a in chunks], *lands)
    return out[:2], list(out[2:2 + nt]), list(out[2 + nt:2 + 2 * nt]), out[-1]


def _scatter_finish(sems, chunks, lands, after, name):
    nt = len(chunks)

    def body(*refs):
        ins = refs[:nt]
        lands_in = refs[nt:2 * nt]
        send_sems, recv_sems = refs[2 * nt:2 * nt + 2]
        _, _, c = _my_place()
        for t in range(nt):
            for j, (px, py) in enumerate(_other_chips()):
                cp = pltpu.make_async_remote_copy(src_ref=ins[t].at[2 * px + py], dst_ref=lands_in[t].at[2 * px + py],
                                                  send_sem=send_sems.at[3 * t + j], recv_sem=recv_sems.at[3 * t + j],
                                                  device_id=(px, py, c), device_id_type=MESH)
                cp.wait_send()
                cp.wait_recv()

    out = pl.pallas_call(
        body, name=name, in_specs=[HBM] * (2 * nt) + [SEM, SEM, ANY],
        out_shape=tuple(pltpu.HBM(a.shape, a.dtype) for a in lands), out_specs=tuple([HBM] * nt),
        input_output_aliases={nt + i: i for i in range(nt)},
        compiler_params=pltpu.CompilerParams(has_side_effects=DATAFLOW),
    )(*chunks, *lands, *sems, after)
    return list(out)


def _swap_start(views, name):
    nt = len(views)

    def body(*refs):
        ins = refs[:nt]
        lands = refs[nt:2 * nt]
        send_sems, recv_sems = refs[2 * nt:2 * nt + 2]
        refs[-1][...] = jnp.zeros(TOKEN.shape, F32)
        x, y, c = _my_place()
        for t in range(nt):
            pltpu.make_async_remote_copy(
                src_ref=ins[t].at[pl.ds(0, views[t].shape[0]), pl.ds(1 - c, 1)], dst_ref=lands[t],
                send_sem=send_sems.at[t], recv_sem=recv_sems.at[t], device_id=(x, y, 1 - c),
                device_id_type=MESH).start()

    shapes = [(a.shape[0], 1) + a.shape[2:] for a in views]
    lands = [_in_hbm(lax.empty(s, a.dtype)) for s, a in zip(shapes, views)]
    out = pl.pallas_call(
        body, name=name, in_specs=[HBM] * (2 * nt),
        out_shape=(pltpu.SemaphoreType.DMA((nt,)), pltpu.SemaphoreType.DMA((nt,)),
                   *[pltpu.HBM(a.shape, a.dtype) for a in views],
                   *[pltpu.HBM(s, a.dtype) for s, a in zip(shapes, views)], TOKEN),
        out_specs=(SEM, SEM, *[HBM] * (2 * nt), VMEM_SPEC),
        input_output_aliases={i: 2 + i for i in range(2 * nt)},
        compiler_params=pltpu.CompilerParams(has_side_effects=DATAFLOW),
    )(*[_in_hbm(a) for a in views], *lands)
    return out[:2], list(out[2:2 + nt]), list(out[2 + nt:2 + 2 * nt]), out[-1]


def _swap_finish(sems, views, lands, after, name):
    nt = len(views)

    def body(*refs):
        ins = refs[:nt]
        lands_in = refs[nt:2 * nt]
        send_sems, recv_sems = refs[2 * nt:2 * nt + 2]
        x, y, c = _my_place()
        for t in range(nt):
            cp = pltpu.make_async_remote_copy(
                src_ref=ins[t].at[pl.ds(0, views[t].shape[0]), pl.ds(1 - c, 1)], dst_ref=lands_in[t],
                send_sem=send_sems.at[t], recv_sem=recv_sems.at[t], device_id=(x, y, 1 - c), device_id_type=MESH)
            cp.wait_send()
            cp.wait_recv()

    out = pl.pallas_call(
        body, name=name, in_specs=[HBM] * (2 * nt) + [SEM, SEM, ANY],
        out_shape=tuple(pltpu.HBM(a.shape, a.dtype) for a in lands), out_specs=tuple([HBM] * nt),
        input_output_aliases={nt + i: i for i in range(nt)},
        compiler_params=pltpu.CompilerParams(has_side_effects=DATAFLOW),
    )(*views, *lands, *sems, after)
    return list(out)


def _swap_halves(views, name):
    nt = len(views)

    def body(*refs):
        ins = refs[:nt]
        outs = refs[nt:2 * nt]
        send_sems, recv_sems = refs[2 * nt:]
        x, y, c = _my_place()
        copies = [pltpu.make_async_remote_copy(
            src_ref=ins[t].at[pl.ds(0, views[t].shape[0]), pl.ds(1 - c, 1)], dst_ref=outs[t],
            send_sem=send_sems.at[t], recv_sem=recv_sems.at[t], device_id=(x, y, 1 - c), device_id_type=MESH)
            for t in range(nt)]
        for cp in copies:
            cp.start()
        for cp in copies:
            cp.wait()

    return pl.pallas_call(
        body, name=name, in_specs=[ANY] * nt, out_specs=[ANY] * nt,
        out_shape=[jax.ShapeDtypeStruct((a.shape[0], 1) + a.shape[2:], a.dtype) for a in views],
        scratch_shapes=[pltpu.SemaphoreType.DMA((nt,)), pltpu.SemaphoreType.DMA((nt,))],
    )(*views)


def _pair_add_halves(view, recv, name):
    n, _, rows, cols = view.shape
    tile = _row_tile(rows, cols, 4)

    def body(a0_ref, a1_ref, r_ref, o_ref):
        mine = jnp.where(lax.axis_index("c") == 0, a0_ref[...], a1_ref[...])
        o_ref[...] = (mine.astype(F32) + r_ref[...].astype(F32)).astype(o_ref.dtype)

    half = lambda h: pl.BlockSpec((None, None, tile, cols), lambda p, i: (p, h, i, 0))
    return pl.pallas_call(
        body, name=name, grid=(n, rows // tile), in_specs=[half(0), half(1), half(0)],
        out_specs=pl.BlockSpec((None, tile, cols), lambda p, i: (p, i, 0)),
        out_shape=jax.ShapeDtypeStruct((n, rows, cols), view.dtype))(view, view, recv)


def _sum_chunks(lands, chunks, order, name):
    _, rows, cols = chunks.shape
    tile = _row_tile(rows, cols, 5)

    def body(order_ref, l0_ref, l1_ref, l2_ref, own_ref, o_ref):
        o_ref[...] = ((l0_ref[...].astype(F32) + l1_ref[...].astype(F32)) + l2_ref[...].astype(F32)
                      + own_ref[...].astype(F32))

    slot = lambda j: pl.BlockSpec((None, tile, cols), lambda i, order_ref: (order_ref[j], i, 0))
    grid_spec = pltpu.PrefetchScalarGridSpec(
        num_scalar_prefetch=1, grid=(rows // tile,), in_specs=[slot(0), slot(1), slot(2), slot(3)],
        out_specs=pl.BlockSpec((tile, cols), lambda i, order_ref: (i, 0)))
    return pl.pallas_call(body, name=name, grid_spec=grid_spec,
                          out_shape=jax.ShapeDtypeStruct((rows, cols), F32))(order, lands, lands, lands, chunks)


def _adamw_layer(w, g_mine, g_sibling, m, v, layer, prev, name):
    depth, rows, cols = w.shape
    half = rows // 2
    tile = _row_tile(half, cols, 10)
    tiles = half // tile

    def body(w_ref, gm_ref, gs_ref, m_ref, v_ref, *rest):
        d_ref, nm_ref, nv_ref, go_ref = rest[-4:]
        gv = jnp.where(pl.program_id(0) == lax.axis_index("c"), gm_ref[...], gs_ref[...])
        d_ref[...], nm_ref[...], nv_ref[...] = _adamw_math(w_ref[...], gv, m_ref[...], v_ref[...])
        go_ref[...] = gv

    spec = pl.BlockSpec((None, tile, cols), lambda h, i: (layer, h * tiles + i, 0))
    gspec = pl.BlockSpec((tile, cols), lambda h, i: (i, 0))
    shape = jax.ShapeDtypeStruct((depth, rows, cols), F32)
    extra = list(prev)
    aliases = {5 + j: j for j in range(4)} if len(extra) == 4 else {}
    return pl.pallas_call(
        body, name=name, grid=(2, tiles), in_specs=[spec, gspec, gspec, spec, spec] + [ANY] * len(extra),
        out_specs=[spec] * 4, out_shape=[shape] * 4, input_output_aliases=aliases)(w, g_mine, g_sibling, m, v, *extra)


def _sibling_swap(arrs, name):
    nt = len(arrs)

    def body(*refs):
        ins = refs[:nt]
        outs = refs[nt:2 * nt]
        send_sems, recv_sems = refs[2 * nt:]
        x, y, c = _my_place()
        copies = [pltpu.make_async_remote_copy(src_ref=ins[t], dst_ref=outs[t], send_sem=send_sems.at[t],
                                               recv_sem=recv_sems.at[t], device_id=(x, y, 1 - c), device_id_type=MESH)
                  for t in range(nt)]
        for cp in copies:
            cp.start()
        for cp in copies:
            cp.wait()

    return pl.pallas_call(
        body, name=name, in_specs=[ANY] * nt, out_specs=[ANY] * nt,
        out_shape=[jax.ShapeDtypeStruct(a.shape, a.dtype) for a in arrs],
        scratch_shapes=[pltpu.SemaphoreType.DMA((nt,)), pltpu.SemaphoreType.DMA((nt,))],
    )(*arrs)


STREAM_VMEM_BYTES = 32 * 1024 * 1024
SUBLANES = 8


def _row_tile(rows, cols, n_arrays):
    lanes = -(-cols // LANES) * LANES
    for t in range(min(rows, 512), SUBLANES - 1, -1):
        if rows % t == 0 and t % SUBLANES == 0 and 2 * n_arrays * t * lanes * 4 <= STREAM_VMEM_BYTES:
            return t
    return rows


def _sum_leading(a, name):
    n, rows, cols = a.shape
    tile = _row_tile(rows, cols, n + 1)

    def body(a_ref, o_ref):
        acc = a_ref[0].astype(F32)
        for k in range(1, n):
            acc = acc + a_ref[k].astype(F32)
        o_ref[...] = acc

    return pl.pallas_call(
        body, name=name, grid=(rows // tile,), in_specs=[pl.BlockSpec((n, tile, cols), lambda i: (0, i, 0))],
        out_specs=pl.BlockSpec((tile, cols), lambda i: (i, 0)),
        out_shape=jax.ShapeDtypeStruct((rows, cols), F32))(a)


def _adamw_math(w, g, m, v):
    mn = ADAM_B1 * m + (1.0 - ADAM_B1) * g
    vn = ADAM_B2 * v + (1.0 - ADAM_B2) * jnp.square(g)
    m_hat = mn / (1.0 - ADAM_B1 ** ADAM_STEP)
    v_hat = vn / (1.0 - ADAM_B2 ** ADAM_STEP)
    delta = -ADAM_LR * (m_hat / (jnp.sqrt(v_hat) + ADAM_EPS) + ADAM_WD * w)
    return delta, mn, vn


def _adamw_rows(w, g, m, v, name):
    depth, rows, cols = w.shape
    tile = _row_tile(rows, cols, 7)

    def body(w_ref, g_ref, m_ref, v_ref, d_ref, nm_ref, nv_ref):
        d_ref[...], nm_ref[...], nv_ref[...] = _adamw_math(w_ref[...], g_ref[...], m_ref[...], v_ref[...])

    spec = pl.BlockSpec((None, tile, cols), lambda l, i: (l, i, 0))
    shape = jax.ShapeDtypeStruct((depth, rows, cols), F32)
    return pl.pallas_call(body, name=name, grid=(depth, rows // tile), in_specs=[spec] * 4, out_specs=[spec] * 3,
                          out_shape=[shape] * 3)(w, g, m, v)


def _adamw_many(ws, gs, ms, vs, name):
    nt = len(ws)

    def body(*refs):
        for t in range(nt):
            w_ref, g_ref, m_ref, v_ref = (refs[k * nt + t] for k in range(4))
            d_ref, nm_ref, nv_ref = (refs[(4 + k) * nt + t] for k in range(3))
            d_ref[...], nm_ref[...], nv_ref[...] = _adamw_math(w_ref[...], g_ref[...], m_ref[...], v_ref[...])

    shapes = [jax.ShapeDtypeStruct(a.shape, F32) for a in ws]
    out = pl.pallas_call(body, name=name, out_shape=shapes * 3)(*ws, *gs, *ms, *vs)
    return out[:nt], out[nt:2 * nt], out[2 * nt:]


TINY_ROWS_MULTIPLE = 128


def _flat_pack(arrs):
    flat = jnp.concatenate([a.reshape(-1) for a in arrs])
    pad = (-flat.shape[0]) % (TINY_ROWS_MULTIPLE * LANES)
    return jnp.pad(flat, (0, pad)).reshape(-1, LANES)


def _flat_unpack(buf, shapes):
    flat = buf.reshape(-1)
    out = []
    off = 0
    for shp in shapes:
        n = math.prod(shp)
        out.append(flat[off:off + n].reshape(shp))
        off += n
    return out


def kernel(x, norm_mix, w_in, s5_lam_re, s5_lam_im, s5_log_step, s5_b_re, s5_b_im, s5_c_re, s5_c_im, s5_d, s5_w_glu, s5_b_glu, s5_norm, ssd_conv_w, ssd_conv_b, ssd_dt_bias, ssd_a_log, ssd_d, ssd_norm, w_out, norm_ffn, w_gate, w_up, w_down, norm_final, loss_target, m_norm_mix, m_w_in, m_s5_lam_re, m_s5_lam_im, m_s5_log_step, m_s5_b_re, m_s5_b_im, m_s5_c_re, m_s5_c_im, m_s5_d, m_s5_w_glu, m_s5_b_glu, m_s5_norm, m_ssd_conv_w, m_ssd_conv_b, m_ssd_dt_bias, m_ssd_a_log, m_ssd_d, m_ssd_norm, m_w_out, m_norm_ffn, m_w_gate, m_w_up, m_w_down, m_norm_final, v_norm_mix, v_w_in, v_s5_lam_re, v_s5_lam_im, v_s5_log_step, v_s5_b_re, v_s5_b_im, v_s5_c_re, v_s5_c_im, v_s5_d, v_s5_w_glu, v_s5_b_glu, v_s5_norm, v_ssd_conv_w, v_ssd_conv_b, v_ssd_dt_bias, v_ssd_a_log, v_ssd_d, v_ssd_norm, v_w_out, v_norm_ffn, v_w_gate, v_w_up, v_w_down, v_norm_final):
    args = dict(locals())
    w = {k: args[k] for k in WEIGHTS}
    m = {k: args["m_" + k] for k in WEIGHTS}
    v = {k: args["v_" + k] for k in WEIGHTS}
    cx, cy, cc = _my_place()
    chip = 2 * cx + cy

    me = 4 * cx + 2 * cy + cc
    others = _other_chips()
    chunk_order = jnp.stack([2 * px + py for px, py in others] + [chip]).astype(jnp.int32)
    stored = lambda k, a: jnp.swapaxes(a, 1, 2) if k in T_STORED else a

    def my_half(k, layer):
        a = stored(k, w[k])[layer]
        return lax.dynamic_slice_in_dim(a, cc * (a.shape[0] // 2), a.shape[0] // 2, 0).astype(BF16)

    def assemble(names, lands, blocks):
        full = {}
        for k, a, b in zip(names, lands, blocks):
            a = lax.dynamic_update_index_in_dim(a, b, me, 0)
            a = a.reshape(4, 2 * a.shape[1], a.shape[2])
            if k in COL_SHARDED:
                full[k] = _pad_in_proj(a.transpose(1, 0, 2).reshape(a.shape[1], 4 * a.shape[2]))
            else:
                full[k] = a.reshape(4 * a.shape[1], a.shape[2])
        return full

    conv_block = w["ssd_conv_w"].reshape(DEPTH * SSD_CONV, -1)
    first = [my_half("w_in", 0), conv_block]
    second_names = ("s5_w_glu", "w_out")
    second = [my_half(k, 0) for k in second_names]
    ffn0 = [my_half(k, 0) for k in FFN_BIG]
    blocks1 = [my_half(k, 1) for k in BIG]
    sems_a, kept_a, lands_a, token = _gather_start(first, x, "gather0a_start")
    sems_c, kept_c, lands_c, token = _gather_start(second, token, "gather0c_start")
    sems_b, kept_b, lands_b, token = _gather_start(ffn0, token, "gather0b_start")
    sems1, kept1, lands1, token = _gather_start(blocks1, token, "gather1_start")
    sems_a, lands_a = _gather_forward(sems_a, kept_a, lands_a, token, "gather0a_forward")
    lands_a = _gather_finish(sems_a, lands_a, token, "gather0a_finish")
    big0 = assemble(("w_in",), lands_a, first)
    conv_rows = lax.dynamic_update_index_in_dim(lands_a[-1], conv_block, me, 0)
    conv_full = conv_rows.reshape(4, 2, DEPTH, SSD_CONV, -1)[:, 0].transpose(1, 2, 0, 3).reshape(
        DEPTH, SSD_CONV, SSD_CONV_DIM)
    small = {k: w[k] for k in LAYER_SMALL}
    small["ssd_conv_w"] = conv_full
    p0 = {k: a[0] for k, a in small.items()}
    p1 = {k: a[1] for k, a in small.items()}

    p0["norm_mix"] = p0["norm_mix"] + token[0, 0]
    pending = {}

    def pass_on_ffn0(u):
        pending["second"] = _gather_forward(sems_c, kept_c, lands_c, u, "gather0c_forward")
        pending["ffn0"] = _gather_forward(sems_b, kept_b, lands_b, pending["second"][1][0], "gather0b_forward")

    def second_matrices(ys):
        sems, lands = pending["second"]
        return assemble(second_names, _gather_finish(sems, lands, ys, "gather0c_finish"), second)

    def ffn0_matrices(x1):
        sems, lands = pending["ffn0"]
        lands = _gather_finish(sems, lands, x1, "gather0b_finish")
        pending["layer1"] = _gather_forward(sems1, kept1, lands1, lands[0], "gather1_forward")
        return assemble(FFN_BIG, lands, ffn0)

    h1, saved0 = _layer_forward(x[0], p0, big0, 0, pass_on_ffn0, second_matrices, ffn0_matrices)
    big0 = {**big0, **saved0["late_matrices"]}
    sems1, lands1 = pending["layer1"]
    lands1 = _gather_finish(sems1, lands1, h1, "gather1_finish")
    big1 = assemble(BIG, lands1, blocks1)
    h2, saved1 = _layer_forward(h1, p1, big1, 1)
    loss_row, dx, dxb, g_final = _final_loss(h2, w["norm_final"].reshape(1, -1), loss_target[0], "final_loss")
    loss_part, g_final = loss_row[0, 0], g_final[0]

    def halves_view(k, a):
        if k in COL_SHARDED:
            return a.reshape(1, 2, a.shape[0] // 2, a.shape[1])
        return a.reshape(4, 2, a.shape[0] // 8, a.shape[1])

    def to_chunks(k, part):
        if k in COL_SHARDED:
            a = _unpad_in_proj(part[0])
            return a.reshape(a.shape[0], 4, a.shape[1] // 4).transpose(1, 0, 2)
        return part.reshape(4, -1, part.shape[-1])

    def reduce_begin(names, views, tag):
        recv = _swap_halves(views, tag + "swap")
        parts = [_pair_add_halves(a, r, tag + "pair_" + k) for k, a, r in zip(names, views, recv)]
        chunks = [to_chunks(k, p) for k, p in zip(names, parts)]
        return _scatter_start(chunks, tag + "scatter_start")

    def reduce_end(names, handle, after, tag):
        sems, kept, lands, _ = handle
        lands = _scatter_finish(sems, kept, lands, after, tag + "scatter_finish")
        return [_sum_chunks(a, b, chunk_order, tag + "sum_" + k) for k, a, b in zip(names, lands, kept)]

    def swap_begin(names, views, tag):
        return (names, *_swap_start(views, tag + "swap_start"))

    def scatter_begin(handle, after, tag):
        names, sems, views, lands, _ = handle
        recv = _swap_finish(sems, views, lands, after, tag + "swap_finish")
        parts = [_pair_add_halves(a, r, tag + "pair_" + k) for k, a, r in zip(names, views, recv)]
        return _scatter_start([to_chunks(k, p) for k, p in zip(names, parts)], tag + "scatter_start")

    dx, dxb, g1 = _layer_backward(dx, dxb, p1, big1, saved1, 1)
    swap1 = swap_begin(BIG, [halves_view(k, g1[k]) for k in BIG], "grad1_")
    p0["norm_ffn"] = p0["norm_ffn"] + swap1[4][0, 0]

    early = FFN_BIG + ("w_out",)
    middle = ("s5_w_glu", "s5_bc")
    bc_rows = 2 * DEPTH * S5_GROUP * S5_GROUPS

    def send_early(g_so_far):
        pending["round1"] = scatter_begin(swap1, g_so_far["w_out"], "grad1_")
        pending["swap_early"] = swap_begin(early, [halves_view(k, g_so_far[k]) for k in early], "grad0a_")
        return pending["swap_early"][4]

    def send_middle(g_so_far):
        pending["early"] = scatter_begin(pending["swap_early"], g_so_far["s5_w_glu"], "grad0a_")
        rows = lambda names: jnp.stack([a for layer in (g_so_far, g1) for a in (layer[names[0]], layer[names[1]])]
                                       ).reshape(bc_rows, S5_STATE)
        bc = jnp.stack([rows(("s5_b_re", "s5_b_im")), rows(("s5_c_re", "s5_c_im"))])[None]
        pending["swap_middle"] = swap_begin(middle, [halves_view("s5_w_glu", g_so_far["s5_w_glu"]), bc], "grad0b_")
        return pending["swap_middle"][4]

    grad_x, _, g0 = _layer_backward(dx, dxb, p0, big0, saved0, 0, send_early, send_middle)
    g = {k: [g0[k], g1[k]] for k in LAYER_SMALL}
    pending["middle"] = scatter_begin(pending["swap_middle"], grad_x, "grad0b_")
    reduced1 = dict(zip(BIG, reduce_end(BIG, pending["round1"], grad_x, "grad1_")))
    shared1 = dict(zip(BIG, _sibling_swap([reduced1[k] for k in BIG], "grad1_share")))
    round0 = reduce_begin(("w_in",), [halves_view("w_in", g0["w_in"])], "grad0c_")

    delta, new_m, new_v, grads = {}, {}, {}, {}
    adam1 = {}
    layered = tuple(k for k in BIG if k not in COL_SHARDED)
    for k in layered:
        adam1[k] = _adamw_layer(stored(k, w[k]), reduced1[k], shared1[k], stored(k, m[k]), stored(k, v[k]), 1,
                                [round0[3]], "adamw1_" + k)
    follow = adam1[layered[-1]][0]
    reduced0 = dict(zip(early, reduce_end(early, pending["early"], follow, "grad0a_")))
    reduced0.update(zip(middle, reduce_end(middle, pending["middle"], follow, "grad0b_")))
    tiny_names = TINY + ("norm_final",)
    parts = [jnp.stack(g[k]) for k in TINY] + [g_final, loss_part.reshape(1)]
    shapes = [p.shape for p in parts]
    small_blocks = [_flat_pack(parts), reduced0["s5_bc"]]
    small_sems, small_kept, small_lands, small_token = _gather_start(small_blocks, follow, "gather_small_start")
    reduced0.update(zip(("w_in",), reduce_end(("w_in",), round0, small_token, "grad0c_")))
    shared0 = dict(zip(BIG, _sibling_swap([reduced0[k] for k in BIG], "grad0_share")))
    for k in layered:
        outs = _adamw_layer(stored(k, w[k]), reduced0[k], shared0[k], stored(k, m[k]), stored(k, v[k]), 0, adam1[k],
                            "adamw0_" + k)
        delta[k], new_m[k], new_v[k], grads[k] = (stored(k, a) for a in outs)
    both = lambda mine, sib: jnp.where(cc == 0, jnp.concatenate([mine, sib]), jnp.concatenate([sib, mine]))
    grads["w_in"] = jnp.stack([both(reduced0["w_in"], shared0["w_in"]), both(reduced1["w_in"], shared1["w_in"])])
    outs = _adamw_rows(*[_lane_dense(a)[None] for a in (w["w_in"], grads["w_in"], m["w_in"], v["w_in"])],
                       "adamw_w_in")
    delta["w_in"], new_m["w_in"], new_v["w_in"] = (_from_lane_dense(a[0]) for a in outs)

    last = delta["w_in"]
    small_sems, small_lands = _gather_forward(small_sems, small_kept, small_lands, last, "gather_small_forward")
    small_lands = _gather_finish(small_sems, small_lands, last, "gather_small_finish")
    allparts, bc_eighths = (lax.dynamic_update_index_in_dim(a, b, me, 0) for a, b in zip(small_lands, small_blocks))
    unpacked = _flat_unpack(_sum_leading(allparts, "sum_small"), shapes)
    loss = unpacked[-1][0]
    grads.update(zip(tiny_names, unpacked[:-1]))
    width = SSD_CONV_DIM // 4
    grads["ssd_conv_w"] = lax.dynamic_slice_in_dim(grads["ssd_conv_w"], chip * width, width, axis=2)
    bc = bc_eighths.reshape(4, 2, bc_rows // 4, S5_STATE)
    b_sum = bc[:, 0].reshape(DEPTH, 2, S5_GROUP, S5_GROUPS, S5_STATE)
    c_sum = bc[:, 1].reshape(DEPTH, 2, S5_GROUPS, S5_GROUP, S5_STATE)
    grads["s5_c_re"] = c_sum[:, 0]
    grads["s5_c_im"] = c_sum[:, 1]

    b_names = ("s5_b_re", "s5_b_im")
    hp = lambda a: a.transpose(0, 1, 3, 2)
    names = tiny_names + ("s5_c_re", "s5_c_im") + b_names
    view = lambda k, a: hp(a) if k in b_names else (a.reshape(1, -1) if a.ndim == 1 else a)
    g_view = {k: view(k, grads[k]) for k in names if k not in b_names}
    g_view.update({k: b_sum[:, j].transpose(0, 2, 1, 3) for j, k in enumerate(b_names)})
    ds, nms, nvs = _adamw_many([view(k, w[k]) for k in names], [g_view[k] for k in names],
                               [view(k, m[k]) for k in names], [view(k, v[k]) for k in names], "adamw_small")
    for k, a, b, c in zip(names, ds, nms, nvs):
        if k in b_names:
            delta[k], new_m[k], new_v[k], grads[k] = hp(a), hp(b), hp(c), hp(g_view[k])
        else:
            delta[k], new_m[k], new_v[k] = (t.reshape(w[k].shape) for t in (a, b, c))

    return (loss, grad_x[None], *[grads[k] for k in WEIGHTS], *[delta[k] for k in WEIGHTS],
            *[new_m[k] for k in WEIGHTS], *[new_v[k] for k in WEIGHTS])
```

```python
import math

import jax
import jax.numpy as jnp
from jax import lax
from jax.experimental import pallas as pl
from jax.experimental.pallas import tpu as pltpu

F32 = jnp.float32
BF16 = jnp.bfloat16
MESH = pl.DeviceIdType.MESH
ANY = pl.BlockSpec(memory_space=pl.ANY)

D_MODEL = 1024
DEPTH = 2
S5_GROUPS = 64
S5_GROUP = 16
S5_STATE = 64
S5_COLS = S5_GROUPS * S5_STATE
S5_TILE_GROUPS = 8
S5_TILES = S5_GROUPS // S5_TILE_GROUPS
S5_TILE_IN = S5_TILE_GROUPS * S5_GROUP
S5_TILE_ST = S5_TILE_GROUPS * S5_STATE
SEGS = 8
SSD_HEADS = 16
SSD_HEAD_DIM = 64
SSD_GROUPS = 2
SSD_GROUP_HEADS = SSD_HEADS // SSD_GROUPS
SSD_STATE = 128
SSD_CONV = 4
SSD_CHUNK = 128
SSD_WIDTH = 1024
SSD_CONV_DIM = SSD_WIDTH + 2 * SSD_GROUPS * SSD_STATE
IN_MAIN = 3584
IN_PAD = IN_MAIN + 2 * 128
FFN = 2816
EPS = 1e-6
LANES = 128
ROW_TILE = 256
WIDE_ROW_TILE = 512

ADAM_LR = 0.001
ADAM_B1 = 0.9
ADAM_B2 = 0.999
ADAM_EPS = 1e-08
ADAM_WD = 0.01
ADAM_STEP = 10


def _sigmoid(x):
    return 1.0 / (1.0 + jnp.exp(-x))


def _silu(x):
    return x * _sigmoid(x)


def _dsilu(x):
    s = _sigmoid(x)
    return s * (1.0 + x * (1.0 - s))


_GELU_K = math.sqrt(2.0 / math.pi)
_GELU_C = 0.044715


def _gelu(x):
    t = jnp.tanh(_GELU_K * (x + _GELU_C * x * x * x))
    return 0.5 * x * (1.0 + t)


def _dgelu(x):
    t = jnp.tanh(_GELU_K * (x + _GELU_C * x * x * x))
    return 0.5 * (1.0 + t) + 0.5 * x * (1.0 - t * t) * _GELU_K * (1.0 + 3.0 * _GELU_C * x * x)


def _softplus(x):
    e = jnp.exp(-jnp.abs(x))
    u = 1.0 + e
    log1p = jnp.where(u == 1.0, e, jnp.log(u) * e / jnp.where(u == 1.0, 1.0, u - 1.0))
    return jnp.maximum(x, 0.0) + log1p


def _rstd(x):
    return lax.rsqrt(jnp.mean(x * x, axis=-1, keepdims=True) + EPS)


def _rms_bwd(x, r, gain, dy):
    dyg = dy * gain
    dx = r * dyg - x * (r * r * r) * jnp.mean(x * dyg, axis=-1, keepdims=True)
    dgain = jnp.sum(dy * x * r, axis=0, keepdims=True)
    return dx, dgain


def _dot(a, b):
    return jnp.dot(a, b, preferred_element_type=F32)


def _dot_nt(a, b):
    return lax.dot_general(a, b, (((1,), (1,)), ((), ())), preferred_element_type=F32)


def _dot_tn(a, b):
    return lax.dot_general(a, b, (((0,), (0,)), ((), ())), preferred_element_type=F32)


def _row_spec(tile, cols):
    return pl.BlockSpec((tile, cols), lambda i: (i, 0))


def _full_spec(shape):
    nd = len(shape)
    return pl.BlockSpec(shape, lambda *_: (0,) * nd)


def _layer_spec(shape, layer, block=0):
    if layer is None:
        return pl.BlockSpec(tuple(shape), lambda *_: (block, 0), pipeline_mode=pl.Buffered(1))
    return pl.BlockSpec((None,) + tuple(shape), lambda *_: (layer, block, 0), pipeline_mode=pl.Buffered(1))


def _acc_rows(ref, val, first):
    @pl.when(first)
    def _():
        ref[...] = val

    @pl.when(jnp.logical_not(first))
    def _():
        ref[...] += val


def _pick_tile(n, cap):
    best = LANES
    for t in range(LANES, cap + 1, LANES):
        if n % t == 0:
            best = t
    return best


def _mm_tn(a, b, name):
    k, m = a.shape
    _, n = b.shape
    tm = _pick_tile(m, 1536)
    tn = _pick_tile(n, 1536)

    def body(a_ref, b_ref, o_ref):
        o_ref[...] = _dot_tn(a_ref[...], b_ref[...]).astype(BF16)

    return pl.pallas_call(
        body, name=name, grid=(n // tn, m // tm),
        in_specs=[pl.BlockSpec((k, tm), lambda j, i: (0, i)), pl.BlockSpec((k, tn), lambda j, i: (0, j))],
        out_specs=pl.BlockSpec((tm, tn), lambda j, i: (i, j)),
        out_shape=jax.ShapeDtypeStruct((m, n), BF16),
    )(a, b)


def _rms_inproj(x, gain, w_pad, layer, name):
    L = x.shape[0]

    def body(x_ref, g_ref, w_ref, u_ref, z_ref, xbc_ref, dt_ref, h_ref):
        xv = x_ref[...]
        h = (xv * _rstd(xv) * g_ref[...]).astype(BF16)
        h_ref[...] = h
        p = _dot(h, w_ref[...])
        u_ref[...] = p[:, :1024]
        z_ref[...] = p[:, 1024:2048]
        xbc_ref[...] = p[:, 2048:IN_MAIN]
        dt_ref[...] = p[:, IN_MAIN:IN_PAD]

    tile = min(L, WIDE_ROW_TILE)
    return pl.pallas_call(
        body, name=name, grid=(L // tile,),
        in_specs=[_row_spec(tile, D_MODEL), _full_spec((1, D_MODEL)), _layer_spec((D_MODEL, IN_PAD), layer)],
        out_specs=[_row_spec(tile, 1024), _row_spec(tile, 1024), _row_spec(tile, SSD_CONV_DIM),
                   _row_spec(tile, 256), _row_spec(tile, D_MODEL)],
        out_shape=[jax.ShapeDtypeStruct((L, 1024), F32), jax.ShapeDtypeStruct((L, 1024), F32),
                   jax.ShapeDtypeStruct((L, SSD_CONV_DIM), F32), jax.ShapeDtypeStruct((L, 256), F32),
                   jax.ShapeDtypeStruct((L, D_MODEL), BF16)],
    )(x, gain, w_pad)


def _s5_prep_math(lr, li, ls, bre, bim):
    step = jnp.exp(ls)
    mag = jnp.exp(lr * step)
    ang = li * step
    are = mag * jnp.cos(ang)
    aim = mag * jnp.sin(ang)
    den = lr * lr + li * li
    nr = are - 1.0
    ni = aim
    cre = (nr * lr + ni * li) / den
    cim = (ni * lr - nr * li) / den
    bbre = cre[None] * bre - cim[None] * bim
    bbim = cre[None] * bim + cim[None] * bre
    return are, aim, bbre, bbim


def _s5_prep(lr, li, ls, bre, bim, name):
    def body(lr_ref, li_ref, ls_ref, bre_ref, bim_ref, are_ref, aim_ref, bbre_ref, bbim_ref):
        are, aim, bbre, bbim = _s5_prep_math(lr_ref[...], li_ref[...], ls_ref[...], bre_ref[...], bim_ref[...])
        are_ref[...] = are
        aim_ref[...] = aim
        bbre_ref[...] = bbre
        bbim_ref[...] = bbim

    gp = jax.ShapeDtypeStruct((S5_GROUPS, S5_STATE), F32)
    hgp = jax.ShapeDtypeStruct((S5_GROUP, S5_GROUPS, S5_STATE), F32)
    return pl.pallas_call(body, name=name, out_shape=[gp, gp, hgp, hgp])(lr, li, ls, bre, bim)


def _s5_prep_bwd(lr, li, ls, bre, bim, dare, daim, dbbre, dbbim, name):
    def body(lr_ref, li_ref, ls_ref, bre_ref, bim_ref, dare_ref, daim_ref, dbbre_ref, dbbim_ref,
             dlr_ref, dli_ref, dls_ref, dbre_ref, dbim_ref):
        _, vjp = jax.vjp(_s5_prep_math, lr_ref[...], li_ref[...], ls_ref[...], bre_ref[...], bim_ref[...])
        dlr, dli, dls, dbre, dbim = vjp((dare_ref[...], daim_ref[...], dbbre_ref[...], dbbim_ref[...]))
        dlr_ref[...] = dlr
        dli_ref[...] = dli
        dls_ref[...] = dls
        dbre_ref[...] = dbre
        dbim_ref[...] = dbim

    gp = jax.ShapeDtypeStruct((S5_GROUPS, S5_STATE), F32)
    g1 = jax.ShapeDtypeStruct((S5_GROUPS, 1), F32)
    hgp = jax.ShapeDtypeStruct((S5_GROUP, S5_GROUPS, S5_STATE), F32)
    return pl.pallas_call(body, name=name, out_shape=[gp, gp, g1, hgp, hgp])(
        lr, li, ls, bre, bim, dare, daim, dbbre, dbbim)


def _cmul_add(ar, ai, sr, si, br, bi):
    return ar * sr - ai * si + br, ar * si + ai * sr + bi


def _shift_rows_down(v):
    rolled = pltpu.roll(v, 1, 0)
    row = lax.broadcasted_iota(jnp.int32, v.shape, 0)
    return jnp.where(row == 0, 0.0, rolled)


def _shift_rows_up(v):
    rolled = pltpu.roll(v, SEGS - 1, 0)
    row = lax.broadcasted_iota(jnp.int32, v.shape, 0)
    return jnp.where(row == SEGS - 1, 0.0, rolled)


def _segment_power(ar, ai, steps):
    n = 1
    while n < steps:
        ar, ai = ar * ar - ai * ai, 2.0 * ar * ai
        n *= 2
    assert n == steps
    return ar, ai


def _half_segment_entries(ar, ai, first, second, half_steps, shift):
    pr, pi = _segment_power(ar, ai, half_steps)
    er = jnp.zeros_like(first[0])
    ei = jnp.zeros_like(first[1])
    for _ in range(SEGS - 1):
        mr, mi = _cmul_add(pr, pi, er, ei, *first)
        nr, ni = _cmul_add(pr, pi, mr, mi, *second)
        er, ei = shift(nr), shift(ni)
    mr, mi = _cmul_add(pr, pi, er, ei, *first)
    return (er, ei), (mr, mi)


def _s5_scan(u_perm, bre_bd, bim_bd, cre_bd, cim_bd, are, aim, name):
    L = u_perm.shape[0]
    half = L // SEGS // 2

    def body(u_ref, bre_ref, bim_ref, cre_ref, cim_ref, are_ref, aim_ref, y_ref, xr_ref, xi_ref):
        u = u_ref[...].astype(BF16)
        xr_ref[...] = _dot(u, bre_ref[0])
        xi_ref[...] = _dot(u, bim_ref[0])
        ar = jnp.broadcast_to(are_ref[0], (SEGS, S5_TILE_ST))
        ai = jnp.broadcast_to(aim_ref[0], (SEGS, S5_TILE_ST))
        zero = jnp.zeros((SEGS, S5_TILE_ST), F32)
        block = lambda j: pl.ds(pl.multiple_of(j * SEGS, SEGS), SEGS)

        def finals(j, c):
            lo, hi = block(j), block(j + half)
            return (*_cmul_add(ar, ai, c[0], c[1], xr_ref[lo, :], xi_ref[lo, :]),
                    *_cmul_add(ar, ai, c[2], c[3], xr_ref[hi, :], xi_ref[hi, :]))

        f = lax.fori_loop(0, half, finals, (zero,) * 4, unroll=4)
        e_lo, e_hi = _half_segment_entries(ar, ai, f[:2], f[2:], half, _shift_rows_down)

        def scan(j, c):
            lo, hi = block(j), block(j + half)
            s_lo = _cmul_add(ar, ai, c[0], c[1], xr_ref[lo, :], xi_ref[lo, :])
            s_hi = _cmul_add(ar, ai, c[2], c[3], xr_ref[hi, :], xi_ref[hi, :])
            xr_ref[lo, :], xi_ref[lo, :] = s_lo
            xr_ref[hi, :], xi_ref[hi, :] = s_hi
            return (*s_lo, *s_hi)

        lax.fori_loop(0, half, scan, (*e_lo, *e_hi), unroll=8)
        y_ref[...] = (_dot(xr_ref[...].astype(BF16), cre_ref[0]) - _dot(xi_ref[...].astype(BF16), cim_ref[0]))

    tile3 = lambda a, b: pl.BlockSpec((1, a, b), lambda k: (k, 0, 0))
    return pl.pallas_call(
        body, name=name, grid=(S5_TILES,),
        in_specs=[pl.BlockSpec((L, S5_TILE_IN), lambda k: (0, k)),
                  tile3(S5_TILE_IN, S5_TILE_ST), tile3(S5_TILE_IN, S5_TILE_ST),
                  tile3(S5_TILE_ST, S5_TILE_IN), tile3(S5_TILE_ST, S5_TILE_IN),
                  tile3(1, S5_TILE_ST), tile3(1, S5_TILE_ST)],
        out_specs=[pl.BlockSpec((L, S5_TILE_IN), lambda k: (0, k)),
                   pl.BlockSpec((L, S5_TILE_ST), lambda k: (0, k)), pl.BlockSpec((L, S5_TILE_ST), lambda k: (0, k))],
        out_shape=[jax.ShapeDtypeStruct((L, 1024), F32), jax.ShapeDtypeStruct((L, S5_COLS), F32),
                   jax.ShapeDtypeStruct((L, S5_COLS), F32)],
    )(u_perm, bre_bd, bim_bd, cre_bd, cim_bd, are, aim)


def _s5_scan_bwd(dy_perm, u_perm, xr, xi, bret_bd, bimt_bd, cret_bd, cimt_bd, are, aim, name):
    L = u_perm.shape[0]
    steps = L // SEGS
    half = steps // 2

    def body(dy_ref, u_ref, xr_ref, xi_ref, bret_ref, bimt_ref, cret_ref, cimt_ref, are_ref, aim_ref,
             du_ref, dar_ref, dai_ref, dcre_ref, dcim_ref, dbre_ref, dbim_ref, gr_ref, gi_ref):
        dy = dy_ref[...].astype(BF16)
        u = u_ref[...].astype(BF16)
        gr_ref[...] = _dot(dy, cret_ref[0])
        gi_ref[...] = -_dot(dy, cimt_ref[0])
        ar = jnp.broadcast_to(are_ref[0], (SEGS, S5_TILE_ST))
        ai = -jnp.broadcast_to(aim_ref[0], (SEGS, S5_TILE_ST))
        zero = jnp.zeros((SEGS, S5_TILE_ST), F32)
        block = lambda j: pl.ds(pl.multiple_of(j * SEGS, SEGS), SEGS)

        def finals(k, c):
            hi, lo = block(steps - 1 - k), block(half - 1 - k)
            return (*_cmul_add(ar, ai, c[0], c[1], gr_ref[hi, :], gi_ref[hi, :]),
                    *_cmul_add(ar, ai, c[2], c[3], gr_ref[lo, :], gi_ref[lo, :]))

        f = lax.fori_loop(0, half, finals, (zero,) * 4, unroll=4)
        e_hi, e_lo = _half_segment_entries(ar, ai, f[:2], f[2:], half, _shift_rows_up)

        def scan(k, c):
            accr, acci = c[4], c[5]
            j_hi, j_lo = steps - 1 - k, half - 1 - k
            hi, lo = block(j_hi), block(j_lo)
            hr, hi_im = _cmul_add(ar, ai, c[0], c[1], gr_ref[hi, :], gi_ref[hi, :])
            lr, lo_im = _cmul_add(ar, ai, c[2], c[3], gr_ref[lo, :], gi_ref[lo, :])
            gr_ref[hi, :], gi_ref[hi, :] = hr, hi_im
            gr_ref[lo, :], gi_ref[lo, :] = lr, lo_im
            before_hi = block(j_hi - 1)
            before_lo = block(jnp.maximum(j_lo - 1, 0))
            live = (j_lo > 0).astype(F32)
            xhr, xhi = xr_ref[before_hi, :], xi_ref[before_hi, :]
            xlr, xli = xr_ref[before_lo, :] * live, xi_ref[before_lo, :] * live
            accr = accr + (hr * xhr + hi_im * xhi) + (lr * xlr + lo_im * xli)
            acci = acci + (hi_im * xhr - hr * xhi) + (lo_im * xlr - lr * xli)
            return hr, hi_im, lr, lo_im, accr, acci

        out = lax.fori_loop(0, half, scan, (*e_hi, *e_lo, zero, zero), unroll=4)
        accr, acci = out[4], out[5]
        first = pl.ds(0, SEGS)
        last = pl.ds((steps - 1) * SEGS, SEGS)
        xpr = _shift_rows_down(xr_ref[last, :])
        xpi = _shift_rows_down(xi_ref[last, :])
        g0r = gr_ref[first, :]
        g0i = gi_ref[first, :]
        accr = accr + g0r * xpr + g0i * xpi
        acci = acci + g0i * xpr - g0r * xpi
        dar_ref[0] = jnp.sum(accr, axis=0, keepdims=True)
        dai_ref[0] = jnp.sum(acci, axis=0, keepdims=True)

        grb = gr_ref[...].astype(BF16)
        gib = gi_ref[...].astype(BF16)
        du_ref[...] = _dot(grb, bret_ref[0]) + _dot(gib, bimt_ref[0])
        dbre_ref[0] = _dot_tn(u, grb)
        dbim_ref[0] = _dot_tn(u, gib)
        dcre_ref[0] = _dot_tn(dy, xr_ref[...].astype(BF16))
        dcim_ref[0] = -_dot_tn(dy, xi_ref[...].astype(BF16))

    tile3 = lambda a, b: pl.BlockSpec((1, a, b), lambda k: (k, 0, 0))
    col_in = pl.BlockSpec((L, S5_TILE_IN), lambda k: (0, k))
    col_st = pl.BlockSpec((L, S5_TILE_ST), lambda k: (0, k))
    dense = jax.ShapeDtypeStruct((S5_TILES, S5_TILE_IN, S5_TILE_ST), F32)
    vec = jax.ShapeDtypeStruct((S5_TILES, 1, S5_TILE_ST), F32)
    return pl.pallas_call(
        body, name=name, grid=(S5_TILES,),
        in_specs=[col_in, col_in, col_st, col_st,
                  tile3(S5_TILE_ST, S5_TILE_IN), tile3(S5_TILE_ST, S5_TILE_IN),
                  tile3(S5_TILE_IN, S5_TILE_ST), tile3(S5_TILE_IN, S5_TILE_ST),
                  tile3(1, S5_TILE_ST), tile3(1, S5_TILE_ST)],
        out_specs=[col_in, tile3(1, S5_TILE_ST), tile3(1, S5_TILE_ST),
                   tile3(S5_TILE_IN, S5_TILE_ST), tile3(S5_TILE_IN, S5_TILE_ST),
                   tile3(S5_TILE_IN, S5_TILE_ST), tile3(S5_TILE_IN, S5_TILE_ST)],
        out_shape=[jax.ShapeDtypeStruct((L, 1024), F32), vec, vec, dense, dense, dense, dense],
        scratch_shapes=[pltpu.VMEM((L, S5_TILE_ST), F32), pltpu.VMEM((L, S5_TILE_ST), F32)],
    )(dy_perm, u_perm, xr, xi, bret_bd, bimt_bd, cret_bd, cimt_bd, are, aim)


def _s5_post(ys, u, d_skip, w_glu, b_glu, gain, layer, name):
    L = ys.shape[0]

    def body(ys_ref, u_ref, d_ref, w_ref, b_ref, g_ref, ya_ref):
        g = _gelu(ys_ref[...] + d_ref[...] * u_ref[...])
        q = _dot(g.astype(BF16), w_ref[...]) + b_ref[...]
        oa = g * _sigmoid(q)
        ya_ref[...] = (oa * _rstd(oa) * g_ref[...]).astype(BF16)

    vec = _full_spec((1, 1024))
    tile = min(L, WIDE_ROW_TILE)
    return pl.pallas_call(
        body, name=name, grid=(L // tile,),
        in_specs=[_row_spec(tile, 1024), _row_spec(tile, 1024), vec, _layer_spec((1024, 1024), layer), vec, vec],
        out_specs=_row_spec(tile, 1024),
        out_shape=jax.ShapeDtypeStruct((L, 2048), BF16),
    )(ys, u, d_skip, w_glu, b_glu, gain)


def _s5_post_bwd(dx, w_out, ys, u, d_skip, w_glu, b_glu, gain, layer, name):
    L = ys.shape[0]

    def body(dx_ref, wo_ref, ys_ref, u_ref, d_ref, w_ref, b_ref, gn_ref,
             dys_ref, dus_ref, g_ref, dq_ref, dgain_ref, dd_ref, db_ref):
        first = pl.program_id(0) == 0
        uv = u_ref[...]
        yt = ys_ref[...] + d_ref[...] * uv
        g = _gelu(yt)
        gb = g.astype(BF16)
        q = _dot(gb, w_ref[...]) + b_ref[...]
        s = _sigmoid(q)
        oa = g * s
        dya = _dot_nt(dx_ref[...], wo_ref[...])
        doa, dgain = _rms_bwd(oa, _rstd(oa), gn_ref[...], dya)
        dq = doa * g * s * (1.0 - s)
        dqb = dq.astype(BF16)
        dg = doa * s + _dot_nt(dqb, w_ref[...])
        dyt = dg * _dgelu(yt)
        dys_ref[...] = dyt
        dus_ref[...] = dyt * d_ref[...]
        g_ref[...] = gb
        dq_ref[...] = dqb
        _acc_rows(dgain_ref, dgain, first)
        _acc_rows(dd_ref, jnp.sum(dyt * uv, axis=0, keepdims=True), first)
        _acc_rows(db_ref, jnp.sum(dq, axis=0, keepdims=True), first)

    vec = _full_spec((1, 1024))
    row = _row_spec(ROW_TILE, 1024)
    vshape = jax.ShapeDtypeStruct((1, 1024), F32)
    return pl.pallas_call(
        body, name=name, grid=(L // ROW_TILE,),
        in_specs=[row, _layer_spec((1024, 1024), layer, 0), row, row, vec, _layer_spec((1024, 1024), layer), vec,
                  vec],
        out_specs=[row, row, row, row, vec, vec, vec],
        out_shape=[jax.ShapeDtypeStruct((L, 1024), F32), jax.ShapeDtypeStruct((L, 1024), F32),
                   jax.ShapeDtypeStruct((L, 1024), BF16), jax.ShapeDtypeStruct((L, 1024), BF16),
                   vshape, vshape, vshape],
    )(dx, w_out, ys, u, d_skip, w_glu, b_glu, gain)


CONV_TILE = 256


def _shift_time(v, d):
    if d == 0:
        return v
    rolled = pltpu.roll(v, d, 0)
    row = lax.broadcasted_iota(jnp.int32, v.shape, 0)
    return jnp.where(row < d, 0.0, rolled)


def _unshift_time(v, d):
    if d == 0:
        return v
    n = v.shape[0]
    rolled = pltpu.roll(v, n - d, 0)
    row = lax.broadcasted_iota(jnp.int32, v.shape, 0)
    return jnp.where(row >= n - d, 0.0, rolled)


def _ssd_conv(xbc, w, b, name):
    L = xbc.shape[0]

    def body(x_ref, w_ref, b_ref, o_ref):
        xv = x_ref[...]
        pre = jnp.broadcast_to(b_ref[...], xv.shape)
        for k in range(SSD_CONV):
            pre = pre + w_ref[k:k + 1, :] * _shift_time(xv, SSD_CONV - 1 - k)
        o_ref[...] = _silu(pre)

    col = pl.BlockSpec((L, CONV_TILE), lambda j: (0, j))
    return pl.pallas_call(
        body, name=name, grid=(SSD_CONV_DIM // CONV_TILE,),
        in_specs=[col, pl.BlockSpec((8, CONV_TILE), lambda j: (0, j)), pl.BlockSpec((1, CONV_TILE), lambda j: (0, j))],
        out_specs=col, out_shape=jax.ShapeDtypeStruct((L, SSD_CONV_DIM), F32),
    )(xbc, w, b)


def _ssd_conv_bwd(dxc, xbc, w, b, name):
    L = xbc.shape[0]

    def body(d_ref, x_ref, w_ref, b_ref, dx_ref, dw_ref, db_ref):
        xv = x_ref[...]
        shifted = [_shift_time(xv, SSD_CONV - 1 - k) for k in range(SSD_CONV)]
        pre = jnp.broadcast_to(b_ref[...], xv.shape)
        for k in range(SSD_CONV):
            pre = pre + w_ref[k:k + 1, :] * shifted[k]
        dpre = d_ref[...] * _dsilu(pre)
        dx = jnp.zeros_like(xv)
        rows = []
        for k in range(SSD_CONV):
            dx = dx + w_ref[k:k + 1, :] * _unshift_time(dpre, SSD_CONV - 1 - k)
            rows.append(jnp.sum(dpre * shifted[k], axis=0, keepdims=True))
        dx_ref[...] = dx
        dw_ref[...] = jnp.concatenate(rows + [jnp.zeros((8 - SSD_CONV, CONV_TILE), F32)], axis=0)
        db_ref[...] = jnp.sum(dpre, axis=0, keepdims=True)

    col = pl.BlockSpec((L, CONV_TILE), lambda j: (0, j))
    w_spec = pl.BlockSpec((8, CONV_TILE), lambda j: (0, j))
    b_spec = pl.BlockSpec((1, CONV_TILE), lambda j: (0, j))
    return pl.pallas_call(
        body, name=name, grid=(SSD_CONV_DIM // CONV_TILE,),
        in_specs=[col, col, w_spec, b_spec], out_specs=[col, w_spec, b_spec],
        out_shape=[jax.ShapeDtypeStruct((L, SSD_CONV_DIM), F32), jax.ShapeDtypeStruct((8, SSD_CONV_DIM), F32),
                   jax.ShapeDtypeStruct((1, SSD_CONV_DIM), F32)],
    )(dxc, xbc, w, b)


def _tri(lower):
    r = lax.broadcasted_iota(jnp.int32, (SSD_CHUNK, SSD_CHUNK), 0)
    c = lax.broadcasted_iota(jnp.int32, (SSD_CHUNK, SSD_CHUNK), 1)
    return (r >= c) if lower else (r <= c)


def _ssd_chunk_common(dt_ref, bias_ref, alog_ref):
    pre = dt_ref[...] + bias_ref[0]
    dtp = _softplus(pre)
    a_neg = -jnp.exp(alog_ref[0])
    dta = dtp * a_neg
    acum = _select_rows(_tri(True), dta)
    return pre, dtp, a_neg, dta, acum


GROUP_W = SSD_GROUP_HEADS * SSD_HEAD_DIM


def _head_expander():
    r = lax.broadcasted_iota(jnp.int32, (LANES, GROUP_W), 0)
    c = lax.broadcasted_iota(jnp.int32, (LANES, GROUP_W), 1)
    return (c // SSD_HEAD_DIM == r).astype(F32)


def _split_bf16(a, terms):
    parts = []
    rest = a
    for _ in range(terms):
        piece = rest.astype(BF16)
        parts.append(piece)
        rest = rest - piece.astype(F32)
    return parts


def _select_cols(a, sel, terms=3):
    lhs = jnp.concatenate(_split_bf16(a, terms), axis=1)
    rhs = jnp.concatenate([sel.astype(BF16)] * terms, axis=0)
    return _dot(lhs, rhs)


def _select_rows(sel, b, terms=3):
    lhs = jnp.concatenate([sel.astype(BF16)] * terms, axis=1)
    rhs = jnp.concatenate(_split_bf16(b, terms), axis=0)
    return _dot(lhs, rhs)


def _decay_mask(acum_all, acum_t, h, lower):
    seg = acum_all[:, h:h + 1] - acum_t[h:h + 1, :]
    return jnp.where(lower, jnp.exp(jnp.minimum(seg, 0.0)), 0.0)


def _ssd_scan(xc, dt, dt_bias, a_log, d_wide, expand, expand_t, name):
    L = xc.shape[0]
    nc = L // SSD_CHUNK

    def body(x_ref, b_ref, c_ref, dt_ref, bias_ref, alog_ref, d_ref, e_ref, et_ref, y_ref, sp_ref, s_ref, xdt_ref):
        @pl.when(pl.program_id(1) == 0)
        def _():
            s_ref[...] = jnp.zeros_like(s_ref)

        _, dtp_all, _, _, acum_all = _ssd_chunk_common(dt_ref, bias_ref, alog_ref)
        acum_t = acum_all.T
        wide = _select_cols(jnp.concatenate([acum_all, dtp_all], axis=0), e_ref[...])
        acum_e = wide[:SSD_CHUNK]
        alast_e = acum_e[SSD_CHUNK - 1:SSD_CHUNK, :]
        x = x_ref[...]
        xdt = x * wide[SSD_CHUNK:]
        xdt_ref[...] = xdt.astype(BF16)
        bm = b_ref[...].astype(BF16)
        cm = c_ref[...].astype(BF16)
        cb = _dot_nt(cm, bm)
        lower = _tri(True)
        sp = s_ref[...]
        for h in range(SSD_GROUP_HEADS):
            cols = slice(h * SSD_HEAD_DIM, (h + 1) * SSD_HEAD_DIM)
            lm = _decay_mask(acum_all, acum_t, h, lower)
            y_ref[:, cols] = _dot((cb * lm).astype(BF16), xdt_ref[:, cols])
        y_ref[...] += _dot_nt(cm, sp.astype(BF16)) * jnp.exp(acum_e) + d_ref[0] * x
        wgt = xdt * jnp.exp(alast_e - acum_e)
        ealast = jnp.exp(_select_rows(et_ref[...], acum_t)[:, SSD_CHUNK - 1:SSD_CHUNK])
        sp_ref[0, 0] = sp
        s_ref[...] = ealast * sp + _dot_tn(wgt.astype(BF16), bm)

    par = lambda n: pl.BlockSpec((1, 1, n), lambda g, c: (g, 0, 0))
    return pl.pallas_call(
        body, name=name, grid=(SSD_GROUPS, nc),
        in_specs=[pl.BlockSpec((SSD_CHUNK, GROUP_W), lambda g, c: (c, g)),
                  pl.BlockSpec((SSD_CHUNK, SSD_STATE), lambda g, c: (c, 8 + g)),
                  pl.BlockSpec((SSD_CHUNK, SSD_STATE), lambda g, c: (c, 10 + g)),
                  pl.BlockSpec((SSD_CHUNK, LANES), lambda g, c: (c, g)),
                  par(LANES), par(LANES), par(GROUP_W), _full_spec((LANES, GROUP_W)), _full_spec((GROUP_W, LANES))],
        out_specs=[pl.BlockSpec((SSD_CHUNK, GROUP_W), lambda g, c: (c, g)),
                   pl.BlockSpec((1, 1, GROUP_W, SSD_STATE), lambda g, c: (c, g, 0, 0))],
        out_shape=[jax.ShapeDtypeStruct((L, SSD_WIDTH), F32),
                   jax.ShapeDtypeStruct((nc, SSD_GROUPS, GROUP_W, SSD_STATE), F32)],
        scratch_shapes=[pltpu.VMEM((GROUP_W, SSD_STATE), F32), pltpu.VMEM((SSD_CHUNK, GROUP_W), BF16)],
    )(xc, xc, xc, dt, dt_bias, a_log, d_wide, expand, expand_t)


def _ssd_scan_bwd(dy, xc, dt, sprev, dt_bias, a_log, d_wide, expand, expand_t, name):
    L = xc.shape[0]
    nc = L // SSD_CHUNK

    def body(dy_ref, x_ref, b_ref, c_ref, dt_ref, sp_ref, bias_ref, alog_ref, d_ref, e_ref, et_ref,
             dx_ref, db_ref, dc_ref, ddt_ref, dbias_ref, dalog_ref, dd_ref, ds_ref, xdt_ref, dyb_ref):
        first = pl.program_id(1) == 0

        @pl.when(first)
        def _():
            ds_ref[...] = jnp.zeros_like(ds_ref)

        pre, dtp_all, a_neg, _, acum_all = _ssd_chunk_common(dt_ref, bias_ref, alog_ref)
        acum_t = acum_all.T
        e = e_ref[...]
        et = et_ref[...]
        wide = _select_cols(jnp.concatenate([acum_all, dtp_all], axis=0), e)
        acum_e = wide[:SSD_CHUNK]
        dtp_e = wide[SSD_CHUNK:]
        alast_e = acum_e[SSD_CHUNK - 1:SSD_CHUNK, :]
        dstate_e = jnp.exp(alast_e - acum_e)
        x = x_ref[...]
        dy = dy_ref[...]
        xdt = x * dtp_e
        xdt_ref[...] = xdt.astype(BF16)
        dyb_ref[...] = dy.astype(BF16)
        bm = b_ref[...].astype(BF16)
        cm = c_ref[...].astype(BF16)
        cb = _dot_nt(cm, bm)
        sp = sp_ref[0, 0]
        spb = sp.astype(BF16)
        dsn = ds_ref[...]
        dsb = dsn.astype(BF16)
        z = _dot_nt(cm, spb)
        dz = dy * jnp.exp(acum_e)
        dzb = dz.astype(BF16)
        dc_acc = _dot(dzb, spb)
        ealast = jnp.exp(_select_rows(et, acum_t)[:, SSD_CHUNK - 1:SSD_CHUNK])
        ds_ref[...] = _dot_tn(dzb, cm) + ealast * dsn
        dw = _dot_nt(bm, dsb)
        wgt = xdt * dstate_e
        db_acc = _dot(wgt.astype(BF16), dsb)
        lower = _tri(True)
        lane = lax.broadcasted_iota(jnp.int32, (SSD_CHUNK, LANES), 1)
        row = lax.broadcasted_iota(jnp.int32, (SSD_CHUNK, LANES), 0)
        dcb = jnp.zeros((SSD_CHUNK, SSD_CHUNK), F32)
        dacum_all = jnp.zeros((SSD_CHUNK, LANES), F32)
        dacum_cols = jnp.zeros((SSD_CHUNK, LANES), F32)
        for h in range(SSD_GROUP_HEADS):
            cols = slice(h * SSD_HEAD_DIM, (h + 1) * SSD_HEAD_DIM)
            lm = _decay_mask(acum_all, acum_t, h, lower)
            dm = _dot_nt(dyb_ref[:, cols], xdt_ref[:, cols])
            dx_ref[:, cols] = _dot_tn((cb * lm).astype(BF16), dyb_ref[:, cols])
            dm_lm = dm * lm
            dcb = dcb + dm_lm
            q = dm_lm * cb
            dacum_all = jnp.where(lane == h, jnp.sum(q, axis=1, keepdims=True), dacum_all)
            dacum_cols = jnp.where(row == h, jnp.sum(q, axis=0, keepdims=True), dacum_cols)
        dxdt = dx_ref[...] + dw * dstate_e
        sums = _select_cols(jnp.concatenate([dz * z, dw * wgt, dxdt * x, dy * x], axis=0), et, terms=2)
        dacum_off = sums[0:SSD_CHUNK]
        dds_ds = sums[SSD_CHUNK:2 * SSD_CHUNK]
        ddtp_x = sums[2 * SSD_CHUNK:3 * SSD_CHUNK]
        dd_part = sums[3 * SSD_CHUNK:4 * SSD_CHUNK]
        ds_s = jnp.sum(_select_rows(e, dsn * sp, terms=2).T, axis=0, keepdims=True)
        dalast = ds_s * jnp.exp(acum_all[SSD_CHUNK - 1:SSD_CHUNK, :]) + jnp.sum(dds_ds, axis=0, keepdims=True)
        dacum_all = dacum_all - dacum_cols.T + dacum_off - dds_ds + jnp.where(row == SSD_CHUNK - 1, dalast, 0.0)
        dx_ref[...] = d_ref[0] * dy + dxdt * dtp_e
        dcbb = dcb.astype(BF16)
        dc_ref[...] = dc_acc + _dot(dcbb, bm)
        db_ref[...] = db_acc + _dot_tn(dcbb, cm)
        ddta = _select_rows(_tri(False), dacum_all)
        ddt = (ddtp_x + ddta * a_neg) * _sigmoid(pre)
        ddt_ref[...] = ddt
        _acc_rows(dbias_ref, jnp.sum(ddt, axis=0, keepdims=True)[None], first)
        _acc_rows(dalog_ref, (jnp.sum(ddta * dtp_all, axis=0, keepdims=True) * a_neg)[None], first)
        _acc_rows(dd_ref, jnp.sum(dd_part, axis=0, keepdims=True)[None], first)

    rev = lambda c: nc - 1 - c
    par = lambda n: pl.BlockSpec((1, 1, n), lambda g, c: (g, 0, 0))
    pshape = jax.ShapeDtypeStruct((SSD_GROUPS, 1, LANES), F32)
    return pl.pallas_call(
        body, name=name, grid=(SSD_GROUPS, nc),
        in_specs=[pl.BlockSpec((SSD_CHUNK, GROUP_W), lambda g, c: (rev(c), g)),
                  pl.BlockSpec((SSD_CHUNK, GROUP_W), lambda g, c: (rev(c), g)),
                  pl.BlockSpec((SSD_CHUNK, SSD_STATE), lambda g, c: (rev(c), 8 + g)),
                  pl.BlockSpec((SSD_CHUNK, SSD_STATE), lambda g, c: (rev(c), 10 + g)),
                  pl.BlockSpec((SSD_CHUNK, LANES), lambda g, c: (rev(c), g)),
                  pl.BlockSpec((1, 1, GROUP_W, SSD_STATE), lambda g, c: (rev(c), g, 0, 0)),
                  par(LANES), par(LANES), par(GROUP_W), _full_spec((LANES, GROUP_W)), _full_spec((GROUP_W, LANES))],
        out_specs=[pl.BlockSpec((SSD_CHUNK, GROUP_W), lambda g, c: (rev(c), g)),
                   pl.BlockSpec((SSD_CHUNK, SSD_STATE), lambda g, c: (rev(c), g)),
                   pl.BlockSpec((SSD_CHUNK, SSD_STATE), lambda g, c: (rev(c), g)),
                   pl.BlockSpec((SSD_CHUNK, LANES), lambda g, c: (rev(c), g)),
                   par(LANES), par(LANES), par(LANES)],
        out_shape=[jax.ShapeDtypeStruct((L, SSD_WIDTH), F32), jax.ShapeDtypeStruct((L, 256), F32),
                   jax.ShapeDtypeStruct((L, 256), F32), jax.ShapeDtypeStruct((L, 256), F32),
                   pshape, pshape, pshape],
        scratch_shapes=[pltpu.VMEM((GROUP_W, SSD_STATE), F32), pltpu.VMEM((SSD_CHUNK, GROUP_W), BF16),
                        pltpu.VMEM((SSD_CHUNK, GROUP_W), BF16)],
    )(dy, xc, xc, xc, dt, sprev, dt_bias, a_log, d_wide, expand, expand_t)


def _ssd_post(y, z, gain, yab, name):
    L = y.shape[0]

    def body(y_ref, z_ref, g_ref, yab_ref, o_ref):
        ob = y_ref[...] * _silu(z_ref[...])
        o_ref[...] = (ob * _rstd(ob) * g_ref[...]).astype(BF16)

    tile = min(L, WIDE_ROW_TILE)
    row = _row_spec(tile, 1024)
    return pl.pallas_call(body, name=name, grid=(L // tile,), in_specs=[row, row, _full_spec((1, 1024)), ANY],
                          out_specs=pl.BlockSpec((tile, 1024), lambda i: (i, 1)), input_output_aliases={3: 0},
                          out_shape=jax.ShapeDtypeStruct((L, 2048), BF16))(y, z, gain, yab)


def _ssd_post_bwd(dx, w_out, y, z, gain, layer, name):
    L = y.shape[0]

    def body(dx_ref, wo_ref, y_ref, z_ref, g_ref, dy_ref, dz_ref, dgain_ref):
        first = pl.program_id(0) == 0
        yv = y_ref[...]
        zv = z_ref[...]
        sz = _silu(zv)
        ob = yv * sz
        dyb = _dot_nt(dx_ref[...], wo_ref[...])
        dob, dgain = _rms_bwd(ob, _rstd(ob), g_ref[...], dyb)
        dy_ref[...] = dob * sz
        dz_ref[...] = dob * yv * _dsilu(zv)
        _acc_rows(dgain_ref, dgain, first)

    row = _row_spec(ROW_TILE, 1024)
    vec = _full_spec((1, 1024))
    return pl.pallas_call(
        body, name=name, grid=(L // ROW_TILE,),
        in_specs=[row, _layer_spec((1024, 1024), layer, 1), row, row, vec],
        out_specs=[row, row, vec],
        out_shape=[jax.ShapeDtypeStruct((L, 1024), F32), jax.ShapeDtypeStruct((L, 1024), F32),
                   jax.ShapeDtypeStruct((1, 1024), F32)],
    )(dx, w_out, y, z, gain)


def _out_proj(x, yab, w_out, layer, name):
    L = x.shape[0]

    def body(x_ref, yab_ref, w_ref, o_ref):
        o_ref[...] = x_ref[...] + _dot(yab_ref[...], w_ref[...])

    tile = min(L, WIDE_ROW_TILE)
    row = _row_spec(tile, 1024)
    return pl.pallas_call(body, name=name, grid=(L // tile,),
                          in_specs=[row, _row_spec(tile, 2048), _layer_spec((2048, 1024), layer)],
                          out_specs=row, out_shape=jax.ShapeDtypeStruct((L, D_MODEL), F32))(x, yab, w_out)


def _ffn(x, gain, w_gate, w_up, w_down, layer, name):
    L = x.shape[0]

    def body(x_ref, g_ref, wg_ref, wu_ref, wd_ref, o_ref, gt_ref, up_ref):
        xv = x_ref[...]
        h = (xv * _rstd(xv) * g_ref[...]).astype(BF16)
        gt = _dot_nt(h, wg_ref[...])
        up = _dot_nt(h, wu_ref[...])
        gt_ref[...] = gt
        up_ref[...] = up
        o_ref[...] = xv + _dot((_silu(gt) * up).astype(BF16), wd_ref[...])

    row = _row_spec(ROW_TILE, D_MODEL)
    hid = _row_spec(ROW_TILE, FFN)
    return pl.pallas_call(
        body, name=name, grid=(L // ROW_TILE,),
        in_specs=[row, _full_spec((1, D_MODEL)), _layer_spec((FFN, D_MODEL), layer),
                  _layer_spec((FFN, D_MODEL), layer), _layer_spec((FFN, D_MODEL), layer)],
        out_specs=[row, hid, hid],
        out_shape=[jax.ShapeDtypeStruct((L, D_MODEL), F32), jax.ShapeDtypeStruct((L, FFN), F32),
                   jax.ShapeDtypeStruct((L, FFN), F32)],
    )(x, gain, w_gate, w_up, w_down)


def _ffn_bwd(dx2, x1, gt, up, gain, w_gate, w_up, w_down, layer, name):
    L = x1.shape[0]

    def body(d_ref, x_ref, gt_ref, up_ref, g_ref, wg_ref, wu_ref, wd_ref,
             dx_ref, dxb_ref, h_ref, act_ref, dgt_ref, dup_ref, dgain_ref):
        first = pl.program_id(0) == 0
        dv = d_ref[...]
        xv = x_ref[...]
        r = _rstd(xv)
        h_ref[...] = (xv * r * g_ref[...]).astype(BF16)
        gtv = gt_ref[...]
        upv = up_ref[...]
        sg = _silu(gtv)
        act_ref[...] = (sg * upv).astype(BF16)
        dact = _dot_nt(dv.astype(BF16), wd_ref[...])
        dgt = (dact * upv * _dsilu(gtv)).astype(BF16)
        dup = (dact * sg).astype(BF16)
        dgt_ref[...] = dgt
        dup_ref[...] = dup
        dh = _dot(dgt, wg_ref[...]) + _dot(dup, wu_ref[...])
        dxn, dgain = _rms_bwd(xv, r, g_ref[...], dh)
        dx = dv + dxn
        dx_ref[...] = dx
        dxb_ref[...] = dx.astype(BF16)
        _acc_rows(dgain_ref, dgain, first)

    row = _row_spec(ROW_TILE, D_MODEL)
    hid = _row_spec(ROW_TILE, FFN)
    vec = _full_spec((1, D_MODEL))
    return pl.pallas_call(
        body, name=name, grid=(L // ROW_TILE,),
        in_specs=[row, row, hid, hid, vec, _layer_spec((FFN, D_MODEL), layer), _layer_spec((FFN, D_MODEL), layer),
                  _layer_spec((FFN, D_MODEL), layer)],
        out_specs=[row, row, row, hid, hid, hid, vec],
        out_shape=[jax.ShapeDtypeStruct((L, D_MODEL), F32), jax.ShapeDtypeStruct((L, D_MODEL), BF16),
                   jax.ShapeDtypeStruct((L, D_MODEL), BF16),
                   jax.ShapeDtypeStruct((L, FFN), BF16), jax.ShapeDtypeStruct((L, FFN), BF16),
                   jax.ShapeDtypeStruct((L, FFN), BF16), jax.ShapeDtypeStruct((1, D_MODEL), F32)],
    )(dx2, x1, gt, up, gain, w_gate, w_up, w_down)


def _inproj_bwd(dx1, x0, du_skip, du_scan, dz, dxbc, ddt, gain, w_pad, layer, name):
    L = x0.shape[0]

    def body(d_ref, x_ref, dus_ref, duc_ref, dz_ref, dxbc_ref, ddt_ref, g_ref, w_ref,
             dx_ref, dxb_ref, dp_ref, dgain_ref):
        first = pl.program_id(0) == 0
        xv = x_ref[...]
        dp = jnp.concatenate([dus_ref[...] + duc_ref[...], dz_ref[...], dxbc_ref[...], ddt_ref[...]],
                             axis=1).astype(BF16)
        dp_ref[...] = dp
        dh = _dot_nt(dp, w_ref[...])
        dxn, dgain = _rms_bwd(xv, _rstd(xv), g_ref[...], dh)
        dx = d_ref[...] + dxn
        dx_ref[...] = dx
        dxb_ref[...] = dx.astype(BF16)
        _acc_rows(dgain_ref, dgain, first)

    row = _row_spec(ROW_TILE, D_MODEL)
    vec = _full_spec((1, D_MODEL))
    return pl.pallas_call(
        body, name=name, grid=(L // ROW_TILE,),
        in_specs=[row, row, row, row, row, _row_spec(ROW_TILE, SSD_CONV_DIM), _row_spec(ROW_TILE, 256), vec,
                  _layer_spec((D_MODEL, IN_PAD), layer)],
        out_specs=[row, row, _row_spec(ROW_TILE, IN_PAD), vec],
        out_shape=[jax.ShapeDtypeStruct((L, D_MODEL), F32), jax.ShapeDtypeStruct((L, D_MODEL), BF16),
                   jax.ShapeDtypeStruct((L, IN_PAD), BF16), jax.ShapeDtypeStruct((1, D_MODEL), F32)],
    )(dx1, x0, du_skip, du_scan, dz, dxbc, ddt, gain, w_pad)


def _final_loss(x, gain, target, name):
    L = x.shape[0]

    def body(x_ref, g_ref, t_ref, loss_ref, dx_ref, dxb_ref, dgain_ref):
        first = pl.program_id(0) == 0
        xv = x_ref[...]
        r = _rstd(xv)
        err = xv * r * g_ref[...] - t_ref[...]
        part = 0.5 * jnp.sum(jnp.mean(err * err, axis=-1, keepdims=True), axis=0, keepdims=True)
        dx, dgain = _rms_bwd(xv, r, g_ref[...], err * (1.0 / D_MODEL))
        dx_ref[...] = dx
        dxb_ref[...] = dx.astype(BF16)
        _acc_rows(loss_ref, jnp.broadcast_to(part, (1, LANES)), first)
        _acc_rows(dgain_ref, dgain, first)

    row = _row_spec(ROW_TILE, D_MODEL)
    vec = _full_spec((1, D_MODEL))
    return pl.pallas_call(
        body, name=name, grid=(L // ROW_TILE,), in_specs=[row, vec, row],
        out_specs=[_full_spec((1, LANES)), row, row, vec],
        out_shape=[jax.ShapeDtypeStruct((1, LANES), F32), jax.ShapeDtypeStruct((L, D_MODEL), F32),
                   jax.ShapeDtypeStruct((L, D_MODEL), BF16), jax.ShapeDtypeStruct((1, D_MODEL), F32)],
    )(x, gain, target)


def _to_segments(a):
    L, n = a.shape
    return a.reshape(SEGS, L // SEGS, n).transpose(1, 0, 2).reshape(L, n)


def _from_segments(a):
    L, n = a.shape
    return a.reshape(L // SEGS, SEGS, n).transpose(1, 0, 2).reshape(L, n)


def _diag_block(g):
    k, a = divmod(g, S5_TILE_GROUPS)
    return k, slice(a * S5_GROUP, (a + 1) * S5_GROUP), slice(a * S5_STATE, (a + 1) * S5_STATE)


def _block_diag_build(mats, name):
    n = mats.shape[0]

    def body(m_ref, o_ref):
        o_ref[...] = jnp.zeros(o_ref.shape, BF16)
        for q in range(n):
            for g in range(S5_GROUPS):
                k, rows, cols = _diag_block(g)
                o_ref[q, k, rows, cols] = m_ref[q, g].astype(BF16)

    return pl.pallas_call(body, name=name,
                          out_shape=jax.ShapeDtypeStruct((n, S5_TILES, S5_TILE_IN, S5_TILE_ST), BF16))(mats)


def _block_diag_extract(dense, name):
    n = len(dense)

    def body(*refs):
        o_ref = refs[n]
        for q in range(n):
            for g in range(S5_GROUPS):
                k, rows, cols = _diag_block(g)
                o_ref[q, g] = refs[q][k, rows, cols]

    return pl.pallas_call(body, name=name,
                          out_shape=jax.ShapeDtypeStruct((n, S5_GROUPS, S5_GROUP, S5_STATE), F32))(*dense)


def _pad_in_proj(w):
    z = jnp.zeros(w.shape[:-1] + (LANES - SSD_GROUP_HEADS,), w.dtype)
    return jnp.concatenate([w[..., :IN_MAIN + 8], z, w[..., IN_MAIN + 8:], z], axis=-1)


def _unpad_in_proj(w):
    return jnp.concatenate([w[..., :IN_MAIN + 8], w[..., IN_MAIN + LANES:IN_MAIN + LANES + 8]], axis=-1)


def _lane_dense(a):
    return a.reshape(DEPTH, D_MODEL // LANES, LANES, -1).transpose(3, 1, 0, 2).reshape(-1, LANES)


def _from_lane_dense(a):
    return a.reshape(-1, D_MODEL // LANES, DEPTH, LANES).transpose(2, 1, 3, 0).reshape(DEPTH, D_MODEL, -1)


def _pad_heads(v):
    v = v.reshape(SSD_GROUPS, 1, SSD_GROUP_HEADS)
    return jnp.pad(v, ((0, 0), (0, 0), (0, LANES - SSD_GROUP_HEADS)))


def _unpad_heads(v):
    return v[:, 0, :SSD_GROUP_HEADS].reshape(SSD_HEADS)


def _layer_forward(x0, p, big, i, after_inproj=None, before_s5_post=None, before_ffn=None):
    tag = "l%d_" % i
    ls = p["s5_log_step"].reshape(S5_GROUPS, 1)
    b_hgp = (p["s5_b_re"].transpose(2, 0, 1), p["s5_b_im"].transpose(2, 0, 1))
    are, aim, bbre, bbim = _s5_prep(p["s5_lam_re"], p["s5_lam_im"], ls, b_hgp[0], b_hgp[1], tag + "s5_prep")
    mats = jnp.stack([bbre.transpose(1, 0, 2), bbim.transpose(1, 0, 2), p["s5_c_re"], p["s5_c_im"]])
    bre_bd, bim_bd, cret_bd, cimt_bd = _block_diag_build(mats, tag + "s5_blockdiag")
    s5mats = dict(bre_bd=bre_bd, bim_bd=bim_bd, cret_bd=cret_bd, cimt_bd=cimt_bd,
                  bret_bd=bre_bd.transpose(0, 2, 1), bimt_bd=bim_bd.transpose(0, 2, 1),
                  cre_bd=cret_bd.transpose(0, 2, 1), cim_bd=cimt_bd.transpose(0, 2, 1),
                  are=are.reshape(S5_TILES, 1, S5_TILE_ST), aim=aim.reshape(S5_TILES, 1, S5_TILE_ST))

    u, z, xbc, dt, h1 = _rms_inproj(x0, p["norm_mix"].reshape(1, -1), big["w_in"], None, tag + "rms_inproj")
    if after_inproj is not None:
        after_inproj(u)
    u_perm = _to_segments(u)
    ys_perm, xr, xi = _s5_scan(u_perm, bre_bd, bim_bd, s5mats["cre_bd"], s5mats["cim_bd"],
                               s5mats["are"], s5mats["aim"], tag + "s5_scan")
    ys = _from_segments(ys_perm)
    late_matrices = before_s5_post(ys) if before_s5_post is not None else {}
    big = {**big, **late_matrices}
    yab = _s5_post(ys, u, p["s5_d"].reshape(1, -1), big["s5_w_glu"], p["s5_b_glu"].reshape(1, -1),
                   p["s5_norm"].reshape(1, -1), None, tag + "s5_post")

    conv_w = jnp.pad(p["ssd_conv_w"], ((0, 8 - SSD_CONV), (0, 0)))
    conv_b = p["ssd_conv_b"].reshape(1, -1)
    xc = _ssd_conv(xbc, conv_w, conv_b, tag + "ssd_conv")
    expand = _head_expander()
    heads = dict(dt_bias=_pad_heads(p["ssd_dt_bias"]), a_log=_pad_heads(p["ssd_a_log"]),
                 d=jnp.repeat(p["ssd_d"], SSD_HEAD_DIM).reshape(SSD_GROUPS, 1, GROUP_W),
                 expand=expand, expand_t=expand.T)
    y, sprev = _ssd_scan(xc, dt, heads["dt_bias"], heads["a_log"], heads["d"], expand, heads["expand_t"],
                         tag + "ssd_scan")
    yab = _ssd_post(y, z, p["ssd_norm"].reshape(1, -1), yab, tag + "ssd_post")

    x1 = _out_proj(x0, yab, big["w_out"], None, tag + "out_proj")
    ffn_matrices = before_ffn(x1) if before_ffn is not None else {}
    big = {**big, **ffn_matrices}
    late_matrices = {**late_matrices, **ffn_matrices}
    x2, gt, up = _ffn(x1, p["norm_ffn"].reshape(1, -1), big["w_gate"], big["w_up"], big["w_down"], None,
                      tag + "ffn")
    saved = dict(x0=x0, h1=h1, u=u, u_perm=u_perm, z=z, xbc=xbc, dt=dt, xr=xr, xi=xi, ys=ys, yab=yab, xc=xc, y=y,
                 sprev=sprev, x1=x1, gt=gt, up=up, s5mats=s5mats, heads=heads, conv_w=conv_w,
                 conv_b=conv_b, ls=ls, b_hgp=b_hgp, late_matrices=late_matrices)
    return x2, saved


def _layer_backward(dx2, dx2b, p, big, s, i, after_ffn_grads=None, after_s5_grads=None):
    tag = "l%d_" % i
    g = {}
    dx1, dx1b, h2, act, dgt, dup, dgain = _ffn_bwd(dx2, s["x1"], s["gt"], s["up"], p["norm_ffn"].reshape(1, -1),
                                                  big["w_gate"], big["w_up"], big["w_down"], None, tag + "ffn_bwd")
    g["norm_ffn"] = dgain[0]
    g["w_down"] = _mm_tn(act, dx2b, tag + "dw_down")
    g["w_gate"] = _mm_tn(dgt, h2, tag + "dw_gate")
    g["w_up"] = _mm_tn(dup, h2, tag + "dw_up")
    g["w_out"] = _mm_tn(s["yab"], dx1b, tag + "dw_out")
    if after_ffn_grads is not None:
        p = {**p, "s5_norm": p["s5_norm"] + after_ffn_grads(g)[0, 0]}

    dys, du_skip, gelu_b, dq_b, dgain, dd, dbg = _s5_post_bwd(
        dx1b, big["w_out"], s["ys"], s["u"], p["s5_d"].reshape(1, -1), big["s5_w_glu"],
        p["s5_b_glu"].reshape(1, -1), p["s5_norm"].reshape(1, -1), None, tag + "s5_post_bwd")
    g["s5_norm"] = dgain[0]
    g["s5_d"] = dd[0]
    g["s5_b_glu"] = dbg[0]
    g["s5_w_glu"] = _mm_tn(gelu_b, dq_b, tag + "dw_glu")
    m = s["s5mats"]
    du_perm, dar, dai, dcre_d, dcim_d, dbre_d, dbim_d = _s5_scan_bwd(
        _to_segments(dys), s["u_perm"], s["xr"], s["xi"], m["bret_bd"], m["bimt_bd"], m["cret_bd"], m["cimt_bd"],
        m["are"], m["aim"], tag + "s5_scan_bwd")
    du_scan = _from_segments(du_perm)
    diag = _block_diag_extract([dcre_d, dcim_d, dbre_d, dbim_d], tag + "s5_blockdiag_bwd")
    g["s5_c_re"], g["s5_c_im"] = diag[0], diag[1]
    dbbre = diag[2].transpose(1, 0, 2)
    dbbim = diag[3].transpose(1, 0, 2)
    dlr, dli, dls, dbre, dbim = _s5_prep_bwd(
        p["s5_lam_re"], p["s5_lam_im"], s["ls"], s["b_hgp"][0], s["b_hgp"][1],
        dar.reshape(S5_GROUPS, S5_STATE), dai.reshape(S5_GROUPS, S5_STATE), dbbre, dbbim, tag + "s5_prep_bwd")
    g["s5_lam_re"] = dlr
    g["s5_lam_im"] = dli
    g["s5_log_step"] = dls[:, 0]
    g["s5_b_re"] = dbre
    g["s5_b_im"] = dbim
    if after_s5_grads is not None:
        p = {**p, "ssd_norm": p["ssd_norm"] + after_s5_grads(g)[0, 0]}

    dy, dz, dgain = _ssd_post_bwd(dx1b, big["w_out"], s["y"], s["z"], p["ssd_norm"].reshape(1, -1), None,
                                  tag + "ssd_post_bwd")
    g["ssd_norm"] = dgain[0]
    hd = s["heads"]
    dxs, dbm, dcm, ddt, dbias, dalog, dd = _ssd_scan_bwd(dy, s["xc"], s["dt"], s["sprev"], hd["dt_bias"],
                                                       hd["a_log"], hd["d"], hd["expand"], hd["expand_t"],
                                                       tag + "ssd_scan_bwd")
    g["ssd_dt_bias"] = _unpad_heads(dbias)
    g["ssd_a_log"] = _unpad_heads(dalog)
    g["ssd_d"] = _unpad_heads(dd)
    dxc = jnp.concatenate([dxs, dbm, dcm], axis=1)
    dxbc, dcw, dcb = _ssd_conv_bwd(dxc, s["xbc"], s["conv_w"], s["conv_b"], tag + "ssd_conv_bwd")
    g["ssd_conv_w"] = dcw[:SSD_CONV]
    g["ssd_conv_b"] = dcb[0]

    dx0, dx0b, dproj, dgain = _inproj_bwd(dx1, s["x0"], du_skip, du_scan, dz, dxbc, ddt, p["norm_mix"].reshape(1, -1),
                                          big["w_in"], None, tag + "inproj_bwd")
    g["norm_mix"] = dgain[0]
    g["w_in"] = _mm_tn(s["h1"], dproj, tag + "dw_in")
    return dx0, dx0b, g


MIXER_BIG = ("w_in", "s5_w_glu", "w_out")
FFN_BIG = ("w_gate", "w_up", "w_down")
BIG = MIXER_BIG + FFN_BIG
COL_SHARDED = ("w_in",)
T_STORED = ("w_gate", "w_up")
LAYER_SMALL = ("norm_mix", "s5_lam_re", "s5_lam_im", "s5_log_step", "s5_b_re", "s5_b_im", "s5_c_re", "s5_c_im",
               "s5_d", "s5_b_glu", "s5_norm", "ssd_conv_w", "ssd_conv_b", "ssd_dt_bias", "ssd_a_log", "ssd_d",
               "ssd_norm", "norm_ffn")
WEIGHTS = ("norm_mix", "w_in", "s5_lam_re", "s5_lam_im", "s5_log_step", "s5_b_re", "s5_b_im", "s5_c_re", "s5_c_im",
           "s5_d", "s5_w_glu", "s5_b_glu", "s5_norm", "ssd_conv_w", "ssd_conv_b", "ssd_dt_bias", "ssd_a_log",
           "ssd_d", "ssd_norm", "w_out", "norm_ffn", "w_gate", "w_up", "w_down", "norm_final")


S5_BC = ("s5_b_re", "s5_b_im", "s5_c_re", "s5_c_im")
TINY = tuple(k for k in LAYER_SMALL if k not in S5_BC)


def _my_place():
    return lax.axis_index("x"), lax.axis_index("y"), lax.axis_index("c")


HBM = pl.BlockSpec(memory_space=pltpu.HBM)
SEM = pl.BlockSpec(memory_space=pltpu.SEMAPHORE)
DATAFLOW = pltpu.SideEffectType.DATAFLOW_SIDE_EFFECTING


def _in_hbm(a):
    return pltpu.with_memory_space_constraint(a, pltpu.HBM)


TOKEN = jax.ShapeDtypeStruct((8, LANES), F32)
VMEM_SPEC = pl.BlockSpec(memory_space=pltpu.VMEM)


def _gather_start(blocks, after, name):
    nt = len(blocks)

    def body(*refs):
        ins = refs[:nt]
        lands = refs[nt:2 * nt]
        send_sems, recv_sems = refs[2 * nt + 1:2 * nt + 3]
        refs[-1][...] = jnp.zeros(TOKEN.shape, F32)
        x, y, c = _my_place()
        me = 4 * x + 2 * y + c
        peers = [(x, y, 1 - c), (1 - x, y, c), (x, 1 - y, c), (1 - x, 1 - y, c)]
        for t in range(nt):
            for k, peer in enumerate(peers):
                pltpu.make_async_remote_copy(src_ref=ins[t], dst_ref=lands[t].at[me], send_sem=send_sems.at[4 * t + k],
                                             recv_sem=recv_sems.at[4 * t + k], device_id=peer,
                                             device_id_type=MESH).start()

    lands = [_in_hbm(lax.empty((8,) + b.shape, b.dtype)) for b in blocks]
    out = pl.pallas_call(
        body, name=name, in_specs=[HBM] * (2 * nt) + [ANY],
        out_shape=(pltpu.SemaphoreType.DMA((4 * nt,)), pltpu.SemaphoreType.DMA((4 * nt,)),
                   *[pltpu.HBM(b.shape, b.dtype) for b in blocks],
                   *[pltpu.HBM((8,) + b.shape, b.dtype) for b in blocks], TOKEN),
        out_specs=(SEM, SEM, *[HBM] * (2 * nt), VMEM_SPEC),
        input_output_aliases={i: 2 + i for i in range(2 * nt)},
        compiler_params=pltpu.CompilerParams(has_side_effects=DATAFLOW),
    )(*[_in_hbm(b) for b in blocks], *lands, after)
    return out[:2], list(out[2:2 + nt]), list(out[2 + nt:2 + 2 * nt]), out[-1]


def _gather_forward(sems, blocks, lands, after, name):
    nt = len(blocks)

    def body(*refs):
        ins = refs[:nt]
        lands_in = refs[nt:2 * nt]
        send1, recv1 = refs[2 * nt:2 * nt + 2]
        send2, recv2 = refs[2 * nt + 3:2 * nt + 5]
        x, y, c = _my_place()
        me = 4 * x + 2 * y + c
        sibling = (x, y, 1 - c)
        sources = [4 * x + 2 * y + (1 - c), 4 * (1 - x) + 2 * y + c, 4 * x + 2 * (1 - y) + c,
                   4 * (1 - x) + 2 * (1 - y) + c]
        for t in range(nt):
            for k, src in enumerate(sources):
                cp = pltpu.make_async_remote_copy(src_ref=ins[t], dst_ref=lands_in[t].at[src],
                                                  send_sem=send1.at[4 * t + k], recv_sem=recv1.at[4 * t + k],
                                                  device_id=sibling, device_id_type=MESH)
                cp.wait_send()
                cp.wait_recv()
            for k, src in enumerate(sources[1:]):
                pltpu.make_async_remote_copy(src_ref=lands_in[t].at[src], dst_ref=lands_in[t].at[src],
                                             send_sem=send2.at[3 * t + k], recv_sem=recv2.at[3 * t + k],
                                             device_id=sibling, device_id_type=MESH).start()

    out = pl.pallas_call(
        body, name=name, in_specs=[HBM] * (2 * nt) + [SEM, SEM, pl.BlockSpec(memory_space=pl.ANY)],
        out_shape=(pltpu.SemaphoreType.DMA((3 * nt,)), pltpu.SemaphoreType.DMA((3 * nt,)),
                   *[pltpu.HBM(b.shape, b.dtype) for b in blocks],
                   *[pltpu.HBM(a.shape, a.dtype) for a in lands]),
        out_specs=(SEM, SEM, *[HBM] * (2 * nt)),
        input_output_aliases={i: 2 + i for i in range(2 * nt)},
        compiler_params=pltpu.CompilerParams(has_side_effects=DATAFLOW),
    )(*blocks, *lands, *sems, after)
    return out[:2], list(out[2 + nt:])


def _gather_finish(sems, lands, after, name):
    nt = len(lands)

    def body(*refs):
        lands_in = refs[:nt]
        send2, recv2 = refs[nt:nt + 2]
        x, y, c = _my_place()
        sibling = (x, y, 1 - c)
        mine = [4 * (1 - x) + 2 * y + c, 4 * x + 2 * (1 - y) + c, 4 * (1 - x) + 2 * (1 - y) + c]
        theirs = [4 * (1 - x) + 2 * y + 1 - c, 4 * x + 2 * (1 - y) + 1 - c, 4 * (1 - x) + 2 * (1 - y) + 1 - c]
        for t in range(nt):
            for k in range(3):
                cp = pltpu.make_async_remote_copy(src_ref=lands_in[t].at[mine[k]], dst_ref=lands_in[t].at[theirs[k]],
                                                  send_sem=send2.at[3 * t + k], recv_sem=recv2.at[3 * t + k],
                                                  device_id=sibling, device_id_type=MESH)
                cp.wait_send()
                cp.wait_recv()

    out = pl.pallas_call(
        body, name=name, in_specs=[HBM] * nt + [SEM, SEM, pl.BlockSpec(memory_space=pl.ANY)],
        out_shape=tuple(pltpu.HBM(a.shape, a.dtype) for a in lands), out_specs=tuple([HBM] * nt),
        input_output_aliases={i: i for i in range(nt)},
        compiler_params=pltpu.CompilerParams(has_side_effects=DATAFLOW),
    )(*lands, *sems, after)
    return list(out)


def _other_chips():
    x, y, _ = _my_place()
    return [(1 - x, y), (x, 1 - y), (1 - x, 1 - y)]


def _scatter_start(chunks, name):
    nt = len(chunks)

    def body(*refs):
        ins = refs[:nt]
        lands = refs[nt:2 * nt]
        send_sems, recv_sems = refs[2 * nt:2 * nt + 2]
        refs[-1][...] = jnp.zeros(TOKEN.shape, F32)
        x, y, c = _my_place()
        for t in range(nt):
            for j, (px, py) in enumerate(_other_chips()):
                pltpu.make_async_remote_copy(src_ref=ins[t].at[2 * px + py], dst_ref=lands[t].at[2 * x + y],
                                             send_sem=send_sems.at[3 * t + j], recv_sem=recv_sems.at[3 * t + j],
                                             device_id=(px, py, c), device_id_type=MESH).start()

    lands = [_in_hbm(lax.empty(a.shape, a.dtype)) for a in chunks]
    out = pl.pallas_call(
        body, name=name, in_specs=[HBM] * (2 * nt),
        out_shape=(pltpu.SemaphoreType.DMA((3 * nt,)), pltpu.SemaphoreType.DMA((3 * nt,)),
                   *[pltpu.HBM(a.shape, a.dtype) for a in chunks] * 2, TOKEN),
        out_specs=(SEM, SEM, *[HBM] * (2 * nt), VMEM_SPEC),
        input_output_aliases={i: 2 + i for i in range(2 * nt)},
        compiler_params=pltpu.CompilerParams(has_side_effects=DATAFLOW),
    )(*[_in_hbm(a) for a in chunks], *lands)
    return out[:2], list(out[2:2 + nt]), list(out[2 + nt:2 + 2 * nt]), out[-1]


def _scatter_finish(sems, chunks, lands, after, name):
    nt = len(chunks)

    def body(*refs):
        ins = refs[:nt]
        lands_in = refs[nt:2 * nt]
        send_sems, recv_sems = refs[2 * nt:2 * nt + 2]
        _, _, c = _my_place()
        for t in range(nt):
            for j, (px, py) in enumerate(_other_chips()):
                cp = pltpu.make_async_remote_copy(src_ref=ins[t].at[2 * px + py], dst_ref=lands_in[t].at[2 * px + py],
                                                  send_sem=send_sems.at[3 * t + j], recv_sem=recv_sems.at[3 * t + j],
                                                  device_id=(px, py, c), device_id_type=MESH)
                cp.wait_send()
                cp.wait_recv()

    out = pl.pallas_call(
        body, name=name, in_specs=[HBM] * (2 * nt) + [SEM, SEM, ANY],
        out_shape=tuple(pltpu.HBM(a.shape, a.dtype) for a in lands), out_specs=tuple([HBM] * nt),
        input_output_aliases={nt + i: i for i in range(nt)},
        compiler_params=pltpu.CompilerParams(has_side_effects=DATAFLOW),
    )(*chunks, *lands, *sems, after)
    return list(out)


def _swap_start(views, name):
    nt = len(views)

    def body(*refs):
        ins = refs[:nt]
        lands = refs[nt:2 * nt]
        send_sems, recv_sems = refs[2 * nt:2 * nt + 2]
        refs[-1][...] = jnp.zeros(TOKEN.shape, F32)
        x, y, c = _my_place()
        for t in range(nt):
            pltpu.make_async_remote_copy(
                src_ref=ins[t].at[pl.ds(0, views[t].shape[0]), pl.ds(1 - c, 1)], dst_ref=lands[t],
                send_sem=send_sems.at[t], recv_sem=recv_sems.at[t], device_id=(x, y, 1 - c),
                device_id_type=MESH).start()

    shapes = [(a.shape[0], 1) + a.shape[2:] for a in views]
    lands = [_in_hbm(lax.empty(s, a.dtype)) for s, a in zip(shapes, views)]
    out = pl.pallas_call(
        body, name=name, in_specs=[HBM] * (2 * nt),
        out_shape=(pltpu.SemaphoreType.DMA((nt,)), pltpu.SemaphoreType.DMA((nt,)),
                   *[pltpu.HBM(a.shape, a.dtype) for a in views],
                   *[pltpu.HBM(s, a.dtype) for s, a in zip(shapes, views)], TOKEN),
        out_specs=(SEM, SEM, *[HBM] * (2 * nt), VMEM_SPEC),
        input_output_aliases={i: 2 + i for i in range(2 * nt)},
        compiler_params=pltpu.CompilerParams(has_side_effects=DATAFLOW),
    )(*[_in_hbm(a) for a in views], *lands)
    return out[:2], list(out[2:2 + nt]), list(out[2 + nt:2 + 2 * nt]), out[-1]


def _swap_finish(sems, views, lands, after, name):
    nt = len(views)

    def body(*refs):
        ins = refs[:nt]
        lands_in = refs[nt:2 * nt]
        send_sems, recv_sems = refs[2 * nt:2 * nt + 2]
        x, y, c = _my_place()
        for t in range(nt):
            cp = pltpu.make_async_remote_copy(
                src_ref=ins[t].at[pl.ds(0, views[t].shape[0]), pl.ds(1 - c, 1)], dst_ref=lands_in[t],
                send_sem=send_sems.at[t], recv_sem=recv_sems.at[t], device_id=(x, y, 1 - c), device_id_type=MESH)
            cp.wait_send()
            cp.wait_recv()

    out = pl.pallas_call(
        body, name=name, in_specs=[HBM] * (2 * nt) + [SEM, SEM, ANY],
        out_shape=tuple(pltpu.HBM(a.shape, a.dtype) for a in lands), out_specs=tuple([HBM] * nt),
        input_output_aliases={nt + i: i for i in range(nt)},
        compiler_params=pltpu.CompilerParams(has_side_effects=DATAFLOW),
    )(*views, *lands, *sems, after)
    return list(out)


def _swap_halves(views, name):
    nt = len(views)

    def body(*refs):
        ins = refs[:nt]
        outs = refs[nt:2 * nt]
        send_sems, recv_sems = refs[2 * nt:]
        x, y, c = _my_place()
        copies = [pltpu.make_async_remote_copy(
            src_ref=ins[t].at[pl.ds(0, views[t].shape[0]), pl.ds(1 - c, 1)], dst_ref=outs[t],
            send_sem=send_sems.at[t], recv_sem=recv_sems.at[t], device_id=(x, y, 1 - c), device_id_type=MESH)
            for t in range(nt)]
        for cp in copies:
            cp.start()
        for cp in copies:
            cp.wait()

    return pl.pallas_call(
        body, name=name, in_specs=[ANY] * nt, out_specs=[ANY] * nt,
        out_shape=[jax.ShapeDtypeStruct((a.shape[0], 1) + a.shape[2:], a.dtype) for a in views],
        scratch_shapes=[pltpu.SemaphoreType.DMA((nt,)), pltpu.SemaphoreType.DMA((nt,))],
    )(*views)


def _pair_add_halves(view, recv, name):
    n, _, rows, cols = view.shape
    tile = _row_tile(rows, cols, 4)

    def body(a0_ref, a1_ref, r_ref, o_ref):
        mine = jnp.where(lax.axis_index("c") == 0, a0_ref[...], a1_ref[...])
        o_ref[...] = (mine.astype(F32) + r_ref[...].astype(F32)).astype(o_ref.dtype)

    half = lambda h: pl.BlockSpec((None, None, tile, cols), lambda p, i: (p, h, i, 0))
    return pl.pallas_call(
        body, name=name, grid=(n, rows // tile), in_specs=[half(0), half(1), half(0)],
        out_specs=pl.BlockSpec((None, tile, cols), lambda p, i: (p, i, 0)),
        out_shape=jax.ShapeDtypeStruct((n, rows, cols), view.dtype))(view, view, recv)


def _sum_chunks(lands, chunks, order, name):
    _, rows, cols = chunks.shape
    tile = _row_tile(rows, cols, 5)

    def body(order_ref, l0_ref, l1_ref, l2_ref, own_ref, o_ref):
        o_ref[...] = ((l0_ref[...].astype(F32) + l1_ref[...].astype(F32)) + l2_ref[...].astype(F32)
                      + own_ref[...].astype(F32))

    slot = lambda j: pl.BlockSpec((None, tile, cols), lambda i, order_ref: (order_ref[j], i, 0))
    grid_spec = pltpu.PrefetchScalarGridSpec(
        num_scalar_prefetch=1, grid=(rows // tile,), in_specs=[slot(0), slot(1), slot(2), slot(3)],
        out_specs=pl.BlockSpec((tile, cols), lambda i, order_ref: (i, 0)))
    return pl.pallas_call(body, name=name, grid_spec=grid_spec,
                          out_shape=jax.ShapeDtypeStruct((rows, cols), F32))(order, lands, lands, lands, chunks)


def _adamw_layer(w, g_mine, g_sibling, m, v, layer, prev, name):
    depth, rows, cols = w.shape
    half = rows // 2
    tile = _row_tile(half, cols, 10)
    tiles = half // tile

    def body(w_ref, gm_ref, gs_ref, m_ref, v_ref, *rest):
        d_ref, nm_ref, nv_ref, go_ref = rest[-4:]
        gv = jnp.where(pl.program_id(0) == lax.axis_index("c"), gm_ref[...], gs_ref[...])
        d_ref[...], nm_ref[...], nv_ref[...] = _adamw_math(w_ref[...], gv, m_ref[...], v_ref[...])
        go_ref[...] = gv

    spec = pl.BlockSpec((None, tile, cols), lambda h, i: (layer, h * tiles + i, 0))
    gspec = pl.BlockSpec((tile, cols), lambda h, i: (i, 0))
    shape = jax.ShapeDtypeStruct((depth, rows, cols), F32)
    extra = list(prev)
    aliases = {5 + j: j for j in range(4)} if len(extra) == 4 else {}
    return pl.pallas_call(
        body, name=name, grid=(2, tiles), in_specs=[spec, gspec, gspec, spec, spec] + [ANY] * len(extra),
        out_specs=[spec] * 4, out_shape=[shape] * 4, input_output_aliases=aliases)(w, g_mine, g_sibling, m, v, *extra)


def _sibling_swap(arrs, name):
    nt = len(arrs)

    def body(*refs):
        ins = refs[:nt]
        outs = refs[nt:2 * nt]
        send_sems, recv_sems = refs[2 * nt:]
        x, y, c = _my_place()
        copies = [pltpu.make_async_remote_copy(src_ref=ins[t], dst_ref=outs[t], send_sem=send_sems.at[t],
                                               recv_sem=recv_sems.at[t], device_id=(x, y, 1 - c), device_id_type=MESH)
                  for t in range(nt)]
        for cp in copies:
            cp.start()
        for cp in copies:
            cp.wait()

    return pl.pallas_call(
        body, name=name, in_specs=[ANY] * nt, out_specs=[ANY] * nt,
        out_shape=[jax.ShapeDtypeStruct(a.shape, a.dtype) for a in arrs],
        scratch_shapes=[pltpu.SemaphoreType.DMA((nt,)), pltpu.SemaphoreType.DMA((nt,))],
    )(*arrs)


STREAM_VMEM_BYTES = 32 * 1024 * 1024
SUBLANES = 8


def _row_tile(rows, cols, n_arrays):
    lanes = -(-cols // LANES) * LANES
    for t in range(min(rows, 512), SUBLANES - 1, -1):
        if rows % t == 0 and t % SUBLANES == 0 and 2 * n_arrays * t * lanes * 4 <= STREAM_VMEM_BYTES:
            return t
    return rows


def _sum_leading(a, name):
    n, rows, cols = a.shape
    tile = _row_tile(rows, cols, n + 1)

    def body(a_ref, o_ref):
        acc = a_ref[0].astype(F32)
        for k in range(1, n):
            acc = acc + a_ref[k].astype(F32)
        o_ref[...] = acc

    return pl.pallas_call(
        body, name=name, grid=(rows // tile,), in_specs=[pl.BlockSpec((n, tile, cols), lambda i: (0, i, 0))],
        out_specs=pl.BlockSpec((tile, cols), lambda i: (i, 0)),
        out_shape=jax.ShapeDtypeStruct((rows, cols), F32))(a)


def _adamw_math(w, g, m, v):
    mn = ADAM_B1 * m + (1.0 - ADAM_B1) * g
    vn = ADAM_B2 * v + (1.0 - ADAM_B2) * jnp.square(g)
    m_hat = mn / (1.0 - ADAM_B1 ** ADAM_STEP)
    v_hat = vn / (1.0 - ADAM_B2 ** ADAM_STEP)
    delta = -ADAM_LR * (m_hat / (jnp.sqrt(v_hat) + ADAM_EPS) + ADAM_WD * w)
    return delta, mn, vn


def _adamw_rows(w, g, m, v, name):
    depth, rows, cols = w.shape
    tile = _row_tile(rows, cols, 7)

    def body(w_ref, g_ref, m_ref, v_ref, d_ref, nm_ref, nv_ref):
        d_ref[...], nm_ref[...], nv_ref[...] = _adamw_math(w_ref[...], g_ref[...], m_ref[...], v_ref[...])

    spec = pl.BlockSpec((None, tile, cols), lambda l, i: (l, i, 0))
    shape = jax.ShapeDtypeStruct((depth, rows, cols), F32)
    return pl.pallas_call(body, name=name, grid=(depth, rows // tile), in_specs=[spec] * 4, out_specs=[spec] * 3,
                          out_shape=[shape] * 3)(w, g, m, v)


def _adamw_many(ws, gs, ms, vs, name):
    nt = len(ws)

    def body(*refs):
        for t in range(nt):
            w_ref, g_ref, m_ref, v_ref = (refs[k * nt + t] for k in range(4))
            d_ref, nm_ref, nv_ref = (refs[(4 + k) * nt + t] for k in range(3))
            d_ref[...], nm_ref[...], nv_ref[...] = _adamw_math(w_ref[...], g_ref[...], m_ref[...], v_ref[...])

    shapes = [jax.ShapeDtypeStruct(a.shape, F32) for a in ws]
    out = pl.pallas_call(body, name=name, out_shape=shapes * 3)(*ws, *gs, *ms, *vs)
    return out[:nt], out[nt:2 * nt], out[2 * nt:]


TINY_ROWS_MULTIPLE = 128


def _flat_pack(arrs):
    flat = jnp.concatenate([a.reshape(-1) for a in arrs])
    pad = (-flat.shape[0]) % (TINY_ROWS_MULTIPLE * LANES)
    return jnp.pad(flat, (0, pad)).reshape(-1, LANES)


def _flat_unpack(buf, shapes):
    flat = buf.reshape(-1)
    out = []
    off = 0
    for shp in shapes:
        n = math.prod(shp)
        out.append(flat[off:off + n].reshape(shp))
        off += n
    return out


def kernel(x, norm_mix, w_in, s5_lam_re, s5_lam_im, s5_log_step, s5_b_re, s5_b_im, s5_c_re, s5_c_im, s5_d, s5_w_glu, s5_b_glu, s5_norm, ssd_conv_w, ssd_conv_b, ssd_dt_bias, ssd_a_log, ssd_d, ssd_norm, w_out, norm_ffn, w_gate, w_up, w_down, norm_final, loss_target, m_norm_mix, m_w_in, m_s5_lam_re, m_s5_lam_im, m_s5_log_step, m_s5_b_re, m_s5_b_im, m_s5_c_re, m_s5_c_im, m_s5_d, m_s5_w_glu, m_s5_b_glu, m_s5_norm, m_ssd_conv_w, m_ssd_conv_b, m_ssd_dt_bias, m_ssd_a_log, m_ssd_d, m_ssd_norm, m_w_out, m_norm_ffn, m_w_gate, m_w_up, m_w_down, m_norm_final, v_norm_mix, v_w_in, v_s5_lam_re, v_s5_lam_im, v_s5_log_step, v_s5_b_re, v_s5_b_im, v_s5_c_re, v_s5_c_im, v_s5_d, v_s5_w_glu, v_s5_b_glu, v_s5_norm, v_ssd_conv_w, v_ssd_conv_b, v_ssd_dt_bias, v_ssd_a_log, v_ssd_d, v_ssd_norm, v_w_out, v_norm_ffn, v_w_gate, v_w_up, v_w_down, v_norm_final):
    args = dict(locals())
    w = {k: args[k] for k in WEIGHTS}
    m = {k: args["m_" + k] for k in WEIGHTS}
    v = {k: args["v_" + k] for k in WEIGHTS}
    cx, cy, cc = _my_place()
    chip = 2 * cx + cy

    me = 4 * cx + 2 * cy + cc
    others = _other_chips()
    chunk_order = jnp.stack([2 * px + py for px, py in others] + [chip]).astype(jnp.int32)
    stored = lambda k, a: jnp.swapaxes(a, 1, 2) if k in T_STORED else a

    def my_half(k, layer):
        a = stored(k, w[k])[layer]
        return lax.dynamic_slice_in_dim(a, cc * (a.shape[0] // 2), a.shape[0] // 2, 0).astype(BF16)

    def assemble(names, lands, blocks):
        full = {}
        for k, a, b in zip(names, lands, blocks):
            a = lax.dynamic_update_index_in_dim(a, b, me, 0)
            a = a.reshape(4, 2 * a.shape[1], a.shape[2])
            if k in COL_SHARDED:
                full[k] = _pad_in_proj(a.transpose(1, 0, 2).reshape(a.shape[1], 4 * a.shape[2]))
            else:
                full[k] = a.reshape(4 * a.shape[1], a.shape[2])
        return full

    conv_block = w["ssd_conv_w"].reshape(DEPTH * SSD_CONV, -1)
    first = [my_half("w_in", 0), conv_block]
    second_names = ("s5_w_glu", "w_out")
    second = [my_half(k, 0) for k in second_names]
    ffn0 = [my_half(k, 0) for k in FFN_BIG]
    blocks1 = [my_half(k, 1) for k in BIG]
    sems_a, kept_a, lands_a, token = _gather_start(first, x, "gather0a_start")
    sems_c, kept_c, lands_c, token = _gather_start(second, token, "gather0c_start")
    sems_b, kept_b, lands_b, token = _gather_start(ffn0, token, "gather0b_start")
    sems1, kept1, lands1, token = _gather_start(blocks1, token, "gather1_start")
    sems_a, lands_a = _gather_forward(sems_a, kept_a, lands_a, token, "gather0a_forward")
    lands_a = _gather_finish(sems_a, lands_a, token, "gather0a_finish")
    big0 = assemble(("w_in",), lands_a, first)
    conv_rows = lax.dynamic_update_index_in_dim(lands_a[-1], conv_block, me, 0)
    conv_full = conv_rows.reshape(4, 2, DEPTH, SSD_CONV, -1)[:, 0].transpose(1, 2, 0, 3).reshape(
        DEPTH, SSD_CONV, SSD_CONV_DIM)
    small = {k: w[k] for k in LAYER_SMALL}
    small["ssd_conv_w"] = conv_full
    p0 = {k: a[0] for k, a in small.items()}
    p1 = {k: a[1] for k, a in small.items()}

    p0["norm_mix"] = p0["norm_mix"] + token[0, 0]
    pending = {}

    def pass_on_ffn0(u):
        pending["second"] = _gather_forward(sems_c, kept_c, lands_c, u, "gather0c_forward")
        pending["ffn0"] = _gather_forward(sems_b, kept_b, lands_b, pending["second"][1][0], "gather0b_forward")

    def second_matrices(ys):
        sems, lands = pending["second"]
        return assemble(second_names, _gather_finish(sems, lands, ys, "gather0c_finish"), second)

    def ffn0_matrices(x1):
        sems, lands = pending["ffn0"]
        lands = _gather_finish(sems, lands, x1, "gather0b_finish")
        pending["layer1"] = _gather_forward(sems1, kept1, lands1, lands[0], "gather1_forward")
        return assemble(FFN_BIG, lands, ffn0)

    h1, saved0 = _layer_forward(x[0], p0, big0, 0, pass_on_ffn0, second_matrices, ffn0_matrices)
    big0 = {**big0, **saved0["late_matrices"]}
    sems1, lands1 = pending["layer1"]
    lands1 = _gather_finish(sems1, lands1, h1, "gather1_finish")
    big1 = assemble(BIG, lands1, blocks1)
    h2, saved1 = _layer_forward(h1, p1, big1, 1)
    loss_row, dx, dxb, g_final = _final_loss(h2, w["norm_final"].reshape(1, -1), loss_target[0], "final_loss")
    loss_part, g_final = loss_row[0, 0], g_final[0]

    def halves_view(k, a):
        if k in COL_SHARDED:
            return a.reshape(1, 2, a.shape[0] // 2, a.shape[1])
        return a.reshape(4, 2, a.shape[0] // 8, a.shape[1])

    def to_chunks(k, part):
        if k in COL_SHARDED:
            a = _unpad_in_proj(part[0])
            return a.reshape(a.shape[0], 4, a.shape[1] // 4).transpose(1, 0, 2)
        return part.reshape(4, -1, part.shape[-1])

    def reduce_begin(names, views, tag):
        recv = _swap_halves(views, tag + "swap")
        parts = [_pair_add_halves(a, r, tag + "pair_" + k) for k, a, r in zip(names, views, recv)]
        chunks = [to_chunks(k, p) for k, p in zip(names, parts)]
        return _scatter_start(chunks, tag + "scatter_start")

    def reduce_end(names, handle, after, tag):
        sems, kept, lands, _ = handle
        lands = _scatter_finish(sems, kept, lands, after, tag + "scatter_finish")
        return [_sum_chunks(a, b, chunk_order, tag + "sum_" + k) for k, a, b in zip(names, lands, kept)]

    def swap_begin(names, views, tag):
        return (names, *_swap_start(views, tag + "swap_start"))

    def scatter_begin(handle, after, tag):
        names, sems, views, lands, _ = handle
        recv = _swap_finish(sems, views, lands, after, tag + "swap_finish")
        parts = [_pair_add_halves(a, r, tag + "pair_" + k) for k, a, r in zip(names, views, recv)]
        return _scatter_start([to_chunks(k, p) for k, p in zip(names, parts)], tag + "scatter_start")

    dx, dxb, g1 = _layer_backward(dx, dxb, p1, big1, saved1, 1)
    swap1 = swap_begin(BIG, [halves_view(k, g1[k]) for k in BIG], "grad1_")
    p0["norm_ffn"] = p0["norm_ffn"] + swap1[4][0, 0]

    early = FFN_BIG + ("w_out",)
    middle = ("s5_w_glu", "s5_bc")
    bc_rows = 2 * DEPTH * S5_GROUP * S5_GROUPS

    def send_early(g_so_far):
        pending["round1"] = scatter_begin(swap1, g_so_far["w_out"], "grad1_")
        pending["swap_early"] = swap_begin(early, [halves_view(k, g_so_far[k]) for k in early], "grad0a_")
        return pending["swap_early"][4]

    def send_middle(g_so_far):
        pending["early"] = scatter_begin(pending["swap_early"], g_so_far["s5_w_glu"], "grad0a_")
        rows = lambda names: jnp.stack([a for layer in (g_so_far, g1) for a in (layer[names[0]], layer[names[1]])]
                                       ).reshape(bc_rows, S5_STATE)
        bc = jnp.stack([rows(("s5_b_re", "s5_b_im")), rows(("s5_c_re", "s5_c_im"))])[None]
        pending["swap_middle"] = swap_begin(middle, [halves_view("s5_w_glu", g_so_far["s5_w_glu"]), bc], "grad0b_")
        return pending["swap_middle"][4]

    grad_x, _, g0 = _layer_backward(dx, dxb, p0, big0, saved0, 0, send_early, send_middle)
    g = {k: [g0[k], g1[k]] for k in LAYER_SMALL}
    pending["middle"] = scatter_begin(pending["swap_middle"], grad_x, "grad0b_")
    reduced1 = dict(zip(BIG, reduce_end(BIG, pending["round1"], grad_x, "grad1_")))
    shared1 = dict(zip(BIG, _sibling_swap([reduced1[k] for k in BIG], "grad1_share")))
    round0 = reduce_begin(("w_in",), [halves_view("w_in", g0["w_in"])], "grad0c_")

    delta, new_m, new_v, grads = {}, {}, {}, {}
    adam1 = {}
    layered = tuple(k for k in BIG if k not in COL_SHARDED)
    for k in layered:
        adam1[k] = _adamw_layer(stored(k, w[k]), reduced1[k], shared1[k], stored(k, m[k]), stored(k, v[k]), 1,
                                [round0[3]], "adamw1_" + k)
    follow = adam1[layered[-1]][0]
    reduced0 = dict(zip(early, reduce_end(early, pending["early"], follow, "grad0a_")))
    reduced0.update(zip(middle, reduce_end(middle, pending["middle"], follow, "grad0b_")))
    tiny_names = TINY + ("norm_final",)
    parts = [jnp.stack(g[k]) for k in TINY] + [g_final, loss_part.reshape(1)]
    shapes = [p.shape for p in parts]
    small_blocks = [_flat_pack(parts), reduced0["s5_bc"]]
    small_sems, small_kept, small_lands, small_token = _gather_start(small_blocks, follow, "gather_small_start")
    reduced0.update(zip(("w_in",), reduce_end(("w_in",), round0, small_token, "grad0c_")))
    shared0 = dict(zip(BIG, _sibling_swap([reduced0[k] for k in BIG], "grad0_share")))
    for k in layered:
        outs = _adamw_layer(stored(k, w[k]), reduced0[k], shared0[k], stored(k, m[k]), stored(k, v[k]), 0, adam1[k],
                            "adamw0_" + k)
        delta[k], new_m[k], new_v[k], grads[k] = (stored(k, a) for a in outs)
    both = lambda mine, sib: jnp.where(cc == 0, jnp.concatenate([mine, sib]), jnp.concatenate([sib, mine]))
    grads["w_in"] = jnp.stack([both(reduced0["w_in"], shared0["w_in"]), both(reduced1["w_in"], shared1["w_in"])])
    outs = _adamw_rows(*[_lane_dense(a)[None] for a in (w["w_in"], grads["w_in"], m["w_in"], v["w_in"])],
                       "adamw_w_in")
    delta["w_in"], new_m["w_in"], new_v["w_in"] = (_from_lane_dense(a[0]) for a in outs)

    last = delta["w_in"]
    small_sems, small_lands = _gather_forward(small_sems, small_kept, small_lands, last, "gather_small_forward")
    small_lands = _gather_finish(small_sems, small_lands, last, "gather_small_finish")
    allparts, bc_eighths = (lax.dynamic_update_index_in_dim(a, b, me, 0) for a, b in zip(small_lands, small_blocks))
    unpacked = _flat_unpack(_sum_leading(allparts, "sum_small"), shapes)
    loss = unpacked[-1][0]
    grads.update(zip(tiny_names, unpacked[:-1]))
    width = SSD_CONV_DIM // 4
    grads["ssd_conv_w"] = lax.dynamic_slice_in_dim(grads["ssd_conv_w"], chip * width, width, axis=2)
    bc = bc_eighths.reshape(4, 2, bc_rows // 4, S5_STATE)
    b_sum = bc[:, 0].reshape(DEPTH, 2, S5_GROUP, S5_GROUPS, S5_STATE)
    c_sum = bc[:, 1].reshape(DEPTH, 2, S5_GROUPS, S5_GROUP, S5_STATE)
    grads["s5_c_re"] = c_sum[:, 0]
    grads["s5_c_im"] = c_sum[:, 1]

    b_names = ("s5_b_re", "s5_b_im")
    hp = lambda a: a.transpose(0, 1, 3, 2)
    names = tiny_names + ("s5_c_re", "s5_c_im") + b_names
    view = lambda k, a: hp(a) if k in b_names else (a.reshape(1, -1) if a.ndim == 1 else a)
    g_view = {k: view(k, grads[k]) for k in names if k not in b_names}
    g_view.update({k: b_sum[:, j].transpose(0, 2, 1, 3) for j, k in enumerate(b_names)})
    ds, nms, nvs = _adamw_many([view(k, w[k]) for k in names], [g_view[k] for k in names],
                               [view(k, m[k]) for k in names], [view(k, v[k]) for k in names], "adamw_small")
    for k, a, b, c in zip(names, ds, nms, nvs):
        if k in b_names:
            delta[k], new_m[k], new_v[k], grads[k] = hp(a), hp(b), hp(c), hp(g_view[k])
        else:
            delta[k], new_m[k], new_v[k] = (t.reshape(w[k].shape) for t in (a, b, c))

    return (loss, grad_x[None], *[grads[k] for k in WEIGHTS], *[delta[k] for k in WEIGHTS],
            *[new_m[k] for k in WEIGHTS], *[new_v[k] for k in WEIGHTS])
```

```python
import math

import jax
import jax.numpy as jnp
from jax import lax
from jax.experimental import pallas as pl
from jax.experimental.pallas import tpu as pltpu

F32 = jnp.float32
BF16 = jnp.bfloat16
MESH = pl.DeviceIdType.MESH
ANY = pl.BlockSpec(memory_space=pl.ANY)

D_MODEL = 1024
DEPTH = 2
S5_GROUPS = 64
S5_GROUP = 16
S5_STATE = 64
S5_COLS = S5_GROUPS * S5_STATE
S5_TILE_GROUPS = 8
S5_TILES = S5_GROUPS // S5_TILE_GROUPS
S5_TILE_IN = S5_TILE_GROUPS * S5_GROUP
S5_TILE_ST = S5_TILE_GROUPS * S5_STATE
SEGS = 8
SSD_HEADS = 16
SSD_HEAD_DIM = 64
SSD_GROUPS = 2
SSD_GROUP_HEADS = SSD_HEADS // SSD_GROUPS
SSD_STATE = 128
SSD_CONV = 4
SSD_CHUNK = 128
SSD_WIDTH = 1024
SSD_CONV_DIM = SSD_WIDTH + 2 * SSD_GROUPS * SSD_STATE
IN_MAIN = 3584
IN_PAD = IN_MAIN + 2 * 128
FFN = 2816
EPS = 1e-6
LANES = 128
ROW_TILE = 256
WIDE_ROW_TILE = 512

ADAM_LR = 0.001
ADAM_B1 = 0.9
ADAM_B2 = 0.999
ADAM_EPS = 1e-08
ADAM_WD = 0.01
ADAM_STEP = 10


def _sigmoid(x):
    return 1.0 / (1.0 + jnp.exp(-x))


def _silu(x):
    return x * _sigmoid(x)


def _dsilu(x):
    s = _sigmoid(x)
    return s * (1.0 + x * (1.0 - s))


_GELU_K = math.sqrt(2.0 / math.pi)
_GELU_C = 0.044715


def _gelu(x):
    t = jnp.tanh(_GELU_K * (x + _GELU_C * x * x * x))
    return 0.5 * x * (1.0 + t)


def _dgelu(x):
    t = jnp.tanh(_GELU_K * (x + _GELU_C * x * x * x))
    return 0.5 * (1.0 + t) + 0.5 * x * (1.0 - t * t) * _GELU_K * (1.0 + 3.0 * _GELU_C * x * x)


def _softplus(x):
    e = jnp.exp(-jnp.abs(x))
    u = 1.0 + e
    log1p = jnp.where(u == 1.0, e, jnp.log(u) * e / jnp.where(u == 1.0, 1.0, u - 1.0))
    return jnp.maximum(x, 0.0) + log1p


def _rstd(x):
    return lax.rsqrt(jnp.mean(x * x, axis=-1, keepdims=True) + EPS)


def _rms_bwd(x, r, gain, dy):
    dyg = dy * gain
    dx = r * dyg - x * (r * r * r) * jnp.mean(x * dyg, axis=-1, keepdims=True)
    dgain = jnp.sum(dy * x * r, axis=0, keepdims=True)
    return dx, dgain


def _dot(a, b):
    return jnp.dot(a, b, preferred_element_type=F32)


def _dot_nt(a, b):
    return lax.dot_general(a, b, (((1,), (1,)), ((), ())), preferred_element_type=F32)


def _dot_tn(a, b):
    return lax.dot_general(a, b, (((0,), (0,)), ((), ())), preferred_element_type=F32)


def _row_spec(tile, cols):
    return pl.BlockSpec((tile, cols), lambda i: (i, 0))


def _full_spec(shape):
    nd = len(shape)
    return pl.BlockSpec(shape, lambda *_: (0,) * nd)


def _layer_spec(shape, layer, block=0):
    if layer is None:
        return pl.BlockSpec(tuple(shape), lambda *_: (block, 0), pipeline_mode=pl.Buffered(1))
    return pl.BlockSpec((None,) + tuple(shape), lambda *_: (layer, block, 0), pipeline_mode=pl.Buffered(1))


def _acc_rows(ref, val, first):
    @pl.when(first)
    def _():
        ref[...] = val

    @pl.when(jnp.logical_not(first))
    def _():
        ref[...] += val


def _pick_tile(n, cap):
    best = LANES
    for t in range(LANES, cap + 1, LANES):
        if n % t == 0:
            best = t
    return best


def _mm_tn(a, b, name):
    k, m = a.shape
    _, n = b.shape
    tm = _pick_tile(m, 1536)
    tn = _pick_tile(n, 1536)

    def body(a_ref, b_ref, o_ref):
        o_ref[...] = _dot_tn(a_ref[...], b_ref[...]).astype(BF16)

    return pl.pallas_call(
        body, name=name, grid=(n // tn, m // tm),
        in_specs=[pl.BlockSpec((k, tm), lambda j, i: (0, i)), pl.BlockSpec((k, tn), lambda j, i: (0, j))],
        out_specs=pl.BlockSpec((tm, tn), lambda j, i: (i, j)),
        out_shape=jax.ShapeDtypeStruct((m, n), BF16),
    )(a, b)


def _rms_inproj(x, gain, w_pad, layer, name):
    L = x.shape[0]

    def body(x_ref, g_ref, w_ref, u_ref, z_ref, xbc_ref, dt_ref, h_ref):
        xv = x_ref[...]
        h = (xv * _rstd(xv) * g_ref[...]).astype(BF16)
        h_ref[...] = h
        p = _dot(h, w_ref[...])
        u_ref[...] = p[:, :1024]
        z_ref[...] = p[:, 1024:2048]
        xbc_ref[...] = p[:, 2048:IN_MAIN]
        dt_ref[...] = p[:, IN_MAIN:IN_PAD]

    tile = min(L, WIDE_ROW_TILE)
    return pl.pallas_call(
        body, name=name, grid=(L // tile,),
        in_specs=[_row_spec(tile, D_MODEL), _full_spec((1, D_MODEL)), _layer_spec((D_MODEL, IN_PAD), layer)],
        out_specs=[_row_spec(tile, 1024), _row_spec(tile, 1024), _row_spec(tile, SSD_CONV_DIM),
                   _row_spec(tile, 256), _row_spec(tile, D_MODEL)],
        out_shape=[jax.ShapeDtypeStruct((L, 1024), F32), jax.ShapeDtypeStruct((L, 1024), F32),
                   jax.ShapeDtypeStruct((L, SSD_CONV_DIM), F32), jax.ShapeDtypeStruct((L, 256), F32),
                   jax.ShapeDtypeStruct((L, D_MODEL), BF16)],
    )(x, gain, w_pad)


def _s5_prep_math(lr, li, ls, bre, bim):
    step = jnp.exp(ls)
    mag = jnp.exp(lr * step)
    ang = li * step
    are = mag * jnp.cos(ang)
    aim = mag * jnp.sin(ang)
    den = lr * lr + li * li
    nr = are - 1.0
    ni = aim
    cre = (nr * lr + ni * li) / den
    cim = (ni * lr - nr * li) / den
    bbre = cre[None] * bre - cim[None] * bim
    bbim = cre[None] * bim + cim[None] * bre
    return are, aim, bbre, bbim


def _s5_prep(lr, li, ls, bre, bim, name):
    def body(lr_ref, li_ref, ls_ref, bre_ref, bim_ref, are_ref, aim_ref, bbre_ref, bbim_ref):
        are, aim, bbre, bbim = _s5_prep_math(lr_ref[...], li_ref[...], ls_ref[...], bre_ref[...], bim_ref[...])
        are_ref[...] = are
        aim_ref[...] = aim
        bbre_ref[...] = bbre
        bbim_ref[...] = bbim

    gp = jax.ShapeDtypeStruct((S5_GROUPS, S5_STATE), F32)
    hgp = jax.ShapeDtypeStruct((S5_GROUP, S5_GROUPS, S5_STATE), F32)
    return pl.pallas_call(body, name=name, out_shape=[gp, gp, hgp, hgp])(lr, li, ls, bre, bim)


def _s5_prep_bwd(lr, li, ls, bre, bim, dare, daim, dbbre, dbbim, name):
    def body(lr_ref, li_ref, ls_ref, bre_ref, bim_ref, dare_ref, daim_ref, dbbre_ref, dbbim_ref,
             dlr_ref, dli_ref, dls_ref, dbre_ref, dbim_ref):
        _, vjp = jax.vjp(_s5_prep_math, lr_ref[...], li_ref[...], ls_ref[...], bre_ref[...], bim_ref[...])
        dlr, dli, dls, dbre, dbim = vjp((dare_ref[...], daim_ref[...], dbbre_ref[...], dbbim_ref[...]))
        dlr_ref[...] = dlr
        dli_ref[...] = dli
        dls_ref[...] = dls
        dbre_ref[...] = dbre
        dbim_ref[...] = dbim

    gp = jax.ShapeDtypeStruct((S5_GROUPS, S5_STATE), F32)
    g1 = jax.ShapeDtypeStruct((S5_GROUPS, 1), F32)
    hgp = jax.ShapeDtypeStruct((S5_GROUP, S5_GROUPS, S5_STATE), F32)
    return pl.pallas_call(body, name=name, out_shape=[gp, gp, g1, hgp, hgp])(
        lr, li, ls, bre, bim, dare, daim, dbbre, dbbim)


def _cmul_add(ar, ai, sr, si, br, bi):
    return ar * sr - ai * si + br, ar * si + ai * sr + bi


def _shift_rows_down(v):
    rolled = pltpu.roll(v, 1, 0)
    row = lax.broadcasted_iota(jnp.int32, v.shape, 0)
    return jnp.where(row == 0, 0.0, rolled)


def _shift_rows_up(v):
    rolled = pltpu.roll(v, SEGS - 1, 0)
    row = lax.broadcasted_iota(jnp.int32, v.shape, 0)
    return jnp.where(row == SEGS - 1, 0.0, rolled)


def _segment_power(ar, ai, steps):
    n = 1
    while n < steps:
        ar, ai = ar * ar - ai * ai, 2.0 * ar * ai
        n *= 2
    assert n == steps
    return ar, ai


def _half_segment_entries(ar, ai, first, second, half_steps, shift):
    pr, pi = _segment_power(ar, ai, half_steps)
    er = jnp.zeros_like(first[0])
    ei = jnp.zeros_like(first[1])
    for _ in range(SEGS - 1):
        mr, mi = _cmul_add(pr, pi, er, ei, *first)
        nr, ni = _cmul_add(pr, pi, mr, mi, *second)
        er, ei = shift(nr), shift(ni)
    mr, mi = _cmul_add(pr, pi, er, ei, *first)
    return (er, ei), (mr, mi)


def _s5_scan(u_perm, bre_bd, bim_bd, cre_bd, cim_bd, are, aim, name):
    L = u_perm.shape[0]
    half = L // SEGS // 2

    def body(u_ref, bre_ref, bim_ref, cre_ref, cim_ref, are_ref, aim_ref, y_ref, xr_ref, xi_ref):
        u = u_ref[...].astype(BF16)
        xr_ref[...] = _dot(u, bre_ref[0])
        xi_ref[...] = _dot(u, bim_ref[0])
        ar = jnp.broadcast_to(are_ref[0], (SEGS, S5_TILE_ST))
        ai = jnp.broadcast_to(aim_ref[0], (SEGS, S5_TILE_ST))
        zero = jnp.zeros((SEGS, S5_TILE_ST), F32)
        block = lambda j: pl.ds(pl.multiple_of(j * SEGS, SEGS), SEGS)

        def finals(j, c):
            lo, hi = block(j), block(j + half)
            return (*_cmul_add(ar, ai, c[0], c[1], xr_ref[lo, :], xi_ref[lo, :]),
                    *_cmul_add(ar, ai, c[2], c[3], xr_ref[hi, :], xi_ref[hi, :]))

        f = lax.fori_loop(0, half, finals, (zero,) * 4, unroll=4)
        e_lo, e_hi = _half_segment_entries(ar, ai, f[:2], f[2:], half, _shift_rows_down)

        def scan(j, c):
            lo, hi = block(j), block(j + half)
            s_lo = _cmul_add(ar, ai, c[0], c[1], xr_ref[lo, :], xi_ref[lo, :])
            s_hi = _cmul_add(ar, ai, c[2], c[3], xr_ref[hi, :], xi_ref[hi, :])
            xr_ref[lo, :], xi_ref[lo, :] = s_lo
            xr_ref[hi, :], xi_ref[hi, :] = s_hi
            return (*s_lo, *s_hi)

        lax.fori_loop(0, half, scan, (*e_lo, *e_hi), unroll=8)
        y_ref[...] = (_dot(xr_ref[...].astype(BF16), cre_ref[0]) - _dot(xi_ref[...].astype(BF16), cim_ref[0]))

    tile3 = lambda a, b: pl.BlockSpec((1, a, b), lambda k: (k, 0, 0))
    return pl.pallas_call(
        body, name=name, grid=(S5_TILES,),
        in_specs=[pl.BlockSpec((L, S5_TILE_IN), lambda k: (0, k)),
                  tile3(S5_TILE_IN, S5_TILE_ST), tile3(S5_TILE_IN, S5_TILE_ST),
                  tile3(S5_TILE_ST, S5_TILE_IN), tile3(S5_TILE_ST, S5_TILE_IN),
                  tile3(1, S5_TILE_ST), tile3(1, S5_TILE_ST)],
        out_specs=[pl.BlockSpec((L, S5_TILE_IN), lambda k: (0, k)),
                   pl.BlockSpec((L, S5_TILE_ST), lambda k: (0, k)), pl.BlockSpec((L, S5_TILE_ST), lambda k: (0, k))],
        out_shape=[jax.ShapeDtypeStruct((L, 1024), F32), jax.ShapeDtypeStruct((L, S5_COLS), F32),
                   jax.ShapeDtypeStruct((L, S5_COLS), F32)],
    )(u_perm, bre_bd, bim_bd, cre_bd, cim_bd, are, aim)


def _s5_scan_bwd(dy_perm, u_perm, xr, xi, bret_bd, bimt_bd, cret_bd, cimt_bd, are, aim, name):
    L = u_perm.shape[0]
    steps = L // SEGS
    half = steps // 2

    def body(dy_ref, u_ref, xr_ref, xi_ref, bret_ref, bimt_ref, cret_ref, cimt_ref, are_ref, aim_ref,
             du_ref, dar_ref, dai_ref, dcre_ref, dcim_ref, dbre_ref, dbim_ref, gr_ref, gi_ref):
        dy = dy_ref[...].astype(BF16)
        u = u_ref[...].astype(BF16)
        gr_ref[...] = _dot(dy, cret_ref[0])
        gi_ref[...] = -_dot(dy, cimt_ref[0])
        ar = jnp.broadcast_to(are_ref[0], (SEGS, S5_TILE_ST))
        ai = -jnp.broadcast_to(aim_ref[0], (SEGS, S5_TILE_ST))
        zero = jnp.zeros((SEGS, S5_TILE_ST), F32)
        block = lambda j: pl.ds(pl.multiple_of(j * SEGS, SEGS), SEGS)

        def finals(k, c):
            hi, lo = block(steps - 1 - k), block(half - 1 - k)
            return (*_cmul_add(ar, ai, c[0], c[1], gr_ref[hi, :], gi_ref[hi, :]),
                    *_cmul_add(ar, ai, c[2], c[3], gr_ref[lo, :], gi_ref[lo, :]))

        f = lax.fori_loop(0, half, finals, (zero,) * 4, unroll=4)
        e_hi, e_lo = _half_segment_entries(ar, ai, f[:2], f[2:], half, _shift_rows_up)

        def scan(k, c):
            accr, acci = c[4], c[5]
            j_hi, j_lo = steps - 1 - k, half - 1 - k
            hi, lo = block(j_hi), block(j_lo)
            hr, hi_im = _cmul_add(ar, ai, c[0], c[1], gr_ref[hi, :], gi_ref[hi, :])
            lr, lo_im = _cmul_add(ar, ai, c[2], c[3], gr_ref[lo, :], gi_ref[lo, :])
            gr_ref[hi, :], gi_ref[hi, :] = hr, hi_im
            gr_ref[lo, :], gi_ref[lo, :] = lr, lo_im
            before_hi = block(j_hi - 1)
            before_lo = block(jnp.maximum(j_lo - 1, 0))
            live = (j_lo > 0).astype(F32)
            xhr, xhi = xr_ref[before_hi, :], xi_ref[before_hi, :]
            xlr, xli = xr_ref[before_lo, :] * live, xi_ref[before_lo, :] * live
            accr = accr + (hr * xhr + hi_im * xhi) + (lr * xlr + lo_im * xli)
            acci = acci + (hi_im * xhr - hr * xhi) + (lo_im * xlr - lr * xli)
            return hr, hi_im, lr, lo_im, accr, acci

        out = lax.fori_loop(0, half, scan, (*e_hi, *e_lo, zero, zero), unroll=4)
        accr, acci = out[4], out[5]
        first = pl.ds(0, SEGS)
        last = pl.ds((steps - 1) * SEGS, SEGS)
        xpr = _shift_rows_down(xr_ref[last, :])
        xpi = _shift_rows_down(xi_ref[last, :])
        g0r = gr_ref[first, :]
        g0i = gi_ref[first, :]
        accr = accr + g0r * xpr + g0i * xpi
        acci = acci + g0i * xpr - g0r * xpi
        dar_ref[0] = jnp.sum(accr, axis=0, keepdims=True)
        dai_ref[0] = jnp.sum(acci, axis=0, keepdims=True)

        grb = gr_ref[...].astype(BF16)
        gib = gi_ref[...].astype(BF16)
        du_ref[...] = _dot(grb, bret_ref[0]) + _dot(gib, bimt_ref[0])
        dbre_ref[0] = _dot_tn(u, grb)
        dbim_ref[0] = _dot_tn(u, gib)
        dcre_ref[0] = _dot_tn(dy, xr_ref[...].astype(BF16))
        dcim_ref[0] = -_dot_tn(dy, xi_ref[...].astype(BF16))

    tile3 = lambda a, b: pl.BlockSpec((1, a, b), lambda k: (k, 0, 0))
    col_in = pl.BlockSpec((L, S5_TILE_IN), lambda k: (0, k))
    col_st = pl.BlockSpec((L, S5_TILE_ST), lambda k: (0, k))
    dense = jax.ShapeDtypeStruct((S5_TILES, S5_TILE_IN, S5_TILE_ST), F32)
    vec = jax.ShapeDtypeStruct((S5_TILES, 1, S5_TILE_ST), F32)
    return pl.pallas_call(
        body, name=name, grid=(S5_TILES,),
        in_specs=[col_in, col_in, col_st, col_st,
                  tile3(S5_TILE_ST, S5_TILE_IN), tile3(S5_TILE_ST, S5_TILE_IN),
                  tile3(S5_TILE_IN, S5_TILE_ST), tile3(S5_TILE_IN, S5_TILE_ST),
                  tile3(1, S5_TILE_ST), tile3(1, S5_TILE_ST)],
        out_specs=[col_in, tile3(1, S5_TILE_ST), tile3(1, S5_TILE_ST),
                   tile3(S5_TILE_IN, S5_TILE_ST), tile3(S5_TILE_IN, S5_TILE_ST),
                   tile3(S5_TILE_IN, S5_TILE_ST), tile3(S5_TILE_IN, S5_TILE_ST)],
        out_shape=[jax.ShapeDtypeStruct((L, 1024), F32), vec, vec, dense, dense, dense, dense],
        scratch_shapes=[pltpu.VMEM((L, S5_TILE_ST), F32), pltpu.VMEM((L, S5_TILE_ST), F32)],
    )(dy_perm, u_perm, xr, xi, bret_bd, bimt_bd, cret_bd, cimt_bd, are, aim)


def _s5_post(ys, u, d_skip, w_glu, b_glu, gain, layer, name):
    L = ys.shape[0]

    def body(ys_ref, u_ref, d_ref, w_ref, b_ref, g_ref, ya_ref):
        g = _gelu(ys_ref[...] + d_ref[...] * u_ref[...])
        q = _dot(g.astype(BF16), w_ref[...]) + b_ref[...]
        oa = g * _sigmoid(q)
        ya_ref[...] = (oa * _rstd(oa) * g_ref[...]).astype(BF16)

    vec = _full_spec((1, 1024))
    tile = min(L, WIDE_ROW_TILE)
    return pl.pallas_call(
        body, name=name, grid=(L // tile,),
        in_specs=[_row_spec(tile, 1024), _row_spec(tile, 1024), vec, _layer_spec((1024, 1024), layer), vec, vec],
        out_specs=_row_spec(tile, 1024),
        out_shape=jax.ShapeDtypeStruct((L, 2048), BF16),
    )(ys, u, d_skip, w_glu, b_glu, gain)


def _s5_post_bwd(dx, w_out, ys, u, d_skip, w_glu, b_glu, gain, layer, name):
    L = ys.shape[0]

    def body(dx_ref, wo_ref, ys_ref, u_ref, d_ref, w_ref, b_ref, gn_ref,
             dys_ref, dus_ref, g_ref, dq_ref, dgain_ref, dd_ref, db_ref):
        first = pl.program_id(0) == 0
        uv = u_ref[...]
        yt = ys_ref[...] + d_ref[...] * uv
        g = _gelu(yt)
        gb = g.astype(BF16)
        q = _dot(gb, w_ref[...]) + b_ref[...]
        s = _sigmoid(q)
        oa = g * s
        dya = _dot_nt(dx_ref[...], wo_ref[...])
        doa, dgain = _rms_bwd(oa, _rstd(oa), gn_ref[...], dya)
        dq = doa * g * s * (1.0 - s)
        dqb = dq.astype(BF16)
        dg = doa * s + _dot_nt(dqb, w_ref[...])
        dyt = dg * _dgelu(yt)
        dys_ref[...] = dyt
        dus_ref[...] = dyt * d_ref[...]
        g_ref[...] = gb
        dq_ref[...] = dqb
        _acc_rows(dgain_ref, dgain, first)
        _acc_rows(dd_ref, jnp.sum(dyt * uv, axis=0, keepdims=True), first)
        _acc_rows(db_ref, jnp.sum(dq, axis=0, keepdims=True), first)

    vec = _full_spec((1, 1024))
    row = _row_spec(ROW_TILE, 1024)
    vshape = jax.ShapeDtypeStruct((1, 1024), F32)
    return pl.pallas_call(
        body, name=name, grid=(L // ROW_TILE,),
        in_specs=[row, _layer_spec((1024, 1024), layer, 0), row, row, vec, _layer_spec((1024, 1024), layer), vec,
                  vec],
        out_specs=[row, row, row, row, vec, vec, vec],
        out_shape=[jax.ShapeDtypeStruct((L, 1024), F32), jax.ShapeDtypeStruct((L, 1024), F32),
                   jax.ShapeDtypeStruct((L, 1024), BF16), jax.ShapeDtypeStruct((L, 1024), BF16),
                   vshape, vshape, vshape],
    )(dx, w_out, ys, u, d_skip, w_glu, b_glu, gain)


CONV_TILE = 256


def _shift_time(v, d):
    if d == 0:
        return v
    rolled = pltpu.roll(v, d, 0)
    row = lax.broadcasted_iota(jnp.int32, v.shape, 0)
    return jnp.where(row < d, 0.0, rolled)


def _unshift_time(v, d):
    if d == 0:
        return v
    n = v.shape[0]
    rolled = pltpu.roll(v, n - d, 0)
    row = lax.broadcasted_iota(jnp.int32, v.shape, 0)
    return jnp.where(row >= n - d, 0.0, rolled)


def _ssd_conv(xbc, w, b, name):
    L = xbc.shape[0]

    def body(x_ref, w_ref, b_ref, o_ref):
        xv = x_ref[...]
        pre = jnp.broadcast_to(b_ref[...], xv.shape)
        for k in range(SSD_CONV):
            pre = pre + w_ref[k:k + 1, :] * _shift_time(xv, SSD_CONV - 1 - k)
        o_ref[...] = _silu(pre)

    col = pl.BlockSpec((L, CONV_TILE), lambda j: (0, j))
    return pl.pallas_call(
        body, name=name, grid=(SSD_CONV_DIM // CONV_TILE,),
        in_specs=[col, pl.BlockSpec((8, CONV_TILE), lambda j: (0, j)), pl.BlockSpec((1, CONV_TILE), lambda j: (0, j))],
        out_specs=col, out_shape=jax.ShapeDtypeStruct((L, SSD_CONV_DIM), F32),
    )(xbc, w, b)


def _ssd_conv_bwd(dxs, dbm, dcm, xbc, w, b, name):
    L = xbc.shape[0]
    x_tiles = dxs.shape[1] // CONV_TILE

    def body(dxs_ref, dbm_ref, dcm_ref, x_ref, w_ref, b_ref, dx_ref, dw_ref, db_ref):
        j = pl.program_id(0)
        d_val = jnp.where(j < x_tiles, dxs_ref[...], jnp.where(j == x_tiles, dbm_ref[...], dcm_ref[...]))
        xv = x_ref[...]
        shifted = [_shift_time(xv, SSD_CONV - 1 - k) for k in range(SSD_CONV)]
        pre = jnp.broadcast_to(b_ref[...], xv.shape)
        for k in range(SSD_CONV):
            pre = pre + w_ref[k:k + 1, :] * shifted[k]
        dpre = d_val * _dsilu(pre)
        dx = jnp.zeros_like(xv)
        rows = []
        for k in range(SSD_CONV):
            dx = dx + w_ref[k:k + 1, :] * _unshift_time(dpre, SSD_CONV - 1 - k)
            rows.append(jnp.sum(dpre * shifted[k], axis=0, keepdims=True))
        dx_ref[...] = dx
        dw_ref[...] = jnp.concatenate(rows + [jnp.zeros((8 - SSD_CONV, CONV_TILE), F32)], axis=0)
        db_ref[...] = jnp.sum(dpre, axis=0, keepdims=True)

    col = pl.BlockSpec((L, CONV_TILE), lambda j: (0, j))
    w_spec = pl.BlockSpec((8, CONV_TILE), lambda j: (0, j))
    b_spec = pl.BlockSpec((1, CONV_TILE), lambda j: (0, j))
    return pl.pallas_call(
        body, name=name, grid=(SSD_CONV_DIM // CONV_TILE,),
        in_specs=[pl.BlockSpec((L, CONV_TILE), lambda j: (0, jnp.minimum(j, x_tiles - 1))),
                  _full_spec((L, CONV_TILE)), _full_spec((L, CONV_TILE)), col, w_spec, b_spec],
        out_specs=[col, w_spec, b_spec],
        out_shape=[jax.ShapeDtypeStruct((L, SSD_CONV_DIM), F32), jax.ShapeDtypeStruct((8, SSD_CONV_DIM), F32),
                   jax.ShapeDtypeStruct((1, SSD_CONV_DIM), F32)],
    )(dxs, dbm, dcm, xbc, w, b)


def _tri(lower):
    r = lax.broadcasted_iota(jnp.int32, (SSD_CHUNK, SSD_CHUNK), 0)
    c = lax.broadcasted_iota(jnp.int32, (SSD_CHUNK, SSD_CHUNK), 1)
    return (r >= c) if lower else (r <= c)


def _ssd_chunk_common(dt_ref, bias_ref, alog_ref):
    pre = dt_ref[...] + bias_ref[0]
    dtp = _softplus(pre)
    a_neg = -jnp.exp(alog_ref[0])
    dta = dtp * a_neg
    acum = _select_rows(_tri(True), dta)
    return pre, dtp, a_neg, dta, acum


GROUP_W = SSD_GROUP_HEADS * SSD_HEAD_DIM


def _head_expander():
    r = lax.broadcasted_iota(jnp.int32, (LANES, GROUP_W), 0)
    c = lax.broadcasted_iota(jnp.int32, (LANES, GROUP_W), 1)
    return (c // SSD_HEAD_DIM == r).astype(F32)


def _split_bf16(a, terms):
    parts = []
    rest = a
    for _ in range(terms):
        piece = rest.astype(BF16)
        parts.append(piece)
        rest = rest - piece.astype(F32)
    return parts


def _select_cols(a, sel, terms=3):
    lhs = jnp.concatenate(_split_bf16(a, terms), axis=1)
    rhs = jnp.concatenate([sel.astype(BF16)] * terms, axis=0)
    return _dot(lhs, rhs)


def _select_rows(sel, b, terms=3):
    lhs = jnp.concatenate([sel.astype(BF16)] * terms, axis=1)
    rhs = jnp.concatenate(_split_bf16(b, terms), axis=0)
    return _dot(lhs, rhs)


def _decay_mask(acum_all, acum_t, h, lower):
    seg = acum_all[:, h:h + 1] - acum_t[h:h + 1, :]
    return jnp.where(lower, jnp.exp(jnp.minimum(seg, 0.0)), 0.0)


def _ssd_scan(xc, dt, dt_bias, a_log, d_wide, expand, expand_t, name):
    L = xc.shape[0]
    nc = L // SSD_CHUNK

    def body(x_ref, b_ref, c_ref, dt_ref, bias_ref, alog_ref, d_ref, e_ref, et_ref, y_ref, sp_ref, s_ref, xdt_ref):
        @pl.when(pl.program_id(1) == 0)
        def _():
            s_ref[...] = jnp.zeros_like(s_ref)

        _, dtp_all, _, _, acum_all = _ssd_chunk_common(dt_ref, bias_ref, alog_ref)
        acum_t = acum_all.T
        wide = _select_cols(jnp.concatenate([acum_all, dtp_all], axis=0), e_ref[...])
        acum_e = wide[:SSD_CHUNK]
        alast_e = acum_e[SSD_CHUNK - 1:SSD_CHUNK, :]
        x = x_ref[...]
        xdt = x * wide[SSD_CHUNK:]
        xdt_ref[...] = xdt.astype(BF16)
        bm = b_ref[...].astype(BF16)
        cm = c_ref[...].astype(BF16)
        cb = _dot_nt(cm, bm)
        lower = _tri(True)
        sp = s_ref[...]
        for h in range(SSD_GROUP_HEADS):
            cols = slice(h * SSD_HEAD_DIM, (h + 1) * SSD_HEAD_DIM)
            lm = _decay_mask(acum_all, acum_t, h, lower)
            y_ref[:, cols] = _dot((cb * lm).astype(BF16), xdt_ref[:, cols])
        y_ref[...] += _dot_nt(cm, sp.astype(BF16)) * jnp.exp(acum_e) + d_ref[0] * x
        wgt = xdt * jnp.exp(alast_e - acum_e)
        ealast = jnp.exp(_select_rows(et_ref[...], acum_t)[:, SSD_CHUNK - 1:SSD_CHUNK])
        sp_ref[0, 0] = sp
        s_ref[...] = ealast * sp + _dot_tn(wgt.astype(BF16), bm)

    par = lambda n: pl.BlockSpec((1, 1, n), lambda g, c: (g, 0, 0))
    return pl.pallas_call(
        body, name=name, grid=(SSD_GROUPS, nc),
        in_specs=[pl.BlockSpec((SSD_CHUNK, GROUP_W), lambda g, c: (c, g)),
                  pl.BlockSpec((SSD_CHUNK, SSD_STATE), lambda g, c: (c, 8 + g)),
                  pl.BlockSpec((SSD_CHUNK, SSD_STATE), lambda g, c: (c, 10 + g)),
                  pl.BlockSpec((SSD_CHUNK, LANES), lambda g, c: (c, g)),
                  par(LANES), par(LANES), par(GROUP_W), _full_spec((LANES, GROUP_W)), _full_spec((GROUP_W, LANES))],
        out_specs=[pl.BlockSpec((SSD_CHUNK, GROUP_W), lambda g, c: (c, g)),
                   pl.BlockSpec((1, 1, GROUP_W, SSD_STATE), lambda g, c: (c, g, 0, 0))],
        out_shape=[jax.ShapeDtypeStruct((L, SSD_WIDTH), F32),
                   jax.ShapeDtypeStruct((nc, SSD_GROUPS, GROUP_W, SSD_STATE), F32)],
        scratch_shapes=[pltpu.VMEM((GROUP_W, SSD_STATE), F32), pltpu.VMEM((SSD_CHUNK, GROUP_W), BF16)],
    )(xc, xc, xc, dt, dt_bias, a_log, d_wide, expand, expand_t)


def _ssd_scan_bwd(dy, xc, dt, sprev, dt_bias, a_log, d_wide, expand, expand_t, name):
    L = xc.shape[0]
    nc = L // SSD_CHUNK

    def body(dy_ref, x_ref, b_ref, c_ref, dt_ref, sp_ref, bias_ref, alog_ref, d_ref, e_ref, et_ref,
             dx_ref, db_ref, dc_ref, ddt_ref, dbias_ref, dalog_ref, dd_ref, ds_ref, xdt_ref, dyb_ref):
        first = pl.program_id(1) == 0

        @pl.when(first)
        def _():
            ds_ref[...] = jnp.zeros_like(ds_ref)

        pre, dtp_all, a_neg, _, acum_all = _ssd_chunk_common(dt_ref, bias_ref, alog_ref)
        acum_t = acum_all.T
        e = e_ref[...]
        et = et_ref[...]
        wide = _select_cols(jnp.concatenate([acum_all, dtp_all], axis=0), e)
        acum_e = wide[:SSD_CHUNK]
        dtp_e = wide[SSD_CHUNK:]
        alast_e = acum_e[SSD_CHUNK - 1:SSD_CHUNK, :]
        dstate_e = jnp.exp(alast_e - acum_e)
        x = x_ref[...]
        dy = dy_ref[...]
        xdt = x * dtp_e
        xdt_ref[...] = xdt.astype(BF16)
        dyb_ref[...] = dy.astype(BF16)
        bm = b_ref[...].astype(BF16)
        cm = c_ref[...].astype(BF16)
        cb = _dot_nt(cm, bm)
        sp = sp_ref[0, 0]
        spb = sp.astype(BF16)
        dsn = ds_ref[...]
        dsb = dsn.astype(BF16)
        z = _dot_nt(cm, spb)
        dz = dy * jnp.exp(acum_e)
        dzb = dz.astype(BF16)
        dc_acc = _dot(dzb, spb)
        ealast = jnp.exp(_select_rows(et, acum_t)[:, SSD_CHUNK - 1:SSD_CHUNK])
        ds_ref[...] = _dot_tn(dzb, cm) + ealast * dsn
        dw = _dot_nt(bm, dsb)
        wgt = xdt * dstate_e
        db_acc = _dot(wgt.astype(BF16), dsb)
        lower = _tri(True)
        lane = lax.broadcasted_iota(jnp.int32, (SSD_CHUNK, LANES), 1)
        row = lax.broadcasted_iota(jnp.int32, (SSD_CHUNK, LANES), 0)
        dcb = jnp.zeros((SSD_CHUNK, SSD_CHUNK), F32)
        dacum_all = jnp.zeros((SSD_CHUNK, LANES), F32)
        dacum_cols = jnp.zeros((SSD_CHUNK, LANES), F32)
        for h in range(SSD_GROUP_HEADS):
            cols = slice(h * SSD_HEAD_DIM, (h + 1) * SSD_HEAD_DIM)
            lm = _decay_mask(acum_all, acum_t, h, lower)
            dm = _dot_nt(dyb_ref[:, cols], xdt_ref[:, cols])
            dx_ref[:, cols] = _dot_tn((cb * lm).astype(BF16), dyb_ref[:, cols])
            dm_lm = dm * lm
            dcb = dcb + dm_lm
            q = dm_lm * cb
            dacum_all = jnp.where(lane == h, jnp.sum(q, axis=1, keepdims=True), dacum_all)
            dacum_cols = jnp.where(row == h, jnp.sum(q, axis=0, keepdims=True), dacum_cols)
        dxdt = dx_ref[...] + dw * dstate_e
        sums = _select_cols(jnp.concatenate([dz * z, dw * wgt, dxdt * x, dy * x], axis=0), et, terms=2)
        dacum_off = sums[0:SSD_CHUNK]
        dds_ds = sums[SSD_CHUNK:2 * SSD_CHUNK]
        ddtp_x = sums[2 * SSD_CHUNK:3 * SSD_CHUNK]
        dd_part = sums[3 * SSD_CHUNK:4 * SSD_CHUNK]
        ds_s = jnp.sum(_select_rows(e, dsn * sp, terms=2).T, axis=0, keepdims=True)
        dalast = ds_s * jnp.exp(acum_all[SSD_CHUNK - 1:SSD_CHUNK, :]) + jnp.sum(dds_ds, axis=0, keepdims=True)
        dacum_all = dacum_all - dacum_cols.T + dacum_off - dds_ds + jnp.where(row == SSD_CHUNK - 1, dalast, 0.0)
        dx_ref[...] = d_ref[0] * dy + dxdt * dtp_e
        dcbb = dcb.astype(BF16)
        dc_ref[...] = dc_acc + _dot(dcbb, bm)
        db_ref[...] = db_acc + _dot_tn(dcbb, cm)
        ddta = _select_rows(_tri(False), dacum_all)
        ddt = (ddtp_x + ddta * a_neg) * _sigmoid(pre)
        ddt_ref[...] = ddt
        _acc_rows(dbias_ref, jnp.sum(ddt, axis=0, keepdims=True)[None], first)
        _acc_rows(dalog_ref, (jnp.sum(ddta * dtp_all, axis=0, keepdims=True) * a_neg)[None], first)
        _acc_rows(dd_ref, jnp.sum(dd_part, axis=0, keepdims=True)[None], first)

    rev = lambda c: nc - 1 - c
    par = lambda n: pl.BlockSpec((1, 1, n), lambda g, c: (g, 0, 0))
    pshape = jax.ShapeDtypeStruct((SSD_GROUPS, 1, LANES), F32)
    return pl.pallas_call(
        body, name=name, grid=(SSD_GROUPS, nc),
        in_specs=[pl.BlockSpec((SSD_CHUNK, GROUP_W), lambda g, c: (rev(c), g)),
                  pl.BlockSpec((SSD_CHUNK, GROUP_W), lambda g, c: (rev(c), g)),
                  pl.BlockSpec((SSD_CHUNK, SSD_STATE), lambda g, c: (rev(c), 8 + g)),
                  pl.BlockSpec((SSD_CHUNK, SSD_STATE), lambda g, c: (rev(c), 10 + g)),
                  pl.BlockSpec((SSD_CHUNK, LANES), lambda g, c: (rev(c), g)),
                  pl.BlockSpec((1, 1, GROUP_W, SSD_STATE), lambda g, c: (rev(c), g, 0, 0)),
                  par(LANES), par(LANES), par(GROUP_W), _full_spec((LANES, GROUP_W)), _full_spec((GROUP_W, LANES))],
        out_specs=[pl.BlockSpec((SSD_CHUNK, GROUP_W), lambda g, c: (rev(c), g)),
                   pl.BlockSpec((SSD_CHUNK, SSD_STATE), lambda g, c: (rev(c), g)),
                   pl.BlockSpec((SSD_CHUNK, SSD_STATE), lambda g, c: (rev(c), g)),
                   pl.BlockSpec((SSD_CHUNK, LANES), lambda g, c: (rev(c), g)),
                   par(LANES), par(LANES), par(LANES)],
        out_shape=[jax.ShapeDtypeStruct((L, SSD_WIDTH), F32), jax.ShapeDtypeStruct((L, 256), F32),
                   jax.ShapeDtypeStruct((L, 256), F32), jax.ShapeDtypeStruct((L, 256), F32),
                   pshape, pshape, pshape],
        scratch_shapes=[pltpu.VMEM((GROUP_W, SSD_STATE), F32), pltpu.VMEM((SSD_CHUNK, GROUP_W), BF16),
                        pltpu.VMEM((SSD_CHUNK, GROUP_W), BF16)],
    )(dy, xc, xc, xc, dt, sprev, dt_bias, a_log, d_wide, expand, expand_t)


def _ssd_post(y, z, gain, yab, name):
    L = y.shape[0]

    def body(y_ref, z_ref, g_ref, yab_ref, o_ref):
        ob = y_ref[...] * _silu(z_ref[...])
        o_ref[...] = (ob * _rstd(ob) * g_ref[...]).astype(BF16)

    tile = min(L, WIDE_ROW_TILE)
    row = _row_spec(tile, 1024)
    return pl.pallas_call(body, name=name, grid=(L // tile,), in_specs=[row, row, _full_spec((1, 1024)), ANY],
                          out_specs=pl.BlockSpec((tile, 1024), lambda i: (i, 1)), input_output_aliases={3: 0},
                          out_shape=jax.ShapeDtypeStruct((L, 2048), BF16))(y, z, gain, yab)


def _ssd_post_bwd(dx, w_out, y, z, gain, layer, name):
    L = y.shape[0]

    def body(dx_ref, wo_ref, y_ref, z_ref, g_ref, dy_ref, dz_ref, dgain_ref):
        first = pl.program_id(0) == 0
        yv = y_ref[...]
        zv = z_ref[...]
        sz = _silu(zv)
        ob = yv * sz
        dyb = _dot_nt(dx_ref[...], wo_ref[...])
        dob, dgain = _rms_bwd(ob, _rstd(ob), g_ref[...], dyb)
        dy_ref[...] = dob * sz
        dz_ref[...] = dob * yv * _dsilu(zv)
        _acc_rows(dgain_ref, dgain, first)

    row = _row_spec(ROW_TILE, 1024)
    vec = _full_spec((1, 1024))
    return pl.pallas_call(
        body, name=name, grid=(L // ROW_TILE,),
        in_specs=[row, _layer_spec((1024, 1024), layer, 1), row, row, vec],
        out_specs=[row, row, vec],
        out_shape=[jax.ShapeDtypeStruct((L, 1024), F32), jax.ShapeDtypeStruct((L, 1024), F32),
                   jax.ShapeDtypeStruct((1, 1024), F32)],
    )(dx, w_out, y, z, gain)


def _out_proj(x, yab, w_out, layer, name):
    L = x.shape[0]

    def body(x_ref, yab_ref, w_ref, o_ref):
        o_ref[...] = x_ref[...] + _dot(yab_ref[...], w_ref[...])

    tile = min(L, WIDE_ROW_TILE)
    row = _row_spec(tile, 1024)
    return pl.pallas_call(body, name=name, grid=(L // tile,),
                          in_specs=[row, _row_spec(tile, 2048), _layer_spec((2048, 1024), layer)],
                          out_specs=row, out_shape=jax.ShapeDtypeStruct((L, D_MODEL), F32))(x, yab, w_out)


def _ffn(x, gain, w_gate, w_up, w_down, layer, name):
    L = x.shape[0]

    def body(x_ref, g_ref, wg_ref, wu_ref, wd_ref, o_ref, gt_ref, up_ref):
        xv = x_ref[...]
        h = (xv * _rstd(xv) * g_ref[...]).astype(BF16)
        gt = _dot_nt(h, wg_ref[...])
        up = _dot_nt(h, wu_ref[...])
        gt_ref[...] = gt
        up_ref[...] = up
        o_ref[...] = xv + _dot((_silu(gt) * up).astype(BF16), wd_ref[...])

    row = _row_spec(ROW_TILE, D_MODEL)
    hid = _row_spec(ROW_TILE, FFN)
    return pl.pallas_call(
        body, name=name, grid=(L // ROW_TILE,),
        in_specs=[row, _full_spec((1, D_MODEL)), _layer_spec((FFN, D_MODEL), layer),
                  _layer_spec((FFN, D_MODEL), layer), _layer_spec((FFN, D_MODEL), layer)],
        out_specs=[row, hid, hid],
        out_shape=[jax.ShapeDtypeStruct((L, D_MODEL), F32), jax.ShapeDtypeStruct((L, FFN), F32),
                   jax.ShapeDtypeStruct((L, FFN), F32)],
    )(x, gain, w_gate, w_up, w_down)


def _ffn_bwd(dx2, x1, gt, up, gain, w_gate, w_up, w_down, layer, name):
    L = x1.shape[0]

    def body(d_ref, x_ref, gt_ref, up_ref, g_ref, wg_ref, wu_ref, wd_ref,
             dx_ref, dxb_ref, h_ref, act_ref, dgt_ref, dup_ref, dgain_ref):
        first = pl.program_id(0) == 0
        dv = d_ref[...]
        xv = x_ref[...]
        r = _rstd(xv)
        h_ref[...] = (xv * r * g_ref[...]).astype(BF16)
        gtv = gt_ref[...]
        upv = up_ref[...]
        sg = _silu(gtv)
        act_ref[...] = (sg * upv).astype(BF16)
        dact = _dot_nt(dv.astype(BF16), wd_ref[...])
        dgt = (dact * upv * _dsilu(gtv)).astype(BF16)
        dup = (dact * sg).astype(BF16)
        dgt_ref[...] = dgt
        dup_ref[...] = dup
        dh = _dot(dgt, wg_ref[...]) + _dot(dup, wu_ref[...])
        dxn, dgain = _rms_bwd(xv, r, g_ref[...], dh)
        dx = dv + dxn
        dx_ref[...] = dx
        dxb_ref[...] = dx.astype(BF16)
        _acc_rows(dgain_ref, dgain, first)

    row = _row_spec(ROW_TILE, D_MODEL)
    hid = _row_spec(ROW_TILE, FFN)
    vec = _full_spec((1, D_MODEL))
    return pl.pallas_call(
        body, name=name, grid=(L // ROW_TILE,),
        in_specs=[row, row, hid, hid, vec, _layer_spec((FFN, D_MODEL), layer), _layer_spec((FFN, D_MODEL), layer),
                  _layer_spec((FFN, D_MODEL), layer)],
        out_specs=[row, row, row, hid, hid, hid, vec],
        out_shape=[jax.ShapeDtypeStruct((L, D_MODEL), F32), jax.ShapeDtypeStruct((L, D_MODEL), BF16),
                   jax.ShapeDtypeStruct((L, D_MODEL), BF16),
                   jax.ShapeDtypeStruct((L, FFN), BF16), jax.ShapeDtypeStruct((L, FFN), BF16),
                   jax.ShapeDtypeStruct((L, FFN), BF16), jax.ShapeDtypeStruct((1, D_MODEL), F32)],
    )(dx2, x1, gt, up, gain, w_gate, w_up, w_down)


def _inproj_bwd(dx1, x0, du_skip, du_scan, dz, dxbc, ddt, gain, w_pad, layer, name):
    L = x0.shape[0]

    def body(d_ref, x_ref, dus_ref, duc_ref, dz_ref, dxbc_ref, ddt_ref, g_ref, w_ref,
             dx_ref, dxb_ref, dp_ref, dgain_ref):
        first = pl.program_id(0) == 0
        xv = x_ref[...]
        dp = jnp.concatenate([dus_ref[...] + duc_ref[...], dz_ref[...], dxbc_ref[...], ddt_ref[...]],
                             axis=1).astype(BF16)
        dp_ref[...] = dp
        dh = _dot_nt(dp, w_ref[...])
        dxn, dgain = _rms_bwd(xv, _rstd(xv), g_ref[...], dh)
        dx = d_ref[...] + dxn
        dx_ref[...] = dx
        dxb_ref[...] = dx.astype(BF16)
        _acc_rows(dgain_ref, dgain, first)

    row = _row_spec(ROW_TILE, D_MODEL)
    vec = _full_spec((1, D_MODEL))
    return pl.pallas_call(
        body, name=name, grid=(L // ROW_TILE,),
        in_specs=[row, row, row, row, row, _row_spec(ROW_TILE, SSD_CONV_DIM), _row_spec(ROW_TILE, 256), vec,
                  _layer_spec((D_MODEL, IN_PAD), layer)],
        out_specs=[row, row, _row_spec(ROW_TILE, IN_PAD), vec],
        out_shape=[jax.ShapeDtypeStruct((L, D_MODEL), F32), jax.ShapeDtypeStruct((L, D_MODEL), BF16),
                   jax.ShapeDtypeStruct((L, IN_PAD), BF16), jax.ShapeDtypeStruct((1, D_MODEL), F32)],
    )(dx1, x0, du_skip, du_scan, dz, dxbc, ddt, gain, w_pad)


def _final_loss(x, gain, target, name):
    L = x.shape[0]

    def body(x_ref, g_ref, t_ref, loss_ref, dx_ref, dxb_ref, dgain_ref):
        first = pl.program_id(0) == 0
        xv = x_ref[...]
        r = _rstd(xv)
        err = xv * r * g_ref[...] - t_ref[...]
        part = 0.5 * jnp.sum(jnp.mean(err * err, axis=-1, keepdims=True), axis=0, keepdims=True)
        dx, dgain = _rms_bwd(xv, r, g_ref[...], err * (1.0 / D_MODEL))
        dx_ref[...] = dx
        dxb_ref[...] = dx.astype(BF16)
        _acc_rows(loss_ref, jnp.broadcast_to(part, (1, LANES)), first)
        _acc_rows(dgain_ref, dgain, first)

    row = _row_spec(ROW_TILE, D_MODEL)
    vec = _full_spec((1, D_MODEL))
    return pl.pallas_call(
        body, name=name, grid=(L // ROW_TILE,), in_specs=[row, vec, row],
        out_specs=[_full_spec((1, LANES)), row, row, vec],
        out_shape=[jax.ShapeDtypeStruct((1, LANES), F32), jax.ShapeDtypeStruct((L, D_MODEL), F32),
                   jax.ShapeDtypeStruct((L, D_MODEL), BF16), jax.ShapeDtypeStruct((1, D_MODEL), F32)],
    )(x, gain, target)


def _to_segments(a):
    L, n = a.shape
    return a.reshape(SEGS, L // SEGS, n).transpose(1, 0, 2).reshape(L, n)


def _from_segments(a):
    L, n = a.shape
    return a.reshape(L // SEGS, SEGS, n).transpose(1, 0, 2).reshape(L, n)


def _diag_block(g):
    k, a = divmod(g, S5_TILE_GROUPS)
    return k, slice(a * S5_GROUP, (a + 1) * S5_GROUP), slice(a * S5_STATE, (a + 1) * S5_STATE)


def _block_diag_build(mats, name):
    n = mats.shape[0]

    def body(m_ref, o_ref):
        o_ref[...] = jnp.zeros(o_ref.shape, BF16)
        for q in range(n):
            for g in range(S5_GROUPS):
                k, rows, cols = _diag_block(g)
                o_ref[q, k, rows, cols] = m_ref[q, g].astype(BF16)

    return pl.pallas_call(body, name=name,
                          out_shape=jax.ShapeDtypeStruct((n, S5_TILES, S5_TILE_IN, S5_TILE_ST), BF16))(mats)


def _block_diag_extract(dense, name):
    n = len(dense)

    def body(*refs):
        o_ref = refs[n]
        for q in range(n):
            for g in range(S5_GROUPS):
                k, rows, cols = _diag_block(g)
                o_ref[q, g] = refs[q][k, rows, cols]

    return pl.pallas_call(body, name=name,
                          out_shape=jax.ShapeDtypeStruct((n, S5_GROUPS, S5_GROUP, S5_STATE), F32))(*dense)


def _pad_in_proj(w):
    z = jnp.zeros(w.shape[:-1] + (LANES - SSD_GROUP_HEADS,), w.dtype)
    return jnp.concatenate([w[..., :IN_MAIN + 8], z, w[..., IN_MAIN + 8:], z], axis=-1)


def _unpad_in_proj(w):
    return jnp.concatenate([w[..., :IN_MAIN + 8], w[..., IN_MAIN + LANES:IN_MAIN + LANES + 8]], axis=-1)


def _lane_dense(a):
    return a.reshape(DEPTH, D_MODEL // LANES, LANES, -1).transpose(3, 1, 0, 2).reshape(-1, LANES)


def _from_lane_dense(a):
    return a.reshape(-1, D_MODEL // LANES, DEPTH, LANES).transpose(2, 1, 3, 0).reshape(DEPTH, D_MODEL, -1)


def _pad_heads(v):
    v = v.reshape(SSD_GROUPS, 1, SSD_GROUP_HEADS)
    return jnp.pad(v, ((0, 0), (0, 0), (0, LANES - SSD_GROUP_HEADS)))


def _unpad_heads(v):
    return v[:, 0, :SSD_GROUP_HEADS].reshape(SSD_HEADS)


def _layer_forward(x0, p, big, i, after_inproj=None, before_s5_post=None, before_ffn=None):
    tag = "l%d_" % i
    ls = p["s5_log_step"].reshape(S5_GROUPS, 1)
    b_hgp = (p["s5_b_re"].transpose(2, 0, 1), p["s5_b_im"].transpose(2, 0, 1))
    are, aim, bbre, bbim = _s5_prep(p["s5_lam_re"], p["s5_lam_im"], ls, b_hgp[0], b_hgp[1], tag + "s5_prep")
    mats = jnp.stack([bbre.transpose(1, 0, 2), bbim.transpose(1, 0, 2), p["s5_c_re"], p["s5_c_im"]])
    bre_bd, bim_bd, cret_bd, cimt_bd = _block_diag_build(mats, tag + "s5_blockdiag")
    s5mats = dict(bre_bd=bre_bd, bim_bd=bim_bd, cret_bd=cret_bd, cimt_bd=cimt_bd,
                  bret_bd=bre_bd.transpose(0, 2, 1), bimt_bd=bim_bd.transpose(0, 2, 1),
                  cre_bd=cret_bd.transpose(0, 2, 1), cim_bd=cimt_bd.transpose(0, 2, 1),
                  are=are.reshape(S5_TILES, 1, S5_TILE_ST), aim=aim.reshape(S5_TILES, 1, S5_TILE_ST))

    u, z, xbc, dt, h1 = _rms_inproj(x0, p["norm_mix"].reshape(1, -1), big["w_in"], None, tag + "rms_inproj")
    if after_inproj is not None:
        after_inproj(u)
    u_perm = _to_segments(u)
    ys_perm, xr, xi = _s5_scan(u_perm, bre_bd, bim_bd, s5mats["cre_bd"], s5mats["cim_bd"],
                               s5mats["are"], s5mats["aim"], tag + "s5_scan")
    ys = _from_segments(ys_perm)
    late_matrices = before_s5_post(ys) if before_s5_post is not None else {}
    big = {**big, **late_matrices}
    yab = _s5_post(ys, u, p["s5_d"].reshape(1, -1), big["s5_w_glu"], p["s5_b_glu"].reshape(1, -1),
                   p["s5_norm"].reshape(1, -1), None, tag + "s5_post")

    conv_w = jnp.pad(p["ssd_conv_w"], ((0, 8 - SSD_CONV), (0, 0)))
    conv_b = p["ssd_conv_b"].reshape(1, -1)
    xc = _ssd_conv(xbc, conv_w, conv_b, tag + "ssd_conv")
    expand = _head_expander()
    heads = dict(dt_bias=_pad_heads(p["ssd_dt_bias"]), a_log=_pad_heads(p["ssd_a_log"]),
                 d=jnp.repeat(p["ssd_d"], SSD_HEAD_DIM).reshape(SSD_GROUPS, 1, GROUP_W),
                 expand=expand, expand_t=expand.T)
    y, sprev = _ssd_scan(xc, dt, heads["dt_bias"], heads["a_log"], heads["d"], expand, heads["expand_t"],
                         tag + "ssd_scan")
    yab = _ssd_post(y, z, p["ssd_norm"].reshape(1, -1), yab, tag + "ssd_post")

    x1 = _out_proj(x0, yab, big["w_out"], None, tag + "out_proj")
    ffn_matrices = before_ffn(x1) if before_ffn is not None else {}
    big = {**big, **ffn_matrices}
    late_matrices = {**late_matrices, **ffn_matrices}
    x2, gt, up = _ffn(x1, p["norm_ffn"].reshape(1, -1), big["w_gate"], big["w_up"], big["w_down"], None,
                      tag + "ffn")
    saved = dict(x0=x0, h1=h1, u=u, u_perm=u_perm, z=z, xbc=xbc, dt=dt, xr=xr, xi=xi, ys=ys, yab=yab, xc=xc, y=y,
                 sprev=sprev, x1=x1, gt=gt, up=up, s5mats=s5mats, heads=heads, conv_w=conv_w,
                 conv_b=conv_b, ls=ls, b_hgp=b_hgp, late_matrices=late_matrices)
    return x2, saved


def _layer_backward(dx2, dx2b, p, big, s, i, after_ffn_grads=None, after_s5_grads=None):
    tag = "l%d_" % i
    g = {}
    dx1, dx1b, h2, act, dgt, dup, dgain = _ffn_bwd(dx2, s["x1"], s["gt"], s["up"], p["norm_ffn"].reshape(1, -1),
                                                  big["w_gate"], big["w_up"], big["w_down"], None, tag + "ffn_bwd")
    g["norm_ffn"] = dgain[0]
    g["w_down"] = _mm_tn(act, dx2b, tag + "dw_down")
    g["w_gate"] = _mm_tn(dgt, h2, tag + "dw_gate")
    g["w_up"] = _mm_tn(dup, h2, tag + "dw_up")
    g["w_out"] = _mm_tn(s["yab"], dx1b, tag + "dw_out")
    if after_ffn_grads is not None:
        p = {**p, "s5_norm": p["s5_norm"] + after_ffn_grads(g)[0, 0]}

    dys, du_skip, gelu_b, dq_b, dgain, dd, dbg = _s5_post_bwd(
        dx1b, big["w_out"], s["ys"], s["u"], p["s5_d"].reshape(1, -1), big["s5_w_glu"],
        p["s5_b_glu"].reshape(1, -1), p["s5_norm"].reshape(1, -1), None, tag + "s5_post_bwd")
    g["s5_norm"] = dgain[0]
    g["s5_d"] = dd[0]
    g["s5_b_glu"] = dbg[0]
    g["s5_w_glu"] = _mm_tn(gelu_b, dq_b, tag + "dw_glu")
    m = s["s5mats"]
    du_perm, dar, dai, dcre_d, dcim_d, dbre_d, dbim_d = _s5_scan_bwd(
        _to_segments(dys), s["u_perm"], s["xr"], s["xi"], m["bret_bd"], m["bimt_bd"], m["cret_bd"], m["cimt_bd"],
        m["are"], m["aim"], tag + "s5_scan_bwd")
    du_scan = _from_segments(du_perm)
    diag = _block_diag_extract([dcre_d, dcim_d, dbre_d, dbim_d], tag + "s5_blockdiag_bwd")
    g["s5_c_re"], g["s5_c_im"] = diag[0], diag[1]
    dbbre = diag[2].transpose(1, 0, 2)
    dbbim = diag[3].transpose(1, 0, 2)
    dlr, dli, dls, dbre, dbim = _s5_prep_bwd(
        p["s5_lam_re"], p["s5_lam_im"], s["ls"], s["b_hgp"][0], s["b_hgp"][1],
        dar.reshape(S5_GROUPS, S5_STATE), dai.reshape(S5_GROUPS, S5_STATE), dbbre, dbbim, tag + "s5_prep_bwd")
    g["s5_lam_re"] = dlr
    g["s5_lam_im"] = dli
    g["s5_log_step"] = dls[:, 0]
    g["s5_b_re"] = dbre
    g["s5_b_im"] = dbim
    if after_s5_grads is not None:
        p = {**p, "ssd_norm": p["ssd_norm"] + after_s5_grads(g)[0, 0]}

    dy, dz, dgain = _ssd_post_bwd(dx1b, big["w_out"], s["y"], s["z"], p["ssd_norm"].reshape(1, -1), None,
                                  tag + "ssd_post_bwd")
    g["ssd_norm"] = dgain[0]
    hd = s["heads"]
    dxs, dbm, dcm, ddt, dbias, dalog, dd = _ssd_scan_bwd(dy, s["xc"], s["dt"], s["sprev"], hd["dt_bias"],
                                                       hd["a_log"], hd["d"], hd["expand"], hd["expand_t"],
                                                       tag + "ssd_scan_bwd")
    g["ssd_dt_bias"] = _unpad_heads(dbias)
    g["ssd_a_log"] = _unpad_heads(dalog)
    g["ssd_d"] = _unpad_heads(dd)
    dxbc, dcw, dcb = _ssd_conv_bwd(dxs, dbm, dcm, s["xbc"], s["conv_w"], s["conv_b"], tag + "ssd_conv_bwd")
    g["ssd_conv_w"] = dcw[:SSD_CONV]
    g["ssd_conv_b"] = dcb[0]

    dx0, dx0b, dproj, dgain = _inproj_bwd(dx1, s["x0"], du_skip, du_scan, dz, dxbc, ddt, p["norm_mix"].reshape(1, -1),
                                          big["w_in"], None, tag + "inproj_bwd")
    g["norm_mix"] = dgain[0]
    g["w_in"] = _mm_tn(s["h1"], dproj, tag + "dw_in")
    return dx0, dx0b, g


MIXER_BIG = ("w_in", "s5_w_glu", "w_out")
FFN_BIG = ("w_gate", "w_up", "w_down")
BIG = MIXER_BIG + FFN_BIG
COL_SHARDED = ("w_in",)
T_STORED = ("w_gate", "w_up")
LAYER_SMALL = ("norm_mix", "s5_lam_re", "s5_lam_im", "s5_log_step", "s5_b_re", "s5_b_im", "s5_c_re", "s5_c_im",
               "s5_d", "s5_b_glu", "s5_norm", "ssd_conv_w", "ssd_conv_b", "ssd_dt_bias", "ssd_a_log", "ssd_d",
               "ssd_norm", "norm_ffn")
WEIGHTS = ("norm_mix", "w_in", "s5_lam_re", "s5_lam_im", "s5_log_step", "s5_b_re", "s5_b_im", "s5_c_re", "s5_c_im",
           "s5_d", "s5_w_glu", "s5_b_glu", "s5_norm", "ssd_conv_w", "ssd_conv_b", "ssd_dt_bias", "ssd_a_log",
           "ssd_d", "ssd_norm", "w_out", "norm_ffn", "w_gate", "w_up", "w_down", "norm_final")


S5_BC = ("s5_b_re", "s5_b_im", "s5_c_re", "s5_c_im")
TINY = tuple(k for k in LAYER_SMALL if k not in S5_BC)


def _my_place():
    return lax.axis_index("x"), lax.axis_index("y"), lax.axis_index("c")


HBM = pl.BlockSpec(memory_space=pltpu.HBM)
SEM = pl.BlockSpec(memory_space=pltpu.SEMAPHORE)
DATAFLOW = pltpu.SideEffectType.DATAFLOW_SIDE_EFFECTING


def _in_hbm(a):
    return pltpu.with_memory_space_constraint(a, pltpu.HBM)


TOKEN = jax.ShapeDtypeStruct((8, LANES), F32)
VMEM_SPEC = pl.BlockSpec(memory_space=pltpu.VMEM)


def _gather_start(blocks, after, name):
    nt = len(blocks)

    def body(*refs):
        ins = refs[:nt]
        lands = refs[nt:2 * nt]
        send_sems, recv_sems = refs[2 * nt + 1:2 * nt + 3]
        refs[-1][...] = jnp.zeros(TOKEN.shape, F32)
        x, y, c = _my_place()
        me = 4 * x + 2 * y + c
        peers = [(x, y, 1 - c), (1 - x, y, c), (x, 1 - y, c), (1 - x, 1 - y, c)]
        for t in range(nt):
            for k, peer in enumerate(peers):
                pltpu.make_async_remote_copy(src_ref=ins[t], dst_ref=lands[t].at[me], send_sem=send_sems.at[4 * t + k],
                                             recv_sem=recv_sems.at[4 * t + k], device_id=peer,
                                             device_id_type=MESH).start()

    lands = [_in_hbm(lax.empty((8,) + b.shape, b.dtype)) for b in blocks]
    out = pl.pallas_call(
        body, name=name, in_specs=[HBM] * (2 * nt) + [ANY],
        out_shape=(pltpu.SemaphoreType.DMA((4 * nt,)), pltpu.SemaphoreType.DMA((4 * nt,)),
                   *[pltpu.HBM(b.shape, b.dtype) for b in blocks],
                   *[pltpu.HBM((8,) + b.shape, b.dtype) for b in blocks], TOKEN),
        out_specs=(SEM, SEM, *[HBM] * (2 * nt), VMEM_SPEC),
        input_output_aliases={i: 2 + i for i in range(2 * nt)},
        compiler_params=pltpu.CompilerParams(has_side_effects=DATAFLOW),
    )(*[_in_hbm(b) for b in blocks], *lands, after)
    return out[:2], list(out[2:2 + nt]), list(out[2 + nt:2 + 2 * nt]), out[-1]


def _gather_forward(sems, blocks, lands, after, name):
    nt = len(blocks)

    def body(*refs):
        ins = refs[:nt]
        lands_in = refs[nt:2 * nt]
        send1, recv1 = refs[2 * nt:2 * nt + 2]
        send2, recv2 = refs[2 * nt + 3:2 * nt + 5]
        x, y, c = _my_place()
        me = 4 * x + 2 * y + c
        sibling = (x, y, 1 - c)
        sources = [4 * x + 2 * y + (1 - c), 4 * (1 - x) + 2 * y + c, 4 * x + 2 * (1 - y) + c,
                   4 * (1 - x) + 2 * (1 - y) + c]
        for t in range(nt):
            for k, src in enumerate(sources):
                cp = pltpu.make_async_remote_copy(src_ref=ins[t], dst_ref=lands_in[t].at[src],
                                                  send_sem=send1.at[4 * t + k], recv_sem=recv1.at[4 * t + k],
                                                  device_id=sibling, device_id_type=MESH)
                cp.wait_send()
                cp.wait_recv()
            for k, src in enumerate(sources[1:]):
                pltpu.make_async_remote_copy(src_ref=lands_in[t].at[src], dst_ref=lands_in[t].at[src],
                                             send_sem=send2.at[3 * t + k], recv_sem=recv2.at[3 * t + k],
                                             device_id=sibling, device_id_type=MESH).start()

    out = pl.pallas_call(
        body, name=name, in_specs=[HBM] * (2 * nt) + [SEM, SEM, pl.BlockSpec(memory_space=pl.ANY)],
        out_shape=(pltpu.SemaphoreType.DMA((3 * nt,)), pltpu.SemaphoreType.DMA((3 * nt,)),
                   *[pltpu.HBM(b.shape, b.dtype) for b in blocks],
                   *[pltpu.HBM(a.shape, a.dtype) for a in lands]),
        out_specs=(SEM, SEM, *[HBM] * (2 * nt)),
        input_output_aliases={i: 2 + i for i in range(2 * nt)},
        compiler_params=pltpu.CompilerParams(has_side_effects=DATAFLOW),
    )(*blocks, *lands, *sems, after)
    return out[:2], list(out[2 + nt:])


def _gather_finish(sems, lands, after, name):
    nt = len(lands)

    def body(*refs):
        lands_in = refs[:nt]
        send2, recv2 = refs[nt:nt + 2]
        x, y, c = _my_place()
        sibling = (x, y, 1 - c)
        mine = [4 * (1 - x) + 2 * y + c, 4 * x + 2 * (1 - y) + c, 4 * (1 - x) + 2 * (1 - y) + c]
        theirs = [4 * (1 - x) + 2 * y + 1 - c, 4 * x + 2 * (1 - y) + 1 - c, 4 * (1 - x) + 2 * (1 - y) + 1 - c]
        for t in range(nt):
            for k in range(3):
                cp = pltpu.make_async_remote_copy(src_ref=lands_in[t].at[mine[k]], dst_ref=lands_in[t].at[theirs[k]],
                                                  send_sem=send2.at[3 * t + k], recv_sem=recv2.at[3 * t + k],
                                                  device_id=sibling, device_id_type=MESH)
                cp.wait_send()
                cp.wait_recv()

    out = pl.pallas_call(
        body, name=name, in_specs=[HBM] * nt + [SEM, SEM, pl.BlockSpec(memory_space=pl.ANY)],
        out_shape=tuple(pltpu.HBM(a.shape, a.dtype) for a in lands), out_specs=tuple([HBM] * nt),
        input_output_aliases={i: i for i in range(nt)},
        compiler_params=pltpu.CompilerParams(has_side_effects=DATAFLOW),
    )(*lands, *sems, after)
    return list(out)


def _other_chips():
    x, y, _ = _my_place()
    return [(1 - x, y), (x, 1 - y), (1 - x, 1 - y)]


def _scatter_start(chunks, name):
    nt = len(chunks)

    def body(*refs):
        ins = refs[:nt]
        lands = refs[nt:2 * nt]
        send_sems, recv_sems = refs[2 * nt:2 * nt + 2]
        refs[-1][...] = jnp.zeros(TOKEN.shape, F32)
        x, y, c = _my_place()
        for t in range(nt):
            for j, (px, py) in enumerate(_other_chips()):
                pltpu.make_async_remote_copy(src_ref=ins[t].at[2 * px + py], dst_ref=lands[t].at[2 * x + y],
                                             send_sem=send_sems.at[3 * t + j], recv_sem=recv_sems.at[3 * t + j],
                                             device_id=(px, py, c), device_id_type=MESH).start()

    lands = [_in_hbm(lax.empty(a.shape, a.dtype)) for a in chunks]
    out = pl.pallas_call(
        body, name=name, in_specs=[HBM] * (2 * nt),
        out_shape=(pltpu.SemaphoreType.DMA((3 * nt,)), pltpu.SemaphoreType.DMA((3 * nt,)),
                   *[pltpu.HBM(a.shape, a.dtype) for a in chunks] * 2, TOKEN),
        out_specs=(SEM, SEM, *[HBM] * (2 * nt), VMEM_SPEC),
        input_output_aliases={i: 2 + i for i in range(2 * nt)},
        compiler_params=pltpu.CompilerParams(has_side_effects=DATAFLOW),
    )(*[_in_hbm(a) for a in chunks], *lands)
    return out[:2], list(out[2:2 + nt]), list(out[2 + nt:2 + 2 * nt]), out[-1]


def _scatter_finish(sems, chunks, lands, after, name):
    nt = len(chunks)

    def body(*refs):
        ins = refs[:nt]
        lands_in = refs[nt:2 * nt]
        send_sems, recv_sems = refs[2 * nt:2 * nt + 2]
        _, _, c = _my_place()
        for t in range(nt):
            for j, (px, py) in enumerate(_other_chips()):
                cp = pltpu.make_async_remote_copy(src_ref=ins[t].at[2 * px + py], dst_ref=lands_in[t].at[2 * px + py],
                                                  send_sem=send_sems.at[3 * t + j], recv_sem=recv_sems.at[3 * t + j],
                                                  device_id=(px, py, c), device_id_type=MESH)
                cp.wait_send()
                cp.wait_recv()

    out = pl.pallas_call(
        body, name=name, in_specs=[HBM] * (2 * nt) + [SEM, SEM, ANY],
        out_shape=tuple(pltpu.HBM(a.shape, a.dtype) for a in lands), out_specs=tuple([HBM] * nt),
        input_output_aliases={nt + i: i for i in range(nt)},
        compiler_params=pltpu.CompilerParams(has_side_effects=DATAFLOW),
    )(*chunks, *lands, *sems, after)
    return list(out)


def _swap_start(views, name):
    nt = len(views)

    def body(*refs):
        ins = refs[:nt]
        lands = refs[nt:2 * nt]
        send_sems, recv_sems = refs[2 * nt:2 * nt + 2]
        refs[-1][...] = jnp.zeros(TOKEN.shape, F32)
        x, y, c = _my_place()
        for t in range(nt):
            pltpu.make_async_remote_copy(
                src_ref=ins[t].at[pl.ds(0, views[t].shape[0]), pl.ds(1 - c, 1)], dst_ref=lands[t],
                send_sem=send_sems.at[t], recv_sem=recv_sems.at[t], device_id=(x, y, 1 - c),
                device_id_type=MESH).start()

    shapes = [(a.shape[0], 1) + a.shape[2:] for a in views]
    lands = [_in_hbm(lax.empty(s, a.dtype)) for s, a in zip(shapes, views)]
    out = pl.pallas_call(
        body, name=name, in_specs=[HBM] * (2 * nt),
        out_shape=(pltpu.SemaphoreType.DMA((nt,)), pltpu.SemaphoreType.DMA((nt,)),
                   *[pltpu.HBM(a.shape, a.dtype) for a in views],
                   *[pltpu.HBM(s, a.dtype) for s, a in zip(shapes, views)], TOKEN),
        out_specs=(SEM, SEM, *[HBM] * (2 * nt), VMEM_SPEC),
        input_output_aliases={i: 2 + i for i in range(2 * nt)},
        compiler_params=pltpu.CompilerParams(has_side_effects=DATAFLOW),
    )(*[_in_hbm(a) for a in views], *lands)
    return out[:2], list(out[2:2 + nt]), list(out[2 + nt:2 + 2 * nt]), out[-1]


def _swap_finish(sems, views, lands, after, name):
    nt = len(views)

    def body(*refs):
        ins = refs[:nt]
        lands_in = refs[nt:2 * nt]
        send_sems, recv_sems = refs[2 * nt:2 * nt + 2]
        x, y, c = _my_place()
        for t in range(nt):
            cp = pltpu.make_async_remote_copy(
                src_ref=ins[t].at[pl.ds(0, views[t].shape[0]), pl.ds(1 - c, 1)], dst_ref=lands_in[t],
                send_sem=send_sems.at[t], recv_sem=recv_sems.at[t], device_id=(x, y, 1 - c), device_id_type=MESH)
            cp.wait_send()
            cp.wait_recv()

    out = pl.pallas_call(
        body, name=name, in_specs=[HBM] * (2 * nt) + [SEM, SEM, ANY],
        out_shape=tuple(pltpu.HBM(a.shape, a.dtype) for a in lands), out_specs=tuple([HBM] * nt),
        input_output_aliases={nt + i: i for i in range(nt)},
        compiler_params=pltpu.CompilerParams(has_side_effects=DATAFLOW),
    )(*views, *lands, *sems, after)
    return list(out)


def _swap_halves(views, name):
    nt = len(views)

    def body(*refs):
        ins = refs[:nt]
        outs = refs[nt:2 * nt]
        send_sems, recv_sems = refs[2 * nt:]
        x, y, c = _my_place()
        copies = [pltpu.make_async_remote_copy(
            src_ref=ins[t].at[pl.ds(0, views[t].shape[0]), pl.ds(1 - c, 1)], dst_ref=outs[t],
            send_sem=send_sems.at[t], recv_sem=recv_sems.at[t], device_id=(x, y, 1 - c), device_id_type=MESH)
            for t in range(nt)]
        for cp in copies:
            cp.start()
        for cp in copies:
            cp.wait()

    return pl.pallas_call(
        body, name=name, in_specs=[ANY] * nt, out_specs=[ANY] * nt,
        out_shape=[jax.ShapeDtypeStruct((a.shape[0], 1) + a.shape[2:], a.dtype) for a in views],
        scratch_shapes=[pltpu.SemaphoreType.DMA((nt,)), pltpu.SemaphoreType.DMA((nt,))],
    )(*views)


def _pair_add_halves(view, recv, name):
    n, _, rows, cols = view.shape
    tile = _row_tile(rows, cols, 4)

    def body(a0_ref, a1_ref, r_ref, o_ref):
        mine = jnp.where(lax.axis_index("c") == 0, a0_ref[...], a1_ref[...])
        o_ref[...] = (mine.astype(F32) + r_ref[...].astype(F32)).astype(o_ref.dtype)

    half = lambda h: pl.BlockSpec((None, None, tile, cols), lambda p, i: (p, h, i, 0))
    return pl.pallas_call(
        body, name=name, grid=(n, rows // tile), in_specs=[half(0), half(1), half(0)],
        out_specs=pl.BlockSpec((None, tile, cols), lambda p, i: (p, i, 0)),
        out_shape=jax.ShapeDtypeStruct((n, rows, cols), view.dtype))(view, view, recv)


def _sum_chunks(lands, chunks, order, name):
    _, rows, cols = chunks.shape
    tile = _row_tile(rows, cols, 5)

    def body(order_ref, l0_ref, l1_ref, l2_ref, own_ref, o_ref):
        o_ref[...] = ((l0_ref[...].astype(F32) + l1_ref[...].astype(F32)) + l2_ref[...].astype(F32)
                      + own_ref[...].astype(F32))

    slot = lambda j: pl.BlockSpec((None, tile, cols), lambda i, order_ref: (order_ref[j], i, 0))
    grid_spec = pltpu.PrefetchScalarGridSpec(
        num_scalar_prefetch=1, grid=(rows // tile,), in_specs=[slot(0), slot(1), slot(2), slot(3)],
        out_specs=pl.BlockSpec((tile, cols), lambda i, order_ref: (i, 0)))
    return pl.pallas_call(body, name=name, grid_spec=grid_spec,
                          out_shape=jax.ShapeDtypeStruct((rows, cols), F32))(order, lands, lands, lands, chunks)


def _adamw_layer(w, g_mine, g_sibling, m, v, layer, prev, name):
    depth, rows, cols = w.shape
    half = rows // 2
    tile = _row_tile(half, cols, 10)
    tiles = half // tile

    def body(w_ref, gm_ref, gs_ref, m_ref, v_ref, *rest):
        d_ref, nm_ref, nv_ref, go_ref = rest[-4:]
        gv = jnp.where(pl.program_id(0) == lax.axis_index("c"), gm_ref[...], gs_ref[...])
        d_ref[...], nm_ref[...], nv_ref[...] = _adamw_math(w_ref[...], gv, m_ref[...], v_ref[...])
        go_ref[...] = gv

    spec = pl.BlockSpec((None, tile, cols), lambda h, i: (layer, h * tiles + i, 0))
    gspec = pl.BlockSpec((tile, cols), lambda h, i: (i, 0))
    shape = jax.ShapeDtypeStruct((depth, rows, cols), F32)
    extra = list(prev)
    aliases = {5 + j: j for j in range(4)} if len(extra) == 4 else {}
    return pl.pallas_call(
        body, name=name, grid=(2, tiles), in_specs=[spec, gspec, gspec, spec, spec] + [ANY] * len(extra),
        out_specs=[spec] * 4, out_shape=[shape] * 4, input_output_aliases=aliases)(w, g_mine, g_sibling, m, v, *extra)


def _sibling_swap(arrs, name):
    nt = len(arrs)

    def body(*refs):
        ins = refs[:nt]
        outs = refs[nt:2 * nt]
        send_sems, recv_sems = refs[2 * nt:]
        x, y, c = _my_place()
        copies = [pltpu.make_async_remote_copy(src_ref=ins[t], dst_ref=outs[t], send_sem=send_sems.at[t],
                                               recv_sem=recv_sems.at[t], device_id=(x, y, 1 - c), device_id_type=MESH)
                  for t in range(nt)]
        for cp in copies:
            cp.start()
        for cp in copies:
            cp.wait()

    return pl.pallas_call(
        body, name=name, in_specs=[ANY] * nt, out_specs=[ANY] * nt,
        out_shape=[jax.ShapeDtypeStruct(a.shape, a.dtype) for a in arrs],
        scratch_shapes=[pltpu.SemaphoreType.DMA((nt,)), pltpu.SemaphoreType.DMA((nt,))],
    )(*arrs)


STREAM_VMEM_BYTES = 32 * 1024 * 1024
SUBLANES = 8


def _row_tile(rows, cols, n_arrays):
    lanes = -(-cols // LANES) * LANES
    for t in range(min(rows, 512), SUBLANES - 1, -1):
        if rows % t == 0 and t % SUBLANES == 0 and 2 * n_arrays * t * lanes * 4 <= STREAM_VMEM_BYTES:
            return t
    return rows


def _sum_leading(a, name):
    n, rows, cols = a.shape
    tile = _row_tile(rows, cols, n + 1)

    def body(a_ref, o_ref):
        acc = a_ref[0].astype(F32)
        for k in range(1, n):
            acc = acc + a_ref[k].astype(F32)
        o_ref[...] = acc

    return pl.pallas_call(
        body, name=name, grid=(rows // tile,), in_specs=[pl.BlockSpec((n, tile, cols), lambda i: (0, i, 0))],
        out_specs=pl.BlockSpec((tile, cols), lambda i: (i, 0)),
        out_shape=jax.ShapeDtypeStruct((rows, cols), F32))(a)


def _adamw_math(w, g, m, v):
    mn = ADAM_B1 * m + (1.0 - ADAM_B1) * g
    vn = ADAM_B2 * v + (1.0 - ADAM_B2) * jnp.square(g)
    m_hat = mn / (1.0 - ADAM_B1 ** ADAM_STEP)
    v_hat = vn / (1.0 - ADAM_B2 ** ADAM_STEP)
    delta = -ADAM_LR * (m_hat / (jnp.sqrt(v_hat) + ADAM_EPS) + ADAM_WD * w)
    return delta, mn, vn


def _adamw_rows(w, g, m, v, name):
    depth, rows, cols = w.shape
    tile = _row_tile(rows, cols, 7)

    def body(w_ref, g_ref, m_ref, v_ref, d_ref, nm_ref, nv_ref):
        d_ref[...], nm_ref[...], nv_ref[...] = _adamw_math(w_ref[...], g_ref[...], m_ref[...], v_ref[...])

    spec = pl.BlockSpec((None, tile, cols), lambda l, i: (l, i, 0))
    shape = jax.ShapeDtypeStruct((depth, rows, cols), F32)
    return pl.pallas_call(body, name=name, grid=(depth, rows // tile), in_specs=[spec] * 4, out_specs=[spec] * 3,
                          out_shape=[shape] * 3)(w, g, m, v)


def _adamw_many(ws, gs, ms, vs, name):
    nt = len(ws)

    def body(*refs):
        for t in range(nt):
            w_ref, g_ref, m_ref, v_ref = (refs[k * nt + t] for k in range(4))
            d_ref, nm_ref, nv_ref = (refs[(4 + k) * nt + t] for k in range(3))
            d_ref[...], nm_ref[...], nv_ref[...] = _adamw_math(w_ref[...], g_ref[...], m_ref[...], v_ref[...])

    shapes = [jax.ShapeDtypeStruct(a.shape, F32) for a in ws]
    out = pl.pallas_call(body, name=name, out_shape=shapes * 3)(*ws, *gs, *ms, *vs)
    return out[:nt], out[nt:2 * nt], out[2 * nt:]


TINY_ROWS_MULTIPLE = 128


def _flat_pack(arrs):
    flat = jnp.concatenate([a.reshape(-1) for a in arrs])
    pad = (-flat.shape[0]) % (TINY_ROWS_MULTIPLE * LANES)
    return jnp.pad(flat, (0, pad)).reshape(-1, LANES)


def _flat_unpack(buf, shapes):
    flat = buf.reshape(-1)
    out = []
    off = 0
    for shp in shapes:
        n = math.prod(shp)
        out.append(flat[off:off + n].reshape(shp))
        off += n
    return out


def kernel(x, norm_mix, w_in, s5_lam_re, s5_lam_im, s5_log_step, s5_b_re, s5_b_im, s5_c_re, s5_c_im, s5_d, s5_w_glu, s5_b_glu, s5_norm, ssd_conv_w, ssd_conv_b, ssd_dt_bias, ssd_a_log, ssd_d, ssd_norm, w_out, norm_ffn, w_gate, w_up, w_down, norm_final, loss_target, m_norm_mix, m_w_in, m_s5_lam_re, m_s5_lam_im, m_s5_log_step, m_s5_b_re, m_s5_b_im, m_s5_c_re, m_s5_c_im, m_s5_d, m_s5_w_glu, m_s5_b_glu, m_s5_norm, m_ssd_conv_w, m_ssd_conv_b, m_ssd_dt_bias, m_ssd_a_log, m_ssd_d, m_ssd_norm, m_w_out, m_norm_ffn, m_w_gate, m_w_up, m_w_down, m_norm_final, v_norm_mix, v_w_in, v_s5_lam_re, v_s5_lam_im, v_s5_log_step, v_s5_b_re, v_s5_b_im, v_s5_c_re, v_s5_c_im, v_s5_d, v_s5_w_glu, v_s5_b_glu, v_s5_norm, v_ssd_conv_w, v_ssd_conv_b, v_ssd_dt_bias, v_ssd_a_log, v_ssd_d, v_ssd_norm, v_w_out, v_norm_ffn, v_w_gate, v_w_up, v_w_down, v_norm_final):
    args = dict(locals())
    w = {k: args[k] for k in WEIGHTS}
    m = {k: args["m_" + k] for k in WEIGHTS}
    v = {k: args["v_" + k] for k in WEIGHTS}
    cx, cy, cc = _my_place()
    chip = 2 * cx + cy

    me = 4 * cx + 2 * cy + cc
    others = _other_chips()
    chunk_order = jnp.stack([2 * px + py for px, py in others] + [chip]).astype(jnp.int32)
    stored = lambda k, a: jnp.swapaxes(a, 1, 2) if k in T_STORED else a

    def my_half(k, layer):
        a = stored(k, w[k])[layer]
        return lax.dynamic_slice_in_dim(a, cc * (a.shape[0] // 2), a.shape[0] // 2, 0).astype(BF16)

    def assemble(names, lands, blocks):
        full = {}
        for k, a, b in zip(names, lands, blocks):
            a = lax.dynamic_update_index_in_dim(a, b, me, 0)
            a = a.reshape(4, 2 * a.shape[1], a.shape[2])
            if k in COL_SHARDED:
                full[k] = _pad_in_proj(a.transpose(1, 0, 2).reshape(a.shape[1], 4 * a.shape[2]))
            else:
                full[k] = a.reshape(4 * a.shape[1], a.shape[2])
        return full

    conv_block = w["ssd_conv_w"].reshape(DEPTH * SSD_CONV, -1)
    first = [my_half("w_in", 0), conv_block]
    second_names = ("s5_w_glu", "w_out")
    second = [my_half(k, 0) for k in second_names]
    ffn0 = [my_half(k, 0) for k in FFN_BIG]
    blocks1 = [my_half(k, 1) for k in BIG]
    sems_a, kept_a, lands_a, token = _gather_start(first, x, "gather0a_start")
    sems_c, kept_c, lands_c, token = _gather_start(second, token, "gather0c_start")
    sems_b, kept_b, lands_b, token = _gather_start(ffn0, token, "gather0b_start")
    sems1, kept1, lands1, token = _gather_start(blocks1, token, "gather1_start")
    sems_a, lands_a = _gather_forward(sems_a, kept_a, lands_a, token, "gather0a_forward")
    lands_a = _gather_finish(sems_a, lands_a, token, "gather0a_finish")
    big0 = assemble(("w_in",), lands_a, first)
    conv_rows = lax.dynamic_update_index_in_dim(lands_a[-1], conv_block, me, 0)
    conv_full = conv_rows.reshape(4, 2, DEPTH, SSD_CONV, -1)[:, 0].transpose(1, 2, 0, 3).reshape(
        DEPTH, SSD_CONV, SSD_CONV_DIM)
    small = {k: w[k] for k in LAYER_SMALL}
    small["ssd_conv_w"] = conv_full
    p0 = {k: a[0] for k, a in small.items()}
    p1 = {k: a[1] for k, a in small.items()}

    p0["norm_mix"] = p0["norm_mix"] + token[0, 0]
    pending = {}

    def pass_on_ffn0(u):
        pending["second"] = _gather_forward(sems_c, kept_c, lands_c, u, "gather0c_forward")
        pending["ffn0"] = _gather_forward(sems_b, kept_b, lands_b, pending["second"][1][0], "gather0b_forward")

    def second_matrices(ys):
        sems, lands = pending["second"]
        return assemble(second_names, _gather_finish(sems, lands, ys, "gather0c_finish"), second)

    def ffn0_matrices(x1):
        sems, lands = pending["ffn0"]
        lands = _gather_finish(sems, lands, x1, "gather0b_finish")
        pending["layer1"] = _gather_forward(sems1, kept1, lands1, lands[0], "gather1_forward")
        return assemble(FFN_BIG, lands, ffn0)

    h1, saved0 = _layer_forward(x[0], p0, big0, 0, pass_on_ffn0, second_matrices, ffn0_matrices)
    big0 = {**big0, **saved0["late_matrices"]}
    sems1, lands1 = pending["layer1"]
    lands1 = _gather_finish(sems1, lands1, h1, "gather1_finish")
    big1 = assemble(BIG, lands1, blocks1)
    h2, saved1 = _layer_forward(h1, p1, big1, 1)
    loss_row, dx, dxb, g_final = _final_loss(h2, w["norm_final"].reshape(1, -1), loss_target[0], "final_loss")
    loss_part, g_final = loss_row[0, 0], g_final[0]

    def halves_view(k, a):
        if k in COL_SHARDED:
            return a.reshape(1, 2, a.shape[0] // 2, a.shape[1])
        return a.reshape(4, 2, a.shape[0] // 8, a.shape[1])

    def to_chunks(k, part):
        if k in COL_SHARDED:
            a = _unpad_in_proj(part[0])
            return a.reshape(a.shape[0], 4, a.shape[1] // 4).transpose(1, 0, 2)
        return part.reshape(4, -1, part.shape[-1])

    def reduce_begin(names, views, tag):
        recv = _swap_halves(views, tag + "swap")
        parts = [_pair_add_halves(a, r, tag + "pair_" + k) for k, a, r in zip(names, views, recv)]
        chunks = [to_chunks(k, p) for k, p in zip(names, parts)]
        return _scatter_start(chunks, tag + "scatter_start")

    def reduce_end(names, handle, after, tag):
        sems, kept, lands, _ = handle
        lands = _scatter_finish(sems, kept, lands, after, tag + "scatter_finish")
        return [_sum_chunks(a, b, chunk_order, tag + "sum_" + k) for k, a, b in zip(names, lands, kept)]

    def swap_begin(names, views, tag):
        return (names, *_swap_start(views, tag + "swap_start"))

    def scatter_begin(handle, after, tag):
        names, sems, views, lands, _ = handle
        recv = _swap_finish(sems, views, lands, after, tag + "swap_finish")
        parts = [_pair_add_halves(a, r, tag + "pair_" + k) for k, a, r in zip(names, views, recv)]
        return _scatter_start([to_chunks(k, p) for k, p in zip(names, parts)], tag + "scatter_start")

    dx, dxb, g1 = _layer_backward(dx, dxb, p1, big1, saved1, 1)
    swap1 = swap_begin(BIG, [halves_view(k, g1[k]) for k in BIG], "grad1_")
    p0["norm_ffn"] = p0["norm_ffn"] + swap1[4][0, 0]

    early = FFN_BIG + ("w_out",)
    middle = ("s5_w_glu", "s5_bc")
    bc_rows = 2 * DEPTH * S5_GROUP * S5_GROUPS

    def send_early(g_so_far):
        pending["round1"] = scatter_begin(swap1, g_so_far["w_out"], "grad1_")
        pending["swap_early"] = swap_begin(early, [halves_view(k, g_so_far[k]) for k in early], "grad0a_")
        return pending["swap_early"][4]

    def send_middle(g_so_far):
        pending["early"] = scatter_begin(pending["swap_early"], g_so_far["s5_w_glu"], "grad0a_")
        rows = lambda names: jnp.stack([a for layer in (g_so_far, g1) for a in (layer[names[0]], layer[names[1]])]
                                       ).reshape(bc_rows, S5_STATE)
        bc = jnp.stack([rows(("s5_b_re", "s5_b_im")), rows(("s5_c_re", "s5_c_im"))])[None]
        pending["swap_middle"] = swap_begin(middle, [halves_view("s5_w_glu", g_so_far["s5_w_glu"]), bc], "grad0b_")
        return pending["swap_middle"][4]

    grad_x, _, g0 = _layer_backward(dx, dxb, p0, big0, saved0, 0, send_early, send_middle)
    g = {k: [g0[k], g1[k]] for k in LAYER_SMALL}
    pending["middle"] = scatter_begin(pending["swap_middle"], grad_x, "grad0b_")
    reduced1 = dict(zip(BIG, reduce_end(BIG, pending["round1"], grad_x, "grad1_")))
    shared1 = dict(zip(BIG, _sibling_swap([reduced1[k] for k in BIG], "grad1_share")))
    round0 = reduce_begin(("w_in",), [halves_view("w_in", g0["w_in"])], "grad0c_")

    delta, new_m, new_v, grads = {}, {}, {}, {}
    adam1 = {}
    layered = tuple(k for k in BIG if k not in COL_SHARDED)
    for k in layered:
        adam1[k] = _adamw_layer(stored(k, w[k]), reduced1[k], shared1[k], stored(k, m[k]), stored(k, v[k]), 1,
                                [round0[3]], "adamw1_" + k)
    follow = adam1[layered[-1]][0]
    reduced0 = dict(zip(early, reduce_end(early, pending["early"], follow, "grad0a_")))
    reduced0.update(zip(middle, reduce_end(middle, pending["middle"], follow, "grad0b_")))
    tiny_names = TINY + ("norm_final",)
    parts = [jnp.stack(g[k]) for k in TINY] + [g_final, loss_part.reshape(1)]
    shapes = [p.shape for p in parts]
    small_blocks = [_flat_pack(parts), reduced0["s5_bc"]]
    small_sems, small_kept, small_lands, small_token = _gather_start(small_blocks, follow, "gather_small_start")
    reduced0.update(zip(("w_in",), reduce_end(("w_in",), round0, small_token, "grad0c_")))
    shared0 = dict(zip(BIG, _sibling_swap([reduced0[k] for k in BIG], "grad0_share")))
    for k in layered:
        outs = _adamw_layer(stored(k, w[k]), reduced0[k], shared0[k], stored(k, m[k]), stored(k, v[k]), 0, adam1[k],
                            "adamw0_" + k)
        delta[k], new_m[k], new_v[k], grads[k] = (stored(k, a) for a in outs)
    both = lambda mine, sib: jnp.where(cc == 0, jnp.concatenate([mine, sib]), jnp.concatenate([sib, mine]))
    grads["w_in"] = jnp.stack([both(reduced0["w_in"], shared0["w_in"]), both(reduced1["w_in"], shared1["w_in"])])
    outs = _adamw_rows(*[_lane_dense(a)[None] for a in (w["w_in"], grads["w_in"], m["w_in"], v["w_in"])],
                       "adamw_w_in")
    delta["w_in"], new_m["w_in"], new_v["w_in"] = (_from_lane_dense(a[0]) for a in outs)

    last = delta["w_in"]
    small_sems, small_lands = _gather_forward(small_sems, small_kept, small_lands, last, "gather_small_forward")
    small_lands = _gather_finish(small_sems, small_lands, last, "gather_small_finish")
    allparts, bc_eighths = (lax.dynamic_update_index_in_dim(a, b, me, 0) for a, b in zip(small_lands, small_blocks))
    unpacked = _flat_unpack(_sum_leading(allparts, "sum_small"), shapes)
    loss = unpacked[-1][0]
    grads.update(zip(tiny_names, unpacked[:-1]))
    width = SSD_CONV_DIM // 4
    grads["ssd_conv_w"] = lax.dynamic_slice_in_dim(grads["ssd_conv_w"], chip * width, width, axis=2)
    bc = bc_eighths.reshape(4, 2, bc_rows // 4, S5_STATE)
    b_sum = bc[:, 0].reshape(DEPTH, 2, S5_GROUP, S5_GROUPS, S5_STATE)
    c_sum = bc[:, 1].reshape(DEPTH, 2, S5_GROUPS, S5_GROUP, S5_STATE)
    grads["s5_c_re"] = c_sum[:, 0]
    grads["s5_c_im"] = c_sum[:, 1]

    b_names = ("s5_b_re", "s5_b_im")
    hp = lambda a: a.transpose(0, 1, 3, 2)
    names = tiny_names + ("s5_c_re", "s5_c_im") + b_names
    view = lambda k, a: hp(a) if k in b_names else (a.reshape(1, -1) if a.ndim == 1 else a)
    g_view = {k: view(k, grads[k]) for k in names if k not in b_names}
    g_view.update({k: b_sum[:, j].transpose(0, 2, 1, 3) for j, k in enumerate(b_names)})
    ds, nms, nvs = _adamw_many([view(k, w[k]) for k in names], [g_view[k] for k in names],
                               [view(k, m[k]) for k in names], [view(k, v[k]) for k in names], "adamw_small")
    for k, a, b, c in zip(names, ds, nms, nvs):
        if k in b_names:
            delta[k], new_m[k], new_v[k], grads[k] = hp(a), hp(b), hp(c), hp(g_view[k])
        else:
            delta[k], new_m[k], new_v[k] = (t.reshape(w[k].shape) for t in (a, b, c))

    return (loss, grad_x[None], *[grads[k] for k in WEIGHTS], *[delta[k] for k in WEIGHTS],
            *[new_m[k] for k in WEIGHTS], *[new_v[k] for k in WEIGHTS])
```
